```python
import jax, jax.numpy as jnp
from jax import lax
import numpy as np

D_MODEL = 1024
BATCH = 32
SEQ = 2048
DEPTH = 2

N_A_LAYERS = DEPTH // 2
N_B_LAYERS = DEPTH - N_A_LAYERS
PLE_DIM = 256
CONV_WIDTH = 31
CONV_EXPAND = 2
CONV_CH = CONV_EXPAND * D_MODEL
HEAD_DIM = 64
N_HEADS = D_MODEL // HEAD_DIM
ATTN_WIDTH = N_HEADS * HEAD_DIM
DILATION_GROUPS = ((128, 1), (512, 4), (2048, 16))
N_GROUPS = len(DILATION_GROUPS)
ROPE_THETA = 500000.0
ROPE_DIM = HEAD_DIM // 4
EPS = 1e-6
NEG_INF = -1e30

kernel_name = "yoco_conformer_dilated_hybrid"


def rmsnorm(x, g):
    xf = x.astype(jnp.float32)
    y = xf * lax.rsqrt(jnp.mean(xf * xf, axis=-1, keepdims=True) + EPS)
    return (y * g.astype(jnp.float32)).astype(x.dtype)


def layernorm(x, g, b):
    xf = x.astype(jnp.float32)
    mu = jnp.mean(xf, axis=-1, keepdims=True)
    var = jnp.mean(jnp.square(xf - mu), axis=-1, keepdims=True)
    y = (xf - mu) * lax.rsqrt(var + EPS) * g.astype(jnp.float32) + b.astype(jnp.float32)
    return y.astype(x.dtype)


def rope_partial(x, pos):
    half = ROPE_DIM // 2
    inv = ROPE_THETA ** (-jnp.arange(half, dtype=jnp.float32) * (2.0 / ROPE_DIM))
    ang = pos.astype(jnp.float32)[:, None] * inv[None, :]
    cos = jnp.cos(ang)[None, :, None, :]
    sin = jnp.sin(ang)[None, :, None, :]
    xf = x.astype(jnp.float32)
    x1 = xf[..., :half]
    x2 = xf[..., half:ROPE_DIM]
    out = jnp.concatenate([x1 * cos - x2 * sin, x1 * sin + x2 * cos, xf[..., ROPE_DIM:]], axis=-1)
    return out.astype(x.dtype)


def mixer_a(u, w_in, conv_w, conv_b, ln_g, ln_b, w_out):
    a, b, z = jnp.split(u @ w_in, 3, axis=-1)
    glu = a * jax.nn.sigmoid(b)
    y = lax.conv_general_dilated(
        glu, conv_w[:, None, :], window_strides=(1,),
        padding=[(CONV_WIDTH - 1, 0)],
        dimension_numbers=('NWC', 'WIO', 'NWC'),
        feature_group_count=CONV_CH)
    y = jax.nn.silu(layernorm(y + conv_b, ln_g, ln_b))
    return (y * jax.nn.silu(z)) @ w_out


def dilated_attention(q, k, v, span, dil):
    B, S, H, Dh = q.shape
    L = S // dil
    nb = -(-L // span)
    Lp = nb * span
    pad = Lp - L
    N = B * dil

    def to_res(t):
        return t.reshape(B, L, dil, H, Dh).transpose(0, 2, 1, 3, 4).reshape(N, L, H, Dh)

    qb = jnp.pad(to_res(q), ((0, 0), (0, pad), (0, 0), (0, 0))).reshape(N, nb, span, H, Dh)
    kp = jnp.pad(to_res(k), ((0, 0), (span, pad), (0, 0), (0, 0)))
    vp = jnp.pad(to_res(v), ((0, 0), (span, pad), (0, 0), (0, 0)))

    def band(t):
        return jnp.concatenate([t[:, :Lp].reshape(N, nb, span, H, Dh),
                                t[:, span:].reshape(N, nb, span, H, Dh)], axis=2)

    kb, vb = band(kp), band(vp)
    s = jnp.einsum('nbqhd,nbkhd->nbhqk', qb, kb).astype(jnp.float32) * (Dh ** -0.5)
    qi = jnp.arange(span)[:, None]
    kj = jnp.arange(2 * span)[None, :]
    dist = qi - kj + span
    key_idx = jnp.arange(nb)[:, None] * span + jnp.arange(2 * span)[None, :] - span
    mask = ((dist >= 0) & (dist <= span))[None] & (key_idx >= 0)[:, None, :]
    s = jnp.where(mask[None, :, None], s, NEG_INF)
    lse = jax.nn.logsumexp(s, axis=-1)
    prob = jnp.exp(s - lse[..., None])
    o = jnp.einsum('nbhqk,nbkhd->nbqhd', prob.astype(v.dtype), vb)
    o = o.reshape(B, dil, Lp, H, Dh)[:, :, :L].transpose(0, 2, 1, 3, 4).reshape(B, S, H, Dh)
    lse = lse.transpose(0, 1, 3, 2).reshape(B, dil, Lp, H)[:, :, :L]
    lse = lse.transpose(0, 2, 1, 3).reshape(B, S, H)
    return o, lse


def mixer_b(u, w_in, q_g, w_out, k_sh, v_sh, pos):
    B, S, _ = u.shape
    proj = u @ w_in
    q_all = proj[..., :N_GROUPS * ATTN_WIDTH].reshape(B, S, N_GROUPS, N_HEADS, HEAD_DIM)
    gate = proj[..., N_GROUPS * ATTN_WIDTH:]
    outs, lses = [], []
    for g, (window, dil) in enumerate(DILATION_GROUPS):
        q = rope_partial(rmsnorm(q_all[:, :, g], q_g[g]), pos)
        o, lse = dilated_attention(q, k_sh, v_sh, window // dil, dil)
        outs.append(o)
        lses.append(lse)
    wts = jax.nn.softmax(jnp.stack(lses, axis=0), axis=0)
    o = jnp.einsum('gbsh,gbshd->bshd', wts.astype(outs[0].dtype), jnp.stack(outs, axis=0))
    return (o.reshape(B, S, ATTN_WIDTH) * jax.nn.silu(gate)) @ w_out


def _fwd_setup_inputs(seed: int = 0) -> dict:
    key = jax.random.key(seed)
    ks = jax.random.split(key, 20)
    f32 = jnp.float32

    def nrm(k, shape, fan_in):
        return jax.random.normal(k, shape, f32) * (fan_in ** -0.5)

    def gain(k, shape):
        return 1.0 + 0.02 * jax.random.normal(k, shape, f32)

    return {
        "x": jax.random.normal(ks[0], (BATCH, SEQ, D_MODEL), f32),
        "p": jax.random.normal(ks[1], (DEPTH, BATCH, SEQ, PLE_DIM), f32),
        "norm_g": gain(ks[2], (DEPTH, D_MODEL)),
        "w_in_a": nrm(ks[3], (N_A_LAYERS, D_MODEL, 3 * CONV_CH), D_MODEL),
        "conv_w": nrm(ks[4], (N_A_LAYERS, CONV_WIDTH, CONV_CH), CONV_WIDTH),
        "conv_b": 0.02 * jax.random.normal(ks[5], (N_A_LAYERS, CONV_CH), f32),
        "ln_g": gain(ks[6], (N_A_LAYERS, CONV_CH)),
        "ln_b": 0.02 * jax.random.normal(ks[7], (N_A_LAYERS, CONV_CH), f32),
        "w_out_a": nrm(ks[8], (N_A_LAYERS, CONV_CH, D_MODEL), CONV_CH),
        "kv_norm_g": gain(ks[9], (D_MODEL,)),
        "w_kv": nrm(ks[10], (D_MODEL, 2 * ATTN_WIDTH), D_MODEL),
        "k_norm_g": gain(ks[11], (HEAD_DIM,)),
        "w_in_b": nrm(ks[12], (N_B_LAYERS, D_MODEL, (N_GROUPS + 1) * ATTN_WIDTH), D_MODEL),
        "q_norm_g": gain(ks[13], (N_B_LAYERS, N_GROUPS, HEAD_DIM)),
        "w_out_b": nrm(ks[14], (N_B_LAYERS, ATTN_WIDTH, D_MODEL), ATTN_WIDTH),
        "ple_norm_g": gain(ks[15], (DEPTH, D_MODEL)),
        "w_ple_gate": nrm(ks[16], (DEPTH, D_MODEL, D_MODEL), D_MODEL),
        "w_ple_proj": nrm(ks[17], (DEPTH, PLE_DIM, D_MODEL), PLE_DIM),
    }


def _fwd_reference(x, p, norm_g, w_in_a, conv_w, conv_b, ln_g, ln_b, w_out_a,
              kv_norm_g, w_kv, k_norm_g, w_in_b, q_norm_g, w_out_b,
              ple_norm_g, w_ple_gate, w_ple_proj):
    B, S, _ = x.shape
    pos = jnp.arange(S)
    k_sh = None
    v_sh = None
    for i in range(DEPTH):
        u = rmsnorm(x, norm_g[i])
        if i < N_A_LAYERS:
            j = i
            h = x + mixer_a(u, w_in_a[j], conv_w[j], conv_b[j], ln_g[j], ln_b[j], w_out_a[j])
        else:
            j = i - N_A_LAYERS
            h = x + mixer_b(u, w_in_b[j], q_norm_g[j], w_out_b[j], k_sh, v_sh, pos)
        ple_gate = jax.nn.sigmoid(rmsnorm(h, ple_norm_g[i]) @ w_ple_gate[i])
        h = h + ple_gate * (p[i] @ w_ple_proj[i])
        if i == N_A_LAYERS - 1:
            k_sh, v_sh = jnp.split(rmsnorm(h, kv_norm_g) @ w_kv, 2, axis=-1)
            k_sh = rope_partial(rmsnorm(k_sh.reshape(B, S, N_HEADS, HEAD_DIM), k_norm_g), pos)
            v_sh = v_sh.reshape(B, S, N_HEADS, HEAD_DIM)
        x = h
    return x


import jax as _jax
import jax.numpy as _jnp

TWIN_FORMAT = 'train_step'
FWD_PARAMS = ['x', 'p', 'norm_g', 'w_in_a', 'conv_w', 'conv_b', 'ln_g', 'ln_b', 'w_out_a', 'kv_norm_g', 'w_kv', 'k_norm_g', 'w_in_b', 'q_norm_g', 'w_out_b', 'ple_norm_g', 'w_ple_gate', 'w_ple_proj']
TWIN_WEIGHTS = ['norm_g', 'w_in_a', 'conv_w', 'conv_b', 'ln_g', 'ln_b', 'w_out_a', 'kv_norm_g', 'w_kv', 'k_norm_g', 'w_in_b', 'q_norm_g', 'w_out_b', 'ple_norm_g', 'w_ple_gate', 'w_ple_proj']
TWIN_DIFF_INPUT = 'x'
TWIN_INPUTS = ['x', 'p', 'norm_g', 'w_in_a', 'conv_w', 'conv_b', 'ln_g', 'ln_b', 'w_out_a', 'kv_norm_g', 'w_kv', 'k_norm_g', 'w_in_b', 'q_norm_g', 'w_out_b', 'ple_norm_g', 'w_ple_gate', 'w_ple_proj', 'loss_target', 'm_norm_g', 'm_w_in_a', 'm_conv_w', 'm_conv_b', 'm_ln_g', 'm_ln_b', 'm_w_out_a', 'm_kv_norm_g', 'm_w_kv', 'm_k_norm_g', 'm_w_in_b', 'm_q_norm_g', 'm_w_out_b', 'm_ple_norm_g', 'm_w_ple_gate', 'm_w_ple_proj', 'v_norm_g', 'v_w_in_a', 'v_conv_w', 'v_conv_b', 'v_ln_g', 'v_ln_b', 'v_w_out_a', 'v_kv_norm_g', 'v_w_kv', 'v_k_norm_g', 'v_w_in_b', 'v_q_norm_g', 'v_w_out_b', 'v_ple_norm_g', 'v_w_ple_gate', 'v_w_ple_proj']
TWIN_OUTPUTS = ['loss', 'grad_x', 'grad_norm_g', 'grad_w_in_a', 'grad_conv_w', 'grad_conv_b', 'grad_ln_g', 'grad_ln_b', 'grad_w_out_a', 'grad_kv_norm_g', 'grad_w_kv', 'grad_k_norm_g', 'grad_w_in_b', 'grad_q_norm_g', 'grad_w_out_b', 'grad_ple_norm_g', 'grad_w_ple_gate', 'grad_w_ple_proj', 'delta_norm_g', 'delta_w_in_a', 'delta_conv_w', 'delta_conv_b', 'delta_ln_g', 'delta_ln_b', 'delta_w_out_a', 'delta_kv_norm_g', 'delta_w_kv', 'delta_k_norm_g', 'delta_w_in_b', 'delta_q_norm_g', 'delta_w_out_b', 'delta_ple_norm_g', 'delta_w_ple_gate', 'delta_w_ple_proj', 'new_m_norm_g', 'new_m_w_in_a', 'new_m_conv_w', 'new_m_conv_b', 'new_m_ln_g', 'new_m_ln_b', 'new_m_w_out_a', 'new_m_kv_norm_g', 'new_m_w_kv', 'new_m_k_norm_g', 'new_m_w_in_b', 'new_m_q_norm_g', 'new_m_w_out_b', 'new_m_ple_norm_g', 'new_m_w_ple_gate', 'new_m_w_ple_proj', 'new_v_norm_g', 'new_v_w_in_a', 'new_v_conv_w', 'new_v_conv_b', 'new_v_ln_g', 'new_v_ln_b', 'new_v_w_out_a', 'new_v_kv_norm_g', 'new_v_w_kv', 'new_v_k_norm_g', 'new_v_w_in_b', 'new_v_q_norm_g', 'new_v_w_out_b', 'new_v_ple_norm_g', 'new_v_w_ple_gate', 'new_v_w_ple_proj']
TWIN_LEAF_KINDS = {'loss': 'loss', 'grad_x': 'grad_x', 'grad_norm_g': 'grad_w', 'grad_w_in_a': 'grad_w', 'grad_conv_w': 'grad_w', 'grad_conv_b': 'grad_w', 'grad_ln_g': 'grad_w', 'grad_ln_b': 'grad_w', 'grad_w_out_a': 'grad_w', 'grad_kv_norm_g': 'grad_w', 'grad_w_kv': 'grad_w', 'grad_k_norm_g': 'grad_w', 'grad_w_in_b': 'grad_w', 'grad_q_norm_g': 'grad_w', 'grad_w_out_b': 'grad_w', 'grad_ple_norm_g': 'grad_w', 'grad_w_ple_gate': 'grad_w', 'grad_w_ple_proj': 'grad_w', 'delta_norm_g': 'delta_w', 'delta_w_in_a': 'delta_w', 'delta_conv_w': 'delta_w', 'delta_conv_b': 'delta_w', 'delta_ln_g': 'delta_w', 'delta_ln_b': 'delta_w', 'delta_w_out_a': 'delta_w', 'delta_kv_norm_g': 'delta_w', 'delta_w_kv': 'delta_w', 'delta_k_norm_g': 'delta_w', 'delta_w_in_b': 'delta_w', 'delta_q_norm_g': 'delta_w', 'delta_w_out_b': 'delta_w', 'delta_ple_norm_g': 'delta_w', 'delta_w_ple_gate': 'delta_w', 'delta_w_ple_proj': 'delta_w', 'new_m_norm_g': 'new_m', 'new_m_w_in_a': 'new_m', 'new_m_conv_w': 'new_m', 'new_m_conv_b': 'new_m', 'new_m_ln_g': 'new_m', 'new_m_ln_b': 'new_m', 'new_m_w_out_a': 'new_m', 'new_m_kv_norm_g': 'new_m', 'new_m_w_kv': 'new_m', 'new_m_k_norm_g': 'new_m', 'new_m_w_in_b': 'new_m', 'new_m_q_norm_g': 'new_m', 'new_m_w_out_b': 'new_m', 'new_m_ple_norm_g': 'new_m', 'new_m_w_ple_gate': 'new_m', 'new_m_w_ple_proj': 'new_m', 'new_v_norm_g': 'new_v', 'new_v_w_in_a': 'new_v', 'new_v_conv_w': 'new_v', 'new_v_conv_b': 'new_v', 'new_v_ln_g': 'new_v', 'new_v_ln_b': 'new_v', 'new_v_w_out_a': 'new_v', 'new_v_kv_norm_g': 'new_v', 'new_v_w_kv': 'new_v', 'new_v_k_norm_g': 'new_v', 'new_v_w_in_b': 'new_v', 'new_v_q_norm_g': 'new_v', 'new_v_w_out_b': 'new_v', 'new_v_ple_norm_g': 'new_v', 'new_v_w_ple_gate': 'new_v', 'new_v_w_ple_proj': 'new_v'}


def _forward(args):
    return _fwd_reference(*[args[k] for k in FWD_PARAMS])


def _output_shape():
    out = _jax.eval_shape(lambda: _forward(_fwd_setup_inputs(0)))
    return out.shape, out.dtype

N_MICROBATCH = 1
ADAM_LR = 0.001
ADAM_B1 = 0.9
ADAM_B2 = 0.999
ADAM_EPS = 1e-08
ADAM_WD = 0.01
ADAM_STEP = 10
PER_EXAMPLE_BATCH_AXIS = {'x': 0, 'p': 1, 'loss_target': 0}
SHARED_INPUTS = []
_WEIGHT_DTYPES = {'norm_g': _jnp.float32, 'w_in_a': _jnp.float32, 'conv_w': _jnp.float32, 'conv_b': _jnp.float32, 'ln_g': _jnp.float32, 'ln_b': _jnp.float32, 'w_out_a': _jnp.float32, 'kv_norm_g': _jnp.float32, 'w_kv': _jnp.float32, 'k_norm_g': _jnp.float32, 'w_in_b': _jnp.float32, 'q_norm_g': _jnp.float32, 'w_out_b': _jnp.float32, 'ple_norm_g': _jnp.float32, 'w_ple_gate': _jnp.float32, 'w_ple_proj': _jnp.float32}
MOMENT_SCALE = {'norm_g': 6.454719e+00, 'w_in_a': 1.226989e-01, 'conv_w': 1.422351e-01, 'conv_b': 9.854318e-01, 'ln_g': 4.686372e+00, 'ln_b': 3.109087e+00, 'w_out_a': 2.715918e-01, 'kv_norm_g': 1.556041e-01, 'w_kv': 5.468405e-02, 'k_norm_g': 1.132346e+00, 'w_in_b': 3.832054e-02, 'q_norm_g': 4.139655e-01, 'w_out_b': 5.696554e-02, 'ple_norm_g': 1.890821e+00, 'w_ple_gate': 1.522137e-01, 'w_ple_proj': 6.884557e-01}


def _to_microbatches(a, axis):
    t = _jnp.moveaxis(a, axis, 0)
    t = t.reshape((N_MICROBATCH, t.shape[0] // N_MICROBATCH) + t.shape[1:])
    return _jnp.moveaxis(t, 1, axis + 1)


def setup_inputs(seed: int = 0) -> dict:
    inp = _fwd_setup_inputs(seed)
    key = _jax.random.fold_in(_jax.random.key(seed), 7919)
    shape, _ = _output_shape()
    out = dict(inp)
    out["loss_target"] = _jax.random.normal(_jax.random.fold_in(key, 0), shape, _jnp.float32)
    for i, name in enumerate(TWIN_WEIGHTS):
        w = inp[name].astype(_jnp.float32)
        if MOMENT_SCALE is None:
            s = _jnp.sqrt(_jnp.mean(_jnp.square(w)) + 1e-30)
        else:
            s = MOMENT_SCALE[name]
        km, kv = _jax.random.split(_jax.random.fold_in(key, i + 1))
        out[name] = w
        out["m_" + name] = s * _jax.random.normal(km, w.shape, _jnp.float32)
        out["v_" + name] = (s * s) * _jax.random.uniform(kv, w.shape, _jnp.float32, 0.5, 1.5)
    if N_MICROBATCH > 1:
        for name, axis in PER_EXAMPLE_BATCH_AXIS.items():
            out[name] = _to_microbatches(out[name], axis)
    return {'x': out['x'], 'p': out['p'], 'norm_g': out['norm_g'], 'w_in_a': out['w_in_a'], 'conv_w': out['conv_w'], 'conv_b': out['conv_b'], 'ln_g': out['ln_g'], 'ln_b': out['ln_b'], 'w_out_a': out['w_out_a'], 'kv_norm_g': out['kv_norm_g'], 'w_kv': out['w_kv'], 'k_norm_g': out['k_norm_g'], 'w_in_b': out['w_in_b'], 'q_norm_g': out['q_norm_g'], 'w_out_b': out['w_out_b'], 'ple_norm_g': out['ple_norm_g'], 'w_ple_gate': out['w_ple_gate'], 'w_ple_proj': out['w_ple_proj'], 'loss_target': out['loss_target'], 'm_norm_g': out['m_norm_g'], 'm_w_in_a': out['m_w_in_a'], 'm_conv_w': out['m_conv_w'], 'm_conv_b': out['m_conv_b'], 'm_ln_g': out['m_ln_g'], 'm_ln_b': out['m_ln_b'], 'm_w_out_a': out['m_w_out_a'], 'm_kv_norm_g': out['m_kv_norm_g'], 'm_w_kv': out['m_w_kv'], 'm_k_norm_g': out['m_k_norm_g'], 'm_w_in_b': out['m_w_in_b'], 'm_q_norm_g': out['m_q_norm_g'], 'm_w_out_b': out['m_w_out_b'], 'm_ple_norm_g': out['m_ple_norm_g'], 'm_w_ple_gate': out['m_w_ple_gate'], 'm_w_ple_proj': out['m_w_ple_proj'], 'v_norm_g': out['v_norm_g'], 'v_w_in_a': out['v_w_in_a'], 'v_conv_w': out['v_conv_w'], 'v_conv_b': out['v_conv_b'], 'v_ln_g': out['v_ln_g'], 'v_ln_b': out['v_ln_b'], 'v_w_out_a': out['v_w_out_a'], 'v_kv_norm_g': out['v_kv_norm_g'], 'v_w_kv': out['v_w_kv'], 'v_k_norm_g': out['v_k_norm_g'], 'v_w_in_b': out['v_w_in_b'], 'v_q_norm_g': out['v_q_norm_g'], 'v_w_out_b': out['v_w_out_b'], 'v_ple_norm_g': out['v_ple_norm_g'], 'v_w_ple_gate': out['v_w_ple_gate'], 'v_w_ple_proj': out['v_w_ple_proj']}


def _loss(weights, diff, rest, loss_target):
    with _jax.named_scope("forward"):
        args = {**rest, TWIN_DIFF_INPUT: diff, **{k: w.astype(_WEIGHT_DTYPES[k]) for k, w in weights.items()}}
        y = _forward(args)
    with _jax.named_scope("loss_head"):
        err = _jnp.square(y.astype(_jnp.float32) - loss_target)
        return 0.5 * _jnp.sum(_jnp.mean(err, axis=-1)) if err.ndim else 0.5 * err


def _adamw(w, g, m, v):
    m = ADAM_B1 * m + (1.0 - ADAM_B1) * g
    v = ADAM_B2 * v + (1.0 - ADAM_B2) * _jnp.square(g)
    m_hat = m / (1.0 - ADAM_B1 ** ADAM_STEP)
    v_hat = v / (1.0 - ADAM_B2 ** ADAM_STEP)
    delta = -ADAM_LR * (m_hat / (_jnp.sqrt(v_hat) + ADAM_EPS) + ADAM_WD * w)
    return delta, m, v


def reference(x, p, norm_g, w_in_a, conv_w, conv_b, ln_g, ln_b, w_out_a, kv_norm_g, w_kv, k_norm_g, w_in_b, q_norm_g, w_out_b, ple_norm_g, w_ple_gate, w_ple_proj, loss_target, m_norm_g, m_w_in_a, m_conv_w, m_conv_b, m_ln_g, m_ln_b, m_w_out_a, m_kv_norm_g, m_w_kv, m_k_norm_g, m_w_in_b, m_q_norm_g, m_w_out_b, m_ple_norm_g, m_w_ple_gate, m_w_ple_proj, v_norm_g, v_w_in_a, v_conv_w, v_conv_b, v_ln_g, v_ln_b, v_w_out_a, v_kv_norm_g, v_w_kv, v_k_norm_g, v_w_in_b, v_q_norm_g, v_w_out_b, v_ple_norm_g, v_w_ple_gate, v_w_ple_proj):
    given = dict(x=x, p=p, norm_g=norm_g, w_in_a=w_in_a, conv_w=conv_w, conv_b=conv_b, ln_g=ln_g, ln_b=ln_b, w_out_a=w_out_a, kv_norm_g=kv_norm_g, w_kv=w_kv, k_norm_g=k_norm_g, w_in_b=w_in_b, q_norm_g=q_norm_g, w_out_b=w_out_b, ple_norm_g=ple_norm_g, w_ple_gate=w_ple_gate, w_ple_proj=w_ple_proj, loss_target=loss_target, m_norm_g=m_norm_g, m_w_in_a=m_w_in_a, m_conv_w=m_conv_w, m_conv_b=m_conv_b, m_ln_g=m_ln_g, m_ln_b=m_ln_b, m_w_out_a=m_w_out_a, m_kv_norm_g=m_kv_norm_g, m_w_kv=m_w_kv, m_k_norm_g=m_k_norm_g, m_w_in_b=m_w_in_b, m_q_norm_g=m_q_norm_g, m_w_out_b=m_w_out_b, m_ple_norm_g=m_ple_norm_g, m_w_ple_gate=m_w_ple_gate, m_w_ple_proj=m_w_ple_proj, v_norm_g=v_norm_g, v_w_in_a=v_w_in_a, v_conv_w=v_conv_w, v_conv_b=v_conv_b, v_ln_g=v_ln_g, v_ln_b=v_ln_b, v_w_out_a=v_w_out_a, v_kv_norm_g=v_kv_norm_g, v_w_kv=v_w_kv, v_k_norm_g=v_k_norm_g, v_w_in_b=v_w_in_b, v_q_norm_g=v_q_norm_g, v_w_out_b=v_w_out_b, v_ple_norm_g=v_ple_norm_g, v_w_ple_gate=v_w_ple_gate, v_w_ple_proj=v_w_ple_proj)
    weights = {n: given[n] for n in TWIN_WEIGHTS}
    shared = {n: given[n] for n in SHARED_INPUTS}
    per_example = {n: given[n] for n in ['x', 'p']}
    grad_fn = _jax.value_and_grad(_loss, argnums=(0, 1))

    def one_microbatch(ex, loss_target):
        ex = dict(ex)
        diff = ex.pop(TWIN_DIFF_INPUT)
        return grad_fn(weights, diff, {**shared, **ex}, loss_target)

    if N_MICROBATCH == 1:
        loss, (grad_w, grad_x) = one_microbatch(per_example, given["loss_target"])
    else:
        def body(carry, xs):
            loss_sum, grad_sum = carry
            l_k, (gw_k, gx_k) = one_microbatch(xs[0], xs[1])
            with _jax.named_scope("update"):
                return (loss_sum + l_k, _jax.tree.map(_jnp.add, grad_sum, gw_k)), gx_k

        init = (_jnp.zeros((), _jnp.float32), _jax.tree.map(_jnp.zeros_like, weights))
        (loss, grad_w), grad_x = _jax.lax.scan(body, init, (per_example, given["loss_target"]))
    with _jax.named_scope("update"):
        delta_w, new_m, new_v = {}, {}, {}
        for n in TWIN_WEIGHTS:
            delta_w[n], new_m[n], new_v[n] = _adamw(weights[n], grad_w[n], given["m_" + n], given["v_" + n])
    return (loss, grad_x, *[grad_w[n] for n in TWIN_WEIGHTS], *[delta_w[n] for n in TWIN_WEIGHTS],
            *[new_m[n] for n in TWIN_WEIGHTS], *[new_v[n] for n in TWIN_WEIGHTS])
```

```python
import functools

import jax
import jax.numpy as jnp
from jax import lax
from jax.experimental import pallas as pl
from jax.experimental.pallas import tpu as pltpu

F32 = jnp.float32
BF16 = jnp.bfloat16
MESH = pl.DeviceIdType.MESH

EPS = 1e-6
NEG_INF = -1e30
HEAD_DIM = 64
ROPE_DIM = 16
ROPE_THETA = 500000.0
CONV_WIDTH = 31
HALO = 32
SPAN = 128
DILATIONS = (1, 4, 16)
ADAM_LR, ADAM_B1, ADAM_B2, ADAM_EPS, ADAM_WD, ADAM_STEP = 0.001, 0.9, 0.999, 1e-08, 0.01, 10
N_CHIPS = 4
VMEM_LIMIT = 56 * 1024 * 1024


def _tile(n, target, mult=128):
    best = None
    t = mult
    while t <= min(n, target):
        if n % t == 0:
            best = t
        t += mult
    return best if best is not None else n


def _params(*sem):
    return pltpu.CompilerParams(dimension_semantics=tuple(sem) if sem else None, vmem_limit_bytes=VMEM_LIMIT)


def _sigmoid(x):
    return 1.0 / (1.0 + jnp.exp(-x))


def _seg_allsum64(x):
    w = x.shape[1]
    lane = lax.broadcasted_iota(jnp.int32, x.shape, 1)
    for s in (1, 2, 4, 8, 16, 32):
        up = pltpu.roll(x, s, 1)
        dn = pltpu.roll(x, w - s, 1)
        x = x + jnp.where((lane & s) != 0, up, dn)
    return x


def _colsum(x):
    return jnp.sum(x, axis=0, keepdims=True)


def _w_index(kind, per_k, per_n):
    if kind == "col":
        return lambda kb, nb: (nb // per_n, kb, nb % per_n)
    return lambda kb, nb: (kb // per_k, kb % per_k, nb)


def _mm_nn(name, a, w, kind, *, out_dtype=F32, resid=None, a_col0=0):
    t = a.shape[0]
    _, r, c = w.shape
    k, n = (r, N_CHIPS * c) if kind == "col" else (N_CHIPS * r, c)
    tm = _tile(t, 512, 8)
    tk = _tile(r, 1024) if kind == "row" else _tile(k, 1024)
    tn = _tile(c, 512)
    nk = k // tk
    widx = _w_index(kind, r // tk if kind == "row" else 1, c // tn)

    def body(*refs):
        if resid is None:
            a_ref, w_ref, o_ref, acc = refs
        else:
            a_ref, w_ref, r_ref, o_ref, acc = refs
        kk = pl.program_id(2)

        @pl.when(kk == 0)
        def _():
            acc[...] = jnp.zeros_like(acc)

        acc[...] += jnp.dot(a_ref[...].astype(BF16), w_ref[...], preferred_element_type=F32)

        @pl.when(kk == nk - 1)
        def _():
            out = acc[...]
            if resid is not None:
                out = out + r_ref[...]
            o_ref[...] = out.astype(out_dtype)

    in_specs = [pl.BlockSpec((tm, tk), lambda i, j, kk: (i, a_col0 * nk + kk)),
                pl.BlockSpec((None, tk, tn), lambda i, j, kk: widx(kk, j))]
    args = [a, w]
    if resid is not None:
        in_specs.append(pl.BlockSpec((tm, tn), lambda i, j, kk: (i, j)))
        args.append(resid)
    return pl.pallas_call(
        body, name=name, grid=(t // tm, n // tn, nk), in_specs=in_specs,
        out_specs=pl.BlockSpec((tm, tn), lambda i, j, kk: (i, j)),
        out_shape=jax.ShapeDtypeStruct((t, n), out_dtype),
        scratch_shapes=[pltpu.VMEM((tm, tn), F32)],
        compiler_params=_params("parallel", "parallel", "arbitrary"))(*args)


def _mm_nt(name, d, w, kind, *, out_dtype=F32):
    t = d.shape[0]
    _, r, c = w.shape
    k, n = (r, N_CHIPS * c) if kind == "col" else (N_CHIPS * r, c)
    tm = _tile(t, 512, 8)
    to = _tile(r, 512)
    tc = _tile(c, 1024)
    nc = n // tc
    widx = _w_index(kind, r // to, c // tc)

    def body(d_ref, w_ref, o_ref, acc):
        kk = pl.program_id(2)

        @pl.when(kk == 0)
        def _():
            acc[...] = jnp.zeros_like(acc)

        acc[...] += lax.dot_general(d_ref[...].astype(BF16), w_ref[...], (((1,), (1,)), ((), ())),
                                    preferred_element_type=F32)

        @pl.when(kk == nc - 1)
        def _():
            o_ref[...] = acc[...].astype(out_dtype)

    return pl.pallas_call(
        body, name=name, grid=(t // tm, k // to, nc),
        in_specs=[pl.BlockSpec((tm, tc), lambda i, j, kk: (i, kk)),
                  pl.BlockSpec((None, to, tc), lambda i, j, kk: widx(j, kk))],
        out_specs=pl.BlockSpec((tm, to), lambda i, j, kk: (i, j)),
        out_shape=jax.ShapeDtypeStruct((t, k), out_dtype),
        scratch_shapes=[pltpu.VMEM((tm, to), F32)],
        compiler_params=_params("parallel", "parallel", "arbitrary"))(d, w)


def _mm_tn(name, a, d, kind, shard_shape, *, a_col0=0):
    t = a.shape[0]
    r, c = shard_shape
    k, n = (r, N_CHIPS * c) if kind == "col" else (N_CHIPS * r, c)
    tkm = _tile(r, 512)
    tn = _tile(c, 512)
    tt = _tile(t, 1024, 8)
    nt = t // tt
    widx = _w_index(kind, r // tkm, c // tn)

    def body(a_ref, d_ref, o_ref, acc):
        kk = pl.program_id(2)

        @pl.when(kk == 0)
        def _():
            acc[...] = jnp.zeros_like(acc)

        acc[...] += lax.dot_general(a_ref[...].astype(BF16), d_ref[...].astype(BF16), (((0,), (0,)), ((), ())),
                                    preferred_element_type=F32)

        @pl.when(kk == nt - 1)
        def _():
            o_ref[...] = acc[...].astype(BF16)

    return pl.pallas_call(
        body, name=name, grid=(k // tkm, n // tn, nt),
        in_specs=[pl.BlockSpec((tt, tkm), lambda i, j, kk: (kk, a_col0 * (k // tkm) + i)),
                  pl.BlockSpec((tt, tn), lambda i, j, kk: (kk, j))],
        out_specs=pl.BlockSpec((None, tkm, tn), lambda i, j, kk: widx(i, j)),
        out_shape=jax.ShapeDtypeStruct((N_CHIPS, r, c), BF16),
        scratch_shapes=[pltpu.VMEM((tkm, tn), F32)],
        compiler_params=_params("parallel", "parallel", "arbitrary"))(a, d)


def _row_spec(tr, w, col=0):
    return pl.BlockSpec((tr, w), lambda i: (i, col))


def _full_spec(shape):
    return pl.BlockSpec(shape, lambda i: tuple(0 for _ in shape))


def _rms_fwd(name, x, gains):
    t, dm = x.shape
    tr = _tile(t, 256, 8)
    n = len(gains)

    def body(x_ref, *refs):
        xv = x_ref[...]
        xn = xv * lax.rsqrt(jnp.mean(xv * xv, axis=-1, keepdims=True) + EPS)
        for g_ref, o_ref in zip(refs[:n], refs[n:]):
            o_ref[...] = (xn * g_ref[...]).astype(BF16)

    outs = pl.pallas_call(
        body, name=name, grid=(t // tr,),
        in_specs=[_row_spec(tr, dm)] + [_full_spec((1, dm))] * n,
        out_specs=[_row_spec(tr, dm)] * n,
        out_shape=[jax.ShapeDtypeStruct((t, dm), BF16)] * n,
        compiler_params=_params("parallel"))(x, *gains)
    return list(outs)


def _rms_bwd(name, x, resid, pairs):
    t, dm = x.shape
    tr = _tile(t, 256, 8)
    n = len(pairs)

    def body(x_ref, r_ref, *refs):
        ins, outs = refs[:2 * n], refs[2 * n:]
        i = pl.program_id(0)
        xv = x_ref[...]
        rs = lax.rsqrt(jnp.mean(xv * xv, axis=-1, keepdims=True) + EPS)
        xn = xv * rs
        total = r_ref[...]
        for kx in range(n):
            g_ref, du_ref = ins[2 * kx], ins[2 * kx + 1]
            dg_ref = outs[1 + kx]
            du = du_ref[...]

            @pl.when(i == 0)
            def _():
                dg_ref[...] = jnp.zeros_like(dg_ref)

            dg_ref[...] += _colsum(du * xn)
            dxh = du * g_ref[...]
            total = total + rs * (dxh - xn * jnp.mean(dxh * xn, axis=-1, keepdims=True))
        outs[0][...] = total

    in_specs = [_row_spec(tr, dm), _row_spec(tr, dm)]
    args = [x, resid]
    for g, du in pairs:
        in_specs += [_full_spec((1, dm)), _row_spec(tr, dm)]
        args += [g, du]
    outs = pl.pallas_call(
        body, name=name, grid=(t // tr,), in_specs=in_specs,
        out_specs=[_row_spec(tr, dm)] + [_full_spec((1, dm))] * n,
        out_shape=[jax.ShapeDtypeStruct((t, dm), F32)] + [jax.ShapeDtypeStruct((1, dm), F32)] * n,
        compiler_params=_params("arbitrary"))(*args)
    return outs[0], list(outs[1:])


def _glu_ext(ext, a_ref, b_ref, ah_ref, bh_ref, first, tr):
    gh = ah_ref[...] * _sigmoid(bh_ref[...])
    ext[pl.ds(0, HALO), :] = jnp.where(first, 0.0, gh)
    ext[pl.ds(HALO, tr), :] = a_ref[...] * _sigmoid(b_ref[...])


def _mixa_fwd(name, pa, cw, cb, lg, lb, seq):
    t, w3 = pa.shape
    cc = w3 // 3
    tr = _tile(seq, 128, HALO)
    per_seq = seq // tr
    hb = tr // HALO

    def body(a_ref, b_ref, z_ref, ah_ref, bh_ref, cw_ref, cb_ref, lg_ref, lb_ref, c_ref, m_ref, ext):
        i = pl.program_id(0)
        _glu_ext(ext, a_ref, b_ref, ah_ref, bh_ref, (i % per_seq) == 0, tr)
        acc = jnp.broadcast_to(cb_ref[...], (tr, cc))
        for k in range(CONV_WIDTH):
            acc = acc + ext[pl.ds(HALO - (CONV_WIDTH - 1) + k, tr), :] * cw_ref[pl.ds(k, 1), :]
        c_ref[...] = acc
        xc = acc - jnp.mean(acc, axis=-1, keepdims=True)
        nrm = xc * lax.rsqrt(jnp.mean(xc * xc, axis=-1, keepdims=True) + EPS)
        l = nrm * lg_ref[...] + lb_ref[...]
        z = z_ref[...]
        m_ref[...] = (l * _sigmoid(l) * z * _sigmoid(z)).astype(BF16)

    halo = lambda col: pl.BlockSpec((HALO, cc), lambda i: (jnp.maximum(i * hb - 1, 0), col))
    return pl.pallas_call(
        body, name=name, grid=(t // tr,),
        in_specs=[_row_spec(tr, cc, 0), _row_spec(tr, cc, 1), _row_spec(tr, cc, 2), halo(0), halo(1),
                  _full_spec((HALO, cc)), _full_spec((1, cc)), _full_spec((1, cc)), _full_spec((1, cc))],
        out_specs=[_row_spec(tr, cc), _row_spec(tr, cc)],
        out_shape=[jax.ShapeDtypeStruct((t, cc), F32), jax.ShapeDtypeStruct((t, cc), BF16)],
        scratch_shapes=[pltpu.VMEM((tr + HALO, cc), F32)],
        compiler_params=_params("parallel"))(pa, pa, pa, pa, pa, cw, cb, lg, lb)


def _mixa_bwd1(name, c, pa, dm, lg, lb):
    t, cc = c.shape
    tr = _tile(t, 128, 8)

    def body(c_ref, z_ref, dm_ref, lg_ref, lb_ref, dc_ref, dz_ref, dlg_ref, dlb_ref, dcb_ref):
        i = pl.program_id(0)
        cv = c_ref[...]
        xc = cv - jnp.mean(cv, axis=-1, keepdims=True)
        rs = lax.rsqrt(jnp.mean(xc * xc, axis=-1, keepdims=True) + EPS)
        nrm = xc * rs
        l = nrm * lg_ref[...] + lb_ref[...]
        z = z_ref[...]
        sl, sz = _sigmoid(l), _sigmoid(z)
        dmv = dm_ref[...]
        ds = dmv * (z * sz)
        dzz = dmv * (l * sl)
        dz_ref[...] = (dzz * (sz * (1.0 + z * (1.0 - sz)))).astype(BF16)
        dl = ds * (sl * (1.0 + l * (1.0 - sl)))
        dn = dl * lg_ref[...]
        dc = rs * (dn - jnp.mean(dn, axis=-1, keepdims=True) - nrm * jnp.mean(dn * nrm, axis=-1, keepdims=True))
        dc_ref[...] = dc

        @pl.when(i == 0)
        def _():
            dlg_ref[...] = jnp.zeros_like(dlg_ref)
            dlb_ref[...] = jnp.zeros_like(dlb_ref)
            dcb_ref[...] = jnp.zeros_like(dcb_ref)

        dlg_ref[...] += _colsum(dl * nrm)
        dlb_ref[...] += _colsum(dl)
        dcb_ref[...] += _colsum(dc)

    vec = jax.ShapeDtypeStruct((1, cc), F32)
    return pl.pallas_call(
        body, name=name, grid=(t // tr,),
        in_specs=[_row_spec(tr, cc), _row_spec(tr, cc, 2), _row_spec(tr, cc), _full_spec((1, cc)), _full_spec((1, cc))],
        out_specs=[_row_spec(tr, cc), _row_spec(tr, cc)] + [_full_spec((1, cc))] * 3,
        out_shape=[jax.ShapeDtypeStruct((t, cc), F32), jax.ShapeDtypeStruct((t, cc), BF16), vec, vec, vec],
        compiler_params=_params("arbitrary"))(c, pa, dm, lg, lb)


def _mixa_bwd2(name, dc, pa, dz, cw, seq):
    t, cc = dc.shape
    tr = _tile(seq, 128, HALO)
    per_seq = seq // tr
    hb = tr // HALO
    last_halo = t // HALO - 1

    def body(dc_ref, dcn_ref, a_ref, b_ref, ah_ref, bh_ref, dz_ref, cw_ref, dp_ref, dcw_ref, ext_g, ext_d):
        i = pl.program_id(0)
        pos = i % per_seq
        _glu_ext(ext_g, a_ref, b_ref, ah_ref, bh_ref, pos == 0, tr)
        dcv = dc_ref[...]
        ext_d[pl.ds(0, tr), :] = dcv
        ext_d[pl.ds(tr, HALO), :] = jnp.where(pos == per_seq - 1, 0.0, dcn_ref[...])

        @pl.when(i == 0)
        def _():
            dcw_ref[...] = jnp.zeros_like(dcw_ref)

        dglu = jnp.zeros((tr, cc), F32)
        for k in range(CONV_WIDTH):
            dglu = dglu + ext_d[pl.ds(CONV_WIDTH - 1 - k, tr), :] * cw_ref[pl.ds(k, 1), :]
            dcw_ref[pl.ds(k, 1), :] += _colsum(dcv * ext_g[pl.ds(HALO - (CONV_WIDTH - 1) + k, tr), :])
        av = a_ref[...]
        sb = _sigmoid(b_ref[...])
        dp_ref[:, pl.ds(0, cc)] = (dglu * sb).astype(BF16)
        dp_ref[:, pl.ds(cc, cc)] = (dglu * av * sb * (1.0 - sb)).astype(BF16)
        dp_ref[:, pl.ds(2 * cc, cc)] = dz_ref[...]

    prev = lambda col: pl.BlockSpec((HALO, cc), lambda i: (jnp.maximum(i * hb - 1, 0), col))
    nxt = pl.BlockSpec((HALO, cc), lambda i: (jnp.minimum((i + 1) * hb, last_halo), 0))
    return pl.pallas_call(
        body, name=name, grid=(t // tr,),
        in_specs=[_row_spec(tr, cc), nxt, _row_spec(tr, cc, 0), _row_spec(tr, cc, 1), prev(0), prev(1),
                  _row_spec(tr, cc), _full_spec((HALO, cc))],
        out_specs=[_row_spec(tr, 3 * cc), _full_spec((HALO, cc))],
        out_shape=[jax.ShapeDtypeStruct((t, 3 * cc), BF16), jax.ShapeDtypeStruct((HALO, cc), F32)],
        scratch_shapes=[pltpu.VMEM((tr + HALO, cc), F32), pltpu.VMEM((tr + HALO, cc), F32)],
        compiler_params=_params("arbitrary"))(dc, dc, pa, pa, pa, pa, dz, cw)


def _ple_fwd(name, h, gpre, pp):
    t, dm = h.shape
    tr = _tile(t, 256, 8)

    def body(h_ref, g_ref, p_ref, o_ref):
        o_ref[...] = h_ref[...] + _sigmoid(g_ref[...]) * p_ref[...]

    return pl.pallas_call(
        body, name=name, grid=(t // tr,), in_specs=[_row_spec(tr, dm)] * 3, out_specs=_row_spec(tr, dm),
        out_shape=jax.ShapeDtypeStruct((t, dm), F32), compiler_params=_params("parallel"))(h, gpre, pp)


def _ple_loss(name, h, gpre, pp, target):
    t, dm = h.shape
    tr = _tile(t, 256, 8)

    def body(h_ref, g_ref, p_ref, t_ref, dy_ref, sq_ref):
        i = pl.program_id(0)
        err = h_ref[...] + _sigmoid(g_ref[...]) * p_ref[...] - t_ref[...]
        dy_ref[...] = err * (1.0 / dm)

        @pl.when(i == 0)
        def _():
            sq_ref[...] = jnp.zeros_like(sq_ref)

        sq_ref[...] += jnp.sum(jnp.sum(err * err, axis=1, keepdims=True), axis=0, keepdims=True)

    return pl.pallas_call(
        body, name=name, grid=(t // tr,), in_specs=[_row_spec(tr, dm)] * 4,
        out_specs=[_row_spec(tr, dm), _full_spec((1, 1))],
        out_shape=[jax.ShapeDtypeStruct((t, dm), F32), jax.ShapeDtypeStruct((1, 1), F32)],
        compiler_params=_params("arbitrary"))(h, gpre, pp, target)


def _ple_bwd(name, dy, gpre, pp):
    t, dm = dy.shape
    tr = _tile(t, 256, 8)

    def body(dy_ref, g_ref, p_ref, dg_ref, dp_ref):
        sg = _sigmoid(g_ref[...])
        dyv = dy_ref[...]
        dg_ref[...] = (dyv * p_ref[...] * sg * (1.0 - sg)).astype(BF16)
        dp_ref[...] = (dyv * sg).astype(BF16)

    return pl.pallas_call(
        body, name=name, grid=(t // tr,), in_specs=[_row_spec(tr, dm)] * 3, out_specs=[_row_spec(tr, dm)] * 2,
        out_shape=[jax.ShapeDtypeStruct((t, dm), BF16)] * 2, compiler_params=_params("parallel"))(dy, gpre, pp)


def _rope_tables(seq):
    half = ROPE_DIM // 2
    inv = ROPE_THETA ** (-jnp.arange(half, dtype=F32) * (2.0 / ROPE_DIM))
    ang = jnp.arange(seq).astype(F32)[:, None] * inv[None, :]
    cos, sin = jnp.cos(ang), jnp.sin(ang)
    rest = HEAD_DIM - ROPE_DIM
    one = jnp.ones((seq, rest), F32)
    zero = jnp.zeros((seq, rest), F32)
    zh = jnp.zeros((seq, half), F32)
    tc = jnp.concatenate([cos, cos, one], axis=1)
    ta = jnp.concatenate([-sin, zh, zero], axis=1)
    tb = jnp.concatenate([zh, sin, zero], axis=1)
    return [jnp.tile(tb_, (1, 128 // HEAD_DIM)) for tb_ in (tc, ta, tb)]


def _wide(tab_ref, w):
    return jnp.tile(tab_ref[...], (1, w // 128))


def _hnr_fwd(name, src, width, gain, tables, seq):
    t = src.shape[0]
    tr = _tile(seq, 256, 8)
    per_seq = seq // tr

    def body(x_ref, g_ref, tc_ref, ta_ref, tb_ref, o_ref):
        xv = x_ref[...]
        rs = lax.rsqrt(_seg_allsum64(xv * xv) * (1.0 / HEAD_DIM) + EPS)
        y = xv * rs * g_ref[...]
        o_ref[...] = (y * _wide(tc_ref, width) + pltpu.roll(y, width - ROPE_DIM // 2, 1) * _wide(ta_ref, width)
                      + pltpu.roll(y, ROPE_DIM // 2, 1) * _wide(tb_ref, width))

    tab = pl.BlockSpec((tr, 128), lambda i: (i % per_seq, 0))
    return pl.pallas_call(
        body, name=name, grid=(t // tr,),
        in_specs=[_row_spec(tr, width), _full_spec((1, width)), tab, tab, tab],
        out_specs=_row_spec(tr, width), out_shape=jax.ShapeDtypeStruct((t, width), F32),
        compiler_params=_params("parallel"))(src, gain, *tables)


def _hnr_bwd_math(xv, gain, dout, tc, ta, tb, width):
    dy = dout * tc + pltpu.roll(dout * ta, ROPE_DIM // 2, 1) + pltpu.roll(dout * tb, width - ROPE_DIM // 2, 1)
    rs = lax.rsqrt(_seg_allsum64(xv * xv) * (1.0 / HEAD_DIM) + EPS)
    xn = xv * rs
    dyh = dy * gain
    dx = rs * (dyh - xn * (_seg_allsum64(dyh * xn) * (1.0 / HEAD_DIM)))
    return dx, _colsum(dy * xn)


def _q_bwd(name, p1, gain, tables, dqs, dgt, seq):
    t, w4 = p1.shape
    da = w4 // 4
    width = 3 * da
    tr = _tile(seq, 128, 8)
    per_seq = seq // tr

    def body(x_ref, g_ref, tc_ref, ta_ref, tb_ref, d0_ref, d1_ref, d2_ref, dgt_ref, o_ref, dg_ref):
        i = pl.program_id(0)
        dout = jnp.concatenate([d0_ref[...], d1_ref[...], d2_ref[...]], axis=1)
        dx, dg = _hnr_bwd_math(x_ref[...], g_ref[...], dout, _wide(tc_ref, width), _wide(ta_ref, width),
                               _wide(tb_ref, width), width)

        @pl.when(i == 0)
        def _():
            dg_ref[...] = jnp.zeros_like(dg_ref)

        dg_ref[...] += dg
        o_ref[:, pl.ds(0, width)] = dx.astype(BF16)
        o_ref[:, pl.ds(width, da)] = dgt_ref[...]

    tab = pl.BlockSpec((tr, 128), lambda i: (i % per_seq, 0))
    return pl.pallas_call(
        body, name=name, grid=(t // tr,),
        in_specs=[_row_spec(tr, width), _full_spec((1, width)), tab, tab, tab] + [_row_spec(tr, da)] * 4,
        out_specs=[_row_spec(tr, w4), _full_spec((1, width))],
        out_shape=[jax.ShapeDtypeStruct((t, w4), BF16), jax.ShapeDtypeStruct((1, width), F32)],
        compiler_params=_params("arbitrary"))(p1, gain, *tables, *dqs, dgt)


def _k_bwd(name, kv, gain, tables, dks, dvs, seq):
    t, w2 = kv.shape
    da = w2 // 2
    tr = _tile(seq, 256, 8)
    per_seq = seq // tr

    def body(x_ref, g_ref, tc_ref, ta_ref, tb_ref, k0, k1, k2, v0, v1, v2, o_ref, dg_ref):
        i = pl.program_id(0)
        dout = k0[...] + k1[...] + k2[...]
        dx, dg = _hnr_bwd_math(x_ref[...], g_ref[...], dout, _wide(tc_ref, da), _wide(ta_ref, da), _wide(tb_ref, da), da)

        @pl.when(i == 0)
        def _():
            dg_ref[...] = jnp.zeros_like(dg_ref)

        dg_ref[...] += dg
        o_ref[:, pl.ds(0, da)] = dx.astype(BF16)
        o_ref[:, pl.ds(da, da)] = (v0[...] + v1[...] + v2[...]).astype(BF16)

    tab = pl.BlockSpec((tr, 128), lambda i: (i % per_seq, 0))
    return pl.pallas_call(
        body, name=name, grid=(t // tr,),
        in_specs=[_row_spec(tr, da), _full_spec((1, da)), tab, tab, tab] + [_row_spec(tr, da)] * 6,
        out_specs=[_row_spec(tr, w2), _full_spec((1, da))],
        out_shape=[jax.ShapeDtypeStruct((t, w2), BF16), jax.ShapeDtypeStruct((1, da), F32)],
        compiler_params=_params("arbitrary"))(kv, gain, *tables, *dks, *dvs)


def _unit_rows(ref, dil, r, blk):
    start = r + dil * SPAN * blk
    if dil == 1:
        return ref[pl.ds(start, SPAN), :]
    return ref[pl.ds(start, SPAN, stride=dil), :]


def _store_rows(ref, dil, r, blk, val):
    start = r + dil * SPAN * blk
    if dil == 1:
        ref[pl.ds(start, SPAN), :] = val
    else:
        ref[pl.ds(start, SPAN, stride=dil), :] = val


def _band_mask(with_prev):
    nk = 2 * SPAN if with_prev else SPAN
    qi = lax.broadcasted_iota(jnp.int32, (SPAN, nk), 0)
    kj = lax.broadcasted_iota(jnp.int32, (SPAN, nk), 1)
    if with_prev:
        return (kj >= qi) & (kj <= qi + SPAN)
    return kj <= qi


_NT = (((1,), (1,)), ((), ()))
_TN = (((0,), (0,)), ((), ()))


def _attn_fwd(name, qn, kn, kv, group, seq):
    t, da = kn.shape
    dil = DILATIONS[group]
    nblk = seq // (dil * SPAN)
    hp = da // 128
    scale = HEAD_DIM ** -0.5

    def body(q_ref, k_ref, v_ref, o_ref, l_ref):
        for r in range(dil):
            kc = vc = None
            for blk in range(nblk):
                kp, vp = kc, vc
                q = _unit_rows(q_ref, dil, r, blk) * scale
                kc = _unit_rows(k_ref, dil, r, blk)
                vc = _unit_rows(v_ref, dil, r, blk)
                with_prev = blk > 0
                kcat = jnp.concatenate([kp, kc], axis=0) if with_prev else kc
                vcat = jnp.concatenate([vp, vc], axis=0) if with_prev else vc
                mask = _band_mask(with_prev)
                outs, lses = [], []
                for hh in range(128 // HEAD_DIM):
                    sl = slice(hh * HEAD_DIM, (hh + 1) * HEAD_DIM)
                    s = lax.dot_general(q[:, sl].astype(BF16), kcat[:, sl].astype(BF16), _NT, preferred_element_type=F32)
                    s = jnp.where(mask, s, NEG_INF)
                    mx = jnp.max(s, axis=-1, keepdims=True)
                    p = jnp.exp(s - mx)
                    den = jnp.sum(p, axis=-1, keepdims=True)
                    o = jnp.dot(p.astype(BF16), vcat[:, sl].astype(BF16), preferred_element_type=F32) / den
                    outs.append(o)
                    lses.append(jnp.broadcast_to(mx + jnp.log(den), (SPAN, HEAD_DIM)))
                _store_rows(o_ref, dil, r, blk, jnp.concatenate(outs, axis=1))
                _store_rows(l_ref, dil, r, blk, jnp.concatenate(lses, axis=1))

    blk_spec = lambda off: pl.BlockSpec((seq, 128), lambda b, h: (b, off + h))
    return pl.pallas_call(
        body, name=name, grid=(t // seq, hp),
        in_specs=[blk_spec(group * hp), blk_spec(0), blk_spec(hp)],
        out_specs=[blk_spec(0), blk_spec(0)],
        out_shape=[jax.ShapeDtypeStruct((t, da), F32)] * 2,
        compiler_params=_params("parallel", "parallel"))(qn, kn, kv)


def _attn_bwd(name, qn, kn, kv, do, lse, dsum, group, seq):
    t, da = kn.shape
    dil = DILATIONS[group]
    nblk = seq // (dil * SPAN)
    hp = da // 128
    scale = HEAD_DIM ** -0.5
    nh = 128 // HEAD_DIM

    def body(q_ref, k_ref, v_ref, do_ref, l_ref, d_ref, dq_ref, dk_ref, dv_ref):
        for r in range(dil):
            kc = vc = None
            pend_k = pend_v = None
            for blk in range(nblk):
                kp, vp = kc, vc
                q = _unit_rows(q_ref, dil, r, blk) * scale
                kc = _unit_rows(k_ref, dil, r, blk)
                vc = _unit_rows(v_ref, dil, r, blk)
                dov = _unit_rows(do_ref, dil, r, blk)
                lrow = _unit_rows(l_ref, dil, r, blk)
                drow = _unit_rows(d_ref, dil, r, blk)
                with_prev = blk > 0
                kcat = jnp.concatenate([kp, kc], axis=0) if with_prev else kc
                vcat = jnp.concatenate([vp, vc], axis=0) if with_prev else vc
                mask = _band_mask(with_prev)
                dqs, dkcs, dvcs = [], [], []
                for hh in range(nh):
                    sl = slice(hh * HEAD_DIM, (hh + 1) * HEAD_DIM)
                    qh = q[:, sl].astype(BF16)
                    doh = dov[:, sl].astype(BF16)
                    kh = kcat[:, sl].astype(BF16)
                    s = lax.dot_general(qh, kh, _NT, preferred_element_type=F32)
                    p = jnp.where(mask, jnp.exp(s - lrow[:, hh * HEAD_DIM:hh * HEAD_DIM + 1]), 0.0)
                    dp = lax.dot_general(doh, vcat[:, sl].astype(BF16), _NT, preferred_element_type=F32)
                    ds = (p * (dp - drow[:, hh * HEAD_DIM:hh * HEAD_DIM + 1])).astype(BF16)
                    dqs.append(jnp.dot(ds, kh, preferred_element_type=F32) * scale)
                    dkcs.append(lax.dot_general(ds, qh, _TN, preferred_element_type=F32))
                    dvcs.append(lax.dot_general(p.astype(BF16), doh, _TN, preferred_element_type=F32))
                _store_rows(dq_ref, dil, r, blk, jnp.concatenate(dqs, axis=1))
                dkcat = jnp.concatenate(dkcs, axis=1)
                dvcat = jnp.concatenate(dvcs, axis=1)
                if with_prev:
                    _store_rows(dk_ref, dil, r, blk - 1, pend_k + dkcat[:SPAN])
                    _store_rows(dv_ref, dil, r, blk - 1, pend_v + dvcat[:SPAN])
                    pend_k, pend_v = dkcat[SPAN:], dvcat[SPAN:]
                else:
                    pend_k, pend_v = dkcat, dvcat
            _store_rows(dk_ref, dil, r, nblk - 1, pend_k)
            _store_rows(dv_ref, dil, r, nblk - 1, pend_v)

    blk_spec = lambda off: pl.BlockSpec((seq, 128), lambda b, h: (b, off + h))
    return pl.pallas_call(
        body, name=name, grid=(t // seq, hp),
        in_specs=[blk_spec(group * hp), blk_spec(0), blk_spec(hp), blk_spec(0), blk_spec(0), blk_spec(0)],
        out_specs=[blk_spec(0)] * 3,
        out_shape=[jax.ShapeDtypeStruct((t, da), F32)] * 3,
        compiler_params=_params("parallel", "parallel"))(qn, kn, kv, do, lse, dsum)


def _attn_merge(name, os_, ls_, p1):
    t, da = os_[0].shape
    tr = _tile(t, 256, 8)

    def body(o0, o1, o2, l0, l1, l2, g_ref, o_ref, l_ref, m_ref):
        a0, a1, a2 = l0[...], l1[...], l2[...]
        mx = jnp.maximum(jnp.maximum(a0, a1), a2)
        e0, e1, e2 = jnp.exp(a0 - mx), jnp.exp(a1 - mx), jnp.exp(a2 - mx)
        den = e0 + e1 + e2
        o = (e0 * o0[...] + e1 * o1[...] + e2 * o2[...]) / den
        o_ref[...] = o
        l_ref[...] = mx + jnp.log(den)
        g = g_ref[...]
        m_ref[...] = (o * g * _sigmoid(g)).astype(BF16)

    return pl.pallas_call(
        body, name=name, grid=(t // tr,),
        in_specs=[_row_spec(tr, da)] * 6 + [_row_spec(tr, da, 3)],
        out_specs=[_row_spec(tr, da)] * 3,
        out_shape=[jax.ShapeDtypeStruct((t, da), F32)] * 2 + [jax.ShapeDtypeStruct((t, da), BF16)],
        compiler_params=_params("parallel"))(*os_, *ls_, p1)


def _gate_bwd(name, dm, o, p1):
    t, da = o.shape
    tr = _tile(t, 256, 8)

    def body(dm_ref, o_ref, g_ref, do_ref, dg_ref, ds_ref):
        g = g_ref[...]
        sg = _sigmoid(g)
        dmv, ov = dm_ref[...], o_ref[...]
        do = dmv * (g * sg)
        do_ref[...] = do
        dg_ref[...] = (dmv * ov * (sg * (1.0 + g * (1.0 - sg)))).astype(BF16)
        ds_ref[...] = _seg_allsum64(do * ov)

    return pl.pallas_call(
        body, name=name, grid=(t // tr,),
        in_specs=[_row_spec(tr, da), _row_spec(tr, da), _row_spec(tr, da, 3)],
        out_specs=[_row_spec(tr, da)] * 3,
        out_shape=[jax.ShapeDtypeStruct((t, da), F32), jax.ShapeDtypeStruct((t, da), BF16), jax.ShapeDtypeStruct((t, da), F32)],
        compiler_params=_params("parallel"))(dm, o, p1)


def _cast_bf16(name, w2d):
    r, c = w2d.shape
    tr = _tile(r, 256, 8)

    def body(x_ref, o_ref):
        o_ref[...] = x_ref[...].astype(BF16)

    return pl.pallas_call(
        body, name=name, grid=(r // tr,), in_specs=[_row_spec(tr, c)], out_specs=_row_spec(tr, c),
        out_shape=jax.ShapeDtypeStruct((r, c), BF16), compiler_params=_params("parallel"))(w2d)


def _adamw(name, w, g, m, v):
    r, c = w.shape
    tr = _tile(r, 256, 8)
    c1 = 1.0 - ADAM_B1 ** ADAM_STEP
    c2 = 1.0 - ADAM_B2 ** ADAM_STEP

    def body(w_ref, g_ref, m_ref, v_ref, d_ref, nm_ref, nv_ref):
        gv = g_ref[...]
        nm = ADAM_B1 * m_ref[...] + (1.0 - ADAM_B1) * gv
        nv = ADAM_B2 * v_ref[...] + (1.0 - ADAM_B2) * (gv * gv)
        nm_ref[...] = nm
        nv_ref[...] = nv
        d_ref[...] = -ADAM_LR * ((nm / c1) / (jnp.sqrt(nv / c2) + ADAM_EPS) + ADAM_WD * w_ref[...])

    sds = jax.ShapeDtypeStruct((r, c), F32)
    return pl.pallas_call(
        body, name=name, grid=(r // tr,), in_specs=[_row_spec(tr, c)] * 4, out_specs=[_row_spec(tr, c)] * 3,
        out_shape=[sds] * 3, compiler_params=_params("parallel"))(w, g, m, v)


def _pair_sum(name, gd, recv, core):
    _, r, c = gd.shape
    rh = r // 2
    tr = _tile(rh, 256, 8)
    nrt = rh // tr

    def body(c_ref, a_ref, b_ref, o_ref):
        o_ref[...] = (a_ref[...].astype(F32) + b_ref[...].astype(F32)).astype(BF16)

    grid_spec = pltpu.PrefetchScalarGridSpec(
        num_scalar_prefetch=1, grid=(N_CHIPS, nrt),
        in_specs=[pl.BlockSpec((None, tr, c), lambda j, i, cr: (j, cr[0] * nrt + i, 0)),
                  pl.BlockSpec((None, tr, c), lambda j, i, cr: (j, i, 0))],
        out_specs=pl.BlockSpec((None, tr, c), lambda j, i, cr: (j, i, 0)))
    return pl.pallas_call(
        body, name=name, grid_spec=grid_spec, out_shape=jax.ShapeDtypeStruct((N_CHIPS, rh, c), BF16),
        compiler_params=_params("parallel", "parallel"))(core, gd, recv)


def _chip_sum(name, parts):
    _, rh, c = parts.shape
    tr = _tile(rh, 256, 8)

    def body(p_ref, o_ref):
        acc = p_ref[0].astype(F32)
        for j in range(1, N_CHIPS):
            acc = acc + p_ref[j].astype(F32)
        o_ref[...] = acc

    return pl.pallas_call(
        body, name=name, grid=(rh // tr,),
        in_specs=[pl.BlockSpec((N_CHIPS, tr, c), lambda i: (0, i, 0))], out_specs=_row_spec(tr, c),
        out_shape=jax.ShapeDtypeStruct((rh, c), F32), compiler_params=_params("parallel"))(parts)


HBM = pl.BlockSpec(memory_space=pl.ANY)


def _place():
    x, y, c = lax.axis_index("x"), lax.axis_index("y"), lax.axis_index("c")
    chips = [(1 - x, y), (x, 1 - y), (1 - x, 1 - y)]
    return x, y, c, chips


def _half(ref, hc):
    rows = ref.shape[0] // 2
    return ref.at[pl.ds(hc * rows, rows)]


def _gather_weights(shards):
    n = len(shards)

    def body(*refs):
        ins, outs = refs[:n], refs[n:2 * n]
        send_sems, recv_sems, loc_sems = refs[2 * n:]
        x, y, c, chips = _place()
        mine = 2 * x + y
        local, first, passed = [], [], []
        for i in range(n):
            cp = pltpu.make_async_copy(ins[i], outs[i].at[mine], loc_sems.at[i])
            cp.start()
            local.append(cp)
        for i in range(n):
            for k, chip in enumerate(chips):
                cp = pltpu.make_async_remote_copy(
                    src_ref=_half(ins[i], c), dst_ref=_half(outs[i].at[mine], c),
                    send_sem=send_sems.at[6 * i + k], recv_sem=recv_sems.at[6 * i + k],
                    device_id=(chip[0], chip[1], c), device_id_type=MESH)
                cp.start()
                first.append(cp)
        for i in range(n):
            for k, chip in enumerate(chips):
                theirs = 2 * chip[0] + chip[1]
                landed = _half(outs[i].at[theirs], c)
                pltpu.make_async_remote_copy(
                    src_ref=landed, dst_ref=landed, send_sem=send_sems.at[6 * i + k], recv_sem=recv_sems.at[6 * i + k],
                    device_id=(chip[0], chip[1], c), device_id_type=MESH).wait_recv()
                cp = pltpu.make_async_remote_copy(
                    src_ref=landed, dst_ref=landed, send_sem=send_sems.at[6 * i + 3 + k], recv_sem=recv_sems.at[6 * i + 3 + k],
                    device_id=(x, y, 1 - c), device_id_type=MESH)
                cp.start()
                passed.append(cp)
        for i in range(n):
            for k, chip in enumerate(chips):
                theirs = 2 * chip[0] + chip[1]
                other = _half(outs[i].at[theirs], 1 - c)
                pltpu.make_async_remote_copy(
                    src_ref=other, dst_ref=other, send_sem=send_sems.at[6 * i + 3 + k], recv_sem=recv_sems.at[6 * i + 3 + k],
                    device_id=(x, y, 1 - c), device_id_type=MESH).wait_recv()
        for cp in first + passed:
            cp.wait_send()
        for cp in local:
            cp.wait()

    return pl.pallas_call(
        body, name="gather_weights", in_specs=[HBM] * n, out_specs=[HBM] * n,
        out_shape=[jax.ShapeDtypeStruct((N_CHIPS,) + s.shape, s.dtype) for s in shards],
        scratch_shapes=[pltpu.SemaphoreType.DMA((6 * n,)), pltpu.SemaphoreType.DMA((6 * n,)), pltpu.SemaphoreType.DMA((n,))],
        compiler_params=pltpu.CompilerParams(has_side_effects=True))(*shards)


def _pair_exchange(grads):
    n = len(grads)

    def body(*refs):
        ins, outs = refs[:n], refs[n:2 * n]
        send_sems, recv_sems = refs[2 * n:]
        x, y, c, _ = _place()
        cps = []
        for i in range(n):
            rows = ins[i].shape[1] // 2
            cp = pltpu.make_async_remote_copy(
                src_ref=ins[i].at[:, pl.ds((1 - c) * rows, rows), :], dst_ref=outs[i],
                send_sem=send_sems.at[i], recv_sem=recv_sems.at[i], device_id=(x, y, 1 - c), device_id_type=MESH)
            cp.start()
            cps.append(cp)
        for cp in cps:
            cp.wait()

    return pl.pallas_call(
        body, name="pair_exchange", in_specs=[HBM] * n, out_specs=[HBM] * n,
        out_shape=[jax.ShapeDtypeStruct((N_CHIPS, g.shape[1] // 2, g.shape[2]), g.dtype) for g in grads],
        scratch_shapes=[pltpu.SemaphoreType.DMA((n,)), pltpu.SemaphoreType.DMA((n,))],
        compiler_params=pltpu.CompilerParams(has_side_effects=True))(*grads)


def _chip_scatter(sums):
    n = len(sums)

    def body(*refs):
        ins, outs = refs[:n], refs[n:2 * n]
        send_sems, recv_sems, loc_sems = refs[2 * n:]
        x, y, c, chips = _place()
        mine = 2 * x + y
        cps, local = [], []
        for i in range(n):
            cp = pltpu.make_async_copy(ins[i].at[mine], outs[i].at[mine], loc_sems.at[i])
            cp.start()
            local.append(cp)
            for k, chip in enumerate(chips):
                theirs = 2 * chip[0] + chip[1]
                cp = pltpu.make_async_remote_copy(
                    src_ref=ins[i].at[theirs], dst_ref=outs[i].at[mine],
                    send_sem=send_sems.at[3 * i + k], recv_sem=recv_sems.at[3 * i + k],
                    device_id=(chip[0], chip[1], c), device_id_type=MESH)
                cp.start()
                cps.append((cp, i, k, theirs))
        for cp, i, k, theirs in cps:
            cp.wait_send()
            pltpu.make_async_remote_copy(
                src_ref=ins[i].at[theirs], dst_ref=outs[i].at[theirs],
                send_sem=send_sems.at[3 * i + k], recv_sem=recv_sems.at[3 * i + k],
                device_id=(x, y, c), device_id_type=MESH).wait_recv()
        for cp in local:
            cp.wait()

    return pl.pallas_call(
        body, name="chip_scatter", in_specs=[HBM] * n, out_specs=[HBM] * n,
        out_shape=[jax.ShapeDtypeStruct(s.shape, s.dtype) for s in sums],
        scratch_shapes=[pltpu.SemaphoreType.DMA((3 * n,)), pltpu.SemaphoreType.DMA((3 * n,)), pltpu.SemaphoreType.DMA((n,))],
        compiler_params=pltpu.CompilerParams(has_side_effects=True))(*sums)


def _sibling_join(halves):
    n = len(halves)

    def body(*refs):
        ins, outs = refs[:n], refs[n:2 * n]
        send_sems, recv_sems, loc_sems = refs[2 * n:]
        x, y, c, _ = _place()
        cps, local = [], []
        for i in range(n):
            cp = pltpu.make_async_copy(ins[i], _half(outs[i], c), loc_sems.at[i])
            cp.start()
            local.append(cp)
            cp = pltpu.make_async_remote_copy(
                src_ref=ins[i], dst_ref=_half(outs[i], c), send_sem=send_sems.at[i], recv_sem=recv_sems.at[i],
                device_id=(x, y, 1 - c), device_id_type=MESH)
            cp.start()
            cps.append(cp)
        for i, cp in enumerate(cps):
            cp.wait_send()
            pltpu.make_async_remote_copy(
                src_ref=ins[i], dst_ref=_half(outs[i], 1 - c), send_sem=send_sems.at[i], recv_sem=recv_sems.at[i],
                device_id=(x, y, 1 - c), device_id_type=MESH).wait_recv()
        for cp in local:
            cp.wait()

    return pl.pallas_call(
        body, name="sibling_join", in_specs=[HBM] * n, out_specs=[HBM] * n,
        out_shape=[jax.ShapeDtypeStruct((2 * h.shape[0], h.shape[1]), h.dtype) for h in halves],
        scratch_shapes=[pltpu.SemaphoreType.DMA((n,)), pltpu.SemaphoreType.DMA((n,)), pltpu.SemaphoreType.DMA((n,))],
        compiler_params=pltpu.CompilerParams(has_side_effects=True))(*halves)


def _gather8(name, block, reduce):
    m, n = block.shape

    def body(x_ref, out_ref, *scratch):
        if reduce:
            all_ref, send_sems, recv_sems, local_sem = scratch
        else:
            all_ref = out_ref
            send_sems, recv_sems, local_sem = scratch
        x, y, c, chips = _place()
        me, sibling = (x, y, c), (x, y, 1 - c)

        def rows(px, py, pc):
            return all_ref.at[pl.ds((4 * px + 2 * py + pc) * m, m), :]

        def copy(k, blk, to, src=None):
            return pltpu.make_async_remote_copy(
                src_ref=rows(*blk) if src is None else src, dst_ref=rows(*blk),
                send_sem=send_sems.at[k], recv_sem=recv_sems.at[k], device_id=to, device_id_type=MESH)

        mine = pltpu.make_async_copy(x_ref, rows(*me), local_sem)
        mine.start()
        first = [copy(0, me, sibling, src=x_ref)]
        first += [copy(1 + j, me, (chip[0], chip[1], c), src=x_ref) for j, chip in enumerate(chips)]
        for cp in first:
            cp.start()
        passed = [copy(4 + j, (chip[0], chip[1], c), sibling) for j, chip in enumerate(chips)]
        for j, chip in enumerate(chips):
            copy(1 + j, (chip[0], chip[1], c), me).wait_recv()
            passed[j].start()
        copy(0, sibling, me).wait_recv()
        for j, chip in enumerate(chips):
            copy(4 + j, (chip[0], chip[1], 1 - c), me).wait_recv()
        for cp in first + passed:
            cp.wait_send()
        mine.wait()
        if reduce:
            acc = all_ref[pl.ds(0, m), :]
            for d in range(1, 8):
                acc = acc + all_ref[pl.ds(d * m, m), :]
            out_ref[...] = acc

    sems = [pltpu.SemaphoreType.DMA((7,)), pltpu.SemaphoreType.DMA((7,)), pltpu.SemaphoreType.DMA]
    scratch = ([pltpu.VMEM((8 * m, n), F32)] if reduce else []) + sems
    return pl.pallas_call(
        body, name=name,
        out_shape=jax.ShapeDtypeStruct((m, n) if reduce else (8 * m, n), F32),
        in_specs=[pl.BlockSpec(memory_space=pltpu.VMEM)], out_specs=pl.BlockSpec(memory_space=pltpu.VMEM),
        scratch_shapes=scratch, compiler_params=pltpu.CompilerParams(has_side_effects=True))(block)


def _pad_rows(a, rows):
    return jnp.concatenate([a, jnp.zeros((rows - a.shape[0], a.shape[1]), a.dtype)], axis=0)


def kernel(x, p, norm_g, w_in_a, conv_w, conv_b, ln_g, ln_b, w_out_a, kv_norm_g, w_kv, k_norm_g, w_in_b, q_norm_g, w_out_b, ple_norm_g, w_ple_gate, w_ple_proj, loss_target, m_norm_g, m_w_in_a, m_conv_w, m_conv_b, m_ln_g, m_ln_b, m_w_out_a, m_kv_norm_g, m_w_kv, m_k_norm_g, m_w_in_b, m_q_norm_g, m_w_out_b, m_ple_norm_g, m_w_ple_gate, m_w_ple_proj, v_norm_g, v_w_in_a, v_conv_w, v_conv_b, v_ln_g, v_ln_b, v_w_out_a, v_kv_norm_g, v_w_kv, v_k_norm_g, v_w_in_b, v_q_norm_g, v_w_out_b, v_ple_norm_g, v_w_ple_gate, v_w_ple_proj):
    nb, seq, dm = x.shape
    t = nb * seq
    ple = p.shape[-1]
    ccs = conv_w.shape[-1]
    cc = N_CHIPS * ccs
    da = dm
    nheads = da // HEAD_DIM
    assert seq == DILATIONS[-1] * SPAN and da % 128 == 0 and ccs % 128 == 0

    core = lax.axis_index("c").astype(jnp.int32).reshape(1)
    chip = 2 * lax.axis_index("x") + lax.axis_index("y")

    x2 = x.reshape(t, dm)
    tgt2 = loss_target.reshape(t, dm)
    p0 = p[0].reshape(t, ple)
    p1 = p[1].reshape(t, ple)

    big = [
        ("w_in_a", w_in_a[0], "col"), ("w_out_a", w_out_a[0], "row"), ("w_kv", w_kv, "col"),
        ("w_in_b", w_in_b[0], "col"), ("w_out_b", w_out_b[0], "row"),
        ("w_ple_gate0", w_ple_gate[0], "row"), ("w_ple_gate1", w_ple_gate[1], "row"),
        ("w_ple_proj0", w_ple_proj[0], "col"), ("w_ple_proj1", w_ple_proj[1], "col"),
    ]
    kinds = {nm: kd for nm, _, kd in big}
    shard_shape = {nm: w.shape for nm, w, _ in big}
    gathered = _gather_weights([_cast_bf16("cast_" + nm, w) for nm, w, _ in big])
    W = {nm: g for (nm, _, _), g in zip(big, gathered)}

    vec_rows = 40
    small = _pad_rows(jnp.concatenate([conv_w[0], conv_b, ln_g, ln_b], axis=0), vec_rows)
    allv = _gather8("gather_conv_vectors", small, reduce=False).reshape(N_CHIPS, 2, vec_rows, ccs)[:, 0]
    allv = allv.transpose(1, 0, 2).reshape(vec_rows, cc)
    cw_full, cb_full, lg_full, lb_full = allv[:HALO], allv[31:32], allv[32:33], allv[33:34]
    cw_full = cw_full * (lax.broadcasted_iota(jnp.int32, (HALO, 1), 0) < CONV_WIDTH).astype(F32)

    tables = _rope_tables(seq)
    gain_q = jnp.tile(q_norm_g[0][:, None, :], (1, nheads, 1)).reshape(1, 3 * da)
    gain_k = jnp.tile(k_norm_g[None, :], (1, nheads))
    g0, g1 = norm_g[0:1], norm_g[1:2]
    pg0, pg1 = ple_norm_g[0:1], ple_norm_g[1:2]
    kvg = kv_norm_g[None, :]

    (u0,) = _rms_fwd("rms_u0", x2, [g0])
    pa = _mm_nn("mm_in_a", u0, W["w_in_a"], "col")
    conv_out, m_a = _mixa_fwd("mixa_fwd", pa, cw_full, cb_full, lg_full, lb_full, seq)
    h0 = _mm_nn("mm_out_a", m_a, W["w_out_a"], "row", resid=x2)
    (r0,) = _rms_fwd("rms_r0", h0, [pg0])
    gpre0 = _mm_nn("mm_gate0", r0, W["w_ple_gate0"], "row")
    pp0 = _mm_nn("mm_proj0", p0, W["w_ple_proj0"], "col")
    x1 = _ple_fwd("ple_fwd0", h0, gpre0, pp0)
    kvn, u1 = _rms_fwd("rms_kv_u1", x1, [kvg, g1])
    kv = _mm_nn("mm_kv", kvn, W["w_kv"], "col")
    kn = _hnr_fwd("k_norm_rope", kv, da, gain_k, tables, seq)
    pb = _mm_nn("mm_in_b", u1, W["w_in_b"], "col")
    qn = _hnr_fwd("q_norm_rope", pb, 3 * da, gain_q, tables, seq)
    og, lg_ = [], []
    for g in range(3):
        o_g, l_g = _attn_fwd("attn_fwd%d" % g, qn, kn, kv, g, seq)
        og.append(o_g)
        lg_.append(l_g)
    o, lse, m_b = _attn_merge("attn_merge", og, lg_, pb)
    h1 = _mm_nn("mm_out_b", m_b, W["w_out_b"], "row", resid=x1)
    (r1,) = _rms_fwd("rms_r1", h1, [pg1])
    gpre1 = _mm_nn("mm_gate1", r1, W["w_ple_gate1"], "row")
    pp1 = _mm_nn("mm_proj1", p1, W["w_ple_proj1"], "col")
    dy, sq = _ple_loss("ple_loss", h1, gpre1, pp1, tgt2)
    loss = lax.psum(0.5 * sq[0, 0] / dm, ("x", "y", "c"))

    G = {}
    dgp1, dpp1 = _ple_bwd("ple_bwd1", dy, gpre1, pp1)
    G["w_ple_gate1"] = _mm_tn("tn_gate1", r1, dgp1, "row", shard_shape["w_ple_gate1"])
    G["w_ple_proj1"] = _mm_tn("tn_proj1", p1, dpp1, "col", shard_shape["w_ple_proj1"])
    dr1 = _mm_nt("nt_gate1", dgp1, W["w_ple_gate1"], "row")
    dh1, (dpg1,) = _rms_bwd("rms_bwd_r1", h1, dy, [(pg1, dr1)])
    G["w_out_b"] = _mm_tn("tn_out_b", m_b, dh1, "row", shard_shape["w_out_b"])
    dm_b = _mm_nt("nt_out_b", dh1, W["w_out_b"], "row")
    d_o, dgt, dsum = _gate_bwd("gate_bwd", dm_b, o, pb)
    dqs, dks, dvs = [], [], []
    for g in range(3):
        dq_g, dk_g, dv_g = _attn_bwd("attn_bwd%d" % g, qn, kn, kv, d_o, lse, dsum, g, seq)
        dqs.append(dq_g)
        dks.append(dk_g)
        dvs.append(dv_g)
    dpb, dgq = _q_bwd("q_bwd", pb, gain_q, tables, dqs, dgt, seq)
    dkv, dgk = _k_bwd("k_bwd", kv, gain_k, tables, dks, dvs, seq)
    G["w_in_b"] = _mm_tn("tn_in_b", u1, dpb, "col", shard_shape["w_in_b"])
    du1 = _mm_nt("nt_in_b", dpb, W["w_in_b"], "col")
    G["w_kv"] = _mm_tn("tn_kv", kvn, dkv, "col", shard_shape["w_kv"])
    dkvn = _mm_nt("nt_kv", dkv, W["w_kv"], "col")
    dx1, (dg1, dkvg) = _rms_bwd("rms_bwd_x1", x1, dh1, [(g1, du1), (kvg, dkvn)])
    dgp0, dpp0 = _ple_bwd("ple_bwd0", dx1, gpre0, pp0)
    G["w_ple_gate0"] = _mm_tn("tn_gate0", r0, dgp0, "row", shard_shape["w_ple_gate0"])
    G["w_ple_proj0"] = _mm_tn("tn_proj0", p0, dpp0, "col", shard_shape["w_ple_proj0"])
    dr0 = _mm_nt("nt_gate0", dgp0, W["w_ple_gate0"], "row")
    dh0, (dpg0,) = _rms_bwd("rms_bwd_r0", h0, dx1, [(pg0, dr0)])
    G["w_out_a"] = _mm_tn("tn_out_a", m_a, dh0, "row", shard_shape["w_out_a"])
    dm_a = _mm_nt("nt_out_a", dh0, W["w_out_a"], "row")
    dc, dz, dlg, dlb, dcb = _mixa_bwd1("mixa_bwd1", conv_out, pa, dm_a, lg_full, lb_full)
    dpa, dcw = _mixa_bwd2("mixa_bwd2", dc, pa, dz, cw_full, seq)
    G["w_in_a"] = _mm_tn("tn_in_a", u0, dpa, "col", shard_shape["w_in_a"])
    du0 = _mm_nt("nt_in_a", dpa, W["w_in_a"], "col")
    dx, (dg0,) = _rms_bwd("rms_bwd_x", x2, dh0, [(g0, du0)])
    grad_x = dx.reshape(nb, seq, dm)

    names = [nm for nm, _, _ in big]
    recv = _pair_exchange([G[nm] for nm in names])
    sums = [_pair_sum("pair_sum_" + nm, G[nm], rc, core) for nm, rc in zip(names, recv)]
    parts = _chip_scatter(sums)
    halves = [_chip_sum("chip_sum_" + nm, pt) for nm, pt in zip(names, parts)]
    gfull = dict(zip(names, _sibling_join(halves)))

    def as_rows(a):
        return a.reshape(-1, dm)

    small_parts = [as_rows(dcw), as_rows(dcb), as_rows(dlg), as_rows(dlb), dg0, dg1, dkvg, dpg0, dpg1, as_rows(dgk), as_rows(dgq)]
    counts = [a.shape[0] for a in small_parts]
    total = sum(counts)
    packed = _pad_rows(jnp.concatenate(small_parts, axis=0), -(-total // 8) * 8)
    red = _gather8("reduce_small", packed, reduce=True)
    pieces, off = [], 0
    for n_ in counts:
        pieces.append(red[off:off + n_])
        off += n_
    r_dcw, r_dcb, r_dlg, r_dlb, r_g0, r_g1, r_kvg, r_pg0, r_pg1, r_gk, r_gq = pieces
    my_cols = lambda a: lax.dynamic_slice_in_dim(a.reshape(-1, cc), chip * ccs, ccs, axis=1)
    small_grads = {
        "norm_g": jnp.concatenate([r_g0, r_g1], axis=0),
        "conv_w": my_cols(r_dcw)[:CONV_WIDTH],
        "conv_b": my_cols(r_dcb), "ln_g": my_cols(r_dlg), "ln_b": my_cols(r_dlb),
        "kv_norm_g": r_kvg,
        "k_norm_g": r_gk.reshape(nheads, HEAD_DIM).sum(axis=0, keepdims=True),
        "q_norm_g": r_gq.reshape(3, nheads, HEAD_DIM).sum(axis=1),
        "ple_norm_g": jnp.concatenate([r_pg0, r_pg1], axis=0),
    }

    given = dict(norm_g=norm_g, w_in_a=w_in_a, conv_w=conv_w, conv_b=conv_b, ln_g=ln_g, ln_b=ln_b, w_out_a=w_out_a,
                 kv_norm_g=kv_norm_g, w_kv=w_kv, k_norm_g=k_norm_g, w_in_b=w_in_b, q_norm_g=q_norm_g, w_out_b=w_out_b,
                 ple_norm_g=ple_norm_g, w_ple_gate=w_ple_gate, w_ple_proj=w_ple_proj)
    mom1 = dict(norm_g=m_norm_g, w_in_a=m_w_in_a, conv_w=m_conv_w, conv_b=m_conv_b, ln_g=m_ln_g, ln_b=m_ln_b,
                w_out_a=m_w_out_a, kv_norm_g=m_kv_norm_g, w_kv=m_w_kv, k_norm_g=m_k_norm_g, w_in_b=m_w_in_b,
                q_norm_g=m_q_norm_g, w_out_b=m_w_out_b, ple_norm_g=m_ple_norm_g, w_ple_gate=m_w_ple_gate,
                w_ple_proj=m_w_ple_proj)
    mom2 = dict(norm_g=v_norm_g, w_in_a=v_w_in_a, conv_w=v_conv_w, conv_b=v_conv_b, ln_g=v_ln_g, ln_b=v_ln_b,
                w_out_a=v_w_out_a, kv_norm_g=v_kv_norm_g, w_kv=v_w_kv, k_norm_g=v_k_norm_g, w_in_b=v_w_in_b,
                q_norm_g=v_q_norm_g, w_out_b=v_w_out_b, ple_norm_g=v_ple_norm_g, w_ple_gate=v_w_ple_gate,
                w_ple_proj=v_w_ple_proj)
    order = ["norm_g", "w_in_a", "conv_w", "conv_b", "ln_g", "ln_b", "w_out_a", "kv_norm_g", "w_kv", "k_norm_g", "w_in_b",
             "q_norm_g", "w_out_b", "ple_norm_g", "w_ple_gate", "w_ple_proj"]
    grads, deltas, new_m, new_v = {}, {}, {}, {}
    for nm in order:
        shape = given[nm].shape
        if nm in ("w_ple_gate", "w_ple_proj"):
            g2 = jnp.concatenate([gfull[nm + "0"], gfull[nm + "1"]], axis=0)
        elif nm in gfull:
            g2 = gfull[nm]
        else:
            g2 = small_grads[nm]
        two_d = g2.shape
        d2, m2, v2 = _adamw("adamw_" + nm, given[nm].reshape(two_d), g2, mom1[nm].reshape(two_d), mom2[nm].reshape(two_d))
        grads[nm], deltas[nm], new_m[nm], new_v[nm] = (a.reshape(shape) for a in (g2, d2, m2, v2))

    return (loss, grad_x, *[grads[n_] for n_ in order], *[deltas[n_] for n_ in order],
            *[new_m[n_] for n_ in order], *[new_v[n_] for n_ in order])
```

```python
import functools

import jax
import jax.numpy as jnp
from jax import lax
from jax.experimental import pallas as pl
from jax.experimental.pallas import tpu as pltpu

F32 = jnp.float32
BF16 = jnp.bfloat16
MESH = pl.DeviceIdType.MESH

EPS = 1e-6
NEG_INF = -1e30
HEAD_DIM = 64
ROPE_DIM = 16
ROPE_THETA = 500000.0
CONV_WIDTH = 31
SUBLANES = 8
HALO = 32
SPAN = 128
DILATIONS = (1, 4, 16)
ADAM_LR, ADAM_B1, ADAM_B2, ADAM_EPS, ADAM_WD, ADAM_STEP = 0.001, 0.9, 0.999, 1e-08, 0.01, 10
N_CHIPS = 4
VMEM_LIMIT = 56 * 1024 * 1024


def _tile(n, target, mult=128):
    best = None
    t = mult
    while t <= min(n, target):
        if n % t == 0:
            best = t
        t += mult
    return best if best is not None else n


def _params(*sem):
    return pltpu.CompilerParams(dimension_semantics=tuple(sem) if sem else None, vmem_limit_bytes=VMEM_LIMIT)


def _sigmoid(x):
    return 1.0 / (1.0 + jnp.exp(-x))


def _seg_allsum64(x):
    tr, w = x.shape
    cw = 256 if w % 256 == 0 else 128
    n = w // cw
    ri = lax.shift_right_logical(lax.broadcasted_iota(jnp.int32, (cw, cw), 0), 6)
    ci = lax.shift_right_logical(lax.broadcasted_iota(jnp.int32, (cw, cw), 1), 6)
    ones = (ri == ci).astype(BF16)
    hi = x.astype(BF16)
    lo = (x - hi.astype(F32)).astype(BF16)

    def stack(v):
        return jnp.concatenate([v[:, j * cw:(j + 1) * cw] for j in range(n)], axis=0)

    s = (jnp.dot(stack(hi), ones, preferred_element_type=F32)
         + jnp.dot(stack(lo), ones, preferred_element_type=F32))
    return jnp.concatenate([s[j * tr:(j + 1) * tr] for j in range(n)], axis=1)


def _colsum(x):
    return jnp.sum(x, axis=0, keepdims=True)


def _shards_view(w, kind):
    return w if kind == "col" else w.reshape(1, -1, w.shape[2])


def _mm_nn(name, a, w, kind, *, out_dtype=F32, resid=None):
    t = a.shape[0]
    w = _shards_view(w, kind)
    ns, k, c = w.shape
    n = ns * c
    tm = _tile(t, 1024, 8)
    tk = _tile(k, 2048)
    tn = _tile(c, 1024)
    nk = k // tk
    per = c // tn

    def body(*refs):
        if resid is None:
            a_ref, w_ref, o_ref = refs[:3]
        else:
            a_ref, w_ref, r_ref, o_ref = refs[:4]
        part = jnp.dot(a_ref[...].astype(BF16), w_ref[...], preferred_element_type=F32)

        def finish(out):
            if resid is not None:
                out = out + r_ref[...]
            o_ref[...] = out.astype(out_dtype)

        if nk == 1:
            finish(part)
            return
        acc = refs[-1]
        kk = pl.program_id(2)

        @pl.when(kk == 0)
        def _():
            acc[...] = part

        @pl.when(kk > 0)
        def _():
            acc[...] += part

        @pl.when(kk == nk - 1)
        def _():
            finish(acc[...])

    in_specs = [pl.BlockSpec((tm, tk), lambda i, j, kk: (i, kk)),
                pl.BlockSpec((None, tk, tn), lambda i, j, kk: (j // per, kk, j % per))]
    args = [a, w]
    if resid is not None:
        in_specs.append(pl.BlockSpec((tm, tn), lambda i, j, kk: (i, j)))
        args.append(resid)
    return pl.pallas_call(
        body, name=name, grid=(t // tm, n // tn, nk), in_specs=in_specs,
        out_specs=pl.BlockSpec((tm, tn), lambda i, j, kk: (i, j)),
        out_shape=jax.ShapeDtypeStruct((t, n), out_dtype),
        scratch_shapes=[pltpu.VMEM((tm, tn), F32)] if nk > 1 else [],
        compiler_params=_params("parallel", "parallel", "arbitrary"))(*args)


def _mm_nt(name, d, w, kind, *, out_dtype=F32):
    t = d.shape[0]
    w = _shards_view(w, kind)
    ns, k, c = w.shape
    n = ns * c
    tm = _tile(t, 1024, 8)
    to = _tile(k, 1024)
    tc = _tile(c, 1536)
    nc = n // tc
    per = c // tc

    def body(d_ref, w_ref, o_ref, *scratch):
        part = lax.dot_general(d_ref[...].astype(BF16), w_ref[...], (((1,), (1,)), ((), ())),
                               preferred_element_type=F32)
        if nc == 1:
            o_ref[...] = part.astype(out_dtype)
            return
        acc = scratch[0]
        kk = pl.program_id(2)

        @pl.when(kk == 0)
        def _():
            acc[...] = part

        @pl.when(kk > 0)
        def _():
            acc[...] += part

        @pl.when(kk == nc - 1)
        def _():
            o_ref[...] = acc[...].astype(out_dtype)

    return pl.pallas_call(
        body, name=name, grid=(t // tm, k // to, nc),
        in_specs=[pl.BlockSpec((tm, tc), lambda i, j, kk: (i, kk)),
                  pl.BlockSpec((None, to, tc), lambda i, j, kk: (kk // per, j, kk % per))],
        out_specs=pl.BlockSpec((tm, to), lambda i, j, kk: (i, j)),
        out_shape=jax.ShapeDtypeStruct((t, k), out_dtype),
        scratch_shapes=[pltpu.VMEM((tm, to), F32)] if nc > 1 else [],
        compiler_params=_params("parallel", "parallel", "arbitrary"))(d, w)


def _mm_tn(name, a, d, kind, shard_shape):
    t, k = a.shape
    n = d.shape[1]
    ns = N_CHIPS if kind == "col" else 1
    c = n // ns
    tkm = _tile(k, 1024)
    tn = _tile(c, 1536)
    tt = _tile(t, 1024, 8)
    nt = t // tt
    per = c // tn

    def body(a_ref, d_ref, o_ref, acc):
        kk = pl.program_id(2)
        part = lax.dot_general(a_ref[...].astype(BF16), d_ref[...].astype(BF16), (((0,), (0,)), ((), ())),
                               preferred_element_type=F32)

        @pl.when(kk == 0)
        def _():
            acc[...] = part

        @pl.when(kk > 0)
        def _():
            acc[...] += part

        @pl.when(kk == nt - 1)
        def _():
            o_ref[...] = acc[...].astype(BF16)

    out = pl.pallas_call(
        body, name=name, grid=(k // tkm, n // tn, nt),
        in_specs=[pl.BlockSpec((tt, tkm), lambda i, j, kk: (kk, i)),
                  pl.BlockSpec((tt, tn), lambda i, j, kk: (kk, j))],
        out_specs=pl.BlockSpec((None, tkm, tn), lambda i, j, kk: (j // per, i, j % per)),
        out_shape=jax.ShapeDtypeStruct((ns, k, c), BF16),
        scratch_shapes=[pltpu.VMEM((tkm, tn), F32)],
        compiler_params=_params("parallel", "parallel", "arbitrary"))(a, d)
    return out.reshape((N_CHIPS,) + tuple(shard_shape))


def _row_spec(tr, w, col=0):
    return pl.BlockSpec((tr, w), lambda i: (i, col))


def _full_spec(shape):
    return pl.BlockSpec(shape, lambda i: tuple(0 for _ in shape))


def _rms_fwd(name, x, gains):
    t, dm = x.shape
    tr = _tile(t, 256, 8)
    n = len(gains)

    def body(x_ref, *refs):
        xv = x_ref[...]
        xn = xv * lax.rsqrt(jnp.mean(xv * xv, axis=-1, keepdims=True) + EPS)
        for g_ref, o_ref in zip(refs[:n], refs[n:]):
            o_ref[...] = (xn * g_ref[...]).astype(BF16)

    outs = pl.pallas_call(
        body, name=name, grid=(t // tr,),
        in_specs=[_row_spec(tr, dm)] + [_full_spec((1, dm))] * n,
        out_specs=[_row_spec(tr, dm)] * n,
        out_shape=[jax.ShapeDtypeStruct((t, dm), BF16)] * n,
        compiler_params=_params("parallel"))(x, *gains)
    return list(outs)


def _rms_bwd(name, x, resid, pairs):
    t, dm = x.shape
    tr = _tile(t, 256, 8)
    n = len(pairs)

    def body(x_ref, r_ref, *refs):
        ins, outs = refs[:2 * n], refs[2 * n:]
        i = pl.program_id(0)
        xv = x_ref[...]
        rs = lax.rsqrt(jnp.mean(xv * xv, axis=-1, keepdims=True) + EPS)
        xn = xv * rs
        total = r_ref[...]
        for kx in range(n):
            g_ref, du_ref = ins[2 * kx], ins[2 * kx + 1]
            dg_ref = outs[1 + kx]
            du = du_ref[...]

            @pl.when(i == 0)
            def _():
                dg_ref[...] = jnp.zeros_like(dg_ref)

            dg_ref[...] += _colsum(du * xn)
            dxh = du * g_ref[...]
            total = total + rs * (dxh - xn * jnp.mean(dxh * xn, axis=-1, keepdims=True))
        outs[0][...] = total

    in_specs = [_row_spec(tr, dm), _row_spec(tr, dm)]
    args = [x, resid]
    for g, du in pairs:
        in_specs += [_full_spec((1, dm)), _row_spec(tr, dm)]
        args += [g, du]
    outs = pl.pallas_call(
        body, name=name, grid=(t // tr,), in_specs=in_specs,
        out_specs=[_row_spec(tr, dm)] + [_full_spec((1, dm))] * n,
        out_shape=[jax.ShapeDtypeStruct((t, dm), F32)] + [jax.ShapeDtypeStruct((1, dm), F32)] * n,
        compiler_params=_params("arbitrary"))(*args)
    return outs[0], list(outs[1:])


def _shifted_copies(ext, sh, rows):
    for s in range(1, SUBLANES):
        sh[s - 1] = ext[pl.ds(s, rows), :]


def _window(ext, sh, off, tr):
    s = off % SUBLANES
    src = ext if s == 0 else sh.at[s - 1]
    return src[pl.ds(off - s, tr), :]


def _mixa_fwd(name, pa, cw, cb, lg, lb, seq):
    t, w3 = pa.shape
    cc = w3 // 3
    tr = _tile(seq, 128, HALO)
    per_seq = seq // tr
    hb = tr // HALO
    lead = HALO - (CONV_WIDTH - 1)

    def body(a_ref, b_ref, z_ref, ah_ref, bh_ref, cw_ref, cb_ref, lg_ref, lb_ref, c_ref, m_ref, ext, sh):
        i = pl.program_id(0)
        gh = ah_ref[...].astype(F32) * _sigmoid(bh_ref[...].astype(F32))
        ext[pl.ds(0, HALO), :] = jnp.where((i % per_seq) == 0, 0.0, gh)
        ext[pl.ds(HALO, tr), :] = a_ref[...].astype(F32) * _sigmoid(b_ref[...].astype(F32))
        _shifted_copies(ext, sh, tr + HALO - SUBLANES)
        acc = jnp.broadcast_to(cb_ref[...], (tr, cc))
        for k in range(CONV_WIDTH):
            acc = acc + _window(ext, sh, lead + k, tr) * cw_ref[pl.ds(k, 1), :]
        c_ref[...] = acc
        xc = acc - jnp.mean(acc, axis=-1, keepdims=True)
        nrm = xc * lax.rsqrt(jnp.mean(xc * xc, axis=-1, keepdims=True) + EPS)
        l = nrm * lg_ref[...] + lb_ref[...]
        z = z_ref[...].astype(F32)
        m_ref[...] = (l * _sigmoid(l) * z * _sigmoid(z)).astype(BF16)

    halo = lambda col: pl.BlockSpec((HALO, cc), lambda i: (jnp.maximum(i * hb - 1, 0), col))
    return pl.pallas_call(
        body, name=name, grid=(t // tr,),
        in_specs=[_row_spec(tr, cc, 0), _row_spec(tr, cc, 1), _row_spec(tr, cc, 2), halo(0), halo(1),
                  _full_spec((HALO, cc)), _full_spec((1, cc)), _full_spec((1, cc)), _full_spec((1, cc))],
        out_specs=[_row_spec(tr, cc), _row_spec(tr, cc)],
        out_shape=[jax.ShapeDtypeStruct((t, cc), F32), jax.ShapeDtypeStruct((t, cc), BF16)],
        scratch_shapes=[pltpu.VMEM((tr + HALO, cc), F32), pltpu.VMEM((SUBLANES - 1, tr + HALO - SUBLANES, cc), F32)],
        compiler_params=_params("parallel"))(pa, pa, pa, pa, pa, cw, cb, lg, lb)


def _mixa_bwd1(name, c, pa, dm, lg, lb):
    t, cc = c.shape
    tr = _tile(t, 128, 8)

    def body(c_ref, z_ref, dm_ref, lg_ref, lb_ref, dc_ref, dz_ref, dlg_ref, dlb_ref, dcb_ref):
        i = pl.program_id(0)
        cv = c_ref[...]
        xc = cv - jnp.mean(cv, axis=-1, keepdims=True)
        rs = lax.rsqrt(jnp.mean(xc * xc, axis=-1, keepdims=True) + EPS)
        nrm = xc * rs
        l = nrm * lg_ref[...] + lb_ref[...]
        z = z_ref[...].astype(F32)
        sl, sz = _sigmoid(l), _sigmoid(z)
        dmv = dm_ref[...]
        ds = dmv * (z * sz)
        dzz = dmv * (l * sl)
        dz_ref[...] = (dzz * (sz * (1.0 + z * (1.0 - sz)))).astype(BF16)
        dl = ds * (sl * (1.0 + l * (1.0 - sl)))
        dn = dl * lg_ref[...]
        dc = rs * (dn - jnp.mean(dn, axis=-1, keepdims=True) - nrm * jnp.mean(dn * nrm, axis=-1, keepdims=True))
        dc_ref[...] = dc

        @pl.when(i == 0)
        def _():
            dlg_ref[...] = jnp.zeros_like(dlg_ref)
            dlb_ref[...] = jnp.zeros_like(dlb_ref)
            dcb_ref[...] = jnp.zeros_like(dcb_ref)

        dlg_ref[...] += _colsum(dl * nrm)
        dlb_ref[...] += _colsum(dl)
        dcb_ref[...] += _colsum(dc)

    vec = jax.ShapeDtypeStruct((1, cc), F32)
    return pl.pallas_call(
        body, name=name, grid=(t // tr,),
        in_specs=[_row_spec(tr, cc), _row_spec(tr, cc, 2), _row_spec(tr, cc), _full_spec((1, cc)), _full_spec((1, cc))],
        out_specs=[_row_spec(tr, cc), _row_spec(tr, cc)] + [_full_spec((1, cc))] * 3,
        out_shape=[jax.ShapeDtypeStruct((t, cc), F32), jax.ShapeDtypeStruct((t, cc), BF16), vec, vec, vec],
        compiler_params=_params("arbitrary"))(c, pa, dm, lg, lb)


def _mixa_bwd2(name, dc, pa, dz, cw, seq):
    t, cc = dc.shape
    tr = _tile(seq, 128, HALO)
    per_seq = seq // tr
    hb = tr // HALO
    steps = t // tr
    last_halo = t // HALO - 1

    def body(dc_ref, dcn_ref, a_ref, b_ref, dz_ref, cw_ref, dp_ref, dcw_ref, ext, sh, sums):
        i = pl.program_id(0)
        ext[pl.ds(0, tr), :] = dc_ref[...]
        ext[pl.ds(tr, HALO), :] = jnp.where((i % per_seq) == per_seq - 1, 0.0, dcn_ref[...])
        _shifted_copies(ext, sh, tr + HALO - SUBLANES)

        @pl.when(i == 0)
        def _():
            sums[...] = jnp.zeros_like(sums)
            dcw_ref[...] = jnp.zeros_like(dcw_ref)

        av = a_ref[...].astype(F32)
        sb = _sigmoid(b_ref[...].astype(F32))
        glu = av * sb
        dglu = jnp.zeros((tr, cc), F32)
        for k in range(CONV_WIDTH):
            wd = _window(ext, sh, CONV_WIDTH - 1 - k, tr)
            dglu = dglu + wd * cw_ref[pl.ds(k, 1), :]
            sums[pl.ds(k * SUBLANES, SUBLANES), :] += (wd * glu).reshape(tr // SUBLANES, SUBLANES, cc).sum(axis=0)
        dp_ref[:, pl.ds(0, cc)] = (dglu * sb).astype(BF16)
        dp_ref[:, pl.ds(cc, cc)] = (dglu * av * sb * (1.0 - sb)).astype(BF16)
        dp_ref[:, pl.ds(2 * cc, cc)] = dz_ref[...]

        @pl.when(i == steps - 1)
        def _():
            for k in range(CONV_WIDTH):
                dcw_ref[pl.ds(k, 1), :] = _colsum(sums[pl.ds(k * SUBLANES, SUBLANES), :])

    nxt = pl.BlockSpec((HALO, cc), lambda i: (jnp.minimum((i + 1) * hb, last_halo), 0))
    return pl.pallas_call(
        body, name=name, grid=(steps,),
        in_specs=[_row_spec(tr, cc), nxt, _row_spec(tr, cc, 0), _row_spec(tr, cc, 1), _row_spec(tr, cc),
                  _full_spec((HALO, cc))],
        out_specs=[_row_spec(tr, 3 * cc), _full_spec((HALO, cc))],
        out_shape=[jax.ShapeDtypeStruct((t, 3 * cc), BF16), jax.ShapeDtypeStruct((HALO, cc), F32)],
        scratch_shapes=[pltpu.VMEM((tr + HALO, cc), F32), pltpu.VMEM((SUBLANES - 1, tr + HALO - SUBLANES, cc), F32),
                        pltpu.VMEM((HALO * SUBLANES, cc), F32)],
        compiler_params=_params("arbitrary"))(dc, dc, pa, pa, dz, cw)


def _ple_fwd(name, h, gpre, pp):
    t, dm = h.shape
    tr = _tile(t, 256, 8)

    def body(h_ref, g_ref, p_ref, o_ref):
        o_ref[...] = h_ref[...] + _sigmoid(g_ref[...]) * p_ref[...]

    return pl.pallas_call(
        body, name=name, grid=(t // tr,), in_specs=[_row_spec(tr, dm)] * 3, out_specs=_row_spec(tr, dm),
        out_shape=jax.ShapeDtypeStruct((t, dm), F32), compiler_params=_params("parallel"))(h, gpre, pp)


def _ple_loss(name, h, gpre, pp, target):
    t, dm = h.shape
    tr = _tile(t, 256, 8)

    def body(h_ref, g_ref, p_ref, t_ref, dy_ref, sq_ref):
        i = pl.program_id(0)
        err = h_ref[...] + _sigmoid(g_ref[...]) * p_ref[...] - t_ref[...]
        dy_ref[...] = err * (1.0 / dm)

        @pl.when(i == 0)
        def _():
            sq_ref[...] = jnp.zeros_like(sq_ref)

        sq_ref[...] += jnp.sum(jnp.sum(err * err, axis=1, keepdims=True), axis=0, keepdims=True)

    return pl.pallas_call(
        body, name=name, grid=(t // tr,), in_specs=[_row_spec(tr, dm)] * 4,
        out_specs=[_row_spec(tr, dm), _full_spec((1, 1))],
        out_shape=[jax.ShapeDtypeStruct((t, dm), F32), jax.ShapeDtypeStruct((1, 1), F32)],
        compiler_params=_params("arbitrary"))(h, gpre, pp, target)


def _ple_bwd(name, dy, gpre, pp):
    t, dm = dy.shape
    tr = _tile(t, 256, 8)

    def body(dy_ref, g_ref, p_ref, dg_ref, dp_ref):
        sg = _sigmoid(g_ref[...])
        dyv = dy_ref[...]
        dg_ref[...] = (dyv * p_ref[...] * sg * (1.0 - sg)).astype(BF16)
        dp_ref[...] = (dyv * sg).astype(BF16)

    return pl.pallas_call(
        body, name=name, grid=(t // tr,), in_specs=[_row_spec(tr, dm)] * 3, out_specs=[_row_spec(tr, dm)] * 2,
        out_shape=[jax.ShapeDtypeStruct((t, dm), BF16)] * 2, compiler_params=_params("parallel"))(dy, gpre, pp)


def _rope_tables(seq):
    half = ROPE_DIM // 2
    inv = ROPE_THETA ** (-jnp.arange(half, dtype=F32) * (2.0 / ROPE_DIM))
    ang = jnp.arange(seq).astype(F32)[:, None] * inv[None, :]
    cos, sin = jnp.cos(ang), jnp.sin(ang)
    rest = HEAD_DIM - ROPE_DIM
    one = jnp.ones((seq, rest), F32)
    zero = jnp.zeros((seq, rest), F32)
    zh = jnp.zeros((seq, half), F32)
    tc = jnp.concatenate([cos, cos, one], axis=1)
    ta = jnp.concatenate([-sin, zh, zero], axis=1)
    tb = jnp.concatenate([zh, sin, zero], axis=1)
    return [jnp.tile(tb_, (1, 128 // HEAD_DIM)) for tb_ in (tc, ta, tb)]


def _wide(tab_ref, w):
    return jnp.tile(tab_ref[...], (1, w // 128))


def _hnr_fwd(name, src, width, gain, tables, seq):
    t = src.shape[0]
    tr = _tile(seq, 256, 8)
    per_seq = seq // tr

    def body(x_ref, g_ref, tc_ref, ta_ref, tb_ref, o_ref):
        xv = x_ref[...].astype(F32)
        rs = lax.rsqrt(_seg_allsum64(xv * xv) * (1.0 / HEAD_DIM) + EPS)
        y = xv * rs * g_ref[...]
        o_ref[...] = (y * _wide(tc_ref, width) + pltpu.roll(y, width - ROPE_DIM // 2, 1) * _wide(ta_ref, width)
                      + pltpu.roll(y, ROPE_DIM // 2, 1) * _wide(tb_ref, width))

    tab = pl.BlockSpec((tr, 128), lambda i: (i % per_seq, 0))
    return pl.pallas_call(
        body, name=name, grid=(t // tr,),
        in_specs=[_row_spec(tr, width), _full_spec((1, width)), tab, tab, tab],
        out_specs=_row_spec(tr, width), out_shape=jax.ShapeDtypeStruct((t, width), F32),
        compiler_params=_params("parallel"))(src, gain, *tables)


def _hnr_bwd_math(xv, gain, dout, tc, ta, tb, width):
    dy = dout * tc + pltpu.roll(dout * ta, ROPE_DIM // 2, 1) + pltpu.roll(dout * tb, width - ROPE_DIM // 2, 1)
    rs = lax.rsqrt(_seg_allsum64(xv * xv) * (1.0 / HEAD_DIM) + EPS)
    xn = xv * rs
    dyh = dy * gain
    dx = rs * (dyh - xn * (_seg_allsum64(dyh * xn) * (1.0 / HEAD_DIM)))
    return dx, _colsum(dy * xn)


def _q_bwd(name, p1, gain, tables, dqs, dgt, seq):
    t, w4 = p1.shape
    da = w4 // 4
    width = 3 * da
    tr = _tile(seq, 128, 8)
    per_seq = seq // tr

    def body(x_ref, g_ref, tc_ref, ta_ref, tb_ref, d0_ref, d1_ref, d2_ref, dgt_ref, o_ref, dg_ref):
        i = pl.program_id(0)
        dout = jnp.concatenate([d0_ref[...], d1_ref[...], d2_ref[...]], axis=1)
        dx, dg = _hnr_bwd_math(x_ref[...].astype(F32), g_ref[...], dout, _wide(tc_ref, width), _wide(ta_ref, width),
                               _wide(tb_ref, width), width)

        @pl.when(i == 0)
        def _():
            dg_ref[...] = jnp.zeros_like(dg_ref)

        dg_ref[...] += dg
        o_ref[:, pl.ds(0, width)] = dx.astype(BF16)
        o_ref[:, pl.ds(width, da)] = dgt_ref[...]

    tab = pl.BlockSpec((tr, 128), lambda i: (i % per_seq, 0))
    return pl.pallas_call(
        body, name=name, grid=(t // tr,),
        in_specs=[_row_spec(tr, width), _full_spec((1, width)), tab, tab, tab] + [_row_spec(tr, da)] * 4,
        out_specs=[_row_spec(tr, w4), _full_spec((1, width))],
        out_shape=[jax.ShapeDtypeStruct((t, w4), BF16), jax.ShapeDtypeStruct((1, width), F32)],
        compiler_params=_params("arbitrary"))(p1, gain, *tables, *dqs, dgt)


def _k_bwd(name, kv, gain, tables, dks, dvs, seq):
    t, w2 = kv.shape
    da = w2 // 2
    tr = _tile(seq, 256, 8)
    per_seq = seq // tr

    def body(x_ref, g_ref, tc_ref, ta_ref, tb_ref, k0, k1, k2, v0, v1, v2, o_ref, dg_ref):
        i = pl.program_id(0)
        dout = k0[...] + k1[...] + k2[...]
        dx, dg = _hnr_bwd_math(x_ref[...], g_ref[...], dout, _wide(tc_ref, da), _wide(ta_ref, da), _wide(tb_ref, da), da)

        @pl.when(i == 0)
        def _():
            dg_ref[...] = jnp.zeros_like(dg_ref)

        dg_ref[...] += dg
        o_ref[:, pl.ds(0, da)] = dx.astype(BF16)
        o_ref[:, pl.ds(da, da)] = (v0[...] + v1[...] + v2[...]).astype(BF16)

    tab = pl.BlockSpec((tr, 128), lambda i: (i % per_seq, 0))
    return pl.pallas_call(
        body, name=name, grid=(t // tr,),
        in_specs=[_row_spec(tr, da), _full_spec((1, da)), tab, tab, tab] + [_row_spec(tr, da)] * 6,
        out_specs=[_row_spec(tr, w2), _full_spec((1, da))],
        out_shape=[jax.ShapeDtypeStruct((t, w2), BF16), jax.ShapeDtypeStruct((1, da), F32)],
        compiler_params=_params("arbitrary"))(kv, gain, *tables, *dks, *dvs)


def _unit_rows(ref, dil, r, blk):
    start = r + dil * SPAN * blk
    if dil == 1:
        return ref[pl.ds(start, SPAN), :]
    return ref[pl.ds(start, SPAN, stride=dil), :]


def _store_rows(ref, dil, r, blk, val):
    start = r + dil * SPAN * blk
    if dil == 1:
        ref[pl.ds(start, SPAN), :] = val
    else:
        ref[pl.ds(start, SPAN, stride=dil), :] = val


def _band_mask(with_prev):
    nk = 2 * SPAN if with_prev else SPAN
    qi = lax.broadcasted_iota(jnp.int32, (SPAN, nk), 0)
    kj = lax.broadcasted_iota(jnp.int32, (SPAN, nk), 1)
    if with_prev:
        return (kj >= qi) & (kj <= qi + SPAN)
    return kj <= qi


_NT = (((1,), (1,)), ((), ()))
_TN = (((0,), (0,)), ((), ()))


def _attn_fwd(name, qn, kn, kv, group, seq):
    t, da = kn.shape
    dil = DILATIONS[group]
    nblk = seq // (dil * SPAN)
    hp = da // 128
    scale = HEAD_DIM ** -0.5

    def body(q_ref, k_ref, v_ref, o_ref, l_ref):
        for r in range(dil):
            kc = vc = None
            for blk in range(nblk):
                kp, vp = kc, vc
                q = _unit_rows(q_ref, dil, r, blk) * scale
                kc = _unit_rows(k_ref, dil, r, blk)
                vc = _unit_rows(v_ref, dil, r, blk)
                with_prev = blk > 0
                kcat = jnp.concatenate([kp, kc], axis=0) if with_prev else kc
                vcat = jnp.concatenate([vp, vc], axis=0) if with_prev else vc
                mask = _band_mask(with_prev)
                outs, lses = [], []
                for hh in range(128 // HEAD_DIM):
                    sl = slice(hh * HEAD_DIM, (hh + 1) * HEAD_DIM)
                    s = lax.dot_general(q[:, sl].astype(BF16), kcat[:, sl].astype(BF16), _NT, preferred_element_type=F32)
                    s = jnp.where(mask, s, NEG_INF)
                    mx = jnp.max(s, axis=-1, keepdims=True)
                    p = jnp.exp(s - mx)
                    den = jnp.sum(p, axis=-1, keepdims=True)
                    o = jnp.dot(p.astype(BF16), vcat[:, sl].astype(BF16), preferred_element_type=F32) / den
                    outs.append(o)
                    lses.append(jnp.broadcast_to(mx + jnp.log(den), (SPAN, HEAD_DIM)))
                _store_rows(o_ref, dil, r, blk, jnp.concatenate(outs, axis=1))
                _store_rows(l_ref, dil, r, blk, jnp.concatenate(lses, axis=1))

    blk_spec = lambda off: pl.BlockSpec((seq, 128), lambda b, h: (b, off + h))
    return pl.pallas_call(
        body, name=name, grid=(t // seq, hp),
        in_specs=[blk_spec(group * hp), blk_spec(0), blk_spec(hp)],
        out_specs=[blk_spec(0), blk_spec(0)],
        out_shape=[jax.ShapeDtypeStruct((t, da), F32)] * 2,
        compiler_params=_params("parallel", "parallel"))(qn, kn, kv)


def _attn_bwd(name, qn, kn, kv, do, lse, dsum, group, seq):
    t, da = kn.shape
    dil = DILATIONS[group]
    nblk = seq // (dil * SPAN)
    hp = da // 128
    scale = HEAD_DIM ** -0.5
    nh = 128 // HEAD_DIM

    def body(q_ref, k_ref, v_ref, do_ref, l_ref, d_ref, dq_ref, dk_ref, dv_ref):
        for r in range(dil):
            kc = vc = None
            pend_k = pend_v = None
            for blk in range(nblk):
                kp, vp = kc, vc
                q = _unit_rows(q_ref, dil, r, blk) * scale
                kc = _unit_rows(k_ref, dil, r, blk)
                vc = _unit_rows(v_ref, dil, r, blk)
                dov = _unit_rows(do_ref, dil, r, blk)
                lrow = _unit_rows(l_ref, dil, r, blk)
                drow = _unit_rows(d_ref, dil, r, blk)
                with_prev = blk > 0
                kcat = jnp.concatenate([kp, kc], axis=0) if with_prev else kc
                vcat = jnp.concatenate([vp, vc], axis=0) if with_prev else vc
                mask = _band_mask(with_prev)
                dqs, dkcs, dvcs = [], [], []
                for hh in range(nh):
                    sl = slice(hh * HEAD_DIM, (hh + 1) * HEAD_DIM)
                    qh = q[:, sl].astype(BF16)
                    doh = dov[:, sl].astype(BF16)
                    kh = kcat[:, sl].astype(BF16)
                    s = lax.dot_general(qh, kh, _NT, preferred_element_type=F32)
                    p = jnp.where(mask, jnp.exp(s - lrow[:, hh * HEAD_DIM:hh * HEAD_DIM + 1]), 0.0)
                    dp = lax.dot_general(doh, vcat[:, sl].astype(BF16), _NT, preferred_element_type=F32)
                    ds = (p * (dp - drow[:, hh * HEAD_DIM:hh * HEAD_DIM + 1])).astype(BF16)
                    dqs.append(jnp.dot(ds, kh, preferred_element_type=F32) * scale)
                    dkcs.append(lax.dot_general(ds, qh, _TN, preferred_element_type=F32))
                    dvcs.append(lax.dot_general(p.astype(BF16), doh, _TN, preferred_element_type=F32))
                _store_rows(dq_ref, dil, r, blk, jnp.concatenate(dqs, axis=1))
                dkcat = jnp.concatenate(dkcs, axis=1)
                dvcat = jnp.concatenate(dvcs, axis=1)
                if with_prev:
                    _store_rows(dk_ref, dil, r, blk - 1, pend_k + dkcat[:SPAN])
                    _store_rows(dv_ref, dil, r, blk - 1, pend_v + dvcat[:SPAN])
                    pend_k, pend_v = dkcat[SPAN:], dvcat[SPAN:]
                else:
                    pend_k, pend_v = dkcat, dvcat
            _store_rows(dk_ref, dil, r, nblk - 1, pend_k)
            _store_rows(dv_ref, dil, r, nblk - 1, pend_v)

    blk_spec = lambda off: pl.BlockSpec((seq, 128), lambda b, h: (b, off + h))
    return pl.pallas_call(
        body, name=name, grid=(t // seq, hp),
        in_specs=[blk_spec(group * hp), blk_spec(0), blk_spec(hp), blk_spec(0), blk_spec(0), blk_spec(0)],
        out_specs=[blk_spec(0)] * 3,
        out_shape=[jax.ShapeDtypeStruct((t, da), F32)] * 3,
        compiler_params=_params("parallel", "parallel"))(qn, kn, kv, do, lse, dsum)


def _attn_merge(name, os_, ls_, p1):
    t, da = os_[0].shape
    tr = _tile(t, 256, 8)

    def body(o0, o1, o2, l0, l1, l2, g_ref, o_ref, l_ref, m_ref):
        a0, a1, a2 = l0[...], l1[...], l2[...]
        mx = jnp.maximum(jnp.maximum(a0, a1), a2)
        e0, e1, e2 = jnp.exp(a0 - mx), jnp.exp(a1 - mx), jnp.exp(a2 - mx)
        den = e0 + e1 + e2
        o = (e0 * o0[...] + e1 * o1[...] + e2 * o2[...]) / den
        o_ref[...] = o
        l_ref[...] = mx + jnp.log(den)
        g = g_ref[...].astype(F32)
        m_ref[...] = (o * g * _sigmoid(g)).astype(BF16)

    return pl.pallas_call(
        body, name=name, grid=(t // tr,),
        in_specs=[_row_spec(tr, da)] * 6 + [_row_spec(tr, da, 3)],
        out_specs=[_row_spec(tr, da)] * 3,
        out_shape=[jax.ShapeDtypeStruct((t, da), F32)] * 2 + [jax.ShapeDtypeStruct((t, da), BF16)],
        compiler_params=_params("parallel"))(*os_, *ls_, p1)


def _gate_bwd(name, dm, o, p1):
    t, da = o.shape
    tr = _tile(t, 256, 8)

    def body(dm_ref, o_ref, g_ref, do_ref, dg_ref, ds_ref):
        g = g_ref[...].astype(F32)
        sg = _sigmoid(g)
        dmv, ov = dm_ref[...], o_ref[...]
        do = dmv * (g * sg)
        do_ref[...] = do
        dg_ref[...] = (dmv * ov * (sg * (1.0 + g * (1.0 - sg)))).astype(BF16)
        ds_ref[...] = _seg_allsum64(do * ov)

    return pl.pallas_call(
        body, name=name, grid=(t // tr,),
        in_specs=[_row_spec(tr, da), _row_spec(tr, da), _row_spec(tr, da, 3)],
        out_specs=[_row_spec(tr, da)] * 3,
        out_shape=[jax.ShapeDtypeStruct((t, da), F32), jax.ShapeDtypeStruct((t, da), BF16), jax.ShapeDtypeStruct((t, da), F32)],
        compiler_params=_params("parallel"))(dm, o, p1)


def _cast_bf16(name, w2d):
    r, c = w2d.shape
    tr = _tile(r, 256, 8)

    def body(x_ref, o_ref):
        o_ref[...] = x_ref[...].astype(BF16)

    return pl.pallas_call(
        body, name=name, grid=(r // tr,), in_specs=[_row_spec(tr, c)], out_specs=_row_spec(tr, c),
        out_shape=jax.ShapeDtypeStruct((r, c), BF16), compiler_params=_params("parallel"))(w2d)


def _adamw(name, w, g, m, v):
    r, c = w.shape
    tr = _tile(r, 256, 8)
    c1 = 1.0 - ADAM_B1 ** ADAM_STEP
    c2 = 1.0 - ADAM_B2 ** ADAM_STEP

    def body(w_ref, g_ref, m_ref, v_ref, d_ref, nm_ref, nv_ref):
        gv = g_ref[...]
        nm = ADAM_B1 * m_ref[...] + (1.0 - ADAM_B1) * gv
        nv = ADAM_B2 * v_ref[...] + (1.0 - ADAM_B2) * (gv * gv)
        nm_ref[...] = nm
        nv_ref[...] = nv
        d_ref[...] = -ADAM_LR * ((nm / c1) / (jnp.sqrt(nv / c2) + ADAM_EPS) + ADAM_WD * w_ref[...])

    sds = jax.ShapeDtypeStruct((r, c), F32)
    return pl.pallas_call(
        body, name=name, grid=(r // tr,), in_specs=[_row_spec(tr, c)] * 4, out_specs=[_row_spec(tr, c)] * 3,
        out_shape=[sds] * 3, compiler_params=_params("parallel"))(w, g, m, v)


def _pair_sum(name, gd, recv, core):
    _, r, c = gd.shape
    rh = r // 2
    tr = _tile(rh, 256, 8)
    nrt = rh // tr

    def body(c_ref, a_ref, b_ref, o_ref):
        o_ref[...] = (a_ref[...].astype(F32) + b_ref[...].astype(F32)).astype(BF16)

    grid_spec = pltpu.PrefetchScalarGridSpec(
        num_scalar_prefetch=1, grid=(N_CHIPS, nrt),
        in_specs=[pl.BlockSpec((None, tr, c), lambda j, i, cr: (j, cr[0] * nrt + i, 0)),
                  pl.BlockSpec((None, tr, c), lambda j, i, cr: (j, i, 0))],
        out_specs=pl.BlockSpec((None, tr, c), lambda j, i, cr: (j, i, 0)))
    return pl.pallas_call(
        body, name=name, grid_spec=grid_spec, out_shape=jax.ShapeDtypeStruct((N_CHIPS, rh, c), BF16),
        compiler_params=_params("parallel", "parallel"))(core, gd, recv)


def _chip_sum(name, parts):
    _, rh, c = parts.shape
    tr = _tile(rh, 256, 8)

    def body(p_ref, o_ref):
        acc = p_ref[0].astype(F32)
        for j in range(1, N_CHIPS):
            acc = acc + p_ref[j].astype(F32)
        o_ref[...] = acc

    return pl.pallas_call(
        body, name=name, grid=(rh // tr,),
        in_specs=[pl.BlockSpec((N_CHIPS, tr, c), lambda i: (0, i, 0))], out_specs=_row_spec(tr, c),
        out_shape=jax.ShapeDtypeStruct((rh, c), F32), compiler_params=_params("parallel"))(parts)


HBM = pl.BlockSpec(memory_space=pl.ANY)


def _place():
    x, y, c = lax.axis_index("x"), lax.axis_index("y"), lax.axis_index("c")
    chips = [(1 - x, y), (x, 1 - y), (1 - x, 1 - y)]
    return x, y, c, chips


def _half(ref, hc):
    rows = ref.shape[0] // 2
    return ref.at[pl.ds(hc * rows, rows)]


def _gather_weights(shards):
    n = len(shards)

    def body(*refs):
        ins, outs = refs[:n], refs[n:2 * n]
        send_sems, recv_sems, loc_sems = refs[2 * n:]
        x, y, c, chips = _place()
        mine = 2 * x + y
        local, first, passed = [], [], []
        for i in range(n):
            cp = pltpu.make_async_copy(ins[i], outs[i].at[mine], loc_sems.at[i])
            cp.start()
            local.append(cp)
        for i in range(n):
            for k, chip in enumerate(chips):
                cp = pltpu.make_async_remote_copy(
                    src_ref=_half(ins[i], c), dst_ref=_half(outs[i].at[mine], c),
                    send_sem=send_sems.at[6 * i + k], recv_sem=recv_sems.at[6 * i + k],
                    device_id=(chip[0], chip[1], c), device_id_type=MESH)
                cp.start()
                first.append(cp)
        for i in range(n):
            for k, chip in enumerate(chips):
                theirs = 2 * chip[0] + chip[1]
                landed = _half(outs[i].at[theirs], c)
                pltpu.make_async_remote_copy(
                    src_ref=landed, dst_ref=landed, send_sem=send_sems.at[6 * i + k], recv_sem=recv_sems.at[6 * i + k],
                    device_id=(chip[0], chip[1], c), device_id_type=MESH).wait_recv()
                cp = pltpu.make_async_remote_copy(
                    src_ref=landed, dst_ref=landed, send_sem=send_sems.at[6 * i + 3 + k], recv_sem=recv_sems.at[6 * i + 3 + k],
                    device_id=(x, y, 1 - c), device_id_type=MESH)
                cp.start()
                passed.append(cp)
        for i in range(n):
            for k, chip in enumerate(chips):
                theirs = 2 * chip[0] + chip[1]
                other = _half(outs[i].at[theirs], 1 - c)
                pltpu.make_async_remote_copy(
                    src_ref=other, dst_ref=other, send_sem=send_sems.at[6 * i + 3 + k], recv_sem=recv_sems.at[6 * i + 3 + k],
                    device_id=(x, y, 1 - c), device_id_type=MESH).wait_recv()
        for cp in first + passed:
            cp.wait_send()
        for cp in local:
            cp.wait()

    return pl.pallas_call(
        body, name="gather_weights", in_specs=[HBM] * n, out_specs=[HBM] * n,
        out_shape=[jax.ShapeDtypeStruct((N_CHIPS,) + s.shape, s.dtype) for s in shards],
        scratch_shapes=[pltpu.SemaphoreType.DMA((6 * n,)), pltpu.SemaphoreType.DMA((6 * n,)), pltpu.SemaphoreType.DMA((n,))],
        )(*shards)


def _pair_exchange(grads):
    n = len(grads)

    def body(*refs):
        ins, outs = refs[:n], refs[n:2 * n]
        send_sems, recv_sems = refs[2 * n:]
        x, y, c, _ = _place()
        cps = []
        for i in range(n):
            rows = ins[i].shape[1] // 2
            cp = pltpu.make_async_remote_copy(
                src_ref=ins[i].at[:, pl.ds((1 - c) * rows, rows), :], dst_ref=outs[i],
                send_sem=send_sems.at[i], recv_sem=recv_sems.at[i], device_id=(x, y, 1 - c), device_id_type=MESH)
            cp.start()
            cps.append(cp)
        for cp in cps:
            cp.wait()

    return pl.pallas_call(
        body, name="pair_exchange", in_specs=[HBM] * n, out_specs=[HBM] * n,
        out_shape=[jax.ShapeDtypeStruct((N_CHIPS, g.shape[1] // 2, g.shape[2]), g.dtype) for g in grads],
        scratch_shapes=[pltpu.SemaphoreType.DMA((n,)), pltpu.SemaphoreType.DMA((n,))],
        )(*grads)


def _chip_scatter(sums):
    n = len(sums)

    def body(*refs):
        ins, outs = refs[:n], refs[n:2 * n]
        send_sems, recv_sems, loc_sems = refs[2 * n:]
        x, y, c, chips = _place()
        mine = 2 * x + y
        cps, local = [], []
        for i in range(n):
            cp = pltpu.make_async_copy(ins[i].at[mine], outs[i].at[mine], loc_sems.at[i])
            cp.start()
            local.append(cp)
            for k, chip in enumerate(chips):
                theirs = 2 * chip[0] + chip[1]
                cp = pltpu.make_async_remote_copy(
                    src_ref=ins[i].at[theirs], dst_ref=outs[i].at[mine],
                    send_sem=send_sems.at[3 * i + k], recv_sem=recv_sems.at[3 * i + k],
                    device_id=(chip[0], chip[1], c), device_id_type=MESH)
                cp.start()
                cps.append((cp, i, k, theirs))
        for cp, i, k, theirs in cps:
            cp.wait_send()
            pltpu.make_async_remote_copy(
                src_ref=ins[i].at[theirs], dst_ref=outs[i].at[theirs],
                send_sem=send_sems.at[3 * i + k], recv_sem=recv_sems.at[3 * i + k],
                device_id=(x, y, c), device_id_type=MESH).wait_recv()
        for cp in local:
            cp.wait()

    return pl.pallas_call(
        body, name="chip_scatter", in_specs=[HBM] * n, out_specs=[HBM] * n,
        out_shape=[jax.ShapeDtypeStruct(s.shape, s.dtype) for s in sums],
        scratch_shapes=[pltpu.SemaphoreType.DMA((3 * n,)), pltpu.SemaphoreType.DMA((3 * n,)), pltpu.SemaphoreType.DMA((n,))],
        )(*sums)


def _sibling_join(halves):
    n = len(halves)

    def body(*refs):
        ins, outs = refs[:n], refs[n:2 * n]
        send_sems, recv_sems, loc_sems = refs[2 * n:]
        x, y, c, _ = _place()
        cps, local = [], []
        for i in range(n):
            cp = pltpu.make_async_copy(ins[i], _half(outs[i], c), loc_sems.at[i])
            cp.start()
            local.append(cp)
            cp = pltpu.make_async_remote_copy(
                src_ref=ins[i], dst_ref=_half(outs[i], c), send_sem=send_sems.at[i], recv_sem=recv_sems.at[i],
                device_id=(x, y, 1 - c), device_id_type=MESH)
            cp.start()
            cps.append(cp)
        for i, cp in enumerate(cps):
            cp.wait_send()
            pltpu.make_async_remote_copy(
                src_ref=ins[i], dst_ref=_half(outs[i], 1 - c), send_sem=send_sems.at[i], recv_sem=recv_sems.at[i],
                device_id=(x, y, 1 - c), device_id_type=MESH).wait_recv()
        for cp in local:
            cp.wait()

    return pl.pallas_call(
        body, name="sibling_join", in_specs=[HBM] * n, out_specs=[HBM] * n,
        out_shape=[jax.ShapeDtypeStruct((2 * h.shape[0], h.shape[1]), h.dtype) for h in halves],
        scratch_shapes=[pltpu.SemaphoreType.DMA((n,)), pltpu.SemaphoreType.DMA((n,)), pltpu.SemaphoreType.DMA((n,))],
        )(*halves)


def _gather8(name, block, reduce):
    m, n = block.shape

    def body(x_ref, out_ref, *scratch):
        if reduce:
            all_ref, send_sems, recv_sems, local_sem = scratch
        else:
            all_ref = out_ref
            send_sems, recv_sems, local_sem = scratch
        x, y, c, chips = _place()
        me, sibling = (x, y, c), (x, y, 1 - c)

        def rows(px, py, pc):
            return all_ref.at[pl.ds((4 * px + 2 * py + pc) * m, m), :]

        def copy(k, blk, to, src=None):
            return pltpu.make_async_remote_copy(
                src_ref=rows(*blk) if src is None else src, dst_ref=rows(*blk),
                send_sem=send_sems.at[k], recv_sem=recv_sems.at[k], device_id=to, device_id_type=MESH)

        mine = pltpu.make_async_copy(x_ref, rows(*me), local_sem)
        mine.start()
        first = [copy(0, me, sibling, src=x_ref)]
        first += [copy(1 + j, me, (chip[0], chip[1], c), src=x_ref) for j, chip in enumerate(chips)]
        for cp in first:
            cp.start()
        passed = [copy(4 + j, (chip[0], chip[1], c), sibling) for j, chip in enumerate(chips)]
        for j, chip in enumerate(chips):
            copy(1 + j, (chip[0], chip[1], c), me).wait_recv()
            passed[j].start()
        copy(0, sibling, me).wait_recv()
        for j, chip in enumerate(chips):
            copy(4 + j, (chip[0], chip[1], 1 - c), me).wait_recv()
        for cp in first + passed:
            cp.wait_send()
        mine.wait()
        if reduce:
            acc = all_ref[pl.ds(0, m), :]
            for d in range(1, 8):
                acc = acc + all_ref[pl.ds(d * m, m), :]
            out_ref[...] = acc

    sems = [pltpu.SemaphoreType.DMA((7,)), pltpu.SemaphoreType.DMA((7,)), pltpu.SemaphoreType.DMA]
    scratch = ([pltpu.VMEM((8 * m, n), F32)] if reduce else []) + sems
    return pl.pallas_call(
        body, name=name,
        out_shape=jax.ShapeDtypeStruct((m, n) if reduce else (8 * m, n), F32),
        in_specs=[pl.BlockSpec(memory_space=pltpu.VMEM)], out_specs=pl.BlockSpec(memory_space=pltpu.VMEM),
        scratch_shapes=scratch)(block)


def _pad_rows(a, rows):
    return jnp.concatenate([a, jnp.zeros((rows - a.shape[0], a.shape[1]), a.dtype)], axis=0)


def kernel(x, p, norm_g, w_in_a, conv_w, conv_b, ln_g, ln_b, w_out_a, kv_norm_g, w_kv, k_norm_g, w_in_b, q_norm_g, w_out_b, ple_norm_g, w_ple_gate, w_ple_proj, loss_target, m_norm_g, m_w_in_a, m_conv_w, m_conv_b, m_ln_g, m_ln_b, m_w_out_a, m_kv_norm_g, m_w_kv, m_k_norm_g, m_w_in_b, m_q_norm_g, m_w_out_b, m_ple_norm_g, m_w_ple_gate, m_w_ple_proj, v_norm_g, v_w_in_a, v_conv_w, v_conv_b, v_ln_g, v_ln_b, v_w_out_a, v_kv_norm_g, v_w_kv, v_k_norm_g, v_w_in_b, v_q_norm_g, v_w_out_b, v_ple_norm_g, v_w_ple_gate, v_w_ple_proj):
    nb, seq, dm = x.shape
    t = nb * seq
    ple = p.shape[-1]
    ccs = conv_w.shape[-1]
    cc = N_CHIPS * ccs
    da = dm
    nheads = da // HEAD_DIM
    assert seq == DILATIONS[-1] * SPAN and da % 128 == 0 and ccs % 128 == 0

    core = lax.axis_index("c").astype(jnp.int32).reshape(1)
    chip = 2 * lax.axis_index("x") + lax.axis_index("y")

    x2 = x.reshape(t, dm)
    tgt2 = loss_target.reshape(t, dm)
    p0 = p[0].reshape(t, ple)
    p1 = p[1].reshape(t, ple)

    big = [
        ("w_in_a", w_in_a[0], "col"), ("w_out_a", w_out_a[0], "row"), ("w_kv", w_kv, "col"),
        ("w_in_b", w_in_b[0], "col"), ("w_out_b", w_out_b[0], "row"),
        ("w_ple_gate0", w_ple_gate[0], "row"), ("w_ple_gate1", w_ple_gate[1], "row"),
        ("w_ple_proj0", w_ple_proj[0], "col"), ("w_ple_proj1", w_ple_proj[1], "col"),
    ]
    kinds = {nm: kd for nm, _, kd in big}
    shard_shape = {nm: w.shape for nm, w, _ in big}
    gathered = _gather_weights([_cast_bf16("cast_" + nm, w) for nm, w, _ in big])
    W = {nm: g for (nm, _, _), g in zip(big, gathered)}

    vec_rows = 40
    small = _pad_rows(jnp.concatenate([conv_w[0], conv_b, ln_g, ln_b], axis=0), vec_rows)
    allv = _gather8("gather_conv_vectors", small, reduce=False).reshape(N_CHIPS, 2, vec_rows, ccs)[:, 0]
    allv = allv.transpose(1, 0, 2).reshape(vec_rows, cc)
    cw_full, cb_full, lg_full, lb_full = allv[:HALO], allv[31:32], allv[32:33], allv[33:34]
    cw_full = cw_full * (lax.broadcasted_iota(jnp.int32, (HALO, 1), 0) < CONV_WIDTH).astype(F32)

    tables = _rope_tables(seq)
    gain_q = jnp.tile(q_norm_g[0][:, None, :], (1, nheads, 1)).reshape(1, 3 * da)
    gain_k = jnp.tile(k_norm_g[None, :], (1, nheads))
    g0, g1 = norm_g[0:1], norm_g[1:2]
    pg0, pg1 = ple_norm_g[0:1], ple_norm_g[1:2]
    kvg = kv_norm_g[None, :]

    (u0,) = _rms_fwd("rms_u0", x2, [g0])
    pa = _mm_nn("mm_in_a", u0, W["w_in_a"], "col", out_dtype=BF16)
    conv_out, m_a = _mixa_fwd("mixa_fwd", pa, cw_full, cb_full, lg_full, lb_full, seq)
    h0 = _mm_nn("mm_out_a", m_a, W["w_out_a"], "row", resid=x2)
    (r0,) = _rms_fwd("rms_r0", h0, [pg0])
    gpre0 = _mm_nn("mm_gate0", r0, W["w_ple_gate0"], "row")
    pp0 = _mm_nn("mm_proj0", p0, W["w_ple_proj0"], "col")
    x1 = _ple_fwd("ple_fwd0", h0, gpre0, pp0)
    kvn, u1 = _rms_fwd("rms_kv_u1", x1, [kvg, g1])
    kv = _mm_nn("mm_kv", kvn, W["w_kv"], "col")
    kn = _hnr_fwd("k_norm_rope", kv, da, gain_k, tables, seq)
    pb = _mm_nn("mm_in_b", u1, W["w_in_b"], "col", out_dtype=BF16)
    qn = _hnr_fwd("q_norm_rope", pb, 3 * da, gain_q, tables, seq)
    og, lg_ = [], []
    for g in range(3):
        o_g, l_g = _attn_fwd("attn_fwd%d" % g, qn, kn, kv, g, seq)
        og.append(o_g)
        lg_.append(l_g)
    o, lse, m_b = _attn_merge("attn_merge", og, lg_, pb)
    h1 = _mm_nn("mm_out_b", m_b, W["w_out_b"], "row", resid=x1)
    (r1,) = _rms_fwd("rms_r1", h1, [pg1])
    gpre1 = _mm_nn("mm_gate1", r1, W["w_ple_gate1"], "row")
    pp1 = _mm_nn("mm_proj1", p1, W["w_ple_proj1"], "col")
    dy, sq = _ple_loss("ple_loss", h1, gpre1, pp1, tgt2)
    loss = lax.psum(0.5 * sq[0, 0] / dm, ("x", "y", "c"))

    G = {}
    dgp1, dpp1 = _ple_bwd("ple_bwd1", dy, gpre1, pp1)
    G["w_ple_gate1"] = _mm_tn("tn_gate1", r1, dgp1, "row", shard_shape["w_ple_gate1"])
    G["w_ple_proj1"] = _mm_tn("tn_proj1", p1, dpp1, "col", shard_shape["w_ple_proj1"])
    dr1 = _mm_nt("nt_gate1", dgp1, W["w_ple_gate1"], "row")
    dh1, (dpg1,) = _rms_bwd("rms_bwd_r1", h1, dy, [(pg1, dr1)])
    G["w_out_b"] = _mm_tn("tn_out_b", m_b, dh1, "row", shard_shape["w_out_b"])
    dm_b = _mm_nt("nt_out_b", dh1, W["w_out_b"], "row")
    d_o, dgt, dsum = _gate_bwd("gate_bwd", dm_b, o, pb)
    dqs, dks, dvs = [], [], []
    for g in range(3):
        dq_g, dk_g, dv_g = _attn_bwd("attn_bwd%d" % g, qn, kn, kv, d_o, lse, dsum, g, seq)
        dqs.append(dq_g)
        dks.append(dk_g)
        dvs.append(dv_g)
    dpb, dgq = _q_bwd("q_bwd", pb, gain_q, tables, dqs, dgt, seq)
    dkv, dgk = _k_bwd("k_bwd", kv, gain_k, tables, dks, dvs, seq)
    G["w_in_b"] = _mm_tn("tn_in_b", u1, dpb, "col", shard_shape["w_in_b"])
    du1 = _mm_nt("nt_in_b", dpb, W["w_in_b"], "col")
    G["w_kv"] = _mm_tn("tn_kv", kvn, dkv, "col", shard_shape["w_kv"])
    dkvn = _mm_nt("nt_kv", dkv, W["w_kv"], "col")
    dx1, (dg1, dkvg) = _rms_bwd("rms_bwd_x1", x1, dh1, [(g1, du1), (kvg, dkvn)])
    dgp0, dpp0 = _ple_bwd("ple_bwd0", dx1, gpre0, pp0)
    G["w_ple_gate0"] = _mm_tn("tn_gate0", r0, dgp0, "row", shard_shape["w_ple_gate0"])
    G["w_ple_proj0"] = _mm_tn("tn_proj0", p0, dpp0, "col", shard_shape["w_ple_proj0"])
    dr0 = _mm_nt("nt_gate0", dgp0, W["w_ple_gate0"], "row")
    dh0, (dpg0,) = _rms_bwd("rms_bwd_r0", h0, dx1, [(pg0, dr0)])
    G["w_out_a"] = _mm_tn("tn_out_a", m_a, dh0, "row", shard_shape["w_out_a"])
    dm_a = _mm_nt("nt_out_a", dh0, W["w_out_a"], "row")
    dc, dz, dlg, dlb, dcb = _mixa_bwd1("mixa_bwd1", conv_out, pa, dm_a, lg_full, lb_full)
    dpa, dcw = _mixa_bwd2("mixa_bwd2", dc, pa, dz, cw_full, seq)
    G["w_in_a"] = _mm_tn("tn_in_a", u0, dpa, "col", shard_shape["w_in_a"])
    du0 = _mm_nt("nt_in_a", dpa, W["w_in_a"], "col")
    dx, (dg0,) = _rms_bwd("rms_bwd_x", x2, dh0, [(g0, du0)])
    grad_x = dx.reshape(nb, seq, dm)

    names = [nm for nm, _, _ in big]
    recv = _pair_exchange([G[nm] for nm in names])
    sums = [_pair_sum("pair_sum_" + nm, G[nm], rc, core) for nm, rc in zip(names, recv)]
    parts = _chip_scatter(sums)
    halves = [_chip_sum("chip_sum_" + nm, pt) for nm, pt in zip(names, parts)]
    gfull = dict(zip(names, _sibling_join(halves)))

    def as_rows(a):
        return a.reshape(-1, dm)

    small_parts = [as_rows(dcw), as_rows(dcb), as_rows(dlg), as_rows(dlb), dg0, dg1, dkvg, dpg0, dpg1, as_rows(dgk), as_rows(dgq)]
    counts = [a.shape[0] for a in small_parts]
    total = sum(counts)
    packed = _pad_rows(jnp.concatenate(small_parts, axis=0), -(-total // 8) * 8)
    red = _gather8("reduce_small", packed, reduce=True)
    pieces, off = [], 0
    for n_ in counts:
        pieces.append(red[off:off + n_])
        off += n_
    r_dcw, r_dcb, r_dlg, r_dlb, r_g0, r_g1, r_kvg, r_pg0, r_pg1, r_gk, r_gq = pieces
    my_cols = lambda a: lax.dynamic_slice_in_dim(a.reshape(-1, cc), chip * ccs, ccs, axis=1)
    small_grads = {
        "norm_g": jnp.concatenate([r_g0, r_g1], axis=0),
        "conv_w": my_cols(r_dcw)[:CONV_WIDTH],
        "conv_b": my_cols(r_dcb), "ln_g": my_cols(r_dlg), "ln_b": my_cols(r_dlb),
        "kv_norm_g": r_kvg,
        "k_norm_g": r_gk.reshape(nheads, HEAD_DIM).sum(axis=0, keepdims=True),
        "q_norm_g": r_gq.reshape(3, nheads, HEAD_DIM).sum(axis=1),
        "ple_norm_g": jnp.concatenate([r_pg0, r_pg1], axis=0),
    }

    given = dict(norm_g=norm_g, w_in_a=w_in_a, conv_w=conv_w, conv_b=conv_b, ln_g=ln_g, ln_b=ln_b, w_out_a=w_out_a,
                 kv_norm_g=kv_norm_g, w_kv=w_kv, k_norm_g=k_norm_g, w_in_b=w_in_b, q_norm_g=q_norm_g, w_out_b=w_out_b,
                 ple_norm_g=ple_norm_g, w_ple_gate=w_ple_gate, w_ple_proj=w_ple_proj)
    mom1 = dict(norm_g=m_norm_g, w_in_a=m_w_in_a, conv_w=m_conv_w, conv_b=m_conv_b, ln_g=m_ln_g, ln_b=m_ln_b,
                w_out_a=m_w_out_a, kv_norm_g=m_kv_norm_g, w_kv=m_w_kv, k_norm_g=m_k_norm_g, w_in_b=m_w_in_b,
                q_norm_g=m_q_norm_g, w_out_b=m_w_out_b, ple_norm_g=m_ple_norm_g, w_ple_gate=m_w_ple_gate,
                w_ple_proj=m_w_ple_proj)
    mom2 = dict(norm_g=v_norm_g, w_in_a=v_w_in_a, conv_w=v_conv_w, conv_b=v_conv_b, ln_g=v_ln_g, ln_b=v_ln_b,
                w_out_a=v_w_out_a, kv_norm_g=v_kv_norm_g, w_kv=v_w_kv, k_norm_g=v_k_norm_g, w_in_b=v_w_in_b,
                q_norm_g=v_q_norm_g, w_out_b=v_w_out_b, ple_norm_g=v_ple_norm_g, w_ple_gate=v_w_ple_gate,
                w_ple_proj=v_w_ple_proj)
    order = ["norm_g", "w_in_a", "conv_w", "conv_b", "ln_g", "ln_b", "w_out_a", "kv_norm_g", "w_kv", "k_norm_g", "w_in_b",
             "q_norm_g", "w_out_b", "ple_norm_g", "w_ple_gate", "w_ple_proj"]
    grads, deltas, new_m, new_v = {}, {}, {}, {}
    for nm in order:
        shape = given[nm].shape
        if nm in ("w_ple_gate", "w_ple_proj"):
            g2 = jnp.concatenate([gfull[nm + "0"], gfull[nm + "1"]], axis=0)
        elif nm in gfull:
            g2 = gfull[nm]
        else:
            g2 = small_grads[nm]
        two_d = g2.shape
        d2, m2, v2 = _adamw("adamw_" + nm, given[nm].reshape(two_d), g2, mom1[nm].reshape(two_d), mom2[nm].reshape(two_d))
        grads[nm], deltas[nm], new_m[nm], new_v[nm] = (a.reshape(shape) for a in (g2, d2, m2, v2))

    return (loss, grad_x, *[grads[n_] for n_ in order], *[deltas[n_] for n_ in order],
            *[new_m[n_] for n_ in order], *[new_v[n_] for n_ in order])
```

```python
import functools

import jax
import jax.numpy as jnp
from jax import lax
from jax.experimental import pallas as pl
from jax.experimental.pallas import tpu as pltpu

F32 = jnp.float32
BF16 = jnp.bfloat16
MESH = pl.DeviceIdType.MESH

EPS = 1e-6
NEG_INF = -1e30
HEAD_DIM = 64
ROPE_DIM = 16
ROPE_THETA = 500000.0
CONV_WIDTH = 31
SUBLANES = 8
CONV_ROWS = 64
HALO = 32
SPAN = 128
DILATIONS = (1, 4, 16)
ADAM_LR, ADAM_B1, ADAM_B2, ADAM_EPS, ADAM_WD, ADAM_STEP = 0.001, 0.9, 0.999, 1e-08, 0.01, 10
N_CHIPS = 4
VMEM_LIMIT = 56 * 1024 * 1024


def _tile(n, target, mult=128):
    best = None
    t = mult
    while t <= min(n, target):
        if n % t == 0:
            best = t
        t += mult
    return best if best is not None else n


def _params(*sem):
    return pltpu.CompilerParams(dimension_semantics=tuple(sem) if sem else None, vmem_limit_bytes=VMEM_LIMIT)


def _sigmoid(x):
    return 1.0 / (1.0 + jnp.exp(-x))


def _seg_allsum64(x):
    tr, w = x.shape
    cw = 256 if w % 256 == 0 else 128
    n = w // cw
    ri = lax.shift_right_logical(lax.broadcasted_iota(jnp.int32, (cw, cw), 0), 6)
    ci = lax.shift_right_logical(lax.broadcasted_iota(jnp.int32, (cw, cw), 1), 6)
    ones = (ri == ci).astype(BF16)
    hi = x.astype(BF16)
    lo = (x - hi.astype(F32)).astype(BF16)

    def stack(v):
        return jnp.concatenate([v[:, j * cw:(j + 1) * cw] for j in range(n)], axis=0)

    s = (jnp.dot(stack(hi), ones, preferred_element_type=F32)
         + jnp.dot(stack(lo), ones, preferred_element_type=F32))
    return jnp.concatenate([s[j * tr:(j + 1) * tr] for j in range(n)], axis=1)


def _colsum(x):
    return jnp.sum(x, axis=0, keepdims=True)


def _shards_view(w, kind):
    return w if kind == "col" else w.reshape(1, -1, w.shape[2])


def _mm_nn(name, a, w, kind, *, out_dtype=F32, resid=None):
    t = a.shape[0]
    w = _shards_view(w, kind)
    ns, k, c = w.shape
    n = ns * c
    tm = _tile(t, 1024, 8)
    tk = _tile(k, 2048)
    tn = _tile(c, 1024)
    nk = k // tk
    per = c // tn

    def body(*refs):
        if resid is None:
            a_ref, w_ref, o_ref = refs[:3]
        else:
            a_ref, w_ref, r_ref, o_ref = refs[:4]
        part = jnp.dot(a_ref[...].astype(BF16), w_ref[...], preferred_element_type=F32)

        def finish(out):
            if resid is not None:
                out = out + r_ref[...]
            o_ref[...] = out.astype(out_dtype)

        if nk == 1:
            finish(part)
            return
        acc = refs[-1]
        kk = pl.program_id(2)

        @pl.when(kk == 0)
        def _():
            acc[...] = part

        @pl.when(kk > 0)
        def _():
            acc[...] += part

        @pl.when(kk == nk - 1)
        def _():
            finish(acc[...])

    in_specs = [pl.BlockSpec((tm, tk), lambda i, j, kk: (i, kk)),
                pl.BlockSpec((None, tk, tn), lambda i, j, kk: (j // per, kk, j % per))]
    args = [a, w]
    if resid is not None:
        in_specs.append(pl.BlockSpec((tm, tn), lambda i, j, kk: (i, j)))
        args.append(resid)
    return pl.pallas_call(
        body, name=name, grid=(t // tm, n // tn, nk), in_specs=in_specs,
        out_specs=pl.BlockSpec((tm, tn), lambda i, j, kk: (i, j)),
        out_shape=jax.ShapeDtypeStruct((t, n), out_dtype),
        scratch_shapes=[pltpu.VMEM((tm, tn), F32)] if nk > 1 else [],
        compiler_params=_params("parallel", "parallel", "arbitrary"))(*args)


def _mm_nt(name, d, w, kind, *, out_dtype=F32):
    t = d.shape[0]
    w = _shards_view(w, kind)
    ns, k, c = w.shape
    n = ns * c
    tm = _tile(t, 1024, 8)
    to = _tile(k, 1024)
    tc = _tile(c, 1536)
    nc = n // tc
    per = c // tc

    def body(d_ref, w_ref, o_ref, *scratch):
        part = lax.dot_general(d_ref[...].astype(BF16), w_ref[...], (((1,), (1,)), ((), ())),
                               preferred_element_type=F32)
        if nc == 1:
            o_ref[...] = part.astype(out_dtype)
            return
        acc = scratch[0]
        kk = pl.program_id(2)

        @pl.when(kk == 0)
        def _():
            acc[...] = part

        @pl.when(kk > 0)
        def _():
            acc[...] += part

        @pl.when(kk == nc - 1)
        def _():
            o_ref[...] = acc[...].astype(out_dtype)

    return pl.pallas_call(
        body, name=name, grid=(t // tm, k // to, nc),
        in_specs=[pl.BlockSpec((tm, tc), lambda i, j, kk: (i, kk)),
                  pl.BlockSpec((None, to, tc), lambda i, j, kk: (kk // per, j, kk % per))],
        out_specs=pl.BlockSpec((tm, to), lambda i, j, kk: (i, j)),
        out_shape=jax.ShapeDtypeStruct((t, k), out_dtype),
        scratch_shapes=[pltpu.VMEM((tm, to), F32)] if nc > 1 else [],
        compiler_params=_params("parallel", "parallel", "arbitrary"))(d, w)


def _mm_tn(name, a, d, kind, shard_shape):
    t, k = a.shape
    n = d.shape[1]
    ns = N_CHIPS if kind == "col" else 1
    c = n // ns
    tkm = _tile(k, 1024)
    tn = _tile(c, 1536)
    tt = _tile(t, 1024, 8)
    nt = t // tt
    per = c // tn

    def body(a_ref, d_ref, o_ref, acc):
        kk = pl.program_id(2)
        part = lax.dot_general(a_ref[...].astype(BF16), d_ref[...].astype(BF16), (((0,), (0,)), ((), ())),
                               preferred_element_type=F32)

        @pl.when(kk == 0)
        def _():
            acc[...] = part

        @pl.when(kk > 0)
        def _():
            acc[...] += part

        @pl.when(kk == nt - 1)
        def _():
            o_ref[...] = acc[...].astype(BF16)

    out = pl.pallas_call(
        body, name=name, grid=(k // tkm, n // tn, nt),
        in_specs=[pl.BlockSpec((tt, tkm), lambda i, j, kk: (kk, i)),
                  pl.BlockSpec((tt, tn), lambda i, j, kk: (kk, j))],
        out_specs=pl.BlockSpec((None, tkm, tn), lambda i, j, kk: (j // per, i, j % per)),
        out_shape=jax.ShapeDtypeStruct((ns, k, c), BF16),
        scratch_shapes=[pltpu.VMEM((tkm, tn), F32)],
        compiler_params=_params("parallel", "parallel", "arbitrary"))(a, d)
    return out.reshape((N_CHIPS,) + tuple(shard_shape))


def _row_spec(tr, w, col=0):
    return pl.BlockSpec((tr, w), lambda i: (i, col))


def _full_spec(shape):
    return pl.BlockSpec(shape, lambda i: tuple(0 for _ in shape))


def _rms_fwd(name, x, gains):
    t, dm = x.shape
    tr = _tile(t, 256, 8)
    n = len(gains)

    def body(x_ref, *refs):
        xv = x_ref[...]
        xn = xv * lax.rsqrt(jnp.mean(xv * xv, axis=-1, keepdims=True) + EPS)
        for g_ref, o_ref in zip(refs[:n], refs[n:]):
            o_ref[...] = (xn * g_ref[...]).astype(BF16)

    outs = pl.pallas_call(
        body, name=name, grid=(t // tr,),
        in_specs=[_row_spec(tr, dm)] + [_full_spec((1, dm))] * n,
        out_specs=[_row_spec(tr, dm)] * n,
        out_shape=[jax.ShapeDtypeStruct((t, dm), BF16)] * n,
        compiler_params=_params("parallel"))(x, *gains)
    return list(outs)


def _rms_bwd(name, x, resid, pairs):
    t, dm = x.shape
    tr = _tile(t, 256, 8)
    n = len(pairs)

    def body(x_ref, r_ref, *refs):
        ins, outs = refs[:2 * n], refs[2 * n:]
        i = pl.program_id(0)
        xv = x_ref[...]
        rs = lax.rsqrt(jnp.mean(xv * xv, axis=-1, keepdims=True) + EPS)
        xn = xv * rs
        total = r_ref[...]
        for kx in range(n):
            g_ref, du_ref = ins[2 * kx], ins[2 * kx + 1]
            dg_ref = outs[1 + kx]
            du = du_ref[...]

            @pl.when(i == 0)
            def _():
                dg_ref[...] = jnp.zeros_like(dg_ref)

            dg_ref[...] += _colsum(du * xn)
            dxh = du * g_ref[...]
            total = total + rs * (dxh - xn * jnp.mean(dxh * xn, axis=-1, keepdims=True))
        outs[0][...] = total

    in_specs = [_row_spec(tr, dm), _row_spec(tr, dm)]
    args = [x, resid]
    for g, du in pairs:
        in_specs += [_full_spec((1, dm)), _row_spec(tr, dm)]
        args += [g, du]
    outs = pl.pallas_call(
        body, name=name, grid=(t // tr,), in_specs=in_specs,
        out_specs=[_row_spec(tr, dm)] + [_full_spec((1, dm))] * n,
        out_shape=[jax.ShapeDtypeStruct((t, dm), F32)] + [jax.ShapeDtypeStruct((1, dm), F32)] * n,
        compiler_params=_params("arbitrary"))(*args)
    return outs[0], list(outs[1:])


def _shifted_copies(ext, sh, rows):
    for s in range(1, SUBLANES):
        sh[s - 1] = ext[pl.ds(s, rows), :]


def _window(ext, sh, off, row0, rows, lanes):
    s = off % SUBLANES
    src = ext if s == 0 else sh.at[s - 1]
    return src[pl.ds(off - s + row0, rows), lanes]


def _mixa_fwd(name, pa, cw, cb, lg, lb, seq):
    t, w3 = pa.shape
    cc = w3 // 3
    tr = _tile(seq, 128, HALO)
    per_seq = seq // tr
    hb = tr // HALO
    lead = HALO - (CONV_WIDTH - 1)

    def body(a_ref, b_ref, z_ref, ah_ref, bh_ref, cw_ref, cb_ref, lg_ref, lb_ref, c_ref, m_ref, ext, sh):
        i = pl.program_id(0)
        gh = ah_ref[...].astype(F32) * _sigmoid(bh_ref[...].astype(F32))
        ext[pl.ds(0, HALO), :] = jnp.where((i % per_seq) == 0, 0.0, gh)
        ext[pl.ds(HALO, tr), :] = a_ref[...].astype(F32) * _sigmoid(b_ref[...].astype(F32))
        _shifted_copies(ext, sh, tr + HALO - SUBLANES)
        for lc in range(cc // 128):
            lanes = pl.ds(lc * 128, 128)
            taps = [cw_ref[pl.ds(k, 1), lanes] for k in range(CONV_WIDTH)]
            for row0 in range(0, tr, CONV_ROWS):
                acc = jnp.broadcast_to(cb_ref[:, lanes], (CONV_ROWS, 128))
                for k in range(CONV_WIDTH):
                    acc = acc + _window(ext, sh, lead + k, row0, CONV_ROWS, lanes) * taps[k]
                c_ref[pl.ds(row0, CONV_ROWS), lanes] = acc
        acc = c_ref[...]
        xc = acc - jnp.mean(acc, axis=-1, keepdims=True)
        nrm = xc * lax.rsqrt(jnp.mean(xc * xc, axis=-1, keepdims=True) + EPS)
        l = nrm * lg_ref[...] + lb_ref[...]
        z = z_ref[...].astype(F32)
        m_ref[...] = (l * _sigmoid(l) * z * _sigmoid(z)).astype(BF16)

    halo = lambda col: pl.BlockSpec((HALO, cc), lambda i: (jnp.maximum(i * hb - 1, 0), col))
    return pl.pallas_call(
        body, name=name, grid=(t // tr,),
        in_specs=[_row_spec(tr, cc, 0), _row_spec(tr, cc, 1), _row_spec(tr, cc, 2), halo(0), halo(1),
                  _full_spec((HALO, cc)), _full_spec((1, cc)), _full_spec((1, cc)), _full_spec((1, cc))],
        out_specs=[_row_spec(tr, cc), _row_spec(tr, cc)],
        out_shape=[jax.ShapeDtypeStruct((t, cc), F32), jax.ShapeDtypeStruct((t, cc), BF16)],
        scratch_shapes=[pltpu.VMEM((tr + HALO, cc), F32), pltpu.VMEM((SUBLANES - 1, tr + HALO - SUBLANES, cc), F32)],
        compiler_params=_params("parallel"))(pa, pa, pa, pa, pa, cw, cb, lg, lb)


def _mixa_bwd1(name, c, pa, dm, lg, lb):
    t, cc = c.shape
    tr = _tile(t, 128, 8)

    def body(c_ref, z_ref, dm_ref, lg_ref, lb_ref, dc_ref, dz_ref, dlg_ref, dlb_ref, dcb_ref):
        i = pl.program_id(0)
        cv = c_ref[...]
        xc = cv - jnp.mean(cv, axis=-1, keepdims=True)
        rs = lax.rsqrt(jnp.mean(xc * xc, axis=-1, keepdims=True) + EPS)
        nrm = xc * rs
        l = nrm * lg_ref[...] + lb_ref[...]
        z = z_ref[...].astype(F32)
        sl, sz = _sigmoid(l), _sigmoid(z)
        dmv = dm_ref[...]
        ds = dmv * (z * sz)
        dzz = dmv * (l * sl)
        dz_ref[...] = (dzz * (sz * (1.0 + z * (1.0 - sz)))).astype(BF16)
        dl = ds * (sl * (1.0 + l * (1.0 - sl)))
        dn = dl * lg_ref[...]
        dc = rs * (dn - jnp.mean(dn, axis=-1, keepdims=True) - nrm * jnp.mean(dn * nrm, axis=-1, keepdims=True))
        dc_ref[...] = dc

        @pl.when(i == 0)
        def _():
            dlg_ref[...] = jnp.zeros_like(dlg_ref)
            dlb_ref[...] = jnp.zeros_like(dlb_ref)
            dcb_ref[...] = jnp.zeros_like(dcb_ref)

        dlg_ref[...] += _colsum(dl * nrm)
        dlb_ref[...] += _colsum(dl)
        dcb_ref[...] += _colsum(dc)

    vec = jax.ShapeDtypeStruct((1, cc), F32)
    return pl.pallas_call(
        body, name=name, grid=(t // tr,),
        in_specs=[_row_spec(tr, cc), _row_spec(tr, cc, 2), _row_spec(tr, cc), _full_spec((1, cc)), _full_spec((1, cc))],
        out_specs=[_row_spec(tr, cc), _row_spec(tr, cc)] + [_full_spec((1, cc))] * 3,
        out_shape=[jax.ShapeDtypeStruct((t, cc), F32), jax.ShapeDtypeStruct((t, cc), BF16), vec, vec, vec],
        compiler_params=_params("arbitrary"))(c, pa, dm, lg, lb)


def _mixa_bwd2(name, dc, pa, dz, cw, seq):
    t, cc = dc.shape
    tr = _tile(seq, 128, HALO)
    per_seq = seq // tr
    hb = tr // HALO
    steps = t // tr
    last_halo = t // HALO - 1

    def body(dc_ref, dcn_ref, a_ref, b_ref, dz_ref, cw_ref, dp_ref, dcw_ref, ext, sh, sums, glu_s, dglu_s):
        i = pl.program_id(0)
        ext[pl.ds(0, tr), :] = dc_ref[...]
        ext[pl.ds(tr, HALO), :] = jnp.where((i % per_seq) == per_seq - 1, 0.0, dcn_ref[...])
        _shifted_copies(ext, sh, tr + HALO - SUBLANES)

        @pl.when(i == 0)
        def _():
            sums[...] = jnp.zeros_like(sums)
            dcw_ref[...] = jnp.zeros_like(dcw_ref)

        av = a_ref[...].astype(F32)
        sb = _sigmoid(b_ref[...].astype(F32))
        glu_s[...] = av * sb
        for lc in range(cc // 128):
            lanes = pl.ds(lc * 128, 128)
            taps = [cw_ref[pl.ds(k, 1), lanes] for k in range(CONV_WIDTH)]
            for row0 in range(0, tr, CONV_ROWS):
                glu = glu_s[pl.ds(row0, CONV_ROWS), lanes]
                dglu = jnp.zeros((CONV_ROWS, 128), F32)
                for k in range(CONV_WIDTH):
                    wd = _window(ext, sh, CONV_WIDTH - 1 - k, row0, CONV_ROWS, lanes)
                    dglu = dglu + wd * taps[k]
                    sums[pl.ds(k * SUBLANES, SUBLANES), lanes] += (wd * glu).reshape(CONV_ROWS // SUBLANES, SUBLANES, 128).sum(axis=0)
                dglu_s[pl.ds(row0, CONV_ROWS), lanes] = dglu
        dglu = dglu_s[...]
        dp_ref[:, pl.ds(0, cc)] = (dglu * sb).astype(BF16)
        dp_ref[:, pl.ds(cc, cc)] = (dglu * av * sb * (1.0 - sb)).astype(BF16)
        dp_ref[:, pl.ds(2 * cc, cc)] = dz_ref[...]

        @pl.when(i == steps - 1)
        def _():
            for k in range(CONV_WIDTH):
                dcw_ref[pl.ds(k, 1), :] = _colsum(sums[pl.ds(k * SUBLANES, SUBLANES), :])

    nxt = pl.BlockSpec((HALO, cc), lambda i: (jnp.minimum((i + 1) * hb, last_halo), 0))
    return pl.pallas_call(
        body, name=name, grid=(steps,),
        in_specs=[_row_spec(tr, cc), nxt, _row_spec(tr, cc, 0), _row_spec(tr, cc, 1), _row_spec(tr, cc),
                  _full_spec((HALO, cc))],
        out_specs=[_row_spec(tr, 3 * cc), _full_spec((HALO, cc))],
        out_shape=[jax.ShapeDtypeStruct((t, 3 * cc), BF16), jax.ShapeDtypeStruct((HALO, cc), F32)],
        scratch_shapes=[pltpu.VMEM((tr + HALO, cc), F32), pltpu.VMEM((SUBLANES - 1, tr + HALO - SUBLANES, cc), F32),
                        pltpu.VMEM((HALO * SUBLANES, cc), F32), pltpu.VMEM((tr, cc), F32), pltpu.VMEM((tr, cc), F32)],
        compiler_params=_params("arbitrary"))(dc, dc, pa, pa, dz, cw)


def _ple_fwd(name, h, gpre, pp):
    t, dm = h.shape
    tr = _tile(t, 256, 8)

    def body(h_ref, g_ref, p_ref, o_ref):
        o_ref[...] = h_ref[...] + _sigmoid(g_ref[...]) * p_ref[...]

    return pl.pallas_call(
        body, name=name, grid=(t // tr,), in_specs=[_row_spec(tr, dm)] * 3, out_specs=_row_spec(tr, dm),
        out_shape=jax.ShapeDtypeStruct((t, dm), F32), compiler_params=_params("parallel"))(h, gpre, pp)


def _ple_loss(name, h, gpre, pp, target):
    t, dm = h.shape
    tr = _tile(t, 256, 8)

    def body(h_ref, g_ref, p_ref, t_ref, dy_ref, sq_ref):
        i = pl.program_id(0)
        err = h_ref[...] + _sigmoid(g_ref[...]) * p_ref[...] - t_ref[...]
        dy_ref[...] = err * (1.0 / dm)

        @pl.when(i == 0)
        def _():
            sq_ref[...] = jnp.zeros_like(sq_ref)

        sq_ref[...] += jnp.sum(jnp.sum(err * err, axis=1, keepdims=True), axis=0, keepdims=True)

    return pl.pallas_call(
        body, name=name, grid=(t // tr,), in_specs=[_row_spec(tr, dm)] * 4,
        out_specs=[_row_spec(tr, dm), _full_spec((1, 1))],
        out_shape=[jax.ShapeDtypeStruct((t, dm), F32), jax.ShapeDtypeStruct((1, 1), F32)],
        compiler_params=_params("arbitrary"))(h, gpre, pp, target)


def _ple_bwd(name, dy, gpre, pp):
    t, dm = dy.shape
    tr = _tile(t, 256, 8)

    def body(dy_ref, g_ref, p_ref, dg_ref, dp_ref):
        sg = _sigmoid(g_ref[...])
        dyv = dy_ref[...]
        dg_ref[...] = (dyv * p_ref[...] * sg * (1.0 - sg)).astype(BF16)
        dp_ref[...] = (dyv * sg).astype(BF16)

    return pl.pallas_call(
        body, name=name, grid=(t // tr,), in_specs=[_row_spec(tr, dm)] * 3, out_specs=[_row_spec(tr, dm)] * 2,
        out_shape=[jax.ShapeDtypeStruct((t, dm), BF16)] * 2, compiler_params=_params("parallel"))(dy, gpre, pp)


def _rope_tables(seq):
    half = ROPE_DIM // 2
    inv = ROPE_THETA ** (-jnp.arange(half, dtype=F32) * (2.0 / ROPE_DIM))
    ang = jnp.arange(seq).astype(F32)[:, None] * inv[None, :]
    cos, sin = jnp.cos(ang), jnp.sin(ang)
    rest = HEAD_DIM - ROPE_DIM
    one = jnp.ones((seq, rest), F32)
    zero = jnp.zeros((seq, rest), F32)
    zh = jnp.zeros((seq, half), F32)
    tc = jnp.concatenate([cos, cos, one], axis=1)
    ta = jnp.concatenate([-sin, zh, zero], axis=1)
    tb = jnp.concatenate([zh, sin, zero], axis=1)
    return [jnp.tile(tb_, (1, 128 // HEAD_DIM)) for tb_ in (tc, ta, tb)]


def _wide(tab_ref, w):
    return jnp.tile(tab_ref[...], (1, w // 128))


def _hnr_fwd(name, src, width, gain, tables, seq):
    t = src.shape[0]
    tr = _tile(seq, 256, 8)
    per_seq = seq // tr

    def body(x_ref, g_ref, tc_ref, ta_ref, tb_ref, o_ref):
        xv = x_ref[...].astype(F32)
        rs = lax.rsqrt(_seg_allsum64(xv * xv) * (1.0 / HEAD_DIM) + EPS)
        y = xv * rs * g_ref[...]
        o_ref[...] = (y * _wide(tc_ref, width) + pltpu.roll(y, width - ROPE_DIM // 2, 1) * _wide(ta_ref, width)
                      + pltpu.roll(y, ROPE_DIM // 2, 1) * _wide(tb_ref, width))

    tab = pl.BlockSpec((tr, 128), lambda i: (i % per_seq, 0))
    return pl.pallas_call(
        body, name=name, grid=(t // tr,),
        in_specs=[_row_spec(tr, width), _full_spec((1, width)), tab, tab, tab],
        out_specs=_row_spec(tr, width), out_shape=jax.ShapeDtypeStruct((t, width), F32),
        compiler_params=_params("parallel"))(src, gain, *tables)


def _hnr_bwd_math(xv, gain, dout, tc, ta, tb, width):
    dy = dout * tc + pltpu.roll(dout * ta, ROPE_DIM // 2, 1) + pltpu.roll(dout * tb, width - ROPE_DIM // 2, 1)
    rs = lax.rsqrt(_seg_allsum64(xv * xv) * (1.0 / HEAD_DIM) + EPS)
    xn = xv * rs
    dyh = dy * gain
    dx = rs * (dyh - xn * (_seg_allsum64(dyh * xn) * (1.0 / HEAD_DIM)))
    return dx, _colsum(dy * xn)


def _q_bwd(name, p1, gain, tables, dqs, dgt, seq):
    t, w4 = p1.shape
    da = w4 // 4
    width = 3 * da
    tr = _tile(seq, 128, 8)
    per_seq = seq // tr

    def body(x_ref, g_ref, tc_ref, ta_ref, tb_ref, d0_ref, d1_ref, d2_ref, dgt_ref, o_ref, dg_ref):
        i = pl.program_id(0)
        dout = jnp.concatenate([d0_ref[...], d1_ref[...], d2_ref[...]], axis=1)
        dx, dg = _hnr_bwd_math(x_ref[...].astype(F32), g_ref[...], dout, _wide(tc_ref, width), _wide(ta_ref, width),
                               _wide(tb_ref, width), width)

        @pl.when(i == 0)
        def _():
            dg_ref[...] = jnp.zeros_like(dg_ref)

        dg_ref[...] += dg
        o_ref[:, pl.ds(0, width)] = dx.astype(BF16)
        o_ref[:, pl.ds(width, da)] = dgt_ref[...]

    tab = pl.BlockSpec((tr, 128), lambda i: (i % per_seq, 0))
    return pl.pallas_call(
        body, name=name, grid=(t // tr,),
        in_specs=[_row_spec(tr, width), _full_spec((1, width)), tab, tab, tab] + [_row_spec(tr, da)] * 4,
        out_specs=[_row_spec(tr, w4), _full_spec((1, width))],
        out_shape=[jax.ShapeDtypeStruct((t, w4), BF16), jax.ShapeDtypeStruct((1, width), F32)],
        compiler_params=_params("arbitrary"))(p1, gain, *tables, *dqs, dgt)


def _k_bwd(name, kv, gain, tables, dks, dvs, seq):
    t, w2 = kv.shape
    da = w2 // 2
    tr = _tile(seq, 256, 8)
    per_seq = seq // tr

    def body(x_ref, g_ref, tc_ref, ta_ref, tb_ref, k0, k1, k2, v0, v1, v2, o_ref, dg_ref):
        i = pl.program_id(0)
        dout = k0[...] + k1[...] + k2[...]
        dx, dg = _hnr_bwd_math(x_ref[...], g_ref[...], dout, _wide(tc_ref, da), _wide(ta_ref, da), _wide(tb_ref, da), da)

        @pl.when(i == 0)
        def _():
            dg_ref[...] = jnp.zeros_like(dg_ref)

        dg_ref[...] += dg
        o_ref[:, pl.ds(0, da)] = dx.astype(BF16)
        o_ref[:, pl.ds(da, da)] = (v0[...] + v1[...] + v2[...]).astype(BF16)

    tab = pl.BlockSpec((tr, 128), lambda i: (i % per_seq, 0))
    return pl.pallas_call(
        body, name=name, grid=(t // tr,),
        in_specs=[_row_spec(tr, da), _full_spec((1, da)), tab, tab, tab] + [_row_spec(tr, da)] * 6,
        out_specs=[_row_spec(tr, w2), _full_spec((1, da))],
        out_shape=[jax.ShapeDtypeStruct((t, w2), BF16), jax.ShapeDtypeStruct((1, da), F32)],
        compiler_params=_params("arbitrary"))(kv, gain, *tables, *dks, *dvs)


def _unit_rows(ref, dil, r, blk):
    start = r + dil * SPAN * blk
    if dil == 1:
        return ref[pl.ds(start, SPAN), :]
    return ref[pl.ds(start, SPAN, stride=dil), :]


def _store_rows(ref, dil, r, blk, val):
    start = r + dil * SPAN * blk
    if dil == 1:
        ref[pl.ds(start, SPAN), :] = val
    else:
        ref[pl.ds(start, SPAN, stride=dil), :] = val


def _band_mask(with_prev):
    nk = 2 * SPAN if with_prev else SPAN
    qi = lax.broadcasted_iota(jnp.int32, (SPAN, nk), 0)
    kj = lax.broadcasted_iota(jnp.int32, (SPAN, nk), 1)
    if with_prev:
        return (kj >= qi) & (kj <= qi + SPAN)
    return kj <= qi


_NT = (((1,), (1,)), ((), ()))
_TN = (((0,), (0,)), ((), ()))


def _attn_fwd(name, qn, kn, kv, group, seq):
    t, da = kn.shape
    dil = DILATIONS[group]
    nblk = seq // (dil * SPAN)
    hp = da // 128
    scale = HEAD_DIM ** -0.5

    def body(q_ref, k_ref, v_ref, o_ref, l_ref):
        for r in range(dil):
            kc = vc = None
            for blk in range(nblk):
                kp, vp = kc, vc
                q = _unit_rows(q_ref, dil, r, blk) * scale
                kc = _unit_rows(k_ref, dil, r, blk)
                vc = _unit_rows(v_ref, dil, r, blk)
                with_prev = blk > 0
                kcat = jnp.concatenate([kp, kc], axis=0) if with_prev else kc
                vcat = jnp.concatenate([vp, vc], axis=0) if with_prev else vc
                mask = _band_mask(with_prev)
                outs, lses = [], []
                for hh in range(128 // HEAD_DIM):
                    sl = slice(hh * HEAD_DIM, (hh + 1) * HEAD_DIM)
                    s = lax.dot_general(q[:, sl].astype(BF16), kcat[:, sl].astype(BF16), _NT, preferred_element_type=F32)
                    s = jnp.where(mask, s, NEG_INF)
                    mx = jnp.max(s, axis=-1, keepdims=True)
                    p = jnp.exp(s - mx)
                    den = jnp.sum(p, axis=-1, keepdims=True)
                    o = jnp.dot(p.astype(BF16), vcat[:, sl].astype(BF16), preferred_element_type=F32) / den
                    outs.append(o)
                    lses.append(jnp.broadcast_to(mx + jnp.log(den), (SPAN, HEAD_DIM)))
                _store_rows(o_ref, dil, r, blk, jnp.concatenate(outs, axis=1))
                _store_rows(l_ref, dil, r, blk, jnp.concatenate(lses, axis=1))

    blk_spec = lambda off: pl.BlockSpec((seq, 128), lambda b, h: (b, off + h))
    return pl.pallas_call(
        body, name=name, grid=(t // seq, hp),
        in_specs=[blk_spec(group * hp), blk_spec(0), blk_spec(hp)],
        out_specs=[blk_spec(0), blk_spec(0)],
        out_shape=[jax.ShapeDtypeStruct((t, da), F32)] * 2,
        compiler_params=_params("parallel", "parallel"))(qn, kn, kv)


def _attn_bwd(name, qn, kn, kv, do, lse, dsum, group, seq):
    t, da = kn.shape
    dil = DILATIONS[group]
    nblk = seq // (dil * SPAN)
    hp = da // 128
    scale = HEAD_DIM ** -0.5
    nh = 128 // HEAD_DIM

    def body(q_ref, k_ref, v_ref, do_ref, l_ref, d_ref, dq_ref, dk_ref, dv_ref):
        for r in range(dil):
            kc = vc = None
            pend_k = pend_v = None
            for blk in range(nblk):
                kp, vp = kc, vc
                q = _unit_rows(q_ref, dil, r, blk) * scale
                kc = _unit_rows(k_ref, dil, r, blk)
                vc = _unit_rows(v_ref, dil, r, blk)
                dov = _unit_rows(do_ref, dil, r, blk)
                lrow = _unit_rows(l_ref, dil, r, blk)
                drow = _unit_rows(d_ref, dil, r, blk)
                with_prev = blk > 0
                kcat = jnp.concatenate([kp, kc], axis=0) if with_prev else kc
                vcat = jnp.concatenate([vp, vc], axis=0) if with_prev else vc
                mask = _band_mask(with_prev)
                dqs, dkcs, dvcs = [], [], []
                for hh in range(nh):
                    sl = slice(hh * HEAD_DIM, (hh + 1) * HEAD_DIM)
                    qh = q[:, sl].astype(BF16)
                    doh = dov[:, sl].astype(BF16)
                    kh = kcat[:, sl].astype(BF16)
                    s = lax.dot_general(qh, kh, _NT, preferred_element_type=F32)
                    p = jnp.where(mask, jnp.exp(s - lrow[:, hh * HEAD_DIM:hh * HEAD_DIM + 1]), 0.0)
                    dp = lax.dot_general(doh, vcat[:, sl].astype(BF16), _NT, preferred_element_type=F32)
                    ds = (p * (dp - drow[:, hh * HEAD_DIM:hh * HEAD_DIM + 1])).astype(BF16)
                    dqs.append(jnp.dot(ds, kh, preferred_element_type=F32) * scale)
                    dkcs.append(lax.dot_general(ds, qh, _TN, preferred_element_type=F32))
                    dvcs.append(lax.dot_general(p.astype(BF16), doh, _TN, preferred_element_type=F32))
                _store_rows(dq_ref, dil, r, blk, jnp.concatenate(dqs, axis=1))
                dkcat = jnp.concatenate(dkcs, axis=1)
                dvcat = jnp.concatenate(dvcs, axis=1)
                if with_prev:
                    _store_rows(dk_ref, dil, r, blk - 1, pend_k + dkcat[:SPAN])
                    _store_rows(dv_ref, dil, r, blk - 1, pend_v + dvcat[:SPAN])
                    pend_k, pend_v = dkcat[SPAN:], dvcat[SPAN:]
                else:
                    pend_k, pend_v = dkcat, dvcat
            _store_rows(dk_ref, dil, r, nblk - 1, pend_k)
            _store_rows(dv_ref, dil, r, nblk - 1, pend_v)

    blk_spec = lambda off: pl.BlockSpec((seq, 128), lambda b, h: (b, off + h))
    return pl.pallas_call(
        body, name=name, grid=(t // seq, hp),
        in_specs=[blk_spec(group * hp), blk_spec(0), blk_spec(hp), blk_spec(0), blk_spec(0), blk_spec(0)],
        out_specs=[blk_spec(0)] * 3,
        out_shape=[jax.ShapeDtypeStruct((t, da), F32)] * 3,
        compiler_params=_params("parallel", "parallel"))(qn, kn, kv, do, lse, dsum)


def _attn_merge(name, os_, ls_, p1):
    t, da = os_[0].shape
    tr = _tile(t, 256, 8)

    def body(o0, o1, o2, l0, l1, l2, g_ref, o_ref, l_ref, m_ref):
        a0, a1, a2 = l0[...], l1[...], l2[...]
        mx = jnp.maximum(jnp.maximum(a0, a1), a2)
        e0, e1, e2 = jnp.exp(a0 - mx), jnp.exp(a1 - mx), jnp.exp(a2 - mx)
        den = e0 + e1 + e2
        o = (e0 * o0[...] + e1 * o1[...] + e2 * o2[...]) / den
        o_ref[...] = o
        l_ref[...] = mx + jnp.log(den)
        g = g_ref[...].astype(F32)
        m_ref[...] = (o * g * _sigmoid(g)).astype(BF16)

    return pl.pallas_call(
        body, name=name, grid=(t // tr,),
        in_specs=[_row_spec(tr, da)] * 6 + [_row_spec(tr, da, 3)],
        out_specs=[_row_spec(tr, da)] * 3,
        out_shape=[jax.ShapeDtypeStruct((t, da), F32)] * 2 + [jax.ShapeDtypeStruct((t, da), BF16)],
        compiler_params=_params("parallel"))(*os_, *ls_, p1)


def _gate_bwd(name, dm, o, p1):
    t, da = o.shape
    tr = _tile(t, 256, 8)

    def body(dm_ref, o_ref, g_ref, do_ref, dg_ref, ds_ref):
        g = g_ref[...].astype(F32)
        sg = _sigmoid(g)
        dmv, ov = dm_ref[...], o_ref[...]
        do = dmv * (g * sg)
        do_ref[...] = do
        dg_ref[...] = (dmv * ov * (sg * (1.0 + g * (1.0 - sg)))).astype(BF16)
        ds_ref[...] = _seg_allsum64(do * ov)

    return pl.pallas_call(
        body, name=name, grid=(t // tr,),
        in_specs=[_row_spec(tr, da), _row_spec(tr, da), _row_spec(tr, da, 3)],
        out_specs=[_row_spec(tr, da)] * 3,
        out_shape=[jax.ShapeDtypeStruct((t, da), F32), jax.ShapeDtypeStruct((t, da), BF16), jax.ShapeDtypeStruct((t, da), F32)],
        compiler_params=_params("parallel"))(dm, o, p1)


def _cast_bf16(name, w2d, chip):
    r, c = w2d.shape
    tr = _tile(r, 256, 16)

    def body(chip_ref, x_ref, o_ref):
        o_ref[...] = x_ref[...].astype(BF16)

    grid_spec = pltpu.PrefetchScalarGridSpec(
        num_scalar_prefetch=1, grid=(r // tr,),
        in_specs=[pl.BlockSpec((tr, c), lambda i, m: (i, 0))],
        out_specs=pl.BlockSpec((None, tr, c), lambda i, m: (m[0], i, 0)))
    return pl.pallas_call(
        body, name=name, grid_spec=grid_spec, out_shape=jax.ShapeDtypeStruct((N_CHIPS, r, c), BF16),
        compiler_params=_params("parallel"))(chip, w2d)


def _adamw(name, w, g, m, v):
    r, c = w.shape
    tr = _tile(r, 256, 8)
    c1 = 1.0 - ADAM_B1 ** ADAM_STEP
    c2 = 1.0 - ADAM_B2 ** ADAM_STEP

    def body(w_ref, g_ref, m_ref, v_ref, d_ref, nm_ref, nv_ref):
        gv = g_ref[...]
        nm = ADAM_B1 * m_ref[...] + (1.0 - ADAM_B1) * gv
        nv = ADAM_B2 * v_ref[...] + (1.0 - ADAM_B2) * (gv * gv)
        nm_ref[...] = nm
        nv_ref[...] = nv
        d_ref[...] = -ADAM_LR * ((nm / c1) / (jnp.sqrt(nv / c2) + ADAM_EPS) + ADAM_WD * w_ref[...])

    sds = jax.ShapeDtypeStruct((r, c), F32)
    return pl.pallas_call(
        body, name=name, grid=(r // tr,), in_specs=[_row_spec(tr, c)] * 4, out_specs=[_row_spec(tr, c)] * 3,
        out_shape=[sds] * 3, compiler_params=_params("parallel"))(w, g, m, v)


def _pair_sum(name, gd, recv, core):
    _, r, c = gd.shape
    rh = r // 2
    tr = _tile(rh, 256, 8)
    nrt = rh // tr

    def body(c_ref, a_ref, b_ref, o_ref):
        o_ref[...] = (a_ref[...].astype(F32) + b_ref[...].astype(F32)).astype(BF16)

    grid_spec = pltpu.PrefetchScalarGridSpec(
        num_scalar_prefetch=1, grid=(N_CHIPS, nrt),
        in_specs=[pl.BlockSpec((None, tr, c), lambda j, i, cr: (j, cr[0] * nrt + i, 0)),
                  pl.BlockSpec((None, tr, c), lambda j, i, cr: (j, i, 0))],
        out_specs=pl.BlockSpec((None, tr, c), lambda j, i, cr: (j, i, 0)))
    return pl.pallas_call(
        body, name=name, grid_spec=grid_spec, out_shape=jax.ShapeDtypeStruct((N_CHIPS, rh, c), BF16),
        compiler_params=_params("parallel", "parallel"))(core, gd, recv)


def _chip_sum(name, sums, parts, order):
    _, rh, c = parts.shape
    tr = _tile(rh, 256, 16)
    nrt = rh // tr

    def body(o_ref_, s_ref, p1_ref, p2_ref, p3_ref, o_ref):
        acc = s_ref[...].astype(F32)
        for p_ref in (p1_ref, p2_ref, p3_ref):
            acc = acc + p_ref[...].astype(F32)
        o_ref[...] = acc

    slot = lambda k: pl.BlockSpec((None, tr, c), lambda i, o: (o[k], i, 0))
    grid_spec = pltpu.PrefetchScalarGridSpec(
        num_scalar_prefetch=1, grid=(nrt,),
        in_specs=[slot(0), slot(1), slot(2), slot(3)],
        out_specs=pl.BlockSpec((tr, c), lambda i, o: (o[N_CHIPS] * nrt + i, 0)))
    return pl.pallas_call(
        body, name=name, grid_spec=grid_spec, out_shape=jax.ShapeDtypeStruct((2 * rh, c), F32),
        compiler_params=_params("parallel"))(order, sums, parts, parts, parts)


HBM = pl.BlockSpec(memory_space=pl.ANY)


def _place():
    x, y, c = lax.axis_index("x"), lax.axis_index("y"), lax.axis_index("c")
    chips = [(1 - x, y), (x, 1 - y), (1 - x, 1 - y)]
    return x, y, c, chips


def _half(ref, hc):
    rows = ref.shape[0] // 2
    return ref.at[pl.ds(hc * rows, rows)]


def _gather_weights(shards):
    n = len(shards)

    def body(*refs):
        outs = refs[n:2 * n]
        send_sems, recv_sems = refs[2 * n:]
        x, y, c, chips = _place()
        mine = 2 * x + y
        first, passed = [], []
        for i in range(n):
            for k, chip in enumerate(chips):
                cp = pltpu.make_async_remote_copy(
                    src_ref=_half(outs[i].at[mine], c), dst_ref=_half(outs[i].at[mine], c),
                    send_sem=send_sems.at[6 * i + k], recv_sem=recv_sems.at[6 * i + k],
                    device_id=(chip[0], chip[1], c), device_id_type=MESH)
                cp.start()
                first.append(cp)
        for i in range(n):
            for k, chip in enumerate(chips):
                theirs = 2 * chip[0] + chip[1]
                landed = _half(outs[i].at[theirs], c)
                pltpu.make_async_remote_copy(
                    src_ref=landed, dst_ref=landed, send_sem=send_sems.at[6 * i + k], recv_sem=recv_sems.at[6 * i + k],
                    device_id=(chip[0], chip[1], c), device_id_type=MESH).wait_recv()
                cp = pltpu.make_async_remote_copy(
                    src_ref=landed, dst_ref=landed, send_sem=send_sems.at[6 * i + 3 + k], recv_sem=recv_sems.at[6 * i + 3 + k],
                    device_id=(x, y, 1 - c), device_id_type=MESH)
                cp.start()
                passed.append(cp)
        for i in range(n):
            for k, chip in enumerate(chips):
                theirs = 2 * chip[0] + chip[1]
                other = _half(outs[i].at[theirs], 1 - c)
                pltpu.make_async_remote_copy(
                    src_ref=other, dst_ref=other, send_sem=send_sems.at[6 * i + 3 + k], recv_sem=recv_sems.at[6 * i + 3 + k],
                    device_id=(x, y, 1 - c), device_id_type=MESH).wait_recv()
        for cp in first + passed:
            cp.wait_send()

    return pl.pallas_call(
        body, name="gather_weights", in_specs=[HBM] * n, out_specs=[HBM] * n,
        out_shape=[jax.ShapeDtypeStruct(s.shape, s.dtype) for s in shards],
        input_output_aliases={i: i for i in range(n)},
        scratch_shapes=[pltpu.SemaphoreType.DMA((6 * n,)), pltpu.SemaphoreType.DMA((6 * n,))],
        )(*shards)


def _pair_exchange(grads):
    n = len(grads)

    def body(*refs):
        ins, outs = refs[:n], refs[n:2 * n]
        send_sems, recv_sems = refs[2 * n:]
        x, y, c, _ = _place()
        cps = []
        for i in range(n):
            rows = ins[i].shape[1] // 2
            cp = pltpu.make_async_remote_copy(
                src_ref=ins[i].at[:, pl.ds((1 - c) * rows, rows), :], dst_ref=outs[i],
                send_sem=send_sems.at[i], recv_sem=recv_sems.at[i], device_id=(x, y, 1 - c), device_id_type=MESH)
            cp.start()
            cps.append(cp)
        for cp in cps:
            cp.wait()

    return pl.pallas_call(
        body, name="pair_exchange", in_specs=[HBM] * n, out_specs=[HBM] * n,
        out_shape=[jax.ShapeDtypeStruct((N_CHIPS, g.shape[1] // 2, g.shape[2]), g.dtype) for g in grads],
        scratch_shapes=[pltpu.SemaphoreType.DMA((n,)), pltpu.SemaphoreType.DMA((n,))],
        )(*grads)


def _chip_scatter(sums):
    n = len(sums)

    def body(*refs):
        ins, outs = refs[:n], refs[n:2 * n]
        send_sems, recv_sems = refs[2 * n:]
        x, y, c, chips = _place()
        mine = 2 * x + y
        cps = []
        for i in range(n):
            for k, chip in enumerate(chips):
                theirs = 2 * chip[0] + chip[1]
                cp = pltpu.make_async_remote_copy(
                    src_ref=ins[i].at[theirs], dst_ref=outs[i].at[mine],
                    send_sem=send_sems.at[3 * i + k], recv_sem=recv_sems.at[3 * i + k],
                    device_id=(chip[0], chip[1], c), device_id_type=MESH)
                cp.start()
                cps.append((cp, i, k, theirs))
        for cp, i, k, theirs in cps:
            cp.wait_send()
            pltpu.make_async_remote_copy(
                src_ref=ins[i].at[theirs], dst_ref=outs[i].at[theirs],
                send_sem=send_sems.at[3 * i + k], recv_sem=recv_sems.at[3 * i + k],
                device_id=(x, y, c), device_id_type=MESH).wait_recv()

    return pl.pallas_call(
        body, name="chip_scatter", in_specs=[HBM] * n, out_specs=[HBM] * n,
        out_shape=[jax.ShapeDtypeStruct(s.shape, s.dtype) for s in sums],
        scratch_shapes=[pltpu.SemaphoreType.DMA((3 * n,)), pltpu.SemaphoreType.DMA((3 * n,))],
        )(*sums)


def _sibling_join(grads):
    n = len(grads)

    def body(*refs):
        outs = refs[n:2 * n]
        send_sems, recv_sems = refs[2 * n:]
        x, y, c, _ = _place()
        cps = []
        for i in range(n):
            cp = pltpu.make_async_remote_copy(
                src_ref=_half(outs[i], c), dst_ref=_half(outs[i], c), send_sem=send_sems.at[i], recv_sem=recv_sems.at[i],
                device_id=(x, y, 1 - c), device_id_type=MESH)
            cp.start()
            cps.append(cp)
        for i, cp in enumerate(cps):
            cp.wait_send()
            pltpu.make_async_remote_copy(
                src_ref=_half(outs[i], 1 - c), dst_ref=_half(outs[i], 1 - c), send_sem=send_sems.at[i],
                recv_sem=recv_sems.at[i], device_id=(x, y, 1 - c), device_id_type=MESH).wait_recv()

    return pl.pallas_call(
        body, name="sibling_join", in_specs=[HBM] * n, out_specs=[HBM] * n,
        out_shape=[jax.ShapeDtypeStruct(g.shape, g.dtype) for g in grads],
        input_output_aliases={i: i for i in range(n)},
        scratch_shapes=[pltpu.SemaphoreType.DMA((n,)), pltpu.SemaphoreType.DMA((n,))],
        )(*grads)


def _gather8(name, block, reduce):
    m, n = block.shape

    def body(x_ref, out_ref, *scratch):
        if reduce:
            all_ref, send_sems, recv_sems, local_sem = scratch
        else:
            all_ref = out_ref
            send_sems, recv_sems, local_sem = scratch
        x, y, c, chips = _place()
        me, sibling = (x, y, c), (x, y, 1 - c)

        def rows(px, py, pc):
            return all_ref.at[pl.ds((4 * px + 2 * py + pc) * m, m), :]

        def copy(k, blk, to, src=None):
            return pltpu.make_async_remote_copy(
                src_ref=rows(*blk) if src is None else src, dst_ref=rows(*blk),
                send_sem=send_sems.at[k], recv_sem=recv_sems.at[k], device_id=to, device_id_type=MESH)

        mine = pltpu.make_async_copy(x_ref, rows(*me), local_sem)
        mine.start()
        first = [copy(0, me, sibling, src=x_ref)]
        first += [copy(1 + j, me, (chip[0], chip[1], c), src=x_ref) for j, chip in enumerate(chips)]
        for cp in first:
            cp.start()
        passed = [copy(4 + j, (chip[0], chip[1], c), sibling) for j, chip in enumerate(chips)]
        for j, chip in enumerate(chips):
            copy(1 + j, (chip[0], chip[1], c), me).wait_recv()
            passed[j].start()
        copy(0, sibling, me).wait_recv()
        for j, chip in enumerate(chips):
            copy(4 + j, (chip[0], chip[1], 1 - c), me).wait_recv()
        for cp in first + passed:
            cp.wait_send()
        mine.wait()
        if reduce:
            acc = all_ref[pl.ds(0, m), :]
            for d in range(1, 8):
                acc = acc + all_ref[pl.ds(d * m, m), :]
            out_ref[...] = acc

    sems = [pltpu.SemaphoreType.DMA((7,)), pltpu.SemaphoreType.DMA((7,)), pltpu.SemaphoreType.DMA]
    scratch = ([pltpu.VMEM((8 * m, n), F32)] if reduce else []) + sems
    return pl.pallas_call(
        body, name=name,
        out_shape=jax.ShapeDtypeStruct((m, n) if reduce else (8 * m, n), F32),
        in_specs=[pl.BlockSpec(memory_space=pltpu.VMEM)], out_specs=pl.BlockSpec(memory_space=pltpu.VMEM),
        scratch_shapes=scratch)(block)


def _pad_rows(a, rows):
    return jnp.concatenate([a, jnp.zeros((rows - a.shape[0], a.shape[1]), a.dtype)], axis=0)


def kernel(x, p, norm_g, w_in_a, conv_w, conv_b, ln_g, ln_b, w_out_a, kv_norm_g, w_kv, k_norm_g, w_in_b, q_norm_g, w_out_b, ple_norm_g, w_ple_gate, w_ple_proj, loss_target, m_norm_g, m_w_in_a, m_conv_w, m_conv_b, m_ln_g, m_ln_b, m_w_out_a, m_kv_norm_g, m_w_kv, m_k_norm_g, m_w_in_b, m_q_norm_g, m_w_out_b, m_ple_norm_g, m_w_ple_gate, m_w_ple_proj, v_norm_g, v_w_in_a, v_conv_w, v_conv_b, v_ln_g, v_ln_b, v_w_out_a, v_kv_norm_g, v_w_kv, v_k_norm_g, v_w_in_b, v_q_norm_g, v_w_out_b, v_ple_norm_g, v_w_ple_gate, v_w_ple_proj):
    nb, seq, dm = x.shape
    t = nb * seq
    ple = p.shape[-1]
    ccs = conv_w.shape[-1]
    cc = N_CHIPS * ccs
    da = dm
    nheads = da // HEAD_DIM
    assert seq == DILATIONS[-1] * SPAN and da % 128 == 0 and ccs % 128 == 0

    core = lax.axis_index("c").astype(jnp.int32).reshape(1)
    chip = (2 * lax.axis_index("x") + lax.axis_index("y")).astype(jnp.int32)
    chip1 = chip.reshape(1)
    sum_order = jnp.concatenate([(chip1 + k) % N_CHIPS for k in range(N_CHIPS)] + [core])

    x2 = x.reshape(t, dm)
    tgt2 = loss_target.reshape(t, dm)
    p0 = p[0].reshape(t, ple)
    p1 = p[1].reshape(t, ple)

    big = [
        ("w_in_a", w_in_a[0], "col"), ("w_out_a", w_out_a[0], "row"), ("w_kv", w_kv, "col"),
        ("w_in_b", w_in_b[0], "col"), ("w_out_b", w_out_b[0], "row"),
        ("w_ple_gate0", w_ple_gate[0], "row"), ("w_ple_gate1", w_ple_gate[1], "row"),
        ("w_ple_proj0", w_ple_proj[0], "col"), ("w_ple_proj1", w_ple_proj[1], "col"),
    ]
    shard_shape = {nm: w.shape for nm, w, _ in big}
    gathered = _gather_weights([_cast_bf16("cast_" + nm, w, chip1) for nm, w, _ in big])
    W = {nm: g for (nm, _, _), g in zip(big, gathered)}

    vec_rows = 40
    small = _pad_rows(jnp.concatenate([conv_w[0], conv_b, ln_g, ln_b], axis=0), vec_rows)
    allv = _gather8("gather_conv_vectors", small, reduce=False).reshape(N_CHIPS, 2, vec_rows, ccs)[:, 0]
    allv = allv.transpose(1, 0, 2).reshape(vec_rows, cc)
    cw_full, cb_full, lg_full, lb_full = allv[:HALO], allv[31:32], allv[32:33], allv[33:34]
    cw_full = cw_full * (lax.broadcasted_iota(jnp.int32, (HALO, 1), 0) < CONV_WIDTH).astype(F32)

    tables = _rope_tables(seq)
    gain_q = jnp.tile(q_norm_g[0][:, None, :], (1, nheads, 1)).reshape(1, 3 * da)
    gain_k = jnp.tile(k_norm_g[None, :], (1, nheads))
    g0, g1 = norm_g[0:1], norm_g[1:2]
    pg0, pg1 = ple_norm_g[0:1], ple_norm_g[1:2]
    kvg = kv_norm_g[None, :]

    (u0,) = _rms_fwd("rms_u0", x2, [g0])
    pa = _mm_nn("mm_in_a", u0, W["w_in_a"], "col", out_dtype=BF16)
    conv_out, m_a = _mixa_fwd("mixa_fwd", pa, cw_full, cb_full, lg_full, lb_full, seq)
    h0 = _mm_nn("mm_out_a", m_a, W["w_out_a"], "row", resid=x2)
    (r0,) = _rms_fwd("rms_r0", h0, [pg0])
    gpre0 = _mm_nn("mm_gate0", r0, W["w_ple_gate0"], "row")
    pp0 = _mm_nn("mm_proj0", p0, W["w_ple_proj0"], "col")
    x1 = _ple_fwd("ple_fwd0", h0, gpre0, pp0)
    kvn, u1 = _rms_fwd("rms_kv_u1", x1, [kvg, g1])
    kv = _mm_nn("mm_kv", kvn, W["w_kv"], "col")
    kn = _hnr_fwd("k_norm_rope", kv, da, gain_k, tables, seq)
    pb = _mm_nn("mm_in_b", u1, W["w_in_b"], "col", out_dtype=BF16)
    qn = _hnr_fwd("q_norm_rope", pb, 3 * da, gain_q, tables, seq)
    og, lg_ = [], []
    for g in range(3):
        o_g, l_g = _attn_fwd("attn_fwd%d" % g, qn, kn, kv, g, seq)
        og.append(o_g)
        lg_.append(l_g)
    o, lse, m_b = _attn_merge("attn_merge", og, lg_, pb)
    h1 = _mm_nn("mm_out_b", m_b, W["w_out_b"], "row", resid=x1)
    (r1,) = _rms_fwd("rms_r1", h1, [pg1])
    gpre1 = _mm_nn("mm_gate1", r1, W["w_ple_gate1"], "row")
    pp1 = _mm_nn("mm_proj1", p1, W["w_ple_proj1"], "col")
    dy, sq = _ple_loss("ple_loss", h1, gpre1, pp1, tgt2)
    loss = lax.psum(0.5 * sq[0, 0] / dm, ("x", "y", "c"))

    G = {}
    dgp1, dpp1 = _ple_bwd("ple_bwd1", dy, gpre1, pp1)
    G["w_ple_gate1"] = _mm_tn("tn_gate1", r1, dgp1, "row", shard_shape["w_ple_gate1"])
    G["w_ple_proj1"] = _mm_tn("tn_proj1", p1, dpp1, "col", shard_shape["w_ple_proj1"])
    dr1 = _mm_nt("nt_gate1", dgp1, W["w_ple_gate1"], "row")
    dh1, (dpg1,) = _rms_bwd("rms_bwd_r1", h1, dy, [(pg1, dr1)])
    G["w_out_b"] = _mm_tn("tn_out_b", m_b, dh1, "row", shard_shape["w_out_b"])
    dm_b = _mm_nt("nt_out_b", dh1, W["w_out_b"], "row")
    d_o, dgt, dsum = _gate_bwd("gate_bwd", dm_b, o, pb)
    dqs, dks, dvs = [], [], []
    for g in range(3):
        dq_g, dk_g, dv_g = _attn_bwd("attn_bwd%d" % g, qn, kn, kv, d_o, lse, dsum, g, seq)
        dqs.append(dq_g)
        dks.append(dk_g)
        dvs.append(dv_g)
    dpb, dgq = _q_bwd("q_bwd", pb, gain_q, tables, dqs, dgt, seq)
    dkv, dgk = _k_bwd("k_bwd", kv, gain_k, tables, dks, dvs, seq)
    G["w_in_b"] = _mm_tn("tn_in_b", u1, dpb, "col", shard_shape["w_in_b"])
    du1 = _mm_nt("nt_in_b", dpb, W["w_in_b"], "col")
    G["w_kv"] = _mm_tn("tn_kv", kvn, dkv, "col", shard_shape["w_kv"])
    dkvn = _mm_nt("nt_kv", dkv, W["w_kv"], "col")
    dx1, (dg1, dkvg) = _rms_bwd("rms_bwd_x1", x1, dh1, [(g1, du1), (kvg, dkvn)])
    dgp0, dpp0 = _ple_bwd("ple_bwd0", dx1, gpre0, pp0)
    G["w_ple_gate0"] = _mm_tn("tn_gate0", r0, dgp0, "row", shard_shape["w_ple_gate0"])
    G["w_ple_proj0"] = _mm_tn("tn_proj0", p0, dpp0, "col", shard_shape["w_ple_proj0"])
    dr0 = _mm_nt("nt_gate0", dgp0, W["w_ple_gate0"], "row")
    dh0, (dpg0,) = _rms_bwd("rms_bwd_r0", h0, dx1, [(pg0, dr0)])
    G["w_out_a"] = _mm_tn("tn_out_a", m_a, dh0, "row", shard_shape["w_out_a"])
    dm_a = _mm_nt("nt_out_a", dh0, W["w_out_a"], "row")
    dc, dz, dlg, dlb, dcb = _mixa_bwd1("mixa_bwd1", conv_out, pa, dm_a, lg_full, lb_full)
    dpa, dcw = _mixa_bwd2("mixa_bwd2", dc, pa, dz, cw_full, seq)
    G["w_in_a"] = _mm_tn("tn_in_a", u0, dpa, "col", shard_shape["w_in_a"])
    du0 = _mm_nt("nt_in_a", dpa, W["w_in_a"], "col")
    dx, (dg0,) = _rms_bwd("rms_bwd_x", x2, dh0, [(g0, du0)])
    grad_x = dx.reshape(nb, seq, dm)

    names = [nm for nm, _, _ in big]
    recv = _pair_exchange([G[nm] for nm in names])
    sums = [_pair_sum("pair_sum_" + nm, G[nm], rc, core) for nm, rc in zip(names, recv)]
    parts = _chip_scatter(sums)
    halves = [_chip_sum("chip_sum_" + nm, sm, pt, sum_order) for nm, sm, pt in zip(names, sums, parts)]
    gfull = dict(zip(names, _sibling_join(halves)))

    def as_rows(a):
        return a.reshape(-1, dm)

    small_parts = [as_rows(dcw), as_rows(dcb), as_rows(dlg), as_rows(dlb), dg0, dg1, dkvg, dpg0, dpg1, as_rows(dgk), as_rows(dgq)]
    counts = [a.shape[0] for a in small_parts]
    total = sum(counts)
    packed = _pad_rows(jnp.concatenate(small_parts, axis=0), -(-total // 8) * 8)
    red = _gather8("reduce_small", packed, reduce=True)
    pieces, off = [], 0
    for n_ in counts:
        pieces.append(red[off:off + n_])
        off += n_
    r_dcw, r_dcb, r_dlg, r_dlb, r_g0, r_g1, r_kvg, r_pg0, r_pg1, r_gk, r_gq = pieces
    my_cols = lambda a: lax.dynamic_slice_in_dim(a.reshape(-1, cc), chip * ccs, ccs, axis=1)
    small_grads = {
        "norm_g": jnp.concatenate([r_g0, r_g1], axis=0),
        "conv_w": my_cols(r_dcw)[:CONV_WIDTH],
        "conv_b": my_cols(r_dcb), "ln_g": my_cols(r_dlg), "ln_b": my_cols(r_dlb),
        "kv_norm_g": r_kvg,
        "k_norm_g": r_gk.reshape(nheads, HEAD_DIM).sum(axis=0, keepdims=True),
        "q_norm_g": r_gq.reshape(3, nheads, HEAD_DIM).sum(axis=1),
        "ple_norm_g": jnp.concatenate([r_pg0, r_pg1], axis=0),
    }

    given = dict(norm_g=norm_g, w_in_a=w_in_a, conv_w=conv_w, conv_b=conv_b, ln_g=ln_g, ln_b=ln_b, w_out_a=w_out_a,
                 kv_norm_g=kv_norm_g, w_kv=w_kv, k_norm_g=k_norm_g, w_in_b=w_in_b, q_norm_g=q_norm_g, w_out_b=w_out_b,
                 ple_norm_g=ple_norm_g, w_ple_gate=w_ple_gate, w_ple_proj=w_ple_proj)
    mom1 = dict(norm_g=m_norm_g, w_in_a=m_w_in_a, conv_w=m_conv_w, conv_b=m_conv_b, ln_g=m_ln_g, ln_b=m_ln_b,
                w_out_a=m_w_out_a, kv_norm_g=m_kv_norm_g, w_kv=m_w_kv, k_norm_g=m_k_norm_g, w_in_b=m_w_in_b,
                q_norm_g=m_q_norm_g, w_out_b=m_w_out_b, ple_norm_g=m_ple_norm_g, w_ple_gate=m_w_ple_gate,
                w_ple_proj=m_w_ple_proj)
    mom2 = dict(norm_g=v_norm_g, w_in_a=v_w_in_a, conv_w=v_conv_w, conv_b=v_conv_b, ln_g=v_ln_g, ln_b=v_ln_b,
                w_out_a=v_w_out_a, kv_norm_g=v_kv_norm_g, w_kv=v_w_kv, k_norm_g=v_k_norm_g, w_in_b=v_w_in_b,
                q_norm_g=v_q_norm_g, w_out_b=v_w_out_b, ple_norm_g=v_ple_norm_g, w_ple_gate=v_w_ple_gate,
                w_ple_proj=v_w_ple_proj)
    order = ["norm_g", "w_in_a", "conv_w", "conv_b", "ln_g", "ln_b", "w_out_a", "kv_norm_g", "w_kv", "k_norm_g", "w_in_b",
             "q_norm_g", "w_out_b", "ple_norm_g", "w_ple_gate", "w_ple_proj"]
    grads, deltas, new_m, new_v = {}, {}, {}, {}
    for nm in order:
        shape = given[nm].shape
        if nm in ("w_ple_gate", "w_ple_proj"):
            g2 = jnp.concatenate([gfull[nm + "0"], gfull[nm + "1"]], axis=0)
        elif nm in gfull:
            g2 = gfull[nm]
        else:
            g2 = small_grads[nm]
        two_d = g2.shape
        d2, m2, v2 = _adamw("adamw_" + nm, given[nm].reshape(two_d), g2, mom1[nm].reshape(two_d), mom2[nm].reshape(two_d))
        grads[nm], deltas[nm], new_m[nm], new_v[nm] = (a.reshape(shape) for a in (g2, d2, m2, v2))

    return (loss, grad_x, *[grads[n_] for n_ in order], *[deltas[n_] for n_ in order],
            *[new_m[n_] for n_ in order], *[new_v[n_] for n_ in order])
```

```python
import functools

import jax
import jax.numpy as jnp
from jax import lax
from jax.experimental import pallas as pl
from jax.experimental.pallas import tpu as pltpu

F32 = jnp.float32
BF16 = jnp.bfloat16
MESH = pl.DeviceIdType.MESH

EPS = 1e-6
NEG_INF = -1e30
HEAD_DIM = 64
ROPE_DIM = 16
ROPE_THETA = 500000.0
CONV_WIDTH = 31
SUBLANES = 8
CONV_ROWS = 64
HALO = 32
SPAN = 128
DILATIONS = (1, 4, 16)
ADAM_LR, ADAM_B1, ADAM_B2, ADAM_EPS, ADAM_WD, ADAM_STEP = 0.001, 0.9, 0.999, 1e-08, 0.01, 10
N_CHIPS = 4
VMEM_LIMIT = 56 * 1024 * 1024


def _tile(n, target, mult=128):
    best = None
    t = mult
    while t <= min(n, target):
        if n % t == 0:
            best = t
        t += mult
    return best if best is not None else n


def _params(*sem):
    return pltpu.CompilerParams(dimension_semantics=tuple(sem) if sem else None, vmem_limit_bytes=VMEM_LIMIT)


def _sigmoid(x):
    return 1.0 / (1.0 + jnp.exp(-x))


def _seg_allsum64(x):
    tr, w = x.shape
    cw = 256 if w % 256 == 0 else 128
    n = w // cw
    ri = lax.shift_right_logical(lax.broadcasted_iota(jnp.int32, (cw, cw), 0), 6)
    ci = lax.shift_right_logical(lax.broadcasted_iota(jnp.int32, (cw, cw), 1), 6)
    ones = (ri == ci).astype(BF16)
    hi = x.astype(BF16)
    lo = (x - hi.astype(F32)).astype(BF16)

    def stack(v):
        return jnp.concatenate([v[:, j * cw:(j + 1) * cw] for j in range(n)], axis=0)

    s = (jnp.dot(stack(hi), ones, preferred_element_type=F32)
         + jnp.dot(stack(lo), ones, preferred_element_type=F32))
    return jnp.concatenate([s[j * tr:(j + 1) * tr] for j in range(n)], axis=1)


def _colsum(x):
    return jnp.sum(x, axis=0, keepdims=True)


def _shards_view(w, kind):
    return w if kind == "col" else w.reshape(1, -1, w.shape[2])


def _mm_nn(name, a, w, kind, *, out_dtype=F32, resid=None):
    t = a.shape[0]
    w = _shards_view(w, kind)
    ns, k, c = w.shape
    n = ns * c
    tm = _tile(t, 1024, 8)
    tk = _tile(k, 2048)
    tn = _tile(c, 1024)
    nk = k // tk
    per = c // tn

    def body(*refs):
        if resid is None:
            a_ref, w_ref, o_ref = refs[:3]
        else:
            a_ref, w_ref, r_ref, o_ref = refs[:4]
        part = jnp.dot(a_ref[...].astype(BF16), w_ref[...], preferred_element_type=F32)

        def finish(out):
            if resid is not None:
                out = out + r_ref[...]
            o_ref[...] = out.astype(out_dtype)

        if nk == 1:
            finish(part)
            return
        acc = refs[-1]
        kk = pl.program_id(2)

        @pl.when(kk == 0)
        def _():
            acc[...] = part

        @pl.when(kk > 0)
        def _():
            acc[...] += part

        @pl.when(kk == nk - 1)
        def _():
            finish(acc[...])

    in_specs = [pl.BlockSpec((tm, tk), lambda i, j, kk: (i, kk)),
                pl.BlockSpec((None, tk, tn), lambda i, j, kk: (j // per, kk, j % per))]
    args = [a, w]
    if resid is not None:
        in_specs.append(pl.BlockSpec((tm, tn), lambda i, j, kk: (i, j)))
        args.append(resid)
    return pl.pallas_call(
        body, name=name, grid=(t // tm, n // tn, nk), in_specs=in_specs,
        out_specs=pl.BlockSpec((tm, tn), lambda i, j, kk: (i, j)),
        out_shape=jax.ShapeDtypeStruct((t, n), out_dtype),
        scratch_shapes=[pltpu.VMEM((tm, tn), F32)] if nk > 1 else [],
        compiler_params=_params("parallel", "parallel", "arbitrary"))(*args)


def _mm_nt(name, d, w, kind, *, out_dtype=F32):
    t = d.shape[0]
    w = _shards_view(w, kind)
    ns, k, c = w.shape
    n = ns * c
    tm = _tile(t, 1024, 8)
    to = _tile(k, 1024)
    tc = _tile(c, 1536)
    nc = n // tc
    per = c // tc

    def body(d_ref, w_ref, o_ref, *scratch):
        part = lax.dot_general(d_ref[...].astype(BF16), w_ref[...], (((1,), (1,)), ((), ())),
                               preferred_element_type=F32)
        if nc == 1:
            o_ref[...] = part.astype(out_dtype)
            return
        acc = scratch[0]
        kk = pl.program_id(2)

        @pl.when(kk == 0)
        def _():
            acc[...] = part

        @pl.when(kk > 0)
        def _():
            acc[...] += part

        @pl.when(kk == nc - 1)
        def _():
            o_ref[...] = acc[...].astype(out_dtype)

    return pl.pallas_call(
        body, name=name, grid=(t // tm, k // to, nc),
        in_specs=[pl.BlockSpec((tm, tc), lambda i, j, kk: (i, kk)),
                  pl.BlockSpec((None, to, tc), lambda i, j, kk: (kk // per, j, kk % per))],
        out_specs=pl.BlockSpec((tm, to), lambda i, j, kk: (i, j)),
        out_shape=jax.ShapeDtypeStruct((t, k), out_dtype),
        scratch_shapes=[pltpu.VMEM((tm, to), F32)] if nc > 1 else [],
        compiler_params=_params("parallel", "parallel", "arbitrary"))(d, w)


def _mm_tn(name, a, d, kind, shard_shape):
    t, k = a.shape
    n = d.shape[1]
    ns = N_CHIPS if kind == "col" else 1
    c = n // ns
    tkm = _tile(k, 1024)
    tn = _tile(c, 1536)
    tt = _tile(t, 1024, 8)
    nt = t // tt
    per = c // tn

    def body(a_ref, d_ref, o_ref, acc):
        kk = pl.program_id(2)
        part = lax.dot_general(a_ref[...].astype(BF16), d_ref[...].astype(BF16), (((0,), (0,)), ((), ())),
                               preferred_element_type=F32)

        @pl.when(kk == 0)
        def _():
            acc[...] = part

        @pl.when(kk > 0)
        def _():
            acc[...] += part

        @pl.when(kk == nt - 1)
        def _():
            o_ref[...] = acc[...].astype(BF16)

    out = pl.pallas_call(
        body, name=name, grid=(k // tkm, n // tn, nt),
        in_specs=[pl.BlockSpec((tt, tkm), lambda i, j, kk: (kk, i)),
                  pl.BlockSpec((tt, tn), lambda i, j, kk: (kk, j))],
        out_specs=pl.BlockSpec((None, tkm, tn), lambda i, j, kk: (j // per, i, j % per)),
        out_shape=jax.ShapeDtypeStruct((ns, k, c), BF16),
        scratch_shapes=[pltpu.VMEM((tkm, tn), F32)],
        compiler_params=_params("parallel", "parallel", "arbitrary"))(a, d)
    return out.reshape((N_CHIPS,) + tuple(shard_shape))


def _row_spec(tr, w, col=0):
    return pl.BlockSpec((tr, w), lambda i: (i, col))


def _full_spec(shape):
    return pl.BlockSpec(shape, lambda i: tuple(0 for _ in shape))


def _rms_fwd(name, x, gains):
    t, dm = x.shape
    tr = _tile(t, 256, 8)
    n = len(gains)

    def body(x_ref, *refs):
        xv = x_ref[...]
        xn = xv * lax.rsqrt(jnp.mean(xv * xv, axis=-1, keepdims=True) + EPS)
        for g_ref, o_ref in zip(refs[:n], refs[n:]):
            o_ref[...] = (xn * g_ref[...]).astype(BF16)

    outs = pl.pallas_call(
        body, name=name, grid=(t // tr,),
        in_specs=[_row_spec(tr, dm)] + [_full_spec((1, dm))] * n,
        out_specs=[_row_spec(tr, dm)] * n,
        out_shape=[jax.ShapeDtypeStruct((t, dm), BF16)] * n,
        compiler_params=_params("parallel"))(x, *gains)
    return list(outs)


def _ple_bwd_math(dy, gpre, pp):
    sg = _sigmoid(gpre)
    return (dy * pp * sg * (1.0 - sg)).astype(BF16), (dy * sg).astype(BF16)


def _rms_bwd(name, x, resid, pairs, ple=None):
    t, dm = x.shape
    tr = _tile(t, 256, 8)
    n = len(pairs)
    n_in = 2 * n + (2 if ple is not None else 0)

    def body(x_ref, r_ref, *refs):
        ins, outs = refs[:n_in], refs[n_in:]
        i = pl.program_id(0)
        xv = x_ref[...]
        rs = lax.rsqrt(jnp.mean(xv * xv, axis=-1, keepdims=True) + EPS)
        xn = xv * rs
        total = r_ref[...]
        for kx in range(n):
            g_ref, du_ref = ins[2 * kx], ins[2 * kx + 1]
            dg_ref = outs[1 + kx]
            du = du_ref[...].astype(F32)

            @pl.when(i == 0)
            def _():
                dg_ref[...] = jnp.zeros_like(dg_ref)

            dg_ref[...] += _colsum(du * xn)
            dxh = du * g_ref[...]
            total = total + rs * (dxh - xn * jnp.mean(dxh * xn, axis=-1, keepdims=True))
        outs[0][...] = total
        if ple is not None:
            outs[1 + n][...], outs[2 + n][...] = _ple_bwd_math(total, ins[2 * n][...].astype(F32), ins[2 * n + 1][...].astype(F32))

    in_specs = [_row_spec(tr, dm), _row_spec(tr, dm)]
    args = [x, resid]
    for g, du in pairs:
        in_specs += [_full_spec((1, dm)), _row_spec(tr, dm)]
        args += [g, du]
    out_specs = [_row_spec(tr, dm)] + [_full_spec((1, dm))] * n
    out_shape = [jax.ShapeDtypeStruct((t, dm), F32)] + [jax.ShapeDtypeStruct((1, dm), F32)] * n
    if ple is not None:
        in_specs += [_row_spec(tr, dm)] * 2
        args += list(ple)
        out_specs += [_row_spec(tr, dm)] * 2
        out_shape += [jax.ShapeDtypeStruct((t, dm), BF16)] * 2
    outs = pl.pallas_call(
        body, name=name, grid=(t // tr,), in_specs=in_specs, out_specs=out_specs, out_shape=out_shape,
        compiler_params=_params("arbitrary"))(*args)
    if ple is not None:
        return outs[0], list(outs[1:1 + n]), outs[1 + n], outs[2 + n]
    return outs[0], list(outs[1:])


def _shifted_copies(ext, sh, rows):
    for s in range(1, SUBLANES):
        sh[s - 1] = ext[pl.ds(s, rows), :]


def _window(ext, sh, off, row0, rows, lanes):
    s = off % SUBLANES
    src = ext if s == 0 else sh.at[s - 1]
    return src[pl.ds(off - s + row0, rows), lanes]


def _mixa_fwd(name, pa, cw, cb, lg, lb, seq):
    t, w3 = pa.shape
    cc = w3 // 3
    tr = _tile(seq, 128, HALO)
    per_seq = seq // tr
    hb = tr // HALO
    lead = HALO - (CONV_WIDTH - 1)

    def body(a_ref, b_ref, z_ref, ah_ref, bh_ref, cw_ref, cb_ref, lg_ref, lb_ref, c_ref, m_ref, ext, sh):
        i = pl.program_id(0)
        gh = ah_ref[...].astype(F32) * _sigmoid(bh_ref[...].astype(F32))
        ext[pl.ds(0, HALO), :] = jnp.where((i % per_seq) == 0, 0.0, gh)
        ext[pl.ds(HALO, tr), :] = a_ref[...].astype(F32) * _sigmoid(b_ref[...].astype(F32))
        _shifted_copies(ext, sh, tr + HALO - SUBLANES)
        for lc in range(cc // 128):
            lanes = pl.ds(lc * 128, 128)
            taps = [cw_ref[pl.ds(k, 1), lanes] for k in range(CONV_WIDTH)]
            for row0 in range(0, tr, CONV_ROWS):
                acc = jnp.broadcast_to(cb_ref[:, lanes], (CONV_ROWS, 128))
                for k in range(CONV_WIDTH):
                    acc = acc + _window(ext, sh, lead + k, row0, CONV_ROWS, lanes) * taps[k]
                c_ref[pl.ds(row0, CONV_ROWS), lanes] = acc
        acc = c_ref[...]
        xc = acc - jnp.mean(acc, axis=-1, keepdims=True)
        nrm = xc * lax.rsqrt(jnp.mean(xc * xc, axis=-1, keepdims=True) + EPS)
        l = nrm * lg_ref[...] + lb_ref[...]
        z = z_ref[...].astype(F32)
        m_ref[...] = (l * _sigmoid(l) * z * _sigmoid(z)).astype(BF16)

    halo = lambda col: pl.BlockSpec((HALO, cc), lambda i: (jnp.maximum(i * hb - 1, 0), col))
    return pl.pallas_call(
        body, name=name, grid=(t // tr,),
        in_specs=[_row_spec(tr, cc, 0), _row_spec(tr, cc, 1), _row_spec(tr, cc, 2), halo(0), halo(1),
                  _full_spec((HALO, cc)), _full_spec((1, cc)), _full_spec((1, cc)), _full_spec((1, cc))],
        out_specs=[_row_spec(tr, cc), _row_spec(tr, cc)],
        out_shape=[jax.ShapeDtypeStruct((t, cc), F32), jax.ShapeDtypeStruct((t, cc), BF16)],
        scratch_shapes=[pltpu.VMEM((tr + HALO, cc), F32), pltpu.VMEM((SUBLANES - 1, tr + HALO - SUBLANES, cc), F32)],
        compiler_params=_params("parallel"))(pa, pa, pa, pa, pa, cw, cb, lg, lb)


def _mixa_bwd1(name, c, pa, dm, lg, lb):
    t, cc = c.shape
    tr = _tile(t, 128, 8)

    def body(c_ref, z_ref, dm_ref, lg_ref, lb_ref, dc_ref, dz_ref, dlg_ref, dlb_ref, dcb_ref):
        i = pl.program_id(0)
        cv = c_ref[...]
        xc = cv - jnp.mean(cv, axis=-1, keepdims=True)
        rs = lax.rsqrt(jnp.mean(xc * xc, axis=-1, keepdims=True) + EPS)
        nrm = xc * rs
        l = nrm * lg_ref[...] + lb_ref[...]
        z = z_ref[...].astype(F32)
        sl, sz = _sigmoid(l), _sigmoid(z)
        dmv = dm_ref[...].astype(F32)
        ds = dmv * (z * sz)
        dzz = dmv * (l * sl)
        dz_ref[...] = (dzz * (sz * (1.0 + z * (1.0 - sz)))).astype(BF16)
        dl = ds * (sl * (1.0 + l * (1.0 - sl)))
        dn = dl * lg_ref[...]
        dc = rs * (dn - jnp.mean(dn, axis=-1, keepdims=True) - nrm * jnp.mean(dn * nrm, axis=-1, keepdims=True))
        dc_ref[...] = dc

        @pl.when(i == 0)
        def _():
            dlg_ref[...] = jnp.zeros_like(dlg_ref)
            dlb_ref[...] = jnp.zeros_like(dlb_ref)
            dcb_ref[...] = jnp.zeros_like(dcb_ref)

        dlg_ref[...] += _colsum(dl * nrm)
        dlb_ref[...] += _colsum(dl)
        dcb_ref[...] += _colsum(dc)

    vec = jax.ShapeDtypeStruct((1, cc), F32)
    return pl.pallas_call(
        body, name=name, grid=(t // tr,),
        in_specs=[_row_spec(tr, cc), _row_spec(tr, cc, 2), _row_spec(tr, cc), _full_spec((1, cc)), _full_spec((1, cc))],
        out_specs=[_row_spec(tr, cc), _row_spec(tr, cc)] + [_full_spec((1, cc))] * 3,
        out_shape=[jax.ShapeDtypeStruct((t, cc), F32), jax.ShapeDtypeStruct((t, cc), BF16), vec, vec, vec],
        compiler_params=_params("arbitrary"))(c, pa, dm, lg, lb)


def _mixa_bwd2(name, dc, pa, dz, cw, seq):
    t, cc = dc.shape
    tr = _tile(seq, 128, HALO)
    per_seq = seq // tr
    hb = tr // HALO
    steps = t // tr
    last_halo = t // HALO - 1

    def body(dc_ref, dcn_ref, a_ref, b_ref, dz_ref, cw_ref, dp_ref, dcw_ref, ext, sh, sums):
        i = pl.program_id(0)
        ext[pl.ds(0, tr), :] = dc_ref[...]
        ext[pl.ds(tr, HALO), :] = jnp.where((i % per_seq) == per_seq - 1, 0.0, dcn_ref[...])
        _shifted_copies(ext, sh, tr + HALO - SUBLANES)

        @pl.when(i == 0)
        def _():
            sums[...] = jnp.zeros_like(sums)
            dcw_ref[...] = jnp.zeros_like(dcw_ref)

        av = a_ref[...].astype(F32)
        sb = _sigmoid(b_ref[...].astype(F32))
        glu = av * sb
        dglu = jnp.zeros((tr, cc), F32)
        for k in range(CONV_WIDTH):
            wd = _window(ext, sh, CONV_WIDTH - 1 - k, 0, tr, slice(None))
            dglu = dglu + wd * cw_ref[pl.ds(k, 1), :]
            sums[pl.ds(k * SUBLANES, SUBLANES), :] += (wd * glu).reshape(tr // SUBLANES, SUBLANES, cc).sum(axis=0)
        dp_ref[:, pl.ds(0, cc)] = (dglu * sb).astype(BF16)
        dp_ref[:, pl.ds(cc, cc)] = (dglu * av * sb * (1.0 - sb)).astype(BF16)
        dp_ref[:, pl.ds(2 * cc, cc)] = dz_ref[...]

        @pl.when(i == steps - 1)
        def _():
            for k in range(CONV_WIDTH):
                dcw_ref[pl.ds(k, 1), :] = _colsum(sums[pl.ds(k * SUBLANES, SUBLANES), :])

    nxt = pl.BlockSpec((HALO, cc), lambda i: (jnp.minimum((i + 1) * hb, last_halo), 0))
    return pl.pallas_call(
        body, name=name, grid=(steps,),
        in_specs=[_row_spec(tr, cc), nxt, _row_spec(tr, cc, 0), _row_spec(tr, cc, 1), _row_spec(tr, cc),
                  _full_spec((HALO, cc))],
        out_specs=[_row_spec(tr, 3 * cc), _full_spec((HALO, cc))],
        out_shape=[jax.ShapeDtypeStruct((t, 3 * cc), BF16), jax.ShapeDtypeStruct((HALO, cc), F32)],
        scratch_shapes=[pltpu.VMEM((tr + HALO, cc), F32), pltpu.VMEM((SUBLANES - 1, tr + HALO - SUBLANES, cc), F32),
                        pltpu.VMEM((HALO * SUBLANES, cc), F32)],
        compiler_params=_params("arbitrary"))(dc, dc, pa, pa, dz, cw)


def _ple_fwd(name, h, gpre, pp):
    t, dm = h.shape
    tr = _tile(t, 256, 8)

    def body(h_ref, g_ref, p_ref, o_ref):
        o_ref[...] = h_ref[...] + _sigmoid(g_ref[...].astype(F32)) * p_ref[...].astype(F32)

    return pl.pallas_call(
        body, name=name, grid=(t // tr,), in_specs=[_row_spec(tr, dm)] * 3, out_specs=_row_spec(tr, dm),
        out_shape=jax.ShapeDtypeStruct((t, dm), F32), compiler_params=_params("parallel"))(h, gpre, pp)


def _ple_loss(name, h, gpre, pp, target):
    t, dm = h.shape
    tr = _tile(t, 256, 8)

    def body(h_ref, g_ref, p_ref, t_ref, dy_ref, dg_ref, dp_ref, sq_ref):
        i = pl.program_id(0)
        gpre_v, pp_v = g_ref[...].astype(F32), p_ref[...].astype(F32)
        err = h_ref[...] + _sigmoid(gpre_v) * pp_v - t_ref[...]
        dy = err * (1.0 / dm)
        dy_ref[...] = dy
        dg_ref[...], dp_ref[...] = _ple_bwd_math(dy, gpre_v, pp_v)

        @pl.when(i == 0)
        def _():
            sq_ref[...] = jnp.zeros_like(sq_ref)

        sq_ref[...] += jnp.sum(jnp.sum(err * err, axis=1, keepdims=True), axis=0, keepdims=True)

    return pl.pallas_call(
        body, name=name, grid=(t // tr,), in_specs=[_row_spec(tr, dm)] * 4,
        out_specs=[_row_spec(tr, dm)] * 3 + [_full_spec((1, 1))],
        out_shape=[jax.ShapeDtypeStruct((t, dm), F32)] + [jax.ShapeDtypeStruct((t, dm), BF16)] * 2
        + [jax.ShapeDtypeStruct((1, 1), F32)],
        compiler_params=_params("arbitrary"))(h, gpre, pp, target)


def _rope_tables(seq):
    half = ROPE_DIM // 2
    inv = ROPE_THETA ** (-jnp.arange(half, dtype=F32) * (2.0 / ROPE_DIM))
    ang = jnp.arange(seq).astype(F32)[:, None] * inv[None, :]
    cos, sin = jnp.cos(ang), jnp.sin(ang)
    rest = HEAD_DIM - ROPE_DIM
    one = jnp.ones((seq, rest), F32)
    zero = jnp.zeros((seq, rest), F32)
    zh = jnp.zeros((seq, half), F32)
    tc = jnp.concatenate([cos, cos, one], axis=1)
    ta = jnp.concatenate([-sin, zh, zero], axis=1)
    tb = jnp.concatenate([zh, sin, zero], axis=1)
    return [jnp.tile(tb_, (1, 128 // HEAD_DIM)) for tb_ in (tc, ta, tb)]


def _wide(tab_ref, w):
    return jnp.tile(tab_ref[...], (1, w // 128))


def _hnr_fwd(name, src, width, gain, tables, seq):
    t = src.shape[0]
    tr = _tile(seq, 256, 8)
    per_seq = seq // tr

    def body(x_ref, g_ref, tc_ref, ta_ref, tb_ref, o_ref):
        xv = x_ref[...].astype(F32)
        rs = lax.rsqrt(_seg_allsum64(xv * xv) * (1.0 / HEAD_DIM) + EPS)
        y = xv * rs * g_ref[...]
        o_ref[...] = (y * _wide(tc_ref, width) + pltpu.roll(y, width - ROPE_DIM // 2, 1) * _wide(ta_ref, width)
                      + pltpu.roll(y, ROPE_DIM // 2, 1) * _wide(tb_ref, width))

    tab = pl.BlockSpec((tr, 128), lambda i: (i % per_seq, 0))
    return pl.pallas_call(
        body, name=name, grid=(t // tr,),
        in_specs=[_row_spec(tr, width), _full_spec((1, width)), tab, tab, tab],
        out_specs=_row_spec(tr, width), out_shape=jax.ShapeDtypeStruct((t, width), F32),
        compiler_params=_params("parallel"))(src, gain, *tables)


def _hnr_bwd_math(xv, gain, dout, tc, ta, tb, width):
    dy = dout * tc + pltpu.roll(dout * ta, ROPE_DIM // 2, 1) + pltpu.roll(dout * tb, width - ROPE_DIM // 2, 1)
    rs = lax.rsqrt(_seg_allsum64(xv * xv) * (1.0 / HEAD_DIM) + EPS)
    xn = xv * rs
    dyh = dy * gain
    dx = rs * (dyh - xn * (_seg_allsum64(dyh * xn) * (1.0 / HEAD_DIM)))
    return dx, _colsum(dy * xn)


def _q_bwd(name, p1, gain, tables, dqs, dgt, seq):
    t, w4 = p1.shape
    da = w4 // 4
    width = 3 * da
    tr = _tile(seq, 128, 8)
    per_seq = seq // tr

    def body(x_ref, g_ref, tc_ref, ta_ref, tb_ref, d0_ref, d1_ref, d2_ref, dgt_ref, o_ref, dg_ref):
        i = pl.program_id(0)
        dout = jnp.concatenate([d0_ref[...], d1_ref[...], d2_ref[...]], axis=1)
        dx, dg = _hnr_bwd_math(x_ref[...].astype(F32), g_ref[...], dout, _wide(tc_ref, width), _wide(ta_ref, width),
                               _wide(tb_ref, width), width)

        @pl.when(i == 0)
        def _():
            dg_ref[...] = jnp.zeros_like(dg_ref)

        dg_ref[...] += dg
        o_ref[:, pl.ds(0, width)] = dx.astype(BF16)
        o_ref[:, pl.ds(width, da)] = dgt_ref[...]

    tab = pl.BlockSpec((tr, 128), lambda i: (i % per_seq, 0))
    return pl.pallas_call(
        body, name=name, grid=(t // tr,),
        in_specs=[_row_spec(tr, width), _full_spec((1, width)), tab, tab, tab] + [_row_spec(tr, da)] * 4,
        out_specs=[_row_spec(tr, w4), _full_spec((1, width))],
        out_shape=[jax.ShapeDtypeStruct((t, w4), BF16), jax.ShapeDtypeStruct((1, width), F32)],
        compiler_params=_params("arbitrary"))(p1, gain, *tables, *dqs, dgt)


def _k_bwd(name, kv, gain, tables, dk, dv, seq):
    t, w2 = kv.shape
    da = w2 // 2
    tr = _tile(seq, 256, 8)
    per_seq = seq // tr

    def body(x_ref, g_ref, tc_ref, ta_ref, tb_ref, dk_ref, dv_ref, o_ref, dg_ref):
        i = pl.program_id(0)
        dx, dg = _hnr_bwd_math(x_ref[...], g_ref[...], dk_ref[...], _wide(tc_ref, da), _wide(ta_ref, da),
                               _wide(tb_ref, da), da)

        @pl.when(i == 0)
        def _():
            dg_ref[...] = jnp.zeros_like(dg_ref)

        dg_ref[...] += dg
        o_ref[:, pl.ds(0, da)] = dx.astype(BF16)
        o_ref[:, pl.ds(da, da)] = dv_ref[...].astype(BF16)

    tab = pl.BlockSpec((tr, 128), lambda i: (i % per_seq, 0))
    return pl.pallas_call(
        body, name=name, grid=(t // tr,),
        in_specs=[_row_spec(tr, da), _full_spec((1, da)), tab, tab, tab] + [_row_spec(tr, da)] * 2,
        out_specs=[_row_spec(tr, w2), _full_spec((1, da))],
        out_shape=[jax.ShapeDtypeStruct((t, w2), BF16), jax.ShapeDtypeStruct((1, da), F32)],
        compiler_params=_params("arbitrary"))(kv, gain, *tables, dk, dv)


def _unit_rows(ref, dil, r, blk):
    start = r + dil * SPAN * blk
    if dil == 1:
        return ref[pl.ds(start, SPAN), :]
    return ref[pl.ds(start, SPAN, stride=dil), :]


def _store_rows(ref, dil, r, blk, val):
    start = r + dil * SPAN * blk
    if dil == 1:
        ref[pl.ds(start, SPAN), :] = val
    else:
        ref[pl.ds(start, SPAN, stride=dil), :] = val


def _band_mask(with_prev):
    nk = 2 * SPAN if with_prev else SPAN
    qi = lax.broadcasted_iota(jnp.int32, (SPAN, nk), 0)
    kj = lax.broadcasted_iota(jnp.int32, (SPAN, nk), 1)
    if with_prev:
        return (kj >= qi) & (kj <= qi + SPAN)
    return kj <= qi


_NT = (((1,), (1,)), ((), ()))
_TN = (((0,), (0,)), ((), ()))


def _group_fwd(q_ref, k_ref, v_ref, o_ref, l_ref, dil, seq):
    nblk = seq // (dil * SPAN)
    scale = HEAD_DIM ** -0.5
    for r in range(dil):
        kc = vc = None
        for blk in range(nblk):
            kp, vp = kc, vc
            q = _unit_rows(q_ref, dil, r, blk) * scale
            kc = _unit_rows(k_ref, dil, r, blk)
            vc = _unit_rows(v_ref, dil, r, blk)
            with_prev = blk > 0
            kcat = jnp.concatenate([kp, kc], axis=0) if with_prev else kc
            vcat = jnp.concatenate([vp, vc], axis=0) if with_prev else vc
            mask = _band_mask(with_prev)
            outs, lses = [], []
            for hh in range(128 // HEAD_DIM):
                sl = slice(hh * HEAD_DIM, (hh + 1) * HEAD_DIM)
                s = lax.dot_general(q[:, sl].astype(BF16), kcat[:, sl].astype(BF16), _NT, preferred_element_type=F32)
                s = jnp.where(mask, s, NEG_INF)
                mx = jnp.max(s, axis=-1, keepdims=True)
                p = jnp.exp(s - mx)
                den = jnp.sum(p, axis=-1, keepdims=True)
                o = jnp.dot(p.astype(BF16), vcat[:, sl].astype(BF16), preferred_element_type=F32) / den
                outs.append(o)
                lses.append(jnp.broadcast_to(mx + jnp.log(den), (SPAN, HEAD_DIM)))
            _store_rows(o_ref, dil, r, blk, jnp.concatenate(outs, axis=1))
            _store_rows(l_ref, dil, r, blk, jnp.concatenate(lses, axis=1))


def _attn_fwd(name, qn, kn, kv, p1, seq):
    t, da = kn.shape
    hp = da // 128
    ng = len(DILATIONS)

    def body(q0_ref, q1_ref, q2_ref, k_ref, v_ref, g_ref, o_ref, l_ref, m_ref, og, lg):
        for g, q_ref in enumerate((q0_ref, q1_ref, q2_ref)):
            _group_fwd(q_ref, k_ref, v_ref, og.at[g], lg.at[g], DILATIONS[g], seq)
        a0, a1, a2 = lg[0], lg[1], lg[2]
        mx = jnp.maximum(jnp.maximum(a0, a1), a2)
        e0, e1, e2 = jnp.exp(a0 - mx), jnp.exp(a1 - mx), jnp.exp(a2 - mx)
        den = e0 + e1 + e2
        o = (e0 * og[0] + e1 * og[1] + e2 * og[2]) / den
        o_ref[...] = o
        l_ref[...] = mx + jnp.log(den)
        gt = g_ref[...].astype(F32)
        m_ref[...] = (o * gt * _sigmoid(gt)).astype(BF16)

    blk_spec = lambda off: pl.BlockSpec((seq, 128), lambda b, h: (b, off + h))
    return pl.pallas_call(
        body, name=name, grid=(t // seq, hp),
        in_specs=[blk_spec(0), blk_spec(hp), blk_spec(2 * hp), blk_spec(0), blk_spec(hp), blk_spec(3 * hp)],
        out_specs=[blk_spec(0)] * 3,
        out_shape=[jax.ShapeDtypeStruct((t, da), F32)] * 2 + [jax.ShapeDtypeStruct((t, da), BF16)],
        scratch_shapes=[pltpu.VMEM((ng, seq, 128), F32), pltpu.VMEM((ng, seq, 128), F32)],
        compiler_params=_params("parallel", "parallel"))(qn, qn, qn, kn, kv, p1)


def _group_bwd(q_ref, k_ref, v_ref, do_ref, l_ref, d_ref, dq_ref, dk_ref, dv_ref, dil, seq, first):
    nblk = seq // (dil * SPAN)
    scale = HEAD_DIM ** -0.5
    nh = 128 // HEAD_DIM

    def put(ref, r, blk, val):
        if not first:
            val = val + _unit_rows(ref, dil, r, blk)
        _store_rows(ref, dil, r, blk, val)

    for r in range(dil):
        kc = vc = None
        pend_k = pend_v = None
        for blk in range(nblk):
            kp, vp = kc, vc
            q = _unit_rows(q_ref, dil, r, blk) * scale
            kc = _unit_rows(k_ref, dil, r, blk)
            vc = _unit_rows(v_ref, dil, r, blk)
            dov = _unit_rows(do_ref, dil, r, blk)
            lrow = _unit_rows(l_ref, dil, r, blk)
            drow = _unit_rows(d_ref, dil, r, blk)
            with_prev = blk > 0
            kcat = jnp.concatenate([kp, kc], axis=0) if with_prev else kc
            vcat = jnp.concatenate([vp, vc], axis=0) if with_prev else vc
            mask = _band_mask(with_prev)
            dqs, dkcs, dvcs = [], [], []
            for hh in range(nh):
                sl = slice(hh * HEAD_DIM, (hh + 1) * HEAD_DIM)
                qh = q[:, sl].astype(BF16)
                doh = dov[:, sl].astype(BF16)
                kh = kcat[:, sl].astype(BF16)
                s = lax.dot_general(qh, kh, _NT, preferred_element_type=F32)
                p = jnp.where(mask, jnp.exp(s - lrow[:, hh * HEAD_DIM:hh * HEAD_DIM + 1]), 0.0)
                dp = lax.dot_general(doh, vcat[:, sl].astype(BF16), _NT, preferred_element_type=F32)
                ds = (p * (dp - drow[:, hh * HEAD_DIM:hh * HEAD_DIM + 1])).astype(BF16)
                dqs.append(jnp.dot(ds, kh, preferred_element_type=F32) * scale)
                dkcs.append(lax.dot_general(ds, qh, _TN, preferred_element_type=F32))
                dvcs.append(lax.dot_general(p.astype(BF16), doh, _TN, preferred_element_type=F32))
            _store_rows(dq_ref, dil, r, blk, jnp.concatenate(dqs, axis=1))
            dkcat = jnp.concatenate(dkcs, axis=1)
            dvcat = jnp.concatenate(dvcs, axis=1)
            if with_prev:
                put(dk_ref, r, blk - 1, pend_k + dkcat[:SPAN])
                put(dv_ref, r, blk - 1, pend_v + dvcat[:SPAN])
                pend_k, pend_v = dkcat[SPAN:], dvcat[SPAN:]
            else:
                pend_k, pend_v = dkcat, dvcat
        put(dk_ref, r, nblk - 1, pend_k)
        put(dv_ref, r, nblk - 1, pend_v)


def _attn_bwd(name, qn, kn, kv, do, lse, dsum, seq):
    t, da = kn.shape
    hp = da // 128

    def body(q0_ref, q1_ref, q2_ref, k_ref, v_ref, do_ref, l_ref, d_ref, dq0_ref, dq1_ref, dq2_ref, dk_ref, dv_ref):
        groups = ((q0_ref, dq0_ref), (q1_ref, dq1_ref), (q2_ref, dq2_ref))
        for g, (q_ref, dq_ref) in enumerate(groups):
            _group_bwd(q_ref, k_ref, v_ref, do_ref, l_ref, d_ref, dq_ref, dk_ref, dv_ref, DILATIONS[g], seq, g == 0)

    blk_spec = lambda off: pl.BlockSpec((seq, 128), lambda b, h: (b, off + h))
    return pl.pallas_call(
        body, name=name, grid=(t // seq, hp),
        in_specs=[blk_spec(0), blk_spec(hp), blk_spec(2 * hp), blk_spec(0), blk_spec(hp), blk_spec(0), blk_spec(0), blk_spec(0)],
        out_specs=[blk_spec(0)] * 5,
        out_shape=[jax.ShapeDtypeStruct((t, da), F32)] * 5,
        compiler_params=_params("parallel", "parallel"))(qn, qn, qn, kn, kv, do, lse, dsum)


def _gate_bwd(name, dm, o, p1):
    t, da = o.shape
    tr = _tile(t, 256, 8)

    def body(dm_ref, o_ref, g_ref, do_ref, dg_ref, ds_ref):
        g = g_ref[...].astype(F32)
        sg = _sigmoid(g)
        dmv, ov = dm_ref[...].astype(F32), o_ref[...]
        do = dmv * (g * sg)
        do_ref[...] = do
        dg_ref[...] = (dmv * ov * (sg * (1.0 + g * (1.0 - sg)))).astype(BF16)
        ds_ref[...] = _seg_allsum64(do * ov)

    return pl.pallas_call(
        body, name=name, grid=(t // tr,),
        in_specs=[_row_spec(tr, da), _row_spec(tr, da), _row_spec(tr, da, 3)],
        out_specs=[_row_spec(tr, da)] * 3,
        out_shape=[jax.ShapeDtypeStruct((t, da), F32), jax.ShapeDtypeStruct((t, da), BF16), jax.ShapeDtypeStruct((t, da), F32)],
        compiler_params=_params("parallel"))(dm, o, p1)


def _cast_bf16(name, w2d, chip):
    r, c = w2d.shape
    tr = _tile(r, 256, 16)

    def body(chip_ref, x_ref, o_ref):
        o_ref[...] = x_ref[...].astype(BF16)

    grid_spec = pltpu.PrefetchScalarGridSpec(
        num_scalar_prefetch=1, grid=(r // tr,),
        in_specs=[pl.BlockSpec((tr, c), lambda i, m: (i, 0))],
        out_specs=pl.BlockSpec((None, tr, c), lambda i, m: (m[0], i, 0)))
    return pl.pallas_call(
        body, name=name, grid_spec=grid_spec, out_shape=jax.ShapeDtypeStruct((N_CHIPS, r, c), BF16),
        compiler_params=_params("parallel"))(chip, w2d)


def _adamw(name, w, g, m, v):
    r, c = w.shape
    tr = _tile(r, 256, 8)
    c1 = 1.0 - ADAM_B1 ** ADAM_STEP
    c2 = 1.0 - ADAM_B2 ** ADAM_STEP

    def body(w_ref, g_ref, m_ref, v_ref, d_ref, nm_ref, nv_ref):
        gv = g_ref[...]
        nm = ADAM_B1 * m_ref[...] + (1.0 - ADAM_B1) * gv
        nv = ADAM_B2 * v_ref[...] + (1.0 - ADAM_B2) * (gv * gv)
        nm_ref[...] = nm
        nv_ref[...] = nv
        d_ref[...] = -ADAM_LR * ((nm / c1) / (jnp.sqrt(nv / c2) + ADAM_EPS) + ADAM_WD * w_ref[...])

    sds = jax.ShapeDtypeStruct((r, c), F32)
    return pl.pallas_call(
        body, name=name, grid=(r // tr,), in_specs=[_row_spec(tr, c)] * 4, out_specs=[_row_spec(tr, c)] * 3,
        out_shape=[sds] * 3, compiler_params=_params("parallel"))(w, g, m, v)


def _pair_sum(name, gd, recv, core):
    _, r, c = gd.shape
    rh = r // 2
    tr = _tile(rh, 256, 8)
    nrt = rh // tr

    def body(c_ref, a_ref, b_ref, o_ref):
        o_ref[...] = (a_ref[...].astype(F32) + b_ref[...].astype(F32)).astype(BF16)

    grid_spec = pltpu.PrefetchScalarGridSpec(
        num_scalar_prefetch=1, grid=(N_CHIPS, nrt),
        in_specs=[pl.BlockSpec((None, tr, c), lambda j, i, cr: (j, cr[0] * nrt + i, 0)),
                  pl.BlockSpec((None, tr, c), lambda j, i, cr: (j, i, 0))],
        out_specs=pl.BlockSpec((None, tr, c), lambda j, i, cr: (j, i, 0)))
    return pl.pallas_call(
        body, name=name, grid_spec=grid_spec, out_shape=jax.ShapeDtypeStruct((N_CHIPS, rh, c), BF16),
        compiler_params=_params("parallel", "parallel"))(core, gd, recv)


def _chip_sum(name, sums, parts, order):
    _, rh, c = parts.shape
    tr = _tile(rh, 256, 16)
    nrt = rh // tr

    def body(o_ref_, s_ref, p1_ref, p2_ref, p3_ref, o_ref):
        acc = s_ref[...].astype(F32)
        for p_ref in (p1_ref, p2_ref, p3_ref):
            acc = acc + p_ref[...].astype(F32)
        o_ref[...] = acc

    slot = lambda k: pl.BlockSpec((None, tr, c), lambda i, o: (o[k], i, 0))
    grid_spec = pltpu.PrefetchScalarGridSpec(
        num_scalar_prefetch=1, grid=(nrt,),
        in_specs=[slot(0), slot(1), slot(2), slot(3)],
        out_specs=pl.BlockSpec((tr, c), lambda i, o: (o[N_CHIPS] * nrt + i, 0)))
    return pl.pallas_call(
        body, name=name, grid_spec=grid_spec, out_shape=jax.ShapeDtypeStruct((2 * rh, c), F32),
        compiler_params=_params("parallel"))(order, sums, parts, parts, parts)


HBM = pl.BlockSpec(memory_space=pl.ANY)


def _place():
    x, y, c = lax.axis_index("x"), lax.axis_index("y"), lax.axis_index("c")
    chips = [(1 - x, y), (x, 1 - y), (1 - x, 1 - y)]
    return x, y, c, chips


def _half(ref, hc):
    rows = ref.shape[0] // 2
    return ref.at[pl.ds(hc * rows, rows)]


def _gather_weights(shards):
    n = len(shards)

    def body(*refs):
        outs = refs[n:2 * n]
        send_sems, recv_sems = refs[2 * n:]
        x, y, c, chips = _place()
        mine = 2 * x + y
        first, passed = [], []
        for i in range(n):
            for k, chip in enumerate(chips):
                cp = pltpu.make_async_remote_copy(
                    src_ref=_half(outs[i].at[mine], c), dst_ref=_half(outs[i].at[mine], c),
                    send_sem=send_sems.at[6 * i + k], recv_sem=recv_sems.at[6 * i + k],
                    device_id=(chip[0], chip[1], c), device_id_type=MESH)
                cp.start()
                first.append(cp)
        for i in range(n):
            for k, chip in enumerate(chips):
                theirs = 2 * chip[0] + chip[1]
                landed = _half(outs[i].at[theirs], c)
                pltpu.make_async_remote_copy(
                    src_ref=landed, dst_ref=landed, send_sem=send_sems.at[6 * i + k], recv_sem=recv_sems.at[6 * i + k],
                    device_id=(chip[0], chip[1], c), device_id_type=MESH).wait_recv()
                cp = pltpu.make_async_remote_copy(
                    src_ref=landed, dst_ref=landed, send_sem=send_sems.at[6 * i + 3 + k], recv_sem=recv_sems.at[6 * i + 3 + k],
                    device_id=(x, y, 1 - c), device_id_type=MESH)
                cp.start()
                passed.append(cp)
        for i in range(n):
            for k, chip in enumerate(chips):
                theirs = 2 * chip[0] + chip[1]
                other = _half(outs[i].at[theirs], 1 - c)
                pltpu.make_async_remote_copy(
                    src_ref=other, dst_ref=other, send_sem=send_sems.at[6 * i + 3 + k], recv_sem=recv_sems.at[6 * i + 3 + k],
                    device_id=(x, y, 1 - c), device_id_type=MESH).wait_recv()
        for cp in first + passed:
            cp.wait_send()

    return pl.pallas_call(
        body, name="gather_weights", in_specs=[HBM] * n, out_specs=[HBM] * n,
        out_shape=[jax.ShapeDtypeStruct(s.shape, s.dtype) for s in shards],
        input_output_aliases={i: i for i in range(n)},
        scratch_shapes=[pltpu.SemaphoreType.DMA((6 * n,)), pltpu.SemaphoreType.DMA((6 * n,))],
        )(*shards)


def _pair_exchange(grads):
    n = len(grads)

    def body(*refs):
        ins, outs = refs[:n], refs[n:2 * n]
        send_sems, recv_sems = refs[2 * n:]
        x, y, c, _ = _place()
        cps = []
        for i in range(n):
            rows = ins[i].shape[1] // 2
            cp = pltpu.make_async_remote_copy(
                src_ref=ins[i].at[:, pl.ds((1 - c) * rows, rows), :], dst_ref=outs[i],
                send_sem=send_sems.at[i], recv_sem=recv_sems.at[i], device_id=(x, y, 1 - c), device_id_type=MESH)
            cp.start()
            cps.append(cp)
        for cp in cps:
            cp.wait()

    return pl.pallas_call(
        body, name="pair_exchange", in_specs=[HBM] * n, out_specs=[HBM] * n,
        out_shape=[jax.ShapeDtypeStruct((N_CHIPS, g.shape[1] // 2, g.shape[2]), g.dtype) for g in grads],
        scratch_shapes=[pltpu.SemaphoreType.DMA((n,)), pltpu.SemaphoreType.DMA((n,))],
        )(*grads)


def _chip_scatter(sums):
    n = len(sums)

    def body(*refs):
        ins, outs = refs[:n], refs[n:2 * n]
        send_sems, recv_sems = refs[2 * n:]
        x, y, c, chips = _place()
        mine = 2 * x + y
        cps = []
        for i in range(n):
            for k, chip in enumerate(chips):
                theirs = 2 * chip[0] + chip[1]
                cp = pltpu.make_async_remote_copy(
                    src_ref=ins[i].at[theirs], dst_ref=outs[i].at[mine],
                    send_sem=send_sems.at[3 * i + k], recv_sem=recv_sems.at[3 * i + k],
                    device_id=(chip[0], chip[1], c), device_id_type=MESH)
                cp.start()
                cps.append((cp, i, k, theirs))
        for cp, i, k, theirs in cps:
            cp.wait_send()
            pltpu.make_async_remote_copy(
                src_ref=ins[i].at[theirs], dst_ref=outs[i].at[theirs],
                send_sem=send_sems.at[3 * i + k], recv_sem=recv_sems.at[3 * i + k],
                device_id=(x, y, c), device_id_type=MESH).wait_recv()

    return pl.pallas_call(
        body, name="chip_scatter", in_specs=[HBM] * n, out_specs=[HBM] * n,
        out_shape=[jax.ShapeDtypeStruct(s.shape, s.dtype) for s in sums],
        scratch_shapes=[pltpu.SemaphoreType.DMA((3 * n,)), pltpu.SemaphoreType.DMA((3 * n,))],
        )(*sums)


def _sibling_join(grads):
    n = len(grads)

    def body(*refs):
        outs = refs[n:2 * n]
        send_sems, recv_sems = refs[2 * n:]
        x, y, c, _ = _place()
        cps = []
        for i in range(n):
            cp = pltpu.make_async_remote_copy(
                src_ref=_half(outs[i], c), dst_ref=_half(outs[i], c), send_sem=send_sems.at[i], recv_sem=recv_sems.at[i],
                device_id=(x, y, 1 - c), device_id_type=MESH)
            cp.start()
            cps.append(cp)
        for i, cp in enumerate(cps):
            cp.wait_send()
            pltpu.make_async_remote_copy(
                src_ref=_half(outs[i], 1 - c), dst_ref=_half(outs[i], 1 - c), send_sem=send_sems.at[i],
                recv_sem=recv_sems.at[i], device_id=(x, y, 1 - c), device_id_type=MESH).wait_recv()

    return pl.pallas_call(
        body, name="sibling_join", in_specs=[HBM] * n, out_specs=[HBM] * n,
        out_shape=[jax.ShapeDtypeStruct(g.shape, g.dtype) for g in grads],
        input_output_aliases={i: i for i in range(n)},
        scratch_shapes=[pltpu.SemaphoreType.DMA((n,)), pltpu.SemaphoreType.DMA((n,))],
        )(*grads)


def _gather8(name, block, reduce):
    m, n = block.shape

    def body(x_ref, out_ref, *scratch):
        if reduce:
            all_ref, send_sems, recv_sems, local_sem = scratch
        else:
            all_ref = out_ref
            send_sems, recv_sems, local_sem = scratch
        x, y, c, chips = _place()
        me, sibling = (x, y, c), (x, y, 1 - c)

        def rows(px, py, pc):
            return all_ref.at[pl.ds((4 * px + 2 * py + pc) * m, m), :]

        def copy(k, blk, to, src=None):
            return pltpu.make_async_remote_copy(
                src_ref=rows(*blk) if src is None else src, dst_ref=rows(*blk),
                send_sem=send_sems.at[k], recv_sem=recv_sems.at[k], device_id=to, device_id_type=MESH)

        mine = pltpu.make_async_copy(x_ref, rows(*me), local_sem)
        mine.start()
        first = [copy(0, me, sibling, src=x_ref)]
        first += [copy(1 + j, me, (chip[0], chip[1], c), src=x_ref) for j, chip in enumerate(chips)]
        for cp in first:
            cp.start()
        passed = [copy(4 + j, (chip[0], chip[1], c), sibling) for j, chip in enumerate(chips)]
        for j, chip in enumerate(chips):
            copy(1 + j, (chip[0], chip[1], c), me).wait_recv()
            passed[j].start()
        copy(0, sibling, me).wait_recv()
        for j, chip in enumerate(chips):
            copy(4 + j, (chip[0], chip[1], 1 - c), me).wait_recv()
        for cp in first + passed:
            cp.wait_send()
        mine.wait()
        if reduce:
            acc = all_ref[pl.ds(0, m), :]
            for d in range(1, 8):
                acc = acc + all_ref[pl.ds(d * m, m), :]
            out_ref[...] = acc

    sems = [pltpu.SemaphoreType.DMA((7,)), pltpu.SemaphoreType.DMA((7,)), pltpu.SemaphoreType.DMA]
    scratch = ([pltpu.VMEM((8 * m, n), F32)] if reduce else []) + sems
    return pl.pallas_call(
        body, name=name,
        out_shape=jax.ShapeDtypeStruct((m, n) if reduce else (8 * m, n), F32),
        in_specs=[pl.BlockSpec(memory_space=pltpu.VMEM)], out_specs=pl.BlockSpec(memory_space=pltpu.VMEM),
        scratch_shapes=scratch)(block)


def _pad_rows(a, rows):
    return jnp.concatenate([a, jnp.zeros((rows - a.shape[0], a.shape[1]), a.dtype)], axis=0)


def kernel(x, p, norm_g, w_in_a, conv_w, conv_b, ln_g, ln_b, w_out_a, kv_norm_g, w_kv, k_norm_g, w_in_b, q_norm_g, w_out_b, ple_norm_g, w_ple_gate, w_ple_proj, loss_target, m_norm_g, m_w_in_a, m_conv_w, m_conv_b, m_ln_g, m_ln_b, m_w_out_a, m_kv_norm_g, m_w_kv, m_k_norm_g, m_w_in_b, m_q_norm_g, m_w_out_b, m_ple_norm_g, m_w_ple_gate, m_w_ple_proj, v_norm_g, v_w_in_a, v_conv_w, v_conv_b, v_ln_g, v_ln_b, v_w_out_a, v_kv_norm_g, v_w_kv, v_k_norm_g, v_w_in_b, v_q_norm_g, v_w_out_b, v_ple_norm_g, v_w_ple_gate, v_w_ple_proj):
    nb, seq, dm = x.shape
    t = nb * seq
    ple = p.shape[-1]
    ccs = conv_w.shape[-1]
    cc = N_CHIPS * ccs
    da = dm
    nheads = da // HEAD_DIM
    assert seq == DILATIONS[-1] * SPAN and da % 128 == 0 and ccs % 128 == 0

    core = lax.axis_index("c").astype(jnp.int32).reshape(1)
    chip = (2 * lax.axis_index("x") + lax.axis_index("y")).astype(jnp.int32)
    chip1 = chip.reshape(1)
    sum_order = jnp.concatenate([(chip1 + k) % N_CHIPS for k in range(N_CHIPS)] + [core])

    x2 = x.reshape(t, dm)
    tgt2 = loss_target.reshape(t, dm)
    p0 = p[0].reshape(t, ple)
    p1 = p[1].reshape(t, ple)

    big = [
        ("w_in_a", w_in_a[0], "col"), ("w_out_a", w_out_a[0], "row"), ("w_kv", w_kv, "col"),
        ("w_in_b", w_in_b[0], "col"), ("w_out_b", w_out_b[0], "row"),
        ("w_ple_gate0", w_ple_gate[0], "row"), ("w_ple_gate1", w_ple_gate[1], "row"),
        ("w_ple_proj0", w_ple_proj[0], "col"), ("w_ple_proj1", w_ple_proj[1], "col"),
    ]
    shard_shape = {nm: w.shape for nm, w, _ in big}
    gathered = _gather_weights([_cast_bf16("cast_" + nm, w, chip1) for nm, w, _ in big])
    W = {nm: g for (nm, _, _), g in zip(big, gathered)}

    vec_rows = 40
    small = _pad_rows(jnp.concatenate([conv_w[0], conv_b, ln_g, ln_b], axis=0), vec_rows)
    allv = _gather8("gather_conv_vectors", small, reduce=False).reshape(N_CHIPS, 2, vec_rows, ccs)[:, 0]
    allv = allv.transpose(1, 0, 2).reshape(vec_rows, cc)
    cw_full, cb_full, lg_full, lb_full = allv[:HALO], allv[31:32], allv[32:33], allv[33:34]
    cw_full = cw_full * (lax.broadcasted_iota(jnp.int32, (HALO, 1), 0) < CONV_WIDTH).astype(F32)

    tables = _rope_tables(seq)
    gain_q = jnp.tile(q_norm_g[0][:, None, :], (1, nheads, 1)).reshape(1, 3 * da)
    gain_k = jnp.tile(k_norm_g[None, :], (1, nheads))
    g0, g1 = norm_g[0:1], norm_g[1:2]
    pg0, pg1 = ple_norm_g[0:1], ple_norm_g[1:2]
    kvg = kv_norm_g[None, :]

    (u0,) = _rms_fwd("rms_u0", x2, [g0])
    pa = _mm_nn("mm_in_a", u0, W["w_in_a"], "col", out_dtype=BF16)
    conv_out, m_a = _mixa_fwd("mixa_fwd", pa, cw_full, cb_full, lg_full, lb_full, seq)
    h0 = _mm_nn("mm_out_a", m_a, W["w_out_a"], "row", resid=x2)
    (r0,) = _rms_fwd("rms_r0", h0, [pg0])
    gpre0 = _mm_nn("mm_gate0", r0, W["w_ple_gate0"], "row", out_dtype=BF16)
    pp0 = _mm_nn("mm_proj0", p0, W["w_ple_proj0"], "col", out_dtype=BF16)
    x1 = _ple_fwd("ple_fwd0", h0, gpre0, pp0)
    kvn, u1 = _rms_fwd("rms_kv_u1", x1, [kvg, g1])
    kv = _mm_nn("mm_kv", kvn, W["w_kv"], "col")
    kn = _hnr_fwd("k_norm_rope", kv, da, gain_k, tables, seq)
    pb = _mm_nn("mm_in_b", u1, W["w_in_b"], "col", out_dtype=BF16)
    qn = _hnr_fwd("q_norm_rope", pb, 3 * da, gain_q, tables, seq)
    o, lse, m_b = _attn_fwd("attn_fwd", qn, kn, kv, pb, seq)
    h1 = _mm_nn("mm_out_b", m_b, W["w_out_b"], "row", resid=x1)
    (r1,) = _rms_fwd("rms_r1", h1, [pg1])
    gpre1 = _mm_nn("mm_gate1", r1, W["w_ple_gate1"], "row", out_dtype=BF16)
    pp1 = _mm_nn("mm_proj1", p1, W["w_ple_proj1"], "col", out_dtype=BF16)
    dy, dgp1, dpp1, sq = _ple_loss("ple_loss", h1, gpre1, pp1, tgt2)
    loss = lax.psum(0.5 * sq[0, 0] / dm, ("x", "y", "c"))

    G = {}
    G["w_ple_gate1"] = _mm_tn("tn_gate1", r1, dgp1, "row", shard_shape["w_ple_gate1"])
    G["w_ple_proj1"] = _mm_tn("tn_proj1", p1, dpp1, "col", shard_shape["w_ple_proj1"])
    dr1 = _mm_nt("nt_gate1", dgp1, W["w_ple_gate1"], "row", out_dtype=BF16)
    dh1, (dpg1,) = _rms_bwd("rms_bwd_r1", h1, dy, [(pg1, dr1)])
    G["w_out_b"] = _mm_tn("tn_out_b", m_b, dh1, "row", shard_shape["w_out_b"])
    dm_b = _mm_nt("nt_out_b", dh1, W["w_out_b"], "row", out_dtype=BF16)
    d_o, dgt, dsum = _gate_bwd("gate_bwd", dm_b, o, pb)
    dq0, dq1, dq2, dk, dv = _attn_bwd("attn_bwd", qn, kn, kv, d_o, lse, dsum, seq)
    dpb, dgq = _q_bwd("q_bwd", pb, gain_q, tables, [dq0, dq1, dq2], dgt, seq)
    dkv, dgk = _k_bwd("k_bwd", kv, gain_k, tables, dk, dv, seq)
    G["w_in_b"] = _mm_tn("tn_in_b", u1, dpb, "col", shard_shape["w_in_b"])
    du1 = _mm_nt("nt_in_b", dpb, W["w_in_b"], "col", out_dtype=BF16)
    G["w_kv"] = _mm_tn("tn_kv", kvn, dkv, "col", shard_shape["w_kv"])
    dkvn = _mm_nt("nt_kv", dkv, W["w_kv"], "col", out_dtype=BF16)
    dx1, (dg1, dkvg), dgp0, dpp0 = _rms_bwd("rms_bwd_x1", x1, dh1, [(g1, du1), (kvg, dkvn)], ple=(gpre0, pp0))
    G["w_ple_gate0"] = _mm_tn("tn_gate0", r0, dgp0, "row", shard_shape["w_ple_gate0"])
    G["w_ple_proj0"] = _mm_tn("tn_proj0", p0, dpp0, "col", shard_shape["w_ple_proj0"])
    dr0 = _mm_nt("nt_gate0", dgp0, W["w_ple_gate0"], "row", out_dtype=BF16)
    dh0, (dpg0,) = _rms_bwd("rms_bwd_r0", h0, dx1, [(pg0, dr0)])
    G["w_out_a"] = _mm_tn("tn_out_a", m_a, dh0, "row", shard_shape["w_out_a"])
    dm_a = _mm_nt("nt_out_a", dh0, W["w_out_a"], "row", out_dtype=BF16)
    dc, dz, dlg, dlb, dcb = _mixa_bwd1("mixa_bwd1", conv_out, pa, dm_a, lg_full, lb_full)
    dpa, dcw = _mixa_bwd2("mixa_bwd2", dc, pa, dz, cw_full, seq)
    G["w_in_a"] = _mm_tn("tn_in_a", u0, dpa, "col", shard_shape["w_in_a"])
    du0 = _mm_nt("nt_in_a", dpa, W["w_in_a"], "col", out_dtype=BF16)
    dx, (dg0,) = _rms_bwd("rms_bwd_x", x2, dh0, [(g0, du0)])
    grad_x = dx.reshape(nb, seq, dm)

    names = [nm for nm, _, _ in big]
    recv = _pair_exchange([G[nm] for nm in names])
    sums = [_pair_sum("pair_sum_" + nm, G[nm], rc, core) for nm, rc in zip(names, recv)]
    parts = _chip_scatter(sums)
    halves = [_chip_sum("chip_sum_" + nm, sm, pt, sum_order) for nm, sm, pt in zip(names, sums, parts)]
    gfull = dict(zip(names, _sibling_join(halves)))

    def as_rows(a):
        return a.reshape(-1, dm)

    small_parts = [as_rows(dcw), as_rows(dcb), as_rows(dlg), as_rows(dlb), dg0, dg1, dkvg, dpg0, dpg1, as_rows(dgk), as_rows(dgq)]
    counts = [a.shape[0] for a in small_parts]
    total = sum(counts)
    packed = _pad_rows(jnp.concatenate(small_parts, axis=0), -(-total // 8) * 8)
    red = _gather8("reduce_small", packed, reduce=True)
    pieces, off = [], 0
    for n_ in counts:
        pieces.append(red[off:off + n_])
        off += n_
    r_dcw, r_dcb, r_dlg, r_dlb, r_g0, r_g1, r_kvg, r_pg0, r_pg1, r_gk, r_gq = pieces
    my_cols = lambda a: lax.dynamic_slice_in_dim(a.reshape(-1, cc), chip * ccs, ccs, axis=1)
    small_grads = {
        "norm_g": jnp.concatenate([r_g0, r_g1], axis=0),
        "conv_w": my_cols(r_dcw)[:CONV_WIDTH],
        "conv_b": my_cols(r_dcb), "ln_g": my_cols(r_dlg), "ln_b": my_cols(r_dlb),
        "kv_norm_g": r_kvg,
        "k_norm_g": r_gk.reshape(nheads, HEAD_DIM).sum(axis=0, keepdims=True),
        "q_norm_g": r_gq.reshape(3, nheads, HEAD_DIM).sum(axis=1),
        "ple_norm_g": jnp.concatenate([r_pg0, r_pg1], axis=0),
    }

    given = dict(norm_g=norm_g, w_in_a=w_in_a, conv_w=conv_w, conv_b=conv_b, ln_g=ln_g, ln_b=ln_b, w_out_a=w_out_a,
                 kv_norm_g=kv_norm_g, w_kv=w_kv, k_norm_g=k_norm_g, w_in_b=w_in_b, q_norm_g=q_norm_g, w_out_b=w_out_b,
                 ple_norm_g=ple_norm_g, w_ple_gate=w_ple_gate, w_ple_proj=w_ple_proj)
    mom1 = dict(norm_g=m_norm_g, w_in_a=m_w_in_a, conv_w=m_conv_w, conv_b=m_conv_b, ln_g=m_ln_g, ln_b=m_ln_b,
                w_out_a=m_w_out_a, kv_norm_g=m_kv_norm_g, w_kv=m_w_kv, k_norm_g=m_k_norm_g, w_in_b=m_w_in_b,
                q_norm_g=m_q_norm_g, w_out_b=m_w_out_b, ple_norm_g=m_ple_norm_g, w_ple_gate=m_w_ple_gate,
                w_ple_proj=m_w_ple_proj)
    mom2 = dict(norm_g=v_norm_g, w_in_a=v_w_in_a, conv_w=v_conv_w, conv_b=v_conv_b, ln_g=v_ln_g, ln_b=v_ln_b,
                w_out_a=v_w_out_a, kv_norm_g=v_kv_norm_g, w_kv=v_w_kv, k_norm_g=v_k_norm_g, w_in_b=v_w_in_b,
                q_norm_g=v_q_norm_g, w_out_b=v_w_out_b, ple_norm_g=v_ple_norm_g, w_ple_gate=v_w_ple_gate,
                w_ple_proj=v_w_ple_proj)
    order = ["norm_g", "w_in_a", "conv_w", "conv_b", "ln_g", "ln_b", "w_out_a", "kv_norm_g", "w_kv", "k_norm_g", "w_in_b",
             "q_norm_g", "w_out_b", "ple_norm_g", "w_ple_gate", "w_ple_proj"]
    grads, deltas, new_m, new_v = {}, {}, {}, {}
    for nm in order:
        shape = given[nm].shape
        if nm in ("w_ple_gate", "w_ple_proj"):
            g2 = jnp.concatenate([gfull[nm + "0"], gfull[nm + "1"]], axis=0)
        elif nm in gfull:
            g2 = gfull[nm]
        else:
            g2 = small_grads[nm]
        two_d = g2.shape
        d2, m2, v2 = _adamw("adamw_" + nm, given[nm].reshape(two_d), g2, mom1[nm].reshape(two_d), mom2[nm].reshape(two_d))
        grads[nm], deltas[nm], new_m[nm], new_v[nm] = (a.reshape(shape) for a in (g2, d2, m2, v2))

    return (loss, grad_x, *[grads[n_] for n_ in order], *[deltas[n_] for n_ in order],
            *[new_m[n_] for n_ in order], *[new_v[n_] for n_ in order])
```

```python
import functools

import jax
import jax.numpy as jnp
from jax import lax
from jax.experimental import pallas as pl
from jax.experimental.pallas import tpu as pltpu

F32 = jnp.float32
BF16 = jnp.bfloat16
MESH = pl.DeviceIdType.MESH

EPS = 1e-6
NEG_INF = -1e30
HEAD_DIM = 64
ROPE_DIM = 16
ROPE_THETA = 500000.0
CONV_WIDTH = 31
SUBLANES = 8
CONV_ROWS = 64
HALO = 32
SPAN = 128
DILATIONS = (1, 4, 16)
ADAM_LR, ADAM_B1, ADAM_B2, ADAM_EPS, ADAM_WD, ADAM_STEP = 0.001, 0.9, 0.999, 1e-08, 0.01, 10
N_CHIPS = 4
VMEM_LIMIT = 56 * 1024 * 1024


def _tile(n, target, mult=128):
    best = None
    t = mult
    while t <= min(n, target):
        if n % t == 0:
            best = t
        t += mult
    return best if best is not None else n


def _params(*sem):
    return pltpu.CompilerParams(dimension_semantics=tuple(sem) if sem else None, vmem_limit_bytes=VMEM_LIMIT)


def _sigmoid(x):
    return 1.0 / (1.0 + jnp.exp(-x))


def _seg_allsum64(x):
    tr, w = x.shape
    cw = 256 if w % 256 == 0 else 128
    n = w // cw
    ri = lax.shift_right_logical(lax.broadcasted_iota(jnp.int32, (cw, cw), 0), 6)
    ci = lax.shift_right_logical(lax.broadcasted_iota(jnp.int32, (cw, cw), 1), 6)
    ones = (ri == ci).astype(BF16)
    hi = x.astype(BF16)
    lo = (x - hi.astype(F32)).astype(BF16)

    def stack(v):
        return jnp.concatenate([v[:, j * cw:(j + 1) * cw] for j in range(n)], axis=0)

    s = (jnp.dot(stack(hi), ones, preferred_element_type=F32)
         + jnp.dot(stack(lo), ones, preferred_element_type=F32))
    return jnp.concatenate([s[j * tr:(j + 1) * tr] for j in range(n)], axis=1)


def _colsum(x):
    return jnp.sum(x, axis=0, keepdims=True)


def _shards_view(w, kind):
    return w if kind == "col" else w.reshape(1, -1, w.shape[2])


def _mm_nn(name, a, w, kind, *, out_dtype=F32, resid=None, plan=None):
    t = a.shape[0]
    w = _shards_view(w, kind)
    ns, k, c = w.shape
    n = ns * c
    tm = _tile(t, 1024, 8)
    tk = _tile(k, 2048)
    tn = _tile(c, 1024)
    nk = k // tk
    per = c // tn

    def body(*refs):
        if resid is None:
            a_ref, w_ref, o_ref = refs[:3]
        else:
            a_ref, w_ref, r_ref, o_ref = refs[:4]
        part = jnp.dot(a_ref[...].astype(BF16), w_ref[...], preferred_element_type=F32)

        def finish(out):
            if resid is not None:
                out = out + r_ref[...]
            o_ref[...] = out.astype(out_dtype)

        if nk == 1:
            finish(part)
            return
        acc = refs[-1]
        kk = pl.program_id(2)

        @pl.when(kk == 0)
        def _():
            acc[...] = part

        @pl.when(kk > 0)
        def _():
            acc[...] += part

        @pl.when(kk == nk - 1)
        def _():
            finish(acc[...])

    in_specs = [pl.BlockSpec((tm, tk), lambda i, j, kk: (i, kk)),
                pl.BlockSpec((None, tk, tn), lambda i, j, kk: (j // per, kk, j % per))]
    args = [a, w]
    if resid is not None:
        in_specs.append(pl.BlockSpec((tm, tn), lambda i, j, kk: (i, j)))
        args.append(resid)
    (out,), carried = _hosted_call(
        body, plan, name=name, grid=(t // tm, n // tn, nk), in_specs=in_specs,
        out_specs=[pl.BlockSpec((tm, tn), lambda i, j, kk: (i, j))],
        out_shape=[jax.ShapeDtypeStruct((t, n), out_dtype)],
        scratch_shapes=[pltpu.VMEM((tm, tn), F32)] if nk > 1 else [],
        args=args, sem=("parallel", "parallel", "arbitrary"))
    return out if plan is None else (out, carried)


def _mm_nt(name, d, w, kind, *, out_dtype=F32, plan=None):
    t = d.shape[0]
    w = _shards_view(w, kind)
    ns, k, c = w.shape
    n = ns * c
    tm = _tile(t, 1024, 8)
    to = _tile(k, 1024)
    tc = _tile(c, 1536)
    nc = n // tc
    per = c // tc

    def body(d_ref, w_ref, o_ref, *scratch):
        part = lax.dot_general(d_ref[...].astype(BF16), w_ref[...], (((1,), (1,)), ((), ())),
                               preferred_element_type=F32)
        if nc == 1:
            o_ref[...] = part.astype(out_dtype)
            return
        acc = scratch[0]
        kk = pl.program_id(2)

        @pl.when(kk == 0)
        def _():
            acc[...] = part

        @pl.when(kk > 0)
        def _():
            acc[...] += part

        @pl.when(kk == nc - 1)
        def _():
            o_ref[...] = acc[...].astype(out_dtype)

    (out,), carried = _hosted_call(
        body, plan, name=name, grid=(t // tm, k // to, nc),
        in_specs=[pl.BlockSpec((tm, tc), lambda i, j, kk: (i, kk)),
                  pl.BlockSpec((None, to, tc), lambda i, j, kk: (kk // per, j, kk % per))],
        out_specs=[pl.BlockSpec((tm, to), lambda i, j, kk: (i, j))],
        out_shape=[jax.ShapeDtypeStruct((t, k), out_dtype)],
        scratch_shapes=[pltpu.VMEM((tm, to), F32)] if nc > 1 else [],
        args=[d, w], sem=("parallel", "parallel", "arbitrary"))
    return out if plan is None else (out, carried)


def _mm_tn(name, a, d, kind, shard_shape):
    t, k = a.shape
    n = d.shape[1]
    ns = N_CHIPS if kind == "col" else 1
    c = n // ns
    tkm = _tile(k, 1024)
    tn = _tile(c, 1536)
    tt = _tile(t, 1024, 8)
    nt = t // tt
    per = c // tn

    def body(a_ref, d_ref, o_ref, acc):
        kk = pl.program_id(2)
        part = lax.dot_general(a_ref[...].astype(BF16), d_ref[...].astype(BF16), (((0,), (0,)), ((), ())),
                               preferred_element_type=F32)

        @pl.when(kk == 0)
        def _():
            acc[...] = part

        @pl.when(kk > 0)
        def _():
            acc[...] += part

        @pl.when(kk == nt - 1)
        def _():
            o_ref[...] = acc[...].astype(BF16)

    out = pl.pallas_call(
        body, name=name, grid=(k // tkm, n // tn, nt),
        in_specs=[pl.BlockSpec((tt, tkm), lambda i, j, kk: (kk, i)),
                  pl.BlockSpec((tt, tn), lambda i, j, kk: (kk, j))],
        out_specs=pl.BlockSpec((None, tkm, tn), lambda i, j, kk: (j // per, i, j % per)),
        out_shape=jax.ShapeDtypeStruct((ns, k, c), BF16),
        scratch_shapes=[pltpu.VMEM((tkm, tn), F32)],
        compiler_params=_params("parallel", "parallel", "arbitrary"))(a, d)
    return out.reshape((N_CHIPS,) + tuple(shard_shape))


def _row_spec(tr, w, col=0):
    return pl.BlockSpec((tr, w), lambda i: (i, col))


def _full_spec(shape):
    return pl.BlockSpec(shape, lambda i: tuple(0 for _ in shape))


def _rms_fwd(name, x, gains):
    t, dm = x.shape
    tr = _tile(t, 256, 8)
    n = len(gains)

    def body(x_ref, *refs):
        xv = x_ref[...]
        xn = xv * lax.rsqrt(jnp.mean(xv * xv, axis=-1, keepdims=True) + EPS)
        for g_ref, o_ref in zip(refs[:n], refs[n:]):
            o_ref[...] = (xn * g_ref[...]).astype(BF16)

    outs = pl.pallas_call(
        body, name=name, grid=(t // tr,),
        in_specs=[_row_spec(tr, dm)] + [_full_spec((1, dm))] * n,
        out_specs=[_row_spec(tr, dm)] * n,
        out_shape=[jax.ShapeDtypeStruct((t, dm), BF16)] * n,
        compiler_params=_params("parallel"))(x, *gains)
    return list(outs)


def _ple_bwd_math(dy, gpre, pp):
    sg = _sigmoid(gpre)
    return (dy * pp * sg * (1.0 - sg)).astype(BF16), (dy * sg).astype(BF16)


def _rms_bwd(name, x, resid, pairs, ple=None):
    t, dm = x.shape
    tr = _tile(t, 256, 8)
    n = len(pairs)
    n_in = 2 * n + (2 if ple is not None else 0)

    def body(x_ref, r_ref, *refs):
        ins, outs = refs[:n_in], refs[n_in:]
        i = pl.program_id(0)
        xv = x_ref[...]
        rs = lax.rsqrt(jnp.mean(xv * xv, axis=-1, keepdims=True) + EPS)
        xn = xv * rs
        total = r_ref[...]
        for kx in range(n):
            g_ref, du_ref = ins[2 * kx], ins[2 * kx + 1]
            dg_ref = outs[1 + kx]
            du = du_ref[...].astype(F32)

            @pl.when(i == 0)
            def _():
                dg_ref[...] = jnp.zeros_like(dg_ref)

            dg_ref[...] += _colsum(du * xn)
            dxh = du * g_ref[...]
            total = total + rs * (dxh - xn * jnp.mean(dxh * xn, axis=-1, keepdims=True))
        outs[0][...] = total
        if ple is not None:
            outs[1 + n][...], outs[2 + n][...] = _ple_bwd_math(total, ins[2 * n][...].astype(F32), ins[2 * n + 1][...].astype(F32))

    in_specs = [_row_spec(tr, dm), _row_spec(tr, dm)]
    args = [x, resid]
    for g, du in pairs:
        in_specs += [_full_spec((1, dm)), _row_spec(tr, dm)]
        args += [g, du]
    out_specs = [_row_spec(tr, dm)] + [_full_spec((1, dm))] * n
    out_shape = [jax.ShapeDtypeStruct((t, dm), F32)] + [jax.ShapeDtypeStruct((1, dm), F32)] * n
    if ple is not None:
        in_specs += [_row_spec(tr, dm)] * 2
        args += list(ple)
        out_specs += [_row_spec(tr, dm)] * 2
        out_shape += [jax.ShapeDtypeStruct((t, dm), BF16)] * 2
    outs = pl.pallas_call(
        body, name=name, grid=(t // tr,), in_specs=in_specs, out_specs=out_specs, out_shape=out_shape,
        compiler_params=_params("arbitrary"))(*args)
    if ple is not None:
        return outs[0], list(outs[1:1 + n]), outs[1 + n], outs[2 + n]
    return outs[0], list(outs[1:])


def _shifted_copies(ext, sh, rows):
    for s in range(1, SUBLANES):
        sh[s - 1] = ext[pl.ds(s, rows), :]


def _window(ext, sh, off, row0, rows, lanes):
    s = off % SUBLANES
    src = ext if s == 0 else sh.at[s - 1]
    return src[pl.ds(off - s + row0, rows), lanes]


def _mixa_fwd(name, pa, cw, cb, lg, lb, seq):
    t, w3 = pa.shape
    cc = w3 // 3
    tr = _tile(seq, 128, HALO)
    per_seq = seq // tr
    hb = tr // HALO
    lead = HALO - (CONV_WIDTH - 1)

    def body(a_ref, b_ref, z_ref, ah_ref, bh_ref, cw_ref, cb_ref, lg_ref, lb_ref, c_ref, m_ref, ext, sh):
        i = pl.program_id(0)
        gh = ah_ref[...].astype(F32) * _sigmoid(bh_ref[...].astype(F32))
        ext[pl.ds(0, HALO), :] = jnp.where((i % per_seq) == 0, 0.0, gh)
        ext[pl.ds(HALO, tr), :] = a_ref[...].astype(F32) * _sigmoid(b_ref[...].astype(F32))
        _shifted_copies(ext, sh, tr + HALO - SUBLANES)
        for lc in range(cc // 128):
            lanes = pl.ds(lc * 128, 128)
            taps = [cw_ref[pl.ds(k, 1), lanes] for k in range(CONV_WIDTH)]
            for row0 in range(0, tr, CONV_ROWS):
                acc = jnp.broadcast_to(cb_ref[:, lanes], (CONV_ROWS, 128))
                for k in range(CONV_WIDTH):
                    acc = acc + _window(ext, sh, lead + k, row0, CONV_ROWS, lanes) * taps[k]
                c_ref[pl.ds(row0, CONV_ROWS), lanes] = acc
        acc = c_ref[...]
        xc = acc - jnp.mean(acc, axis=-1, keepdims=True)
        nrm = xc * lax.rsqrt(jnp.mean(xc * xc, axis=-1, keepdims=True) + EPS)
        l = nrm * lg_ref[...] + lb_ref[...]
        z = z_ref[...].astype(F32)
        m_ref[...] = (l * _sigmoid(l) * z * _sigmoid(z)).astype(BF16)

    halo = lambda col: pl.BlockSpec((HALO, cc), lambda i: (jnp.maximum(i * hb - 1, 0), col))
    return pl.pallas_call(
        body, name=name, grid=(t // tr,),
        in_specs=[_row_spec(tr, cc, 0), _row_spec(tr, cc, 1), _row_spec(tr, cc, 2), halo(0), halo(1),
                  _full_spec((HALO, cc)), _full_spec((1, cc)), _full_spec((1, cc)), _full_spec((1, cc))],
        out_specs=[_row_spec(tr, cc), _row_spec(tr, cc)],
        out_shape=[jax.ShapeDtypeStruct((t, cc), F32), jax.ShapeDtypeStruct((t, cc), BF16)],
        scratch_shapes=[pltpu.VMEM((tr + HALO, cc), F32), pltpu.VMEM((SUBLANES - 1, tr + HALO - SUBLANES, cc), F32)],
        compiler_params=_params("parallel"))(pa, pa, pa, pa, pa, cw, cb, lg, lb)


def _mixa_bwd1(name, c, pa, dm, lg, lb, plan=None):
    t, cc = c.shape
    tr = _tile(t, 128, 8)

    def body(c_ref, z_ref, dm_ref, lg_ref, lb_ref, dc_ref, dz_ref, dlg_ref, dlb_ref, dcb_ref):
        i = pl.program_id(0)
        cv = c_ref[...]
        xc = cv - jnp.mean(cv, axis=-1, keepdims=True)
        rs = lax.rsqrt(jnp.mean(xc * xc, axis=-1, keepdims=True) + EPS)
        nrm = xc * rs
        l = nrm * lg_ref[...] + lb_ref[...]
        z = z_ref[...].astype(F32)
        sl, sz = _sigmoid(l), _sigmoid(z)
        dmv = dm_ref[...].astype(F32)
        ds = dmv * (z * sz)
        dzz = dmv * (l * sl)
        dz_ref[...] = (dzz * (sz * (1.0 + z * (1.0 - sz)))).astype(BF16)
        dl = ds * (sl * (1.0 + l * (1.0 - sl)))
        dn = dl * lg_ref[...]
        dc = rs * (dn - jnp.mean(dn, axis=-1, keepdims=True) - nrm * jnp.mean(dn * nrm, axis=-1, keepdims=True))
        dc_ref[...] = dc

        @pl.when(i == 0)
        def _():
            dlg_ref[...] = jnp.zeros_like(dlg_ref)
            dlb_ref[...] = jnp.zeros_like(dlb_ref)
            dcb_ref[...] = jnp.zeros_like(dcb_ref)

        dlg_ref[...] += _colsum(dl * nrm)
        dlb_ref[...] += _colsum(dl)
        dcb_ref[...] += _colsum(dc)

    vec = jax.ShapeDtypeStruct((1, cc), F32)
    outs, carried = _hosted_call(
        body, plan, name=name, grid=(t // tr,),
        in_specs=[_row_spec(tr, cc), _row_spec(tr, cc, 2), _row_spec(tr, cc), _full_spec((1, cc)), _full_spec((1, cc))],
        out_specs=[_row_spec(tr, cc), _row_spec(tr, cc)] + [_full_spec((1, cc))] * 3,
        out_shape=[jax.ShapeDtypeStruct((t, cc), F32), jax.ShapeDtypeStruct((t, cc), BF16), vec, vec, vec],
        scratch_shapes=[], args=[c, pa, dm, lg, lb], sem=("arbitrary",))
    return outs if plan is None else (outs, carried)


def _mixa_bwd2(name, dc, pa, dz, cw, seq):
    t, cc = dc.shape
    tr = _tile(seq, 128, HALO)
    per_seq = seq // tr
    hb = tr // HALO
    steps = t // tr
    last_halo = t // HALO - 1

    def body(dc_ref, dcn_ref, a_ref, b_ref, dz_ref, cw_ref, dp_ref, dcw_ref, ext, sh, sums):
        i = pl.program_id(0)
        ext[pl.ds(0, tr), :] = dc_ref[...]
        ext[pl.ds(tr, HALO), :] = jnp.where((i % per_seq) == per_seq - 1, 0.0, dcn_ref[...])
        _shifted_copies(ext, sh, tr + HALO - SUBLANES)

        @pl.when(i == 0)
        def _():
            sums[...] = jnp.zeros_like(sums)
            dcw_ref[...] = jnp.zeros_like(dcw_ref)

        av = a_ref[...].astype(F32)
        sb = _sigmoid(b_ref[...].astype(F32))
        glu = av * sb
        dglu = jnp.zeros((tr, cc), F32)
        for k in range(CONV_WIDTH):
            wd = _window(ext, sh, CONV_WIDTH - 1 - k, 0, tr, slice(None))
            dglu = dglu + wd * cw_ref[pl.ds(k, 1), :]
            sums[pl.ds(k * SUBLANES, SUBLANES), :] += (wd * glu).reshape(tr // SUBLANES, SUBLANES, cc).sum(axis=0)
        dp_ref[:, pl.ds(0, cc)] = (dglu * sb).astype(BF16)
        dp_ref[:, pl.ds(cc, cc)] = (dglu * av * sb * (1.0 - sb)).astype(BF16)
        dp_ref[:, pl.ds(2 * cc, cc)] = dz_ref[...]

        @pl.when(i == steps - 1)
        def _():
            for k in range(CONV_WIDTH):
                dcw_ref[pl.ds(k, 1), :] = _colsum(sums[pl.ds(k * SUBLANES, SUBLANES), :])

    nxt = pl.BlockSpec((HALO, cc), lambda i: (jnp.minimum((i + 1) * hb, last_halo), 0))
    return pl.pallas_call(
        body, name=name, grid=(steps,),
        in_specs=[_row_spec(tr, cc), nxt, _row_spec(tr, cc, 0), _row_spec(tr, cc, 1), _row_spec(tr, cc),
                  _full_spec((HALO, cc))],
        out_specs=[_row_spec(tr, 3 * cc), _full_spec((HALO, cc))],
        out_shape=[jax.ShapeDtypeStruct((t, 3 * cc), BF16), jax.ShapeDtypeStruct((HALO, cc), F32)],
        scratch_shapes=[pltpu.VMEM((tr + HALO, cc), F32), pltpu.VMEM((SUBLANES - 1, tr + HALO - SUBLANES, cc), F32),
                        pltpu.VMEM((HALO * SUBLANES, cc), F32)],
        compiler_params=_params("arbitrary"))(dc, dc, pa, pa, dz, cw)


def _ple_fwd(name, h, gpre, pp):
    t, dm = h.shape
    tr = _tile(t, 256, 8)

    def body(h_ref, g_ref, p_ref, o_ref):
        o_ref[...] = h_ref[...] + _sigmoid(g_ref[...].astype(F32)) * p_ref[...].astype(F32)

    return pl.pallas_call(
        body, name=name, grid=(t // tr,), in_specs=[_row_spec(tr, dm)] * 3, out_specs=_row_spec(tr, dm),
        out_shape=jax.ShapeDtypeStruct((t, dm), F32), compiler_params=_params("parallel"))(h, gpre, pp)


def _ple_loss(name, h, gpre, pp, target):
    t, dm = h.shape
    tr = _tile(t, 256, 8)

    def body(h_ref, g_ref, p_ref, t_ref, dy_ref, dg_ref, dp_ref, sq_ref):
        i = pl.program_id(0)
        gpre_v, pp_v = g_ref[...].astype(F32), p_ref[...].astype(F32)
        err = h_ref[...] + _sigmoid(gpre_v) * pp_v - t_ref[...]
        dy = err * (1.0 / dm)
        dy_ref[...] = dy
        dg_ref[...], dp_ref[...] = _ple_bwd_math(dy, gpre_v, pp_v)

        @pl.when(i == 0)
        def _():
            sq_ref[...] = jnp.zeros_like(sq_ref)

        sq_ref[...] += jnp.sum(jnp.sum(err * err, axis=1, keepdims=True), axis=0, keepdims=True)

    return pl.pallas_call(
        body, name=name, grid=(t // tr,), in_specs=[_row_spec(tr, dm)] * 4,
        out_specs=[_row_spec(tr, dm)] * 3 + [_full_spec((1, 1))],
        out_shape=[jax.ShapeDtypeStruct((t, dm), F32)] + [jax.ShapeDtypeStruct((t, dm), BF16)] * 2
        + [jax.ShapeDtypeStruct((1, 1), F32)],
        compiler_params=_params("arbitrary"))(h, gpre, pp, target)


def _rope_tables(seq):
    half = ROPE_DIM // 2
    inv = ROPE_THETA ** (-jnp.arange(half, dtype=F32) * (2.0 / ROPE_DIM))
    ang = jnp.arange(seq).astype(F32)[:, None] * inv[None, :]
    cos, sin = jnp.cos(ang), jnp.sin(ang)
    rest = HEAD_DIM - ROPE_DIM
    one = jnp.ones((seq, rest), F32)
    zero = jnp.zeros((seq, rest), F32)
    zh = jnp.zeros((seq, half), F32)
    tc = jnp.concatenate([cos, cos, one], axis=1)
    ta = jnp.concatenate([-sin, zh, zero], axis=1)
    tb = jnp.concatenate([zh, sin, zero], axis=1)
    return [jnp.tile(tb_, (1, 128 // HEAD_DIM)) for tb_ in (tc, ta, tb)]


def _wide(tab_ref, w):
    return jnp.tile(tab_ref[...], (1, w // 128))


def _hnr_fwd(name, src, width, gain, tables, seq):
    t = src.shape[0]
    tr = _tile(seq, 256, 8)
    per_seq = seq // tr

    def body(x_ref, g_ref, tc_ref, ta_ref, tb_ref, o_ref):
        xv = x_ref[...].astype(F32)
        rs = lax.rsqrt(_seg_allsum64(xv * xv) * (1.0 / HEAD_DIM) + EPS)
        y = xv * rs * g_ref[...]
        o_ref[...] = (y * _wide(tc_ref, width) + pltpu.roll(y, width - ROPE_DIM // 2, 1) * _wide(ta_ref, width)
                      + pltpu.roll(y, ROPE_DIM // 2, 1) * _wide(tb_ref, width))

    tab = pl.BlockSpec((tr, 128), lambda i: (i % per_seq, 0))
    return pl.pallas_call(
        body, name=name, grid=(t // tr,),
        in_specs=[_row_spec(tr, width), _full_spec((1, width)), tab, tab, tab],
        out_specs=_row_spec(tr, width), out_shape=jax.ShapeDtypeStruct((t, width), F32),
        compiler_params=_params("parallel"))(src, gain, *tables)


def _hnr_bwd_math(xv, gain, dout, tc, ta, tb, width):
    dy = dout * tc + pltpu.roll(dout * ta, ROPE_DIM // 2, 1) + pltpu.roll(dout * tb, width - ROPE_DIM // 2, 1)
    rs = lax.rsqrt(_seg_allsum64(xv * xv) * (1.0 / HEAD_DIM) + EPS)
    xn = xv * rs
    dyh = dy * gain
    dx = rs * (dyh - xn * (_seg_allsum64(dyh * xn) * (1.0 / HEAD_DIM)))
    return dx, _colsum(dy * xn)


def _q_bwd(name, p1, gain, tables, dqs, dgt, seq):
    t, w4 = p1.shape
    da = w4 // 4
    width = 3 * da
    tr = _tile(seq, 128, 8)
    per_seq = seq // tr

    def body(x_ref, g_ref, tc_ref, ta_ref, tb_ref, d0_ref, d1_ref, d2_ref, dgt_ref, o_ref, dg_ref):
        i = pl.program_id(0)
        dout = jnp.concatenate([d0_ref[...], d1_ref[...], d2_ref[...]], axis=1)
        dx, dg = _hnr_bwd_math(x_ref[...].astype(F32), g_ref[...], dout, _wide(tc_ref, width), _wide(ta_ref, width),
                               _wide(tb_ref, width), width)

        @pl.when(i == 0)
        def _():
            dg_ref[...] = jnp.zeros_like(dg_ref)

        dg_ref[...] += dg
        o_ref[:, pl.ds(0, width)] = dx.astype(BF16)
        o_ref[:, pl.ds(width, da)] = dgt_ref[...]

    tab = pl.BlockSpec((tr, 128), lambda i: (i % per_seq, 0))
    return pl.pallas_call(
        body, name=name, grid=(t // tr,),
        in_specs=[_row_spec(tr, width), _full_spec((1, width)), tab, tab, tab] + [_row_spec(tr, da)] * 4,
        out_specs=[_row_spec(tr, w4), _full_spec((1, width))],
        out_shape=[jax.ShapeDtypeStruct((t, w4), BF16), jax.ShapeDtypeStruct((1, width), F32)],
        compiler_params=_params("arbitrary"))(p1, gain, *tables, *dqs, dgt)


def _k_bwd(name, kv, gain, tables, dk, dv, seq):
    t, w2 = kv.shape
    da = w2 // 2
    tr = _tile(seq, 256, 8)
    per_seq = seq // tr

    def body(x_ref, g_ref, tc_ref, ta_ref, tb_ref, dk_ref, dv_ref, o_ref, dg_ref):
        i = pl.program_id(0)
        dx, dg = _hnr_bwd_math(x_ref[...], g_ref[...], dk_ref[...], _wide(tc_ref, da), _wide(ta_ref, da),
                               _wide(tb_ref, da), da)

        @pl.when(i == 0)
        def _():
            dg_ref[...] = jnp.zeros_like(dg_ref)

        dg_ref[...] += dg
        o_ref[:, pl.ds(0, da)] = dx.astype(BF16)
        o_ref[:, pl.ds(da, da)] = dv_ref[...].astype(BF16)

    tab = pl.BlockSpec((tr, 128), lambda i: (i % per_seq, 0))
    return pl.pallas_call(
        body, name=name, grid=(t // tr,),
        in_specs=[_row_spec(tr, da), _full_spec((1, da)), tab, tab, tab] + [_row_spec(tr, da)] * 2,
        out_specs=[_row_spec(tr, w2), _full_spec((1, da))],
        out_shape=[jax.ShapeDtypeStruct((t, w2), BF16), jax.ShapeDtypeStruct((1, da), F32)],
        compiler_params=_params("arbitrary"))(kv, gain, *tables, dk, dv)


def _unit_rows(ref, dil, r, blk):
    start = r + dil * SPAN * blk
    if dil == 1:
        return ref[pl.ds(start, SPAN), :]
    return ref[pl.ds(start, SPAN, stride=dil), :]


def _store_rows(ref, dil, r, blk, val):
    start = r + dil * SPAN * blk
    if dil == 1:
        ref[pl.ds(start, SPAN), :] = val
    else:
        ref[pl.ds(start, SPAN, stride=dil), :] = val


def _band_mask(with_prev):
    nk = 2 * SPAN if with_prev else SPAN
    qi = lax.broadcasted_iota(jnp.int32, (SPAN, nk), 0)
    kj = lax.broadcasted_iota(jnp.int32, (SPAN, nk), 1)
    if with_prev:
        return (kj >= qi) & (kj <= qi + SPAN)
    return kj <= qi


_NT = (((1,), (1,)), ((), ()))
_TN = (((0,), (0,)), ((), ()))


def _group_fwd(q_ref, k_ref, v_ref, o_ref, l_ref, dil, seq):
    nblk = seq // (dil * SPAN)
    scale = HEAD_DIM ** -0.5
    for r in range(dil):
        kc = vc = None
        for blk in range(nblk):
            kp, vp = kc, vc
            q = _unit_rows(q_ref, dil, r, blk) * scale
            kc = _unit_rows(k_ref, dil, r, blk)
            vc = _unit_rows(v_ref, dil, r, blk)
            with_prev = blk > 0
            kcat = jnp.concatenate([kp, kc], axis=0) if with_prev else kc
            vcat = jnp.concatenate([vp, vc], axis=0) if with_prev else vc
            mask = _band_mask(with_prev)
            outs, lses = [], []
            for hh in range(128 // HEAD_DIM):
                sl = slice(hh * HEAD_DIM, (hh + 1) * HEAD_DIM)
                s = lax.dot_general(q[:, sl].astype(BF16), kcat[:, sl].astype(BF16), _NT, preferred_element_type=F32)
                s = jnp.where(mask, s, NEG_INF)
                mx = jnp.max(s, axis=-1, keepdims=True)
                p = jnp.exp(s - mx)
                den = jnp.sum(p, axis=-1, keepdims=True)
                o = jnp.dot(p.astype(BF16), vcat[:, sl].astype(BF16), preferred_element_type=F32) / den
                outs.append(o)
                lses.append(jnp.broadcast_to(mx + jnp.log(den), (SPAN, HEAD_DIM)))
            _store_rows(o_ref, dil, r, blk, jnp.concatenate(outs, axis=1))
            _store_rows(l_ref, dil, r, blk, jnp.concatenate(lses, axis=1))


def _attn_fwd(name, qn, kn, kv, p1, seq):
    t, da = kn.shape
    hp = da // 128
    ng = len(DILATIONS)

    def body(q0_ref, q1_ref, q2_ref, k_ref, v_ref, g_ref, o_ref, l_ref, m_ref, og, lg):
        for g, q_ref in enumerate((q0_ref, q1_ref, q2_ref)):
            _group_fwd(q_ref, k_ref, v_ref, og.at[g], lg.at[g], DILATIONS[g], seq)
        a0, a1, a2 = lg[0], lg[1], lg[2]
        mx = jnp.maximum(jnp.maximum(a0, a1), a2)
        e0, e1, e2 = jnp.exp(a0 - mx), jnp.exp(a1 - mx), jnp.exp(a2 - mx)
        den = e0 + e1 + e2
        o = (e0 * og[0] + e1 * og[1] + e2 * og[2]) / den
        o_ref[...] = o
        l_ref[...] = mx + jnp.log(den)
        gt = g_ref[...].astype(F32)
        m_ref[...] = (o * gt * _sigmoid(gt)).astype(BF16)

    blk_spec = lambda off: pl.BlockSpec((seq, 128), lambda b, h: (b, off + h))
    return pl.pallas_call(
        body, name=name, grid=(t // seq, hp),
        in_specs=[blk_spec(0), blk_spec(hp), blk_spec(2 * hp), blk_spec(0), blk_spec(hp), blk_spec(3 * hp)],
        out_specs=[blk_spec(0)] * 3,
        out_shape=[jax.ShapeDtypeStruct((t, da), F32)] * 2 + [jax.ShapeDtypeStruct((t, da), BF16)],
        scratch_shapes=[pltpu.VMEM((ng, seq, 128), F32), pltpu.VMEM((ng, seq, 128), F32)],
        compiler_params=_params("parallel", "parallel"))(qn, qn, qn, kn, kv, p1)


def _group_bwd(q_ref, k_ref, v_ref, do_ref, l_ref, d_ref, dq_ref, dk_ref, dv_ref, dil, seq, first):
    nblk = seq // (dil * SPAN)
    scale = HEAD_DIM ** -0.5
    nh = 128 // HEAD_DIM

    def put(ref, r, blk, val):
        if not first:
            val = val + _unit_rows(ref, dil, r, blk)
        _store_rows(ref, dil, r, blk, val)

    for r in range(dil):
        kc = vc = None
        pend_k = pend_v = None
        for blk in range(nblk):
            kp, vp = kc, vc
            q = _unit_rows(q_ref, dil, r, blk) * scale
            kc = _unit_rows(k_ref, dil, r, blk)
            vc = _unit_rows(v_ref, dil, r, blk)
            dov = _unit_rows(do_ref, dil, r, blk)
            lrow = _unit_rows(l_ref, dil, r, blk)
            drow = _unit_rows(d_ref, dil, r, blk)
            with_prev = blk > 0
            kcat = jnp.concatenate([kp, kc], axis=0) if with_prev else kc
            vcat = jnp.concatenate([vp, vc], axis=0) if with_prev else vc
            mask = _band_mask(with_prev)
            dqs, dkcs, dvcs = [], [], []
            for hh in range(nh):
                sl = slice(hh * HEAD_DIM, (hh + 1) * HEAD_DIM)
                qh = q[:, sl].astype(BF16)
                doh = dov[:, sl].astype(BF16)
                kh = kcat[:, sl].astype(BF16)
                s = lax.dot_general(qh, kh, _NT, preferred_element_type=F32)
                p = jnp.where(mask, jnp.exp(s - lrow[:, hh * HEAD_DIM:hh * HEAD_DIM + 1]), 0.0)
                dp = lax.dot_general(doh, vcat[:, sl].astype(BF16), _NT, preferred_element_type=F32)
                ds = (p * (dp - drow[:, hh * HEAD_DIM:hh * HEAD_DIM + 1])).astype(BF16)
                dqs.append(jnp.dot(ds, kh, preferred_element_type=F32) * scale)
                dkcs.append(lax.dot_general(ds, qh, _TN, preferred_element_type=F32))
                dvcs.append(lax.dot_general(p.astype(BF16), doh, _TN, preferred_element_type=F32))
            _store_rows(dq_ref, dil, r, blk, jnp.concatenate(dqs, axis=1))
            dkcat = jnp.concatenate(dkcs, axis=1)
            dvcat = jnp.concatenate(dvcs, axis=1)
            if with_prev:
                put(dk_ref, r, blk - 1, pend_k + dkcat[:SPAN])
                put(dv_ref, r, blk - 1, pend_v + dvcat[:SPAN])
                pend_k, pend_v = dkcat[SPAN:], dvcat[SPAN:]
            else:
                pend_k, pend_v = dkcat, dvcat
        put(dk_ref, r, nblk - 1, pend_k)
        put(dv_ref, r, nblk - 1, pend_v)


def _attn_bwd(name, qn, kn, kv, do, lse, dsum, seq):
    t, da = kn.shape
    hp = da // 128

    def body(q0_ref, q1_ref, q2_ref, k_ref, v_ref, do_ref, l_ref, d_ref, dq0_ref, dq1_ref, dq2_ref, dk_ref, dv_ref):
        groups = ((q0_ref, dq0_ref), (q1_ref, dq1_ref), (q2_ref, dq2_ref))
        for g, (q_ref, dq_ref) in enumerate(groups):
            _group_bwd(q_ref, k_ref, v_ref, do_ref, l_ref, d_ref, dq_ref, dk_ref, dv_ref, DILATIONS[g], seq, g == 0)

    blk_spec = lambda off: pl.BlockSpec((seq, 128), lambda b, h: (b, off + h))
    return pl.pallas_call(
        body, name=name, grid=(t // seq, hp),
        in_specs=[blk_spec(0), blk_spec(hp), blk_spec(2 * hp), blk_spec(0), blk_spec(hp), blk_spec(0), blk_spec(0), blk_spec(0)],
        out_specs=[blk_spec(0)] * 5,
        out_shape=[jax.ShapeDtypeStruct((t, da), F32)] * 5,
        compiler_params=_params("parallel", "parallel"))(qn, qn, qn, kn, kv, do, lse, dsum)


def _gate_bwd(name, dm, o, p1):
    t, da = o.shape
    tr = _tile(t, 256, 8)

    def body(dm_ref, o_ref, g_ref, do_ref, dg_ref, ds_ref):
        g = g_ref[...].astype(F32)
        sg = _sigmoid(g)
        dmv, ov = dm_ref[...].astype(F32), o_ref[...]
        do = dmv * (g * sg)
        do_ref[...] = do
        dg_ref[...] = (dmv * ov * (sg * (1.0 + g * (1.0 - sg)))).astype(BF16)
        ds_ref[...] = _seg_allsum64(do * ov)

    return pl.pallas_call(
        body, name=name, grid=(t // tr,),
        in_specs=[_row_spec(tr, da), _row_spec(tr, da), _row_spec(tr, da, 3)],
        out_specs=[_row_spec(tr, da)] * 3,
        out_shape=[jax.ShapeDtypeStruct((t, da), F32), jax.ShapeDtypeStruct((t, da), BF16), jax.ShapeDtypeStruct((t, da), F32)],
        compiler_params=_params("parallel"))(dm, o, p1)


def _cast_bf16(name, w2d, chip):
    r, c = w2d.shape
    tr = _tile(r, 256, 16)

    def body(chip_ref, x_ref, o_ref):
        o_ref[...] = x_ref[...].astype(BF16)

    grid_spec = pltpu.PrefetchScalarGridSpec(
        num_scalar_prefetch=1, grid=(r // tr,),
        in_specs=[pl.BlockSpec((tr, c), lambda i, m: (i, 0))],
        out_specs=pl.BlockSpec((None, tr, c), lambda i, m: (m[0], i, 0)))
    return pl.pallas_call(
        body, name=name, grid_spec=grid_spec, out_shape=jax.ShapeDtypeStruct((N_CHIPS, r, c), BF16),
        compiler_params=_params("parallel"))(chip, w2d)


def _adamw(name, w, g, m, v):
    r, c = w.shape
    tr = _tile(r, 256, 8)
    c1 = 1.0 - ADAM_B1 ** ADAM_STEP
    c2 = 1.0 - ADAM_B2 ** ADAM_STEP

    def body(w_ref, g_ref, m_ref, v_ref, d_ref, nm_ref, nv_ref):
        gv = g_ref[...]
        nm = ADAM_B1 * m_ref[...] + (1.0 - ADAM_B1) * gv
        nv = ADAM_B2 * v_ref[...] + (1.0 - ADAM_B2) * (gv * gv)
        nm_ref[...] = nm
        nv_ref[...] = nv
        d_ref[...] = -ADAM_LR * ((nm / c1) / (jnp.sqrt(nv / c2) + ADAM_EPS) + ADAM_WD * w_ref[...])

    sds = jax.ShapeDtypeStruct((r, c), F32)
    return pl.pallas_call(
        body, name=name, grid=(r // tr,), in_specs=[_row_spec(tr, c)] * 4, out_specs=[_row_spec(tr, c)] * 3,
        out_shape=[sds] * 3, compiler_params=_params("parallel"))(w, g, m, v)


def _pair_sum(name, gd, recv, core):
    _, r, c = gd.shape
    rh = r // 2
    tr = _tile(rh, 256, 8)
    nrt = rh // tr

    def body(c_ref, a_ref, b_ref, o_ref):
        o_ref[...] = (a_ref[...].astype(F32) + b_ref[...].astype(F32)).astype(BF16)

    grid_spec = pltpu.PrefetchScalarGridSpec(
        num_scalar_prefetch=1, grid=(N_CHIPS, nrt),
        in_specs=[pl.BlockSpec((None, tr, c), lambda j, i, cr: (j, cr[0] * nrt + i, 0)),
                  pl.BlockSpec((None, tr, c), lambda j, i, cr: (j, i, 0))],
        out_specs=pl.BlockSpec((None, tr, c), lambda j, i, cr: (j, i, 0)))
    return pl.pallas_call(
        body, name=name, grid_spec=grid_spec, out_shape=jax.ShapeDtypeStruct((N_CHIPS, rh, c), BF16),
        compiler_params=_params("parallel", "parallel"))(core, gd, recv)


def _chip_sum(name, sums, parts, order):
    _, rh, c = parts.shape
    tr = _tile(rh, 256, 16)
    nrt = rh // tr

    def body(o_ref_, s_ref, p1_ref, p2_ref, p3_ref, o_ref):
        acc = s_ref[...].astype(F32)
        for p_ref in (p1_ref, p2_ref, p3_ref):
            acc = acc + p_ref[...].astype(F32)
        o_ref[...] = acc

    slot = lambda k: pl.BlockSpec((None, tr, c), lambda i, o: (o[k], i, 0))
    grid_spec = pltpu.PrefetchScalarGridSpec(
        num_scalar_prefetch=1, grid=(nrt,),
        in_specs=[slot(0), slot(1), slot(2), slot(3)],
        out_specs=pl.BlockSpec((tr, c), lambda i, o: (o[N_CHIPS] * nrt + i, 0)))
    return pl.pallas_call(
        body, name=name, grid_spec=grid_spec, out_shape=jax.ShapeDtypeStruct((2 * rh, c), F32),
        compiler_params=_params("parallel"))(order, sums, parts, parts, parts)


HBM = pl.BlockSpec(memory_space=pl.ANY)


def _place():
    x, y, c = lax.axis_index("x"), lax.axis_index("y"), lax.axis_index("c")
    chips = [(1 - x, y), (x, 1 - y), (1 - x, 1 - y)]
    return x, y, c, chips


def _half(ref, hc):
    rows = ref.shape[0] // 2
    return ref.at[pl.ds(hc * rows, rows)]


class _Plan:
    def __init__(self, ins, out_shapes, aliases, n_sems, start, finish):
        self.ins, self.out_shapes, self.aliases, self.n_sems = list(ins), list(out_shapes), dict(aliases), n_sems
        self.start, self.finish = start, finish


def _gather_plan(shards):
    n = len(shards)

    def ici(place, outs, send_sems, recv_sems, i, k, slot):
        x, y, c, chips = place
        half = _half(outs[i].at[slot], c)
        return pltpu.make_async_remote_copy(
            src_ref=half, dst_ref=half, send_sem=send_sems.at[6 * i + k], recv_sem=recv_sems.at[6 * i + k],
            device_id=(chips[k][0], chips[k][1], c), device_id_type=MESH)

    def d2d(place, outs, send_sems, recv_sems, i, k, slot, hc):
        x, y, c, chips = place
        half = _half(outs[i].at[slot], hc)
        return pltpu.make_async_remote_copy(
            src_ref=half, dst_ref=half, send_sem=send_sems.at[6 * i + 3 + k], recv_sem=recv_sems.at[6 * i + 3 + k],
            device_id=(x, y, 1 - c), device_id_type=MESH)

    def start(place, ins, outs, send_sems, recv_sems):
        x, y, c, chips = place
        for i in range(n):
            for k in range(3):
                ici(place, outs, send_sems, recv_sems, i, k, 2 * x + y).start()

    def finish(place, ins, outs, send_sems, recv_sems):
        x, y, c, chips = place
        theirs = [2 * chip[0] + chip[1] for chip in chips]
        for i in range(n):
            for k in range(3):
                ici(place, outs, send_sems, recv_sems, i, k, theirs[k]).wait_recv()
                d2d(place, outs, send_sems, recv_sems, i, k, theirs[k], c).start()
        for i in range(n):
            for k in range(3):
                d2d(place, outs, send_sems, recv_sems, i, k, theirs[k], 1 - c).wait_recv()
        for i in range(n):
            for k in range(3):
                ici(place, outs, send_sems, recv_sems, i, k, 2 * x + y).wait_send()
                d2d(place, outs, send_sems, recv_sems, i, k, theirs[k], c).wait_send()

    return _Plan(shards, [jax.ShapeDtypeStruct(s.shape, s.dtype) for s in shards], {i: i for i in range(n)}, 6 * n,
                 start, finish)


def _scatter_plan(sums):
    n = len(sums)

    def copy(place, ins, outs, send_sems, recv_sems, i, k, src_slot, dst_slot):
        x, y, c, chips = place
        return pltpu.make_async_remote_copy(
            src_ref=ins[i].at[src_slot], dst_ref=outs[i].at[dst_slot],
            send_sem=send_sems.at[3 * i + k], recv_sem=recv_sems.at[3 * i + k],
            device_id=(chips[k][0], chips[k][1], c), device_id_type=MESH)

    def start(place, ins, outs, send_sems, recv_sems):
        x, y, c, chips = place
        for i in range(n):
            for k, chip in enumerate(chips):
                copy(place, ins, outs, send_sems, recv_sems, i, k, 2 * chip[0] + chip[1], 2 * x + y).start()

    def finish(place, ins, outs, send_sems, recv_sems):
        x, y, c, chips = place
        for i in range(n):
            for k, chip in enumerate(chips):
                theirs = 2 * chip[0] + chip[1]
                copy(place, ins, outs, send_sems, recv_sems, i, k, theirs, 2 * x + y).wait_send()
                copy(place, ins, outs, send_sems, recv_sems, i, k, theirs, theirs).wait_recv()

    return _Plan(sums, [jax.ShapeDtypeStruct(s.shape, s.dtype) for s in sums], {}, 3 * n, start, finish)


def _hosted_call(body, plan, *, name, grid, in_specs, out_specs, out_shape, scratch_shapes, args, sem):
    in_specs, out_specs, out_shape, scratch_shapes = list(in_specs), list(out_specs), list(out_shape), list(scratch_shapes)
    if plan is None:
        res = pl.pallas_call(body, name=name, grid=grid, in_specs=in_specs, out_specs=out_specs, out_shape=out_shape,
                             scratch_shapes=scratch_shapes, compiler_params=_params(*sem))(*args)
        return list(res), []
    n_in, n_out, n_scr = len(in_specs), len(out_specs), len(scratch_shapes)
    p_in, p_out = len(plan.ins), len(plan.out_shapes)

    def hosted(*refs):
        refs = list(refs)
        ins, pins = refs[:n_in], refs[n_in:n_in + p_in]
        outs = refs[n_in + p_in:n_in + p_in + n_out]
        pouts = refs[n_in + p_in + n_out:n_in + p_in + n_out + p_out]
        scr = refs[n_in + p_in + n_out + p_out:n_in + p_in + n_out + p_out + n_scr]
        send_sems, recv_sems = refs[-2:]
        place = _place()
        ids = [pl.program_id(d) for d in range(len(grid))]
        first = functools.reduce(jnp.logical_and, [i == 0 for i in ids])
        last = functools.reduce(jnp.logical_and, [i == g - 1 for i, g in zip(ids, grid)])

        @pl.when(first)
        def _():
            plan.start(place, pins, pouts, send_sems, recv_sems)

        body(*ins, *outs, *scr)

        @pl.when(last)
        def _():
            plan.finish(place, pins, pouts, send_sems, recv_sems)

    res = pl.pallas_call(
        hosted, name=name, grid=grid, in_specs=in_specs + [HBM] * p_in, out_specs=out_specs + [HBM] * p_out,
        out_shape=out_shape + plan.out_shapes,
        input_output_aliases={n_in + i: n_out + o for i, o in plan.aliases.items()},
        scratch_shapes=scratch_shapes + [pltpu.SemaphoreType.DMA((plan.n_sems,)), pltpu.SemaphoreType.DMA((plan.n_sems,))],
        compiler_params=_params(*(("arbitrary",) * len(grid))))(*args, *plan.ins)
    return list(res[:n_out]), list(res[n_out:])


def _run_plan(name, plan):
    p_in = len(plan.ins)

    def body(*refs):
        ins, outs = refs[:p_in], refs[p_in:p_in + len(plan.out_shapes)]
        send_sems, recv_sems = refs[-2:]
        place = _place()
        plan.start(place, ins, outs, send_sems, recv_sems)
        plan.finish(place, ins, outs, send_sems, recv_sems)

    return pl.pallas_call(
        body, name=name, in_specs=[HBM] * p_in, out_specs=[HBM] * len(plan.out_shapes), out_shape=plan.out_shapes,
        input_output_aliases=plan.aliases,
        scratch_shapes=[pltpu.SemaphoreType.DMA((plan.n_sems,)), pltpu.SemaphoreType.DMA((plan.n_sems,))],
        )(*plan.ins)


def _pair_exchange(name, grads):
    n = len(grads)

    def body(*refs):
        ins, outs = refs[:n], refs[n:2 * n]
        send_sems, recv_sems = refs[2 * n:]
        x, y, c, _ = _place()
        cps = []
        for i in range(n):
            rows = ins[i].shape[1] // 2
            cp = pltpu.make_async_remote_copy(
                src_ref=ins[i].at[:, pl.ds((1 - c) * rows, rows), :], dst_ref=outs[i],
                send_sem=send_sems.at[i], recv_sem=recv_sems.at[i], device_id=(x, y, 1 - c), device_id_type=MESH)
            cp.start()
            cps.append(cp)
        for cp in cps:
            cp.wait()

    return pl.pallas_call(
        body, name=name, in_specs=[HBM] * n, out_specs=[HBM] * n,
        out_shape=[jax.ShapeDtypeStruct((N_CHIPS, g.shape[1] // 2, g.shape[2]), g.dtype) for g in grads],
        scratch_shapes=[pltpu.SemaphoreType.DMA((n,)), pltpu.SemaphoreType.DMA((n,))],
        )(*grads)


def _sibling_join(grads):
    n = len(grads)

    def body(*refs):
        outs = refs[n:2 * n]
        send_sems, recv_sems = refs[2 * n:]
        x, y, c, _ = _place()
        cps = []
        for i in range(n):
            cp = pltpu.make_async_remote_copy(
                src_ref=_half(outs[i], c), dst_ref=_half(outs[i], c), send_sem=send_sems.at[i], recv_sem=recv_sems.at[i],
                device_id=(x, y, 1 - c), device_id_type=MESH)
            cp.start()
            cps.append(cp)
        for i, cp in enumerate(cps):
            cp.wait_send()
            pltpu.make_async_remote_copy(
                src_ref=_half(outs[i], 1 - c), dst_ref=_half(outs[i], 1 - c), send_sem=send_sems.at[i],
                recv_sem=recv_sems.at[i], device_id=(x, y, 1 - c), device_id_type=MESH).wait_recv()

    return pl.pallas_call(
        body, name="sibling_join", in_specs=[HBM] * n, out_specs=[HBM] * n,
        out_shape=[jax.ShapeDtypeStruct(g.shape, g.dtype) for g in grads],
        input_output_aliases={i: i for i in range(n)},
        scratch_shapes=[pltpu.SemaphoreType.DMA((n,)), pltpu.SemaphoreType.DMA((n,))],
        )(*grads)


def _gather8(name, block, reduce):
    m, n = block.shape

    def body(x_ref, out_ref, *scratch):
        if reduce:
            all_ref, send_sems, recv_sems, local_sem = scratch
        else:
            all_ref = out_ref
            send_sems, recv_sems, local_sem = scratch
        x, y, c, chips = _place()
        me, sibling = (x, y, c), (x, y, 1 - c)

        def rows(px, py, pc):
            return all_ref.at[pl.ds((4 * px + 2 * py + pc) * m, m), :]

        def copy(k, blk, to, src=None):
            return pltpu.make_async_remote_copy(
                src_ref=rows(*blk) if src is None else src, dst_ref=rows(*blk),
                send_sem=send_sems.at[k], recv_sem=recv_sems.at[k], device_id=to, device_id_type=MESH)

        mine = pltpu.make_async_copy(x_ref, rows(*me), local_sem)
        mine.start()
        first = [copy(0, me, sibling, src=x_ref)]
        first += [copy(1 + j, me, (chip[0], chip[1], c), src=x_ref) for j, chip in enumerate(chips)]
        for cp in first:
            cp.start()
        passed = [copy(4 + j, (chip[0], chip[1], c), sibling) for j, chip in enumerate(chips)]
        for j, chip in enumerate(chips):
            copy(1 + j, (chip[0], chip[1], c), me).wait_recv()
            passed[j].start()
        copy(0, sibling, me).wait_recv()
        for j, chip in enumerate(chips):
            copy(4 + j, (chip[0], chip[1], 1 - c), me).wait_recv()
        for cp in first + passed:
            cp.wait_send()
        mine.wait()
        if reduce:
            acc = all_ref[pl.ds(0, m), :]
            for d in range(1, 8):
                acc = acc + all_ref[pl.ds(d * m, m), :]
            out_ref[...] = acc

    sems = [pltpu.SemaphoreType.DMA((7,)), pltpu.SemaphoreType.DMA((7,)), pltpu.SemaphoreType.DMA]
    scratch = ([pltpu.VMEM((8 * m, n), F32)] if reduce else []) + sems
    return pl.pallas_call(
        body, name=name,
        out_shape=jax.ShapeDtypeStruct((m, n) if reduce else (8 * m, n), F32),
        in_specs=[pl.BlockSpec(memory_space=pltpu.VMEM)], out_specs=pl.BlockSpec(memory_space=pltpu.VMEM),
        scratch_shapes=scratch)(block)


def _pad_rows(a, rows):
    return jnp.concatenate([a, jnp.zeros((rows - a.shape[0], a.shape[1]), a.dtype)], axis=0)


def kernel(x, p, norm_g, w_in_a, conv_w, conv_b, ln_g, ln_b, w_out_a, kv_norm_g, w_kv, k_norm_g, w_in_b, q_norm_g, w_out_b, ple_norm_g, w_ple_gate, w_ple_proj, loss_target, m_norm_g, m_w_in_a, m_conv_w, m_conv_b, m_ln_g, m_ln_b, m_w_out_a, m_kv_norm_g, m_w_kv, m_k_norm_g, m_w_in_b, m_q_norm_g, m_w_out_b, m_ple_norm_g, m_w_ple_gate, m_w_ple_proj, v_norm_g, v_w_in_a, v_conv_w, v_conv_b, v_ln_g, v_ln_b, v_w_out_a, v_kv_norm_g, v_w_kv, v_k_norm_g, v_w_in_b, v_q_norm_g, v_w_out_b, v_ple_norm_g, v_w_ple_gate, v_w_ple_proj):
    nb, seq, dm = x.shape
    t = nb * seq
    ple = p.shape[-1]
    ccs = conv_w.shape[-1]
    cc = N_CHIPS * ccs
    da = dm
    nheads = da // HEAD_DIM
    assert seq == DILATIONS[-1] * SPAN and da % 128 == 0 and ccs % 128 == 0

    core = lax.axis_index("c").astype(jnp.int32).reshape(1)
    chip = (2 * lax.axis_index("x") + lax.axis_index("y")).astype(jnp.int32)
    chip1 = chip.reshape(1)
    sum_order = jnp.concatenate([(chip1 + k) % N_CHIPS for k in range(N_CHIPS)] + [core])

    x2 = x.reshape(t, dm)
    tgt2 = loss_target.reshape(t, dm)
    p0 = p[0].reshape(t, ple)
    p1 = p[1].reshape(t, ple)

    big = [
        ("w_in_a", w_in_a[0], "col"), ("w_out_a", w_out_a[0], "row"), ("w_kv", w_kv, "col"),
        ("w_in_b", w_in_b[0], "col"), ("w_out_b", w_out_b[0], "row"),
        ("w_ple_gate0", w_ple_gate[0], "row"), ("w_ple_gate1", w_ple_gate[1], "row"),
        ("w_ple_proj0", w_ple_proj[0], "col"), ("w_ple_proj1", w_ple_proj[1], "col"),
    ]
    shard_shape = {nm: w.shape for nm, w, _ in big}
    names = [nm for nm, _, _ in big]
    own = [_cast_bf16("cast_" + nm, w, chip1) for nm, w, _ in big]
    W = {names[0]: _run_plan("gather_w_in_a", _gather_plan(own[:1]))[0]}

    vec_rows = 40
    small = _pad_rows(jnp.concatenate([conv_w[0], conv_b, ln_g, ln_b], axis=0), vec_rows)
    allv = _gather8("gather_conv_vectors", small, reduce=False).reshape(N_CHIPS, 2, vec_rows, ccs)[:, 0]
    allv = allv.transpose(1, 0, 2).reshape(vec_rows, cc)
    cw_full, cb_full, lg_full, lb_full = allv[:HALO], allv[31:32], allv[32:33], allv[33:34]
    cw_full = cw_full * (lax.broadcasted_iota(jnp.int32, (HALO, 1), 0) < CONV_WIDTH).astype(F32)

    tables = _rope_tables(seq)
    gain_q = jnp.tile(q_norm_g[0][:, None, :], (1, nheads, 1)).reshape(1, 3 * da)
    gain_k = jnp.tile(k_norm_g[None, :], (1, nheads))
    g0, g1 = norm_g[0:1], norm_g[1:2]
    pg0, pg1 = ple_norm_g[0:1], ple_norm_g[1:2]
    kvg = kv_norm_g[None, :]

    (u0,) = _rms_fwd("rms_u0", x2, [g0])
    pa, gathered = _mm_nn("mm_in_a", u0, W["w_in_a"], "col", out_dtype=BF16, plan=_gather_plan(own[1:]))
    W.update(zip(names[1:], gathered))
    conv_out, m_a = _mixa_fwd("mixa_fwd", pa, cw_full, cb_full, lg_full, lb_full, seq)
    h0 = _mm_nn("mm_out_a", m_a, W["w_out_a"], "row", resid=x2)
    (r0,) = _rms_fwd("rms_r0", h0, [pg0])
    gpre0 = _mm_nn("mm_gate0", r0, W["w_ple_gate0"], "row", out_dtype=BF16)
    pp0 = _mm_nn("mm_proj0", p0, W["w_ple_proj0"], "col", out_dtype=BF16)
    x1 = _ple_fwd("ple_fwd0", h0, gpre0, pp0)
    kvn, u1 = _rms_fwd("rms_kv_u1", x1, [kvg, g1])
    kv = _mm_nn("mm_kv", kvn, W["w_kv"], "col")
    kn = _hnr_fwd("k_norm_rope", kv, da, gain_k, tables, seq)
    pb = _mm_nn("mm_in_b", u1, W["w_in_b"], "col", out_dtype=BF16)
    qn = _hnr_fwd("q_norm_rope", pb, 3 * da, gain_q, tables, seq)
    o, lse, m_b = _attn_fwd("attn_fwd", qn, kn, kv, pb, seq)
    h1 = _mm_nn("mm_out_b", m_b, W["w_out_b"], "row", resid=x1)
    (r1,) = _rms_fwd("rms_r1", h1, [pg1])
    gpre1 = _mm_nn("mm_gate1", r1, W["w_ple_gate1"], "row", out_dtype=BF16)
    pp1 = _mm_nn("mm_proj1", p1, W["w_ple_proj1"], "col", out_dtype=BF16)
    dy, dgp1, dpp1, sq = _ple_loss("ple_loss", h1, gpre1, pp1, tgt2)
    loss = lax.psum(0.5 * sq[0, 0] / dm, ("x", "y", "c"))

    G = {}
    G["w_ple_gate1"] = _mm_tn("tn_gate1", r1, dgp1, "row", shard_shape["w_ple_gate1"])
    G["w_ple_proj1"] = _mm_tn("tn_proj1", p1, dpp1, "col", shard_shape["w_ple_proj1"])
    dr1 = _mm_nt("nt_gate1", dgp1, W["w_ple_gate1"], "row", out_dtype=BF16)
    dh1, (dpg1,) = _rms_bwd("rms_bwd_r1", h1, dy, [(pg1, dr1)])
    G["w_out_b"] = _mm_tn("tn_out_b", m_b, dh1, "row", shard_shape["w_out_b"])
    dm_b = _mm_nt("nt_out_b", dh1, W["w_out_b"], "row", out_dtype=BF16)
    d_o, dgt, dsum = _gate_bwd("gate_bwd", dm_b, o, pb)
    dq0, dq1, dq2, dk, dv = _attn_bwd("attn_bwd", qn, kn, kv, d_o, lse, dsum, seq)
    dpb, dgq = _q_bwd("q_bwd", pb, gain_q, tables, [dq0, dq1, dq2], dgt, seq)
    dkv, dgk = _k_bwd("k_bwd", kv, gain_k, tables, dk, dv, seq)
    G["w_in_b"] = _mm_tn("tn_in_b", u1, dpb, "col", shard_shape["w_in_b"])
    du1 = _mm_nt("nt_in_b", dpb, W["w_in_b"], "col", out_dtype=BF16)
    G["w_kv"] = _mm_tn("tn_kv", kvn, dkv, "col", shard_shape["w_kv"])
    dkvn = _mm_nt("nt_kv", dkv, W["w_kv"], "col", out_dtype=BF16)
    dx1, (dg1, dkvg), dgp0, dpp0 = _rms_bwd("rms_bwd_x1", x1, dh1, [(g1, du1), (kvg, dkvn)], ple=(gpre0, pp0))
    G["w_ple_gate0"] = _mm_tn("tn_gate0", r0, dgp0, "row", shard_shape["w_ple_gate0"])
    G["w_ple_proj0"] = _mm_tn("tn_proj0", p0, dpp0, "col", shard_shape["w_ple_proj0"])
    dr0 = _mm_nt("nt_gate0", dgp0, W["w_ple_gate0"], "row", out_dtype=BF16)
    dh0, (dpg0,) = _rms_bwd("rms_bwd_r0", h0, dx1, [(pg0, dr0)])
    G["w_out_a"] = _mm_tn("tn_out_a", m_a, dh0, "row", shard_shape["w_out_a"])
    dm_a = _mm_nt("nt_out_a", dh0, W["w_out_a"], "row", out_dtype=BF16)

    def pair_sums(tag, batch):
        recv = _pair_exchange("pair_exchange_" + tag, [G[nm] for nm in batch])
        return [_pair_sum("pair_sum_" + nm, G[nm], rc, core) for nm, rc in zip(batch, recv)]

    late = ["w_kv", "w_in_b", "w_out_b", "w_ple_gate1", "w_ple_proj1"]
    early = ["w_in_a", "w_out_a", "w_ple_gate0", "w_ple_proj0"]
    sums_late = pair_sums("late", late)
    (dc, dz, dlg, dlb, dcb), parts_late = _mixa_bwd1("mixa_bwd1", conv_out, pa, dm_a, lg_full, lb_full,
                                                     plan=_scatter_plan(sums_late))
    dpa, dcw = _mixa_bwd2("mixa_bwd2", dc, pa, dz, cw_full, seq)
    G["w_in_a"] = _mm_tn("tn_in_a", u0, dpa, "col", shard_shape["w_in_a"])
    sums_early = pair_sums("early", early)
    du0, parts_early = _mm_nt("nt_in_a", dpa, W["w_in_a"], "col", out_dtype=BF16, plan=_scatter_plan(sums_early))
    dx, (dg0,) = _rms_bwd("rms_bwd_x", x2, dh0, [(g0, du0)])
    grad_x = dx.reshape(nb, seq, dm)

    sums = dict(zip(late + early, sums_late + sums_early))
    parts = dict(zip(late + early, parts_late + parts_early))
    halves = [_chip_sum("chip_sum_" + nm, sums[nm], parts[nm], sum_order) for nm in names]
    gfull = dict(zip(names, _sibling_join(halves)))

    def as_rows(a):
        return a.reshape(-1, dm)

    small_parts = [as_rows(dcw), as_rows(dcb), as_rows(dlg), as_rows(dlb), dg0, dg1, dkvg, dpg0, dpg1, as_rows(dgk), as_rows(dgq)]
    counts = [a.shape[0] for a in small_parts]
    total = sum(counts)
    packed = _pad_rows(jnp.concatenate(small_parts, axis=0), -(-total // 8) * 8)
    red = _gather8("reduce_small", packed, reduce=True)
    pieces, off = [], 0
    for n_ in counts:
        pieces.append(red[off:off + n_])
        off += n_
    r_dcw, r_dcb, r_dlg, r_dlb, r_g0, r_g1, r_kvg, r_pg0, r_pg1, r_gk, r_gq = pieces
    my_cols = lambda a: lax.dynamic_slice_in_dim(a.reshape(-1, cc), chip * ccs, ccs, axis=1)
    small_grads = {
        "norm_g": jnp.concatenate([r_g0, r_g1], axis=0),
        "conv_w": my_cols(r_dcw)[:CONV_WIDTH],
        "conv_b": my_cols(r_dcb), "ln_g": my_cols(r_dlg), "ln_b": my_cols(r_dlb),
        "kv_norm_g": r_kvg,
        "k_norm_g": r_gk.reshape(nheads, HEAD_DIM).sum(axis=0, keepdims=True),
        "q_norm_g": r_gq.reshape(3, nheads, HEAD_DIM).sum(axis=1),
        "ple_norm_g": jnp.concatenate([r_pg0, r_pg1], axis=0),
    }

    given = dict(norm_g=norm_g, w_in_a=w_in_a, conv_w=conv_w, conv_b=conv_b, ln_g=ln_g, ln_b=ln_b, w_out_a=w_out_a,
                 kv_norm_g=kv_norm_g, w_kv=w_kv, k_norm_g=k_norm_g, w_in_b=w_in_b, q_norm_g=q_norm_g, w_out_b=w_out_b,
                 ple_norm_g=ple_norm_g, w_ple_gate=w_ple_gate, w_ple_proj=w_ple_proj)
    mom1 = dict(norm_g=m_norm_g, w_in_a=m_w_in_a, conv_w=m_conv_w, conv_b=m_conv_b, ln_g=m_ln_g, ln_b=m_ln_b,
                w_out_a=m_w_out_a, kv_norm_g=m_kv_norm_g, w_kv=m_w_kv, k_norm_g=m_k_norm_g, w_in_b=m_w_in_b,
                q_norm_g=m_q_norm_g, w_out_b=m_w_out_b, ple_norm_g=m_ple_norm_g, w_ple_gate=m_w_ple_gate,
                w_ple_proj=m_w_ple_proj)
    mom2 = dict(norm_g=v_norm_g, w_in_a=v_w_in_a, conv_w=v_conv_w, conv_b=v_conv_b, ln_g=v_ln_g, ln_b=v_ln_b,
                w_out_a=v_w_out_a, kv_norm_g=v_kv_norm_g, w_kv=v_w_kv, k_norm_g=v_k_norm_g, w_in_b=v_w_in_b,
                q_norm_g=v_q_norm_g, w_out_b=v_w_out_b, ple_norm_g=v_ple_norm_g, w_ple_gate=v_w_ple_gate,
                w_ple_proj=v_w_ple_proj)
    order = ["norm_g", "w_in_a", "conv_w", "conv_b", "ln_g", "ln_b", "w_out_a", "kv_norm_g", "w_kv", "k_norm_g", "w_in_b",
             "q_norm_g", "w_out_b", "ple_norm_g", "w_ple_gate", "w_ple_proj"]
    grads, deltas, new_m, new_v = {}, {}, {}, {}
    for nm in order:
        shape = given[nm].shape
        if nm in ("w_ple_gate", "w_ple_proj"):
            g2 = jnp.concatenate([gfull[nm + "0"], gfull[nm + "1"]], axis=0)
        elif nm in gfull:
            g2 = gfull[nm]
        else:
            g2 = small_grads[nm]
        two_d = g2.shape
        d2, m2, v2 = _adamw("adamw_" + nm, given[nm].reshape(two_d), g2, mom1[nm].reshape(two_d), mom2[nm].reshape(two_d))
        grads[nm], deltas[nm], new_m[nm], new_v[nm] = (a.reshape(shape) for a in (g2, d2, m2, v2))

    return (loss, grad_x, *[grads[n_] for n_ in order], *[deltas[n_] for n_ in order],
            *[new_m[n_] for n_ in order], *[new_v[n_] for n_ in order])
```

```python
import functools

import jax
import jax.numpy as jnp
from jax import lax
from jax.experimental import pallas as pl
from jax.experimental.pallas import tpu as pltpu

F32 = jnp.float32
BF16 = jnp.bfloat16
MESH = pl.DeviceIdType.MESH

EPS = 1e-6
NEG_INF = -1e30
HEAD_DIM = 64
ROPE_DIM = 16
ROPE_THETA = 500000.0
CONV_WIDTH = 31
SUBLANES = 8
CONV_ROWS = 64
HALO = 32
SPAN = 128
DILATIONS = (1, 4, 16)
ADAM_LR, ADAM_B1, ADAM_B2, ADAM_EPS, ADAM_WD, ADAM_STEP = 0.001, 0.9, 0.999, 1e-08, 0.01, 10
N_CHIPS = 4
VMEM_LIMIT = 56 * 1024 * 1024


def _tile(n, target, mult=128):
    best = None
    t = mult
    while t <= min(n, target):
        if n % t == 0:
            best = t
        t += mult
    return best if best is not None else n


def _params(*sem):
    return pltpu.CompilerParams(dimension_semantics=tuple(sem) if sem else None, vmem_limit_bytes=VMEM_LIMIT)


def _sigmoid(x):
    return 1.0 / (1.0 + jnp.exp(-x))


def _seg_allsum64(x):
    tr, w = x.shape
    cw = 256 if w % 256 == 0 else 128
    n = w // cw
    ri = lax.shift_right_logical(lax.broadcasted_iota(jnp.int32, (cw, cw), 0), 6)
    ci = lax.shift_right_logical(lax.broadcasted_iota(jnp.int32, (cw, cw), 1), 6)
    ones = (ri == ci).astype(BF16)
    hi = x.astype(BF16)
    lo = (x - hi.astype(F32)).astype(BF16)

    def stack(v):
        return jnp.concatenate([v[:, j * cw:(j + 1) * cw] for j in range(n)], axis=0)

    s = (jnp.dot(stack(hi), ones, preferred_element_type=F32)
         + jnp.dot(stack(lo), ones, preferred_element_type=F32))
    return jnp.concatenate([s[j * tr:(j + 1) * tr] for j in range(n)], axis=1)


def _colsum(x):
    return jnp.sum(x, axis=0, keepdims=True)


def _shards_view(w, kind):
    return w if kind == "col" else w.reshape(1, -1, w.shape[2])


def _mm_nn(name, a, w, kind, *, out_dtype=F32, resid=None, plan=None):
    t = a.shape[0]
    w = _shards_view(w, kind)
    ns, k, c = w.shape
    n = ns * c
    tm = _tile(t, 1024, 8)
    tk = _tile(k, 2048)
    tn = _tile(c, 1024)
    nk = k // tk
    per = c // tn

    def body(*refs):
        if resid is None:
            a_ref, w_ref, o_ref = refs[:3]
        else:
            a_ref, w_ref, r_ref, o_ref = refs[:4]
        part = jnp.dot(a_ref[...].astype(BF16), w_ref[...], preferred_element_type=F32)

        def finish(out):
            if resid is not None:
                out = out + r_ref[...]
            o_ref[...] = out.astype(out_dtype)

        if nk == 1:
            finish(part)
            return
        acc = refs[-1]
        kk = pl.program_id(2)

        @pl.when(kk == 0)
        def _():
            acc[...] = part

        @pl.when(kk > 0)
        def _():
            acc[...] += part

        @pl.when(kk == nk - 1)
        def _():
            finish(acc[...])

    in_specs = [pl.BlockSpec((tm, tk), lambda i, j, kk: (i, kk)),
                pl.BlockSpec((None, tk, tn), lambda i, j, kk: (j // per, kk, j % per))]
    args = [a, w]
    if resid is not None:
        in_specs.append(pl.BlockSpec((tm, tn), lambda i, j, kk: (i, j)))
        args.append(resid)
    (out,), carried = _hosted_call(
        body, plan, name=name, grid=(t // tm, n // tn, nk), in_specs=in_specs,
        out_specs=[pl.BlockSpec((tm, tn), lambda i, j, kk: (i, j))],
        out_shape=[jax.ShapeDtypeStruct((t, n), out_dtype)],
        scratch_shapes=[pltpu.VMEM((tm, tn), F32)] if nk > 1 else [],
        args=args, sem=("parallel", "parallel", "arbitrary"))
    return out if plan is None else (out, carried)


def _mm_nt(name, d, w, kind, *, out_dtype=F32, plan=None):
    t = d.shape[0]
    w = _shards_view(w, kind)
    ns, k, c = w.shape
    n = ns * c
    tm = _tile(t, 1024, 8)
    to = _tile(k, 1024)
    tc = _tile(c, 1536)
    nc = n // tc
    per = c // tc

    def body(d_ref, w_ref, o_ref, *scratch):
        part = lax.dot_general(d_ref[...].astype(BF16), w_ref[...], (((1,), (1,)), ((), ())),
                               preferred_element_type=F32)
        if nc == 1:
            o_ref[...] = part.astype(out_dtype)
            return
        acc = scratch[0]
        kk = pl.program_id(2)

        @pl.when(kk == 0)
        def _():
            acc[...] = part

        @pl.when(kk > 0)
        def _():
            acc[...] += part

        @pl.when(kk == nc - 1)
        def _():
            o_ref[...] = acc[...].astype(out_dtype)

    (out,), carried = _hosted_call(
        body, plan, name=name, grid=(t // tm, k // to, nc),
        in_specs=[pl.BlockSpec((tm, tc), lambda i, j, kk: (i, kk)),
                  pl.BlockSpec((None, to, tc), lambda i, j, kk: (kk // per, j, kk % per))],
        out_specs=[pl.BlockSpec((tm, to), lambda i, j, kk: (i, j))],
        out_shape=[jax.ShapeDtypeStruct((t, k), out_dtype)],
        scratch_shapes=[pltpu.VMEM((tm, to), F32)] if nc > 1 else [],
        args=[d, w], sem=("parallel", "parallel", "arbitrary"))
    return out if plan is None else (out, carried)


def _mm_tn(name, a, d, kind, shard_shape):
    t, k = a.shape
    n = d.shape[1]
    ns = N_CHIPS if kind == "col" else 1
    c = n // ns
    tkm = _tile(k, 1024)
    tn = _tile(c, 1536)
    tt = _tile(t, 1024, 8)
    nt = t // tt
    per = c // tn

    def body(a_ref, d_ref, o_ref, acc):
        kk = pl.program_id(2)
        part = lax.dot_general(a_ref[...].astype(BF16), d_ref[...].astype(BF16), (((0,), (0,)), ((), ())),
                               preferred_element_type=F32)

        @pl.when(kk == 0)
        def _():
            acc[...] = part

        @pl.when(kk > 0)
        def _():
            acc[...] += part

        @pl.when(kk == nt - 1)
        def _():
            o_ref[...] = acc[...].astype(BF16)

    out = pl.pallas_call(
        body, name=name, grid=(k // tkm, n // tn, nt),
        in_specs=[pl.BlockSpec((tt, tkm), lambda i, j, kk: (kk, i)),
                  pl.BlockSpec((tt, tn), lambda i, j, kk: (kk, j))],
        out_specs=pl.BlockSpec((None, tkm, tn), lambda i, j, kk: (j // per, i, j % per)),
        out_shape=jax.ShapeDtypeStruct((ns, k, c), BF16),
        scratch_shapes=[pltpu.VMEM((tkm, tn), F32)],
        compiler_params=_params("parallel", "parallel", "arbitrary"))(a, d)
    return out.reshape((N_CHIPS,) + tuple(shard_shape))


def _row_spec(tr, w, col=0):
    return pl.BlockSpec((tr, w), lambda i: (i, col))


def _full_spec(shape):
    return pl.BlockSpec(shape, lambda i: tuple(0 for _ in shape))


def _rms_fwd(name, x, gains):
    t, dm = x.shape
    tr = _tile(t, 256, 8)
    n = len(gains)

    def body(x_ref, *refs):
        xv = x_ref[...]
        xn = xv * lax.rsqrt(jnp.mean(xv * xv, axis=-1, keepdims=True) + EPS)
        for g_ref, o_ref in zip(refs[:n], refs[n:]):
            o_ref[...] = (xn * g_ref[...]).astype(BF16)

    outs = pl.pallas_call(
        body, name=name, grid=(t // tr,),
        in_specs=[_row_spec(tr, dm)] + [_full_spec((1, dm))] * n,
        out_specs=[_row_spec(tr, dm)] * n,
        out_shape=[jax.ShapeDtypeStruct((t, dm), BF16)] * n,
        compiler_params=_params("parallel"))(x, *gains)
    return list(outs)


def _ple_bwd_math(dy, gpre, pp):
    sg = _sigmoid(gpre)
    return (dy * pp * sg * (1.0 - sg)).astype(BF16), (dy * sg).astype(BF16)


def _rms_bwd(name, x, resid, pairs, ple=None):
    t, dm = x.shape
    tr = _tile(t, 256, 8)
    n = len(pairs)
    n_in = 2 * n + (2 if ple is not None else 0)

    def body(x_ref, r_ref, *refs):
        ins, outs = refs[:n_in], refs[n_in:]
        i = pl.program_id(0)
        xv = x_ref[...]
        rs = lax.rsqrt(jnp.mean(xv * xv, axis=-1, keepdims=True) + EPS)
        xn = xv * rs
        total = r_ref[...]
        for kx in range(n):
            g_ref, du_ref = ins[2 * kx], ins[2 * kx + 1]
            dg_ref = outs[1 + kx]
            du = du_ref[...].astype(F32)

            @pl.when(i == 0)
            def _():
                dg_ref[...] = jnp.zeros_like(dg_ref)

            dg_ref[...] += _colsum(du * xn)
            dxh = du * g_ref[...]
            total = total + rs * (dxh - xn * jnp.mean(dxh * xn, axis=-1, keepdims=True))
        outs[0][...] = total
        if ple is not None:
            outs[1 + n][...], outs[2 + n][...] = _ple_bwd_math(total, ins[2 * n][...].astype(F32), ins[2 * n + 1][...].astype(F32))

    in_specs = [_row_spec(tr, dm), _row_spec(tr, dm)]
    args = [x, resid]
    for g, du in pairs:
        in_specs += [_full_spec((1, dm)), _row_spec(tr, dm)]
        args += [g, du]
    out_specs = [_row_spec(tr, dm)] + [_full_spec((1, dm))] * n
    out_shape = [jax.ShapeDtypeStruct((t, dm), F32)] + [jax.ShapeDtypeStruct((1, dm), F32)] * n
    if ple is not None:
        in_specs += [_row_spec(tr, dm)] * 2
        args += list(ple)
        out_specs += [_row_spec(tr, dm)] * 2
        out_shape += [jax.ShapeDtypeStruct((t, dm), BF16)] * 2
    outs = pl.pallas_call(
        body, name=name, grid=(t // tr,), in_specs=in_specs, out_specs=out_specs, out_shape=out_shape,
        compiler_params=_params("arbitrary"))(*args)
    if ple is not None:
        return outs[0], list(outs[1:1 + n]), outs[1 + n], outs[2 + n]
    return outs[0], list(outs[1:])


def _shifted_copies(ext, sh, rows):
    for s in range(1, SUBLANES):
        sh[s - 1] = ext[pl.ds(s, rows), :]


def _window(ext, sh, off, row0, rows, lanes):
    s = off % SUBLANES
    src = ext if s == 0 else sh.at[s - 1]
    return src[pl.ds(off - s + row0, rows), lanes]


def _mixa_fwd(name, pa, cw, cb, lg, lb, seq):
    t, w3 = pa.shape
    cc = w3 // 3
    tr = _tile(seq, 128, HALO)
    per_seq = seq // tr
    hb = tr // HALO
    lead = HALO - (CONV_WIDTH - 1)

    def body(a_ref, b_ref, z_ref, ah_ref, bh_ref, cw_ref, cb_ref, lg_ref, lb_ref, c_ref, m_ref, ext, sh):
        i = pl.program_id(0)
        gh = ah_ref[...].astype(F32) * _sigmoid(bh_ref[...].astype(F32))
        ext[pl.ds(0, HALO), :] = jnp.where((i % per_seq) == 0, 0.0, gh)
        ext[pl.ds(HALO, tr), :] = a_ref[...].astype(F32) * _sigmoid(b_ref[...].astype(F32))
        _shifted_copies(ext, sh, tr + HALO - SUBLANES)
        for lc in range(cc // 128):
            lanes = pl.ds(lc * 128, 128)
            taps = [cw_ref[pl.ds(k, 1), lanes] for k in range(CONV_WIDTH)]
            for row0 in range(0, tr, CONV_ROWS):
                acc = jnp.broadcast_to(cb_ref[:, lanes], (CONV_ROWS, 128))
                for k in range(CONV_WIDTH):
                    acc = acc + _window(ext, sh, lead + k, row0, CONV_ROWS, lanes) * taps[k]
                c_ref[pl.ds(row0, CONV_ROWS), lanes] = acc
        acc = c_ref[...]
        xc = acc - jnp.mean(acc, axis=-1, keepdims=True)
        nrm = xc * lax.rsqrt(jnp.mean(xc * xc, axis=-1, keepdims=True) + EPS)
        l = nrm * lg_ref[...] + lb_ref[...]
        z = z_ref[...].astype(F32)
        m_ref[...] = (l * _sigmoid(l) * z * _sigmoid(z)).astype(BF16)

    halo = lambda col: pl.BlockSpec((HALO, cc), lambda i: (jnp.maximum(i * hb - 1, 0), col))
    return pl.pallas_call(
        body, name=name, grid=(t // tr,),
        in_specs=[_row_spec(tr, cc, 0), _row_spec(tr, cc, 1), _row_spec(tr, cc, 2), halo(0), halo(1),
                  _full_spec((HALO, cc)), _full_spec((1, cc)), _full_spec((1, cc)), _full_spec((1, cc))],
        out_specs=[_row_spec(tr, cc), _row_spec(tr, cc)],
        out_shape=[jax.ShapeDtypeStruct((t, cc), F32), jax.ShapeDtypeStruct((t, cc), BF16)],
        scratch_shapes=[pltpu.VMEM((tr + HALO, cc), F32), pltpu.VMEM((SUBLANES - 1, tr + HALO - SUBLANES, cc), F32)],
        compiler_params=_params("parallel"))(pa, pa, pa, pa, pa, cw, cb, lg, lb)


def _mixa_bwd1(name, c, pa, dm, lg, lb, plan=None):
    t, cc = c.shape
    tr = _tile(t, 128, 8)

    def body(c_ref, z_ref, dm_ref, lg_ref, lb_ref, dc_ref, dz_ref, dlg_ref, dlb_ref, dcb_ref):
        i = pl.program_id(0)
        cv = c_ref[...]
        xc = cv - jnp.mean(cv, axis=-1, keepdims=True)
        rs = lax.rsqrt(jnp.mean(xc * xc, axis=-1, keepdims=True) + EPS)
        nrm = xc * rs
        l = nrm * lg_ref[...] + lb_ref[...]
        z = z_ref[...].astype(F32)
        sl, sz = _sigmoid(l), _sigmoid(z)
        dmv = dm_ref[...].astype(F32)
        ds = dmv * (z * sz)
        dzz = dmv * (l * sl)
        dz_ref[...] = (dzz * (sz * (1.0 + z * (1.0 - sz)))).astype(BF16)
        dl = ds * (sl * (1.0 + l * (1.0 - sl)))
        dn = dl * lg_ref[...]
        dc = rs * (dn - jnp.mean(dn, axis=-1, keepdims=True) - nrm * jnp.mean(dn * nrm, axis=-1, keepdims=True))
        dc_ref[...] = dc

        @pl.when(i == 0)
        def _():
            dlg_ref[...] = jnp.zeros_like(dlg_ref)
            dlb_ref[...] = jnp.zeros_like(dlb_ref)
            dcb_ref[...] = jnp.zeros_like(dcb_ref)

        dlg_ref[...] += _colsum(dl * nrm)
        dlb_ref[...] += _colsum(dl)
        dcb_ref[...] += _colsum(dc)

    vec = jax.ShapeDtypeStruct((1, cc), F32)
    outs, carried = _hosted_call(
        body, plan, name=name, grid=(t // tr,),
        in_specs=[_row_spec(tr, cc), _row_spec(tr, cc, 2), _row_spec(tr, cc), _full_spec((1, cc)), _full_spec((1, cc))],
        out_specs=[_row_spec(tr, cc), _row_spec(tr, cc)] + [_full_spec((1, cc))] * 3,
        out_shape=[jax.ShapeDtypeStruct((t, cc), F32), jax.ShapeDtypeStruct((t, cc), BF16), vec, vec, vec],
        scratch_shapes=[], args=[c, pa, dm, lg, lb], sem=("arbitrary",))
    return outs if plan is None else (outs, carried)


def _mixa_bwd2(name, dc, pa, dz, cw, seq):
    t, cc = dc.shape
    tr = _tile(seq, 128, HALO)
    per_seq = seq // tr
    hb = tr // HALO
    steps = t // tr
    last_halo = t // HALO - 1

    def body(dc_ref, dcn_ref, a_ref, b_ref, dz_ref, cw_ref, dp_ref, dcw_ref, ext, sh, sums):
        i = pl.program_id(0)
        ext[pl.ds(0, tr), :] = dc_ref[...]
        ext[pl.ds(tr, HALO), :] = jnp.where((i % per_seq) == per_seq - 1, 0.0, dcn_ref[...])
        _shifted_copies(ext, sh, tr + HALO - SUBLANES)

        @pl.when(i == 0)
        def _():
            sums[...] = jnp.zeros_like(sums)
            dcw_ref[...] = jnp.zeros_like(dcw_ref)

        av = a_ref[...].astype(F32)
        sb = _sigmoid(b_ref[...].astype(F32))
        glu = av * sb
        dglu = jnp.zeros((tr, cc), F32)
        for k in range(CONV_WIDTH):
            wd = _window(ext, sh, CONV_WIDTH - 1 - k, 0, tr, slice(None))
            dglu = dglu + wd * cw_ref[pl.ds(k, 1), :]
            sums[pl.ds(k * SUBLANES, SUBLANES), :] += (wd * glu).reshape(tr // SUBLANES, SUBLANES, cc).sum(axis=0)
        dp_ref[:, pl.ds(0, cc)] = (dglu * sb).astype(BF16)
        dp_ref[:, pl.ds(cc, cc)] = (dglu * av * sb * (1.0 - sb)).astype(BF16)
        dp_ref[:, pl.ds(2 * cc, cc)] = dz_ref[...]

        @pl.when(i == steps - 1)
        def _():
            for k in range(CONV_WIDTH):
                dcw_ref[pl.ds(k, 1), :] = _colsum(sums[pl.ds(k * SUBLANES, SUBLANES), :])

    nxt = pl.BlockSpec((HALO, cc), lambda i: (jnp.minimum((i + 1) * hb, last_halo), 0))
    return pl.pallas_call(
        body, name=name, grid=(steps,),
        in_specs=[_row_spec(tr, cc), nxt, _row_spec(tr, cc, 0), _row_spec(tr, cc, 1), _row_spec(tr, cc),
                  _full_spec((HALO, cc))],
        out_specs=[_row_spec(tr, 3 * cc), _full_spec((HALO, cc))],
        out_shape=[jax.ShapeDtypeStruct((t, 3 * cc), BF16), jax.ShapeDtypeStruct((HALO, cc), F32)],
        scratch_shapes=[pltpu.VMEM((tr + HALO, cc), F32), pltpu.VMEM((SUBLANES - 1, tr + HALO - SUBLANES, cc), F32),
                        pltpu.VMEM((HALO * SUBLANES, cc), F32)],
        compiler_params=_params("arbitrary"))(dc, dc, pa, pa, dz, cw)


def _ple_fwd(name, h, gpre, pp):
    t, dm = h.shape
    tr = _tile(t, 256, 8)

    def body(h_ref, g_ref, p_ref, o_ref):
        o_ref[...] = h_ref[...] + _sigmoid(g_ref[...].astype(F32)) * p_ref[...].astype(F32)

    return pl.pallas_call(
        body, name=name, grid=(t // tr,), in_specs=[_row_spec(tr, dm)] * 3, out_specs=_row_spec(tr, dm),
        out_shape=jax.ShapeDtypeStruct((t, dm), F32), compiler_params=_params("parallel"))(h, gpre, pp)


def _ple_loss(name, h, gpre, pp, target):
    t, dm = h.shape
    tr = _tile(t, 256, 8)

    def body(h_ref, g_ref, p_ref, t_ref, dy_ref, dg_ref, dp_ref, sq_ref):
        i = pl.program_id(0)
        gpre_v, pp_v = g_ref[...].astype(F32), p_ref[...].astype(F32)
        err = h_ref[...] + _sigmoid(gpre_v) * pp_v - t_ref[...]
        dy = err * (1.0 / dm)
        dy_ref[...] = dy
        dg_ref[...], dp_ref[...] = _ple_bwd_math(dy, gpre_v, pp_v)

        @pl.when(i == 0)
        def _():
            sq_ref[...] = jnp.zeros_like(sq_ref)

        sq_ref[...] += jnp.sum(jnp.sum(err * err, axis=1, keepdims=True), axis=0, keepdims=True)

    return pl.pallas_call(
        body, name=name, grid=(t // tr,), in_specs=[_row_spec(tr, dm)] * 4,
        out_specs=[_row_spec(tr, dm)] * 3 + [_full_spec((1, 1))],
        out_shape=[jax.ShapeDtypeStruct((t, dm), F32)] + [jax.ShapeDtypeStruct((t, dm), BF16)] * 2
        + [jax.ShapeDtypeStruct((1, 1), F32)],
        compiler_params=_params("arbitrary"))(h, gpre, pp, target)


def _rope_tables(seq):
    half = ROPE_DIM // 2
    inv = ROPE_THETA ** (-jnp.arange(half, dtype=F32) * (2.0 / ROPE_DIM))
    ang = jnp.arange(seq).astype(F32)[:, None] * inv[None, :]
    cos, sin = jnp.cos(ang), jnp.sin(ang)
    rest = HEAD_DIM - ROPE_DIM
    one = jnp.ones((seq, rest), F32)
    zero = jnp.zeros((seq, rest), F32)
    zh = jnp.zeros((seq, half), F32)
    tc = jnp.concatenate([cos, cos, one], axis=1)
    ta = jnp.concatenate([-sin, zh, zero], axis=1)
    tb = jnp.concatenate([zh, sin, zero], axis=1)
    return [jnp.tile(tb_, (1, 128 // HEAD_DIM)) for tb_ in (tc, ta, tb)]


def _wide(tab_ref, w):
    return jnp.tile(tab_ref[...], (1, w // 128))


def _hnr_fwd(name, src, width, gain, tables, seq):
    t = src.shape[0]
    tr = _tile(seq, 256, 8)
    per_seq = seq // tr

    def body(x_ref, g_ref, tc_ref, ta_ref, tb_ref, o_ref):
        xv = x_ref[...].astype(F32)
        rs = lax.rsqrt(_seg_allsum64(xv * xv) * (1.0 / HEAD_DIM) + EPS)
        y = xv * rs * g_ref[...]
        o_ref[...] = (y * _wide(tc_ref, width) + pltpu.roll(y, width - ROPE_DIM // 2, 1) * _wide(ta_ref, width)
                      + pltpu.roll(y, ROPE_DIM // 2, 1) * _wide(tb_ref, width))

    tab = pl.BlockSpec((tr, 128), lambda i: (i % per_seq, 0))
    return pl.pallas_call(
        body, name=name, grid=(t // tr,),
        in_specs=[_row_spec(tr, width), _full_spec((1, width)), tab, tab, tab],
        out_specs=_row_spec(tr, width), out_shape=jax.ShapeDtypeStruct((t, width), F32),
        compiler_params=_params("parallel"))(src, gain, *tables)


def _hnr_bwd_math(xv, gain, dout, tc, ta, tb, width):
    dy = dout * tc + pltpu.roll(dout * ta, ROPE_DIM // 2, 1) + pltpu.roll(dout * tb, width - ROPE_DIM // 2, 1)
    rs = lax.rsqrt(_seg_allsum64(xv * xv) * (1.0 / HEAD_DIM) + EPS)
    xn = xv * rs
    dyh = dy * gain
    dx = rs * (dyh - xn * (_seg_allsum64(dyh * xn) * (1.0 / HEAD_DIM)))
    return dx, _colsum(dy * xn)


def _q_bwd(name, p1, gain, tables, dqs, dgt, seq):
    t, w4 = p1.shape
    da = w4 // 4
    width = 3 * da
    tr = _tile(seq, 128, 8)
    per_seq = seq // tr

    def body(x_ref, g_ref, tc_ref, ta_ref, tb_ref, d0_ref, d1_ref, d2_ref, dgt_ref, o_ref, dg_ref):
        i = pl.program_id(0)
        dout = jnp.concatenate([d0_ref[...], d1_ref[...], d2_ref[...]], axis=1)
        dx, dg = _hnr_bwd_math(x_ref[...].astype(F32), g_ref[...], dout, _wide(tc_ref, width), _wide(ta_ref, width),
                               _wide(tb_ref, width), width)

        @pl.when(i == 0)
        def _():
            dg_ref[...] = jnp.zeros_like(dg_ref)

        dg_ref[...] += dg
        o_ref[:, pl.ds(0, width)] = dx.astype(BF16)
        o_ref[:, pl.ds(width, da)] = dgt_ref[...]

    tab = pl.BlockSpec((tr, 128), lambda i: (i % per_seq, 0))
    return pl.pallas_call(
        body, name=name, grid=(t // tr,),
        in_specs=[_row_spec(tr, width), _full_spec((1, width)), tab, tab, tab] + [_row_spec(tr, da)] * 4,
        out_specs=[_row_spec(tr, w4), _full_spec((1, width))],
        out_shape=[jax.ShapeDtypeStruct((t, w4), BF16), jax.ShapeDtypeStruct((1, width), F32)],
        compiler_params=_params("arbitrary"))(p1, gain, *tables, *dqs, dgt)


def _k_bwd(name, kv, gain, tables, dk, dv, seq):
    t, w2 = kv.shape
    da = w2 // 2
    tr = _tile(seq, 256, 8)
    per_seq = seq // tr

    def body(x_ref, g_ref, tc_ref, ta_ref, tb_ref, dk_ref, dv_ref, o_ref, dg_ref):
        i = pl.program_id(0)
        dx, dg = _hnr_bwd_math(x_ref[...], g_ref[...], dk_ref[...], _wide(tc_ref, da), _wide(ta_ref, da),
                               _wide(tb_ref, da), da)

        @pl.when(i == 0)
        def _():
            dg_ref[...] = jnp.zeros_like(dg_ref)

        dg_ref[...] += dg
        o_ref[:, pl.ds(0, da)] = dx.astype(BF16)
        o_ref[:, pl.ds(da, da)] = dv_ref[...].astype(BF16)

    tab = pl.BlockSpec((tr, 128), lambda i: (i % per_seq, 0))
    return pl.pallas_call(
        body, name=name, grid=(t // tr,),
        in_specs=[_row_spec(tr, da), _full_spec((1, da)), tab, tab, tab] + [_row_spec(tr, da)] * 2,
        out_specs=[_row_spec(tr, w2), _full_spec((1, da))],
        out_shape=[jax.ShapeDtypeStruct((t, w2), BF16), jax.ShapeDtypeStruct((1, da), F32)],
        compiler_params=_params("arbitrary"))(kv, gain, *tables, dk, dv)


def _unit_index(dil, r, blk):
    if dil == 1:
        start = blk * SPAN
        return pl.ds(start if isinstance(start, int) else pl.multiple_of(start, SPAN), SPAN)
    return pl.ds(r + dil * SPAN * blk, SPAN, stride=dil)


def _unit_rows(ref, dil, r, blk):
    return ref[_unit_index(dil, r, blk), :]


def _store_rows(ref, dil, r, blk, val):
    ref[_unit_index(dil, r, blk), :] = val


def _over_units(dil, nblk, unit, carry0, after=None):
    if dil == 1:
        carry = unit(0, 0, False, carry0)
        carry = lax.fori_loop(1, nblk, lambda blk, c: unit(0, blk, True, c), carry)
        if after is not None:
            after(0, carry)
        return

    def residue(r, _):
        carry = carry0
        for blk in range(nblk):
            carry = unit(r, blk, blk > 0, carry)
        if after is not None:
            after(r, carry)
        return 0

    lax.fori_loop(0, dil, residue, 0)


QROWS = SPAN // 2


def _half_keys(prev, cur, h, with_prev):
    if with_prev:
        return jnp.concatenate([prev[h * QROWS:], cur[:(h + 1) * QROWS]], axis=0)
    return cur[:(h + 1) * QROWS]


def _half_mask(h, with_prev):
    nk = SPAN + QROWS if with_prev else (h + 1) * QROWS
    qi = lax.broadcasted_iota(jnp.int32, (QROWS, nk), 0)
    kj = lax.broadcasted_iota(jnp.int32, (QROWS, nk), 1)
    if with_prev:
        return (kj >= qi) & (kj <= qi + SPAN)
    return kj <= qi + h * QROWS


_NT = (((1,), (1,)), ((), ()))
_TN = (((0,), (0,)), ((), ()))
_HEAD_LANES = [slice(hh * HEAD_DIM, (hh + 1) * HEAD_DIM) for hh in range(128 // HEAD_DIM)]


def _group_fwd(q_ref, k_ref, v_ref, o_ref, l_ref, dil, seq):
    nblk = seq // (dil * SPAN)
    scale = HEAD_DIM ** -0.5

    def unit(r, blk, with_prev, carry):
        kp, vp = carry if with_prev else (None, None)
        q = _unit_rows(q_ref, dil, r, blk) * scale
        kc = _unit_rows(k_ref, dil, r, blk)
        vc = _unit_rows(v_ref, dil, r, blk)
        o_rows, l_rows = [], []
        for h in range(SPAN // QROWS):
            qh = q[h * QROWS:(h + 1) * QROWS]
            kk = _half_keys(kp, kc, h, with_prev)
            vv = _half_keys(vp, vc, h, with_prev)
            mask = _half_mask(h, with_prev)
            outs, lses = [], []
            for sl in _HEAD_LANES:
                s = lax.dot_general(qh[:, sl].astype(BF16), kk[:, sl].astype(BF16), _NT, preferred_element_type=F32)
                s = jnp.where(mask, s, NEG_INF)
                mx = jnp.max(s, axis=-1, keepdims=True)
                p = jnp.exp(s - mx)
                den = jnp.sum(p, axis=-1, keepdims=True)
                o = jnp.dot(p.astype(BF16), vv[:, sl].astype(BF16), preferred_element_type=F32) / den
                outs.append(o)
                lses.append(jnp.broadcast_to(mx + jnp.log(den), (QROWS, HEAD_DIM)))
            o_rows.append(jnp.concatenate(outs, axis=1))
            l_rows.append(jnp.concatenate(lses, axis=1))
        _store_rows(o_ref, dil, r, blk, jnp.concatenate(o_rows, axis=0))
        _store_rows(l_ref, dil, r, blk, jnp.concatenate(l_rows, axis=0))
        return kc, vc

    _over_units(dil, nblk, unit, None)


def _attn_fwd(name, qn, kn, kv, p1, seq):
    t, da = kn.shape
    hp = da // 128
    ng = len(DILATIONS)

    def body(q0_ref, q1_ref, q2_ref, k_ref, v_ref, g_ref, o_ref, l_ref, m_ref, og, lg):
        for g, q_ref in enumerate((q0_ref, q1_ref, q2_ref)):
            _group_fwd(q_ref, k_ref, v_ref, og.at[g], lg.at[g], DILATIONS[g], seq)
        a0, a1, a2 = lg[0], lg[1], lg[2]
        mx = jnp.maximum(jnp.maximum(a0, a1), a2)
        e0, e1, e2 = jnp.exp(a0 - mx), jnp.exp(a1 - mx), jnp.exp(a2 - mx)
        den = e0 + e1 + e2
        o = (e0 * og[0] + e1 * og[1] + e2 * og[2]) / den
        o_ref[...] = o
        l_ref[...] = mx + jnp.log(den)
        gt = g_ref[...].astype(F32)
        m_ref[...] = (o * gt * _sigmoid(gt)).astype(BF16)

    blk_spec = lambda off: pl.BlockSpec((seq, 128), lambda b, h: (b, off + h))
    return pl.pallas_call(
        body, name=name, grid=(t // seq, hp),
        in_specs=[blk_spec(0), blk_spec(hp), blk_spec(2 * hp), blk_spec(0), blk_spec(hp), blk_spec(3 * hp)],
        out_specs=[blk_spec(0)] * 3,
        out_shape=[jax.ShapeDtypeStruct((t, da), F32)] * 2 + [jax.ShapeDtypeStruct((t, da), BF16)],
        scratch_shapes=[pltpu.VMEM((ng, seq, 128), F32), pltpu.VMEM((ng, seq, 128), F32)],
        compiler_params=_params("parallel", "parallel"))(qn, qn, qn, kn, kv, p1)


def _group_bwd(q_ref, k_ref, v_ref, do_ref, l_ref, d_ref, dq_ref, dk_ref, dv_ref, dil, seq, first):
    nblk = seq // (dil * SPAN)
    scale = HEAD_DIM ** -0.5

    def put(ref, r, blk, val):
        if not first:
            val = val + _unit_rows(ref, dil, r, blk)
        _store_rows(ref, dil, r, blk, val)

    def unit(r, blk, with_prev, carry):
        kp, vp, pend_k, pend_v = carry if with_prev else (None,) * 4
        q = _unit_rows(q_ref, dil, r, blk) * scale
        kc = _unit_rows(k_ref, dil, r, blk)
        vc = _unit_rows(v_ref, dil, r, blk)
        dov = _unit_rows(do_ref, dil, r, blk)
        lrow = _unit_rows(l_ref, dil, r, blk)
        drow = _unit_rows(d_ref, dil, r, blk)
        dq_rows, dk_halves, dv_halves = [], [], []
        for h in range(SPAN // QROWS):
            rows = slice(h * QROWS, (h + 1) * QROWS)
            kk = _half_keys(kp, kc, h, with_prev)
            vv = _half_keys(vp, vc, h, with_prev)
            mask = _half_mask(h, with_prev)
            dqs, dkcs, dvcs = [], [], []
            for hh, sl in enumerate(_HEAD_LANES):
                stat = slice(hh * HEAD_DIM, hh * HEAD_DIM + 1)
                qh = q[rows, sl].astype(BF16)
                doh = dov[rows, sl].astype(BF16)
                kh = kk[:, sl].astype(BF16)
                s = lax.dot_general(qh, kh, _NT, preferred_element_type=F32)
                p = jnp.where(mask, jnp.exp(s - lrow[rows, stat]), 0.0)
                dp = lax.dot_general(doh, vv[:, sl].astype(BF16), _NT, preferred_element_type=F32)
                ds = (p * (dp - drow[rows, stat])).astype(BF16)
                dqs.append(jnp.dot(ds, kh, preferred_element_type=F32) * scale)
                dkcs.append(lax.dot_general(ds, qh, _TN, preferred_element_type=F32))
                dvcs.append(lax.dot_general(p.astype(BF16), doh, _TN, preferred_element_type=F32))
            dq_rows.append(jnp.concatenate(dqs, axis=1))
            dk_halves.append(jnp.concatenate(dkcs, axis=1))
            dv_halves.append(jnp.concatenate(dvcs, axis=1))
        _store_rows(dq_ref, dil, r, blk, jnp.concatenate(dq_rows, axis=0))
        pad = jnp.zeros((QROWS, 128), F32)

        def split(halves):
            lower, upper = halves
            if with_prev:
                prev = lower[:SPAN] + jnp.concatenate([pad, upper[:QROWS]], axis=0)
                cur = jnp.concatenate([lower[SPAN:], pad], axis=0) + upper[QROWS:]
                return prev, cur
            return None, jnp.concatenate([lower, pad], axis=0) + upper

        dk_prev, dk_cur = split(dk_halves)
        dv_prev, dv_cur = split(dv_halves)
        if with_prev:
            put(dk_ref, r, blk - 1, pend_k + dk_prev)
            put(dv_ref, r, blk - 1, pend_v + dv_prev)
        return kc, vc, dk_cur, dv_cur

    def after(r, carry):
        put(dk_ref, r, nblk - 1, carry[2])
        put(dv_ref, r, nblk - 1, carry[3])

    _over_units(dil, nblk, unit, None, after)


def _attn_bwd(name, qn, kn, kv, do, lse, dsum, seq):
    t, da = kn.shape
    hp = da // 128

    def body(q0_ref, q1_ref, q2_ref, k_ref, v_ref, do_ref, l_ref, d_ref, dq0_ref, dq1_ref, dq2_ref, dk_ref, dv_ref):
        groups = ((q0_ref, dq0_ref), (q1_ref, dq1_ref), (q2_ref, dq2_ref))
        for g, (q_ref, dq_ref) in enumerate(groups):
            _group_bwd(q_ref, k_ref, v_ref, do_ref, l_ref, d_ref, dq_ref, dk_ref, dv_ref, DILATIONS[g], seq, g == 0)

    blk_spec = lambda off: pl.BlockSpec((seq, 128), lambda b, h: (b, off + h))
    return pl.pallas_call(
        body, name=name, grid=(t // seq, hp),
        in_specs=[blk_spec(0), blk_spec(hp), blk_spec(2 * hp), blk_spec(0), blk_spec(hp), blk_spec(0), blk_spec(0), blk_spec(0)],
        out_specs=[blk_spec(0)] * 5,
        out_shape=[jax.ShapeDtypeStruct((t, da), F32)] * 5,
        compiler_params=_params("parallel", "parallel"))(qn, qn, qn, kn, kv, do, lse, dsum)


def _gate_bwd(name, dm, o, p1):
    t, da = o.shape
    tr = _tile(t, 256, 8)

    def body(dm_ref, o_ref, g_ref, do_ref, dg_ref, ds_ref):
        g = g_ref[...].astype(F32)
        sg = _sigmoid(g)
        dmv, ov = dm_ref[...].astype(F32), o_ref[...]
        do = dmv * (g * sg)
        do_ref[...] = do
        dg_ref[...] = (dmv * ov * (sg * (1.0 + g * (1.0 - sg)))).astype(BF16)
        ds_ref[...] = _seg_allsum64(do * ov)

    return pl.pallas_call(
        body, name=name, grid=(t // tr,),
        in_specs=[_row_spec(tr, da), _row_spec(tr, da), _row_spec(tr, da, 3)],
        out_specs=[_row_spec(tr, da)] * 3,
        out_shape=[jax.ShapeDtypeStruct((t, da), F32), jax.ShapeDtypeStruct((t, da), BF16), jax.ShapeDtypeStruct((t, da), F32)],
        compiler_params=_params("parallel"))(dm, o, p1)


def _cast_bf16(name, w2d, chip):
    r, c = w2d.shape
    tr = _tile(r, 256, 16)

    def body(chip_ref, x_ref, o_ref):
        o_ref[...] = x_ref[...].astype(BF16)

    grid_spec = pltpu.PrefetchScalarGridSpec(
        num_scalar_prefetch=1, grid=(r // tr,),
        in_specs=[pl.BlockSpec((tr, c), lambda i, m: (i, 0))],
        out_specs=pl.BlockSpec((None, tr, c), lambda i, m: (m[0], i, 0)))
    return pl.pallas_call(
        body, name=name, grid_spec=grid_spec, out_shape=jax.ShapeDtypeStruct((N_CHIPS, r, c), BF16),
        compiler_params=_params("parallel"))(chip, w2d)


def _adamw(name, w, g, m, v):
    r, c = w.shape
    tr = _tile(r, 256, 8)
    c1 = 1.0 - ADAM_B1 ** ADAM_STEP
    c2 = 1.0 - ADAM_B2 ** ADAM_STEP

    def body(w_ref, g_ref, m_ref, v_ref, d_ref, nm_ref, nv_ref):
        gv = g_ref[...]
        nm = ADAM_B1 * m_ref[...] + (1.0 - ADAM_B1) * gv
        nv = ADAM_B2 * v_ref[...] + (1.0 - ADAM_B2) * (gv * gv)
        nm_ref[...] = nm
        nv_ref[...] = nv
        d_ref[...] = -ADAM_LR * ((nm / c1) / (jnp.sqrt(nv / c2) + ADAM_EPS) + ADAM_WD * w_ref[...])

    sds = jax.ShapeDtypeStruct((r, c), F32)
    return pl.pallas_call(
        body, name=name, grid=(r // tr,), in_specs=[_row_spec(tr, c)] * 4, out_specs=[_row_spec(tr, c)] * 3,
        out_shape=[sds] * 3, compiler_params=_params("parallel"))(w, g, m, v)


def _pair_sum(name, gd, recv, core):
    _, r, c = gd.shape
    rh = r // 2
    tr = _tile(rh, 256, 8)
    nrt = rh // tr

    def body(c_ref, a_ref, b_ref, o_ref):
        o_ref[...] = (a_ref[...].astype(F32) + b_ref[...].astype(F32)).astype(BF16)

    grid_spec = pltpu.PrefetchScalarGridSpec(
        num_scalar_prefetch=1, grid=(N_CHIPS, nrt),
        in_specs=[pl.BlockSpec((None, tr, c), lambda j, i, cr: (j, cr[0] * nrt + i, 0)),
                  pl.BlockSpec((None, tr, c), lambda j, i, cr: (j, i, 0))],
        out_specs=pl.BlockSpec((None, tr, c), lambda j, i, cr: (j, i, 0)))
    return pl.pallas_call(
        body, name=name, grid_spec=grid_spec, out_shape=jax.ShapeDtypeStruct((N_CHIPS, rh, c), BF16),
        compiler_params=_params("parallel", "parallel"))(core, gd, recv)


def _chip_sum(name, sums, parts, order):
    _, rh, c = parts.shape
    tr = _tile(rh, 256, 16)
    nrt = rh // tr

    def body(o_ref_, s_ref, p1_ref, p2_ref, p3_ref, o_ref):
        acc = s_ref[...].astype(F32)
        for p_ref in (p1_ref, p2_ref, p3_ref):
            acc = acc + p_ref[...].astype(F32)
        o_ref[...] = acc

    slot = lambda k: pl.BlockSpec((None, tr, c), lambda i, o: (o[k], i, 0))
    grid_spec = pltpu.PrefetchScalarGridSpec(
        num_scalar_prefetch=1, grid=(nrt,),
        in_specs=[slot(0), slot(1), slot(2), slot(3)],
        out_specs=pl.BlockSpec((tr, c), lambda i, o: (o[N_CHIPS] * nrt + i, 0)))
    return pl.pallas_call(
        body, name=name, grid_spec=grid_spec, out_shape=jax.ShapeDtypeStruct((2 * rh, c), F32),
        compiler_params=_params("parallel"))(order, sums, parts, parts, parts)


HBM = pl.BlockSpec(memory_space=pl.ANY)


def _place():
    x, y, c = lax.axis_index("x"), lax.axis_index("y"), lax.axis_index("c")
    chips = [(1 - x, y), (x, 1 - y), (1 - x, 1 - y)]
    return x, y, c, chips


def _half(ref, hc):
    rows = ref.shape[0] // 2
    return ref.at[pl.ds(hc * rows, rows)]


class _Plan:
    def __init__(self, ins, out_shapes, aliases, n_sems, start, finish):
        self.ins, self.out_shapes, self.aliases, self.n_sems = list(ins), list(out_shapes), dict(aliases), n_sems
        self.start, self.finish = start, finish


def _gather_plan(shards):
    n = len(shards)

    def ici(place, outs, send_sems, recv_sems, i, k, slot):
        x, y, c, chips = place
        half = _half(outs[i].at[slot], c)
        return pltpu.make_async_remote_copy(
            src_ref=half, dst_ref=half, send_sem=send_sems.at[6 * i + k], recv_sem=recv_sems.at[6 * i + k],
            device_id=(chips[k][0], chips[k][1], c), device_id_type=MESH)

    def d2d(place, outs, send_sems, recv_sems, i, k, slot, hc):
        x, y, c, chips = place
        half = _half(outs[i].at[slot], hc)
        return pltpu.make_async_remote_copy(
            src_ref=half, dst_ref=half, send_sem=send_sems.at[6 * i + 3 + k], recv_sem=recv_sems.at[6 * i + 3 + k],
            device_id=(x, y, 1 - c), device_id_type=MESH)

    def start(place, ins, outs, send_sems, recv_sems):
        x, y, c, chips = place
        for i in range(n):
            for k in range(3):
                ici(place, outs, send_sems, recv_sems, i, k, 2 * x + y).start()

    def finish(place, ins, outs, send_sems, recv_sems):
        x, y, c, chips = place
        theirs = [2 * chip[0] + chip[1] for chip in chips]
        for i in range(n):
            for k in range(3):
                ici(place, outs, send_sems, recv_sems, i, k, theirs[k]).wait_recv()
                d2d(place, outs, send_sems, recv_sems, i, k, theirs[k], c).start()
        for i in range(n):
            for k in range(3):
                d2d(place, outs, send_sems, recv_sems, i, k, theirs[k], 1 - c).wait_recv()
        for i in range(n):
            for k in range(3):
                ici(place, outs, send_sems, recv_sems, i, k, 2 * x + y).wait_send()
                d2d(place, outs, send_sems, recv_sems, i, k, theirs[k], c).wait_send()

    return _Plan(shards, [jax.ShapeDtypeStruct(s.shape, s.dtype) for s in shards], {i: i for i in range(n)}, 6 * n,
                 start, finish)


def _scatter_plan(sums):
    n = len(sums)

    def copy(place, ins, outs, send_sems, recv_sems, i, k, src_slot, dst_slot):
        x, y, c, chips = place
        return pltpu.make_async_remote_copy(
            src_ref=ins[i].at[src_slot], dst_ref=outs[i].at[dst_slot],
            send_sem=send_sems.at[3 * i + k], recv_sem=recv_sems.at[3 * i + k],
            device_id=(chips[k][0], chips[k][1], c), device_id_type=MESH)

    def start(place, ins, outs, send_sems, recv_sems):
        x, y, c, chips = place
        for i in range(n):
            for k, chip in enumerate(chips):
                copy(place, ins, outs, send_sems, recv_sems, i, k, 2 * chip[0] + chip[1], 2 * x + y).start()

    def finish(place, ins, outs, send_sems, recv_sems):
        x, y, c, chips = place
        for i in range(n):
            for k, chip in enumerate(chips):
                theirs = 2 * chip[0] + chip[1]
                copy(place, ins, outs, send_sems, recv_sems, i, k, theirs, 2 * x + y).wait_send()
                copy(place, ins, outs, send_sems, recv_sems, i, k, theirs, theirs).wait_recv()

    return _Plan(sums, [jax.ShapeDtypeStruct(s.shape, s.dtype) for s in sums], {}, 3 * n, start, finish)


def _hosted_call(body, plan, *, name, grid, in_specs, out_specs, out_shape, scratch_shapes, args, sem):
    in_specs, out_specs, out_shape, scratch_shapes = list(in_specs), list(out_specs), list(out_shape), list(scratch_shapes)
    if plan is None:
        res = pl.pallas_call(body, name=name, grid=grid, in_specs=in_specs, out_specs=out_specs, out_shape=out_shape,
                             scratch_shapes=scratch_shapes, compiler_params=_params(*sem))(*args)
        return list(res), []
    n_in, n_out, n_scr = len(in_specs), len(out_specs), len(scratch_shapes)
    p_in, p_out = len(plan.ins), len(plan.out_shapes)

    def hosted(*refs):
        refs = list(refs)
        ins, pins = refs[:n_in], refs[n_in:n_in + p_in]
        outs = refs[n_in + p_in:n_in + p_in + n_out]
        pouts = refs[n_in + p_in + n_out:n_in + p_in + n_out + p_out]
        scr = refs[n_in + p_in + n_out + p_out:n_in + p_in + n_out + p_out + n_scr]
        send_sems, recv_sems = refs[-2:]
        place = _place()
        ids = [pl.program_id(d) for d in range(len(grid))]
        first = functools.reduce(jnp.logical_and, [i == 0 for i in ids])
        last = functools.reduce(jnp.logical_and, [i == g - 1 for i, g in zip(ids, grid)])

        @pl.when(first)
        def _():
            plan.start(place, pins, pouts, send_sems, recv_sems)

        body(*ins, *outs, *scr)

        @pl.when(last)
        def _():
            plan.finish(place, pins, pouts, send_sems, recv_sems)

    res = pl.pallas_call(
        hosted, name=name, grid=grid, in_specs=in_specs + [HBM] * p_in, out_specs=out_specs + [HBM] * p_out,
        out_shape=out_shape + plan.out_shapes,
        input_output_aliases={n_in + i: n_out + o for i, o in plan.aliases.items()},
        scratch_shapes=scratch_shapes + [pltpu.SemaphoreType.DMA((plan.n_sems,)), pltpu.SemaphoreType.DMA((plan.n_sems,))],
        compiler_params=_params(*(("arbitrary",) * len(grid))))(*args, *plan.ins)
    return list(res[:n_out]), list(res[n_out:])


def _run_plan(name, plan):
    p_in = len(plan.ins)

    def body(*refs):
        ins, outs = refs[:p_in], refs[p_in:p_in + len(plan.out_shapes)]
        send_sems, recv_sems = refs[-2:]
        place = _place()
        plan.start(place, ins, outs, send_sems, recv_sems)
        plan.finish(place, ins, outs, send_sems, recv_sems)

    return pl.pallas_call(
        body, name=name, in_specs=[HBM] * p_in, out_specs=[HBM] * len(plan.out_shapes), out_shape=plan.out_shapes,
        input_output_aliases=plan.aliases,
        scratch_shapes=[pltpu.SemaphoreType.DMA((plan.n_sems,)), pltpu.SemaphoreType.DMA((plan.n_sems,))],
        )(*plan.ins)


def _pair_exchange(name, grads):
    n = len(grads)

    def body(*refs):
        ins, outs = refs[:n], refs[n:2 * n]
        send_sems, recv_sems = refs[2 * n:]
        x, y, c, _ = _place()
        cps = []
        for i in range(n):
            rows = ins[i].shape[1] // 2
            cp = pltpu.make_async_remote_copy(
                src_ref=ins[i].at[:, pl.ds((1 - c) * rows, rows), :], dst_ref=outs[i],
                send_sem=send_sems.at[i], recv_sem=recv_sems.at[i], device_id=(x, y, 1 - c), device_id_type=MESH)
            cp.start()
            cps.append(cp)
        for cp in cps:
            cp.wait()

    return pl.pallas_call(
        body, name=name, in_specs=[HBM] * n, out_specs=[HBM] * n,
        out_shape=[jax.ShapeDtypeStruct((N_CHIPS, g.shape[1] // 2, g.shape[2]), g.dtype) for g in grads],
        scratch_shapes=[pltpu.SemaphoreType.DMA((n,)), pltpu.SemaphoreType.DMA((n,))],
        )(*grads)


def _sibling_join(grads):
    n = len(grads)

    def body(*refs):
        outs = refs[n:2 * n]
        send_sems, recv_sems = refs[2 * n:]
        x, y, c, _ = _place()
        cps = []
        for i in range(n):
            cp = pltpu.make_async_remote_copy(
                src_ref=_half(outs[i], c), dst_ref=_half(outs[i], c), send_sem=send_sems.at[i], recv_sem=recv_sems.at[i],
                device_id=(x, y, 1 - c), device_id_type=MESH)
            cp.start()
            cps.append(cp)
        for i, cp in enumerate(cps):
            cp.wait_send()
            pltpu.make_async_remote_copy(
                src_ref=_half(outs[i], 1 - c), dst_ref=_half(outs[i], 1 - c), send_sem=send_sems.at[i],
                recv_sem=recv_sems.at[i], device_id=(x, y, 1 - c), device_id_type=MESH).wait_recv()

    return pl.pallas_call(
        body, name="sibling_join", in_specs=[HBM] * n, out_specs=[HBM] * n,
        out_shape=[jax.ShapeDtypeStruct(g.shape, g.dtype) for g in grads],
        input_output_aliases={i: i for i in range(n)},
        scratch_shapes=[pltpu.SemaphoreType.DMA((n,)), pltpu.SemaphoreType.DMA((n,))],
        )(*grads)


def _gather8(name, block, reduce):
    m, n = block.shape

    def body(x_ref, out_ref, *scratch):
        if reduce:
            all_ref, send_sems, recv_sems, local_sem = scratch
        else:
            all_ref = out_ref
            send_sems, recv_sems, local_sem = scratch
        x, y, c, chips = _place()
        me, sibling = (x, y, c), (x, y, 1 - c)

        def rows(px, py, pc):
            return all_ref.at[pl.ds((4 * px + 2 * py + pc) * m, m), :]

        def copy(k, blk, to, src=None):
            return pltpu.make_async_remote_copy(
                src_ref=rows(*blk) if src is None else src, dst_ref=rows(*blk),
                send_sem=send_sems.at[k], recv_sem=recv_sems.at[k], device_id=to, device_id_type=MESH)

        mine = pltpu.make_async_copy(x_ref, rows(*me), local_sem)
        mine.start()
        first = [copy(0, me, sibling, src=x_ref)]
        first += [copy(1 + j, me, (chip[0], chip[1], c), src=x_ref) for j, chip in enumerate(chips)]
        for cp in first:
            cp.start()
        passed = [copy(4 + j, (chip[0], chip[1], c), sibling) for j, chip in enumerate(chips)]
        for j, chip in enumerate(chips):
            copy(1 + j, (chip[0], chip[1], c), me).wait_recv()
            passed[j].start()
        copy(0, sibling, me).wait_recv()
        for j, chip in enumerate(chips):
            copy(4 + j, (chip[0], chip[1], 1 - c), me).wait_recv()
        for cp in first + passed:
            cp.wait_send()
        mine.wait()
        if reduce:
            acc = all_ref[pl.ds(0, m), :]
            for d in range(1, 8):
                acc = acc + all_ref[pl.ds(d * m, m), :]
            out_ref[...] = acc

    sems = [pltpu.SemaphoreType.DMA((7,)), pltpu.SemaphoreType.DMA((7,)), pltpu.SemaphoreType.DMA]
    scratch = ([pltpu.VMEM((8 * m, n), F32)] if reduce else []) + sems
    return pl.pallas_call(
        body, name=name,
        out_shape=jax.ShapeDtypeStruct((m, n) if reduce else (8 * m, n), F32),
        in_specs=[pl.BlockSpec(memory_space=pltpu.VMEM)], out_specs=pl.BlockSpec(memory_space=pltpu.VMEM),
        scratch_shapes=scratch)(block)


def _pad_rows(a, rows):
    return jnp.concatenate([a, jnp.zeros((rows - a.shape[0], a.shape[1]), a.dtype)], axis=0)


def kernel(x, p, norm_g, w_in_a, conv_w, conv_b, ln_g, ln_b, w_out_a, kv_norm_g, w_kv, k_norm_g, w_in_b, q_norm_g, w_out_b, ple_norm_g, w_ple_gate, w_ple_proj, loss_target, m_norm_g, m_w_in_a, m_conv_w, m_conv_b, m_ln_g, m_ln_b, m_w_out_a, m_kv_norm_g, m_w_kv, m_k_norm_g, m_w_in_b, m_q_norm_g, m_w_out_b, m_ple_norm_g, m_w_ple_gate, m_w_ple_proj, v_norm_g, v_w_in_a, v_conv_w, v_conv_b, v_ln_g, v_ln_b, v_w_out_a, v_kv_norm_g, v_w_kv, v_k_norm_g, v_w_in_b, v_q_norm_g, v_w_out_b, v_ple_norm_g, v_w_ple_gate, v_w_ple_proj):
    nb, seq, dm = x.shape
    t = nb * seq
    ple = p.shape[-1]
    ccs = conv_w.shape[-1]
    cc = N_CHIPS * ccs
    da = dm
    nheads = da // HEAD_DIM
    assert seq == DILATIONS[-1] * SPAN and da % 128 == 0 and ccs % 128 == 0

    core = lax.axis_index("c").astype(jnp.int32).reshape(1)
    chip = (2 * lax.axis_index("x") + lax.axis_index("y")).astype(jnp.int32)
    chip1 = chip.reshape(1)
    sum_order = jnp.concatenate([(chip1 + k) % N_CHIPS for k in range(N_CHIPS)] + [core])

    x2 = x.reshape(t, dm)
    tgt2 = loss_target.reshape(t, dm)
    p0 = p[0].reshape(t, ple)
    p1 = p[1].reshape(t, ple)

    big = [
        ("w_in_a", w_in_a[0], "col"), ("w_out_a", w_out_a[0], "row"), ("w_kv", w_kv, "col"),
        ("w_in_b", w_in_b[0], "col"), ("w_out_b", w_out_b[0], "row"),
        ("w_ple_gate0", w_ple_gate[0], "row"), ("w_ple_gate1", w_ple_gate[1], "row"),
        ("w_ple_proj0", w_ple_proj[0], "col"), ("w_ple_proj1", w_ple_proj[1], "col"),
    ]
    shard_shape = {nm: w.shape for nm, w, _ in big}
    names = [nm for nm, _, _ in big]
    own = [_cast_bf16("cast_" + nm, w, chip1) for nm, w, _ in big]
    W = {names[0]: _run_plan("gather_w_in_a", _gather_plan(own[:1]))[0]}

    vec_rows = 40
    small = _pad_rows(jnp.concatenate([conv_w[0], conv_b, ln_g, ln_b], axis=0), vec_rows)
    allv = _gather8("gather_conv_vectors", small, reduce=False).reshape(N_CHIPS, 2, vec_rows, ccs)[:, 0]
    allv = allv.transpose(1, 0, 2).reshape(vec_rows, cc)
    cw_full, cb_full, lg_full, lb_full = allv[:HALO], allv[31:32], allv[32:33], allv[33:34]
    cw_full = cw_full * (lax.broadcasted_iota(jnp.int32, (HALO, 1), 0) < CONV_WIDTH).astype(F32)

    tables = _rope_tables(seq)
    gain_q = jnp.tile(q_norm_g[0][:, None, :], (1, nheads, 1)).reshape(1, 3 * da)
    gain_k = jnp.tile(k_norm_g[None, :], (1, nheads))
    g0, g1 = norm_g[0:1], norm_g[1:2]
    pg0, pg1 = ple_norm_g[0:1], ple_norm_g[1:2]
    kvg = kv_norm_g[None, :]

    (u0,) = _rms_fwd("rms_u0", x2, [g0])
    pa, gathered = _mm_nn("mm_in_a", u0, W["w_in_a"], "col", out_dtype=BF16, plan=_gather_plan(own[1:]))
    W.update(zip(names[1:], gathered))
    conv_out, m_a = _mixa_fwd("mixa_fwd", pa, cw_full, cb_full, lg_full, lb_full, seq)
    h0 = _mm_nn("mm_out_a", m_a, W["w_out_a"], "row", resid=x2)
    (r0,) = _rms_fwd("rms_r0", h0, [pg0])
    gpre0 = _mm_nn("mm_gate0", r0, W["w_ple_gate0"], "row", out_dtype=BF16)
    pp0 = _mm_nn("mm_proj0", p0, W["w_ple_proj0"], "col", out_dtype=BF16)
    x1 = _ple_fwd("ple_fwd0", h0, gpre0, pp0)
    kvn, u1 = _rms_fwd("rms_kv_u1", x1, [kvg, g1])
    kv = _mm_nn("mm_kv", kvn, W["w_kv"], "col")
    kn = _hnr_fwd("k_norm_rope", kv, da, gain_k, tables, seq)
    pb = _mm_nn("mm_in_b", u1, W["w_in_b"], "col", out_dtype=BF16)
    qn = _hnr_fwd("q_norm_rope", pb, 3 * da, gain_q, tables, seq)
    o, lse, m_b = _attn_fwd("attn_fwd", qn, kn, kv, pb, seq)
    h1 = _mm_nn("mm_out_b", m_b, W["w_out_b"], "row", resid=x1)
    (r1,) = _rms_fwd("rms_r1", h1, [pg1])
    gpre1 = _mm_nn("mm_gate1", r1, W["w_ple_gate1"], "row", out_dtype=BF16)
    pp1 = _mm_nn("mm_proj1", p1, W["w_ple_proj1"], "col", out_dtype=BF16)
    dy, dgp1, dpp1, sq = _ple_loss("ple_loss", h1, gpre1, pp1, tgt2)
    loss = lax.psum(0.5 * sq[0, 0] / dm, ("x", "y", "c"))

    G = {}
    G["w_ple_gate1"] = _mm_tn("tn_gate1", r1, dgp1, "row", shard_shape["w_ple_gate1"])
    G["w_ple_proj1"] = _mm_tn("tn_proj1", p1, dpp1, "col", shard_shape["w_ple_proj1"])
    dr1 = _mm_nt("nt_gate1", dgp1, W["w_ple_gate1"], "row", out_dtype=BF16)
    dh1, (dpg1,) = _rms_bwd("rms_bwd_r1", h1, dy, [(pg1, dr1)])
    G["w_out_b"] = _mm_tn("tn_out_b", m_b, dh1, "row", shard_shape["w_out_b"])
    dm_b = _mm_nt("nt_out_b", dh1, W["w_out_b"], "row", out_dtype=BF16)
    d_o, dgt, dsum = _gate_bwd("gate_bwd", dm_b, o, pb)
    dq0, dq1, dq2, dk, dv = _attn_bwd("attn_bwd", qn, kn, kv, d_o, lse, dsum, seq)
    dpb, dgq = _q_bwd("q_bwd", pb, gain_q, tables, [dq0, dq1, dq2], dgt, seq)
    dkv, dgk = _k_bwd("k_bwd", kv, gain_k, tables, dk, dv, seq)
    G["w_in_b"] = _mm_tn("tn_in_b", u1, dpb, "col", shard_shape["w_in_b"])
    du1 = _mm_nt("nt_in_b", dpb, W["w_in_b"], "col", out_dtype=BF16)
    G["w_kv"] = _mm_tn("tn_kv", kvn, dkv, "col", shard_shape["w_kv"])
    dkvn = _mm_nt("nt_kv", dkv, W["w_kv"], "col", out_dtype=BF16)
    dx1, (dg1, dkvg), dgp0, dpp0 = _rms_bwd("rms_bwd_x1", x1, dh1, [(g1, du1), (kvg, dkvn)], ple=(gpre0, pp0))
    G["w_ple_gate0"] = _mm_tn("tn_gate0", r0, dgp0, "row", shard_shape["w_ple_gate0"])
    G["w_ple_proj0"] = _mm_tn("tn_proj0", p0, dpp0, "col", shard_shape["w_ple_proj0"])
    dr0 = _mm_nt("nt_gate0", dgp0, W["w_ple_gate0"], "row", out_dtype=BF16)
    dh0, (dpg0,) = _rms_bwd("rms_bwd_r0", h0, dx1, [(pg0, dr0)])
    G["w_out_a"] = _mm_tn("tn_out_a", m_a, dh0, "row", shard_shape["w_out_a"])
    dm_a = _mm_nt("nt_out_a", dh0, W["w_out_a"], "row", out_dtype=BF16)

    def pair_sums(tag, batch):
        recv = _pair_exchange("pair_exchange_" + tag, [G[nm] for nm in batch])
        return [_pair_sum("pair_sum_" + nm, G[nm], rc, core) for nm, rc in zip(batch, recv)]

    late = ["w_kv", "w_in_b", "w_out_b", "w_ple_gate1", "w_ple_proj1"]
    early = ["w_in_a", "w_out_a", "w_ple_gate0", "w_ple_proj0"]
    sums_late = pair_sums("late", late)
    (dc, dz, dlg, dlb, dcb), parts_late = _mixa_bwd1("mixa_bwd1", conv_out, pa, dm_a, lg_full, lb_full,
                                                     plan=_scatter_plan(sums_late))
    dpa, dcw = _mixa_bwd2("mixa_bwd2", dc, pa, dz, cw_full, seq)
    G["w_in_a"] = _mm_tn("tn_in_a", u0, dpa, "col", shard_shape["w_in_a"])
    sums_early = pair_sums("early", early)
    du0, parts_early = _mm_nt("nt_in_a", dpa, W["w_in_a"], "col", out_dtype=BF16, plan=_scatter_plan(sums_early))
    dx, (dg0,) = _rms_bwd("rms_bwd_x", x2, dh0, [(g0, du0)])
    grad_x = dx.reshape(nb, seq, dm)

    sums = dict(zip(late + early, sums_late + sums_early))
    parts = dict(zip(late + early, parts_late + parts_early))
    halves = [_chip_sum("chip_sum_" + nm, sums[nm], parts[nm], sum_order) for nm in names]
    gfull = dict(zip(names, _sibling_join(halves)))

    def as_rows(a):
        return a.reshape(-1, dm)

    small_parts = [as_rows(dcw), as_rows(dcb), as_rows(dlg), as_rows(dlb), dg0, dg1, dkvg, dpg0, dpg1, as_rows(dgk), as_rows(dgq)]
    counts = [a.shape[0] for a in small_parts]
    total = sum(counts)
    packed = _pad_rows(jnp.concatenate(small_parts, axis=0), -(-total // 8) * 8)
    red = _gather8("reduce_small", packed, reduce=True)
    pieces, off = [], 0
    for n_ in counts:
        pieces.append(red[off:off + n_])
        off += n_
    r_dcw, r_dcb, r_dlg, r_dlb, r_g0, r_g1, r_kvg, r_pg0, r_pg1, r_gk, r_gq = pieces
    my_cols = lambda a: lax.dynamic_slice_in_dim(a.reshape(-1, cc), chip * ccs, ccs, axis=1)
    small_grads = {
        "norm_g": jnp.concatenate([r_g0, r_g1], axis=0),
        "conv_w": my_cols(r_dcw)[:CONV_WIDTH],
        "conv_b": my_cols(r_dcb), "ln_g": my_cols(r_dlg), "ln_b": my_cols(r_dlb),
        "kv_norm_g": r_kvg,
        "k_norm_g": r_gk.reshape(nheads, HEAD_DIM).sum(axis=0, keepdims=True),
        "q_norm_g": r_gq.reshape(3, nheads, HEAD_DIM).sum(axis=1),
        "ple_norm_g": jnp.concatenate([r_pg0, r_pg1], axis=0),
    }

    given = dict(norm_g=norm_g, w_in_a=w_in_a, conv_w=conv_w, conv_b=conv_b, ln_g=ln_g, ln_b=ln_b, w_out_a=w_out_a,
                 kv_norm_g=kv_norm_g, w_kv=w_kv, k_norm_g=k_norm_g, w_in_b=w_in_b, q_norm_g=q_norm_g, w_out_b=w_out_b,
                 ple_norm_g=ple_norm_g, w_ple_gate=w_ple_gate, w_ple_proj=w_ple_proj)
    mom1 = dict(norm_g=m_norm_g, w_in_a=m_w_in_a, conv_w=m_conv_w, conv_b=m_conv_b, ln_g=m_ln_g, ln_b=m_ln_b,
                w_out_a=m_w_out_a, kv_norm_g=m_kv_norm_g, w_kv=m_w_kv, k_norm_g=m_k_norm_g, w_in_b=m_w_in_b,
                q_norm_g=m_q_norm_g, w_out_b=m_w_out_b, ple_norm_g=m_ple_norm_g, w_ple_gate=m_w_ple_gate,
                w_ple_proj=m_w_ple_proj)
    mom2 = dict(norm_g=v_norm_g, w_in_a=v_w_in_a, conv_w=v_conv_w, conv_b=v_conv_b, ln_g=v_ln_g, ln_b=v_ln_b,
                w_out_a=v_w_out_a, kv_norm_g=v_kv_norm_g, w_kv=v_w_kv, k_norm_g=v_k_norm_g, w_in_b=v_w_in_b,
                q_norm_g=v_q_norm_g, w_out_b=v_w_out_b, ple_norm_g=v_ple_norm_g, w_ple_gate=v_w_ple_gate,
                w_ple_proj=v_w_ple_proj)
    order = ["norm_g", "w_in_a", "conv_w", "conv_b", "ln_g", "ln_b", "w_out_a", "kv_norm_g", "w_kv", "k_norm_g", "w_in_b",
             "q_norm_g", "w_out_b", "ple_norm_g", "w_ple_gate", "w_ple_proj"]
    grads, deltas, new_m, new_v = {}, {}, {}, {}
    for nm in order:
        shape = given[nm].shape
        if nm in ("w_ple_gate", "w_ple_proj"):
            g2 = jnp.concatenate([gfull[nm + "0"], gfull[nm + "1"]], axis=0)
        elif nm in gfull:
            g2 = gfull[nm]
        else:
            g2 = small_grads[nm]
        two_d = g2.shape
        d2, m2, v2 = _adamw("adamw_" + nm, given[nm].reshape(two_d), g2, mom1[nm].reshape(two_d), mom2[nm].reshape(two_d))
        grads[nm], deltas[nm], new_m[nm], new_v[nm] = (a.reshape(shape) for a in (g2, d2, m2, v2))

    return (loss, grad_x, *[grads[n_] for n_ in order], *[deltas[n_] for n_ in order],
            *[new_m[n_] for n_ in order], *[new_v[n_] for n_ in order])
```

```python
import functools

import jax
import jax.numpy as jnp
from jax import lax
from jax.experimental import pallas as pl
from jax.experimental.pallas import tpu as pltpu

F32 = jnp.float32
BF16 = jnp.bfloat16
MESH = pl.DeviceIdType.MESH

EPS = 1e-6
NEG_INF = -1e30
HEAD_DIM = 64
ROPE_DIM = 16
ROPE_THETA = 500000.0
CONV_WIDTH = 31
SUBLANES = 8
CONV_ROWS = 64
HALO = 32
SPAN = 128
DILATIONS = (1, 4, 16)
ADAM_LR, ADAM_B1, ADAM_B2, ADAM_EPS, ADAM_WD, ADAM_STEP = 0.001, 0.9, 0.999, 1e-08, 0.01, 10
N_CHIPS = 4
VMEM_LIMIT = 56 * 1024 * 1024


def _tile(n, target, mult=128):
    best = None
    t = mult
    while t <= min(n, target):
        if n % t == 0:
            best = t
        t += mult
    return best if best is not None else n


def _params(*sem):
    return pltpu.CompilerParams(dimension_semantics=tuple(sem) if sem else None, vmem_limit_bytes=VMEM_LIMIT)


def _sigmoid(x):
    return 1.0 / (1.0 + jnp.exp(-x))


def _seg_allsum64(x):
    tr, w = x.shape
    cw = 256 if w % 256 == 0 else 128
    n = w // cw
    ri = lax.shift_right_logical(lax.broadcasted_iota(jnp.int32, (cw, cw), 0), 6)
    ci = lax.shift_right_logical(lax.broadcasted_iota(jnp.int32, (cw, cw), 1), 6)
    ones = (ri == ci).astype(BF16)
    hi = x.astype(BF16)
    lo = (x - hi.astype(F32)).astype(BF16)

    def stack(v):
        return jnp.concatenate([v[:, j * cw:(j + 1) * cw] for j in range(n)], axis=0)

    s = (jnp.dot(stack(hi), ones, preferred_element_type=F32)
         + jnp.dot(stack(lo), ones, preferred_element_type=F32))
    return jnp.concatenate([s[j * tr:(j + 1) * tr] for j in range(n)], axis=1)


def _colsum(x):
    return jnp.sum(x, axis=0, keepdims=True)


def _shards_view(w, kind):
    return w if kind == "col" else w.reshape(1, -1, w.shape[2])


def _mm_nn(name, a, w, kind, *, out_dtype=F32, resid=None, norm_gain=None, plan=None):
    t = a.shape[0]
    w = _shards_view(w, kind)
    ns, k, c = w.shape
    n = ns * c
    tm = _tile(t, 1024 if norm_gain is None else 512, 8)
    tk = _tile(k, 2048)
    tn = _tile(c, 1024)
    nk = k // tk
    per = c // tn
    assert norm_gain is None or tn == n, "the fused RMSNorm needs whole rows in one tile"
    n_in = 2 + (resid is not None) + (norm_gain is not None)

    def body(*refs):
        a_ref, w_ref = refs[:2]
        r_ref = refs[2] if resid is not None else None
        g_ref = refs[n_in - 1] if norm_gain is not None else None
        o_ref = refs[n_in]
        part = jnp.dot(a_ref[...].astype(BF16), w_ref[...], preferred_element_type=F32)

        def finish(out):
            if resid is not None:
                out = out + r_ref[...]
            o_ref[...] = out.astype(out_dtype)
            if norm_gain is not None:
                normed = out * lax.rsqrt(jnp.mean(out * out, axis=-1, keepdims=True) + EPS) * g_ref[...]
                refs[n_in + 1][...] = normed.astype(BF16)

        if nk == 1:
            finish(part)
            return
        acc = refs[-1]
        kk = pl.program_id(2)

        @pl.when(kk == 0)
        def _():
            acc[...] = part

        @pl.when(kk > 0)
        def _():
            acc[...] += part

        @pl.when(kk == nk - 1)
        def _():
            finish(acc[...])

    in_specs = [pl.BlockSpec((tm, tk), lambda i, j, kk: (i, kk)),
                pl.BlockSpec((None, tk, tn), lambda i, j, kk: (j // per, kk, j % per))]
    args = [a, w]
    if resid is not None:
        in_specs.append(pl.BlockSpec((tm, tn), lambda i, j, kk: (i, j)))
        args.append(resid)
    out_specs = [pl.BlockSpec((tm, tn), lambda i, j, kk: (i, j))]
    out_shape = [jax.ShapeDtypeStruct((t, n), out_dtype)]
    if norm_gain is not None:
        in_specs.append(pl.BlockSpec((1, n), lambda i, j, kk: (0, 0)))
        args.append(norm_gain)
        out_specs.append(pl.BlockSpec((tm, tn), lambda i, j, kk: (i, j)))
        out_shape.append(jax.ShapeDtypeStruct((t, n), BF16))
    outs, carried = _hosted_call(
        body, plan, name=name, grid=(t // tm, n // tn, nk), in_specs=in_specs, out_specs=out_specs, out_shape=out_shape,
        scratch_shapes=[pltpu.VMEM((tm, tn), F32)] if nk > 1 else [],
        args=args, sem=("parallel", "parallel", "arbitrary"))
    out = outs[0] if norm_gain is None else tuple(outs)
    return out if plan is None else (out, carried)


def _mm_nt(name, d, w, kind, *, out_dtype=F32, plan=None):
    t = d.shape[0]
    w = _shards_view(w, kind)
    ns, k, c = w.shape
    n = ns * c
    tm = _tile(t, 1024, 8)
    to = _tile(k, 1024)
    tc = _tile(c, 1536)
    nc = n // tc
    per = c // tc

    def body(d_ref, w_ref, o_ref, *scratch):
        part = lax.dot_general(d_ref[...].astype(BF16), w_ref[...], (((1,), (1,)), ((), ())),
                               preferred_element_type=F32)
        if nc == 1:
            o_ref[...] = part.astype(out_dtype)
            return
        acc = scratch[0]
        kk = pl.program_id(2)

        @pl.when(kk == 0)
        def _():
            acc[...] = part

        @pl.when(kk > 0)
        def _():
            acc[...] += part

        @pl.when(kk == nc - 1)
        def _():
            o_ref[...] = acc[...].astype(out_dtype)

    (out,), carried = _hosted_call(
        body, plan, name=name, grid=(t // tm, k // to, nc),
        in_specs=[pl.BlockSpec((tm, tc), lambda i, j, kk: (i, kk)),
                  pl.BlockSpec((None, to, tc), lambda i, j, kk: (kk // per, j, kk % per))],
        out_specs=[pl.BlockSpec((tm, to), lambda i, j, kk: (i, j))],
        out_shape=[jax.ShapeDtypeStruct((t, k), out_dtype)],
        scratch_shapes=[pltpu.VMEM((tm, to), F32)] if nc > 1 else [],
        args=[d, w], sem=("parallel", "parallel", "arbitrary"))
    return out if plan is None else (out, carried)


def _mm_tn(name, a, d, kind, shard_shape):
    t, k = a.shape
    n = d.shape[1]
    ns = N_CHIPS if kind == "col" else 1
    c = n // ns
    tkm = _tile(k, 1024)
    tn = _tile(c, 1536)
    tt = _tile(t, 1024, 8)
    nt = t // tt
    per = c // tn

    def body(a_ref, d_ref, o_ref, acc):
        kk = pl.program_id(2)
        part = lax.dot_general(a_ref[...].astype(BF16), d_ref[...].astype(BF16), (((0,), (0,)), ((), ())),
                               preferred_element_type=F32)

        @pl.when(kk == 0)
        def _():
            acc[...] = part

        @pl.when(kk > 0)
        def _():
            acc[...] += part

        @pl.when(kk == nt - 1)
        def _():
            o_ref[...] = acc[...].astype(BF16)

    out = pl.pallas_call(
        body, name=name, grid=(k // tkm, n // tn, nt),
        in_specs=[pl.BlockSpec((tt, tkm), lambda i, j, kk: (kk, i)),
                  pl.BlockSpec((tt, tn), lambda i, j, kk: (kk, j))],
        out_specs=pl.BlockSpec((None, tkm, tn), lambda i, j, kk: (j // per, i, j % per)),
        out_shape=jax.ShapeDtypeStruct((ns, k, c), BF16),
        scratch_shapes=[pltpu.VMEM((tkm, tn), F32)],
        compiler_params=_params("parallel", "parallel", "arbitrary"))(a, d)
    return out.reshape((N_CHIPS,) + tuple(shard_shape))


def _row_spec(tr, w, col=0):
    return pl.BlockSpec((tr, w), lambda i: (i, col))


def _full_spec(shape):
    return pl.BlockSpec(shape, lambda i: tuple(0 for _ in shape))


def _rms_fwd(name, x, gains):
    t, dm = x.shape
    tr = _tile(t, 256, 8)
    n = len(gains)

    def body(x_ref, *refs):
        xv = x_ref[...]
        xn = xv * lax.rsqrt(jnp.mean(xv * xv, axis=-1, keepdims=True) + EPS)
        for g_ref, o_ref in zip(refs[:n], refs[n:]):
            o_ref[...] = (xn * g_ref[...]).astype(BF16)

    outs = pl.pallas_call(
        body, name=name, grid=(t // tr,),
        in_specs=[_row_spec(tr, dm)] + [_full_spec((1, dm))] * n,
        out_specs=[_row_spec(tr, dm)] * n,
        out_shape=[jax.ShapeDtypeStruct((t, dm), BF16)] * n,
        compiler_params=_params("parallel"))(x, *gains)
    return list(outs)


def _ple_bwd_math(dy, gpre, pp):
    sg = _sigmoid(gpre)
    return (dy * pp * sg * (1.0 - sg)).astype(BF16), (dy * sg).astype(BF16)


def _rms_bwd(name, x, resid, pairs, ple=None):
    t, dm = x.shape
    tr = _tile(t, 256, 8)
    n = len(pairs)
    n_in = 2 * n + (2 if ple is not None else 0)

    def body(x_ref, r_ref, *refs):
        ins, outs = refs[:n_in], refs[n_in:]
        i = pl.program_id(0)
        xv = x_ref[...]
        rs = lax.rsqrt(jnp.mean(xv * xv, axis=-1, keepdims=True) + EPS)
        xn = xv * rs
        total = r_ref[...]
        for kx in range(n):
            g_ref, du_ref = ins[2 * kx], ins[2 * kx + 1]
            dg_ref = outs[1 + kx]
            du = du_ref[...].astype(F32)

            @pl.when(i == 0)
            def _():
                dg_ref[...] = jnp.zeros_like(dg_ref)

            dg_ref[...] += _colsum(du * xn)
            dxh = du * g_ref[...]
            total = total + rs * (dxh - xn * jnp.mean(dxh * xn, axis=-1, keepdims=True))
        outs[0][...] = total
        if ple is not None:
            outs[1 + n][...], outs[2 + n][...] = _ple_bwd_math(total, ins[2 * n][...].astype(F32), ins[2 * n + 1][...].astype(F32))

    in_specs = [_row_spec(tr, dm), _row_spec(tr, dm)]
    args = [x, resid]
    for g, du in pairs:
        in_specs += [_full_spec((1, dm)), _row_spec(tr, dm)]
        args += [g, du]
    out_specs = [_row_spec(tr, dm)] + [_full_spec((1, dm))] * n
    out_shape = [jax.ShapeDtypeStruct((t, dm), F32)] + [jax.ShapeDtypeStruct((1, dm), F32)] * n
    if ple is not None:
        in_specs += [_row_spec(tr, dm)] * 2
        args += list(ple)
        out_specs += [_row_spec(tr, dm)] * 2
        out_shape += [jax.ShapeDtypeStruct((t, dm), BF16)] * 2
    outs = pl.pallas_call(
        body, name=name, grid=(t // tr,), in_specs=in_specs, out_specs=out_specs, out_shape=out_shape,
        compiler_params=_params("arbitrary"))(*args)
    if ple is not None:
        return outs[0], list(outs[1:1 + n]), outs[1 + n], outs[2 + n]
    return outs[0], list(outs[1:])


def _shifted_copies(ext, sh, rows):
    for s in range(1, SUBLANES):
        sh[s - 1] = ext[pl.ds(s, rows), :]


def _window(ext, sh, off, row0, rows, lanes):
    s = off % SUBLANES
    src = ext if s == 0 else sh.at[s - 1]
    return src[pl.ds(off - s + row0, rows), lanes]


def _mixa_fwd(name, pa, cw, cb, lg, lb, seq):
    t, w3 = pa.shape
    cc = w3 // 3
    tr = _tile(seq, 128, HALO)
    per_seq = seq // tr
    hb = tr // HALO
    lead = HALO - (CONV_WIDTH - 1)

    def body(a_ref, b_ref, z_ref, ah_ref, bh_ref, cw_ref, cb_ref, lg_ref, lb_ref, c_ref, m_ref, ext, sh, conv):
        i = pl.program_id(0)
        gh = ah_ref[...].astype(F32) * _sigmoid(bh_ref[...].astype(F32))
        ext[pl.ds(0, HALO), :] = jnp.where((i % per_seq) == 0, 0.0, gh)
        ext[pl.ds(HALO, tr), :] = a_ref[...].astype(F32) * _sigmoid(b_ref[...].astype(F32))
        _shifted_copies(ext, sh, tr + HALO - SUBLANES)
        for lc in range(cc // 128):
            lanes = pl.ds(lc * 128, 128)
            taps = [cw_ref[pl.ds(k, 1), lanes] for k in range(CONV_WIDTH)]
            for row0 in range(0, tr, CONV_ROWS):
                acc = jnp.broadcast_to(cb_ref[:, lanes], (CONV_ROWS, 128))
                for k in range(CONV_WIDTH):
                    acc = acc + _window(ext, sh, lead + k, row0, CONV_ROWS, lanes) * taps[k]
                conv[pl.ds(row0, CONV_ROWS), lanes] = acc
        acc = conv[...]
        c_ref[...] = acc.astype(BF16)
        xc = acc - jnp.mean(acc, axis=-1, keepdims=True)
        nrm = xc * lax.rsqrt(jnp.mean(xc * xc, axis=-1, keepdims=True) + EPS)
        l = nrm * lg_ref[...] + lb_ref[...]
        z = z_ref[...].astype(F32)
        m_ref[...] = (l * _sigmoid(l) * z * _sigmoid(z)).astype(BF16)

    halo = lambda col: pl.BlockSpec((HALO, cc), lambda i: (jnp.maximum(i * hb - 1, 0), col))
    return pl.pallas_call(
        body, name=name, grid=(t // tr,),
        in_specs=[_row_spec(tr, cc, 0), _row_spec(tr, cc, 1), _row_spec(tr, cc, 2), halo(0), halo(1),
                  _full_spec((HALO, cc)), _full_spec((1, cc)), _full_spec((1, cc)), _full_spec((1, cc))],
        out_specs=[_row_spec(tr, cc), _row_spec(tr, cc)],
        out_shape=[jax.ShapeDtypeStruct((t, cc), BF16), jax.ShapeDtypeStruct((t, cc), BF16)],
        scratch_shapes=[pltpu.VMEM((tr + HALO, cc), F32), pltpu.VMEM((SUBLANES - 1, tr + HALO - SUBLANES, cc), F32),
                        pltpu.VMEM((tr, cc), F32)],
        compiler_params=_params("parallel"))(pa, pa, pa, pa, pa, cw, cb, lg, lb)


def _mixa_bwd1(name, c, pa, dm, lg, lb, plan=None):
    t, cc = c.shape
    tr = _tile(t, 128, 8)

    def body(c_ref, z_ref, dm_ref, lg_ref, lb_ref, dc_ref, dz_ref, dlg_ref, dlb_ref, dcb_ref):
        i = pl.program_id(0)
        cv = c_ref[...].astype(F32)
        xc = cv - jnp.mean(cv, axis=-1, keepdims=True)
        rs = lax.rsqrt(jnp.mean(xc * xc, axis=-1, keepdims=True) + EPS)
        nrm = xc * rs
        l = nrm * lg_ref[...] + lb_ref[...]
        z = z_ref[...].astype(F32)
        sl, sz = _sigmoid(l), _sigmoid(z)
        dmv = dm_ref[...].astype(F32)
        ds = dmv * (z * sz)
        dzz = dmv * (l * sl)
        dz_ref[...] = (dzz * (sz * (1.0 + z * (1.0 - sz)))).astype(BF16)
        dl = ds * (sl * (1.0 + l * (1.0 - sl)))
        dn = dl * lg_ref[...]
        dc = rs * (dn - jnp.mean(dn, axis=-1, keepdims=True) - nrm * jnp.mean(dn * nrm, axis=-1, keepdims=True))
        dc_ref[...] = dc.astype(BF16)

        @pl.when(i == 0)
        def _():
            dlg_ref[...] = jnp.zeros_like(dlg_ref)
            dlb_ref[...] = jnp.zeros_like(dlb_ref)
            dcb_ref[...] = jnp.zeros_like(dcb_ref)

        dlg_ref[...] += _colsum(dl * nrm)
        dlb_ref[...] += _colsum(dl)
        dcb_ref[...] += _colsum(dc)

    vec = jax.ShapeDtypeStruct((1, cc), F32)
    outs, carried = _hosted_call(
        body, plan, name=name, grid=(t // tr,),
        in_specs=[_row_spec(tr, cc), _row_spec(tr, cc, 2), _row_spec(tr, cc), _full_spec((1, cc)), _full_spec((1, cc))],
        out_specs=[_row_spec(tr, cc), _row_spec(tr, cc)] + [_full_spec((1, cc))] * 3,
        out_shape=[jax.ShapeDtypeStruct((t, cc), BF16), jax.ShapeDtypeStruct((t, cc), BF16), vec, vec, vec],
        scratch_shapes=[], args=[c, pa, dm, lg, lb], sem=("arbitrary",))
    return outs if plan is None else (outs, carried)


def _mixa_bwd2(name, dc, pa, dz, cw, seq):
    t, cc = dc.shape
    tr = _tile(seq, 128, HALO)
    per_seq = seq // tr
    hb = tr // HALO
    steps = t // tr
    last_halo = t // HALO - 1

    def body(dc_ref, dcn_ref, a_ref, b_ref, dz_ref, cw_ref, dp_ref, dcw_ref, ext, sh, sums):
        i = pl.program_id(0)
        ext[pl.ds(0, tr), :] = dc_ref[...].astype(F32)
        ext[pl.ds(tr, HALO), :] = jnp.where((i % per_seq) == per_seq - 1, 0.0, dcn_ref[...].astype(F32))
        _shifted_copies(ext, sh, tr + HALO - SUBLANES)

        @pl.when(i == 0)
        def _():
            sums[...] = jnp.zeros_like(sums)
            dcw_ref[...] = jnp.zeros_like(dcw_ref)

        av = a_ref[...].astype(F32)
        sb = _sigmoid(b_ref[...].astype(F32))
        glu = av * sb
        dglu = jnp.zeros((tr, cc), F32)
        for k in range(CONV_WIDTH):
            wd = _window(ext, sh, CONV_WIDTH - 1 - k, 0, tr, slice(None))
            dglu = dglu + wd * cw_ref[pl.ds(k, 1), :]
            sums[pl.ds(k * SUBLANES, SUBLANES), :] += (wd * glu).reshape(tr // SUBLANES, SUBLANES, cc).sum(axis=0)
        dp_ref[:, pl.ds(0, cc)] = (dglu * sb).astype(BF16)
        dp_ref[:, pl.ds(cc, cc)] = (dglu * av * sb * (1.0 - sb)).astype(BF16)
        dp_ref[:, pl.ds(2 * cc, cc)] = dz_ref[...]

        @pl.when(i == steps - 1)
        def _():
            for k in range(CONV_WIDTH):
                dcw_ref[pl.ds(k, 1), :] = _colsum(sums[pl.ds(k * SUBLANES, SUBLANES), :])

    nxt = pl.BlockSpec((HALO, cc), lambda i: (jnp.minimum((i + 1) * hb, last_halo), 0))
    return pl.pallas_call(
        body, name=name, grid=(steps,),
        in_specs=[_row_spec(tr, cc), nxt, _row_spec(tr, cc, 0), _row_spec(tr, cc, 1), _row_spec(tr, cc),
                  _full_spec((HALO, cc))],
        out_specs=[_row_spec(tr, 3 * cc), _full_spec((HALO, cc))],
        out_shape=[jax.ShapeDtypeStruct((t, 3 * cc), BF16), jax.ShapeDtypeStruct((HALO, cc), F32)],
        scratch_shapes=[pltpu.VMEM((tr + HALO, cc), F32), pltpu.VMEM((SUBLANES - 1, tr + HALO - SUBLANES, cc), F32),
                        pltpu.VMEM((HALO * SUBLANES, cc), F32)],
        compiler_params=_params("arbitrary"))(dc, dc, pa, pa, dz, cw)


def _ple_fwd(name, h, gpre, pp, gains):
    t, dm = h.shape
    tr = _tile(t, 256, 8)

    n = len(gains)

    def body(h_ref, g_ref, p_ref, *refs):
        o_ref = refs[n]
        xv = h_ref[...] + _sigmoid(g_ref[...].astype(F32)) * p_ref[...].astype(F32)
        o_ref[...] = xv
        xn = xv * lax.rsqrt(jnp.mean(xv * xv, axis=-1, keepdims=True) + EPS)
        for gain_ref, n_ref in zip(refs[:n], refs[n + 1:]):
            n_ref[...] = (xn * gain_ref[...]).astype(BF16)

    return pl.pallas_call(
        body, name=name, grid=(t // tr,), in_specs=[_row_spec(tr, dm)] * 3 + [_full_spec((1, dm))] * n,
        out_specs=[_row_spec(tr, dm)] * (1 + n),
        out_shape=[jax.ShapeDtypeStruct((t, dm), F32)] + [jax.ShapeDtypeStruct((t, dm), BF16)] * n,
        compiler_params=_params("parallel"))(h, gpre, pp, *gains)


def _ple_loss(name, h, gpre, pp, target):
    t, dm = h.shape
    tr = _tile(t, 256, 8)

    def body(h_ref, g_ref, p_ref, t_ref, dy_ref, dg_ref, dp_ref, sq_ref):
        i = pl.program_id(0)
        gpre_v, pp_v = g_ref[...].astype(F32), p_ref[...].astype(F32)
        err = h_ref[...] + _sigmoid(gpre_v) * pp_v - t_ref[...]
        dy = err * (1.0 / dm)
        dy_ref[...] = dy
        dg_ref[...], dp_ref[...] = _ple_bwd_math(dy, gpre_v, pp_v)

        @pl.when(i == 0)
        def _():
            sq_ref[...] = jnp.zeros_like(sq_ref)

        sq_ref[...] += jnp.sum(jnp.sum(err * err, axis=1, keepdims=True), axis=0, keepdims=True)

    return pl.pallas_call(
        body, name=name, grid=(t // tr,), in_specs=[_row_spec(tr, dm)] * 4,
        out_specs=[_row_spec(tr, dm)] * 3 + [_full_spec((1, 1))],
        out_shape=[jax.ShapeDtypeStruct((t, dm), F32)] + [jax.ShapeDtypeStruct((t, dm), BF16)] * 2
        + [jax.ShapeDtypeStruct((1, 1), F32)],
        compiler_params=_params("arbitrary"))(h, gpre, pp, target)


def _rope_tables(seq):
    half = ROPE_DIM // 2
    inv = ROPE_THETA ** (-jnp.arange(half, dtype=F32) * (2.0 / ROPE_DIM))
    ang = jnp.arange(seq).astype(F32)[:, None] * inv[None, :]
    cos, sin = jnp.cos(ang), jnp.sin(ang)
    rest = HEAD_DIM - ROPE_DIM
    one = jnp.ones((seq, rest), F32)
    zero = jnp.zeros((seq, rest), F32)
    zh = jnp.zeros((seq, half), F32)
    tc = jnp.concatenate([cos, cos, one], axis=1)
    ta = jnp.concatenate([-sin, zh, zero], axis=1)
    tb = jnp.concatenate([zh, sin, zero], axis=1)
    return [jnp.tile(tb_, (1, 128 // HEAD_DIM)) for tb_ in (tc, ta, tb)]


def _wide(tab_ref, w):
    return jnp.tile(tab_ref[...], (1, w // 128))


def _hnr_fwd(name, src, width, gain, tables, seq):
    t = src.shape[0]
    tr = _tile(seq, 256, 8)
    per_seq = seq // tr

    def body(x_ref, g_ref, tc_ref, ta_ref, tb_ref, o_ref):
        xv = x_ref[...].astype(F32)
        rs = lax.rsqrt(_seg_allsum64(xv * xv) * (1.0 / HEAD_DIM) + EPS)
        y = xv * rs * g_ref[...]
        o_ref[...] = (y * _wide(tc_ref, width) + pltpu.roll(y, width - ROPE_DIM // 2, 1) * _wide(ta_ref, width)
                      + pltpu.roll(y, ROPE_DIM // 2, 1) * _wide(tb_ref, width))

    tab = pl.BlockSpec((tr, 128), lambda i: (i % per_seq, 0))
    return pl.pallas_call(
        body, name=name, grid=(t // tr,),
        in_specs=[_row_spec(tr, width), _full_spec((1, width)), tab, tab, tab],
        out_specs=_row_spec(tr, width), out_shape=jax.ShapeDtypeStruct((t, width), F32),
        compiler_params=_params("parallel"))(src, gain, *tables)


def _hnr_bwd_math(xv, gain, dout, tc, ta, tb, width):
    dy = dout * tc + pltpu.roll(dout * ta, ROPE_DIM // 2, 1) + pltpu.roll(dout * tb, width - ROPE_DIM // 2, 1)
    rs = lax.rsqrt(_seg_allsum64(xv * xv) * (1.0 / HEAD_DIM) + EPS)
    xn = xv * rs
    dyh = dy * gain
    dx = rs * (dyh - xn * (_seg_allsum64(dyh * xn) * (1.0 / HEAD_DIM)))
    return dx, _colsum(dy * xn)


def _q_bwd(name, p1, gain, tables, dqs, dgt, seq):
    t, w4 = p1.shape
    da = w4 // 4
    width = 3 * da
    tr = _tile(seq, 128, 8)
    per_seq = seq // tr

    def body(x_ref, g_ref, tc_ref, ta_ref, tb_ref, d0_ref, d1_ref, d2_ref, dgt_ref, o_ref, dg_ref):
        i = pl.program_id(0)
        dout = jnp.concatenate([d0_ref[...], d1_ref[...], d2_ref[...]], axis=1)
        dx, dg = _hnr_bwd_math(x_ref[...].astype(F32), g_ref[...], dout, _wide(tc_ref, width), _wide(ta_ref, width),
                               _wide(tb_ref, width), width)

        @pl.when(i == 0)
        def _():
            dg_ref[...] = jnp.zeros_like(dg_ref)

        dg_ref[...] += dg
        o_ref[:, pl.ds(0, width)] = dx.astype(BF16)
        o_ref[:, pl.ds(width, da)] = dgt_ref[...]

    tab = pl.BlockSpec((tr, 128), lambda i: (i % per_seq, 0))
    return pl.pallas_call(
        body, name=name, grid=(t // tr,),
        in_specs=[_row_spec(tr, width), _full_spec((1, width)), tab, tab, tab] + [_row_spec(tr, da)] * 4,
        out_specs=[_row_spec(tr, w4), _full_spec((1, width))],
        out_shape=[jax.ShapeDtypeStruct((t, w4), BF16), jax.ShapeDtypeStruct((1, width), F32)],
        compiler_params=_params("arbitrary"))(p1, gain, *tables, *dqs, dgt)


def _k_bwd(name, kv, gain, tables, dk, dv, seq):
    t, w2 = kv.shape
    da = w2 // 2
    tr = _tile(seq, 256, 8)
    per_seq = seq // tr

    def body(x_ref, g_ref, tc_ref, ta_ref, tb_ref, dk_ref, dv_ref, o_ref, dg_ref):
        i = pl.program_id(0)
        dx, dg = _hnr_bwd_math(x_ref[...], g_ref[...], dk_ref[...], _wide(tc_ref, da), _wide(ta_ref, da),
                               _wide(tb_ref, da), da)

        @pl.when(i == 0)
        def _():
            dg_ref[...] = jnp.zeros_like(dg_ref)

        dg_ref[...] += dg
        o_ref[:, pl.ds(0, da)] = dx.astype(BF16)
        o_ref[:, pl.ds(da, da)] = dv_ref[...].astype(BF16)

    tab = pl.BlockSpec((tr, 128), lambda i: (i % per_seq, 0))
    return pl.pallas_call(
        body, name=name, grid=(t // tr,),
        in_specs=[_row_spec(tr, da), _full_spec((1, da)), tab, tab, tab] + [_row_spec(tr, da)] * 2,
        out_specs=[_row_spec(tr, w2), _full_spec((1, da))],
        out_shape=[jax.ShapeDtypeStruct((t, w2), BF16), jax.ShapeDtypeStruct((1, da), F32)],
        compiler_params=_params("arbitrary"))(kv, gain, *tables, dk, dv)


def _unit_index(dil, r, blk):
    if dil == 1:
        start = blk * SPAN
        return pl.ds(start if isinstance(start, int) else pl.multiple_of(start, SPAN), SPAN)
    return pl.ds(r + dil * SPAN * blk, SPAN, stride=dil)


def _unit_rows(ref, dil, r, blk):
    return ref[_unit_index(dil, r, blk), :]


def _store_rows(ref, dil, r, blk, val):
    ref[_unit_index(dil, r, blk), :] = val


def _over_units(dil, nblk, unit, carry0, after=None):
    for r in range(dil):
        carry = carry0
        for blk in range(nblk):
            carry = unit(r, blk, blk > 0, carry)
        if after is not None:
            after(r, carry)


QROWS = SPAN


def _half_keys(prev, cur, h, with_prev):
    if with_prev:
        return jnp.concatenate([prev[h * QROWS:], cur[:(h + 1) * QROWS]], axis=0)
    return cur[:(h + 1) * QROWS]


def _half_mask(h, with_prev):
    nk = SPAN + QROWS if with_prev else (h + 1) * QROWS
    qi = lax.broadcasted_iota(jnp.int32, (QROWS, nk), 0)
    kj = lax.broadcasted_iota(jnp.int32, (QROWS, nk), 1)
    if with_prev:
        return (kj >= qi) & (kj <= qi + SPAN)
    return kj <= qi + h * QROWS


_NT = (((1,), (1,)), ((), ()))
_TN = (((0,), (0,)), ((), ()))
_HEAD_LANES = [slice(hh * HEAD_DIM, (hh + 1) * HEAD_DIM) for hh in range(128 // HEAD_DIM)]


def _group_fwd(q_ref, k_ref, v_ref, o_ref, l_ref, dil, seq):
    nblk = seq // (dil * SPAN)
    scale = HEAD_DIM ** -0.5

    def unit(r, blk, with_prev, carry):
        kp, vp = carry if with_prev else (None, None)
        q = _unit_rows(q_ref, dil, r, blk) * scale
        kc = _unit_rows(k_ref, dil, r, blk)
        vc = _unit_rows(v_ref, dil, r, blk)
        o_rows, l_rows = [], []
        for h in range(SPAN // QROWS):
            qh = q[h * QROWS:(h + 1) * QROWS]
            kk = _half_keys(kp, kc, h, with_prev)
            vv = _half_keys(vp, vc, h, with_prev)
            mask = _half_mask(h, with_prev)
            outs, lses = [], []
            for sl in _HEAD_LANES:
                s = lax.dot_general(qh[:, sl].astype(BF16), kk[:, sl].astype(BF16), _NT, preferred_element_type=F32)
                s = jnp.where(mask, s, NEG_INF)
                mx = jnp.max(s, axis=-1, keepdims=True)
                p = jnp.exp(s - mx)
                den = jnp.sum(p, axis=-1, keepdims=True)
                o = jnp.dot(p.astype(BF16), vv[:, sl].astype(BF16), preferred_element_type=F32) / den
                outs.append(o)
                lses.append(jnp.broadcast_to(mx + jnp.log(den), (QROWS, HEAD_DIM)))
            o_rows.append(jnp.concatenate(outs, axis=1))
            l_rows.append(jnp.concatenate(lses, axis=1))
        _store_rows(o_ref, dil, r, blk, jnp.concatenate(o_rows, axis=0))
        _store_rows(l_ref, dil, r, blk, jnp.concatenate(l_rows, axis=0))
        return kc, vc

    _over_units(dil, nblk, unit, None)


def _attn_fwd(name, qn, kn, kv, p1, seq):
    t, da = kn.shape
    hp = da // 128
    ng = len(DILATIONS)

    def body(q0_ref, q1_ref, q2_ref, k_ref, v_ref, g_ref, o_ref, l_ref, m_ref, og, lg):
        for g, q_ref in enumerate((q0_ref, q1_ref, q2_ref)):
            _group_fwd(q_ref, k_ref, v_ref, og.at[g], lg.at[g], DILATIONS[g], seq)
        a0, a1, a2 = lg[0], lg[1], lg[2]
        mx = jnp.maximum(jnp.maximum(a0, a1), a2)
        e0, e1, e2 = jnp.exp(a0 - mx), jnp.exp(a1 - mx), jnp.exp(a2 - mx)
        den = e0 + e1 + e2
        o = (e0 * og[0] + e1 * og[1] + e2 * og[2]) / den
        o_ref[...] = o
        l_ref[...] = mx + jnp.log(den)
        gt = g_ref[...].astype(F32)
        m_ref[...] = (o * gt * _sigmoid(gt)).astype(BF16)

    blk_spec = lambda off: pl.BlockSpec((seq, 128), lambda b, h: (b, off + h))
    return pl.pallas_call(
        body, name=name, grid=(t // seq, hp),
        in_specs=[blk_spec(0), blk_spec(hp), blk_spec(2 * hp), blk_spec(0), blk_spec(hp), blk_spec(3 * hp)],
        out_specs=[blk_spec(0)] * 3,
        out_shape=[jax.ShapeDtypeStruct((t, da), F32)] * 2 + [jax.ShapeDtypeStruct((t, da), BF16)],
        scratch_shapes=[pltpu.VMEM((ng, seq, 128), F32), pltpu.VMEM((ng, seq, 128), F32)],
        compiler_params=_params("parallel", "parallel"))(qn, qn, qn, kn, kv, p1)


def _group_bwd(q_ref, k_ref, v_ref, do_ref, l_ref, d_ref, dq_ref, dk_ref, dv_ref, dil, seq, first):
    nblk = seq // (dil * SPAN)
    scale = HEAD_DIM ** -0.5

    def put(ref, r, blk, val):
        if not first:
            val = val + _unit_rows(ref, dil, r, blk)
        _store_rows(ref, dil, r, blk, val)

    def unit(r, blk, with_prev, carry):
        kp, vp, pend_k, pend_v = carry if with_prev else (None,) * 4
        q = _unit_rows(q_ref, dil, r, blk) * scale
        kc = _unit_rows(k_ref, dil, r, blk)
        vc = _unit_rows(v_ref, dil, r, blk)
        dov = _unit_rows(do_ref, dil, r, blk)
        lrow = _unit_rows(l_ref, dil, r, blk)
        drow = _unit_rows(d_ref, dil, r, blk)
        dq_rows, dk_halves, dv_halves = [], [], []
        for h in range(SPAN // QROWS):
            rows = slice(h * QROWS, (h + 1) * QROWS)
            kk = _half_keys(kp, kc, h, with_prev)
            vv = _half_keys(vp, vc, h, with_prev)
            mask = _half_mask(h, with_prev)
            dqs, dkcs, dvcs = [], [], []
            for hh, sl in enumerate(_HEAD_LANES):
                stat = slice(hh * HEAD_DIM, hh * HEAD_DIM + 1)
                qh = q[rows, sl].astype(BF16)
                doh = dov[rows, sl].astype(BF16)
                kh = kk[:, sl].astype(BF16)
                s = lax.dot_general(qh, kh, _NT, preferred_element_type=F32)
                p = jnp.where(mask, jnp.exp(s - lrow[rows, stat]), 0.0)
                dp = lax.dot_general(doh, vv[:, sl].astype(BF16), _NT, preferred_element_type=F32)
                ds = (p * (dp - drow[rows, stat])).astype(BF16)
                dqs.append(jnp.dot(ds, kh, preferred_element_type=F32) * scale)
                dkcs.append(lax.dot_general(ds, qh, _TN, preferred_element_type=F32))
                dvcs.append(lax.dot_general(p.astype(BF16), doh, _TN, preferred_element_type=F32))
            dq_rows.append(jnp.concatenate(dqs, axis=1))
            dk_halves.append(jnp.concatenate(dkcs, axis=1))
            dv_halves.append(jnp.concatenate(dvcs, axis=1))
        _store_rows(dq_ref, dil, r, blk, jnp.concatenate(dq_rows, axis=0))
        pad = jnp.zeros((QROWS, 128), F32)

        def split(halves):
            if len(halves) == 1:
                return (halves[0][:SPAN], halves[0][SPAN:]) if with_prev else (None, halves[0])
            lower, upper = halves
            if with_prev:
                prev = lower[:SPAN] + jnp.concatenate([pad, upper[:QROWS]], axis=0)
                cur = jnp.concatenate([lower[SPAN:], pad], axis=0) + upper[QROWS:]
                return prev, cur
            return None, jnp.concatenate([lower, pad], axis=0) + upper

        dk_prev, dk_cur = split(dk_halves)
        dv_prev, dv_cur = split(dv_halves)
        if with_prev:
            put(dk_ref, r, blk - 1, pend_k + dk_prev)
            put(dv_ref, r, blk - 1, pend_v + dv_prev)
        return kc, vc, dk_cur, dv_cur

    def after(r, carry):
        put(dk_ref, r, nblk - 1, carry[2])
        put(dv_ref, r, nblk - 1, carry[3])

    _over_units(dil, nblk, unit, None, after)


def _attn_bwd(name, qn, kn, kv, do, lse, dsum, seq):
    t, da = kn.shape
    hp = da // 128

    def body(q0_ref, q1_ref, q2_ref, k_ref, v_ref, do_ref, l_ref, d_ref, dq0_ref, dq1_ref, dq2_ref, dk_ref, dv_ref):
        groups = ((q0_ref, dq0_ref), (q1_ref, dq1_ref), (q2_ref, dq2_ref))
        for g, (q_ref, dq_ref) in enumerate(groups):
            _group_bwd(q_ref, k_ref, v_ref, do_ref, l_ref, d_ref, dq_ref, dk_ref, dv_ref, DILATIONS[g], seq, g == 0)

    blk_spec = lambda off: pl.BlockSpec((seq, 128), lambda b, h: (b, off + h))
    return pl.pallas_call(
        body, name=name, grid=(t // seq, hp),
        in_specs=[blk_spec(0), blk_spec(hp), blk_spec(2 * hp), blk_spec(0), blk_spec(hp), blk_spec(0), blk_spec(0), blk_spec(0)],
        out_specs=[blk_spec(0)] * 5,
        out_shape=[jax.ShapeDtypeStruct((t, da), F32)] * 5,
        compiler_params=_params("parallel", "parallel"))(qn, qn, qn, kn, kv, do, lse, dsum)


def _gate_bwd(name, dm, o, p1):
    t, da = o.shape
    tr = _tile(t, 256, 8)

    def body(dm_ref, o_ref, g_ref, do_ref, dg_ref, ds_ref):
        g = g_ref[...].astype(F32)
        sg = _sigmoid(g)
        dmv, ov = dm_ref[...].astype(F32), o_ref[...]
        do = dmv * (g * sg)
        do_ref[...] = do
        dg_ref[...] = (dmv * ov * (sg * (1.0 + g * (1.0 - sg)))).astype(BF16)
        ds_ref[...] = _seg_allsum64(do * ov)

    return pl.pallas_call(
        body, name=name, grid=(t // tr,),
        in_specs=[_row_spec(tr, da), _row_spec(tr, da), _row_spec(tr, da, 3)],
        out_specs=[_row_spec(tr, da)] * 3,
        out_shape=[jax.ShapeDtypeStruct((t, da), F32), jax.ShapeDtypeStruct((t, da), BF16), jax.ShapeDtypeStruct((t, da), F32)],
        compiler_params=_params("parallel"))(dm, o, p1)


def _cast_bf16(name, w2d, chip):
    r, c = w2d.shape
    tr = _tile(r, 256, 16)

    def body(chip_ref, x_ref, o_ref):
        o_ref[...] = x_ref[...].astype(BF16)

    grid_spec = pltpu.PrefetchScalarGridSpec(
        num_scalar_prefetch=1, grid=(r // tr,),
        in_specs=[pl.BlockSpec((tr, c), lambda i, m: (i, 0))],
        out_specs=pl.BlockSpec((None, tr, c), lambda i, m: (m[0], i, 0)))
    return pl.pallas_call(
        body, name=name, grid_spec=grid_spec, out_shape=jax.ShapeDtypeStruct((N_CHIPS, r, c), BF16),
        compiler_params=_params("parallel"))(chip, w2d)


def _adamw(name, w, g, m, v):
    r, c = w.shape
    tr = _tile(r, 256, 8)
    c1 = 1.0 - ADAM_B1 ** ADAM_STEP
    c2 = 1.0 - ADAM_B2 ** ADAM_STEP

    def body(w_ref, g_ref, m_ref, v_ref, d_ref, nm_ref, nv_ref):
        gv = g_ref[...]
        nm = ADAM_B1 * m_ref[...] + (1.0 - ADAM_B1) * gv
        nv = ADAM_B2 * v_ref[...] + (1.0 - ADAM_B2) * (gv * gv)
        nm_ref[...] = nm
        nv_ref[...] = nv
        d_ref[...] = -ADAM_LR * ((nm / c1) / (jnp.sqrt(nv / c2) + ADAM_EPS) + ADAM_WD * w_ref[...])

    sds = jax.ShapeDtypeStruct((r, c), F32)
    return pl.pallas_call(
        body, name=name, grid=(r // tr,), in_specs=[_row_spec(tr, c)] * 4, out_specs=[_row_spec(tr, c)] * 3,
        out_shape=[sds] * 3, compiler_params=_params("parallel"))(w, g, m, v)


def _pair_sum(name, gd, recv, core):
    _, r, c = gd.shape
    rh = r // 2
    tr = _tile(rh, 256, 8)
    nrt = rh // tr

    def body(c_ref, a_ref, b_ref, o_ref):
        o_ref[...] = (a_ref[...].astype(F32) + b_ref[...].astype(F32)).astype(BF16)

    grid_spec = pltpu.PrefetchScalarGridSpec(
        num_scalar_prefetch=1, grid=(N_CHIPS, nrt),
        in_specs=[pl.BlockSpec((None, tr, c), lambda j, i, cr: (j, cr[0] * nrt + i, 0)),
                  pl.BlockSpec((None, tr, c), lambda j, i, cr: (j, i, 0))],
        out_specs=pl.BlockSpec((None, tr, c), lambda j, i, cr: (j, i, 0)))
    return pl.pallas_call(
        body, name=name, grid_spec=grid_spec, out_shape=jax.ShapeDtypeStruct((N_CHIPS, rh, c), BF16),
        compiler_params=_params("parallel", "parallel"))(core, gd, recv)


def _chip_sum(name, sums, parts, order):
    _, rh, c = parts.shape
    tr = _tile(rh, 256, 16)
    nrt = rh // tr

    def body(o_ref_, s_ref, p1_ref, p2_ref, p3_ref, o_ref):
        acc = s_ref[...].astype(F32)
        for p_ref in (p1_ref, p2_ref, p3_ref):
            acc = acc + p_ref[...].astype(F32)
        o_ref[...] = acc

    slot = lambda k: pl.BlockSpec((None, tr, c), lambda i, o: (o[k], i, 0))
    grid_spec = pltpu.PrefetchScalarGridSpec(
        num_scalar_prefetch=1, grid=(nrt,),
        in_specs=[slot(0), slot(1), slot(2), slot(3)],
        out_specs=pl.BlockSpec((tr, c), lambda i, o: (o[N_CHIPS] * nrt + i, 0)))
    return pl.pallas_call(
        body, name=name, grid_spec=grid_spec, out_shape=jax.ShapeDtypeStruct((2 * rh, c), F32),
        compiler_params=_params("parallel"))(order, sums, parts, parts, parts)


HBM = pl.BlockSpec(memory_space=pl.ANY)


def _place():
    x, y, c = lax.axis_index("x"), lax.axis_index("y"), lax.axis_index("c")
    chips = [(1 - x, y), (x, 1 - y), (1 - x, 1 - y)]
    return x, y, c, chips


def _half(ref, hc):
    rows = ref.shape[0] // 2
    return ref.at[pl.ds(hc * rows, rows)]


class _Plan:
    def __init__(self, ins, out_shapes, aliases, n_sems, start, finish):
        self.ins, self.out_shapes, self.aliases, self.n_sems = list(ins), list(out_shapes), dict(aliases), n_sems
        self.start, self.finish = start, finish


def _gather_plan(shards):
    n = len(shards)

    def ici(place, outs, send_sems, recv_sems, i, k, slot):
        x, y, c, chips = place
        half = _half(outs[i].at[slot], c)
        return pltpu.make_async_remote_copy(
            src_ref=half, dst_ref=half, send_sem=send_sems.at[6 * i + k], recv_sem=recv_sems.at[6 * i + k],
            device_id=(chips[k][0], chips[k][1], c), device_id_type=MESH)

    def d2d(place, outs, send_sems, recv_sems, i, k, slot, hc):
        x, y, c, chips = place
        half = _half(outs[i].at[slot], hc)
        return pltpu.make_async_remote_copy(
            src_ref=half, dst_ref=half, send_sem=send_sems.at[6 * i + 3 + k], recv_sem=recv_sems.at[6 * i + 3 + k],
            device_id=(x, y, 1 - c), device_id_type=MESH)

    def start(place, ins, outs, send_sems, recv_sems):
        x, y, c, chips = place
        for i in range(n):
            for k in range(3):
                ici(place, outs, send_sems, recv_sems, i, k, 2 * x + y).start()

    def finish(place, ins, outs, send_sems, recv_sems):
        x, y, c, chips = place
        theirs = [2 * chip[0] + chip[1] for chip in chips]
        for i in range(n):
            for k in range(3):
                ici(place, outs, send_sems, recv_sems, i, k, theirs[k]).wait_recv()
                d2d(place, outs, send_sems, recv_sems, i, k, theirs[k], c).start()
        for i in range(n):
            for k in range(3):
                d2d(place, outs, send_sems, recv_sems, i, k, theirs[k], 1 - c).wait_recv()
        for i in range(n):
            for k in range(3):
                ici(place, outs, send_sems, recv_sems, i, k, 2 * x + y).wait_send()
                d2d(place, outs, send_sems, recv_sems, i, k, theirs[k], c).wait_send()

    return _Plan(shards, [jax.ShapeDtypeStruct(s.shape, s.dtype) for s in shards], {i: i for i in range(n)}, 6 * n,
                 start, finish)


def _scatter_plan(sums):
    n = len(sums)

    def copy(place, ins, outs, send_sems, recv_sems, i, k, src_slot, dst_slot):
        x, y, c, chips = place
        return pltpu.make_async_remote_copy(
            src_ref=ins[i].at[src_slot], dst_ref=outs[i].at[dst_slot],
            send_sem=send_sems.at[3 * i + k], recv_sem=recv_sems.at[3 * i + k],
            device_id=(chips[k][0], chips[k][1], c), device_id_type=MESH)

    def start(place, ins, outs, send_sems, recv_sems):
        x, y, c, chips = place
        for i in range(n):
            for k, chip in enumerate(chips):
                copy(place, ins, outs, send_sems, recv_sems, i, k, 2 * chip[0] + chip[1], 2 * x + y).start()

    def finish(place, ins, outs, send_sems, recv_sems):
        x, y, c, chips = place
        for i in range(n):
            for k, chip in enumerate(chips):
                theirs = 2 * chip[0] + chip[1]
                copy(place, ins, outs, send_sems, recv_sems, i, k, theirs, 2 * x + y).wait_send()
                copy(place, ins, outs, send_sems, recv_sems, i, k, theirs, theirs).wait_recv()

    return _Plan(sums, [jax.ShapeDtypeStruct(s.shape, s.dtype) for s in sums], {}, 3 * n, start, finish)


def _hosted_call(body, plan, *, name, grid, in_specs, out_specs, out_shape, scratch_shapes, args, sem):
    in_specs, out_specs, out_shape, scratch_shapes = list(in_specs), list(out_specs), list(out_shape), list(scratch_shapes)
    if plan is None:
        res = pl.pallas_call(body, name=name, grid=grid, in_specs=in_specs, out_specs=out_specs, out_shape=out_shape,
                             scratch_shapes=scratch_shapes, compiler_params=_params(*sem))(*args)
        return list(res), []
    n_in, n_out, n_scr = len(in_specs), len(out_specs), len(scratch_shapes)
    p_in, p_out = len(plan.ins), len(plan.out_shapes)

    def hosted(*refs):
        refs = list(refs)
        ins, pins = refs[:n_in], refs[n_in:n_in + p_in]
        outs = refs[n_in + p_in:n_in + p_in + n_out]
        pouts = refs[n_in + p_in + n_out:n_in + p_in + n_out + p_out]
        scr = refs[n_in + p_in + n_out + p_out:n_in + p_in + n_out + p_out + n_scr]
        send_sems, recv_sems = refs[-2:]
        place = _place()
        ids = [pl.program_id(d) for d in range(len(grid))]
        first = functools.reduce(jnp.logical_and, [i == 0 for i in ids])
        last = functools.reduce(jnp.logical_and, [i == g - 1 for i, g in zip(ids, grid)])

        @pl.when(first)
        def _():
            plan.start(place, pins, pouts, send_sems, recv_sems)

        body(*ins, *outs, *scr)

        @pl.when(last)
        def _():
            plan.finish(place, pins, pouts, send_sems, recv_sems)

    res = pl.pallas_call(
        hosted, name=name, grid=grid, in_specs=in_specs + [HBM] * p_in, out_specs=out_specs + [HBM] * p_out,
        out_shape=out_shape + plan.out_shapes,
        input_output_aliases={n_in + i: n_out + o for i, o in plan.aliases.items()},
        scratch_shapes=scratch_shapes + [pltpu.SemaphoreType.DMA((plan.n_sems,)), pltpu.SemaphoreType.DMA((plan.n_sems,))],
        compiler_params=_params(*(("arbitrary",) * len(grid))))(*args, *plan.ins)
    return list(res[:n_out]), list(res[n_out:])


def _run_plan(name, plan):
    p_in = len(plan.ins)

    def body(*refs):
        ins, outs = refs[:p_in], refs[p_in:p_in + len(plan.out_shapes)]
        send_sems, recv_sems = refs[-2:]
        place = _place()
        plan.start(place, ins, outs, send_sems, recv_sems)
        plan.finish(place, ins, outs, send_sems, recv_sems)

    return pl.pallas_call(
        body, name=name, in_specs=[HBM] * p_in, out_specs=[HBM] * len(plan.out_shapes), out_shape=plan.out_shapes,
        input_output_aliases=plan.aliases,
        scratch_shapes=[pltpu.SemaphoreType.DMA((plan.n_sems,)), pltpu.SemaphoreType.DMA((plan.n_sems,))],
        )(*plan.ins)


def _pair_exchange(name, grads):
    n = len(grads)

    def body(*refs):
        ins, outs = refs[:n], refs[n:2 * n]
        send_sems, recv_sems = refs[2 * n:]
        x, y, c, _ = _place()
        cps = []
        for i in range(n):
            rows = ins[i].shape[1] // 2
            cp = pltpu.make_async_remote_copy(
                src_ref=ins[i].at[:, pl.ds((1 - c) * rows, rows), :], dst_ref=outs[i],
                send_sem=send_sems.at[i], recv_sem=recv_sems.at[i], device_id=(x, y, 1 - c), device_id_type=MESH)
            cp.start()
            cps.append(cp)
        for cp in cps:
            cp.wait()

    return pl.pallas_call(
        body, name=name, in_specs=[HBM] * n, out_specs=[HBM] * n,
        out_shape=[jax.ShapeDtypeStruct((N_CHIPS, g.shape[1] // 2, g.shape[2]), g.dtype) for g in grads],
        scratch_shapes=[pltpu.SemaphoreType.DMA((n,)), pltpu.SemaphoreType.DMA((n,))],
        )(*grads)


def _sibling_join(grads):
    n = len(grads)

    def body(*refs):
        outs = refs[n:2 * n]
        send_sems, recv_sems = refs[2 * n:]
        x, y, c, _ = _place()
        cps = []
        for i in range(n):
            cp = pltpu.make_async_remote_copy(
                src_ref=_half(outs[i], c), dst_ref=_half(outs[i], c), send_sem=send_sems.at[i], recv_sem=recv_sems.at[i],
                device_id=(x, y, 1 - c), device_id_type=MESH)
            cp.start()
            cps.append(cp)
        for i, cp in enumerate(cps):
            cp.wait_send()
            pltpu.make_async_remote_copy(
                src_ref=_half(outs[i], 1 - c), dst_ref=_half(outs[i], 1 - c), send_sem=send_sems.at[i],
                recv_sem=recv_sems.at[i], device_id=(x, y, 1 - c), device_id_type=MESH).wait_recv()

    return pl.pallas_call(
        body, name="sibling_join", in_specs=[HBM] * n, out_specs=[HBM] * n,
        out_shape=[jax.ShapeDtypeStruct(g.shape, g.dtype) for g in grads],
        input_output_aliases={i: i for i in range(n)},
        scratch_shapes=[pltpu.SemaphoreType.DMA((n,)), pltpu.SemaphoreType.DMA((n,))],
        )(*grads)


def _gather8(name, block, reduce):
    m, n = block.shape

    def body(x_ref, out_ref, *scratch):
        if reduce:
            all_ref, send_sems, recv_sems, local_sem = scratch
        else:
            all_ref = out_ref
            send_sems, recv_sems, local_sem = scratch
        x, y, c, chips = _place()
        me, sibling = (x, y, c), (x, y, 1 - c)

        def rows(px, py, pc):
            return all_ref.at[pl.ds((4 * px + 2 * py + pc) * m, m), :]

        def copy(k, blk, to, src=None):
            return pltpu.make_async_remote_copy(
                src_ref=rows(*blk) if src is None else src, dst_ref=rows(*blk),
                send_sem=send_sems.at[k], recv_sem=recv_sems.at[k], device_id=to, device_id_type=MESH)

        mine = pltpu.make_async_copy(x_ref, rows(*me), local_sem)
        mine.start()
        first = [copy(0, me, sibling, src=x_ref)]
        first += [copy(1 + j, me, (chip[0], chip[1], c), src=x_ref) for j, chip in enumerate(chips)]
        for cp in first:
            cp.start()
        passed = [copy(4 + j, (chip[0], chip[1], c), sibling) for j, chip in enumerate(chips)]
        for j, chip in enumerate(chips):
            copy(1 + j, (chip[0], chip[1], c), me).wait_recv()
            passed[j].start()
        copy(0, sibling, me).wait_recv()
        for j, chip in enumerate(chips):
            copy(4 + j, (chip[0], chip[1], 1 - c), me).wait_recv()
        for cp in first + passed:
            cp.wait_send()
        mine.wait()
        if reduce:
            acc = all_ref[pl.ds(0, m), :]
            for d in range(1, 8):
                acc = acc + all_ref[pl.ds(d * m, m), :]
            out_ref[...] = acc

    sems = [pltpu.SemaphoreType.DMA((7,)), pltpu.SemaphoreType.DMA((7,)), pltpu.SemaphoreType.DMA]
    scratch = ([pltpu.VMEM((8 * m, n), F32)] if reduce else []) + sems
    return pl.pallas_call(
        body, name=name,
        out_shape=jax.ShapeDtypeStruct((m, n) if reduce else (8 * m, n), F32),
        in_specs=[pl.BlockSpec(memory_space=pltpu.VMEM)], out_specs=pl.BlockSpec(memory_space=pltpu.VMEM),
        scratch_shapes=scratch)(block)


def _pad_rows(a, rows):
    return jnp.concatenate([a, jnp.zeros((rows - a.shape[0], a.shape[1]), a.dtype)], axis=0)


def kernel(x, p, norm_g, w_in_a, conv_w, conv_b, ln_g, ln_b, w_out_a, kv_norm_g, w_kv, k_norm_g, w_in_b, q_norm_g, w_out_b, ple_norm_g, w_ple_gate, w_ple_proj, loss_target, m_norm_g, m_w_in_a, m_conv_w, m_conv_b, m_ln_g, m_ln_b, m_w_out_a, m_kv_norm_g, m_w_kv, m_k_norm_g, m_w_in_b, m_q_norm_g, m_w_out_b, m_ple_norm_g, m_w_ple_gate, m_w_ple_proj, v_norm_g, v_w_in_a, v_conv_w, v_conv_b, v_ln_g, v_ln_b, v_w_out_a, v_kv_norm_g, v_w_kv, v_k_norm_g, v_w_in_b, v_q_norm_g, v_w_out_b, v_ple_norm_g, v_w_ple_gate, v_w_ple_proj):
    nb, seq, dm = x.shape
    t = nb * seq
    ple = p.shape[-1]
    ccs = conv_w.shape[-1]
    cc = N_CHIPS * ccs
    da = dm
    nheads = da // HEAD_DIM
    assert seq == DILATIONS[-1] * SPAN and da % 128 == 0 and ccs % 128 == 0

    core = lax.axis_index("c").astype(jnp.int32).reshape(1)
    chip = (2 * lax.axis_index("x") + lax.axis_index("y")).astype(jnp.int32)
    chip1 = chip.reshape(1)
    sum_order = jnp.concatenate([(chip1 + k) % N_CHIPS for k in range(N_CHIPS)] + [core])

    x2 = x.reshape(t, dm)
    tgt2 = loss_target.reshape(t, dm)
    p0 = p[0].reshape(t, ple)
    p1 = p[1].reshape(t, ple)

    big = [
        ("w_in_a", w_in_a[0], "col"), ("w_out_a", w_out_a[0], "row"), ("w_kv", w_kv, "col"),
        ("w_in_b", w_in_b[0], "col"), ("w_out_b", w_out_b[0], "row"),
        ("w_ple_gate0", w_ple_gate[0], "row"), ("w_ple_gate1", w_ple_gate[1], "row"),
        ("w_ple_proj0", w_ple_proj[0], "col"), ("w_ple_proj1", w_ple_proj[1], "col"),
    ]
    shard_shape = {nm: w.shape for nm, w, _ in big}
    names = [nm for nm, _, _ in big]
    own = [_cast_bf16("cast_" + nm, w, chip1) for nm, w, _ in big]
    W = {names[0]: _run_plan("gather_w_in_a", _gather_plan(own[:1]))[0]}

    vec_rows = 40
    small = _pad_rows(jnp.concatenate([conv_w[0], conv_b, ln_g, ln_b], axis=0), vec_rows)
    allv = _gather8("gather_conv_vectors", small, reduce=False).reshape(N_CHIPS, 2, vec_rows, ccs)[:, 0]
    allv = allv.transpose(1, 0, 2).reshape(vec_rows, cc)
    cw_full, cb_full, lg_full, lb_full = allv[:HALO], allv[31:32], allv[32:33], allv[33:34]
    cw_full = cw_full * (lax.broadcasted_iota(jnp.int32, (HALO, 1), 0) < CONV_WIDTH).astype(F32)

    tables = _rope_tables(seq)
    gain_q = jnp.tile(q_norm_g[0][:, None, :], (1, nheads, 1)).reshape(1, 3 * da)
    gain_k = jnp.tile(k_norm_g[None, :], (1, nheads))
    g0, g1 = norm_g[0:1], norm_g[1:2]
    pg0, pg1 = ple_norm_g[0:1], ple_norm_g[1:2]
    kvg = kv_norm_g[None, :]

    (u0,) = _rms_fwd("rms_u0", x2, [g0])
    pa, gathered = _mm_nn("mm_in_a", u0, W["w_in_a"], "col", out_dtype=BF16, plan=_gather_plan(own[1:]))
    W.update(zip(names[1:], gathered))
    conv_out, m_a = _mixa_fwd("mixa_fwd", pa, cw_full, cb_full, lg_full, lb_full, seq)
    h0, r0 = _mm_nn("mm_out_a", m_a, W["w_out_a"], "row", resid=x2, norm_gain=pg0)
    gpre0 = _mm_nn("mm_gate0", r0, W["w_ple_gate0"], "row", out_dtype=BF16)
    pp0 = _mm_nn("mm_proj0", p0, W["w_ple_proj0"], "col", out_dtype=BF16)
    x1, kvn, u1 = _ple_fwd("ple_fwd0", h0, gpre0, pp0, [kvg, g1])
    kv = _mm_nn("mm_kv", kvn, W["w_kv"], "col")
    kn = _hnr_fwd("k_norm_rope", kv, da, gain_k, tables, seq)
    pb = _mm_nn("mm_in_b", u1, W["w_in_b"], "col", out_dtype=BF16)
    qn = _hnr_fwd("q_norm_rope", pb, 3 * da, gain_q, tables, seq)
    o, lse, m_b = _attn_fwd("attn_fwd", qn, kn, kv, pb, seq)
    h1, r1 = _mm_nn("mm_out_b", m_b, W["w_out_b"], "row", resid=x1, norm_gain=pg1)
    gpre1 = _mm_nn("mm_gate1", r1, W["w_ple_gate1"], "row", out_dtype=BF16)
    pp1 = _mm_nn("mm_proj1", p1, W["w_ple_proj1"], "col", out_dtype=BF16)
    dy, dgp1, dpp1, sq = _ple_loss("ple_loss", h1, gpre1, pp1, tgt2)
    loss = lax.psum(0.5 * sq[0, 0] / dm, ("x", "y", "c"))

    G = {}
    G["w_ple_gate1"] = _mm_tn("tn_gate1", r1, dgp1, "row", shard_shape["w_ple_gate1"])
    G["w_ple_proj1"] = _mm_tn("tn_proj1", p1, dpp1, "col", shard_shape["w_ple_proj1"])
    dr1 = _mm_nt("nt_gate1", dgp1, W["w_ple_gate1"], "row", out_dtype=BF16)
    dh1, (dpg1,) = _rms_bwd("rms_bwd_r1", h1, dy, [(pg1, dr1)])
    G["w_out_b"] = _mm_tn("tn_out_b", m_b, dh1, "row", shard_shape["w_out_b"])
    dm_b = _mm_nt("nt_out_b", dh1, W["w_out_b"], "row", out_dtype=BF16)
    d_o, dgt, dsum = _gate_bwd("gate_bwd", dm_b, o, pb)
    dq0, dq1, dq2, dk, dv = _attn_bwd("attn_bwd", qn, kn, kv, d_o, lse, dsum, seq)
    dpb, dgq = _q_bwd("q_bwd", pb, gain_q, tables, [dq0, dq1, dq2], dgt, seq)
    dkv, dgk = _k_bwd("k_bwd", kv, gain_k, tables, dk, dv, seq)
    G["w_in_b"] = _mm_tn("tn_in_b", u1, dpb, "col", shard_shape["w_in_b"])
    du1 = _mm_nt("nt_in_b", dpb, W["w_in_b"], "col", out_dtype=BF16)
    G["w_kv"] = _mm_tn("tn_kv", kvn, dkv, "col", shard_shape["w_kv"])
    dkvn = _mm_nt("nt_kv", dkv, W["w_kv"], "col", out_dtype=BF16)
    dx1, (dg1, dkvg), dgp0, dpp0 = _rms_bwd("rms_bwd_x1", x1, dh1, [(g1, du1), (kvg, dkvn)], ple=(gpre0, pp0))
    G["w_ple_gate0"] = _mm_tn("tn_gate0", r0, dgp0, "row", shard_shape["w_ple_gate0"])
    G["w_ple_proj0"] = _mm_tn("tn_proj0", p0, dpp0, "col", shard_shape["w_ple_proj0"])
    dr0 = _mm_nt("nt_gate0", dgp0, W["w_ple_gate0"], "row", out_dtype=BF16)
    dh0, (dpg0,) = _rms_bwd("rms_bwd_r0", h0, dx1, [(pg0, dr0)])
    G["w_out_a"] = _mm_tn("tn_out_a", m_a, dh0, "row", shard_shape["w_out_a"])
    dm_a = _mm_nt("nt_out_a", dh0, W["w_out_a"], "row", out_dtype=BF16)

    def pair_sums(tag, batch):
        recv = _pair_exchange("pair_exchange_" + tag, [G[nm] for nm in batch])
        return [_pair_sum("pair_sum_" + nm, G[nm], rc, core) for nm, rc in zip(batch, recv)]

    late = ["w_kv", "w_in_b", "w_out_b", "w_ple_gate1", "w_ple_proj1"]
    early = ["w_in_a", "w_out_a", "w_ple_gate0", "w_ple_proj0"]
    sums_late = pair_sums("late", late)
    (dc, dz, dlg, dlb, dcb), parts_late = _mixa_bwd1("mixa_bwd1", conv_out, pa, dm_a, lg_full, lb_full,
                                                     plan=_scatter_plan(sums_late))
    dpa, dcw = _mixa_bwd2("mixa_bwd2", dc, pa, dz, cw_full, seq)
    G["w_in_a"] = _mm_tn("tn_in_a", u0, dpa, "col", shard_shape["w_in_a"])
    sums_early = pair_sums("early", early)
    du0, parts_early = _mm_nt("nt_in_a", dpa, W["w_in_a"], "col", out_dtype=BF16, plan=_scatter_plan(sums_early))
    dx, (dg0,) = _rms_bwd("rms_bwd_x", x2, dh0, [(g0, du0)])
    grad_x = dx.reshape(nb, seq, dm)

    sums = dict(zip(late + early, sums_late + sums_early))
    parts = dict(zip(late + early, parts_late + parts_early))
    halves = [_chip_sum("chip_sum_" + nm, sums[nm], parts[nm], sum_order) for nm in names]
    gfull = dict(zip(names, _sibling_join(halves)))

    def as_rows(a):
        return a.reshape(-1, dm)

    small_parts = [as_rows(dcw), as_rows(dcb), as_rows(dlg), as_rows(dlb), dg0, dg1, dkvg, dpg0, dpg1, as_rows(dgk), as_rows(dgq)]
    counts = [a.shape[0] for a in small_parts]
    total = sum(counts)
    packed = _pad_rows(jnp.concatenate(small_parts, axis=0), -(-total // 8) * 8)
    red = _gather8("reduce_small", packed, reduce=True)
    pieces, off = [], 0
    for n_ in counts:
        pieces.append(red[off:off + n_])
        off += n_
    r_dcw, r_dcb, r_dlg, r_dlb, r_g0, r_g1, r_kvg, r_pg0, r_pg1, r_gk, r_gq = pieces
    my_cols = lambda a: lax.dynamic_slice_in_dim(a.reshape(-1, cc), chip * ccs, ccs, axis=1)
    small_grads = {
        "norm_g": jnp.concatenate([r_g0, r_g1], axis=0),
        "conv_w": my_cols(r_dcw)[:CONV_WIDTH],
        "conv_b": my_cols(r_dcb), "ln_g": my_cols(r_dlg), "ln_b": my_cols(r_dlb),
        "kv_norm_g": r_kvg,
        "k_norm_g": r_gk.reshape(nheads, HEAD_DIM).sum(axis=0, keepdims=True),
        "q_norm_g": r_gq.reshape(3, nheads, HEAD_DIM).sum(axis=1),
        "ple_norm_g": jnp.concatenate([r_pg0, r_pg1], axis=0),
    }

    given = dict(norm_g=norm_g, w_in_a=w_in_a, conv_w=conv_w, conv_b=conv_b, ln_g=ln_g, ln_b=ln_b, w_out_a=w_out_a,
                 kv_norm_g=kv_norm_g, w_kv=w_kv, k_norm_g=k_norm_g, w_in_b=w_in_b, q_norm_g=q_norm_g, w_out_b=w_out_b,
                 ple_norm_g=ple_norm_g, w_ple_gate=w_ple_gate, w_ple_proj=w_ple_proj)
    mom1 = dict(norm_g=m_norm_g, w_in_a=m_w_in_a, conv_w=m_conv_w, conv_b=m_conv_b, ln_g=m_ln_g, ln_b=m_ln_b,
                w_out_a=m_w_out_a, kv_norm_g=m_kv_norm_g, w_kv=m_w_kv, k_norm_g=m_k_norm_g, w_in_b=m_w_in_b,
                q_norm_g=m_q_norm_g, w_out_b=m_w_out_b, ple_norm_g=m_ple_norm_g, w_ple_gate=m_w_ple_gate,
                w_ple_proj=m_w_ple_proj)
    mom2 = dict(norm_g=v_norm_g, w_in_a=v_w_in_a, conv_w=v_conv_w, conv_b=v_conv_b, ln_g=v_ln_g, ln_b=v_ln_b,
                w_out_a=v_w_out_a, kv_norm_g=v_kv_norm_g, w_kv=v_w_kv, k_norm_g=v_k_norm_g, w_in_b=v_w_in_b,
                q_norm_g=v_q_norm_g, w_out_b=v_w_out_b, ple_norm_g=v_ple_norm_g, w_ple_gate=v_w_ple_gate,
                w_ple_proj=v_w_ple_proj)
    order = ["norm_g", "w_in_a", "conv_w", "conv_b", "ln_g", "ln_b", "w_out_a", "kv_norm_g", "w_kv", "k_norm_g", "w_in_b",
             "q_norm_g", "w_out_b", "ple_norm_g", "w_ple_gate", "w_ple_proj"]
    grads, deltas, new_m, new_v = {}, {}, {}, {}
    for nm in order:
        shape = given[nm].shape
        if nm in ("w_ple_gate", "w_ple_proj"):
            g2 = jnp.concatenate([gfull[nm + "0"], gfull[nm + "1"]], axis=0)
        elif nm in gfull:
            g2 = gfull[nm]
        else:
            g2 = small_grads[nm]
        two_d = g2.shape
        d2, m2, v2 = _adamw("adamw_" + nm, given[nm].reshape(two_d), g2, mom1[nm].reshape(two_d), mom2[nm].reshape(two_d))
        grads[nm], deltas[nm], new_m[nm], new_v[nm] = (a.reshape(shape) for a in (g2, d2, m2, v2))

    return (loss, grad_x, *[grads[n_] for n_ in order], *[deltas[n_] for n_ in order],
            *[new_m[n_] for n_ in order], *[new_v[n_] for n_ in order])
```

```python
import functools

import jax
import jax.numpy as jnp
from jax import lax
from jax.experimental import pallas as pl
from jax.experimental.pallas import tpu as pltpu

F32 = jnp.float32
BF16 = jnp.bfloat16
MESH = pl.DeviceIdType.MESH

EPS = 1e-6
NEG_INF = -1e30
HEAD_DIM = 64
ROPE_DIM = 16
ROPE_THETA = 500000.0
CONV_WIDTH = 31
SUBLANES = 8
CONV_ROWS = 64
HALO = 32
SPAN = 128
DILATIONS = (1, 4, 16)
ADAM_LR, ADAM_B1, ADAM_B2, ADAM_EPS, ADAM_WD, ADAM_STEP = 0.001, 0.9, 0.999, 1e-08, 0.01, 10
N_CHIPS = 4
VMEM_LIMIT = 56 * 1024 * 1024


def _tile(n, target, mult=128):
    best = None
    t = mult
    while t <= min(n, target):
        if n % t == 0:
            best = t
        t += mult
    return best if best is not None else n


def _params(*sem):
    return pltpu.CompilerParams(dimension_semantics=tuple(sem) if sem else None, vmem_limit_bytes=VMEM_LIMIT)


def _sigmoid(x):
    return 0.5 * jnp.tanh(0.5 * x) + 0.5


def _seg_allsum64(x):
    tr, w = x.shape
    cw = 256 if w % 256 == 0 else 128
    n = w // cw
    ri = lax.shift_right_logical(lax.broadcasted_iota(jnp.int32, (cw, cw), 0), 6)
    ci = lax.shift_right_logical(lax.broadcasted_iota(jnp.int32, (cw, cw), 1), 6)
    ones = (ri == ci).astype(BF16)
    hi = x.astype(BF16)
    lo = (x - hi.astype(F32)).astype(BF16)

    def stack(v):
        return jnp.concatenate([v[:, j * cw:(j + 1) * cw] for j in range(n)], axis=0)

    s = (jnp.dot(stack(hi), ones, preferred_element_type=F32)
         + jnp.dot(stack(lo), ones, preferred_element_type=F32))
    return jnp.concatenate([s[j * tr:(j + 1) * tr] for j in range(n)], axis=1)


def _colsum(x):
    return jnp.sum(x, axis=0, keepdims=True)


def _shards_view(w, kind):
    return w if kind == "col" else w.reshape(1, -1, w.shape[2])


def _mm_nn(name, a, w, kind, *, out_dtype=F32, resid=None, norm_gain=None, plan=None):
    t = a.shape[0]
    w = _shards_view(w, kind)
    ns, k, c = w.shape
    n = ns * c
    tm = _tile(t, 1024 if norm_gain is None else 512, 8)
    tk = _tile(k, 2048)
    tn = _tile(c, 1024)
    nk = k // tk
    per = c // tn
    assert norm_gain is None or tn == n, "the fused RMSNorm needs whole rows in one tile"
    n_in = 2 + (resid is not None) + (norm_gain is not None)

    def body(*refs):
        a_ref, w_ref = refs[:2]
        r_ref = refs[2] if resid is not None else None
        g_ref = refs[n_in - 1] if norm_gain is not None else None
        o_ref = refs[n_in]
        part = jnp.dot(a_ref[...].astype(BF16), w_ref[...], preferred_element_type=F32)

        def finish(out):
            if resid is not None:
                out = out + r_ref[...]
            o_ref[...] = out.astype(out_dtype)
            if norm_gain is not None:
                normed = out * lax.rsqrt(jnp.mean(out * out, axis=-1, keepdims=True) + EPS) * g_ref[...]
                refs[n_in + 1][...] = normed.astype(BF16)

        if nk == 1:
            finish(part)
            return
        acc = refs[-1]
        kk = pl.program_id(2)

        @pl.when(kk == 0)
        def _():
            acc[...] = part

        @pl.when(kk > 0)
        def _():
            acc[...] += part

        @pl.when(kk == nk - 1)
        def _():
            finish(acc[...])

    in_specs = [pl.BlockSpec((tm, tk), lambda i, j, kk: (i, kk)),
                pl.BlockSpec((None, tk, tn), lambda i, j, kk: (j // per, kk, j % per))]
    args = [a, w]
    if resid is not None:
        in_specs.append(pl.BlockSpec((tm, tn), lambda i, j, kk: (i, j)))
        args.append(resid)
    out_specs = [pl.BlockSpec((tm, tn), lambda i, j, kk: (i, j))]
    out_shape = [jax.ShapeDtypeStruct((t, n), out_dtype)]
    if norm_gain is not None:
        in_specs.append(pl.BlockSpec((1, n), lambda i, j, kk: (0, 0)))
        args.append(norm_gain)
        out_specs.append(pl.BlockSpec((tm, tn), lambda i, j, kk: (i, j)))
        out_shape.append(jax.ShapeDtypeStruct((t, n), BF16))
    outs, carried = _hosted_call(
        body, plan, name=name, grid=(t // tm, n // tn, nk), in_specs=in_specs, out_specs=out_specs, out_shape=out_shape,
        scratch_shapes=[pltpu.VMEM((tm, tn), F32)] if nk > 1 else [],
        args=args, sem=("parallel", "parallel", "arbitrary"))
    out = outs[0] if norm_gain is None else tuple(outs)
    return out if plan is None else (out, carried)


def _mm_nt(name, d, w, kind, *, out_dtype=F32, plan=None):
    t = d.shape[0]
    w = _shards_view(w, kind)
    ns, k, c = w.shape
    n = ns * c
    tm = _tile(t, 1024, 8)
    to = _tile(k, 1024)
    tc = _tile(c, 1536)
    nc = n // tc
    per = c // tc

    def body(d_ref, w_ref, o_ref, *scratch):
        part = lax.dot_general(d_ref[...].astype(BF16), w_ref[...], (((1,), (1,)), ((), ())),
                               preferred_element_type=F32)
        if nc == 1:
            o_ref[...] = part.astype(out_dtype)
            return
        acc = scratch[0]
        kk = pl.program_id(2)

        @pl.when(kk == 0)
        def _():
            acc[...] = part

        @pl.when(kk > 0)
        def _():
            acc[...] += part

        @pl.when(kk == nc - 1)
        def _():
            o_ref[...] = acc[...].astype(out_dtype)

    (out,), carried = _hosted_call(
        body, plan, name=name, grid=(t // tm, k // to, nc),
        in_specs=[pl.BlockSpec((tm, tc), lambda i, j, kk: (i, kk)),
                  pl.BlockSpec((None, to, tc), lambda i, j, kk: (kk // per, j, kk % per))],
        out_specs=[pl.BlockSpec((tm, to), lambda i, j, kk: (i, j))],
        out_shape=[jax.ShapeDtypeStruct((t, k), out_dtype)],
        scratch_shapes=[pltpu.VMEM((tm, to), F32)] if nc > 1 else [],
        args=[d, w], sem=("parallel", "parallel", "arbitrary"))
    return out if plan is None else (out, carried)


def _mm_tn(name, a, d, kind, shard_shape, whole=False):
    t, k = a.shape
    n = d.shape[1]
    ns = N_CHIPS if kind == "col" and not whole else 1
    c = n // ns
    tkm = _tile(k, 1024)
    tn = _tile(c, 1536)
    tt = _tile(t, 1024, 8)
    nt = t // tt
    per = c // tn

    def body(a_ref, d_ref, o_ref, acc):
        kk = pl.program_id(2)
        part = lax.dot_general(a_ref[...].astype(BF16), d_ref[...].astype(BF16), (((0,), (0,)), ((), ())),
                               preferred_element_type=F32)

        @pl.when(kk == 0)
        def _():
            acc[...] = part

        @pl.when(kk > 0)
        def _():
            acc[...] += part

        @pl.when(kk == nt - 1)
        def _():
            o_ref[...] = acc[...].astype(BF16)

    out = pl.pallas_call(
        body, name=name, grid=(k // tkm, n // tn, nt),
        in_specs=[pl.BlockSpec((tt, tkm), lambda i, j, kk: (kk, i)),
                  pl.BlockSpec((tt, tn), lambda i, j, kk: (kk, j))],
        out_specs=pl.BlockSpec((None, tkm, tn), lambda i, j, kk: (j // per, i, j % per)),
        out_shape=jax.ShapeDtypeStruct((ns, k, c), BF16),
        scratch_shapes=[pltpu.VMEM((tkm, tn), F32)],
        compiler_params=_params("parallel", "parallel", "arbitrary"))(a, d)
    if kind == "col" and whole:
        return out.reshape(k, N_CHIPS, n // N_CHIPS).transpose(1, 0, 2)
    return out.reshape((N_CHIPS,) + tuple(shard_shape))


def _row_spec(tr, w, col=0):
    return pl.BlockSpec((tr, w), lambda i: (i, col))


def _full_spec(shape):
    return pl.BlockSpec(shape, lambda i: tuple(0 for _ in shape))


def _rms_fwd(name, x, gains):
    t, dm = x.shape
    tr = _tile(t, 256, 8)
    n = len(gains)

    def body(x_ref, *refs):
        xv = x_ref[...]
        xn = xv * lax.rsqrt(jnp.mean(xv * xv, axis=-1, keepdims=True) + EPS)
        for g_ref, o_ref in zip(refs[:n], refs[n:]):
            o_ref[...] = (xn * g_ref[...]).astype(BF16)

    outs = pl.pallas_call(
        body, name=name, grid=(t // tr,),
        in_specs=[_row_spec(tr, dm)] + [_full_spec((1, dm))] * n,
        out_specs=[_row_spec(tr, dm)] * n,
        out_shape=[jax.ShapeDtypeStruct((t, dm), BF16)] * n,
        compiler_params=_params("parallel"))(x, *gains)
    return list(outs)


def _ple_bwd_math(dy, gpre, pp):
    sg = _sigmoid(gpre)
    return (dy * pp * sg * (1.0 - sg)).astype(BF16), (dy * sg).astype(BF16)


def _rms_bwd(name, x, resid, pairs, ple=None):
    t, dm = x.shape
    tr = _tile(t, 256, 8)
    n = len(pairs)
    n_in = 2 * n + (2 if ple is not None else 0)

    def body(x_ref, r_ref, *refs):
        ins, outs = refs[:n_in], refs[n_in:]
        i = pl.program_id(0)
        xv = x_ref[...]
        rs = lax.rsqrt(jnp.mean(xv * xv, axis=-1, keepdims=True) + EPS)
        xn = xv * rs
        total = r_ref[...]
        for kx in range(n):
            g_ref, du_ref = ins[2 * kx], ins[2 * kx + 1]
            dg_ref = outs[1 + kx]
            du = du_ref[...].astype(F32)

            @pl.when(i == 0)
            def _():
                dg_ref[...] = jnp.zeros_like(dg_ref)

            dg_ref[...] += _colsum(du * xn)
            dxh = du * g_ref[...]
            total = total + rs * (dxh - xn * jnp.mean(dxh * xn, axis=-1, keepdims=True))
        outs[0][...] = total
        if ple is not None:
            outs[1 + n][...], outs[2 + n][...] = _ple_bwd_math(total, ins[2 * n][...].astype(F32), ins[2 * n + 1][...].astype(F32))

    in_specs = [_row_spec(tr, dm), _row_spec(tr, dm)]
    args = [x, resid]
    for g, du in pairs:
        in_specs += [_full_spec((1, dm)), _row_spec(tr, dm)]
        args += [g, du]
    out_specs = [_row_spec(tr, dm)] + [_full_spec((1, dm))] * n
    out_shape = [jax.ShapeDtypeStruct((t, dm), F32)] + [jax.ShapeDtypeStruct((1, dm), F32)] * n
    if ple is not None:
        in_specs += [_row_spec(tr, dm)] * 2
        args += list(ple)
        out_specs += [_row_spec(tr, dm)] * 2
        out_shape += [jax.ShapeDtypeStruct((t, dm), BF16)] * 2
    outs = pl.pallas_call(
        body, name=name, grid=(t // tr,), in_specs=in_specs, out_specs=out_specs, out_shape=out_shape,
        compiler_params=_params("arbitrary"))(*args)
    if ple is not None:
        return outs[0], list(outs[1:1 + n]), outs[1 + n], outs[2 + n]
    return outs[0], list(outs[1:])


def _shifted_copies(ext, sh, rows):
    for s in range(1, SUBLANES):
        sh[s - 1] = ext[pl.ds(s, rows), :]


def _window(ext, sh, off, row0, rows, lanes):
    s = off % SUBLANES
    src = ext if s == 0 else sh.at[s - 1]
    return src[pl.ds(off - s + row0, rows), lanes]


def _mixa_fwd(name, pa, cw, cb, lg, lb, seq):
    t, w3 = pa.shape
    cc = w3 // 3
    tr = _tile(seq, 128, HALO)
    per_seq = seq // tr
    hb = tr // HALO
    lead = HALO - (CONV_WIDTH - 1)

    def body(a_ref, b_ref, z_ref, ah_ref, bh_ref, cw_ref, cb_ref, lg_ref, lb_ref, c_ref, m_ref, ext, sh, conv):
        i = pl.program_id(0)
        gh = ah_ref[...].astype(F32) * _sigmoid(bh_ref[...].astype(F32))
        ext[pl.ds(0, HALO), :] = jnp.where((i % per_seq) == 0, 0.0, gh)
        ext[pl.ds(HALO, tr), :] = a_ref[...].astype(F32) * _sigmoid(b_ref[...].astype(F32))
        _shifted_copies(ext, sh, tr + HALO - SUBLANES)
        for lc in range(cc // 128):
            lanes = pl.ds(lc * 128, 128)
            taps = [cw_ref[pl.ds(k, 1), lanes] for k in range(CONV_WIDTH)]
            for row0 in range(0, tr, CONV_ROWS):
                acc = jnp.broadcast_to(cb_ref[:, lanes], (CONV_ROWS, 128))
                for k in range(CONV_WIDTH):
                    acc = acc + _window(ext, sh, lead + k, row0, CONV_ROWS, lanes) * taps[k]
                conv[pl.ds(row0, CONV_ROWS), lanes] = acc
        acc = conv[...]
        c_ref[...] = acc.astype(BF16)
        xc = acc - jnp.mean(acc, axis=-1, keepdims=True)
        nrm = xc * lax.rsqrt(jnp.mean(xc * xc, axis=-1, keepdims=True) + EPS)
        l = nrm * lg_ref[...] + lb_ref[...]
        z = z_ref[...].astype(F32)
        m_ref[...] = (l * _sigmoid(l) * z * _sigmoid(z)).astype(BF16)

    halo = lambda col: pl.BlockSpec((HALO, cc), lambda i: (jnp.maximum(i * hb - 1, 0), col))
    return pl.pallas_call(
        body, name=name, grid=(t // tr,),
        in_specs=[_row_spec(tr, cc, 0), _row_spec(tr, cc, 1), _row_spec(tr, cc, 2), halo(0), halo(1),
                  _full_spec((HALO, cc)), _full_spec((1, cc)), _full_spec((1, cc)), _full_spec((1, cc))],
        out_specs=[_row_spec(tr, cc), _row_spec(tr, cc)],
        out_shape=[jax.ShapeDtypeStruct((t, cc), BF16), jax.ShapeDtypeStruct((t, cc), BF16)],
        scratch_shapes=[pltpu.VMEM((tr + HALO, cc), F32), pltpu.VMEM((SUBLANES - 1, tr + HALO - SUBLANES, cc), F32),
                        pltpu.VMEM((tr, cc), F32)],
        compiler_params=_params("parallel"))(pa, pa, pa, pa, pa, cw, cb, lg, lb)


def _mixa_bwd1(name, c, pa, dm, lg, lb, plan=None):
    t, cc = c.shape
    tr = _tile(t, 128, 8)

    def body(c_ref, z_ref, dm_ref, lg_ref, lb_ref, dc_ref, dz_ref, dlg_ref, dlb_ref, dcb_ref):
        i = pl.program_id(0)
        cv = c_ref[...].astype(F32)
        xc = cv - jnp.mean(cv, axis=-1, keepdims=True)
        rs = lax.rsqrt(jnp.mean(xc * xc, axis=-1, keepdims=True) + EPS)
        nrm = xc * rs
        l = nrm * lg_ref[...] + lb_ref[...]
        z = z_ref[...].astype(F32)
        sl, sz = _sigmoid(l), _sigmoid(z)
        dmv = dm_ref[...].astype(F32)
        ds = dmv * (z * sz)
        dzz = dmv * (l * sl)
        dz_ref[...] = (dzz * (sz * (1.0 + z * (1.0 - sz)))).astype(BF16)
        dl = ds * (sl * (1.0 + l * (1.0 - sl)))
        dn = dl * lg_ref[...]
        dc = rs * (dn - jnp.mean(dn, axis=-1, keepdims=True) - nrm * jnp.mean(dn * nrm, axis=-1, keepdims=True))
        dc_ref[...] = dc.astype(BF16)

        @pl.when(i == 0)
        def _():
            dlg_ref[...] = jnp.zeros_like(dlg_ref)
            dlb_ref[...] = jnp.zeros_like(dlb_ref)
            dcb_ref[...] = jnp.zeros_like(dcb_ref)

        dlg_ref[...] += _colsum(dl * nrm)
        dlb_ref[...] += _colsum(dl)
        dcb_ref[...] += _colsum(dc)

    vec = jax.ShapeDtypeStruct((1, cc), F32)
    outs, carried = _hosted_call(
        body, plan, name=name, grid=(t // tr,),
        in_specs=[_row_spec(tr, cc), _row_spec(tr, cc, 2), _row_spec(tr, cc), _full_spec((1, cc)), _full_spec((1, cc))],
        out_specs=[_row_spec(tr, cc), _row_spec(tr, cc)] + [_full_spec((1, cc))] * 3,
        out_shape=[jax.ShapeDtypeStruct((t, cc), BF16), jax.ShapeDtypeStruct((t, cc), BF16), vec, vec, vec],
        scratch_shapes=[], args=[c, pa, dm, lg, lb], sem=("arbitrary",))
    return outs if plan is None else (outs, carried)


def _mixa_bwd2(name, dc, pa, dz, cw, seq):
    t, cc = dc.shape
    tr = _tile(seq, 128, HALO)
    per_seq = seq // tr
    hb = tr // HALO
    steps = t // tr
    last_halo = t // HALO - 1

    def body(dc_ref, dcn_ref, a_ref, b_ref, dz_ref, cw_ref, dp_ref, dcw_ref, ext, sh, sums):
        i = pl.program_id(0)
        ext[pl.ds(0, tr), :] = dc_ref[...].astype(F32)
        ext[pl.ds(tr, HALO), :] = jnp.where((i % per_seq) == per_seq - 1, 0.0, dcn_ref[...].astype(F32))
        _shifted_copies(ext, sh, tr + HALO - SUBLANES)

        @pl.when(i == 0)
        def _():
            sums[...] = jnp.zeros_like(sums)
            dcw_ref[...] = jnp.zeros_like(dcw_ref)

        av = a_ref[...].astype(F32)
        sb = _sigmoid(b_ref[...].astype(F32))
        glu = av * sb
        dglu = jnp.zeros((tr, cc), F32)
        for k in range(CONV_WIDTH):
            wd = _window(ext, sh, CONV_WIDTH - 1 - k, 0, tr, slice(None))
            dglu = dglu + wd * cw_ref[pl.ds(k, 1), :]
            sums[pl.ds(k * SUBLANES, SUBLANES), :] += (wd * glu).reshape(tr // SUBLANES, SUBLANES, cc).sum(axis=0)
        dp_ref[:, pl.ds(0, cc)] = (dglu * sb).astype(BF16)
        dp_ref[:, pl.ds(cc, cc)] = (dglu * av * sb * (1.0 - sb)).astype(BF16)
        dp_ref[:, pl.ds(2 * cc, cc)] = dz_ref[...]

        @pl.when(i == steps - 1)
        def _():
            for k in range(CONV_WIDTH):
                dcw_ref[pl.ds(k, 1), :] = _colsum(sums[pl.ds(k * SUBLANES, SUBLANES), :])

    nxt = pl.BlockSpec((HALO, cc), lambda i: (jnp.minimum((i + 1) * hb, last_halo), 0))
    return pl.pallas_call(
        body, name=name, grid=(steps,),
        in_specs=[_row_spec(tr, cc), nxt, _row_spec(tr, cc, 0), _row_spec(tr, cc, 1), _row_spec(tr, cc),
                  _full_spec((HALO, cc))],
        out_specs=[_row_spec(tr, 3 * cc), _full_spec((HALO, cc))],
        out_shape=[jax.ShapeDtypeStruct((t, 3 * cc), BF16), jax.ShapeDtypeStruct((HALO, cc), F32)],
        scratch_shapes=[pltpu.VMEM((tr + HALO, cc), F32), pltpu.VMEM((SUBLANES - 1, tr + HALO - SUBLANES, cc), F32),
                        pltpu.VMEM((HALO * SUBLANES, cc), F32)],
        compiler_params=_params("arbitrary"))(dc, dc, pa, pa, dz, cw)


def _ple_fwd(name, h, gpre, pp, gains):
    t, dm = h.shape
    tr = _tile(t, 256, 8)

    n = len(gains)

    def body(h_ref, g_ref, p_ref, *refs):
        o_ref = refs[n]
        xv = h_ref[...] + _sigmoid(g_ref[...].astype(F32)) * p_ref[...].astype(F32)
        o_ref[...] = xv
        xn = xv * lax.rsqrt(jnp.mean(xv * xv, axis=-1, keepdims=True) + EPS)
        for gain_ref, n_ref in zip(refs[:n], refs[n + 1:]):
            n_ref[...] = (xn * gain_ref[...]).astype(BF16)

    return pl.pallas_call(
        body, name=name, grid=(t // tr,), in_specs=[_row_spec(tr, dm)] * 3 + [_full_spec((1, dm))] * n,
        out_specs=[_row_spec(tr, dm)] * (1 + n),
        out_shape=[jax.ShapeDtypeStruct((t, dm), F32)] + [jax.ShapeDtypeStruct((t, dm), BF16)] * n,
        compiler_params=_params("parallel"))(h, gpre, pp, *gains)


def _ple_loss(name, h, gpre, pp, target):
    t, dm = h.shape
    tr = _tile(t, 256, 8)

    def body(h_ref, g_ref, p_ref, t_ref, dy_ref, dg_ref, dp_ref, sq_ref):
        i = pl.program_id(0)
        gpre_v, pp_v = g_ref[...].astype(F32), p_ref[...].astype(F32)
        err = h_ref[...] + _sigmoid(gpre_v) * pp_v - t_ref[...]
        dy = err * (1.0 / dm)
        dy_ref[...] = dy
        dg_ref[...], dp_ref[...] = _ple_bwd_math(dy, gpre_v, pp_v)

        @pl.when(i == 0)
        def _():
            sq_ref[...] = jnp.zeros_like(sq_ref)

        sq_ref[...] += jnp.sum(jnp.sum(err * err, axis=1, keepdims=True), axis=0, keepdims=True)

    return pl.pallas_call(
        body, name=name, grid=(t // tr,), in_specs=[_row_spec(tr, dm)] * 4,
        out_specs=[_row_spec(tr, dm)] * 3 + [_full_spec((1, 1))],
        out_shape=[jax.ShapeDtypeStruct((t, dm), F32)] + [jax.ShapeDtypeStruct((t, dm), BF16)] * 2
        + [jax.ShapeDtypeStruct((1, 1), F32)],
        compiler_params=_params("arbitrary"))(h, gpre, pp, target)


def _rope_tables(seq):
    half = ROPE_DIM // 2
    inv = ROPE_THETA ** (-jnp.arange(half, dtype=F32) * (2.0 / ROPE_DIM))
    ang = jnp.arange(seq).astype(F32)[:, None] * inv[None, :]
    cos, sin = jnp.cos(ang), jnp.sin(ang)
    rest = HEAD_DIM - ROPE_DIM
    one = jnp.ones((seq, rest), F32)
    zero = jnp.zeros((seq, rest), F32)
    zh = jnp.zeros((seq, half), F32)
    tc = jnp.concatenate([cos, cos, one], axis=1)
    ta = jnp.concatenate([-sin, zh, zero], axis=1)
    tb = jnp.concatenate([zh, sin, zero], axis=1)
    return [jnp.tile(tb_, (1, 128 // HEAD_DIM)) for tb_ in (tc, ta, tb)]


def _wide(tab_ref, w):
    return jnp.tile(tab_ref[...], (1, w // 128))


def _hnr_fwd(name, src, width, gain, tables, seq):
    t = src.shape[0]
    tr = _tile(seq, 256, 8)
    per_seq = seq // tr

    def body(x_ref, g_ref, tc_ref, ta_ref, tb_ref, o_ref):
        xv = x_ref[...].astype(F32)
        rs = lax.rsqrt(_seg_allsum64(xv * xv) * (1.0 / HEAD_DIM) + EPS)
        y = xv * rs * g_ref[...]
        o_ref[...] = (y * _wide(tc_ref, width) + pltpu.roll(y, width - ROPE_DIM // 2, 1) * _wide(ta_ref, width)
                      + pltpu.roll(y, ROPE_DIM // 2, 1) * _wide(tb_ref, width))

    tab = pl.BlockSpec((tr, 128), lambda i: (i % per_seq, 0))
    return pl.pallas_call(
        body, name=name, grid=(t // tr,),
        in_specs=[_row_spec(tr, width), _full_spec((1, width)), tab, tab, tab],
        out_specs=_row_spec(tr, width), out_shape=jax.ShapeDtypeStruct((t, width), F32),
        compiler_params=_params("parallel"))(src, gain, *tables)


def _hnr_bwd_math(xv, gain, dout, tc, ta, tb, width):
    dy = dout * tc + pltpu.roll(dout * ta, ROPE_DIM // 2, 1) + pltpu.roll(dout * tb, width - ROPE_DIM // 2, 1)
    rs = lax.rsqrt(_seg_allsum64(xv * xv) * (1.0 / HEAD_DIM) + EPS)
    xn = xv * rs
    dyh = dy * gain
    dx = rs * (dyh - xn * (_seg_allsum64(dyh * xn) * (1.0 / HEAD_DIM)))
    return dx, _colsum(dy * xn)


def _q_bwd(name, p1, gain, tables, dqs, dgt, seq):
    t, w4 = p1.shape
    da = w4 // 4
    width = 3 * da
    tr = _tile(seq, 128, 8)
    per_seq = seq // tr

    def body(x_ref, g_ref, tc_ref, ta_ref, tb_ref, d0_ref, d1_ref, d2_ref, dgt_ref, o_ref, dg_ref):
        i = pl.program_id(0)
        dout = jnp.concatenate([d0_ref[...], d1_ref[...], d2_ref[...]], axis=1)
        dx, dg = _hnr_bwd_math(x_ref[...].astype(F32), g_ref[...], dout, _wide(tc_ref, width), _wide(ta_ref, width),
                               _wide(tb_ref, width), width)

        @pl.when(i == 0)
        def _():
            dg_ref[...] = jnp.zeros_like(dg_ref)

        dg_ref[...] += dg
        o_ref[:, pl.ds(0, width)] = dx.astype(BF16)
        o_ref[:, pl.ds(width, da)] = dgt_ref[...]

    tab = pl.BlockSpec((tr, 128), lambda i: (i % per_seq, 0))
    return pl.pallas_call(
        body, name=name, grid=(t // tr,),
        in_specs=[_row_spec(tr, width), _full_spec((1, width)), tab, tab, tab] + [_row_spec(tr, da)] * 4,
        out_specs=[_row_spec(tr, w4), _full_spec((1, width))],
        out_shape=[jax.ShapeDtypeStruct((t, w4), BF16), jax.ShapeDtypeStruct((1, width), F32)],
        compiler_params=_params("arbitrary"))(p1, gain, *tables, *dqs, dgt)


def _k_bwd(name, kv, gain, tables, dk, dv, seq):
    t, w2 = kv.shape
    da = w2 // 2
    tr = _tile(seq, 256, 8)
    per_seq = seq // tr

    def body(x_ref, g_ref, tc_ref, ta_ref, tb_ref, dk_ref, dv_ref, o_ref, dg_ref):
        i = pl.program_id(0)
        dx, dg = _hnr_bwd_math(x_ref[...], g_ref[...], dk_ref[...], _wide(tc_ref, da), _wide(ta_ref, da),
                               _wide(tb_ref, da), da)

        @pl.when(i == 0)
        def _():
            dg_ref[...] = jnp.zeros_like(dg_ref)

        dg_ref[...] += dg
        o_ref[:, pl.ds(0, da)] = dx.astype(BF16)
        o_ref[:, pl.ds(da, da)] = dv_ref[...].astype(BF16)

    tab = pl.BlockSpec((tr, 128), lambda i: (i % per_seq, 0))
    return pl.pallas_call(
        body, name=name, grid=(t // tr,),
        in_specs=[_row_spec(tr, da), _full_spec((1, da)), tab, tab, tab] + [_row_spec(tr, da)] * 2,
        out_specs=[_row_spec(tr, w2), _full_spec((1, da))],
        out_shape=[jax.ShapeDtypeStruct((t, w2), BF16), jax.ShapeDtypeStruct((1, da), F32)],
        compiler_params=_params("arbitrary"))(kv, gain, *tables, dk, dv)


def _unit_index(dil, r, blk):
    if dil == 1:
        start = blk * SPAN
        return pl.ds(start if isinstance(start, int) else pl.multiple_of(start, SPAN), SPAN)
    return pl.ds(r + dil * SPAN * blk, SPAN, stride=dil)


def _unit_rows(ref, dil, r, blk):
    return ref[_unit_index(dil, r, blk), :]


def _store_rows(ref, dil, r, blk, val):
    ref[_unit_index(dil, r, blk), :] = val


def _over_units(dil, nblk, unit, carry0, after=None):
    for r in range(dil):
        carry = carry0
        for blk in range(nblk):
            carry = unit(r, blk, blk > 0, carry)
        if after is not None:
            after(r, carry)


QROWS = SPAN


def _half_keys(prev, cur, h, with_prev):
    if with_prev:
        return jnp.concatenate([prev[h * QROWS:], cur[:(h + 1) * QROWS]], axis=0)
    return cur[:(h + 1) * QROWS]


def _half_mask(h, with_prev):
    nk = SPAN + QROWS if with_prev else (h + 1) * QROWS
    qi = lax.broadcasted_iota(jnp.int32, (QROWS, nk), 0)
    kj = lax.broadcasted_iota(jnp.int32, (QROWS, nk), 1)
    if with_prev:
        return (kj >= qi) & (kj <= qi + SPAN)
    return kj <= qi + h * QROWS


_NT = (((1,), (1,)), ((), ()))
_TN = (((0,), (0,)), ((), ()))
_HEAD_LANES = [slice(hh * HEAD_DIM, (hh + 1) * HEAD_DIM) for hh in range(128 // HEAD_DIM)]


def _group_fwd(q_ref, k_ref, v_ref, o_ref, l_ref, dil, seq):
    nblk = seq // (dil * SPAN)
    scale = HEAD_DIM ** -0.5

    def unit(r, blk, with_prev, carry):
        kp, vp = carry if with_prev else (None, None)
        q = _unit_rows(q_ref, dil, r, blk) * scale
        kc = _unit_rows(k_ref, dil, r, blk)
        vc = _unit_rows(v_ref, dil, r, blk)
        o_rows, l_rows = [], []
        for h in range(SPAN // QROWS):
            qh = q[h * QROWS:(h + 1) * QROWS]
            kk = _half_keys(kp, kc, h, with_prev)
            vv = _half_keys(vp, vc, h, with_prev)
            mask = _half_mask(h, with_prev)
            outs, lses = [], []
            for sl in _HEAD_LANES:
                s = lax.dot_general(qh[:, sl].astype(BF16), kk[:, sl].astype(BF16), _NT, preferred_element_type=F32)
                s = jnp.where(mask, s, NEG_INF)
                mx = jnp.max(s, axis=-1, keepdims=True)
                p = jnp.exp(s - mx)
                den = jnp.sum(p, axis=-1, keepdims=True)
                o = jnp.dot(p.astype(BF16), vv[:, sl].astype(BF16), preferred_element_type=F32) / den
                outs.append(o)
                lses.append(jnp.broadcast_to(mx + jnp.log(den), (QROWS, HEAD_DIM)))
            o_rows.append(jnp.concatenate(outs, axis=1))
            l_rows.append(jnp.concatenate(lses, axis=1))
        _store_rows(o_ref, dil, r, blk, jnp.concatenate(o_rows, axis=0))
        _store_rows(l_ref, dil, r, blk, jnp.concatenate(l_rows, axis=0))
        return kc, vc

    _over_units(dil, nblk, unit, None)


def _attn_fwd(name, qn, kn, kv, p1, seq):
    t, da = kn.shape
    hp = da // 128
    ng = len(DILATIONS)

    def body(q0_ref, q1_ref, q2_ref, k_ref, v_ref, g_ref, o_ref, l_ref, m_ref, og, lg):
        for g, q_ref in enumerate((q0_ref, q1_ref, q2_ref)):
            _group_fwd(q_ref, k_ref, v_ref, og.at[g], lg.at[g], DILATIONS[g], seq)
        a0, a1, a2 = lg[0], lg[1], lg[2]
        mx = jnp.maximum(jnp.maximum(a0, a1), a2)
        e0, e1, e2 = jnp.exp(a0 - mx), jnp.exp(a1 - mx), jnp.exp(a2 - mx)
        den = e0 + e1 + e2
        o = (e0 * og[0] + e1 * og[1] + e2 * og[2]) / den
        o_ref[...] = o
        l_ref[...] = mx + jnp.log(den)
        gt = g_ref[...].astype(F32)
        m_ref[...] = (o * gt * _sigmoid(gt)).astype(BF16)

    blk_spec = lambda off: pl.BlockSpec((seq, 128), lambda b, h: (b, off + h))
    return pl.pallas_call(
        body, name=name, grid=(t // seq, hp),
        in_specs=[blk_spec(0), blk_spec(hp), blk_spec(2 * hp), blk_spec(0), blk_spec(hp), blk_spec(3 * hp)],
        out_specs=[blk_spec(0)] * 3,
        out_shape=[jax.ShapeDtypeStruct((t, da), F32)] * 2 + [jax.ShapeDtypeStruct((t, da), BF16)],
        scratch_shapes=[pltpu.VMEM((ng, seq, 128), F32), pltpu.VMEM((ng, seq, 128), F32)],
        compiler_params=_params("parallel", "parallel"))(qn, qn, qn, kn, kv, p1)


def _group_bwd(q_ref, k_ref, v_ref, do_ref, l_ref, d_ref, dq_ref, dk_ref, dv_ref, dil, seq, first):
    nblk = seq // (dil * SPAN)
    scale = HEAD_DIM ** -0.5

    def put(ref, r, blk, val):
        if not first:
            val = val + _unit_rows(ref, dil, r, blk)
        _store_rows(ref, dil, r, blk, val)

    def unit(r, blk, with_prev, carry):
        kp, vp, pend_k, pend_v = carry if with_prev else (None,) * 4
        q = _unit_rows(q_ref, dil, r, blk) * scale
        kc = _unit_rows(k_ref, dil, r, blk)
        vc = _unit_rows(v_ref, dil, r, blk)
        dov = _unit_rows(do_ref, dil, r, blk)
        lrow = _unit_rows(l_ref, dil, r, blk)
        drow = _unit_rows(d_ref, dil, r, blk)
        dq_rows, dk_halves, dv_halves = [], [], []
        for h in range(SPAN // QROWS):
            rows = slice(h * QROWS, (h + 1) * QROWS)
            kk = _half_keys(kp, kc, h, with_prev)
            vv = _half_keys(vp, vc, h, with_prev)
            mask = _half_mask(h, with_prev)
            dqs, dkcs, dvcs = [], [], []
            for hh, sl in enumerate(_HEAD_LANES):
                stat = slice(hh * HEAD_DIM, hh * HEAD_DIM + 1)
                qh = q[rows, sl].astype(BF16)
                doh = dov[rows, sl].astype(BF16)
                kh = kk[:, sl].astype(BF16)
                s = lax.dot_general(qh, kh, _NT, preferred_element_type=F32)
                p = jnp.where(mask, jnp.exp(s - lrow[rows, stat]), 0.0)
                dp = lax.dot_general(doh, vv[:, sl].astype(BF16), _NT, preferred_element_type=F32)
                ds = (p * (dp - drow[rows, stat])).astype(BF16)
                dqs.append(jnp.dot(ds, kh, preferred_element_type=F32) * scale)
                dkcs.append(lax.dot_general(ds, qh, _TN, preferred_element_type=F32))
                dvcs.append(lax.dot_general(p.astype(BF16), doh, _TN, preferred_element_type=F32))
            dq_rows.append(jnp.concatenate(dqs, axis=1))
            dk_halves.append(jnp.concatenate(dkcs, axis=1))
            dv_halves.append(jnp.concatenate(dvcs, axis=1))
        _store_rows(dq_ref, dil, r, blk, jnp.concatenate(dq_rows, axis=0))
        pad = jnp.zeros((QROWS, 128), F32)

        def split(halves):
            if len(halves) == 1:
                return (halves[0][:SPAN], halves[0][SPAN:]) if with_prev else (None, halves[0])
            lower, upper = halves
            if with_prev:
                prev = lower[:SPAN] + jnp.concatenate([pad, upper[:QROWS]], axis=0)
                cur = jnp.concatenate([lower[SPAN:], pad], axis=0) + upper[QROWS:]
                return prev, cur
            return None, jnp.concatenate([lower, pad], axis=0) + upper

        dk_prev, dk_cur = split(dk_halves)
        dv_prev, dv_cur = split(dv_halves)
        if with_prev:
            put(dk_ref, r, blk - 1, pend_k + dk_prev)
            put(dv_ref, r, blk - 1, pend_v + dv_prev)
        return kc, vc, dk_cur, dv_cur

    def after(r, carry):
        put(dk_ref, r, nblk - 1, carry[2])
        put(dv_ref, r, nblk - 1, carry[3])

    _over_units(dil, nblk, unit, None, after)


def _attn_bwd(name, qn, kn, kv, do, lse, dsum, seq):
    t, da = kn.shape
    hp = da // 128

    def body(q0_ref, q1_ref, q2_ref, k_ref, v_ref, do_ref, l_ref, d_ref, dq0_ref, dq1_ref, dq2_ref, dk_ref, dv_ref):
        groups = ((q0_ref, dq0_ref), (q1_ref, dq1_ref), (q2_ref, dq2_ref))
        for g, (q_ref, dq_ref) in enumerate(groups):
            _group_bwd(q_ref, k_ref, v_ref, do_ref, l_ref, d_ref, dq_ref, dk_ref, dv_ref, DILATIONS[g], seq, g == 0)

    blk_spec = lambda off: pl.BlockSpec((seq, 128), lambda b, h: (b, off + h))
    return pl.pallas_call(
        body, name=name, grid=(t // seq, hp),
        in_specs=[blk_spec(0), blk_spec(hp), blk_spec(2 * hp), blk_spec(0), blk_spec(hp), blk_spec(0), blk_spec(0), blk_spec(0)],
        out_specs=[blk_spec(0)] * 5,
        out_shape=[jax.ShapeDtypeStruct((t, da), F32)] * 5,
        compiler_params=_params("parallel", "parallel"))(qn, qn, qn, kn, kv, do, lse, dsum)


def _gate_bwd(name, dm, o, p1):
    t, da = o.shape
    tr = _tile(t, 256, 8)

    def body(dm_ref, o_ref, g_ref, do_ref, dg_ref, ds_ref):
        g = g_ref[...].astype(F32)
        sg = _sigmoid(g)
        dmv, ov = dm_ref[...].astype(F32), o_ref[...]
        do = dmv * (g * sg)
        do_ref[...] = do
        dg_ref[...] = (dmv * ov * (sg * (1.0 + g * (1.0 - sg)))).astype(BF16)
        ds_ref[...] = _seg_allsum64(do * ov)

    return pl.pallas_call(
        body, name=name, grid=(t // tr,),
        in_specs=[_row_spec(tr, da), _row_spec(tr, da), _row_spec(tr, da, 3)],
        out_specs=[_row_spec(tr, da)] * 3,
        out_shape=[jax.ShapeDtypeStruct((t, da), F32), jax.ShapeDtypeStruct((t, da), BF16), jax.ShapeDtypeStruct((t, da), F32)],
        compiler_params=_params("parallel"))(dm, o, p1)


def _cast_bf16(name, w2d, chip):
    r, c = w2d.shape
    tr = _tile(r, 256, 16)

    def body(chip_ref, x_ref, o_ref):
        o_ref[...] = x_ref[...].astype(BF16)

    grid_spec = pltpu.PrefetchScalarGridSpec(
        num_scalar_prefetch=1, grid=(r // tr,),
        in_specs=[pl.BlockSpec((tr, c), lambda i, m: (i, 0))],
        out_specs=pl.BlockSpec((None, tr, c), lambda i, m: (m[0], i, 0)))
    return pl.pallas_call(
        body, name=name, grid_spec=grid_spec, out_shape=jax.ShapeDtypeStruct((N_CHIPS, r, c), BF16),
        compiler_params=_params("parallel"))(chip, w2d)


def _adamw(name, w, g, m, v):
    r, c = w.shape
    tr = _tile(r, 256, 8)
    c1 = 1.0 - ADAM_B1 ** ADAM_STEP
    c2 = 1.0 - ADAM_B2 ** ADAM_STEP

    def body(w_ref, g_ref, m_ref, v_ref, d_ref, nm_ref, nv_ref):
        gv = g_ref[...]
        nm = ADAM_B1 * m_ref[...] + (1.0 - ADAM_B1) * gv
        nv = ADAM_B2 * v_ref[...] + (1.0 - ADAM_B2) * (gv * gv)
        nm_ref[...] = nm
        nv_ref[...] = nv
        d_ref[...] = -ADAM_LR * ((nm / c1) / (jnp.sqrt(nv / c2) + ADAM_EPS) + ADAM_WD * w_ref[...])

    sds = jax.ShapeDtypeStruct((r, c), F32)
    return pl.pallas_call(
        body, name=name, grid=(r // tr,), in_specs=[_row_spec(tr, c)] * 4, out_specs=[_row_spec(tr, c)] * 3,
        out_shape=[sds] * 3, compiler_params=_params("parallel"))(w, g, m, v)


def _pair_sum(name, gd, recv, core):
    _, r, c = gd.shape
    rh = r // 2
    tr = _tile(rh, 256, 8)
    nrt = rh // tr

    def body(c_ref, a_ref, b_ref, o_ref):
        o_ref[...] = (a_ref[...].astype(F32) + b_ref[...].astype(F32)).astype(BF16)

    grid_spec = pltpu.PrefetchScalarGridSpec(
        num_scalar_prefetch=1, grid=(N_CHIPS, nrt),
        in_specs=[pl.BlockSpec((None, tr, c), lambda j, i, cr: (j, cr[0] * nrt + i, 0)),
                  pl.BlockSpec((None, tr, c), lambda j, i, cr: (j, i, 0))],
        out_specs=pl.BlockSpec((None, tr, c), lambda j, i, cr: (j, i, 0)))
    return pl.pallas_call(
        body, name=name, grid_spec=grid_spec, out_shape=jax.ShapeDtypeStruct((N_CHIPS, rh, c), BF16),
        compiler_params=_params("parallel", "parallel"))(core, gd, recv)


def _chip_sum(name, sums, parts, order):
    _, rh, c = parts.shape
    tr = _tile(rh, 256, 16)
    nrt = rh // tr

    def body(o_ref_, s_ref, p1_ref, p2_ref, p3_ref, o_ref):
        acc = s_ref[...].astype(F32)
        for p_ref in (p1_ref, p2_ref, p3_ref):
            acc = acc + p_ref[...].astype(F32)
        o_ref[...] = acc

    slot = lambda k: pl.BlockSpec((None, tr, c), lambda i, o: (o[k], i, 0))
    grid_spec = pltpu.PrefetchScalarGridSpec(
        num_scalar_prefetch=1, grid=(nrt,),
        in_specs=[slot(0), slot(1), slot(2), slot(3)],
        out_specs=pl.BlockSpec((tr, c), lambda i, o: (o[N_CHIPS] * nrt + i, 0)))
    return pl.pallas_call(
        body, name=name, grid_spec=grid_spec, out_shape=jax.ShapeDtypeStruct((2 * rh, c), F32),
        compiler_params=_params("parallel"))(order, sums, parts, parts, parts)


HBM = pl.BlockSpec(memory_space=pl.ANY)


def _place():
    x, y, c = lax.axis_index("x"), lax.axis_index("y"), lax.axis_index("c")
    chips = [(1 - x, y), (x, 1 - y), (1 - x, 1 - y)]
    return x, y, c, chips


def _half(ref, hc):
    rows = ref.shape[0] // 2
    return ref.at[pl.ds(hc * rows, rows)]


class _Plan:
    def __init__(self, ins, out_shapes, aliases, n_sems, start, finish):
        self.ins, self.out_shapes, self.aliases, self.n_sems = list(ins), list(out_shapes), dict(aliases), n_sems
        self.start, self.finish = start, finish


def _gather_plan(shards):
    n = len(shards)

    def ici(place, outs, send_sems, recv_sems, i, k, slot):
        x, y, c, chips = place
        half = _half(outs[i].at[slot], c)
        return pltpu.make_async_remote_copy(
            src_ref=half, dst_ref=half, send_sem=send_sems.at[6 * i + k], recv_sem=recv_sems.at[6 * i + k],
            device_id=(chips[k][0], chips[k][1], c), device_id_type=MESH)

    def d2d(place, outs, send_sems, recv_sems, i, k, slot, hc):
        x, y, c, chips = place
        half = _half(outs[i].at[slot], hc)
        return pltpu.make_async_remote_copy(
            src_ref=half, dst_ref=half, send_sem=send_sems.at[6 * i + 3 + k], recv_sem=recv_sems.at[6 * i + 3 + k],
            device_id=(x, y, 1 - c), device_id_type=MESH)

    def start(place, ins, outs, send_sems, recv_sems):
        x, y, c, chips = place
        for i in range(n):
            for k in range(3):
                ici(place, outs, send_sems, recv_sems, i, k, 2 * x + y).start()

    def finish(place, ins, outs, send_sems, recv_sems):
        x, y, c, chips = place
        theirs = [2 * chip[0] + chip[1] for chip in chips]
        for i in range(n):
            for k in range(3):
                ici(place, outs, send_sems, recv_sems, i, k, theirs[k]).wait_recv()
                d2d(place, outs, send_sems, recv_sems, i, k, theirs[k], c).start()
        for i in range(n):
            for k in range(3):
                d2d(place, outs, send_sems, recv_sems, i, k, theirs[k], 1 - c).wait_recv()
        for i in range(n):
            for k in range(3):
                ici(place, outs, send_sems, recv_sems, i, k, 2 * x + y).wait_send()
                d2d(place, outs, send_sems, recv_sems, i, k, theirs[k], c).wait_send()

    return _Plan(shards, [jax.ShapeDtypeStruct(s.shape, s.dtype) for s in shards], {i: i for i in range(n)}, 6 * n,
                 start, finish)


def _scatter_plan(sums):
    n = len(sums)

    def copy(place, ins, outs, send_sems, recv_sems, i, k, src_slot, dst_slot):
        x, y, c, chips = place
        return pltpu.make_async_remote_copy(
            src_ref=ins[i].at[src_slot], dst_ref=outs[i].at[dst_slot],
            send_sem=send_sems.at[3 * i + k], recv_sem=recv_sems.at[3 * i + k],
            device_id=(chips[k][0], chips[k][1], c), device_id_type=MESH)

    def start(place, ins, outs, send_sems, recv_sems):
        x, y, c, chips = place
        for i in range(n):
            for k, chip in enumerate(chips):
                copy(place, ins, outs, send_sems, recv_sems, i, k, 2 * chip[0] + chip[1], 2 * x + y).start()

    def finish(place, ins, outs, send_sems, recv_sems):
        x, y, c, chips = place
        for i in range(n):
            for k, chip in enumerate(chips):
                theirs = 2 * chip[0] + chip[1]
                copy(place, ins, outs, send_sems, recv_sems, i, k, theirs, 2 * x + y).wait_send()
                copy(place, ins, outs, send_sems, recv_sems, i, k, theirs, theirs).wait_recv()

    return _Plan(sums, [jax.ShapeDtypeStruct(s.shape, s.dtype) for s in sums], {}, 3 * n, start, finish)


def _hosted_call(body, plan, *, name, grid, in_specs, out_specs, out_shape, scratch_shapes, args, sem):
    in_specs, out_specs, out_shape, scratch_shapes = list(in_specs), list(out_specs), list(out_shape), list(scratch_shapes)
    if plan is None:
        res = pl.pallas_call(body, name=name, grid=grid, in_specs=in_specs, out_specs=out_specs, out_shape=out_shape,
                             scratch_shapes=scratch_shapes, compiler_params=_params(*sem))(*args)
        return list(res), []
    n_in, n_out, n_scr = len(in_specs), len(out_specs), len(scratch_shapes)
    p_in, p_out = len(plan.ins), len(plan.out_shapes)

    def hosted(*refs):
        refs = list(refs)
        ins, pins = refs[:n_in], refs[n_in:n_in + p_in]
        outs = refs[n_in + p_in:n_in + p_in + n_out]
        pouts = refs[n_in + p_in + n_out:n_in + p_in + n_out + p_out]
        scr = refs[n_in + p_in + n_out + p_out:n_in + p_in + n_out + p_out + n_scr]
        send_sems, recv_sems = refs[-2:]
        place = _place()
        ids = [pl.program_id(d) for d in range(len(grid))]
        first = functools.reduce(jnp.logical_and, [i == 0 for i in ids])
        last = functools.reduce(jnp.logical_and, [i == g - 1 for i, g in zip(ids, grid)])

        @pl.when(first)
        def _():
            plan.start(place, pins, pouts, send_sems, recv_sems)

        body(*ins, *outs, *scr)

        @pl.when(last)
        def _():
            plan.finish(place, pins, pouts, send_sems, recv_sems)

    res = pl.pallas_call(
        hosted, name=name, grid=grid, in_specs=in_specs + [HBM] * p_in, out_specs=out_specs + [HBM] * p_out,
        out_shape=out_shape + plan.out_shapes,
        input_output_aliases={n_in + i: n_out + o for i, o in plan.aliases.items()},
        scratch_shapes=scratch_shapes + [pltpu.SemaphoreType.DMA((plan.n_sems,)), pltpu.SemaphoreType.DMA((plan.n_sems,))],
        compiler_params=_params(*(("arbitrary",) * len(grid))))(*args, *plan.ins)
    return list(res[:n_out]), list(res[n_out:])


def _run_plan(name, plan):
    p_in = len(plan.ins)

    def body(*refs):
        ins, outs = refs[:p_in], refs[p_in:p_in + len(plan.out_shapes)]
        send_sems, recv_sems = refs[-2:]
        place = _place()
        plan.start(place, ins, outs, send_sems, recv_sems)
        plan.finish(place, ins, outs, send_sems, recv_sems)

    return pl.pallas_call(
        body, name=name, in_specs=[HBM] * p_in, out_specs=[HBM] * len(plan.out_shapes), out_shape=plan.out_shapes,
        input_output_aliases=plan.aliases,
        scratch_shapes=[pltpu.SemaphoreType.DMA((plan.n_sems,)), pltpu.SemaphoreType.DMA((plan.n_sems,))],
        )(*plan.ins)


def _pair_exchange(name, grads):
    n = len(grads)

    def body(*refs):
        ins, outs = refs[:n], refs[n:2 * n]
        send_sems, recv_sems = refs[2 * n:]
        x, y, c, _ = _place()
        cps = []
        for i in range(n):
            rows = ins[i].shape[1] // 2
            cp = pltpu.make_async_remote_copy(
                src_ref=ins[i].at[:, pl.ds((1 - c) * rows, rows), :], dst_ref=outs[i],
                send_sem=send_sems.at[i], recv_sem=recv_sems.at[i], device_id=(x, y, 1 - c), device_id_type=MESH)
            cp.start()
            cps.append(cp)
        for cp in cps:
            cp.wait()

    return pl.pallas_call(
        body, name=name, in_specs=[HBM] * n, out_specs=[HBM] * n,
        out_shape=[jax.ShapeDtypeStruct((N_CHIPS, g.shape[1] // 2, g.shape[2]), g.dtype) for g in grads],
        scratch_shapes=[pltpu.SemaphoreType.DMA((n,)), pltpu.SemaphoreType.DMA((n,))],
        )(*grads)


def _sibling_join(grads):
    n = len(grads)

    def body(*refs):
        outs = refs[n:2 * n]
        send_sems, recv_sems = refs[2 * n:]
        x, y, c, _ = _place()
        cps = []
        for i in range(n):
            cp = pltpu.make_async_remote_copy(
                src_ref=_half(outs[i], c), dst_ref=_half(outs[i], c), send_sem=send_sems.at[i], recv_sem=recv_sems.at[i],
                device_id=(x, y, 1 - c), device_id_type=MESH)
            cp.start()
            cps.append(cp)
        for i, cp in enumerate(cps):
            cp.wait_send()
            pltpu.make_async_remote_copy(
                src_ref=_half(outs[i], 1 - c), dst_ref=_half(outs[i], 1 - c), send_sem=send_sems.at[i],
                recv_sem=recv_sems.at[i], device_id=(x, y, 1 - c), device_id_type=MESH).wait_recv()

    return pl.pallas_call(
        body, name="sibling_join", in_specs=[HBM] * n, out_specs=[HBM] * n,
        out_shape=[jax.ShapeDtypeStruct(g.shape, g.dtype) for g in grads],
        input_output_aliases={i: i for i in range(n)},
        scratch_shapes=[pltpu.SemaphoreType.DMA((n,)), pltpu.SemaphoreType.DMA((n,))],
        )(*grads)


def _gather8(name, block, reduce):
    m, n = block.shape

    def body(x_ref, out_ref, *scratch):
        if reduce:
            all_ref, send_sems, recv_sems, local_sem = scratch
        else:
            all_ref = out_ref
            send_sems, recv_sems, local_sem = scratch
        x, y, c, chips = _place()
        me, sibling = (x, y, c), (x, y, 1 - c)

        def rows(px, py, pc):
            return all_ref.at[pl.ds((4 * px + 2 * py + pc) * m, m), :]

        def copy(k, blk, to, src=None):
            return pltpu.make_async_remote_copy(
                src_ref=rows(*blk) if src is None else src, dst_ref=rows(*blk),
                send_sem=send_sems.at[k], recv_sem=recv_sems.at[k], device_id=to, device_id_type=MESH)

        mine = pltpu.make_async_copy(x_ref, rows(*me), local_sem)
        mine.start()
        first = [copy(0, me, sibling, src=x_ref)]
        first += [copy(1 + j, me, (chip[0], chip[1], c), src=x_ref) for j, chip in enumerate(chips)]
        for cp in first:
            cp.start()
        passed = [copy(4 + j, (chip[0], chip[1], c), sibling) for j, chip in enumerate(chips)]
        for j, chip in enumerate(chips):
            copy(1 + j, (chip[0], chip[1], c), me).wait_recv()
            passed[j].start()
        copy(0, sibling, me).wait_recv()
        for j, chip in enumerate(chips):
            copy(4 + j, (chip[0], chip[1], 1 - c), me).wait_recv()
        for cp in first + passed:
            cp.wait_send()
        mine.wait()
        if reduce:
            acc = all_ref[pl.ds(0, m), :]
            for d in range(1, 8):
                acc = acc + all_ref[pl.ds(d * m, m), :]
            out_ref[...] = acc

    sems = [pltpu.SemaphoreType.DMA((7,)), pltpu.SemaphoreType.DMA((7,)), pltpu.SemaphoreType.DMA]
    scratch = ([pltpu.VMEM((8 * m, n), F32)] if reduce else []) + sems
    return pl.pallas_call(
        body, name=name,
        out_shape=jax.ShapeDtypeStruct((m, n) if reduce else (8 * m, n), F32),
        in_specs=[pl.BlockSpec(memory_space=pltpu.VMEM)], out_specs=pl.BlockSpec(memory_space=pltpu.VMEM),
        scratch_shapes=scratch)(block)


def _pad_rows(a, rows):
    return jnp.concatenate([a, jnp.zeros((rows - a.shape[0], a.shape[1]), a.dtype)], axis=0)


def kernel(x, p, norm_g, w_in_a, conv_w, conv_b, ln_g, ln_b, w_out_a, kv_norm_g, w_kv, k_norm_g, w_in_b, q_norm_g, w_out_b, ple_norm_g, w_ple_gate, w_ple_proj, loss_target, m_norm_g, m_w_in_a, m_conv_w, m_conv_b, m_ln_g, m_ln_b, m_w_out_a, m_kv_norm_g, m_w_kv, m_k_norm_g, m_w_in_b, m_q_norm_g, m_w_out_b, m_ple_norm_g, m_w_ple_gate, m_w_ple_proj, v_norm_g, v_w_in_a, v_conv_w, v_conv_b, v_ln_g, v_ln_b, v_w_out_a, v_kv_norm_g, v_w_kv, v_k_norm_g, v_w_in_b, v_q_norm_g, v_w_out_b, v_ple_norm_g, v_w_ple_gate, v_w_ple_proj):
    nb, seq, dm = x.shape
    t = nb * seq
    ple = p.shape[-1]
    ccs = conv_w.shape[-1]
    cc = N_CHIPS * ccs
    da = dm
    nheads = da // HEAD_DIM
    assert seq == DILATIONS[-1] * SPAN and da % 128 == 0 and ccs % 128 == 0

    core = lax.axis_index("c").astype(jnp.int32).reshape(1)
    chip = (2 * lax.axis_index("x") + lax.axis_index("y")).astype(jnp.int32)
    chip1 = chip.reshape(1)
    sum_order = jnp.concatenate([(chip1 + k) % N_CHIPS for k in range(N_CHIPS)] + [core])

    x2 = x.reshape(t, dm)
    tgt2 = loss_target.reshape(t, dm)
    p0 = p[0].reshape(t, ple)
    p1 = p[1].reshape(t, ple)

    big = [
        ("w_in_a", w_in_a[0], "col"), ("w_out_a", w_out_a[0], "row"), ("w_kv", w_kv, "col"),
        ("w_in_b", w_in_b[0], "col"), ("w_out_b", w_out_b[0], "row"),
        ("w_ple_gate0", w_ple_gate[0], "row"), ("w_ple_gate1", w_ple_gate[1], "row"),
        ("w_ple_proj0", w_ple_proj[0], "col"), ("w_ple_proj1", w_ple_proj[1], "col"),
    ]
    shard_shape = {nm: w.shape for nm, w, _ in big}
    names = [nm for nm, _, _ in big]
    own = [_cast_bf16("cast_" + nm, w, chip1) for nm, w, _ in big]
    W = {names[0]: _run_plan("gather_w_in_a", _gather_plan(own[:1]))[0]}

    vec_rows = 40
    small = _pad_rows(jnp.concatenate([conv_w[0], conv_b, ln_g, ln_b], axis=0), vec_rows)
    allv = _gather8("gather_conv_vectors", small, reduce=False).reshape(N_CHIPS, 2, vec_rows, ccs)[:, 0]
    allv = allv.transpose(1, 0, 2).reshape(vec_rows, cc)
    cw_full, cb_full, lg_full, lb_full = allv[:HALO], allv[31:32], allv[32:33], allv[33:34]
    cw_full = cw_full * (lax.broadcasted_iota(jnp.int32, (HALO, 1), 0) < CONV_WIDTH).astype(F32)

    tables = _rope_tables(seq)
    gain_q = jnp.tile(q_norm_g[0][:, None, :], (1, nheads, 1)).reshape(1, 3 * da)
    gain_k = jnp.tile(k_norm_g[None, :], (1, nheads))
    g0, g1 = norm_g[0:1], norm_g[1:2]
    pg0, pg1 = ple_norm_g[0:1], ple_norm_g[1:2]
    kvg = kv_norm_g[None, :]

    (u0,) = _rms_fwd("rms_u0", x2, [g0])
    pa, gathered = _mm_nn("mm_in_a", u0, W["w_in_a"], "col", out_dtype=BF16, plan=_gather_plan(own[1:]))
    W.update(zip(names[1:], gathered))
    for nm in ("w_ple_proj0", "w_ple_proj1"):
        W[nm] = W[nm].transpose(1, 0, 2).reshape(1, W[nm].shape[1], -1)
    conv_out, m_a = _mixa_fwd("mixa_fwd", pa, cw_full, cb_full, lg_full, lb_full, seq)
    h0, r0 = _mm_nn("mm_out_a", m_a, W["w_out_a"], "row", resid=x2, norm_gain=pg0)
    gpre0 = _mm_nn("mm_gate0", r0, W["w_ple_gate0"], "row", out_dtype=BF16)
    pp0 = _mm_nn("mm_proj0", p0, W["w_ple_proj0"], "col", out_dtype=BF16)
    x1, kvn, u1 = _ple_fwd("ple_fwd0", h0, gpre0, pp0, [kvg, g1])
    kv = _mm_nn("mm_kv", kvn, W["w_kv"], "col")
    kn = _hnr_fwd("k_norm_rope", kv, da, gain_k, tables, seq)
    pb = _mm_nn("mm_in_b", u1, W["w_in_b"], "col", out_dtype=BF16)
    qn = _hnr_fwd("q_norm_rope", pb, 3 * da, gain_q, tables, seq)
    o, lse, m_b = _attn_fwd("attn_fwd", qn, kn, kv, pb, seq)
    h1, r1 = _mm_nn("mm_out_b", m_b, W["w_out_b"], "row", resid=x1, norm_gain=pg1)
    gpre1 = _mm_nn("mm_gate1", r1, W["w_ple_gate1"], "row", out_dtype=BF16)
    pp1 = _mm_nn("mm_proj1", p1, W["w_ple_proj1"], "col", out_dtype=BF16)
    dy, dgp1, dpp1, sq = _ple_loss("ple_loss", h1, gpre1, pp1, tgt2)
    loss = lax.psum(0.5 * sq[0, 0] / dm, ("x", "y", "c"))

    G = {}
    G["w_ple_gate1"] = _mm_tn("tn_gate1", r1, dgp1, "row", shard_shape["w_ple_gate1"])
    G["w_ple_proj1"] = _mm_tn("tn_proj1", p1, dpp1, "col", shard_shape["w_ple_proj1"], whole=True)
    dr1 = _mm_nt("nt_gate1", dgp1, W["w_ple_gate1"], "row", out_dtype=BF16)
    dh1, (dpg1,) = _rms_bwd("rms_bwd_r1", h1, dy, [(pg1, dr1)])
    G["w_out_b"] = _mm_tn("tn_out_b", m_b, dh1, "row", shard_shape["w_out_b"])
    dm_b = _mm_nt("nt_out_b", dh1, W["w_out_b"], "row", out_dtype=BF16)
    d_o, dgt, dsum = _gate_bwd("gate_bwd", dm_b, o, pb)
    dq0, dq1, dq2, dk, dv = _attn_bwd("attn_bwd", qn, kn, kv, d_o, lse, dsum, seq)
    dpb, dgq = _q_bwd("q_bwd", pb, gain_q, tables, [dq0, dq1, dq2], dgt, seq)
    dkv, dgk = _k_bwd("k_bwd", kv, gain_k, tables, dk, dv, seq)
    G["w_in_b"] = _mm_tn("tn_in_b", u1, dpb, "col", shard_shape["w_in_b"])
    du1 = _mm_nt("nt_in_b", dpb, W["w_in_b"], "col", out_dtype=BF16)
    G["w_kv"] = _mm_tn("tn_kv", kvn, dkv, "col", shard_shape["w_kv"])
    dkvn = _mm_nt("nt_kv", dkv, W["w_kv"], "col", out_dtype=BF16)
    dx1, (dg1, dkvg), dgp0, dpp0 = _rms_bwd("rms_bwd_x1", x1, dh1, [(g1, du1), (kvg, dkvn)], ple=(gpre0, pp0))
    G["w_ple_gate0"] = _mm_tn("tn_gate0", r0, dgp0, "row", shard_shape["w_ple_gate0"])
    G["w_ple_proj0"] = _mm_tn("tn_proj0", p0, dpp0, "col", shard_shape["w_ple_proj0"], whole=True)
    dr0 = _mm_nt("nt_gate0", dgp0, W["w_ple_gate0"], "row", out_dtype=BF16)
    dh0, (dpg0,) = _rms_bwd("rms_bwd_r0", h0, dx1, [(pg0, dr0)])
    G["w_out_a"] = _mm_tn("tn_out_a", m_a, dh0, "row", shard_shape["w_out_a"])
    dm_a = _mm_nt("nt_out_a", dh0, W["w_out_a"], "row", out_dtype=BF16)

    def pair_sums(tag, batch):
        recv = _pair_exchange("pair_exchange_" + tag, [G[nm] for nm in batch])
        return [_pair_sum("pair_sum_" + nm, G[nm], rc, core) for nm, rc in zip(batch, recv)]

    late = ["w_kv", "w_in_b", "w_out_b", "w_ple_gate1", "w_ple_proj1"]
    early = ["w_in_a", "w_out_a", "w_ple_gate0", "w_ple_proj0"]
    sums_late = pair_sums("late", late)
    (dc, dz, dlg, dlb, dcb), parts_late = _mixa_bwd1("mixa_bwd1", conv_out, pa, dm_a, lg_full, lb_full,
                                                     plan=_scatter_plan(sums_late))
    dpa, dcw = _mixa_bwd2("mixa_bwd2", dc, pa, dz, cw_full, seq)
    G["w_in_a"] = _mm_tn("tn_in_a", u0, dpa, "col", shard_shape["w_in_a"])
    sums_early = pair_sums("early", early)
    du0, parts_early = _mm_nt("nt_in_a", dpa, W["w_in_a"], "col", out_dtype=BF16, plan=_scatter_plan(sums_early))
    dx, (dg0,) = _rms_bwd("rms_bwd_x", x2, dh0, [(g0, du0)])
    grad_x = dx.reshape(nb, seq, dm)

    sums = dict(zip(late + early, sums_late + sums_early))
    parts = dict(zip(late + early, parts_late + parts_early))
    halves = [_chip_sum("chip_sum_" + nm, sums[nm], parts[nm], sum_order) for nm in names]
    gfull = dict(zip(names, _sibling_join(halves)))

    def as_rows(a):
        return a.reshape(-1, dm)

    small_parts = [as_rows(dcw), as_rows(dcb), as_rows(dlg), as_rows(dlb), dg0, dg1, dkvg, dpg0, dpg1, as_rows(dgk), as_rows(dgq)]
    counts = [a.shape[0] for a in small_parts]
    total = sum(counts)
    packed = _pad_rows(jnp.concatenate(small_parts, axis=0), -(-total // 8) * 8)
    red = _gather8("reduce_small", packed, reduce=True)
    pieces, off = [], 0
    for n_ in counts:
        pieces.append(red[off:off + n_])
        off += n_
    r_dcw, r_dcb, r_dlg, r_dlb, r_g0, r_g1, r_kvg, r_pg0, r_pg1, r_gk, r_gq = pieces
    my_cols = lambda a: lax.dynamic_slice_in_dim(a.reshape(-1, cc), chip * ccs, ccs, axis=1)
    small_grads = {
        "norm_g": jnp.concatenate([r_g0, r_g1], axis=0),
        "conv_w": my_cols(r_dcw)[:CONV_WIDTH],
        "conv_b": my_cols(r_dcb), "ln_g": my_cols(r_dlg), "ln_b": my_cols(r_dlb),
        "kv_norm_g": r_kvg,
        "k_norm_g": r_gk.reshape(nheads, HEAD_DIM).sum(axis=0, keepdims=True),
        "q_norm_g": r_gq.reshape(3, nheads, HEAD_DIM).sum(axis=1),
        "ple_norm_g": jnp.concatenate([r_pg0, r_pg1], axis=0),
    }

    given = dict(norm_g=norm_g, w_in_a=w_in_a, conv_w=conv_w, conv_b=conv_b, ln_g=ln_g, ln_b=ln_b, w_out_a=w_out_a,
                 kv_norm_g=kv_norm_g, w_kv=w_kv, k_norm_g=k_norm_g, w_in_b=w_in_b, q_norm_g=q_norm_g, w_out_b=w_out_b,
                 ple_norm_g=ple_norm_g, w_ple_gate=w_ple_gate, w_ple_proj=w_ple_proj)
    mom1 = dict(norm_g=m_norm_g, w_in_a=m_w_in_a, conv_w=m_conv_w, conv_b=m_conv_b, ln_g=m_ln_g, ln_b=m_ln_b,
                w_out_a=m_w_out_a, kv_norm_g=m_kv_norm_g, w_kv=m_w_kv, k_norm_g=m_k_norm_g, w_in_b=m_w_in_b,
                q_norm_g=m_q_norm_g, w_out_b=m_w_out_b, ple_norm_g=m_ple_norm_g, w_ple_gate=m_w_ple_gate,
                w_ple_proj=m_w_ple_proj)
    mom2 = dict(norm_g=v_norm_g, w_in_a=v_w_in_a, conv_w=v_conv_w, conv_b=v_conv_b, ln_g=v_ln_g, ln_b=v_ln_b,
                w_out_a=v_w_out_a, kv_norm_g=v_kv_norm_g, w_kv=v_w_kv, k_norm_g=v_k_norm_g, w_in_b=v_w_in_b,
                q_norm_g=v_q_norm_g, w_out_b=v_w_out_b, ple_norm_g=v_ple_norm_g, w_ple_gate=v_w_ple_gate,
                w_ple_proj=v_w_ple_proj)
    order = ["norm_g", "w_in_a", "conv_w", "conv_b", "ln_g", "ln_b", "w_out_a", "kv_norm_g", "w_kv", "k_norm_g", "w_in_b",
             "q_norm_g", "w_out_b", "ple_norm_g", "w_ple_gate", "w_ple_proj"]
    grads, deltas, new_m, new_v = {}, {}, {}, {}
    for nm in order:
        shape = given[nm].shape
        if nm in ("w_ple_gate", "w_ple_proj"):
            g2 = jnp.concatenate([gfull[nm + "0"], gfull[nm + "1"]], axis=0)
        elif nm in gfull:
            g2 = gfull[nm]
        else:
            g2 = small_grads[nm]
        two_d = g2.shape
        d2, m2, v2 = _adamw("adamw_" + nm, given[nm].reshape(two_d), g2, mom1[nm].reshape(two_d), mom2[nm].reshape(two_d))
        grads[nm], deltas[nm], new_m[nm], new_v[nm] = (a.reshape(shape) for a in (g2, d2, m2, v2))

    return (loss, grad_x, *[grads[n_] for n_ in order], *[deltas[n_] for n_ in order],
            *[new_m[n_] for n_ in order], *[new_v[n_] for n_ in order])
```

```python
import functools

import jax
import jax.numpy as jnp
from jax import lax
from jax.experimental import pallas as pl
from jax.experimental.pallas import tpu as pltpu

F32 = jnp.float32
BF16 = jnp.bfloat16
MESH = pl.DeviceIdType.MESH

EPS = 1e-6
NEG_INF = -1e30
HEAD_DIM = 64
ROPE_DIM = 16
ROPE_THETA = 500000.0
CONV_WIDTH = 31
SUBLANES = 8
CONV_ROWS = 64
HALO = 32
SPAN = 128
DILATIONS = (1, 4, 16)
ADAM_LR, ADAM_B1, ADAM_B2, ADAM_EPS, ADAM_WD, ADAM_STEP = 0.001, 0.9, 0.999, 1e-08, 0.01, 10
N_CHIPS = 4
VMEM_LIMIT = 56 * 1024 * 1024


def _tile(n, target, mult=128):
    best = None
    t = mult
    while t <= min(n, target):
        if n % t == 0:
            best = t
        t += mult
    return best if best is not None else n


def _params(*sem):
    return pltpu.CompilerParams(dimension_semantics=tuple(sem) if sem else None, vmem_limit_bytes=VMEM_LIMIT)


def _sigmoid(x):
    return 0.5 * jnp.tanh(0.5 * x) + 0.5


def _seg_allsum64(x):
    tr, w = x.shape
    cw = 256 if w % 256 == 0 else 128
    n = w // cw
    ri = lax.shift_right_logical(lax.broadcasted_iota(jnp.int32, (cw, cw), 0), 6)
    ci = lax.shift_right_logical(lax.broadcasted_iota(jnp.int32, (cw, cw), 1), 6)
    ones = (ri == ci).astype(BF16)
    hi = x.astype(BF16)
    lo = (x - hi.astype(F32)).astype(BF16)

    def stack(v):
        return jnp.concatenate([v[:, j * cw:(j + 1) * cw] for j in range(n)], axis=0)

    s = (jnp.dot(stack(hi), ones, preferred_element_type=F32)
         + jnp.dot(stack(lo), ones, preferred_element_type=F32))
    return jnp.concatenate([s[j * tr:(j + 1) * tr] for j in range(n)], axis=1)


def _colsum(x):
    return jnp.sum(x, axis=0, keepdims=True)


def _shards_view(w, kind):
    return w if kind == "col" else w.reshape(1, -1, w.shape[2])


def _mm_nn(name, a, w, kind, *, out_dtype=F32, resid=None, norm_gain=None, heads=None, plan=None):
    t = a.shape[0]
    w = _shards_view(w, kind)
    ns, k, c = w.shape
    n = ns * c
    tm = _tile(t, 1024 if norm_gain is None else 512, 8)
    tk = _tile(k, 2048)
    tn = _tile(c, 1024)
    nk = k // tk
    per = c // tn
    assert norm_gain is None or tn == n, "the fused RMSNorm needs whole rows in one tile"
    assert norm_gain is None or heads is None
    n_in = 2 + (resid is not None) + (norm_gain is not None) + (4 if heads is not None else 0)
    if heads is not None:
        h_gain, h_tables, h_width, h_seq = heads
        assert h_width % tn == 0 and h_seq % tm == 0
        h_blocks = h_width // tn

    def body(*refs):
        a_ref, w_ref = refs[:2]
        r_ref = refs[2] if resid is not None else None
        g_ref = refs[n_in - 1] if norm_gain is not None else None
        o_ref = refs[n_in]
        part = jnp.dot(a_ref[...].astype(BF16), w_ref[...], preferred_element_type=F32)

        def finish(out):
            if resid is not None:
                out = out + r_ref[...]
            stored = out.astype(out_dtype)
            o_ref[...] = stored
            if norm_gain is not None:
                normed = out * lax.rsqrt(jnp.mean(out * out, axis=-1, keepdims=True) + EPS) * g_ref[...]
                refs[n_in + 1][...] = normed.astype(BF16)
            if heads is not None:
                hg_ref, tc_ref, ta_ref, tb_ref = refs[n_in - 4:n_in]

                @pl.when(pl.program_id(1) < h_blocks)
                def _():
                    refs[n_in + 1][...] = _hnr_fwd_math(stored.astype(F32), hg_ref[...], tc_ref, ta_ref, tb_ref)

        if nk == 1:
            finish(part)
            return
        acc = refs[-1]
        kk = pl.program_id(2)

        @pl.when(kk == 0)
        def _():
            acc[...] = part

        @pl.when(kk > 0)
        def _():
            acc[...] += part

        @pl.when(kk == nk - 1)
        def _():
            finish(acc[...])

    in_specs = [pl.BlockSpec((tm, tk), lambda i, j, kk: (i, kk)),
                pl.BlockSpec((None, tk, tn), lambda i, j, kk: (j // per, kk, j % per))]
    args = [a, w]
    if resid is not None:
        in_specs.append(pl.BlockSpec((tm, tn), lambda i, j, kk: (i, j)))
        args.append(resid)
    out_specs = [pl.BlockSpec((tm, tn), lambda i, j, kk: (i, j))]
    out_shape = [jax.ShapeDtypeStruct((t, n), out_dtype)]
    if norm_gain is not None:
        in_specs.append(pl.BlockSpec((1, n), lambda i, j, kk: (0, 0)))
        args.append(norm_gain)
        out_specs.append(pl.BlockSpec((tm, tn), lambda i, j, kk: (i, j)))
        out_shape.append(jax.ShapeDtypeStruct((t, n), BF16))
    if heads is not None:
        last = h_blocks - 1
        per_seq = h_seq // tm
        tab = pl.BlockSpec((tm, 128), lambda i, j, kk: (i % per_seq, 0))
        in_specs += [pl.BlockSpec((1, tn), lambda i, j, kk: (0, jnp.minimum(j, last))), tab, tab, tab]
        args += [h_gain, *h_tables]
        out_specs.append(pl.BlockSpec((tm, tn), lambda i, j, kk: (i, jnp.minimum(j, last))))
        out_shape.append(jax.ShapeDtypeStruct((t, h_width), F32))
    outs, carried = _hosted_call(
        body, plan, name=name, grid=(t // tm, n // tn, nk), in_specs=in_specs, out_specs=out_specs, out_shape=out_shape,
        scratch_shapes=[pltpu.VMEM((tm, tn), F32)] if nk > 1 else [],
        args=args, sem=("parallel", "arbitrary", "arbitrary"))
    out = outs[0] if len(outs) == 1 else tuple(outs)
    return out if plan is None else (out, carried)


def _mm_nt(name, d, w, kind, *, out_dtype=F32, plan=None):
    t = d.shape[0]
    w = _shards_view(w, kind)
    ns, k, c = w.shape
    n = ns * c
    tm = _tile(t, 1024, 8)
    to = _tile(k, 1024)
    tc = _tile(c, 1536)
    nc = n // tc
    per = c // tc

    def body(d_ref, w_ref, o_ref, *scratch):
        part = lax.dot_general(d_ref[...].astype(BF16), w_ref[...], (((1,), (1,)), ((), ())),
                               preferred_element_type=F32)
        if nc == 1:
            o_ref[...] = part.astype(out_dtype)
            return
        acc = scratch[0]
        kk = pl.program_id(2)

        @pl.when(kk == 0)
        def _():
            acc[...] = part

        @pl.when(kk > 0)
        def _():
            acc[...] += part

        @pl.when(kk == nc - 1)
        def _():
            o_ref[...] = acc[...].astype(out_dtype)

    (out,), carried = _hosted_call(
        body, plan, name=name, grid=(t // tm, k // to, nc),
        in_specs=[pl.BlockSpec((tm, tc), lambda i, j, kk: (i, kk)),
                  pl.BlockSpec((None, to, tc), lambda i, j, kk: (kk // per, j, kk % per))],
        out_specs=[pl.BlockSpec((tm, to), lambda i, j, kk: (i, j))],
        out_shape=[jax.ShapeDtypeStruct((t, k), out_dtype)],
        scratch_shapes=[pltpu.VMEM((tm, to), F32)] if nc > 1 else [],
        args=[d, w], sem=("parallel", "parallel", "arbitrary"))
    return out if plan is None else (out, carried)


def _mm_tn(name, a, d, kind, shard_shape, whole=False):
    t, k = a.shape
    n = d.shape[1]
    ns = N_CHIPS if kind == "col" and not whole else 1
    c = n // ns
    tkm = _tile(k, 1024)
    tn = _tile(c, 1536)
    tt = _tile(t, 1024, 8)
    nt = t // tt
    per = c // tn

    def body(a_ref, d_ref, o_ref, acc):
        kk = pl.program_id(2)
        part = lax.dot_general(a_ref[...].astype(BF16), d_ref[...].astype(BF16), (((0,), (0,)), ((), ())),
                               preferred_element_type=F32)

        @pl.when(kk == 0)
        def _():
            acc[...] = part

        @pl.when(kk > 0)
        def _():
            acc[...] += part

        @pl.when(kk == nt - 1)
        def _():
            o_ref[...] = acc[...].astype(BF16)

    out = pl.pallas_call(
        body, name=name, grid=(k // tkm, n // tn, nt),
        in_specs=[pl.BlockSpec((tt, tkm), lambda i, j, kk: (kk, i)),
                  pl.BlockSpec((tt, tn), lambda i, j, kk: (kk, j))],
        out_specs=pl.BlockSpec((None, tkm, tn), lambda i, j, kk: (j // per, i, j % per)),
        out_shape=jax.ShapeDtypeStruct((ns, k, c), BF16),
        scratch_shapes=[pltpu.VMEM((tkm, tn), F32)],
        compiler_params=_params("parallel", "parallel", "arbitrary"))(a, d)
    if kind == "col" and whole:
        return out.reshape(k, N_CHIPS, n // N_CHIPS).transpose(1, 0, 2)
    return out.reshape((N_CHIPS,) + tuple(shard_shape))


def _row_spec(tr, w, col=0):
    return pl.BlockSpec((tr, w), lambda i: (i, col))


def _full_spec(shape):
    return pl.BlockSpec(shape, lambda i: tuple(0 for _ in shape))


def _rms_fwd(name, x, gains, plan=None):
    t, dm = x.shape
    tr = _tile(t, 256, 8)
    n = len(gains)

    def body(x_ref, *refs):
        xv = x_ref[...]
        xn = xv * lax.rsqrt(jnp.mean(xv * xv, axis=-1, keepdims=True) + EPS)
        for g_ref, o_ref in zip(refs[:n], refs[n:]):
            o_ref[...] = (xn * g_ref[...]).astype(BF16)

    outs, carried = _hosted_call(
        body, plan, name=name, grid=(t // tr,),
        in_specs=[_row_spec(tr, dm)] + [_full_spec((1, dm))] * n,
        out_specs=[_row_spec(tr, dm)] * n,
        out_shape=[jax.ShapeDtypeStruct((t, dm), BF16)] * n,
        scratch_shapes=[], args=[x, *gains], sem=("parallel",))
    return outs if plan is None else (outs, carried)


def _ple_bwd_math(dy, gpre, pp):
    sg = _sigmoid(gpre)
    return (dy * pp * sg * (1.0 - sg)).astype(BF16), (dy * sg).astype(BF16)


def _rms_bwd(name, x, resid, pairs, ple=None):
    t, dm = x.shape
    tr = _tile(t, 256, 8)
    n = len(pairs)
    n_in = 2 * n + (2 if ple is not None else 0)

    def body(x_ref, r_ref, *refs):
        ins, outs = refs[:n_in], refs[n_in:]
        i = pl.program_id(0)
        xv = x_ref[...]
        rs = lax.rsqrt(jnp.mean(xv * xv, axis=-1, keepdims=True) + EPS)
        xn = xv * rs
        total = r_ref[...]
        for kx in range(n):
            g_ref, du_ref = ins[2 * kx], ins[2 * kx + 1]
            dg_ref = outs[1 + kx]
            du = du_ref[...].astype(F32)

            @pl.when(i == 0)
            def _():
                dg_ref[...] = jnp.zeros_like(dg_ref)

            dg_ref[...] += _colsum(du * xn)
            dxh = du * g_ref[...]
            total = total + rs * (dxh - xn * jnp.mean(dxh * xn, axis=-1, keepdims=True))
        outs[0][...] = total
        if ple is not None:
            outs[1 + n][...], outs[2 + n][...] = _ple_bwd_math(total, ins[2 * n][...].astype(F32), ins[2 * n + 1][...].astype(F32))

    in_specs = [_row_spec(tr, dm), _row_spec(tr, dm)]
    args = [x, resid]
    for g, du in pairs:
        in_specs += [_full_spec((1, dm)), _row_spec(tr, dm)]
        args += [g, du]
    out_specs = [_row_spec(tr, dm)] + [_full_spec((1, dm))] * n
    out_shape = [jax.ShapeDtypeStruct((t, dm), F32)] + [jax.ShapeDtypeStruct((1, dm), F32)] * n
    if ple is not None:
        in_specs += [_row_spec(tr, dm)] * 2
        args += list(ple)
        out_specs += [_row_spec(tr, dm)] * 2
        out_shape += [jax.ShapeDtypeStruct((t, dm), BF16)] * 2
    outs = pl.pallas_call(
        body, name=name, grid=(t // tr,), in_specs=in_specs, out_specs=out_specs, out_shape=out_shape,
        compiler_params=_params("arbitrary"))(*args)
    if ple is not None:
        return outs[0], list(outs[1:1 + n]), outs[1 + n], outs[2 + n]
    return outs[0], list(outs[1:])


def _shifted_copies(ext, sh, rows):
    for s in range(1, SUBLANES):
        sh[s - 1] = ext[pl.ds(s, rows), :]


def _window(ext, sh, off, row0, rows, lanes):
    s = off % SUBLANES
    src = ext if s == 0 else sh.at[s - 1]
    return src[pl.ds(off - s + row0, rows), lanes]


def _mixa_fwd(name, pa, cw, cb, lg, lb, seq):
    t, w3 = pa.shape
    cc = w3 // 3
    tr = _tile(seq, 128, HALO)
    per_seq = seq // tr
    hb = tr // HALO
    lead = HALO - (CONV_WIDTH - 1)

    def body(a_ref, b_ref, z_ref, ah_ref, bh_ref, cw_ref, cb_ref, lg_ref, lb_ref, c_ref, m_ref, ext, sh, conv):
        i = pl.program_id(0)
        gh = ah_ref[...].astype(F32) * _sigmoid(bh_ref[...].astype(F32))
        ext[pl.ds(0, HALO), :] = jnp.where((i % per_seq) == 0, 0.0, gh)
        ext[pl.ds(HALO, tr), :] = a_ref[...].astype(F32) * _sigmoid(b_ref[...].astype(F32))
        _shifted_copies(ext, sh, tr + HALO - SUBLANES)
        for lc in range(cc // 128):
            lanes = pl.ds(lc * 128, 128)
            taps = [cw_ref[pl.ds(k, 1), lanes] for k in range(CONV_WIDTH)]
            for row0 in range(0, tr, CONV_ROWS):
                acc = jnp.broadcast_to(cb_ref[:, lanes], (CONV_ROWS, 128))
                for k in range(CONV_WIDTH):
                    acc = acc + _window(ext, sh, lead + k, row0, CONV_ROWS, lanes) * taps[k]
                conv[pl.ds(row0, CONV_ROWS), lanes] = acc
        acc = conv[...]
        c_ref[...] = acc.astype(BF16)
        xc = acc - jnp.mean(acc, axis=-1, keepdims=True)
        nrm = xc * lax.rsqrt(jnp.mean(xc * xc, axis=-1, keepdims=True) + EPS)
        l = nrm * lg_ref[...] + lb_ref[...]
        z = z_ref[...].astype(F32)
        m_ref[...] = (l * _sigmoid(l) * z * _sigmoid(z)).astype(BF16)

    halo = lambda col: pl.BlockSpec((HALO, cc), lambda i: (jnp.maximum(i * hb - 1, 0), col))
    return pl.pallas_call(
        body, name=name, grid=(t // tr,),
        in_specs=[_row_spec(tr, cc, 0), _row_spec(tr, cc, 1), _row_spec(tr, cc, 2), halo(0), halo(1),
                  _full_spec((HALO, cc)), _full_spec((1, cc)), _full_spec((1, cc)), _full_spec((1, cc))],
        out_specs=[_row_spec(tr, cc), _row_spec(tr, cc)],
        out_shape=[jax.ShapeDtypeStruct((t, cc), BF16), jax.ShapeDtypeStruct((t, cc), BF16)],
        scratch_shapes=[pltpu.VMEM((tr + HALO, cc), F32), pltpu.VMEM((SUBLANES - 1, tr + HALO - SUBLANES, cc), F32),
                        pltpu.VMEM((tr, cc), F32)],
        compiler_params=_params("parallel"))(pa, pa, pa, pa, pa, cw, cb, lg, lb)


def _mixa_bwd1(name, c, pa, dm, lg, lb, plan=None):
    t, cc = c.shape
    tr = _tile(t, 128, 8)

    def body(c_ref, z_ref, dm_ref, lg_ref, lb_ref, dc_ref, dz_ref, dlg_ref, dlb_ref, dcb_ref):
        i = pl.program_id(0)
        cv = c_ref[...].astype(F32)
        xc = cv - jnp.mean(cv, axis=-1, keepdims=True)
        rs = lax.rsqrt(jnp.mean(xc * xc, axis=-1, keepdims=True) + EPS)
        nrm = xc * rs
        l = nrm * lg_ref[...] + lb_ref[...]
        z = z_ref[...].astype(F32)
        sl, sz = _sigmoid(l), _sigmoid(z)
        dmv = dm_ref[...].astype(F32)
        ds = dmv * (z * sz)
        dzz = dmv * (l * sl)
        dz_ref[...] = (dzz * (sz * (1.0 + z * (1.0 - sz)))).astype(BF16)
        dl = ds * (sl * (1.0 + l * (1.0 - sl)))
        dn = dl * lg_ref[...]
        dc = rs * (dn - jnp.mean(dn, axis=-1, keepdims=True) - nrm * jnp.mean(dn * nrm, axis=-1, keepdims=True))
        dc_ref[...] = dc.astype(BF16)

        @pl.when(i == 0)
        def _():
            dlg_ref[...] = jnp.zeros_like(dlg_ref)
            dlb_ref[...] = jnp.zeros_like(dlb_ref)
            dcb_ref[...] = jnp.zeros_like(dcb_ref)

        dlg_ref[...] += _colsum(dl * nrm)
        dlb_ref[...] += _colsum(dl)
        dcb_ref[...] += _colsum(dc)

    vec = jax.ShapeDtypeStruct((1, cc), F32)
    outs, carried = _hosted_call(
        body, plan, name=name, grid=(t // tr,),
        in_specs=[_row_spec(tr, cc), _row_spec(tr, cc, 2), _row_spec(tr, cc), _full_spec((1, cc)), _full_spec((1, cc))],
        out_specs=[_row_spec(tr, cc), _row_spec(tr, cc)] + [_full_spec((1, cc))] * 3,
        out_shape=[jax.ShapeDtypeStruct((t, cc), BF16), jax.ShapeDtypeStruct((t, cc), BF16), vec, vec, vec],
        scratch_shapes=[], args=[c, pa, dm, lg, lb], sem=("arbitrary",))
    return outs if plan is None else (outs, carried)


def _mixa_bwd2(name, dc, pa, dz, cw, seq):
    t, cc = dc.shape
    tr = _tile(seq, 128, HALO)
    per_seq = seq // tr
    hb = tr // HALO
    steps = t // tr
    last_halo = t // HALO - 1

    def body(dc_ref, dcn_ref, a_ref, b_ref, dz_ref, cw_ref, dp_ref, dcw_ref, ext, sh, sums):
        i = pl.program_id(0)
        ext[pl.ds(0, tr), :] = dc_ref[...].astype(F32)
        ext[pl.ds(tr, HALO), :] = jnp.where((i % per_seq) == per_seq - 1, 0.0, dcn_ref[...].astype(F32))
        _shifted_copies(ext, sh, tr + HALO - SUBLANES)

        @pl.when(i == 0)
        def _():
            sums[...] = jnp.zeros_like(sums)
            dcw_ref[...] = jnp.zeros_like(dcw_ref)

        av = a_ref[...].astype(F32)
        sb = _sigmoid(b_ref[...].astype(F32))
        glu = av * sb
        dglu = jnp.zeros((tr, cc), F32)
        for k in range(CONV_WIDTH):
            wd = _window(ext, sh, CONV_WIDTH - 1 - k, 0, tr, slice(None))
            dglu = dglu + wd * cw_ref[pl.ds(k, 1), :]
            sums[pl.ds(k * SUBLANES, SUBLANES), :] += (wd * glu).reshape(tr // SUBLANES, SUBLANES, cc).sum(axis=0)
        dp_ref[:, pl.ds(0, cc)] = (dglu * sb).astype(BF16)
        dp_ref[:, pl.ds(cc, cc)] = (dglu * av * sb * (1.0 - sb)).astype(BF16)
        dp_ref[:, pl.ds(2 * cc, cc)] = dz_ref[...]

        @pl.when(i == steps - 1)
        def _():
            for k in range(CONV_WIDTH):
                dcw_ref[pl.ds(k, 1), :] = _colsum(sums[pl.ds(k * SUBLANES, SUBLANES), :])

    nxt = pl.BlockSpec((HALO, cc), lambda i: (jnp.minimum((i + 1) * hb, last_halo), 0))
    return pl.pallas_call(
        body, name=name, grid=(steps,),
        in_specs=[_row_spec(tr, cc), nxt, _row_spec(tr, cc, 0), _row_spec(tr, cc, 1), _row_spec(tr, cc),
                  _full_spec((HALO, cc))],
        out_specs=[_row_spec(tr, 3 * cc), _full_spec((HALO, cc))],
        out_shape=[jax.ShapeDtypeStruct((t, 3 * cc), BF16), jax.ShapeDtypeStruct((HALO, cc), F32)],
        scratch_shapes=[pltpu.VMEM((tr + HALO, cc), F32), pltpu.VMEM((SUBLANES - 1, tr + HALO - SUBLANES, cc), F32),
                        pltpu.VMEM((HALO * SUBLANES, cc), F32)],
        compiler_params=_params("arbitrary"))(dc, dc, pa, pa, dz, cw)


def _ple_fwd(name, h, gpre, pp, gains):
    t, dm = h.shape
    tr = _tile(t, 256, 8)

    n = len(gains)

    def body(h_ref, g_ref, p_ref, *refs):
        o_ref = refs[n]
        xv = h_ref[...] + _sigmoid(g_ref[...].astype(F32)) * p_ref[...].astype(F32)
        o_ref[...] = xv
        xn = xv * lax.rsqrt(jnp.mean(xv * xv, axis=-1, keepdims=True) + EPS)
        for gain_ref, n_ref in zip(refs[:n], refs[n + 1:]):
            n_ref[...] = (xn * gain_ref[...]).astype(BF16)

    return pl.pallas_call(
        body, name=name, grid=(t // tr,), in_specs=[_row_spec(tr, dm)] * 3 + [_full_spec((1, dm))] * n,
        out_specs=[_row_spec(tr, dm)] * (1 + n),
        out_shape=[jax.ShapeDtypeStruct((t, dm), F32)] + [jax.ShapeDtypeStruct((t, dm), BF16)] * n,
        compiler_params=_params("parallel"))(h, gpre, pp, *gains)


def _ple_loss(name, h, gpre, pp, target):
    t, dm = h.shape
    tr = _tile(t, 256, 8)

    def body(h_ref, g_ref, p_ref, t_ref, dy_ref, dg_ref, dp_ref, sq_ref):
        i = pl.program_id(0)
        gpre_v, pp_v = g_ref[...].astype(F32), p_ref[...].astype(F32)
        err = h_ref[...] + _sigmoid(gpre_v) * pp_v - t_ref[...]
        dy = err * (1.0 / dm)
        dy_ref[...] = dy
        dg_ref[...], dp_ref[...] = _ple_bwd_math(dy, gpre_v, pp_v)

        @pl.when(i == 0)
        def _():
            sq_ref[...] = jnp.zeros_like(sq_ref)

        sq_ref[...] += jnp.sum(jnp.sum(err * err, axis=1, keepdims=True), axis=0, keepdims=True)

    return pl.pallas_call(
        body, name=name, grid=(t // tr,), in_specs=[_row_spec(tr, dm)] * 4,
        out_specs=[_row_spec(tr, dm)] * 3 + [_full_spec((1, 1))],
        out_shape=[jax.ShapeDtypeStruct((t, dm), F32)] + [jax.ShapeDtypeStruct((t, dm), BF16)] * 2
        + [jax.ShapeDtypeStruct((1, 1), F32)],
        compiler_params=_params("arbitrary"))(h, gpre, pp, target)


def _rope_tables(seq):
    half = ROPE_DIM // 2
    inv = ROPE_THETA ** (-jnp.arange(half, dtype=F32) * (2.0 / ROPE_DIM))
    ang = jnp.arange(seq).astype(F32)[:, None] * inv[None, :]
    cos, sin = jnp.cos(ang), jnp.sin(ang)
    rest = HEAD_DIM - ROPE_DIM
    one = jnp.ones((seq, rest), F32)
    zero = jnp.zeros((seq, rest), F32)
    zh = jnp.zeros((seq, half), F32)
    tc = jnp.concatenate([cos, cos, one], axis=1)
    ta = jnp.concatenate([-sin, zh, zero], axis=1)
    tb = jnp.concatenate([zh, sin, zero], axis=1)
    return [jnp.tile(tb_, (1, 128 // HEAD_DIM)) for tb_ in (tc, ta, tb)]


def _wide(tab_ref, w):
    return jnp.tile(tab_ref[...], (1, w // 128))


def _hnr_fwd_math(xv, gain, tc_ref, ta_ref, tb_ref):
    width = xv.shape[1]
    rs = lax.rsqrt(_seg_allsum64(xv * xv) * (1.0 / HEAD_DIM) + EPS)
    y = xv * rs * gain
    return (y * _wide(tc_ref, width) + pltpu.roll(y, width - ROPE_DIM // 2, 1) * _wide(ta_ref, width)
            + pltpu.roll(y, ROPE_DIM // 2, 1) * _wide(tb_ref, width))


def _hnr_bwd_math(xv, gain, dout, tc, ta, tb, width):
    dy = dout * tc + pltpu.roll(dout * ta, ROPE_DIM // 2, 1) + pltpu.roll(dout * tb, width - ROPE_DIM // 2, 1)
    rs = lax.rsqrt(_seg_allsum64(xv * xv) * (1.0 / HEAD_DIM) + EPS)
    xn = xv * rs
    dyh = dy * gain
    dx = rs * (dyh - xn * (_seg_allsum64(dyh * xn) * (1.0 / HEAD_DIM)))
    return dx, _colsum(dy * xn)


def _q_bwd(name, p1, gain, tables, dqs, dgt, seq):
    t, w4 = p1.shape
    da = w4 // 4
    width = 3 * da
    tr = _tile(seq, 128, 8)
    per_seq = seq // tr

    def body(x_ref, g_ref, tc_ref, ta_ref, tb_ref, d0_ref, d1_ref, d2_ref, dgt_ref, o_ref, dg_ref):
        i = pl.program_id(0)
        dout = jnp.concatenate([d0_ref[...], d1_ref[...], d2_ref[...]], axis=1)
        dx, dg = _hnr_bwd_math(x_ref[...].astype(F32), g_ref[...], dout, _wide(tc_ref, width), _wide(ta_ref, width),
                               _wide(tb_ref, width), width)

        @pl.when(i == 0)
        def _():
            dg_ref[...] = jnp.zeros_like(dg_ref)

        dg_ref[...] += dg
        o_ref[:, pl.ds(0, width)] = dx.astype(BF16)
        o_ref[:, pl.ds(width, da)] = dgt_ref[...]

    tab = pl.BlockSpec((tr, 128), lambda i: (i % per_seq, 0))
    return pl.pallas_call(
        body, name=name, grid=(t // tr,),
        in_specs=[_row_spec(tr, width), _full_spec((1, width)), tab, tab, tab] + [_row_spec(tr, da)] * 4,
        out_specs=[_row_spec(tr, w4), _full_spec((1, width))],
        out_shape=[jax.ShapeDtypeStruct((t, w4), BF16), jax.ShapeDtypeStruct((1, width), F32)],
        compiler_params=_params("arbitrary"))(p1, gain, *tables, *dqs, dgt)


def _k_bwd(name, kv, gain, tables, dk, dv, seq):
    t, w2 = kv.shape
    da = w2 // 2
    tr = _tile(seq, 256, 8)
    per_seq = seq // tr

    def body(x_ref, g_ref, tc_ref, ta_ref, tb_ref, dk_ref, dv_ref, o_ref, dg_ref):
        i = pl.program_id(0)
        dx, dg = _hnr_bwd_math(x_ref[...], g_ref[...], dk_ref[...], _wide(tc_ref, da), _wide(ta_ref, da),
                               _wide(tb_ref, da), da)

        @pl.when(i == 0)
        def _():
            dg_ref[...] = jnp.zeros_like(dg_ref)

        dg_ref[...] += dg
        o_ref[:, pl.ds(0, da)] = dx.astype(BF16)
        o_ref[:, pl.ds(da, da)] = dv_ref[...].astype(BF16)

    tab = pl.BlockSpec((tr, 128), lambda i: (i % per_seq, 0))
    return pl.pallas_call(
        body, name=name, grid=(t // tr,),
        in_specs=[_row_spec(tr, da), _full_spec((1, da)), tab, tab, tab] + [_row_spec(tr, da)] * 2,
        out_specs=[_row_spec(tr, w2), _full_spec((1, da))],
        out_shape=[jax.ShapeDtypeStruct((t, w2), BF16), jax.ShapeDtypeStruct((1, da), F32)],
        compiler_params=_params("arbitrary"))(kv, gain, *tables, dk, dv)


def _unit_index(dil, r, blk):
    if dil == 1:
        start = blk * SPAN
        return pl.ds(start if isinstance(start, int) else pl.multiple_of(start, SPAN), SPAN)
    return pl.ds(r + dil * SPAN * blk, SPAN, stride=dil)


def _unit_rows(ref, dil, r, blk):
    return ref[_unit_index(dil, r, blk), :]


def _store_rows(ref, dil, r, blk, val):
    ref[_unit_index(dil, r, blk), :] = val


def _over_units(dil, nblk, unit, carry0, after=None):
    for r in range(dil):
        carry = carry0
        for blk in range(nblk):
            carry = unit(r, blk, blk > 0, carry)
        if after is not None:
            after(r, carry)


QROWS = SPAN


def _half_keys(prev, cur, h, with_prev):
    if with_prev:
        return jnp.concatenate([prev[h * QROWS:], cur[:(h + 1) * QROWS]], axis=0)
    return cur[:(h + 1) * QROWS]


def _half_mask(h, with_prev):
    nk = SPAN + QROWS if with_prev else (h + 1) * QROWS
    qi = lax.broadcasted_iota(jnp.int32, (QROWS, nk), 0)
    kj = lax.broadcasted_iota(jnp.int32, (QROWS, nk), 1)
    if with_prev:
        return (kj >= qi) & (kj <= qi + SPAN)
    return kj <= qi + h * QROWS


_NT = (((1,), (1,)), ((), ()))
_TN = (((0,), (0,)), ((), ()))
_HEAD_LANES = [slice(hh * HEAD_DIM, (hh + 1) * HEAD_DIM) for hh in range(128 // HEAD_DIM)]


def _group_fwd(q_ref, k_ref, v_ref, o_ref, l_ref, dil, seq):
    nblk = seq // (dil * SPAN)
    scale = HEAD_DIM ** -0.5

    def unit(r, blk, with_prev, carry):
        kp, vp = carry if with_prev else (None, None)
        q = _unit_rows(q_ref, dil, r, blk) * scale
        kc = _unit_rows(k_ref, dil, r, blk)
        vc = _unit_rows(v_ref, dil, r, blk)
        o_rows, l_rows = [], []
        for h in range(SPAN // QROWS):
            qh = q[h * QROWS:(h + 1) * QROWS]
            kk = _half_keys(kp, kc, h, with_prev)
            vv = _half_keys(vp, vc, h, with_prev)
            mask = _half_mask(h, with_prev)
            outs, lses = [], []
            for sl in _HEAD_LANES:
                s = lax.dot_general(qh[:, sl].astype(BF16), kk[:, sl].astype(BF16), _NT, preferred_element_type=F32)
                s = jnp.where(mask, s, NEG_INF)
                mx = jnp.max(s, axis=-1, keepdims=True)
                p = jnp.exp(s - mx)
                den = jnp.sum(p, axis=-1, keepdims=True)
                o = jnp.dot(p.astype(BF16), vv[:, sl].astype(BF16), preferred_element_type=F32) / den
                outs.append(o)
                lses.append(jnp.broadcast_to(mx + jnp.log(den), (QROWS, HEAD_DIM)))
            o_rows.append(jnp.concatenate(outs, axis=1))
            l_rows.append(jnp.concatenate(lses, axis=1))
        _store_rows(o_ref, dil, r, blk, jnp.concatenate(o_rows, axis=0))
        _store_rows(l_ref, dil, r, blk, jnp.concatenate(l_rows, axis=0))
        return kc, vc

    _over_units(dil, nblk, unit, None)


def _attn_fwd(name, qn, kn, kv, p1, seq):
    t, da = kn.shape
    hp = da // 128
    ng = len(DILATIONS)

    def body(q0_ref, q1_ref, q2_ref, k_ref, v_ref, g_ref, o_ref, l_ref, m_ref, og, lg):
        for g, q_ref in enumerate((q0_ref, q1_ref, q2_ref)):
            _group_fwd(q_ref, k_ref, v_ref, og.at[g], lg.at[g], DILATIONS[g], seq)
        a0, a1, a2 = lg[0], lg[1], lg[2]
        mx = jnp.maximum(jnp.maximum(a0, a1), a2)
        e0, e1, e2 = jnp.exp(a0 - mx), jnp.exp(a1 - mx), jnp.exp(a2 - mx)
        den = e0 + e1 + e2
        o = (e0 * og[0] + e1 * og[1] + e2 * og[2]) / den
        o_ref[...] = o
        l_ref[...] = mx + jnp.log(den)
        gt = g_ref[...].astype(F32)
        m_ref[...] = (o * gt * _sigmoid(gt)).astype(BF16)

    blk_spec = lambda off: pl.BlockSpec((seq, 128), lambda b, h: (b, off + h))
    return pl.pallas_call(
        body, name=name, grid=(t // seq, hp),
        in_specs=[blk_spec(0), blk_spec(hp), blk_spec(2 * hp), blk_spec(0), blk_spec(hp), blk_spec(3 * hp)],
        out_specs=[blk_spec(0)] * 3,
        out_shape=[jax.ShapeDtypeStruct((t, da), F32)] * 2 + [jax.ShapeDtypeStruct((t, da), BF16)],
        scratch_shapes=[pltpu.VMEM((ng, seq, 128), F32), pltpu.VMEM((ng, seq, 128), F32)],
        compiler_params=_params("parallel", "parallel"))(qn, qn, qn, kn, kv, p1)


def _group_bwd(q_ref, k_ref, v_ref, do_ref, l_ref, d_ref, dq_ref, dk_ref, dv_ref, dil, seq, first):
    nblk = seq // (dil * SPAN)
    scale = HEAD_DIM ** -0.5

    def put(ref, r, blk, val):
        if not first:
            val = val + _unit_rows(ref, dil, r, blk)
        _store_rows(ref, dil, r, blk, val)

    def unit(r, blk, with_prev, carry):
        kp, vp, pend_k, pend_v = carry if with_prev else (None,) * 4
        q = _unit_rows(q_ref, dil, r, blk) * scale
        kc = _unit_rows(k_ref, dil, r, blk)
        vc = _unit_rows(v_ref, dil, r, blk)
        dov = _unit_rows(do_ref, dil, r, blk)
        lrow = _unit_rows(l_ref, dil, r, blk)
        drow = _unit_rows(d_ref, dil, r, blk)
        dq_rows, dk_halves, dv_halves = [], [], []
        for h in range(SPAN // QROWS):
            rows = slice(h * QROWS, (h + 1) * QROWS)
            kk = _half_keys(kp, kc, h, with_prev)
            vv = _half_keys(vp, vc, h, with_prev)
            mask = _half_mask(h, with_prev)
            dqs, dkcs, dvcs = [], [], []
            for hh, sl in enumerate(_HEAD_LANES):
                stat = slice(hh * HEAD_DIM, hh * HEAD_DIM + 1)
                qh = q[rows, sl].astype(BF16)
                doh = dov[rows, sl].astype(BF16)
                kh = kk[:, sl].astype(BF16)
                s = lax.dot_general(qh, kh, _NT, preferred_element_type=F32)
                p = jnp.where(mask, jnp.exp(s - lrow[rows, stat]), 0.0)
                dp = lax.dot_general(doh, vv[:, sl].astype(BF16), _NT, preferred_element_type=F32)
                ds = (p * (dp - drow[rows, stat])).astype(BF16)
                dqs.append(jnp.dot(ds, kh, preferred_element_type=F32) * scale)
                dkcs.append(lax.dot_general(ds, qh, _TN, preferred_element_type=F32))
                dvcs.append(lax.dot_general(p.astype(BF16), doh, _TN, preferred_element_type=F32))
            dq_rows.append(jnp.concatenate(dqs, axis=1))
            dk_halves.append(jnp.concatenate(dkcs, axis=1))
            dv_halves.append(jnp.concatenate(dvcs, axis=1))
        _store_rows(dq_ref, dil, r, blk, jnp.concatenate(dq_rows, axis=0))
        pad = jnp.zeros((QROWS, 128), F32)

        def split(halves):
            if len(halves) == 1:
                return (halves[0][:SPAN], halves[0][SPAN:]) if with_prev else (None, halves[0])
            lower, upper = halves
            if with_prev:
                prev = lower[:SPAN] + jnp.concatenate([pad, upper[:QROWS]], axis=0)
                cur = jnp.concatenate([lower[SPAN:], pad], axis=0) + upper[QROWS:]
                return prev, cur
            return None, jnp.concatenate([lower, pad], axis=0) + upper

        dk_prev, dk_cur = split(dk_halves)
        dv_prev, dv_cur = split(dv_halves)
        if with_prev:
            put(dk_ref, r, blk - 1, pend_k + dk_prev)
            put(dv_ref, r, blk - 1, pend_v + dv_prev)
        return kc, vc, dk_cur, dv_cur

    def after(r, carry):
        put(dk_ref, r, nblk - 1, carry[2])
        put(dv_ref, r, nblk - 1, carry[3])

    _over_units(dil, nblk, unit, None, after)


def _attn_bwd(name, qn, kn, kv, do, lse, dsum, seq):
    t, da = kn.shape
    hp = da // 128

    def body(q0_ref, q1_ref, q2_ref, k_ref, v_ref, do_ref, l_ref, d_ref, dq0_ref, dq1_ref, dq2_ref, dk_ref, dv_ref):
        groups = ((q0_ref, dq0_ref), (q1_ref, dq1_ref), (q2_ref, dq2_ref))
        for g, (q_ref, dq_ref) in enumerate(groups):
            _group_bwd(q_ref, k_ref, v_ref, do_ref, l_ref, d_ref, dq_ref, dk_ref, dv_ref, DILATIONS[g], seq, g == 0)

    blk_spec = lambda off: pl.BlockSpec((seq, 128), lambda b, h: (b, off + h))
    return pl.pallas_call(
        body, name=name, grid=(t // seq, hp),
        in_specs=[blk_spec(0), blk_spec(hp), blk_spec(2 * hp), blk_spec(0), blk_spec(hp), blk_spec(0), blk_spec(0), blk_spec(0)],
        out_specs=[blk_spec(0)] * 5,
        out_shape=[jax.ShapeDtypeStruct((t, da), F32)] * 5,
        compiler_params=_params("parallel", "parallel"))(qn, qn, qn, kn, kv, do, lse, dsum)


def _gate_bwd(name, dm, o, p1):
    t, da = o.shape
    tr = _tile(t, 256, 8)

    def body(dm_ref, o_ref, g_ref, do_ref, dg_ref, ds_ref):
        g = g_ref[...].astype(F32)
        sg = _sigmoid(g)
        dmv, ov = dm_ref[...].astype(F32), o_ref[...]
        do = dmv * (g * sg)
        do_ref[...] = do
        dg_ref[...] = (dmv * ov * (sg * (1.0 + g * (1.0 - sg)))).astype(BF16)
        ds_ref[...] = _seg_allsum64(do * ov)

    return pl.pallas_call(
        body, name=name, grid=(t // tr,),
        in_specs=[_row_spec(tr, da), _row_spec(tr, da), _row_spec(tr, da, 3)],
        out_specs=[_row_spec(tr, da)] * 3,
        out_shape=[jax.ShapeDtypeStruct((t, da), F32), jax.ShapeDtypeStruct((t, da), BF16), jax.ShapeDtypeStruct((t, da), F32)],
        compiler_params=_params("parallel"))(dm, o, p1)


def _cast_bf16(name, w2d, chip):
    r, c = w2d.shape
    tr = _tile(r, 256, 16)

    def body(chip_ref, x_ref, o_ref):
        o_ref[...] = x_ref[...].astype(BF16)

    grid_spec = pltpu.PrefetchScalarGridSpec(
        num_scalar_prefetch=1, grid=(r // tr,),
        in_specs=[pl.BlockSpec((tr, c), lambda i, m: (i, 0))],
        out_specs=pl.BlockSpec((None, tr, c), lambda i, m: (m[0], i, 0)))
    return pl.pallas_call(
        body, name=name, grid_spec=grid_spec, out_shape=jax.ShapeDtypeStruct((N_CHIPS, r, c), BF16),
        compiler_params=_params("parallel"))(chip, w2d)


def _adamw(name, w, g, m, v):
    r, c = w.shape
    tr = _tile(r, 256, 8)
    c1 = 1.0 - ADAM_B1 ** ADAM_STEP
    c2 = 1.0 - ADAM_B2 ** ADAM_STEP

    def body(w_ref, g_ref, m_ref, v_ref, d_ref, nm_ref, nv_ref):
        gv = g_ref[...]
        nm = ADAM_B1 * m_ref[...] + (1.0 - ADAM_B1) * gv
        nv = ADAM_B2 * v_ref[...] + (1.0 - ADAM_B2) * (gv * gv)
        nm_ref[...] = nm
        nv_ref[...] = nv
        d_ref[...] = -ADAM_LR * ((nm / c1) / (jnp.sqrt(nv / c2) + ADAM_EPS) + ADAM_WD * w_ref[...])

    sds = jax.ShapeDtypeStruct((r, c), F32)
    return pl.pallas_call(
        body, name=name, grid=(r // tr,), in_specs=[_row_spec(tr, c)] * 4, out_specs=[_row_spec(tr, c)] * 3,
        out_shape=[sds] * 3, compiler_params=_params("parallel"))(w, g, m, v)


def _pair_sum(name, gd, recv, core):
    _, r, c = gd.shape
    rh = r // 2
    tr = _tile(rh, 256, 8)
    nrt = rh // tr

    def body(c_ref, a_ref, b_ref, o_ref):
        o_ref[...] = (a_ref[...].astype(F32) + b_ref[...].astype(F32)).astype(BF16)

    grid_spec = pltpu.PrefetchScalarGridSpec(
        num_scalar_prefetch=1, grid=(N_CHIPS, nrt),
        in_specs=[pl.BlockSpec((None, tr, c), lambda j, i, cr: (j, cr[0] * nrt + i, 0)),
                  pl.BlockSpec((None, tr, c), lambda j, i, cr: (j, i, 0))],
        out_specs=pl.BlockSpec((None, tr, c), lambda j, i, cr: (j, i, 0)))
    return pl.pallas_call(
        body, name=name, grid_spec=grid_spec, out_shape=jax.ShapeDtypeStruct((N_CHIPS, rh, c), BF16),
        compiler_params=_params("parallel", "parallel"))(core, gd, recv)


def _chip_sum(name, sums, parts, order):
    _, rh, c = parts.shape
    tr = _tile(rh, 256, 16)
    nrt = rh // tr

    def body(o_ref_, s_ref, p1_ref, p2_ref, p3_ref, o_ref):
        acc = s_ref[...].astype(F32)
        for p_ref in (p1_ref, p2_ref, p3_ref):
            acc = acc + p_ref[...].astype(F32)
        o_ref[...] = acc

    slot = lambda k: pl.BlockSpec((None, tr, c), lambda i, o: (o[k], i, 0))
    grid_spec = pltpu.PrefetchScalarGridSpec(
        num_scalar_prefetch=1, grid=(nrt,),
        in_specs=[slot(0), slot(1), slot(2), slot(3)],
        out_specs=pl.BlockSpec((tr, c), lambda i, o: (o[N_CHIPS] * nrt + i, 0)))
    return pl.pallas_call(
        body, name=name, grid_spec=grid_spec, out_shape=jax.ShapeDtypeStruct((2 * rh, c), F32),
        compiler_params=_params("parallel"))(order, sums, parts, parts, parts)


HBM = pl.BlockSpec(memory_space=pl.ANY)


def _place():
    x, y, c = lax.axis_index("x"), lax.axis_index("y"), lax.axis_index("c")
    chips = [(1 - x, y), (x, 1 - y), (1 - x, 1 - y)]
    return x, y, c, chips


def _half(ref, hc):
    rows = ref.shape[0] // 2
    return ref.at[pl.ds(hc * rows, rows)]


class _Plan:
    def __init__(self, ins, out_shapes, aliases, n_sems, start, finish):
        self.ins, self.out_shapes, self.aliases, self.n_sems = list(ins), list(out_shapes), dict(aliases), n_sems
        self.start, self.finish = start, finish


def _gather_plan(shards):
    n = len(shards)

    def ici(place, outs, send_sems, recv_sems, i, k, slot):
        x, y, c, chips = place
        half = _half(outs[i].at[slot], c)
        return pltpu.make_async_remote_copy(
            src_ref=half, dst_ref=half, send_sem=send_sems.at[6 * i + k], recv_sem=recv_sems.at[6 * i + k],
            device_id=(chips[k][0], chips[k][1], c), device_id_type=MESH)

    def d2d(place, outs, send_sems, recv_sems, i, k, slot, hc):
        x, y, c, chips = place
        half = _half(outs[i].at[slot], hc)
        return pltpu.make_async_remote_copy(
            src_ref=half, dst_ref=half, send_sem=send_sems.at[6 * i + 3 + k], recv_sem=recv_sems.at[6 * i + 3 + k],
            device_id=(x, y, 1 - c), device_id_type=MESH)

    def start(place, ins, outs, send_sems, recv_sems):
        x, y, c, chips = place
        for i in range(n):
            for k in range(3):
                ici(place, outs, send_sems, recv_sems, i, k, 2 * x + y).start()

    def finish(place, ins, outs, send_sems, recv_sems):
        x, y, c, chips = place
        theirs = [2 * chip[0] + chip[1] for chip in chips]
        for i in range(n):
            for k in range(3):
                ici(place, outs, send_sems, recv_sems, i, k, theirs[k]).wait_recv()
                d2d(place, outs, send_sems, recv_sems, i, k, theirs[k], c).start()
        for i in range(n):
            for k in range(3):
                d2d(place, outs, send_sems, recv_sems, i, k, theirs[k], 1 - c).wait_recv()
        for i in range(n):
            for k in range(3):
                ici(place, outs, send_sems, recv_sems, i, k, 2 * x + y).wait_send()
                d2d(place, outs, send_sems, recv_sems, i, k, theirs[k], c).wait_send()

    return _Plan(shards, [jax.ShapeDtypeStruct(s.shape, s.dtype) for s in shards], {i: i for i in range(n)}, 6 * n,
                 start, finish)


def _scatter_plan(sums):
    n = len(sums)

    def copy(place, ins, outs, send_sems, recv_sems, i, k, src_slot, dst_slot):
        x, y, c, chips = place
        return pltpu.make_async_remote_copy(
            src_ref=ins[i].at[src_slot], dst_ref=outs[i].at[dst_slot],
            send_sem=send_sems.at[3 * i + k], recv_sem=recv_sems.at[3 * i + k],
            device_id=(chips[k][0], chips[k][1], c), device_id_type=MESH)

    def start(place, ins, outs, send_sems, recv_sems):
        x, y, c, chips = place
        for i in range(n):
            for k, chip in enumerate(chips):
                copy(place, ins, outs, send_sems, recv_sems, i, k, 2 * chip[0] + chip[1], 2 * x + y).start()

    def finish(place, ins, outs, send_sems, recv_sems):
        x, y, c, chips = place
        for i in range(n):
            for k, chip in enumerate(chips):
                theirs = 2 * chip[0] + chip[1]
                copy(place, ins, outs, send_sems, recv_sems, i, k, theirs, 2 * x + y).wait_send()
                copy(place, ins, outs, send_sems, recv_sems, i, k, theirs, theirs).wait_recv()

    return _Plan(sums, [jax.ShapeDtypeStruct(s.shape, s.dtype) for s in sums], {}, 3 * n, start, finish)


def _hosted_call(body, plan, *, name, grid, in_specs, out_specs, out_shape, scratch_shapes, args, sem):
    in_specs, out_specs, out_shape, scratch_shapes = list(in_specs), list(out_specs), list(out_shape), list(scratch_shapes)
    if plan is None:
        res = pl.pallas_call(body, name=name, grid=grid, in_specs=in_specs, out_specs=out_specs, out_shape=out_shape,
                             scratch_shapes=scratch_shapes, compiler_params=_params(*sem))(*args)
        return list(res), []
    n_in, n_out, n_scr = len(in_specs), len(out_specs), len(scratch_shapes)
    p_in, p_out = len(plan.ins), len(plan.out_shapes)

    def hosted(*refs):
        refs = list(refs)
        ins, pins = refs[:n_in], refs[n_in:n_in + p_in]
        outs = refs[n_in + p_in:n_in + p_in + n_out]
        pouts = refs[n_in + p_in + n_out:n_in + p_in + n_out + p_out]
        scr = refs[n_in + p_in + n_out + p_out:n_in + p_in + n_out + p_out + n_scr]
        send_sems, recv_sems = refs[-2:]
        place = _place()
        ids = [pl.program_id(d) for d in range(len(grid))]
        first = functools.reduce(jnp.logical_and, [i == 0 for i in ids])
        last = functools.reduce(jnp.logical_and, [i == g - 1 for i, g in zip(ids, grid)])

        @pl.when(first)
        def _():
            plan.start(place, pins, pouts, send_sems, recv_sems)

        body(*ins, *outs, *scr)

        @pl.when(last)
        def _():
            plan.finish(place, pins, pouts, send_sems, recv_sems)

    res = pl.pallas_call(
        hosted, name=name, grid=grid, in_specs=in_specs + [HBM] * p_in, out_specs=out_specs + [HBM] * p_out,
        out_shape=out_shape + plan.out_shapes,
        input_output_aliases={n_in + i: n_out + o for i, o in plan.aliases.items()},
        scratch_shapes=scratch_shapes + [pltpu.SemaphoreType.DMA((plan.n_sems,)), pltpu.SemaphoreType.DMA((plan.n_sems,))],
        compiler_params=_params(*(("arbitrary",) * len(grid))))(*args, *plan.ins)
    return list(res[:n_out]), list(res[n_out:])


def _run_plan(name, plan):
    p_in = len(plan.ins)

    def body(*refs):
        ins, outs = refs[:p_in], refs[p_in:p_in + len(plan.out_shapes)]
        send_sems, recv_sems = refs[-2:]
        place = _place()
        plan.start(place, ins, outs, send_sems, recv_sems)
        plan.finish(place, ins, outs, send_sems, recv_sems)

    return pl.pallas_call(
        body, name=name, in_specs=[HBM] * p_in, out_specs=[HBM] * len(plan.out_shapes), out_shape=plan.out_shapes,
        input_output_aliases=plan.aliases,
        scratch_shapes=[pltpu.SemaphoreType.DMA((plan.n_sems,)), pltpu.SemaphoreType.DMA((plan.n_sems,))],
        )(*plan.ins)


def _pair_exchange(name, grads):
    n = len(grads)

    def body(*refs):
        ins, outs = refs[:n], refs[n:2 * n]
        send_sems, recv_sems = refs[2 * n:]
        x, y, c, _ = _place()
        cps = []
        for i in range(n):
            rows = ins[i].shape[1] // 2
            cp = pltpu.make_async_remote_copy(
                src_ref=ins[i].at[:, pl.ds((1 - c) * rows, rows), :], dst_ref=outs[i],
                send_sem=send_sems.at[i], recv_sem=recv_sems.at[i], device_id=(x, y, 1 - c), device_id_type=MESH)
            cp.start()
            cps.append(cp)
        for cp in cps:
            cp.wait()

    return pl.pallas_call(
        body, name=name, in_specs=[HBM] * n, out_specs=[HBM] * n,
        out_shape=[jax.ShapeDtypeStruct((N_CHIPS, g.shape[1] // 2, g.shape[2]), g.dtype) for g in grads],
        scratch_shapes=[pltpu.SemaphoreType.DMA((n,)), pltpu.SemaphoreType.DMA((n,))],
        )(*grads)


def _sibling_join(grads):
    n = len(grads)

    def body(*refs):
        outs = refs[n:2 * n]
        send_sems, recv_sems = refs[2 * n:]
        x, y, c, _ = _place()
        cps = []
        for i in range(n):
            cp = pltpu.make_async_remote_copy(
                src_ref=_half(outs[i], c), dst_ref=_half(outs[i], c), send_sem=send_sems.at[i], recv_sem=recv_sems.at[i],
                device_id=(x, y, 1 - c), device_id_type=MESH)
            cp.start()
            cps.append(cp)
        for i, cp in enumerate(cps):
            cp.wait_send()
            pltpu.make_async_remote_copy(
                src_ref=_half(outs[i], 1 - c), dst_ref=_half(outs[i], 1 - c), send_sem=send_sems.at[i],
                recv_sem=recv_sems.at[i], device_id=(x, y, 1 - c), device_id_type=MESH).wait_recv()

    return pl.pallas_call(
        body, name="sibling_join", in_specs=[HBM] * n, out_specs=[HBM] * n,
        out_shape=[jax.ShapeDtypeStruct(g.shape, g.dtype) for g in grads],
        input_output_aliases={i: i for i in range(n)},
        scratch_shapes=[pltpu.SemaphoreType.DMA((n,)), pltpu.SemaphoreType.DMA((n,))],
        )(*grads)


def _gather8(name, block, reduce):
    m, n = block.shape

    def body(x_ref, out_ref, *scratch):
        if reduce:
            all_ref, send_sems, recv_sems, local_sem = scratch
        else:
            all_ref = out_ref
            send_sems, recv_sems, local_sem = scratch
        x, y, c, chips = _place()
        me, sibling = (x, y, c), (x, y, 1 - c)

        def rows(px, py, pc):
            return all_ref.at[pl.ds((4 * px + 2 * py + pc) * m, m), :]

        def copy(k, blk, to, src=None):
            return pltpu.make_async_remote_copy(
                src_ref=rows(*blk) if src is None else src, dst_ref=rows(*blk),
                send_sem=send_sems.at[k], recv_sem=recv_sems.at[k], device_id=to, device_id_type=MESH)

        mine = pltpu.make_async_copy(x_ref, rows(*me), local_sem)
        mine.start()
        first = [copy(0, me, sibling, src=x_ref)]
        first += [copy(1 + j, me, (chip[0], chip[1], c), src=x_ref) for j, chip in enumerate(chips)]
        for cp in first:
            cp.start()
        passed = [copy(4 + j, (chip[0], chip[1], c), sibling) for j, chip in enumerate(chips)]
        for j, chip in enumerate(chips):
            copy(1 + j, (chip[0], chip[1], c), me).wait_recv()
            passed[j].start()
        copy(0, sibling, me).wait_recv()
        for j, chip in enumerate(chips):
            copy(4 + j, (chip[0], chip[1], 1 - c), me).wait_recv()
        for cp in first + passed:
            cp.wait_send()
        mine.wait()
        if reduce:
            acc = all_ref[pl.ds(0, m), :]
            for d in range(1, 8):
                acc = acc + all_ref[pl.ds(d * m, m), :]
            out_ref[...] = acc

    sems = [pltpu.SemaphoreType.DMA((7,)), pltpu.SemaphoreType.DMA((7,)), pltpu.SemaphoreType.DMA]
    scratch = ([pltpu.VMEM((8 * m, n), F32)] if reduce else []) + sems
    return pl.pallas_call(
        body, name=name,
        out_shape=jax.ShapeDtypeStruct((m, n) if reduce else (8 * m, n), F32),
        in_specs=[pl.BlockSpec(memory_space=pltpu.VMEM)], out_specs=pl.BlockSpec(memory_space=pltpu.VMEM),
        scratch_shapes=scratch)(block)


def _pad_rows(a, rows):
    return jnp.concatenate([a, jnp.zeros((rows - a.shape[0], a.shape[1]), a.dtype)], axis=0)


def kernel(x, p, norm_g, w_in_a, conv_w, conv_b, ln_g, ln_b, w_out_a, kv_norm_g, w_kv, k_norm_g, w_in_b, q_norm_g, w_out_b, ple_norm_g, w_ple_gate, w_ple_proj, loss_target, m_norm_g, m_w_in_a, m_conv_w, m_conv_b, m_ln_g, m_ln_b, m_w_out_a, m_kv_norm_g, m_w_kv, m_k_norm_g, m_w_in_b, m_q_norm_g, m_w_out_b, m_ple_norm_g, m_w_ple_gate, m_w_ple_proj, v_norm_g, v_w_in_a, v_conv_w, v_conv_b, v_ln_g, v_ln_b, v_w_out_a, v_kv_norm_g, v_w_kv, v_k_norm_g, v_w_in_b, v_q_norm_g, v_w_out_b, v_ple_norm_g, v_w_ple_gate, v_w_ple_proj):
    nb, seq, dm = x.shape
    t = nb * seq
    ple = p.shape[-1]
    ccs = conv_w.shape[-1]
    cc = N_CHIPS * ccs
    da = dm
    nheads = da // HEAD_DIM
    assert seq == DILATIONS[-1] * SPAN and da % 128 == 0 and ccs % 128 == 0

    core = lax.axis_index("c").astype(jnp.int32).reshape(1)
    chip = (2 * lax.axis_index("x") + lax.axis_index("y")).astype(jnp.int32)
    chip1 = chip.reshape(1)
    sum_order = jnp.concatenate([(chip1 + k) % N_CHIPS for k in range(N_CHIPS)] + [core])

    x2 = x.reshape(t, dm)
    tgt2 = loss_target.reshape(t, dm)
    p0 = p[0].reshape(t, ple)
    p1 = p[1].reshape(t, ple)

    big = [
        ("w_in_a", w_in_a[0], "col"), ("w_out_a", w_out_a[0], "row"), ("w_kv", w_kv, "col"),
        ("w_in_b", w_in_b[0], "col"), ("w_out_b", w_out_b[0], "row"),
        ("w_ple_gate0", w_ple_gate[0], "row"), ("w_ple_gate1", w_ple_gate[1], "row"),
        ("w_ple_proj0", w_ple_proj[0], "col"), ("w_ple_proj1", w_ple_proj[1], "col"),
    ]
    shard_shape = {nm: w.shape for nm, w, _ in big}
    names = [nm for nm, _, _ in big]
    own = [_cast_bf16("cast_" + nm, w, chip1) for nm, w, _ in big]
    W = {}

    vec_rows = 40
    small = _pad_rows(jnp.concatenate([conv_w[0], conv_b, ln_g, ln_b], axis=0), vec_rows)
    allv = _gather8("gather_conv_vectors", small, reduce=False).reshape(N_CHIPS, 2, vec_rows, ccs)[:, 0]
    allv = allv.transpose(1, 0, 2).reshape(vec_rows, cc)
    cw_full, cb_full, lg_full, lb_full = allv[:HALO], allv[31:32], allv[32:33], allv[33:34]
    cw_full = cw_full * (lax.broadcasted_iota(jnp.int32, (HALO, 1), 0) < CONV_WIDTH).astype(F32)

    tables = _rope_tables(seq)
    gain_q = jnp.tile(q_norm_g[0][:, None, :], (1, nheads, 1)).reshape(1, 3 * da)
    gain_k = jnp.tile(k_norm_g[None, :], (1, nheads))
    g0, g1 = norm_g[0:1], norm_g[1:2]
    pg0, pg1 = ple_norm_g[0:1], ple_norm_g[1:2]
    kvg = kv_norm_g[None, :]

    (u0,), (W[names[0]],) = _rms_fwd("rms_u0", x2, [g0], plan=_gather_plan(own[:1]))
    pa, gathered = _mm_nn("mm_in_a", u0, W["w_in_a"], "col", out_dtype=BF16, plan=_gather_plan(own[1:]))
    W.update(zip(names[1:], gathered))
    for nm in ("w_ple_proj0", "w_ple_proj1"):
        W[nm] = W[nm].transpose(1, 0, 2).reshape(1, W[nm].shape[1], -1)
    conv_out, m_a = _mixa_fwd("mixa_fwd", pa, cw_full, cb_full, lg_full, lb_full, seq)
    h0, r0 = _mm_nn("mm_out_a", m_a, W["w_out_a"], "row", resid=x2, norm_gain=pg0)
    gpre0 = _mm_nn("mm_gate0", r0, W["w_ple_gate0"], "row", out_dtype=BF16)
    pp0 = _mm_nn("mm_proj0", p0, W["w_ple_proj0"], "col", out_dtype=BF16)
    x1, kvn, u1 = _ple_fwd("ple_fwd0", h0, gpre0, pp0, [kvg, g1])
    kv, kn = _mm_nn("mm_kv", kvn, W["w_kv"], "col", heads=(gain_k, tables, da, seq))
    pb, qn = _mm_nn("mm_in_b", u1, W["w_in_b"], "col", out_dtype=BF16, heads=(gain_q, tables, 3 * da, seq))
    o, lse, m_b = _attn_fwd("attn_fwd", qn, kn, kv, pb, seq)
    h1, r1 = _mm_nn("mm_out_b", m_b, W["w_out_b"], "row", resid=x1, norm_gain=pg1)
    gpre1 = _mm_nn("mm_gate1", r1, W["w_ple_gate1"], "row", out_dtype=BF16)
    pp1 = _mm_nn("mm_proj1", p1, W["w_ple_proj1"], "col", out_dtype=BF16)
    dy, dgp1, dpp1, sq = _ple_loss("ple_loss", h1, gpre1, pp1, tgt2)
    loss = lax.psum(0.5 * sq[0, 0] / dm, ("x", "y", "c"))

    G = {}
    G["w_ple_gate1"] = _mm_tn("tn_gate1", r1, dgp1, "row", shard_shape["w_ple_gate1"])
    G["w_ple_proj1"] = _mm_tn("tn_proj1", p1, dpp1, "col", shard_shape["w_ple_proj1"], whole=True)
    dr1 = _mm_nt("nt_gate1", dgp1, W["w_ple_gate1"], "row", out_dtype=BF16)
    dh1, (dpg1,) = _rms_bwd("rms_bwd_r1", h1, dy, [(pg1, dr1)])
    G["w_out_b"] = _mm_tn("tn_out_b", m_b, dh1, "row", shard_shape["w_out_b"])
    dm_b = _mm_nt("nt_out_b", dh1, W["w_out_b"], "row", out_dtype=BF16)
    d_o, dgt, dsum = _gate_bwd("gate_bwd", dm_b, o, pb)
    dq0, dq1, dq2, dk, dv = _attn_bwd("attn_bwd", qn, kn, kv, d_o, lse, dsum, seq)
    dpb, dgq = _q_bwd("q_bwd", pb, gain_q, tables, [dq0, dq1, dq2], dgt, seq)
    dkv, dgk = _k_bwd("k_bwd", kv, gain_k, tables, dk, dv, seq)
    G["w_in_b"] = _mm_tn("tn_in_b", u1, dpb, "col", shard_shape["w_in_b"])
    du1 = _mm_nt("nt_in_b", dpb, W["w_in_b"], "col", out_dtype=BF16)
    G["w_kv"] = _mm_tn("tn_kv", kvn, dkv, "col", shard_shape["w_kv"])
    dkvn = _mm_nt("nt_kv", dkv, W["w_kv"], "col", out_dtype=BF16)
    dx1, (dg1, dkvg), dgp0, dpp0 = _rms_bwd("rms_bwd_x1", x1, dh1, [(g1, du1), (kvg, dkvn)], ple=(gpre0, pp0))
    G["w_ple_gate0"] = _mm_tn("tn_gate0", r0, dgp0, "row", shard_shape["w_ple_gate0"])
    G["w_ple_proj0"] = _mm_tn("tn_proj0", p0, dpp0, "col", shard_shape["w_ple_proj0"], whole=True)
    dr0 = _mm_nt("nt_gate0", dgp0, W["w_ple_gate0"], "row", out_dtype=BF16)
    dh0, (dpg0,) = _rms_bwd("rms_bwd_r0", h0, dx1, [(pg0, dr0)])
    G["w_out_a"] = _mm_tn("tn_out_a", m_a, dh0, "row", shard_shape["w_out_a"])
    dm_a = _mm_nt("nt_out_a", dh0, W["w_out_a"], "row", out_dtype=BF16)

    def pair_sums(tag, batch):
        recv = _pair_exchange("pair_exchange_" + tag, [G[nm] for nm in batch])
        return [_pair_sum("pair_sum_" + nm, G[nm], rc, core) for nm, rc in zip(batch, recv)]

    late = ["w_kv", "w_in_b", "w_out_b", "w_ple_gate1", "w_ple_proj1"]
    early = ["w_in_a", "w_out_a", "w_ple_gate0", "w_ple_proj0"]
    sums_late = pair_sums("late", late)
    (dc, dz, dlg, dlb, dcb), parts_late = _mixa_bwd1("mixa_bwd1", conv_out, pa, dm_a, lg_full, lb_full,
                                                     plan=_scatter_plan(sums_late))
    dpa, dcw = _mixa_bwd2("mixa_bwd2", dc, pa, dz, cw_full, seq)
    G["w_in_a"] = _mm_tn("tn_in_a", u0, dpa, "col", shard_shape["w_in_a"])
    sums_early = pair_sums("early", early)
    du0, parts_early = _mm_nt("nt_in_a", dpa, W["w_in_a"], "col", out_dtype=BF16, plan=_scatter_plan(sums_early))
    dx, (dg0,) = _rms_bwd("rms_bwd_x", x2, dh0, [(g0, du0)])
    grad_x = dx.reshape(nb, seq, dm)

    sums = dict(zip(late + early, sums_late + sums_early))
    parts = dict(zip(late + early, parts_late + parts_early))
    halves = [_chip_sum("chip_sum_" + nm, sums[nm], parts[nm], sum_order) for nm in names]
    gfull = dict(zip(names, _sibling_join(halves)))

    def as_rows(a):
        return a.reshape(-1, dm)

    small_parts = [as_rows(dcw), as_rows(dcb), as_rows(dlg), as_rows(dlb), dg0, dg1, dkvg, dpg0, dpg1, as_rows(dgk), as_rows(dgq)]
    counts = [a.shape[0] for a in small_parts]
    total = sum(counts)
    packed = _pad_rows(jnp.concatenate(small_parts, axis=0), -(-total // 8) * 8)
    red = _gather8("reduce_small", packed, reduce=True)
    pieces, off = [], 0
    for n_ in counts:
        pieces.append(red[off:off + n_])
        off += n_
    r_dcw, r_dcb, r_dlg, r_dlb, r_g0, r_g1, r_kvg, r_pg0, r_pg1, r_gk, r_gq = pieces
    my_cols = lambda a: lax.dynamic_slice_in_dim(a.reshape(-1, cc), chip * ccs, ccs, axis=1)
    small_grads = {
        "norm_g": jnp.concatenate([r_g0, r_g1], axis=0),
        "conv_w": my_cols(r_dcw)[:CONV_WIDTH],
        "conv_b": my_cols(r_dcb), "ln_g": my_cols(r_dlg), "ln_b": my_cols(r_dlb),
        "kv_norm_g": r_kvg,
        "k_norm_g": r_gk.reshape(nheads, HEAD_DIM).sum(axis=0, keepdims=True),
        "q_norm_g": r_gq.reshape(3, nheads, HEAD_DIM).sum(axis=1),
        "ple_norm_g": jnp.concatenate([r_pg0, r_pg1], axis=0),
    }

    given = dict(norm_g=norm_g, w_in_a=w_in_a, conv_w=conv_w, conv_b=conv_b, ln_g=ln_g, ln_b=ln_b, w_out_a=w_out_a,
                 kv_norm_g=kv_norm_g, w_kv=w_kv, k_norm_g=k_norm_g, w_in_b=w_in_b, q_norm_g=q_norm_g, w_out_b=w_out_b,
                 ple_norm_g=ple_norm_g, w_ple_gate=w_ple_gate, w_ple_proj=w_ple_proj)
    mom1 = dict(norm_g=m_norm_g, w_in_a=m_w_in_a, conv_w=m_conv_w, conv_b=m_conv_b, ln_g=m_ln_g, ln_b=m_ln_b,
                w_out_a=m_w_out_a, kv_norm_g=m_kv_norm_g, w_kv=m_w_kv, k_norm_g=m_k_norm_g, w_in_b=m_w_in_b,
                q_norm_g=m_q_norm_g, w_out_b=m_w_out_b, ple_norm_g=m_ple_norm_g, w_ple_gate=m_w_ple_gate,
                w_ple_proj=m_w_ple_proj)
    mom2 = dict(norm_g=v_norm_g, w_in_a=v_w_in_a, conv_w=v_conv_w, conv_b=v_conv_b, ln_g=v_ln_g, ln_b=v_ln_b,
                w_out_a=v_w_out_a, kv_norm_g=v_kv_norm_g, w_kv=v_w_kv, k_norm_g=v_k_norm_g, w_in_b=v_w_in_b,
                q_norm_g=v_q_norm_g, w_out_b=v_w_out_b, ple_norm_g=v_ple_norm_g, w_ple_gate=v_w_ple_gate,
                w_ple_proj=v_w_ple_proj)
    order = ["norm_g", "w_in_a", "conv_w", "conv_b", "ln_g", "ln_b", "w_out_a", "kv_norm_g", "w_kv", "k_norm_g", "w_in_b",
             "q_norm_g", "w_out_b", "ple_norm_g", "w_ple_gate", "w_ple_proj"]
    grads, deltas, new_m, new_v = {}, {}, {}, {}
    for nm in order:
        shape = given[nm].shape
        if nm in ("w_ple_gate", "w_ple_proj"):
            g2 = jnp.concatenate([gfull[nm + "0"], gfull[nm + "1"]], axis=0)
        elif nm in gfull:
            g2 = gfull[nm]
        else:
            g2 = small_grads[nm]
        two_d = g2.shape
        d2, m2, v2 = _adamw("adamw_" + nm, given[nm].reshape(two_d), g2, mom1[nm].reshape(two_d), mom2[nm].reshape(two_d))
        grads[nm], deltas[nm], new_m[nm], new_v[nm] = (a.reshape(shape) for a in (g2, d2, m2, v2))

    return (loss, grad_x, *[grads[n_] for n_ in order], *[deltas[n_] for n_ in order],
            *[new_m[n_] for n_ in order], *[new_v[n_] for n_ in order])
```

```python
import functools

import jax
import jax.numpy as jnp
from jax import lax
from jax.experimental import pallas as pl
from jax.experimental.pallas import tpu as pltpu

F32 = jnp.float32
BF16 = jnp.bfloat16
MESH = pl.DeviceIdType.MESH

EPS = 1e-6
NEG_INF = -1e30
HEAD_DIM = 64
ROPE_DIM = 16
ROPE_THETA = 500000.0
CONV_WIDTH = 31
SUBLANES = 8
CONV_ROWS = 64
HALO = 32
SPAN = 128
DILATIONS = (1, 4, 16)
ADAM_LR, ADAM_B1, ADAM_B2, ADAM_EPS, ADAM_WD, ADAM_STEP = 0.001, 0.9, 0.999, 1e-08, 0.01, 10
N_CHIPS = 4
VMEM_LIMIT = 56 * 1024 * 1024


def _tile(n, target, mult=128):
    best = None
    t = mult
    while t <= min(n, target):
        if n % t == 0:
            best = t
        t += mult
    return best if best is not None else n


def _params(*sem):
    return pltpu.CompilerParams(dimension_semantics=tuple(sem) if sem else None, vmem_limit_bytes=VMEM_LIMIT)


def _sigmoid(x):
    return 0.5 * jnp.tanh(0.5 * x) + 0.5


def _seg_allsum64(x):
    tr, w = x.shape
    cw = 256 if w % 256 == 0 else 128
    n = w // cw
    ri = lax.shift_right_logical(lax.broadcasted_iota(jnp.int32, (cw, cw), 0), 6)
    ci = lax.shift_right_logical(lax.broadcasted_iota(jnp.int32, (cw, cw), 1), 6)
    ones = (ri == ci).astype(BF16)
    hi = x.astype(BF16)
    lo = (x - hi.astype(F32)).astype(BF16)

    def stack(v):
        return jnp.concatenate([v[:, j * cw:(j + 1) * cw] for j in range(n)], axis=0)

    s = (jnp.dot(stack(hi), ones, preferred_element_type=F32)
         + jnp.dot(stack(lo), ones, preferred_element_type=F32))
    return jnp.concatenate([s[j * tr:(j + 1) * tr] for j in range(n)], axis=1)


def _colsum(x):
    return jnp.sum(x, axis=0, keepdims=True)


def _shards_view(w, kind):
    return w if kind == "col" else w.reshape(1, -1, w.shape[2])


def _mm_nn(name, a, w, kind, *, out_dtype=F32, resid=None, norm_gain=None, heads=None, plan=None):
    t = a.shape[0]
    w = _shards_view(w, kind)
    ns, k, c = w.shape
    n = ns * c
    tm = _tile(t, 1024 if norm_gain is None else 512, 8)
    tk = _tile(k, 2048)
    tn = _tile(c, 1024)
    nk = k // tk
    per = c // tn
    assert norm_gain is None or tn == n, "the fused RMSNorm needs whole rows in one tile"
    assert norm_gain is None or heads is None
    n_in = 2 + (resid is not None) + (norm_gain is not None) + (4 if heads is not None else 0)
    if heads is not None:
        h_gain, h_tables, h_width, h_seq = heads
        assert h_width % tn == 0 and h_seq % tm == 0
        h_blocks = h_width // tn

    def body(*refs):
        a_ref, w_ref = refs[:2]
        r_ref = refs[2] if resid is not None else None
        g_ref = refs[n_in - 1] if norm_gain is not None else None
        o_ref = refs[n_in]
        part = jnp.dot(a_ref[...].astype(BF16), w_ref[...], preferred_element_type=F32)

        def finish(out):
            if resid is not None:
                out = out + r_ref[...]
            stored = out.astype(out_dtype)
            o_ref[...] = stored
            if norm_gain is not None:
                normed = out * lax.rsqrt(jnp.mean(out * out, axis=-1, keepdims=True) + EPS) * g_ref[...]
                refs[n_in + 1][...] = normed.astype(BF16)
            if heads is not None:
                hg_ref, tc_ref, ta_ref, tb_ref = refs[n_in - 4:n_in]

                @pl.when(pl.program_id(1) < h_blocks)
                def _():
                    refs[n_in + 1][...] = _hnr_fwd_math(stored.astype(F32), hg_ref[...], tc_ref, ta_ref, tb_ref)

        if nk == 1:
            finish(part)
            return
        acc = refs[-1]
        kk = pl.program_id(2)

        @pl.when(kk == 0)
        def _():
            acc[...] = part

        @pl.when(kk > 0)
        def _():
            acc[...] += part

        @pl.when(kk == nk - 1)
        def _():
            finish(acc[...])

    in_specs = [pl.BlockSpec((tm, tk), lambda i, j, kk: (i, kk)),
                pl.BlockSpec((None, tk, tn), lambda i, j, kk: (j // per, kk, j % per))]
    args = [a, w]
    if resid is not None:
        in_specs.append(pl.BlockSpec((tm, tn), lambda i, j, kk: (i, j)))
        args.append(resid)
    out_specs = [pl.BlockSpec((tm, tn), lambda i, j, kk: (i, j))]
    out_shape = [jax.ShapeDtypeStruct((t, n), out_dtype)]
    if norm_gain is not None:
        in_specs.append(pl.BlockSpec((1, n), lambda i, j, kk: (0, 0)))
        args.append(norm_gain)
        out_specs.append(pl.BlockSpec((tm, tn), lambda i, j, kk: (i, j)))
        out_shape.append(jax.ShapeDtypeStruct((t, n), BF16))
    if heads is not None:
        last = h_blocks - 1
        per_seq = h_seq // tm
        tab = pl.BlockSpec((tm, 128), lambda i, j, kk: (i % per_seq, 0))
        in_specs += [pl.BlockSpec((1, tn), lambda i, j, kk: (0, jnp.minimum(j, last))), tab, tab, tab]
        args += [h_gain, *h_tables]
        out_specs.append(pl.BlockSpec((tm, tn), lambda i, j, kk: (i, jnp.minimum(j, last))))
        out_shape.append(jax.ShapeDtypeStruct((t, h_width), F32))
    outs, carried = _hosted_call(
        body, plan, name=name, grid=(t // tm, n // tn, nk), in_specs=in_specs, out_specs=out_specs, out_shape=out_shape,
        scratch_shapes=[pltpu.VMEM((tm, tn), F32)] if nk > 1 else [],
        args=args, sem=("parallel", "arbitrary", "arbitrary"))
    out = outs[0] if len(outs) == 1 else tuple(outs)
    return out if plan is None else (out, carried)


def _mm_nt(name, d, w, kind, *, out_dtype=F32, plan=None):
    t = d.shape[0]
    w = _shards_view(w, kind)
    ns, k, c = w.shape
    n = ns * c
    tm = _tile(t, 1024, 8)
    to = _tile(k, 1024)
    tc = _tile(c, 1536)
    nc = n // tc
    per = c // tc

    def body(d_ref, w_ref, o_ref, *scratch):
        part = lax.dot_general(d_ref[...].astype(BF16), w_ref[...], (((1,), (1,)), ((), ())),
                               preferred_element_type=F32)
        if nc == 1:
            o_ref[...] = part.astype(out_dtype)
            return
        acc = scratch[0]
        kk = pl.program_id(2)

        @pl.when(kk == 0)
        def _():
            acc[...] = part

        @pl.when(kk > 0)
        def _():
            acc[...] += part

        @pl.when(kk == nc - 1)
        def _():
            o_ref[...] = acc[...].astype(out_dtype)

    (out,), carried = _hosted_call(
        body, plan, name=name, grid=(t // tm, k // to, nc),
        in_specs=[pl.BlockSpec((tm, tc), lambda i, j, kk: (i, kk)),
                  pl.BlockSpec((None, to, tc), lambda i, j, kk: (kk // per, j, kk % per))],
        out_specs=[pl.BlockSpec((tm, to), lambda i, j, kk: (i, j))],
        out_shape=[jax.ShapeDtypeStruct((t, k), out_dtype)],
        scratch_shapes=[pltpu.VMEM((tm, to), F32)] if nc > 1 else [],
        args=[d, w], sem=("parallel", "parallel", "arbitrary"))
    return out if plan is None else (out, carried)


def _mm_tn(name, a, d, kind, shard_shape, whole=False):
    t, k = a.shape
    n = d.shape[1]
    ns = N_CHIPS if kind == "col" and not whole else 1
    c = n // ns
    tkm = _tile(k, 1024)
    tn = _tile(c, 1536)
    tt = _tile(t, 1024, 8)
    nt = t // tt
    per = c // tn

    def body(a_ref, d_ref, o_ref, acc):
        kk = pl.program_id(2)
        part = lax.dot_general(a_ref[...].astype(BF16), d_ref[...].astype(BF16), (((0,), (0,)), ((), ())),
                               preferred_element_type=F32)

        @pl.when(kk == 0)
        def _():
            acc[...] = part

        @pl.when(kk > 0)
        def _():
            acc[...] += part

        @pl.when(kk == nt - 1)
        def _():
            o_ref[...] = acc[...].astype(BF16)

    out = pl.pallas_call(
        body, name=name, grid=(k // tkm, n // tn, nt),
        in_specs=[pl.BlockSpec((tt, tkm), lambda i, j, kk: (kk, i)),
                  pl.BlockSpec((tt, tn), lambda i, j, kk: (kk, j))],
        out_specs=pl.BlockSpec((None, tkm, tn), lambda i, j, kk: (j // per, i, j % per)),
        out_shape=jax.ShapeDtypeStruct((ns, k, c), BF16),
        scratch_shapes=[pltpu.VMEM((tkm, tn), F32)],
        compiler_params=_params("parallel", "parallel", "arbitrary"))(a, d)
    if kind == "col" and whole:
        return out.reshape(k, N_CHIPS, n // N_CHIPS).transpose(1, 0, 2)
    return out.reshape((N_CHIPS,) + tuple(shard_shape))


def _row_spec(tr, w, col=0):
    return pl.BlockSpec((tr, w), lambda i: (i, col))


def _full_spec(shape):
    return pl.BlockSpec(shape, lambda i: tuple(0 for _ in shape))


def _rms_fwd(name, x, gains, plan=None):
    t, dm = x.shape
    tr = _tile(t, 256, 8)
    n = len(gains)

    def body(x_ref, *refs):
        xv = x_ref[...]
        xn = xv * lax.rsqrt(jnp.mean(xv * xv, axis=-1, keepdims=True) + EPS)
        for g_ref, o_ref in zip(refs[:n], refs[n:]):
            o_ref[...] = (xn * g_ref[...]).astype(BF16)

    outs, carried = _hosted_call(
        body, plan, name=name, grid=(t // tr,),
        in_specs=[_row_spec(tr, dm)] + [_full_spec((1, dm))] * n,
        out_specs=[_row_spec(tr, dm)] * n,
        out_shape=[jax.ShapeDtypeStruct((t, dm), BF16)] * n,
        scratch_shapes=[], args=[x, *gains], sem=("parallel",))
    return outs if plan is None else (outs, carried)


def _ple_bwd_math(dy, gpre, pp):
    sg = _sigmoid(gpre)
    return (dy * pp * sg * (1.0 - sg)).astype(BF16), (dy * sg).astype(BF16)


def _rms_bwd(name, x, resid, pairs, ple=None):
    t, dm = x.shape
    tr = _tile(t, 256, 8)
    n = len(pairs)
    n_in = 2 * n + (2 if ple is not None else 0)

    def body(x_ref, r_ref, *refs):
        ins, outs = refs[:n_in], refs[n_in:]
        i = pl.program_id(0)
        xv = x_ref[...]
        rs = lax.rsqrt(jnp.mean(xv * xv, axis=-1, keepdims=True) + EPS)
        xn = xv * rs
        total = r_ref[...]
        for kx in range(n):
            g_ref, du_ref = ins[2 * kx], ins[2 * kx + 1]
            dg_ref = outs[1 + kx]
            du = du_ref[...].astype(F32)

            @pl.when(i == 0)
            def _():
                dg_ref[...] = jnp.zeros_like(dg_ref)

            dg_ref[...] += _colsum(du * xn)
            dxh = du * g_ref[...]
            total = total + rs * (dxh - xn * jnp.mean(dxh * xn, axis=-1, keepdims=True))
        outs[0][...] = total
        if ple is not None:
            outs[1 + n][...], outs[2 + n][...] = _ple_bwd_math(total, ins[2 * n][...].astype(F32), ins[2 * n + 1][...].astype(F32))

    in_specs = [_row_spec(tr, dm), _row_spec(tr, dm)]
    args = [x, resid]
    for g, du in pairs:
        in_specs += [_full_spec((1, dm)), _row_spec(tr, dm)]
        args += [g, du]
    out_specs = [_row_spec(tr, dm)] + [_full_spec((1, dm))] * n
    out_shape = [jax.ShapeDtypeStruct((t, dm), F32)] + [jax.ShapeDtypeStruct((1, dm), F32)] * n
    if ple is not None:
        in_specs += [_row_spec(tr, dm)] * 2
        args += list(ple)
        out_specs += [_row_spec(tr, dm)] * 2
        out_shape += [jax.ShapeDtypeStruct((t, dm), BF16)] * 2
    outs = pl.pallas_call(
        body, name=name, grid=(t // tr,), in_specs=in_specs, out_specs=out_specs, out_shape=out_shape,
        compiler_params=_params("arbitrary"))(*args)
    if ple is not None:
        return outs[0], list(outs[1:1 + n]), outs[1 + n], outs[2 + n]
    return outs[0], list(outs[1:])


def _shifted_copies(ext, sh, rows):
    for s in range(1, SUBLANES):
        sh[s - 1] = ext[pl.ds(s, rows), :]


def _window(ext, sh, off, row0, rows, lanes):
    s = off % SUBLANES
    src = ext if s == 0 else sh.at[s - 1]
    return src[pl.ds(off - s + row0, rows), lanes]


def _mixa_fwd(name, pa, cw, cb, lg, lb, seq):
    t, w3 = pa.shape
    cc = w3 // 3
    tr = _tile(seq, 128, HALO)
    per_seq = seq // tr
    hb = tr // HALO
    lead = HALO - (CONV_WIDTH - 1)

    def body(a_ref, b_ref, z_ref, ah_ref, bh_ref, cw_ref, cb_ref, lg_ref, lb_ref, c_ref, m_ref, ext, sh, conv):
        i = pl.program_id(0)
        gh = ah_ref[...].astype(F32) * _sigmoid(bh_ref[...].astype(F32))
        ext[pl.ds(0, HALO), :] = jnp.where((i % per_seq) == 0, 0.0, gh)
        ext[pl.ds(HALO, tr), :] = a_ref[...].astype(F32) * _sigmoid(b_ref[...].astype(F32))
        _shifted_copies(ext, sh, tr + HALO - SUBLANES)
        for lc in range(cc // 128):
            lanes = pl.ds(lc * 128, 128)
            taps = [cw_ref[pl.ds(k, 1), lanes] for k in range(CONV_WIDTH)]
            for row0 in range(0, tr, CONV_ROWS):
                acc = jnp.broadcast_to(cb_ref[:, lanes], (CONV_ROWS, 128))
                for k in range(CONV_WIDTH):
                    acc = acc + _window(ext, sh, lead + k, row0, CONV_ROWS, lanes) * taps[k]
                conv[pl.ds(row0, CONV_ROWS), lanes] = acc
        acc = conv[...]
        c_ref[...] = acc.astype(BF16)
        xc = acc - jnp.mean(acc, axis=-1, keepdims=True)
        nrm = xc * lax.rsqrt(jnp.mean(xc * xc, axis=-1, keepdims=True) + EPS)
        l = nrm * lg_ref[...] + lb_ref[...]
        z = z_ref[...].astype(F32)
        m_ref[...] = (l * _sigmoid(l) * z * _sigmoid(z)).astype(BF16)

    halo = lambda col: pl.BlockSpec((HALO, cc), lambda i: (jnp.maximum(i * hb - 1, 0), col))
    return pl.pallas_call(
        body, name=name, grid=(t // tr,),
        in_specs=[_row_spec(tr, cc, 0), _row_spec(tr, cc, 1), _row_spec(tr, cc, 2), halo(0), halo(1),
                  _full_spec((HALO, cc)), _full_spec((1, cc)), _full_spec((1, cc)), _full_spec((1, cc))],
        out_specs=[_row_spec(tr, cc), _row_spec(tr, cc)],
        out_shape=[jax.ShapeDtypeStruct((t, cc), BF16), jax.ShapeDtypeStruct((t, cc), BF16)],
        scratch_shapes=[pltpu.VMEM((tr + HALO, cc), F32), pltpu.VMEM((SUBLANES - 1, tr + HALO - SUBLANES, cc), F32),
                        pltpu.VMEM((tr, cc), F32)],
        compiler_params=_params("parallel"))(pa, pa, pa, pa, pa, cw, cb, lg, lb)


def _mixa_bwd1(name, c, pa, dm, lg, lb, plan=None):
    t, cc = c.shape
    tr = _tile(t, 128, 8)

    def body(c_ref, z_ref, dm_ref, lg_ref, lb_ref, dc_ref, dz_ref, dlg_ref, dlb_ref, dcb_ref):
        i = pl.program_id(0)
        cv = c_ref[...].astype(F32)
        xc = cv - jnp.mean(cv, axis=-1, keepdims=True)
        rs = lax.rsqrt(jnp.mean(xc * xc, axis=-1, keepdims=True) + EPS)
        nrm = xc * rs
        l = nrm * lg_ref[...] + lb_ref[...]
        z = z_ref[...].astype(F32)
        sl, sz = _sigmoid(l), _sigmoid(z)
        dmv = dm_ref[...].astype(F32)
        ds = dmv * (z * sz)
        dzz = dmv * (l * sl)
        dz_ref[...] = (dzz * (sz * (1.0 + z * (1.0 - sz)))).astype(BF16)
        dl = ds * (sl * (1.0 + l * (1.0 - sl)))
        dn = dl * lg_ref[...]
        dc = rs * (dn - jnp.mean(dn, axis=-1, keepdims=True) - nrm * jnp.mean(dn * nrm, axis=-1, keepdims=True))
        dc_ref[...] = dc.astype(BF16)

        @pl.when(i == 0)
        def _():
            dlg_ref[...] = jnp.zeros_like(dlg_ref)
            dlb_ref[...] = jnp.zeros_like(dlb_ref)
            dcb_ref[...] = jnp.zeros_like(dcb_ref)

        dlg_ref[...] += _colsum(dl * nrm)
        dlb_ref[...] += _colsum(dl)
        dcb_ref[...] += _colsum(dc)

    vec = jax.ShapeDtypeStruct((1, cc), F32)
    outs, carried = _hosted_call(
        body, plan, name=name, grid=(t // tr,),
        in_specs=[_row_spec(tr, cc), _row_spec(tr, cc, 2), _row_spec(tr, cc), _full_spec((1, cc)), _full_spec((1, cc))],
        out_specs=[_row_spec(tr, cc), _row_spec(tr, cc)] + [_full_spec((1, cc))] * 3,
        out_shape=[jax.ShapeDtypeStruct((t, cc), BF16), jax.ShapeDtypeStruct((t, cc), BF16), vec, vec, vec],
        scratch_shapes=[], args=[c, pa, dm, lg, lb], sem=("arbitrary",))
    return outs if plan is None else (outs, carried)


def _mixa_bwd2(name, dc, pa, dz, cw, seq):
    t, cc = dc.shape
    tr = _tile(seq, 128, HALO)
    per_seq = seq // tr
    hb = tr // HALO
    steps = t // tr
    last_halo = t // HALO - 1

    def body(dc_ref, dcn_ref, a_ref, b_ref, dz_ref, cw_ref, dp_ref, dcw_ref, ext, sh, sums):
        i = pl.program_id(0)
        ext[pl.ds(0, tr), :] = dc_ref[...].astype(F32)
        ext[pl.ds(tr, HALO), :] = jnp.where((i % per_seq) == per_seq - 1, 0.0, dcn_ref[...].astype(F32))
        _shifted_copies(ext, sh, tr + HALO - SUBLANES)

        @pl.when(i == 0)
        def _():
            sums[...] = jnp.zeros_like(sums)
            dcw_ref[...] = jnp.zeros_like(dcw_ref)

        av = a_ref[...].astype(F32)
        sb = _sigmoid(b_ref[...].astype(F32))
        glu = av * sb
        dglu = jnp.zeros((tr, cc), F32)
        for k in range(CONV_WIDTH):
            wd = _window(ext, sh, CONV_WIDTH - 1 - k, 0, tr, slice(None))
            dglu = dglu + wd * cw_ref[pl.ds(k, 1), :]
            sums[pl.ds(k * SUBLANES, SUBLANES), :] += (wd * glu).reshape(tr // SUBLANES, SUBLANES, cc).sum(axis=0)
        dp_ref[:, pl.ds(0, cc)] = (dglu * sb).astype(BF16)
        dp_ref[:, pl.ds(cc, cc)] = (dglu * av * sb * (1.0 - sb)).astype(BF16)
        dp_ref[:, pl.ds(2 * cc, cc)] = dz_ref[...]

        @pl.when(i == steps - 1)
        def _():
            for k in range(CONV_WIDTH):
                dcw_ref[pl.ds(k, 1), :] = _colsum(sums[pl.ds(k * SUBLANES, SUBLANES), :])

    nxt = pl.BlockSpec((HALO, cc), lambda i: (jnp.minimum((i + 1) * hb, last_halo), 0))
    return pl.pallas_call(
        body, name=name, grid=(steps,),
        in_specs=[_row_spec(tr, cc), nxt, _row_spec(tr, cc, 0), _row_spec(tr, cc, 1), _row_spec(tr, cc),
                  _full_spec((HALO, cc))],
        out_specs=[_row_spec(tr, 3 * cc), _full_spec((HALO, cc))],
        out_shape=[jax.ShapeDtypeStruct((t, 3 * cc), BF16), jax.ShapeDtypeStruct((HALO, cc), F32)],
        scratch_shapes=[pltpu.VMEM((tr + HALO, cc), F32), pltpu.VMEM((SUBLANES - 1, tr + HALO - SUBLANES, cc), F32),
                        pltpu.VMEM((HALO * SUBLANES, cc), F32)],
        compiler_params=_params("arbitrary"))(dc, dc, pa, pa, dz, cw)


def _ple_fwd(name, h, gpre, pp, gains):
    t, dm = h.shape
    tr = _tile(t, 256, 8)

    n = len(gains)

    def body(h_ref, g_ref, p_ref, *refs):
        o_ref = refs[n]
        xv = h_ref[...] + _sigmoid(g_ref[...].astype(F32)) * p_ref[...].astype(F32)
        o_ref[...] = xv
        xn = xv * lax.rsqrt(jnp.mean(xv * xv, axis=-1, keepdims=True) + EPS)
        for gain_ref, n_ref in zip(refs[:n], refs[n + 1:]):
            n_ref[...] = (xn * gain_ref[...]).astype(BF16)

    return pl.pallas_call(
        body, name=name, grid=(t // tr,), in_specs=[_row_spec(tr, dm)] * 3 + [_full_spec((1, dm))] * n,
        out_specs=[_row_spec(tr, dm)] * (1 + n),
        out_shape=[jax.ShapeDtypeStruct((t, dm), F32)] + [jax.ShapeDtypeStruct((t, dm), BF16)] * n,
        compiler_params=_params("parallel"))(h, gpre, pp, *gains)


def _ple_loss(name, h, gpre, pp, target):
    t, dm = h.shape
    tr = _tile(t, 256, 8)

    def body(h_ref, g_ref, p_ref, t_ref, dy_ref, dg_ref, dp_ref, sq_ref):
        i = pl.program_id(0)
        gpre_v, pp_v = g_ref[...].astype(F32), p_ref[...].astype(F32)
        err = h_ref[...] + _sigmoid(gpre_v) * pp_v - t_ref[...]
        dy = err * (1.0 / dm)
        dy_ref[...] = dy
        dg_ref[...], dp_ref[...] = _ple_bwd_math(dy, gpre_v, pp_v)

        @pl.when(i == 0)
        def _():
            sq_ref[...] = jnp.zeros_like(sq_ref)

        sq_ref[...] += jnp.sum(jnp.sum(err * err, axis=1, keepdims=True), axis=0, keepdims=True)

    return pl.pallas_call(
        body, name=name, grid=(t // tr,), in_specs=[_row_spec(tr, dm)] * 4,
        out_specs=[_row_spec(tr, dm)] * 3 + [_full_spec((1, 1))],
        out_shape=[jax.ShapeDtypeStruct((t, dm), F32)] + [jax.ShapeDtypeStruct((t, dm), BF16)] * 2
        + [jax.ShapeDtypeStruct((1, 1), F32)],
        compiler_params=_params("arbitrary"))(h, gpre, pp, target)


def _rope_tables(seq):
    half = ROPE_DIM // 2
    inv = ROPE_THETA ** (-jnp.arange(half, dtype=F32) * (2.0 / ROPE_DIM))
    ang = jnp.arange(seq).astype(F32)[:, None] * inv[None, :]
    cos, sin = jnp.cos(ang), jnp.sin(ang)
    rest = HEAD_DIM - ROPE_DIM
    one = jnp.ones((seq, rest), F32)
    zero = jnp.zeros((seq, rest), F32)
    zh = jnp.zeros((seq, half), F32)
    tc = jnp.concatenate([cos, cos, one], axis=1)
    ta = jnp.concatenate([-sin, zh, zero], axis=1)
    tb = jnp.concatenate([zh, sin, zero], axis=1)
    return [jnp.tile(tb_, (1, 128 // HEAD_DIM)) for tb_ in (tc, ta, tb)]


def _wide(tab_ref, w):
    return jnp.tile(tab_ref[...], (1, w // 128))


def _hnr_fwd_math(xv, gain, tc_ref, ta_ref, tb_ref):
    width = xv.shape[1]
    rs = lax.rsqrt(_seg_allsum64(xv * xv) * (1.0 / HEAD_DIM) + EPS)
    y = xv * rs * gain
    return (y * _wide(tc_ref, width) + pltpu.roll(y, width - ROPE_DIM // 2, 1) * _wide(ta_ref, width)
            + pltpu.roll(y, ROPE_DIM // 2, 1) * _wide(tb_ref, width))


def _hnr_bwd_math(xv, gain, dout, tc, ta, tb, width):
    dy = dout * tc + pltpu.roll(dout * ta, ROPE_DIM // 2, 1) + pltpu.roll(dout * tb, width - ROPE_DIM // 2, 1)
    rs = lax.rsqrt(_seg_allsum64(xv * xv) * (1.0 / HEAD_DIM) + EPS)
    xn = xv * rs
    dyh = dy * gain
    dx = rs * (dyh - xn * (_seg_allsum64(dyh * xn) * (1.0 / HEAD_DIM)))
    return dx, _colsum(dy * xn)


def _q_bwd(name, p1, gain, tables, dqs, dgt, seq):
    t, w4 = p1.shape
    da = w4 // 4
    width = 3 * da
    tr = _tile(seq, 128, 8)
    per_seq = seq // tr

    def body(x_ref, g_ref, tc_ref, ta_ref, tb_ref, d0_ref, d1_ref, d2_ref, dgt_ref, o_ref, dg_ref):
        i = pl.program_id(0)
        dout = jnp.concatenate([d0_ref[...], d1_ref[...], d2_ref[...]], axis=1)
        dx, dg = _hnr_bwd_math(x_ref[...].astype(F32), g_ref[...], dout, _wide(tc_ref, width), _wide(ta_ref, width),
                               _wide(tb_ref, width), width)

        @pl.when(i == 0)
        def _():
            dg_ref[...] = jnp.zeros_like(dg_ref)

        dg_ref[...] += dg
        o_ref[:, pl.ds(0, width)] = dx.astype(BF16)
        o_ref[:, pl.ds(width, da)] = dgt_ref[...]

    tab = pl.BlockSpec((tr, 128), lambda i: (i % per_seq, 0))
    return pl.pallas_call(
        body, name=name, grid=(t // tr,),
        in_specs=[_row_spec(tr, width), _full_spec((1, width)), tab, tab, tab] + [_row_spec(tr, da)] * 4,
        out_specs=[_row_spec(tr, w4), _full_spec((1, width))],
        out_shape=[jax.ShapeDtypeStruct((t, w4), BF16), jax.ShapeDtypeStruct((1, width), F32)],
        compiler_params=_params("arbitrary"))(p1, gain, *tables, *dqs, dgt)


def _k_bwd(name, kv, gain, tables, dk, dv, seq):
    t, w2 = kv.shape
    da = w2 // 2
    tr = _tile(seq, 256, 8)
    per_seq = seq // tr

    def body(x_ref, g_ref, tc_ref, ta_ref, tb_ref, dk_ref, dv_ref, o_ref, dg_ref):
        i = pl.program_id(0)
        dx, dg = _hnr_bwd_math(x_ref[...], g_ref[...], dk_ref[...], _wide(tc_ref, da), _wide(ta_ref, da),
                               _wide(tb_ref, da), da)

        @pl.when(i == 0)
        def _():
            dg_ref[...] = jnp.zeros_like(dg_ref)

        dg_ref[...] += dg
        o_ref[:, pl.ds(0, da)] = dx.astype(BF16)
        o_ref[:, pl.ds(da, da)] = dv_ref[...].astype(BF16)

    tab = pl.BlockSpec((tr, 128), lambda i: (i % per_seq, 0))
    return pl.pallas_call(
        body, name=name, grid=(t // tr,),
        in_specs=[_row_spec(tr, da), _full_spec((1, da)), tab, tab, tab] + [_row_spec(tr, da)] * 2,
        out_specs=[_row_spec(tr, w2), _full_spec((1, da))],
        out_shape=[jax.ShapeDtypeStruct((t, w2), BF16), jax.ShapeDtypeStruct((1, da), F32)],
        compiler_params=_params("arbitrary"))(kv, gain, *tables, dk, dv)


def _unit_index(dil, r, blk):
    if dil == 1:
        start = blk * SPAN
        return pl.ds(start if isinstance(start, int) else pl.multiple_of(start, SPAN), SPAN)
    return pl.ds(r + dil * SPAN * blk, SPAN, stride=dil)


def _unit_rows(ref, dil, r, blk):
    return ref[_unit_index(dil, r, blk), :]


def _store_rows(ref, dil, r, blk, val):
    ref[_unit_index(dil, r, blk), :] = val


def _over_units(dil, nblk, unit, carry0, after=None):
    for r in range(dil):
        carry = carry0
        for blk in range(nblk):
            carry = unit(r, blk, blk > 0, carry)
        if after is not None:
            after(r, carry)


def _band_mask(with_prev):
    nk = 2 * SPAN if with_prev else SPAN
    qi = lax.broadcasted_iota(jnp.int32, (SPAN, nk), 0)
    kj = lax.broadcasted_iota(jnp.int32, (SPAN, nk), 1)
    if with_prev:
        return (kj >= qi) & (kj <= qi + SPAN)
    return kj <= qi


_NT = (((1,), (1,)), ((), ()))
_TN = (((0,), (0,)), ((), ()))
N_PAIR = 128 // HEAD_DIM


def _per_head(x):
    head = lax.shift_right_logical(lax.broadcasted_iota(jnp.int32, x.shape, 1), HEAD_DIM.bit_length() - 1)
    return [jnp.where(head == hh, x, 0.0).astype(BF16) for hh in range(N_PAIR)]


def _head_columns(stat):
    return [stat[:, hh * HEAD_DIM:hh * HEAD_DIM + 1] for hh in range(N_PAIR)]


def _by_head(cols):
    head = lax.shift_right_logical(lax.broadcasted_iota(jnp.int32, (cols[0].shape[0], 128), 1), HEAD_DIM.bit_length() - 1)
    out = cols[-1]
    for hh in range(N_PAIR - 2, -1, -1):
        out = jnp.where(head == hh, cols[hh], out)
    return jnp.broadcast_to(out, (cols[0].shape[0], 128))


def _group_fwd(q_ref, k_ref, v_ref, o_ref, l_ref, dil, seq):
    nblk = seq // (dil * SPAN)
    scale = HEAD_DIM ** -0.5

    def unit(r, blk, with_prev, carry):
        kp, vp = carry if with_prev else (None, None)
        q = (_unit_rows(q_ref, dil, r, blk) * scale).astype(BF16)
        kc = _unit_rows(k_ref, dil, r, blk)
        vc = _unit_rows(v_ref, dil, r, blk)
        kcat = jnp.concatenate([kp, kc], axis=0) if with_prev else kc
        vcat = jnp.concatenate([vp, vc], axis=0) if with_prev else vc
        mask = _band_mask(with_prev)
        out = None
        lses, rdens = [], []
        for kh, vh in zip(_per_head(kcat), _per_head(vcat)):
            s = lax.dot_general(q, kh, _NT, preferred_element_type=F32)
            s = jnp.where(mask, s, NEG_INF)
            mx = jnp.max(s, axis=-1, keepdims=True)
            p = jnp.exp(s - mx)
            den = jnp.sum(p, axis=-1, keepdims=True)
            part = jnp.dot(p.astype(BF16), vh, preferred_element_type=F32)
            out = part if out is None else out + part
            lses.append(mx + jnp.log(den))
            rdens.append(1.0 / den)
        _store_rows(o_ref, dil, r, blk, out * _by_head(rdens))
        _store_rows(l_ref, dil, r, blk, _by_head(lses))
        return kc, vc

    _over_units(dil, nblk, unit, None)


def _attn_fwd(name, qn, kn, kv, p1, seq):
    t, da = kn.shape
    hp = da // 128
    ng = len(DILATIONS)

    def body(q0_ref, q1_ref, q2_ref, k_ref, v_ref, g_ref, o_ref, l_ref, m_ref, og, lg):
        for g, q_ref in enumerate((q0_ref, q1_ref, q2_ref)):
            _group_fwd(q_ref, k_ref, v_ref, og.at[g], lg.at[g], DILATIONS[g], seq)
        a0, a1, a2 = lg[0], lg[1], lg[2]
        mx = jnp.maximum(jnp.maximum(a0, a1), a2)
        e0, e1, e2 = jnp.exp(a0 - mx), jnp.exp(a1 - mx), jnp.exp(a2 - mx)
        den = e0 + e1 + e2
        o = (e0 * og[0] + e1 * og[1] + e2 * og[2]) / den
        o_ref[...] = o
        l_ref[...] = mx + jnp.log(den)
        gt = g_ref[...].astype(F32)
        m_ref[...] = (o * gt * _sigmoid(gt)).astype(BF16)

    blk_spec = lambda off: pl.BlockSpec((seq, 128), lambda b, h: (b, off + h))
    return pl.pallas_call(
        body, name=name, grid=(t // seq, hp),
        in_specs=[blk_spec(0), blk_spec(hp), blk_spec(2 * hp), blk_spec(0), blk_spec(hp), blk_spec(3 * hp)],
        out_specs=[blk_spec(0)] * 3,
        out_shape=[jax.ShapeDtypeStruct((t, da), F32)] * 2 + [jax.ShapeDtypeStruct((t, da), BF16)],
        scratch_shapes=[pltpu.VMEM((ng, seq, 128), F32), pltpu.VMEM((ng, seq, 128), F32)],
        compiler_params=_params("parallel", "parallel"))(qn, qn, qn, kn, kv, p1)


def _group_bwd(q_ref, k_ref, v_ref, do_ref, l_ref, d_ref, dq_ref, dk_ref, dv_ref, dil, seq, first):
    nblk = seq // (dil * SPAN)
    scale = HEAD_DIM ** -0.5

    def put(ref, r, blk, val):
        if not first:
            val = val + _unit_rows(ref, dil, r, blk)
        _store_rows(ref, dil, r, blk, val)

    def unit(r, blk, with_prev, carry):
        kp, vp, pend_k, pend_v = carry if with_prev else (None,) * 4
        q = _unit_rows(q_ref, dil, r, blk) * scale
        kc = _unit_rows(k_ref, dil, r, blk)
        vc = _unit_rows(v_ref, dil, r, blk)
        dov = _unit_rows(do_ref, dil, r, blk)
        lcols = _head_columns(_unit_rows(l_ref, dil, r, blk))
        dcols = _head_columns(_unit_rows(d_ref, dil, r, blk))
        kcat = jnp.concatenate([kp, kc], axis=0) if with_prev else kc
        vcat = jnp.concatenate([vp, vc], axis=0) if with_prev else vc
        mask = _band_mask(with_prev)
        qb, dob = q.astype(BF16), dov.astype(BF16)
        dq = dkcat = dvcat = None
        for hh, (kh, vh, qh, doh) in enumerate(zip(_per_head(kcat), _per_head(vcat), _per_head(q), _per_head(dov))):
            s = lax.dot_general(qb, kh, _NT, preferred_element_type=F32)
            p = jnp.where(mask, jnp.exp(s - lcols[hh]), 0.0)
            dp = lax.dot_general(dob, vh, _NT, preferred_element_type=F32)
            ds = (p * (dp - dcols[hh])).astype(BF16)
            parts = (jnp.dot(ds, kh, preferred_element_type=F32),
                     lax.dot_general(ds, qh, _TN, preferred_element_type=F32),
                     lax.dot_general(p.astype(BF16), doh, _TN, preferred_element_type=F32))
            dq, dkcat, dvcat = parts if dq is None else (dq + parts[0], dkcat + parts[1], dvcat + parts[2])
        _store_rows(dq_ref, dil, r, blk, dq * scale)
        if with_prev:
            put(dk_ref, r, blk - 1, pend_k + dkcat[:SPAN])
            put(dv_ref, r, blk - 1, pend_v + dvcat[:SPAN])
            return kc, vc, dkcat[SPAN:], dvcat[SPAN:]
        return kc, vc, dkcat, dvcat

    def after(r, carry):
        put(dk_ref, r, nblk - 1, carry[2])
        put(dv_ref, r, nblk - 1, carry[3])

    _over_units(dil, nblk, unit, None, after)


def _attn_bwd(name, qn, kn, kv, do, lse, dsum, seq):
    t, da = kn.shape
    hp = da // 128

    def body(q0_ref, q1_ref, q2_ref, k_ref, v_ref, do_ref, l_ref, d_ref, dq0_ref, dq1_ref, dq2_ref, dk_ref, dv_ref):
        groups = ((q0_ref, dq0_ref), (q1_ref, dq1_ref), (q2_ref, dq2_ref))
        for g, (q_ref, dq_ref) in enumerate(groups):
            _group_bwd(q_ref, k_ref, v_ref, do_ref, l_ref, d_ref, dq_ref, dk_ref, dv_ref, DILATIONS[g], seq, g == 0)

    blk_spec = lambda off: pl.BlockSpec((seq, 128), lambda b, h: (b, off + h))
    return pl.pallas_call(
        body, name=name, grid=(t // seq, hp),
        in_specs=[blk_spec(0), blk_spec(hp), blk_spec(2 * hp), blk_spec(0), blk_spec(hp), blk_spec(0), blk_spec(0), blk_spec(0)],
        out_specs=[blk_spec(0)] * 5,
        out_shape=[jax.ShapeDtypeStruct((t, da), F32)] * 5,
        compiler_params=_params("parallel", "parallel"))(qn, qn, qn, kn, kv, do, lse, dsum)


def _gate_bwd(name, dm, o, p1):
    t, da = o.shape
    tr = _tile(t, 256, 8)

    def body(dm_ref, o_ref, g_ref, do_ref, dg_ref, ds_ref):
        g = g_ref[...].astype(F32)
        sg = _sigmoid(g)
        dmv, ov = dm_ref[...].astype(F32), o_ref[...]
        do = dmv * (g * sg)
        do_ref[...] = do
        dg_ref[...] = (dmv * ov * (sg * (1.0 + g * (1.0 - sg)))).astype(BF16)
        ds_ref[...] = _seg_allsum64(do * ov)

    return pl.pallas_call(
        body, name=name, grid=(t // tr,),
        in_specs=[_row_spec(tr, da), _row_spec(tr, da), _row_spec(tr, da, 3)],
        out_specs=[_row_spec(tr, da)] * 3,
        out_shape=[jax.ShapeDtypeStruct((t, da), F32), jax.ShapeDtypeStruct((t, da), BF16), jax.ShapeDtypeStruct((t, da), F32)],
        compiler_params=_params("parallel"))(dm, o, p1)


def _cast_bf16(name, w2d, chip):
    r, c = w2d.shape
    tr = _tile(r, 256, 16)

    def body(chip_ref, x_ref, o_ref):
        o_ref[...] = x_ref[...].astype(BF16)

    grid_spec = pltpu.PrefetchScalarGridSpec(
        num_scalar_prefetch=1, grid=(r // tr,),
        in_specs=[pl.BlockSpec((tr, c), lambda i, m: (i, 0))],
        out_specs=pl.BlockSpec((None, tr, c), lambda i, m: (m[0], i, 0)))
    return pl.pallas_call(
        body, name=name, grid_spec=grid_spec, out_shape=jax.ShapeDtypeStruct((N_CHIPS, r, c), BF16),
        compiler_params=_params("parallel"))(chip, w2d)


def _adamw(name, w, g, m, v):
    r, c = w.shape
    tr = _tile(r, 256, 8)
    c1 = 1.0 - ADAM_B1 ** ADAM_STEP
    c2 = 1.0 - ADAM_B2 ** ADAM_STEP

    def body(w_ref, g_ref, m_ref, v_ref, d_ref, nm_ref, nv_ref):
        gv = g_ref[...]
        nm = ADAM_B1 * m_ref[...] + (1.0 - ADAM_B1) * gv
        nv = ADAM_B2 * v_ref[...] + (1.0 - ADAM_B2) * (gv * gv)
        nm_ref[...] = nm
        nv_ref[...] = nv
        d_ref[...] = -ADAM_LR * ((nm / c1) / (jnp.sqrt(nv / c2) + ADAM_EPS) + ADAM_WD * w_ref[...])

    sds = jax.ShapeDtypeStruct((r, c), F32)
    return pl.pallas_call(
        body, name=name, grid=(r // tr,), in_specs=[_row_spec(tr, c)] * 4, out_specs=[_row_spec(tr, c)] * 3,
        out_shape=[sds] * 3, compiler_params=_params("parallel"))(w, g, m, v)


def _pair_sum(name, gd, recv, core):
    _, r, c = gd.shape
    rh = r // 2
    tr = _tile(rh, 256, 8)
    nrt = rh // tr

    def body(c_ref, a_ref, b_ref, o_ref):
        o_ref[...] = (a_ref[...].astype(F32) + b_ref[...].astype(F32)).astype(BF16)

    grid_spec = pltpu.PrefetchScalarGridSpec(
        num_scalar_prefetch=1, grid=(N_CHIPS, nrt),
        in_specs=[pl.BlockSpec((None, tr, c), lambda j, i, cr: (j, cr[0] * nrt + i, 0)),
                  pl.BlockSpec((None, tr, c), lambda j, i, cr: (j, i, 0))],
        out_specs=pl.BlockSpec((None, tr, c), lambda j, i, cr: (j, i, 0)))
    return pl.pallas_call(
        body, name=name, grid_spec=grid_spec, out_shape=jax.ShapeDtypeStruct((N_CHIPS, rh, c), BF16),
        compiler_params=_params("parallel", "parallel"))(core, gd, recv)


def _chip_sum(name, sums, parts, order):
    _, rh, c = parts.shape
    tr = _tile(rh, 256, 16)
    nrt = rh // tr

    def body(o_ref_, s_ref, p1_ref, p2_ref, p3_ref, o_ref):
        acc = s_ref[...].astype(F32)
        for p_ref in (p1_ref, p2_ref, p3_ref):
            acc = acc + p_ref[...].astype(F32)
        o_ref[...] = acc

    slot = lambda k: pl.BlockSpec((None, tr, c), lambda i, o: (o[k], i, 0))
    grid_spec = pltpu.PrefetchScalarGridSpec(
        num_scalar_prefetch=1, grid=(nrt,),
        in_specs=[slot(0), slot(1), slot(2), slot(3)],
        out_specs=pl.BlockSpec((tr, c), lambda i, o: (o[N_CHIPS] * nrt + i, 0)))
    return pl.pallas_call(
        body, name=name, grid_spec=grid_spec, out_shape=jax.ShapeDtypeStruct((2 * rh, c), F32),
        compiler_params=_params("parallel"))(order, sums, parts, parts, parts)


HBM = pl.BlockSpec(memory_space=pl.ANY)


def _place():
    x, y, c = lax.axis_index("x"), lax.axis_index("y"), lax.axis_index("c")
    chips = [(1 - x, y), (x, 1 - y), (1 - x, 1 - y)]
    return x, y, c, chips


def _half(ref, hc):
    rows = ref.shape[0] // 2
    return ref.at[pl.ds(hc * rows, rows)]


class _Plan:
    def __init__(self, ins, out_shapes, aliases, n_sems, start, finish):
        self.ins, self.out_shapes, self.aliases, self.n_sems = list(ins), list(out_shapes), dict(aliases), n_sems
        self.start, self.finish = start, finish


def _gather_plan(shards):
    n = len(shards)

    def ici(place, outs, send_sems, recv_sems, i, k, slot):
        x, y, c, chips = place
        half = _half(outs[i].at[slot], c)
        return pltpu.make_async_remote_copy(
            src_ref=half, dst_ref=half, send_sem=send_sems.at[6 * i + k], recv_sem=recv_sems.at[6 * i + k],
            device_id=(chips[k][0], chips[k][1], c), device_id_type=MESH)

    def d2d(place, outs, send_sems, recv_sems, i, k, slot, hc):
        x, y, c, chips = place
        half = _half(outs[i].at[slot], hc)
        return pltpu.make_async_remote_copy(
            src_ref=half, dst_ref=half, send_sem=send_sems.at[6 * i + 3 + k], recv_sem=recv_sems.at[6 * i + 3 + k],
            device_id=(x, y, 1 - c), device_id_type=MESH)

    def start(place, ins, outs, send_sems, recv_sems):
        x, y, c, chips = place
        for i in range(n):
            for k in range(3):
                ici(place, outs, send_sems, recv_sems, i, k, 2 * x + y).start()

    def finish(place, ins, outs, send_sems, recv_sems):
        x, y, c, chips = place
        theirs = [2 * chip[0] + chip[1] for chip in chips]
        for i in range(n):
            for k in range(3):
                ici(place, outs, send_sems, recv_sems, i, k, theirs[k]).wait_recv()
                d2d(place, outs, send_sems, recv_sems, i, k, theirs[k], c).start()
        for i in range(n):
            for k in range(3):
                d2d(place, outs, send_sems, recv_sems, i, k, theirs[k], 1 - c).wait_recv()
        for i in range(n):
            for k in range(3):
                ici(place, outs, send_sems, recv_sems, i, k, 2 * x + y).wait_send()
                d2d(place, outs, send_sems, recv_sems, i, k, theirs[k], c).wait_send()

    return _Plan(shards, [jax.ShapeDtypeStruct(s.shape, s.dtype) for s in shards], {i: i for i in range(n)}, 6 * n,
                 start, finish)


def _scatter_plan(sums):
    n = len(sums)

    def copy(place, ins, outs, send_sems, recv_sems, i, k, src_slot, dst_slot):
        x, y, c, chips = place
        return pltpu.make_async_remote_copy(
            src_ref=ins[i].at[src_slot], dst_ref=outs[i].at[dst_slot],
            send_sem=send_sems.at[3 * i + k], recv_sem=recv_sems.at[3 * i + k],
            device_id=(chips[k][0], chips[k][1], c), device_id_type=MESH)

    def start(place, ins, outs, send_sems, recv_sems):
        x, y, c, chips = place
        for i in range(n):
            for k, chip in enumerate(chips):
                copy(place, ins, outs, send_sems, recv_sems, i, k, 2 * chip[0] + chip[1], 2 * x + y).start()

    def finish(place, ins, outs, send_sems, recv_sems):
        x, y, c, chips = place
        for i in range(n):
            for k, chip in enumerate(chips):
                theirs = 2 * chip[0] + chip[1]
                copy(place, ins, outs, send_sems, recv_sems, i, k, theirs, 2 * x + y).wait_send()
                copy(place, ins, outs, send_sems, recv_sems, i, k, theirs, theirs).wait_recv()

    return _Plan(sums, [jax.ShapeDtypeStruct(s.shape, s.dtype) for s in sums], {}, 3 * n, start, finish)


def _hosted_call(body, plan, *, name, grid, in_specs, out_specs, out_shape, scratch_shapes, args, sem):
    in_specs, out_specs, out_shape, scratch_shapes = list(in_specs), list(out_specs), list(out_shape), list(scratch_shapes)
    if plan is None:
        res = pl.pallas_call(body, name=name, grid=grid, in_specs=in_specs, out_specs=out_specs, out_shape=out_shape,
                             scratch_shapes=scratch_shapes, compiler_params=_params(*sem))(*args)
        return list(res), []
    n_in, n_out, n_scr = len(in_specs), len(out_specs), len(scratch_shapes)
    p_in, p_out = len(plan.ins), len(plan.out_shapes)

    def hosted(*refs):
        refs = list(refs)
        ins, pins = refs[:n_in], refs[n_in:n_in + p_in]
        outs = refs[n_in + p_in:n_in + p_in + n_out]
        pouts = refs[n_in + p_in + n_out:n_in + p_in + n_out + p_out]
        scr = refs[n_in + p_in + n_out + p_out:n_in + p_in + n_out + p_out + n_scr]
        send_sems, recv_sems = refs[-2:]
        place = _place()
        ids = [pl.program_id(d) for d in range(len(grid))]
        first = functools.reduce(jnp.logical_and, [i == 0 for i in ids])
        last = functools.reduce(jnp.logical_and, [i == g - 1 for i, g in zip(ids, grid)])

        @pl.when(first)
        def _():
            plan.start(place, pins, pouts, send_sems, recv_sems)

        body(*ins, *outs, *scr)

        @pl.when(last)
        def _():
            plan.finish(place, pins, pouts, send_sems, recv_sems)

    res = pl.pallas_call(
        hosted, name=name, grid=grid, in_specs=in_specs + [HBM] * p_in, out_specs=out_specs + [HBM] * p_out,
        out_shape=out_shape + plan.out_shapes,
        input_output_aliases={n_in + i: n_out + o for i, o in plan.aliases.items()},
        scratch_shapes=scratch_shapes + [pltpu.SemaphoreType.DMA((plan.n_sems,)), pltpu.SemaphoreType.DMA((plan.n_sems,))],
        compiler_params=_params(*(("arbitrary",) * len(grid))))(*args, *plan.ins)
    return list(res[:n_out]), list(res[n_out:])


def _run_plan(name, plan):
    p_in = len(plan.ins)

    def body(*refs):
        ins, outs = refs[:p_in], refs[p_in:p_in + len(plan.out_shapes)]
        send_sems, recv_sems = refs[-2:]
        place = _place()
        plan.start(place, ins, outs, send_sems, recv_sems)
        plan.finish(place, ins, outs, send_sems, recv_sems)

    return pl.pallas_call(
        body, name=name, in_specs=[HBM] * p_in, out_specs=[HBM] * len(plan.out_shapes), out_shape=plan.out_shapes,
        input_output_aliases=plan.aliases,
        scratch_shapes=[pltpu.SemaphoreType.DMA((plan.n_sems,)), pltpu.SemaphoreType.DMA((plan.n_sems,))],
        )(*plan.ins)


def _pair_exchange(name, grads):
    n = len(grads)

    def body(*refs):
        ins, outs = refs[:n], refs[n:2 * n]
        send_sems, recv_sems = refs[2 * n:]
        x, y, c, _ = _place()
        cps = []
        for i in range(n):
            rows = ins[i].shape[1] // 2
            cp = pltpu.make_async_remote_copy(
                src_ref=ins[i].at[:, pl.ds((1 - c) * rows, rows), :], dst_ref=outs[i],
                send_sem=send_sems.at[i], recv_sem=recv_sems.at[i], device_id=(x, y, 1 - c), device_id_type=MESH)
            cp.start()
            cps.append(cp)
        for cp in cps:
            cp.wait()

    return pl.pallas_call(
        body, name=name, in_specs=[HBM] * n, out_specs=[HBM] * n,
        out_shape=[jax.ShapeDtypeStruct((N_CHIPS, g.shape[1] // 2, g.shape[2]), g.dtype) for g in grads],
        scratch_shapes=[pltpu.SemaphoreType.DMA((n,)), pltpu.SemaphoreType.DMA((n,))],
        )(*grads)


def _sibling_join(grads):
    n = len(grads)

    def body(*refs):
        outs = refs[n:2 * n]
        send_sems, recv_sems = refs[2 * n:]
        x, y, c, _ = _place()
        cps = []
        for i in range(n):
            cp = pltpu.make_async_remote_copy(
                src_ref=_half(outs[i], c), dst_ref=_half(outs[i], c), send_sem=send_sems.at[i], recv_sem=recv_sems.at[i],
                device_id=(x, y, 1 - c), device_id_type=MESH)
            cp.start()
            cps.append(cp)
        for i, cp in enumerate(cps):
            cp.wait_send()
            pltpu.make_async_remote_copy(
                src_ref=_half(outs[i], 1 - c), dst_ref=_half(outs[i], 1 - c), send_sem=send_sems.at[i],
                recv_sem=recv_sems.at[i], device_id=(x, y, 1 - c), device_id_type=MESH).wait_recv()

    return pl.pallas_call(
        body, name="sibling_join", in_specs=[HBM] * n, out_specs=[HBM] * n,
        out_shape=[jax.ShapeDtypeStruct(g.shape, g.dtype) for g in grads],
        input_output_aliases={i: i for i in range(n)},
        scratch_shapes=[pltpu.SemaphoreType.DMA((n,)), pltpu.SemaphoreType.DMA((n,))],
        )(*grads)


def _gather8(name, block, reduce):
    m, n = block.shape

    def body(x_ref, out_ref, *scratch):
        if reduce:
            all_ref, send_sems, recv_sems, local_sem = scratch
        else:
            all_ref = out_ref
            send_sems, recv_sems, local_sem = scratch
        x, y, c, chips = _place()
        me, sibling = (x, y, c), (x, y, 1 - c)

        def rows(px, py, pc):
            return all_ref.at[pl.ds((4 * px + 2 * py + pc) * m, m), :]

        def copy(k, blk, to, src=None):
            return pltpu.make_async_remote_copy(
                src_ref=rows(*blk) if src is None else src, dst_ref=rows(*blk),
                send_sem=send_sems.at[k], recv_sem=recv_sems.at[k], device_id=to, device_id_type=MESH)

        mine = pltpu.make_async_copy(x_ref, rows(*me), local_sem)
        mine.start()
        first = [copy(0, me, sibling, src=x_ref)]
        first += [copy(1 + j, me, (chip[0], chip[1], c), src=x_ref) for j, chip in enumerate(chips)]
        for cp in first:
            cp.start()
        passed = [copy(4 + j, (chip[0], chip[1], c), sibling) for j, chip in enumerate(chips)]
        for j, chip in enumerate(chips):
            copy(1 + j, (chip[0], chip[1], c), me).wait_recv()
            passed[j].start()
        copy(0, sibling, me).wait_recv()
        for j, chip in enumerate(chips):
            copy(4 + j, (chip[0], chip[1], 1 - c), me).wait_recv()
        for cp in first + passed:
            cp.wait_send()
        mine.wait()
        if reduce:
            acc = all_ref[pl.ds(0, m), :]
            for d in range(1, 8):
                acc = acc + all_ref[pl.ds(d * m, m), :]
            out_ref[...] = acc

    sems = [pltpu.SemaphoreType.DMA((7,)), pltpu.SemaphoreType.DMA((7,)), pltpu.SemaphoreType.DMA]
    scratch = ([pltpu.VMEM((8 * m, n), F32)] if reduce else []) + sems
    return pl.pallas_call(
        body, name=name,
        out_shape=jax.ShapeDtypeStruct((m, n) if reduce else (8 * m, n), F32),
        in_specs=[pl.BlockSpec(memory_space=pltpu.VMEM)], out_specs=pl.BlockSpec(memory_space=pltpu.VMEM),
        scratch_shapes=scratch)(block)


def _pad_rows(a, rows):
    return jnp.concatenate([a, jnp.zeros((rows - a.shape[0], a.shape[1]), a.dtype)], axis=0)


def kernel(x, p, norm_g, w_in_a, conv_w, conv_b, ln_g, ln_b, w_out_a, kv_norm_g, w_kv, k_norm_g, w_in_b, q_norm_g, w_out_b, ple_norm_g, w_ple_gate, w_ple_proj, loss_target, m_norm_g, m_w_in_a, m_conv_w, m_conv_b, m_ln_g, m_ln_b, m_w_out_a, m_kv_norm_g, m_w_kv, m_k_norm_g, m_w_in_b, m_q_norm_g, m_w_out_b, m_ple_norm_g, m_w_ple_gate, m_w_ple_proj, v_norm_g, v_w_in_a, v_conv_w, v_conv_b, v_ln_g, v_ln_b, v_w_out_a, v_kv_norm_g, v_w_kv, v_k_norm_g, v_w_in_b, v_q_norm_g, v_w_out_b, v_ple_norm_g, v_w_ple_gate, v_w_ple_proj):
    nb, seq, dm = x.shape
    t = nb * seq
    ple = p.shape[-1]
    ccs = conv_w.shape[-1]
    cc = N_CHIPS * ccs
    da = dm
    nheads = da // HEAD_DIM
    assert seq == DILATIONS[-1] * SPAN and da % 128 == 0 and ccs % 128 == 0

    core = lax.axis_index("c").astype(jnp.int32).reshape(1)
    chip = (2 * lax.axis_index("x") + lax.axis_index("y")).astype(jnp.int32)
    chip1 = chip.reshape(1)
    sum_order = jnp.concatenate([(chip1 + k) % N_CHIPS for k in range(N_CHIPS)] + [core])

    x2 = x.reshape(t, dm)
    tgt2 = loss_target.reshape(t, dm)
    p0 = p[0].reshape(t, ple)
    p1 = p[1].reshape(t, ple)

    big = [
        ("w_in_a", w_in_a[0], "col"), ("w_out_a", w_out_a[0], "row"), ("w_kv", w_kv, "col"),
        ("w_in_b", w_in_b[0], "col"), ("w_out_b", w_out_b[0], "row"),
        ("w_ple_gate0", w_ple_gate[0], "row"), ("w_ple_gate1", w_ple_gate[1], "row"),
        ("w_ple_proj0", w_ple_proj[0], "col"), ("w_ple_proj1", w_ple_proj[1], "col"),
    ]
    shard_shape = {nm: w.shape for nm, w, _ in big}
    names = [nm for nm, _, _ in big]
    own = [_cast_bf16("cast_" + nm, w, chip1) for nm, w, _ in big]
    W = {}

    vec_rows = 40
    small = _pad_rows(jnp.concatenate([conv_w[0], conv_b, ln_g, ln_b], axis=0), vec_rows)
    allv = _gather8("gather_conv_vectors", small, reduce=False).reshape(N_CHIPS, 2, vec_rows, ccs)[:, 0]
    allv = allv.transpose(1, 0, 2).reshape(vec_rows, cc)
    cw_full, cb_full, lg_full, lb_full = allv[:HALO], allv[31:32], allv[32:33], allv[33:34]
    cw_full = cw_full * (lax.broadcasted_iota(jnp.int32, (HALO, 1), 0) < CONV_WIDTH).astype(F32)

    tables = _rope_tables(seq)
    gain_q = jnp.tile(q_norm_g[0][:, None, :], (1, nheads, 1)).reshape(1, 3 * da)
    gain_k = jnp.tile(k_norm_g[None, :], (1, nheads))
    g0, g1 = norm_g[0:1], norm_g[1:2]
    pg0, pg1 = ple_norm_g[0:1], ple_norm_g[1:2]
    kvg = kv_norm_g[None, :]

    (u0,), (W[names[0]],) = _rms_fwd("rms_u0", x2, [g0], plan=_gather_plan(own[:1]))
    pa, gathered = _mm_nn("mm_in_a", u0, W["w_in_a"], "col", out_dtype=BF16, plan=_gather_plan(own[1:]))
    W.update(zip(names[1:], gathered))
    for nm in ("w_ple_proj0", "w_ple_proj1"):
        W[nm] = W[nm].transpose(1, 0, 2).reshape(1, W[nm].shape[1], -1)
    conv_out, m_a = _mixa_fwd("mixa_fwd", pa, cw_full, cb_full, lg_full, lb_full, seq)
    h0, r0 = _mm_nn("mm_out_a", m_a, W["w_out_a"], "row", resid=x2, norm_gain=pg0)
    gpre0 = _mm_nn("mm_gate0", r0, W["w_ple_gate0"], "row", out_dtype=BF16)
    pp0 = _mm_nn("mm_proj0", p0, W["w_ple_proj0"], "col", out_dtype=BF16)
    x1, kvn, u1 = _ple_fwd("ple_fwd0", h0, gpre0, pp0, [kvg, g1])
    kv, kn = _mm_nn("mm_kv", kvn, W["w_kv"], "col", heads=(gain_k, tables, da, seq))
    pb, qn = _mm_nn("mm_in_b", u1, W["w_in_b"], "col", out_dtype=BF16, heads=(gain_q, tables, 3 * da, seq))
    o, lse, m_b = _attn_fwd("attn_fwd", qn, kn, kv, pb, seq)
    h1, r1 = _mm_nn("mm_out_b", m_b, W["w_out_b"], "row", resid=x1, norm_gain=pg1)
    gpre1 = _mm_nn("mm_gate1", r1, W["w_ple_gate1"], "row", out_dtype=BF16)
    pp1 = _mm_nn("mm_proj1", p1, W["w_ple_proj1"], "col", out_dtype=BF16)
    dy, dgp1, dpp1, sq = _ple_loss("ple_loss", h1, gpre1, pp1, tgt2)
    loss = lax.psum(0.5 * sq[0, 0] / dm, ("x", "y", "c"))

    G = {}
    G["w_ple_gate1"] = _mm_tn("tn_gate1", r1, dgp1, "row", shard_shape["w_ple_gate1"])
    G["w_ple_proj1"] = _mm_tn("tn_proj1", p1, dpp1, "col", shard_shape["w_ple_proj1"], whole=True)
    dr1 = _mm_nt("nt_gate1", dgp1, W["w_ple_gate1"], "row", out_dtype=BF16)
    dh1, (dpg1,) = _rms_bwd("rms_bwd_r1", h1, dy, [(pg1, dr1)])
    G["w_out_b"] = _mm_tn("tn_out_b", m_b, dh1, "row", shard_shape["w_out_b"])
    dm_b = _mm_nt("nt_out_b", dh1, W["w_out_b"], "row", out_dtype=BF16)
    d_o, dgt, dsum = _gate_bwd("gate_bwd", dm_b, o, pb)
    dq0, dq1, dq2, dk, dv = _attn_bwd("attn_bwd", qn, kn, kv, d_o, lse, dsum, seq)
    dpb, dgq = _q_bwd("q_bwd", pb, gain_q, tables, [dq0, dq1, dq2], dgt, seq)
    dkv, dgk = _k_bwd("k_bwd", kv, gain_k, tables, dk, dv, seq)
    G["w_in_b"] = _mm_tn("tn_in_b", u1, dpb, "col", shard_shape["w_in_b"])
    du1 = _mm_nt("nt_in_b", dpb, W["w_in_b"], "col", out_dtype=BF16)
    G["w_kv"] = _mm_tn("tn_kv", kvn, dkv, "col", shard_shape["w_kv"])
    dkvn = _mm_nt("nt_kv", dkv, W["w_kv"], "col", out_dtype=BF16)
    dx1, (dg1, dkvg), dgp0, dpp0 = _rms_bwd("rms_bwd_x1", x1, dh1, [(g1, du1), (kvg, dkvn)], ple=(gpre0, pp0))
    G["w_ple_gate0"] = _mm_tn("tn_gate0", r0, dgp0, "row", shard_shape["w_ple_gate0"])
    G["w_ple_proj0"] = _mm_tn("tn_proj0", p0, dpp0, "col", shard_shape["w_ple_proj0"], whole=True)
    dr0 = _mm_nt("nt_gate0", dgp0, W["w_ple_gate0"], "row", out_dtype=BF16)
    dh0, (dpg0,) = _rms_bwd("rms_bwd_r0", h0, dx1, [(pg0, dr0)])
    G["w_out_a"] = _mm_tn("tn_out_a", m_a, dh0, "row", shard_shape["w_out_a"])
    dm_a = _mm_nt("nt_out_a", dh0, W["w_out_a"], "row", out_dtype=BF16)

    def pair_sums(tag, batch):
        recv = _pair_exchange("pair_exchange_" + tag, [G[nm] for nm in batch])
        return [_pair_sum("pair_sum_" + nm, G[nm], rc, core) for nm, rc in zip(batch, recv)]

    late = ["w_kv", "w_in_b", "w_out_b", "w_ple_gate1", "w_ple_proj1"]
    early = ["w_in_a", "w_out_a", "w_ple_gate0", "w_ple_proj0"]
    sums_late = pair_sums("late", late)
    (dc, dz, dlg, dlb, dcb), parts_late = _mixa_bwd1("mixa_bwd1", conv_out, pa, dm_a, lg_full, lb_full,
                                                     plan=_scatter_plan(sums_late))
    dpa, dcw = _mixa_bwd2("mixa_bwd2", dc, pa, dz, cw_full, seq)
    G["w_in_a"] = _mm_tn("tn_in_a", u0, dpa, "col", shard_shape["w_in_a"])
    sums_early = pair_sums("early", early)
    du0, parts_early = _mm_nt("nt_in_a", dpa, W["w_in_a"], "col", out_dtype=BF16, plan=_scatter_plan(sums_early))
    dx, (dg0,) = _rms_bwd("rms_bwd_x", x2, dh0, [(g0, du0)])
    grad_x = dx.reshape(nb, seq, dm)

    sums = dict(zip(late + early, sums_late + sums_early))
    parts = dict(zip(late + early, parts_late + parts_early))
    halves = [_chip_sum("chip_sum_" + nm, sums[nm], parts[nm], sum_order) for nm in names]
    gfull = dict(zip(names, _sibling_join(halves)))

    def as_rows(a):
        return a.reshape(-1, dm)

    small_parts = [as_rows(dcw), as_rows(dcb), as_rows(dlg), as_rows(dlb), dg0, dg1, dkvg, dpg0, dpg1, as_rows(dgk), as_rows(dgq)]
    counts = [a.shape[0] for a in small_parts]
    total = sum(counts)
    packed = _pad_rows(jnp.concatenate(small_parts, axis=0), -(-total // 8) * 8)
    red = _gather8("reduce_small", packed, reduce=True)
    pieces, off = [], 0
    for n_ in counts:
        pieces.append(red[off:off + n_])
        off += n_
    r_dcw, r_dcb, r_dlg, r_dlb, r_g0, r_g1, r_kvg, r_pg0, r_pg1, r_gk, r_gq = pieces
    my_cols = lambda a: lax.dynamic_slice_in_dim(a.reshape(-1, cc), chip * ccs, ccs, axis=1)
    small_grads = {
        "norm_g": jnp.concatenate([r_g0, r_g1], axis=0),
        "conv_w": my_cols(r_dcw)[:CONV_WIDTH],
        "conv_b": my_cols(r_dcb), "ln_g": my_cols(r_dlg), "ln_b": my_cols(r_dlb),
        "kv_norm_g": r_kvg,
        "k_norm_g": r_gk.reshape(nheads, HEAD_DIM).sum(axis=0, keepdims=True),
        "q_norm_g": r_gq.reshape(3, nheads, HEAD_DIM).sum(axis=1),
        "ple_norm_g": jnp.concatenate([r_pg0, r_pg1], axis=0),
    }

    given = dict(norm_g=norm_g, w_in_a=w_in_a, conv_w=conv_w, conv_b=conv_b, ln_g=ln_g, ln_b=ln_b, w_out_a=w_out_a,
                 kv_norm_g=kv_norm_g, w_kv=w_kv, k_norm_g=k_norm_g, w_in_b=w_in_b, q_norm_g=q_norm_g, w_out_b=w_out_b,
                 ple_norm_g=ple_norm_g, w_ple_gate=w_ple_gate, w_ple_proj=w_ple_proj)
    mom1 = dict(norm_g=m_norm_g, w_in_a=m_w_in_a, conv_w=m_conv_w, conv_b=m_conv_b, ln_g=m_ln_g, ln_b=m_ln_b,
                w_out_a=m_w_out_a, kv_norm_g=m_kv_norm_g, w_kv=m_w_kv, k_norm_g=m_k_norm_g, w_in_b=m_w_in_b,
                q_norm_g=m_q_norm_g, w_out_b=m_w_out_b, ple_norm_g=m_ple_norm_g, w_ple_gate=m_w_ple_gate,
                w_ple_proj=m_w_ple_proj)
    mom2 = dict(norm_g=v_norm_g, w_in_a=v_w_in_a, conv_w=v_conv_w, conv_b=v_conv_b, ln_g=v_ln_g, ln_b=v_ln_b,
                w_out_a=v_w_out_a, kv_norm_g=v_kv_norm_g, w_kv=v_w_kv, k_norm_g=v_k_norm_g, w_in_b=v_w_in_b,
                q_norm_g=v_q_norm_g, w_out_b=v_w_out_b, ple_norm_g=v_ple_norm_g, w_ple_gate=v_w_ple_gate,
                w_ple_proj=v_w_ple_proj)
    order = ["norm_g", "w_in_a", "conv_w", "conv_b", "ln_g", "ln_b", "w_out_a", "kv_norm_g", "w_kv", "k_norm_g", "w_in_b",
             "q_norm_g", "w_out_b", "ple_norm_g", "w_ple_gate", "w_ple_proj"]
    grads, deltas, new_m, new_v = {}, {}, {}, {}
    for nm in order:
        shape = given[nm].shape
        if nm in ("w_ple_gate", "w_ple_proj"):
            g2 = jnp.concatenate([gfull[nm + "0"], gfull[nm + "1"]], axis=0)
        elif nm in gfull:
            g2 = gfull[nm]
        else:
            g2 = small_grads[nm]
        two_d = g2.shape
        d2, m2, v2 = _adamw("adamw_" + nm, given[nm].reshape(two_d), g2, mom1[nm].reshape(two_d), mom2[nm].reshape(two_d))
        grads[nm], deltas[nm], new_m[nm], new_v[nm] = (a.reshape(shape) for a in (g2, d2, m2, v2))

    return (loss, grad_x, *[grads[n_] for n_ in order], *[deltas[n_] for n_ in order],
            *[new_m[n_] for n_ in order], *[new_v[n_] for n_ in order])
```

```python
import functools

import jax
import jax.numpy as jnp
from jax import lax
from jax.experimental import pallas as pl
from jax.experimental.pallas import tpu as pltpu

F32 = jnp.float32
BF16 = jnp.bfloat16
MESH = pl.DeviceIdType.MESH

EPS = 1e-6
NEG_INF = -1e30
HEAD_DIM = 64
ROPE_DIM = 16
ROPE_THETA = 500000.0
CONV_WIDTH = 31
SUBLANES = 8
CONV_ROWS = 64
HALO = 32
SPAN = 128
DILATIONS = (1, 4, 16)
ADAM_LR, ADAM_B1, ADAM_B2, ADAM_EPS, ADAM_WD, ADAM_STEP = 0.001, 0.9, 0.999, 1e-08, 0.01, 10
N_CHIPS = 4
VMEM_LIMIT = 56 * 1024 * 1024


def _tile(n, target, mult=128):
    best = None
    t = mult
    while t <= min(n, target):
        if n % t == 0:
            best = t
        t += mult
    return best if best is not None else n


def _params(*sem):
    return pltpu.CompilerParams(dimension_semantics=tuple(sem) if sem else None, vmem_limit_bytes=VMEM_LIMIT)


def _sigmoid(x):
    return 0.5 * jnp.tanh(0.5 * x) + 0.5


def _seg_allsum64(x, terms=1):
    tr, w = x.shape
    cw = 256 if w % 256 == 0 else 128
    n = w // cw
    shift = HEAD_DIM.bit_length() - 1
    ri = lax.shift_right_logical(lax.broadcasted_iota(jnp.int32, (cw, cw), 0), shift)
    ci = lax.shift_right_logical(lax.broadcasted_iota(jnp.int32, (cw, cw), 1), shift)
    ones = (ri == ci).astype(BF16)

    def stack(v):
        return jnp.concatenate([v[:, j * cw:(j + 1) * cw] for j in range(n)], axis=0)

    hi = x.astype(BF16)
    s = jnp.dot(stack(hi), ones, preferred_element_type=F32)
    if terms == 2:
        lo = (x - hi.astype(F32)).astype(BF16)
        s = s + jnp.dot(stack(lo), ones, preferred_element_type=F32)
    return jnp.concatenate([s[j * tr:(j + 1) * tr] for j in range(n)], axis=1)


def _colsum(x):
    return jnp.sum(x, axis=0, keepdims=True)


def _shards_view(w, kind):
    return w if kind == "col" else w.reshape(1, -1, w.shape[2])


def _mm_nn(name, a, w, kind, *, out_dtype=F32, resid=None, norm_gain=None, heads=None, plan=None):
    t = a.shape[0]
    w = _shards_view(w, kind)
    ns, k, c = w.shape
    n = ns * c
    tm = _tile(t, 1024 if norm_gain is None else 512, 8)
    tk = _tile(k, 2048)
    tn = _tile(c, 1024)
    nk = k // tk
    per = c // tn
    assert norm_gain is None or tn == n, "the fused RMSNorm needs whole rows in one tile"
    assert norm_gain is None or heads is None
    n_in = 2 + (resid is not None) + (norm_gain is not None) + (4 if heads is not None else 0)
    if heads is not None:
        h_gain, h_tables, h_width, h_seq = heads
        assert h_width % tn == 0 and h_seq % tm == 0
        h_blocks = h_width // tn

    def body(*refs):
        a_ref, w_ref = refs[:2]
        r_ref = refs[2] if resid is not None else None
        g_ref = refs[n_in - 1] if norm_gain is not None else None
        o_ref = refs[n_in]
        part = jnp.dot(a_ref[...].astype(BF16), w_ref[...], preferred_element_type=F32)

        def finish(out):
            if resid is not None:
                out = out + r_ref[...]
            stored = out.astype(out_dtype)
            o_ref[...] = stored
            if norm_gain is not None:
                normed = out * lax.rsqrt(jnp.mean(out * out, axis=-1, keepdims=True) + EPS) * g_ref[...]
                refs[n_in + 1][...] = normed.astype(BF16)
            if heads is not None:
                hg_ref, tc_ref, ta_ref, tb_ref = refs[n_in - 4:n_in]

                @pl.when(pl.program_id(1) < h_blocks)
                def _():
                    refs[n_in + 1][...] = _hnr_fwd_math(stored.astype(F32), hg_ref[...], tc_ref, ta_ref, tb_ref)

        if nk == 1:
            finish(part)
            return
        acc = refs[-1]
        kk = pl.program_id(2)

        @pl.when(kk == 0)
        def _():
            acc[...] = part

        @pl.when(kk > 0)
        def _():
            acc[...] += part

        @pl.when(kk == nk - 1)
        def _():
            finish(acc[...])

    in_specs = [pl.BlockSpec((tm, tk), lambda i, j, kk: (i, kk)),
                pl.BlockSpec((None, tk, tn), lambda i, j, kk: (j // per, kk, j % per))]
    args = [a, w]
    if resid is not None:
        in_specs.append(pl.BlockSpec((tm, tn), lambda i, j, kk: (i, j)))
        args.append(resid)
    out_specs = [pl.BlockSpec((tm, tn), lambda i, j, kk: (i, j))]
    out_shape = [jax.ShapeDtypeStruct((t, n), out_dtype)]
    if norm_gain is not None:
        in_specs.append(pl.BlockSpec((1, n), lambda i, j, kk: (0, 0)))
        args.append(norm_gain)
        out_specs.append(pl.BlockSpec((tm, tn), lambda i, j, kk: (i, j)))
        out_shape.append(jax.ShapeDtypeStruct((t, n), BF16))
    if heads is not None:
        last = h_blocks - 1
        per_seq = h_seq // tm
        tab = pl.BlockSpec((tm, 128), lambda i, j, kk: (i % per_seq, 0))
        in_specs += [pl.BlockSpec((1, tn), lambda i, j, kk: (0, jnp.minimum(j, last))), tab, tab, tab]
        args += [h_gain, *h_tables]
        out_specs.append(pl.BlockSpec((tm, tn), lambda i, j, kk: (i, jnp.minimum(j, last))))
        out_shape.append(jax.ShapeDtypeStruct((t, h_width), F32))
    outs, carried = _hosted_call(
        body, plan, name=name, grid=(t // tm, n // tn, nk), in_specs=in_specs, out_specs=out_specs, out_shape=out_shape,
        scratch_shapes=[pltpu.VMEM((tm, tn), F32)] if nk > 1 else [],
        args=args, sem=("parallel", "arbitrary", "arbitrary"))
    out = outs[0] if len(outs) == 1 else tuple(outs)
    return out if plan is None else (out, carried)


def _mm_nt(name, d, w, kind, *, out_dtype=F32, plan=None):
    t = d.shape[0]
    w = _shards_view(w, kind)
    ns, k, c = w.shape
    n = ns * c
    tm = _tile(t, 1024, 8)
    to = _tile(k, 1024)
    tc = _tile(c, 1536)
    nc = n // tc
    per = c // tc

    def body(d_ref, w_ref, o_ref, *scratch):
        part = lax.dot_general(d_ref[...].astype(BF16), w_ref[...], (((1,), (1,)), ((), ())),
                               preferred_element_type=F32)
        if nc == 1:
            o_ref[...] = part.astype(out_dtype)
            return
        acc = scratch[0]
        kk = pl.program_id(2)

        @pl.when(kk == 0)
        def _():
            acc[...] = part

        @pl.when(kk > 0)
        def _():
            acc[...] += part

        @pl.when(kk == nc - 1)
        def _():
            o_ref[...] = acc[...].astype(out_dtype)

    (out,), carried = _hosted_call(
        body, plan, name=name, grid=(t // tm, k // to, nc),
        in_specs=[pl.BlockSpec((tm, tc), lambda i, j, kk: (i, kk)),
                  pl.BlockSpec((None, to, tc), lambda i, j, kk: (kk // per, j, kk % per))],
        out_specs=[pl.BlockSpec((tm, to), lambda i, j, kk: (i, j))],
        out_shape=[jax.ShapeDtypeStruct((t, k), out_dtype)],
        scratch_shapes=[pltpu.VMEM((tm, to), F32)] if nc > 1 else [],
        args=[d, w], sem=("parallel", "parallel", "arbitrary"))
    return out if plan is None else (out, carried)


def _mm_tn(name, a, d, kind, shard_shape, whole=False):
    t, k = a.shape
    n = d.shape[1]
    ns = N_CHIPS if kind == "col" and not whole else 1
    c = n // ns
    tkm = _tile(k, 1024)
    tn = _tile(c, 1536)
    tt = _tile(t, 1024, 8)
    nt = t // tt
    per = c // tn

    def body(a_ref, d_ref, o_ref, acc):
        kk = pl.program_id(2)
        part = lax.dot_general(a_ref[...].astype(BF16), d_ref[...].astype(BF16), (((0,), (0,)), ((), ())),
                               preferred_element_type=F32)

        @pl.when(kk == 0)
        def _():
            acc[...] = part

        @pl.when(kk > 0)
        def _():
            acc[...] += part

        @pl.when(kk == nt - 1)
        def _():
            o_ref[...] = acc[...].astype(BF16)

    out = pl.pallas_call(
        body, name=name, grid=(k // tkm, n // tn, nt),
        in_specs=[pl.BlockSpec((tt, tkm), lambda i, j, kk: (kk, i)),
                  pl.BlockSpec((tt, tn), lambda i, j, kk: (kk, j))],
        out_specs=pl.BlockSpec((None, tkm, tn), lambda i, j, kk: (j // per, i, j % per)),
        out_shape=jax.ShapeDtypeStruct((ns, k, c), BF16),
        scratch_shapes=[pltpu.VMEM((tkm, tn), F32)],
        compiler_params=_params("parallel", "parallel", "arbitrary"))(a, d)
    if kind == "col" and whole:
        return out.reshape(k, N_CHIPS, n // N_CHIPS).transpose(1, 0, 2)
    return out.reshape((N_CHIPS,) + tuple(shard_shape))


def _row_spec(tr, w, col=0):
    return pl.BlockSpec((tr, w), lambda i: (i, col))


def _full_spec(shape):
    return pl.BlockSpec(shape, lambda i: tuple(0 for _ in shape))


def _rms_fwd(name, x, gains, plan=None):
    t, dm = x.shape
    tr = _tile(t, 256, 8)
    n = len(gains)

    def body(x_ref, *refs):
        xv = x_ref[...]
        xn = xv * lax.rsqrt(jnp.mean(xv * xv, axis=-1, keepdims=True) + EPS)
        for g_ref, o_ref in zip(refs[:n], refs[n:]):
            o_ref[...] = (xn * g_ref[...]).astype(BF16)

    outs, carried = _hosted_call(
        body, plan, name=name, grid=(t // tr,),
        in_specs=[_row_spec(tr, dm)] + [_full_spec((1, dm))] * n,
        out_specs=[_row_spec(tr, dm)] * n,
        out_shape=[jax.ShapeDtypeStruct((t, dm), BF16)] * n,
        scratch_shapes=[], args=[x, *gains], sem=("parallel",))
    return outs if plan is None else (outs, carried)


def _ple_bwd_math(dy, gpre, pp):
    sg = _sigmoid(gpre)
    return (dy * pp * sg * (1.0 - sg)).astype(BF16), (dy * sg).astype(BF16)


def _rms_bwd(name, x, resid, pairs, ple=None):
    t, dm = x.shape
    tr = _tile(t, 256, 8)
    n = len(pairs)
    n_in = 2 * n + (2 if ple is not None else 0)

    def body(x_ref, r_ref, *refs):
        ins, outs = refs[:n_in], refs[n_in:]
        i = pl.program_id(0)
        xv = x_ref[...]
        rs = lax.rsqrt(jnp.mean(xv * xv, axis=-1, keepdims=True) + EPS)
        xn = xv * rs
        total = r_ref[...]
        for kx in range(n):
            g_ref, du_ref = ins[2 * kx], ins[2 * kx + 1]
            dg_ref = outs[1 + kx]
            du = du_ref[...].astype(F32)

            @pl.when(i == 0)
            def _():
                dg_ref[...] = jnp.zeros_like(dg_ref)

            dg_ref[...] += _colsum(du * xn)
            dxh = du * g_ref[...]
            total = total + rs * (dxh - xn * jnp.mean(dxh * xn, axis=-1, keepdims=True))
        outs[0][...] = total
        if ple is not None:
            outs[1 + n][...], outs[2 + n][...] = _ple_bwd_math(total, ins[2 * n][...].astype(F32), ins[2 * n + 1][...].astype(F32))

    in_specs = [_row_spec(tr, dm), _row_spec(tr, dm)]
    args = [x, resid]
    for g, du in pairs:
        in_specs += [_full_spec((1, dm)), _row_spec(tr, dm)]
        args += [g, du]
    out_specs = [_row_spec(tr, dm)] + [_full_spec((1, dm))] * n
    out_shape = [jax.ShapeDtypeStruct((t, dm), F32)] + [jax.ShapeDtypeStruct((1, dm), F32)] * n
    if ple is not None:
        in_specs += [_row_spec(tr, dm)] * 2
        args += list(ple)
        out_specs += [_row_spec(tr, dm)] * 2
        out_shape += [jax.ShapeDtypeStruct((t, dm), BF16)] * 2
    outs = pl.pallas_call(
        body, name=name, grid=(t // tr,), in_specs=in_specs, out_specs=out_specs, out_shape=out_shape,
        compiler_params=_params("arbitrary"))(*args)
    if ple is not None:
        return outs[0], list(outs[1:1 + n]), outs[1 + n], outs[2 + n]
    return outs[0], list(outs[1:])


def _shifted_copies(ext, sh, rows):
    for s in range(1, SUBLANES):
        sh[s - 1] = ext[pl.ds(s, rows), :]


def _window(ext, sh, off, row0, rows, lanes):
    s = off % SUBLANES
    src = ext if s == 0 else sh.at[s - 1]
    return src[pl.ds(off - s + row0, rows), lanes]


def _mixa_fwd(name, pa, cw, cb, lg, lb, seq):
    t, w3 = pa.shape
    cc = w3 // 3
    tr = _tile(seq, 128, HALO)
    per_seq = seq // tr
    hb = tr // HALO
    lead = HALO - (CONV_WIDTH - 1)

    def body(a_ref, b_ref, z_ref, ah_ref, bh_ref, cw_ref, cb_ref, lg_ref, lb_ref, c_ref, m_ref, ext, sh, conv):
        i = pl.program_id(0)
        gh = ah_ref[...].astype(F32) * _sigmoid(bh_ref[...].astype(F32))
        ext[pl.ds(0, HALO), :] = jnp.where((i % per_seq) == 0, 0.0, gh)
        ext[pl.ds(HALO, tr), :] = a_ref[...].astype(F32) * _sigmoid(b_ref[...].astype(F32))
        _shifted_copies(ext, sh, tr + HALO - SUBLANES)
        for lc in range(cc // 128):
            lanes = pl.ds(lc * 128, 128)
            taps = [cw_ref[pl.ds(k, 1), lanes] for k in range(CONV_WIDTH)]
            for row0 in range(0, tr, CONV_ROWS):
                acc = jnp.broadcast_to(cb_ref[:, lanes], (CONV_ROWS, 128))
                for k in range(CONV_WIDTH):
                    acc = acc + _window(ext, sh, lead + k, row0, CONV_ROWS, lanes) * taps[k]
                conv[pl.ds(row0, CONV_ROWS), lanes] = acc
        acc = conv[...]
        c_ref[...] = acc.astype(BF16)
        xc = acc - jnp.mean(acc, axis=-1, keepdims=True)
        nrm = xc * lax.rsqrt(jnp.mean(xc * xc, axis=-1, keepdims=True) + EPS)
        l = nrm * lg_ref[...] + lb_ref[...]
        z = z_ref[...].astype(F32)
        m_ref[...] = (l * _sigmoid(l) * z * _sigmoid(z)).astype(BF16)

    halo = lambda col: pl.BlockSpec((HALO, cc), lambda i: (jnp.maximum(i * hb - 1, 0), col))
    return pl.pallas_call(
        body, name=name, grid=(t // tr,),
        in_specs=[_row_spec(tr, cc, 0), _row_spec(tr, cc, 1), _row_spec(tr, cc, 2), halo(0), halo(1),
                  _full_spec((HALO, cc)), _full_spec((1, cc)), _full_spec((1, cc)), _full_spec((1, cc))],
        out_specs=[_row_spec(tr, cc), _row_spec(tr, cc)],
        out_shape=[jax.ShapeDtypeStruct((t, cc), BF16), jax.ShapeDtypeStruct((t, cc), BF16)],
        scratch_shapes=[pltpu.VMEM((tr + HALO, cc), F32), pltpu.VMEM((SUBLANES - 1, tr + HALO - SUBLANES, cc), F32),
                        pltpu.VMEM((tr, cc), F32)],
        compiler_params=_params("parallel"))(pa, pa, pa, pa, pa, cw, cb, lg, lb)


def _mixa_bwd1(name, c, pa, dm, lg, lb, plan=None):
    t, cc = c.shape
    tr = _tile(t, 128, 8)

    def body(c_ref, z_ref, dm_ref, lg_ref, lb_ref, dc_ref, dz_ref, dlg_ref, dlb_ref, dcb_ref):
        i = pl.program_id(0)
        cv = c_ref[...].astype(F32)
        xc = cv - jnp.mean(cv, axis=-1, keepdims=True)
        rs = lax.rsqrt(jnp.mean(xc * xc, axis=-1, keepdims=True) + EPS)
        nrm = xc * rs
        l = nrm * lg_ref[...] + lb_ref[...]
        z = z_ref[...].astype(F32)
        sl, sz = _sigmoid(l), _sigmoid(z)
        dmv = dm_ref[...].astype(F32)
        ds = dmv * (z * sz)
        dzz = dmv * (l * sl)
        dz_ref[...] = (dzz * (sz * (1.0 + z * (1.0 - sz)))).astype(BF16)
        dl = ds * (sl * (1.0 + l * (1.0 - sl)))
        dn = dl * lg_ref[...]
        dc = rs * (dn - jnp.mean(dn, axis=-1, keepdims=True) - nrm * jnp.mean(dn * nrm, axis=-1, keepdims=True))
        dc_ref[...] = dc.astype(BF16)

        @pl.when(i == 0)
        def _():
            dlg_ref[...] = jnp.zeros_like(dlg_ref)
            dlb_ref[...] = jnp.zeros_like(dlb_ref)
            dcb_ref[...] = jnp.zeros_like(dcb_ref)

        dlg_ref[...] += _colsum(dl * nrm)
        dlb_ref[...] += _colsum(dl)
        dcb_ref[...] += _colsum(dc)

    vec = jax.ShapeDtypeStruct((1, cc), F32)
    outs, carried = _hosted_call(
        body, plan, name=name, grid=(t // tr,),
        in_specs=[_row_spec(tr, cc), _row_spec(tr, cc, 2), _row_spec(tr, cc), _full_spec((1, cc)), _full_spec((1, cc))],
        out_specs=[_row_spec(tr, cc), _row_spec(tr, cc)] + [_full_spec((1, cc))] * 3,
        out_shape=[jax.ShapeDtypeStruct((t, cc), BF16), jax.ShapeDtypeStruct((t, cc), BF16), vec, vec, vec],
        scratch_shapes=[], args=[c, pa, dm, lg, lb], sem=("arbitrary",))
    return outs if plan is None else (outs, carried)


def _mixa_bwd2(name, dc, pa, dz, cw, seq):
    t, cc = dc.shape
    tr = _tile(seq, 128, HALO)
    per_seq = seq // tr
    hb = tr // HALO
    steps = t // tr
    last_halo = t // HALO - 1

    def body(dc_ref, dcn_ref, a_ref, b_ref, dz_ref, cw_ref, dp_ref, dcw_ref, ext, sh, sums):
        i = pl.program_id(0)
        ext[pl.ds(0, tr), :] = dc_ref[...].astype(F32)
        ext[pl.ds(tr, HALO), :] = jnp.where((i % per_seq) == per_seq - 1, 0.0, dcn_ref[...].astype(F32))
        _shifted_copies(ext, sh, tr + HALO - SUBLANES)

        @pl.when(i == 0)
        def _():
            sums[...] = jnp.zeros_like(sums)
            dcw_ref[...] = jnp.zeros_like(dcw_ref)

        av = a_ref[...].astype(F32)
        sb = _sigmoid(b_ref[...].astype(F32))
        glu = av * sb
        dglu = jnp.zeros((tr, cc), F32)
        for k in range(CONV_WIDTH):
            wd = _window(ext, sh, CONV_WIDTH - 1 - k, 0, tr, slice(None))
            dglu = dglu + wd * cw_ref[pl.ds(k, 1), :]
            sums[pl.ds(k * SUBLANES, SUBLANES), :] += (wd * glu).reshape(tr // SUBLANES, SUBLANES, cc).sum(axis=0)
        dp_ref[:, pl.ds(0, cc)] = (dglu * sb).astype(BF16)
        dp_ref[:, pl.ds(cc, cc)] = (dglu * av * sb * (1.0 - sb)).astype(BF16)
        dp_ref[:, pl.ds(2 * cc, cc)] = dz_ref[...]

        @pl.when(i == steps - 1)
        def _():
            for k in range(CONV_WIDTH):
                dcw_ref[pl.ds(k, 1), :] = _colsum(sums[pl.ds(k * SUBLANES, SUBLANES), :])

    nxt = pl.BlockSpec((HALO, cc), lambda i: (jnp.minimum((i + 1) * hb, last_halo), 0))
    return pl.pallas_call(
        body, name=name, grid=(steps,),
        in_specs=[_row_spec(tr, cc), nxt, _row_spec(tr, cc, 0), _row_spec(tr, cc, 1), _row_spec(tr, cc),
                  _full_spec((HALO, cc))],
        out_specs=[_row_spec(tr, 3 * cc), _full_spec((HALO, cc))],
        out_shape=[jax.ShapeDtypeStruct((t, 3 * cc), BF16), jax.ShapeDtypeStruct((HALO, cc), F32)],
        scratch_shapes=[pltpu.VMEM((tr + HALO, cc), F32), pltpu.VMEM((SUBLANES - 1, tr + HALO - SUBLANES, cc), F32),
                        pltpu.VMEM((HALO * SUBLANES, cc), F32)],
        compiler_params=_params("arbitrary"))(dc, dc, pa, pa, dz, cw)


def _ple_fwd(name, h, gpre, pp, gains):
    t, dm = h.shape
    tr = _tile(t, 256, 8)

    n = len(gains)

    def body(h_ref, g_ref, p_ref, *refs):
        o_ref = refs[n]
        xv = h_ref[...] + _sigmoid(g_ref[...].astype(F32)) * p_ref[...].astype(F32)
        o_ref[...] = xv
        xn = xv * lax.rsqrt(jnp.mean(xv * xv, axis=-1, keepdims=True) + EPS)
        for gain_ref, n_ref in zip(refs[:n], refs[n + 1:]):
            n_ref[...] = (xn * gain_ref[...]).astype(BF16)

    return pl.pallas_call(
        body, name=name, grid=(t // tr,), in_specs=[_row_spec(tr, dm)] * 3 + [_full_spec((1, dm))] * n,
        out_specs=[_row_spec(tr, dm)] * (1 + n),
        out_shape=[jax.ShapeDtypeStruct((t, dm), F32)] + [jax.ShapeDtypeStruct((t, dm), BF16)] * n,
        compiler_params=_params("parallel"))(h, gpre, pp, *gains)


def _ple_loss(name, h, gpre, pp, target):
    t, dm = h.shape
    tr = _tile(t, 256, 8)

    def body(h_ref, g_ref, p_ref, t_ref, dy_ref, dg_ref, dp_ref, sq_ref):
        i = pl.program_id(0)
        gpre_v, pp_v = g_ref[...].astype(F32), p_ref[...].astype(F32)
        err = h_ref[...] + _sigmoid(gpre_v) * pp_v - t_ref[...]
        dy = err * (1.0 / dm)
        dy_ref[...] = dy
        dg_ref[...], dp_ref[...] = _ple_bwd_math(dy, gpre_v, pp_v)

        @pl.when(i == 0)
        def _():
            sq_ref[...] = jnp.zeros_like(sq_ref)

        sq_ref[...] += jnp.sum(jnp.sum(err * err, axis=1, keepdims=True), axis=0, keepdims=True)

    return pl.pallas_call(
        body, name=name, grid=(t // tr,), in_specs=[_row_spec(tr, dm)] * 4,
        out_specs=[_row_spec(tr, dm)] * 3 + [_full_spec((1, 1))],
        out_shape=[jax.ShapeDtypeStruct((t, dm), F32)] + [jax.ShapeDtypeStruct((t, dm), BF16)] * 2
        + [jax.ShapeDtypeStruct((1, 1), F32)],
        compiler_params=_params("arbitrary"))(h, gpre, pp, target)


def _rope_tables(seq):
    half = ROPE_DIM // 2
    inv = ROPE_THETA ** (-jnp.arange(half, dtype=F32) * (2.0 / ROPE_DIM))
    ang = jnp.arange(seq).astype(F32)[:, None] * inv[None, :]
    cos, sin = jnp.cos(ang), jnp.sin(ang)
    rest = HEAD_DIM - ROPE_DIM
    one = jnp.ones((seq, rest), F32)
    zero = jnp.zeros((seq, rest), F32)
    zh = jnp.zeros((seq, half), F32)
    tc = jnp.concatenate([cos, cos, one], axis=1)
    ta = jnp.concatenate([-sin, zh, zero], axis=1)
    tb = jnp.concatenate([zh, sin, zero], axis=1)
    return [jnp.tile(tb_, (1, 128 // HEAD_DIM)) for tb_ in (tc, ta, tb)]


def _wide(tab_ref, w):
    return jnp.tile(tab_ref[...], (1, w // 128))


def _hnr_fwd_math(xv, gain, tc_ref, ta_ref, tb_ref):
    width = xv.shape[1]
    rs = lax.rsqrt(_seg_allsum64(xv * xv) * (1.0 / HEAD_DIM) + EPS)
    y = xv * rs * gain
    return (y * _wide(tc_ref, width) + pltpu.roll(y, width - ROPE_DIM // 2, 1) * _wide(ta_ref, width)
            + pltpu.roll(y, ROPE_DIM // 2, 1) * _wide(tb_ref, width))


def _hnr_bwd_math(xv, gain, dout, tc, ta, tb, width):
    dy = dout * tc + pltpu.roll(dout * ta, ROPE_DIM // 2, 1) + pltpu.roll(dout * tb, width - ROPE_DIM // 2, 1)
    rs = lax.rsqrt(_seg_allsum64(xv * xv) * (1.0 / HEAD_DIM) + EPS)
    xn = xv * rs
    dyh = dy * gain
    dx = rs * (dyh - xn * (_seg_allsum64(dyh * xn, terms=2) * (1.0 / HEAD_DIM)))
    return dx, _colsum(dy * xn)


def _q_bwd(name, p1, gain, tables, dqs, dgt, seq):
    t, w4 = p1.shape
    da = w4 // 4
    width = 3 * da
    tr = _tile(seq, 128, 8)
    per_seq = seq // tr

    def body(x_ref, g_ref, tc_ref, ta_ref, tb_ref, d0_ref, d1_ref, d2_ref, dgt_ref, o_ref, dg_ref):
        i = pl.program_id(0)
        dout = jnp.concatenate([d0_ref[...], d1_ref[...], d2_ref[...]], axis=1)
        dx, dg = _hnr_bwd_math(x_ref[...].astype(F32), g_ref[...], dout, _wide(tc_ref, width), _wide(ta_ref, width),
                               _wide(tb_ref, width), width)

        @pl.when(i == 0)
        def _():
            dg_ref[...] = jnp.zeros_like(dg_ref)

        dg_ref[...] += dg
        o_ref[:, pl.ds(0, width)] = dx.astype(BF16)
        o_ref[:, pl.ds(width, da)] = dgt_ref[...]

    tab = pl.BlockSpec((tr, 128), lambda i: (i % per_seq, 0))
    return pl.pallas_call(
        body, name=name, grid=(t // tr,),
        in_specs=[_row_spec(tr, width), _full_spec((1, width)), tab, tab, tab] + [_row_spec(tr, da)] * 4,
        out_specs=[_row_spec(tr, w4), _full_spec((1, width))],
        out_shape=[jax.ShapeDtypeStruct((t, w4), BF16), jax.ShapeDtypeStruct((1, width), F32)],
        compiler_params=_params("arbitrary"))(p1, gain, *tables, *dqs, dgt)


def _k_bwd(name, kv, gain, tables, dk, dv, seq):
    t, w2 = kv.shape
    da = w2 // 2
    tr = _tile(seq, 256, 8)
    per_seq = seq // tr

    def body(x_ref, g_ref, tc_ref, ta_ref, tb_ref, dk_ref, dv_ref, o_ref, dg_ref):
        i = pl.program_id(0)
        dx, dg = _hnr_bwd_math(x_ref[...], g_ref[...], dk_ref[...], _wide(tc_ref, da), _wide(ta_ref, da),
                               _wide(tb_ref, da), da)

        @pl.when(i == 0)
        def _():
            dg_ref[...] = jnp.zeros_like(dg_ref)

        dg_ref[...] += dg
        o_ref[:, pl.ds(0, da)] = dx.astype(BF16)
        o_ref[:, pl.ds(da, da)] = dv_ref[...].astype(BF16)

    tab = pl.BlockSpec((tr, 128), lambda i: (i % per_seq, 0))
    return pl.pallas_call(
        body, name=name, grid=(t // tr,),
        in_specs=[_row_spec(tr, da), _full_spec((1, da)), tab, tab, tab] + [_row_spec(tr, da)] * 2,
        out_specs=[_row_spec(tr, w2), _full_spec((1, da))],
        out_shape=[jax.ShapeDtypeStruct((t, w2), BF16), jax.ShapeDtypeStruct((1, da), F32)],
        compiler_params=_params("arbitrary"))(kv, gain, *tables, dk, dv)


def _unit_index(dil, r, blk):
    if dil == 1:
        start = blk * SPAN
        return pl.ds(start if isinstance(start, int) else pl.multiple_of(start, SPAN), SPAN)
    return pl.ds(r + dil * SPAN * blk, SPAN, stride=dil)


def _unit_rows(ref, dil, r, blk):
    return ref[_unit_index(dil, r, blk), :]


def _store_rows(ref, dil, r, blk, val):
    ref[_unit_index(dil, r, blk), :] = val


def _over_units(dil, nblk, unit, carry0, after=None):
    for r in range(dil):
        carry = carry0
        for blk in range(nblk):
            carry = unit(r, blk, blk > 0, carry)
        if after is not None:
            after(r, carry)


def _band_mask(with_prev, step=0, qrows=SPAN):
    nk = SPAN + qrows if with_prev else (step + 1) * qrows
    qi = lax.broadcasted_iota(jnp.int32, (qrows, nk), 0)
    kj = lax.broadcasted_iota(jnp.int32, (qrows, nk), 1)
    if with_prev:
        return (kj >= qi) & (kj <= qi + SPAN)
    return kj <= qi + step * qrows


def _step_keys(prev, cur, with_prev, step=0, qrows=SPAN):
    if with_prev:
        return jnp.concatenate([prev[step * qrows:], cur[:(step + 1) * qrows]], axis=0)
    return cur[:(step + 1) * qrows]


_NT = (((1,), (1,)), ((), ()))
_TN = (((0,), (0,)), ((), ()))
N_PAIR = 128 // HEAD_DIM
FWD_QROWS = SPAN


def _per_head(x):
    head = lax.shift_right_logical(lax.broadcasted_iota(jnp.int32, x.shape, 1), HEAD_DIM.bit_length() - 1)
    return [jnp.where(head == hh, x, 0.0).astype(BF16) for hh in range(N_PAIR)]


def _head_columns(stat):
    return [stat[:, hh * HEAD_DIM:hh * HEAD_DIM + 1] for hh in range(N_PAIR)]


def _by_head(cols):
    head = lax.shift_right_logical(lax.broadcasted_iota(jnp.int32, (cols[0].shape[0], 128), 1), HEAD_DIM.bit_length() - 1)
    out = cols[-1]
    for hh in range(N_PAIR - 2, -1, -1):
        out = jnp.where(head == hh, cols[hh], out)
    return jnp.broadcast_to(out, (cols[0].shape[0], 128))


def _group_fwd(q_ref, k_ref, v_ref, o_ref, l_ref, dil, seq):
    nblk = seq // (dil * SPAN)
    scale = HEAD_DIM ** -0.5

    def unit(r, blk, with_prev, carry):
        q = (_unit_rows(q_ref, dil, r, blk) * scale).astype(BF16)
        kc = _per_head(_unit_rows(k_ref, dil, r, blk))
        vc = _per_head(_unit_rows(v_ref, dil, r, blk))
        kp, vp = carry if with_prev else ([None] * N_PAIR,) * 2
        o_rows, l_rows = [], []
        for step in range(SPAN // FWD_QROWS):
            qs = q[step * FWD_QROWS:(step + 1) * FWD_QROWS]
            mask = _band_mask(with_prev, step, FWD_QROWS)
            out = None
            lses, rdens = [], []
            for hh in range(N_PAIR):
                kh = _step_keys(kp[hh], kc[hh], with_prev, step, FWD_QROWS)
                vh = _step_keys(vp[hh], vc[hh], with_prev, step, FWD_QROWS)
                s = lax.dot_general(qs, kh, _NT, preferred_element_type=F32)
                s = jnp.where(mask, s, NEG_INF)
                mx = jnp.max(s, axis=-1, keepdims=True)
                p = jnp.exp(s - mx)
                den = jnp.sum(p, axis=-1, keepdims=True)
                part = jnp.dot(p.astype(BF16), vh, preferred_element_type=F32)
                out = part if out is None else out + part
                lses.append(mx + jnp.log(den))
                rdens.append(1.0 / den)
            o_rows.append(out * _by_head(rdens))
            l_rows.append(_by_head(lses))
        _store_rows(o_ref, dil, r, blk, jnp.concatenate(o_rows, axis=0))
        _store_rows(l_ref, dil, r, blk, jnp.concatenate(l_rows, axis=0))
        return kc, vc

    _over_units(dil, nblk, unit, None)


def _attn_fwd(name, qn, kn, kv, p1, seq):
    t, da = kn.shape
    hp = da // 128
    ng = len(DILATIONS)

    def body(q0_ref, q1_ref, q2_ref, k_ref, v_ref, g_ref, o_ref, l_ref, m_ref, og, lg):
        for g, q_ref in enumerate((q0_ref, q1_ref, q2_ref)):
            _group_fwd(q_ref, k_ref, v_ref, og.at[g], lg.at[g], DILATIONS[g], seq)
        a0, a1, a2 = lg[0], lg[1], lg[2]
        mx = jnp.maximum(jnp.maximum(a0, a1), a2)
        e0, e1, e2 = jnp.exp(a0 - mx), jnp.exp(a1 - mx), jnp.exp(a2 - mx)
        den = e0 + e1 + e2
        o = (e0 * og[0] + e1 * og[1] + e2 * og[2]) / den
        o_ref[...] = o
        l_ref[...] = mx + jnp.log(den)
        gt = g_ref[...].astype(F32)
        m_ref[...] = (o * gt * _sigmoid(gt)).astype(BF16)

    blk_spec = lambda off: pl.BlockSpec((seq, 128), lambda b, h: (b, off + h))
    return pl.pallas_call(
        body, name=name, grid=(t // seq, hp),
        in_specs=[blk_spec(0), blk_spec(hp), blk_spec(2 * hp), blk_spec(0), blk_spec(hp), blk_spec(3 * hp)],
        out_specs=[blk_spec(0)] * 3,
        out_shape=[jax.ShapeDtypeStruct((t, da), F32)] * 2 + [jax.ShapeDtypeStruct((t, da), BF16)],
        scratch_shapes=[pltpu.VMEM((ng, seq, 128), F32), pltpu.VMEM((ng, seq, 128), F32)],
        compiler_params=_params("parallel", "parallel"))(qn, qn, qn, kn, kv, p1)


def _group_bwd(q_ref, k_ref, v_ref, do_ref, l_ref, d_ref, dq_ref, dk_ref, dv_ref, dil, seq, first):
    nblk = seq // (dil * SPAN)
    scale = HEAD_DIM ** -0.5

    def put(ref, r, blk, val):
        if not first:
            val = val + _unit_rows(ref, dil, r, blk)
        _store_rows(ref, dil, r, blk, val)

    def unit(r, blk, with_prev, carry):
        kp, vp, pend_k, pend_v = carry if with_prev else (None,) * 4
        q = _unit_rows(q_ref, dil, r, blk) * scale
        kc = _per_head(_unit_rows(k_ref, dil, r, blk))
        vc = _per_head(_unit_rows(v_ref, dil, r, blk))
        dov = _unit_rows(do_ref, dil, r, blk)
        lcols = _head_columns(_unit_rows(l_ref, dil, r, blk))
        dcols = _head_columns(_unit_rows(d_ref, dil, r, blk))
        mask = _band_mask(with_prev)
        qb, dob = q.astype(BF16), dov.astype(BF16)
        dq = dkcat = dvcat = None
        for hh, (qh, doh) in enumerate(zip(_per_head(q), _per_head(dov))):
            kh = jnp.concatenate([kp[hh], kc[hh]], axis=0) if with_prev else kc[hh]
            vh = jnp.concatenate([vp[hh], vc[hh]], axis=0) if with_prev else vc[hh]
            s = lax.dot_general(qb, kh, _NT, preferred_element_type=F32)
            p = jnp.where(mask, jnp.exp(s - lcols[hh]), 0.0)
            dp = lax.dot_general(dob, vh, _NT, preferred_element_type=F32)
            ds = (p * (dp - dcols[hh])).astype(BF16)
            parts = (jnp.dot(ds, kh, preferred_element_type=F32),
                     lax.dot_general(ds, qh, _TN, preferred_element_type=F32),
                     lax.dot_general(p.astype(BF16), doh, _TN, preferred_element_type=F32))
            dq, dkcat, dvcat = parts if dq is None else (dq + parts[0], dkcat + parts[1], dvcat + parts[2])
        _store_rows(dq_ref, dil, r, blk, dq * scale)
        if with_prev:
            put(dk_ref, r, blk - 1, pend_k + dkcat[:SPAN])
            put(dv_ref, r, blk - 1, pend_v + dvcat[:SPAN])
            return kc, vc, dkcat[SPAN:], dvcat[SPAN:]
        return kc, vc, dkcat, dvcat

    def after(r, carry):
        put(dk_ref, r, nblk - 1, carry[2])
        put(dv_ref, r, nblk - 1, carry[3])

    _over_units(dil, nblk, unit, None, after)


def _attn_bwd(name, qn, kn, kv, do, lse, dsum, seq):
    t, da = kn.shape
    hp = da // 128

    def body(q0_ref, q1_ref, q2_ref, k_ref, v_ref, do_ref, l_ref, d_ref, dq0_ref, dq1_ref, dq2_ref, dk_ref, dv_ref):
        groups = ((q0_ref, dq0_ref), (q1_ref, dq1_ref), (q2_ref, dq2_ref))
        for g, (q_ref, dq_ref) in enumerate(groups):
            _group_bwd(q_ref, k_ref, v_ref, do_ref, l_ref, d_ref, dq_ref, dk_ref, dv_ref, DILATIONS[g], seq, g == 0)

    blk_spec = lambda off: pl.BlockSpec((seq, 128), lambda b, h: (b, off + h))
    return pl.pallas_call(
        body, name=name, grid=(t // seq, hp),
        in_specs=[blk_spec(0), blk_spec(hp), blk_spec(2 * hp), blk_spec(0), blk_spec(hp), blk_spec(0), blk_spec(0), blk_spec(0)],
        out_specs=[blk_spec(0)] * 5,
        out_shape=[jax.ShapeDtypeStruct((t, da), F32)] * 5,
        compiler_params=_params("parallel", "parallel"))(qn, qn, qn, kn, kv, do, lse, dsum)


def _gate_bwd(name, dm, o, p1):
    t, da = o.shape
    tr = _tile(t, 256, 8)

    def body(dm_ref, o_ref, g_ref, do_ref, dg_ref, ds_ref):
        g = g_ref[...].astype(F32)
        sg = _sigmoid(g)
        dmv, ov = dm_ref[...].astype(F32), o_ref[...]
        do = dmv * (g * sg)
        do_ref[...] = do
        dg_ref[...] = (dmv * ov * (sg * (1.0 + g * (1.0 - sg)))).astype(BF16)
        ds_ref[...] = _seg_allsum64(do * ov, terms=2)

    return pl.pallas_call(
        body, name=name, grid=(t // tr,),
        in_specs=[_row_spec(tr, da), _row_spec(tr, da), _row_spec(tr, da, 3)],
        out_specs=[_row_spec(tr, da)] * 3,
        out_shape=[jax.ShapeDtypeStruct((t, da), F32), jax.ShapeDtypeStruct((t, da), BF16), jax.ShapeDtypeStruct((t, da), F32)],
        compiler_params=_params("parallel"))(dm, o, p1)


def _cast_bf16(name, w2d, chip):
    r, c = w2d.shape
    tr = _tile(r, 256, 16)

    def body(chip_ref, x_ref, o_ref):
        o_ref[...] = x_ref[...].astype(BF16)

    grid_spec = pltpu.PrefetchScalarGridSpec(
        num_scalar_prefetch=1, grid=(r // tr,),
        in_specs=[pl.BlockSpec((tr, c), lambda i, m: (i, 0))],
        out_specs=pl.BlockSpec((None, tr, c), lambda i, m: (m[0], i, 0)))
    return pl.pallas_call(
        body, name=name, grid_spec=grid_spec, out_shape=jax.ShapeDtypeStruct((N_CHIPS, r, c), BF16),
        compiler_params=_params("parallel"))(chip, w2d)


def _adamw(name, w, g, m, v):
    r, c = w.shape
    tr = _tile(r, 256, 8)
    c1 = 1.0 - ADAM_B1 ** ADAM_STEP
    c2 = 1.0 - ADAM_B2 ** ADAM_STEP

    def body(w_ref, g_ref, m_ref, v_ref, d_ref, nm_ref, nv_ref):
        gv = g_ref[...]
        nm = ADAM_B1 * m_ref[...] + (1.0 - ADAM_B1) * gv
        nv = ADAM_B2 * v_ref[...] + (1.0 - ADAM_B2) * (gv * gv)
        nm_ref[...] = nm
        nv_ref[...] = nv
        d_ref[...] = -ADAM_LR * ((nm / c1) / (jnp.sqrt(nv / c2) + ADAM_EPS) + ADAM_WD * w_ref[...])

    sds = jax.ShapeDtypeStruct((r, c), F32)
    return pl.pallas_call(
        body, name=name, grid=(r // tr,), in_specs=[_row_spec(tr, c)] * 4, out_specs=[_row_spec(tr, c)] * 3,
        out_shape=[sds] * 3, compiler_params=_params("parallel"))(w, g, m, v)


def _pair_sum(name, gd, recv, core):
    _, r, c = gd.shape
    rh = r // 2
    tr = _tile(rh, 256, 8)
    nrt = rh // tr

    def body(c_ref, a_ref, b_ref, o_ref):
        o_ref[...] = (a_ref[...].astype(F32) + b_ref[...].astype(F32)).astype(BF16)

    grid_spec = pltpu.PrefetchScalarGridSpec(
        num_scalar_prefetch=1, grid=(N_CHIPS, nrt),
        in_specs=[pl.BlockSpec((None, tr, c), lambda j, i, cr: (j, cr[0] * nrt + i, 0)),
                  pl.BlockSpec((None, tr, c), lambda j, i, cr: (j, i, 0))],
        out_specs=pl.BlockSpec((None, tr, c), lambda j, i, cr: (j, i, 0)))
    return pl.pallas_call(
        body, name=name, grid_spec=grid_spec, out_shape=jax.ShapeDtypeStruct((N_CHIPS, rh, c), BF16),
        compiler_params=_params("parallel", "parallel"))(core, gd, recv)


def _chip_sum(name, sums, parts, order):
    _, rh, c = parts.shape
    tr = _tile(rh, 256, 16)
    nrt = rh // tr

    def body(o_ref_, s_ref, p1_ref, p2_ref, p3_ref, o_ref):
        acc = s_ref[...].astype(F32)
        for p_ref in (p1_ref, p2_ref, p3_ref):
            acc = acc + p_ref[...].astype(F32)
        o_ref[...] = acc

    slot = lambda k: pl.BlockSpec((None, tr, c), lambda i, o: (o[k], i, 0))
    grid_spec = pltpu.PrefetchScalarGridSpec(
        num_scalar_prefetch=1, grid=(nrt,),
        in_specs=[slot(0), slot(1), slot(2), slot(3)],
        out_specs=pl.BlockSpec((tr, c), lambda i, o: (o[N_CHIPS] * nrt + i, 0)))
    return pl.pallas_call(
        body, name=name, grid_spec=grid_spec, out_shape=jax.ShapeDtypeStruct((2 * rh, c), F32),
        compiler_params=_params("parallel"))(order, sums, parts, parts, parts)


HBM = pl.BlockSpec(memory_space=pl.ANY)


def _place():
    x, y, c = lax.axis_index("x"), lax.axis_index("y"), lax.axis_index("c")
    chips = [(1 - x, y), (x, 1 - y), (1 - x, 1 - y)]
    return x, y, c, chips


def _half(ref, hc):
    rows = ref.shape[0] // 2
    return ref.at[pl.ds(hc * rows, rows)]


class _Plan:
    def __init__(self, ins, out_shapes, aliases, n_sems, start, finish):
        self.ins, self.out_shapes, self.aliases, self.n_sems = list(ins), list(out_shapes), dict(aliases), n_sems
        self.start, self.finish = start, finish


def _gather_plan(shards):
    n = len(shards)

    def ici(place, outs, send_sems, recv_sems, i, k, slot):
        x, y, c, chips = place
        half = _half(outs[i].at[slot], c)
        return pltpu.make_async_remote_copy(
            src_ref=half, dst_ref=half, send_sem=send_sems.at[6 * i + k], recv_sem=recv_sems.at[6 * i + k],
            device_id=(chips[k][0], chips[k][1], c), device_id_type=MESH)

    def d2d(place, outs, send_sems, recv_sems, i, k, slot, hc):
        x, y, c, chips = place
        half = _half(outs[i].at[slot], hc)
        return pltpu.make_async_remote_copy(
            src_ref=half, dst_ref=half, send_sem=send_sems.at[6 * i + 3 + k], recv_sem=recv_sems.at[6 * i + 3 + k],
            device_id=(x, y, 1 - c), device_id_type=MESH)

    def start(place, ins, outs, send_sems, recv_sems):
        x, y, c, chips = place
        for i in range(n):
            for k in range(3):
                ici(place, outs, send_sems, recv_sems, i, k, 2 * x + y).start()

    def finish(place, ins, outs, send_sems, recv_sems):
        x, y, c, chips = place
        theirs = [2 * chip[0] + chip[1] for chip in chips]
        for i in range(n):
            for k in range(3):
                ici(place, outs, send_sems, recv_sems, i, k, theirs[k]).wait_recv()
                d2d(place, outs, send_sems, recv_sems, i, k, theirs[k], c).start()
        for i in range(n):
            for k in range(3):
                d2d(place, outs, send_sems, recv_sems, i, k, theirs[k], 1 - c).wait_recv()
        for i in range(n):
            for k in range(3):
                ici(place, outs, send_sems, recv_sems, i, k, 2 * x + y).wait_send()
                d2d(place, outs, send_sems, recv_sems, i, k, theirs[k], c).wait_send()

    return _Plan(shards, [jax.ShapeDtypeStruct(s.shape, s.dtype) for s in shards], {i: i for i in range(n)}, 6 * n,
                 start, finish)


def _scatter_plan(sums):
    n = len(sums)

    def copy(place, ins, outs, send_sems, recv_sems, i, k, src_slot, dst_slot):
        x, y, c, chips = place
        return pltpu.make_async_remote_copy(
            src_ref=ins[i].at[src_slot], dst_ref=outs[i].at[dst_slot],
            send_sem=send_sems.at[3 * i + k], recv_sem=recv_sems.at[3 * i + k],
            device_id=(chips[k][0], chips[k][1], c), device_id_type=MESH)

    def start(place, ins, outs, send_sems, recv_sems):
        x, y, c, chips = place
        for i in range(n):
            for k, chip in enumerate(chips):
                copy(place, ins, outs, send_sems, recv_sems, i, k, 2 * chip[0] + chip[1], 2 * x + y).start()

    def finish(place, ins, outs, send_sems, recv_sems):
        x, y, c, chips = place
        for i in range(n):
            for k, chip in enumerate(chips):
                theirs = 2 * chip[0] + chip[1]
                copy(place, ins, outs, send_sems, recv_sems, i, k, theirs, 2 * x + y).wait_send()
                copy(place, ins, outs, send_sems, recv_sems, i, k, theirs, theirs).wait_recv()

    return _Plan(sums, [jax.ShapeDtypeStruct(s.shape, s.dtype) for s in sums], {}, 3 * n, start, finish)


def _hosted_call(body, plan, *, name, grid, in_specs, out_specs, out_shape, scratch_shapes, args, sem):
    in_specs, out_specs, out_shape, scratch_shapes = list(in_specs), list(out_specs), list(out_shape), list(scratch_shapes)
    if plan is None:
        res = pl.pallas_call(body, name=name, grid=grid, in_specs=in_specs, out_specs=out_specs, out_shape=out_shape,
                             scratch_shapes=scratch_shapes, compiler_params=_params(*sem))(*args)
        return list(res), []
    n_in, n_out, n_scr = len(in_specs), len(out_specs), len(scratch_shapes)
    p_in, p_out = len(plan.ins), len(plan.out_shapes)

    def hosted(*refs):
        refs = list(refs)
        ins, pins = refs[:n_in], refs[n_in:n_in + p_in]
        outs = refs[n_in + p_in:n_in + p_in + n_out]
        pouts = refs[n_in + p_in + n_out:n_in + p_in + n_out + p_out]
        scr = refs[n_in + p_in + n_out + p_out:n_in + p_in + n_out + p_out + n_scr]
        send_sems, recv_sems = refs[-2:]
        place = _place()
        ids = [pl.program_id(d) for d in range(len(grid))]
        first = functools.reduce(jnp.logical_and, [i == 0 for i in ids])
        last = functools.reduce(jnp.logical_and, [i == g - 1 for i, g in zip(ids, grid)])

        @pl.when(first)
        def _():
            plan.start(place, pins, pouts, send_sems, recv_sems)

        body(*ins, *outs, *scr)

        @pl.when(last)
        def _():
            plan.finish(place, pins, pouts, send_sems, recv_sems)

    res = pl.pallas_call(
        hosted, name=name, grid=grid, in_specs=in_specs + [HBM] * p_in, out_specs=out_specs + [HBM] * p_out,
        out_shape=out_shape + plan.out_shapes,
        input_output_aliases={n_in + i: n_out + o for i, o in plan.aliases.items()},
        scratch_shapes=scratch_shapes + [pltpu.SemaphoreType.DMA((plan.n_sems,)), pltpu.SemaphoreType.DMA((plan.n_sems,))],
        compiler_params=_params(*(("arbitrary",) * len(grid))))(*args, *plan.ins)
    return list(res[:n_out]), list(res[n_out:])


def _run_plan(name, plan):
    p_in = len(plan.ins)

    def body(*refs):
        ins, outs = refs[:p_in], refs[p_in:p_in + len(plan.out_shapes)]
        send_sems, recv_sems = refs[-2:]
        place = _place()
        plan.start(place, ins, outs, send_sems, recv_sems)
        plan.finish(place, ins, outs, send_sems, recv_sems)

    return pl.pallas_call(
        body, name=name, in_specs=[HBM] * p_in, out_specs=[HBM] * len(plan.out_shapes), out_shape=plan.out_shapes,
        input_output_aliases=plan.aliases,
        scratch_shapes=[pltpu.SemaphoreType.DMA((plan.n_sems,)), pltpu.SemaphoreType.DMA((plan.n_sems,))],
        )(*plan.ins)


def _pair_exchange(name, grads):
    n = len(grads)

    def body(*refs):
        ins, outs = refs[:n], refs[n:2 * n]
        send_sems, recv_sems = refs[2 * n:]
        x, y, c, _ = _place()
        cps = []
        for i in range(n):
            rows = ins[i].shape[1] // 2
            cp = pltpu.make_async_remote_copy(
                src_ref=ins[i].at[:, pl.ds((1 - c) * rows, rows), :], dst_ref=outs[i],
                send_sem=send_sems.at[i], recv_sem=recv_sems.at[i], device_id=(x, y, 1 - c), device_id_type=MESH)
            cp.start()
            cps.append(cp)
        for cp in cps:
            cp.wait()

    return pl.pallas_call(
        body, name=name, in_specs=[HBM] * n, out_specs=[HBM] * n,
        out_shape=[jax.ShapeDtypeStruct((N_CHIPS, g.shape[1] // 2, g.shape[2]), g.dtype) for g in grads],
        scratch_shapes=[pltpu.SemaphoreType.DMA((n,)), pltpu.SemaphoreType.DMA((n,))],
        )(*grads)


def _sibling_join(grads):
    n = len(grads)

    def body(*refs):
        outs = refs[n:2 * n]
        send_sems, recv_sems = refs[2 * n:]
        x, y, c, _ = _place()
        cps = []
        for i in range(n):
            cp = pltpu.make_async_remote_copy(
                src_ref=_half(outs[i], c), dst_ref=_half(outs[i], c), send_sem=send_sems.at[i], recv_sem=recv_sems.at[i],
                device_id=(x, y, 1 - c), device_id_type=MESH)
            cp.start()
            cps.append(cp)
        for i, cp in enumerate(cps):
            cp.wait_send()
            pltpu.make_async_remote_copy(
                src_ref=_half(outs[i], 1 - c), dst_ref=_half(outs[i], 1 - c), send_sem=send_sems.at[i],
                recv_sem=recv_sems.at[i], device_id=(x, y, 1 - c), device_id_type=MESH).wait_recv()

    return pl.pallas_call(
        body, name="sibling_join", in_specs=[HBM] * n, out_specs=[HBM] * n,
        out_shape=[jax.ShapeDtypeStruct(g.shape, g.dtype) for g in grads],
        input_output_aliases={i: i for i in range(n)},
        scratch_shapes=[pltpu.SemaphoreType.DMA((n,)), pltpu.SemaphoreType.DMA((n,))],
        )(*grads)


def _gather8(name, block, reduce):
    m, n = block.shape

    def body(x_ref, out_ref, *scratch):
        if reduce:
            all_ref, send_sems, recv_sems, local_sem = scratch
        else:
            all_ref = out_ref
            send_sems, recv_sems, local_sem = scratch
        x, y, c, chips = _place()
        me, sibling = (x, y, c), (x, y, 1 - c)

        def rows(px, py, pc):
            return all_ref.at[pl.ds((4 * px + 2 * py + pc) * m, m), :]

        def copy(k, blk, to, src=None):
            return pltpu.make_async_remote_copy(
                src_ref=rows(*blk) if src is None else src, dst_ref=rows(*blk),
                send_sem=send_sems.at[k], recv_sem=recv_sems.at[k], device_id=to, device_id_type=MESH)

        mine = pltpu.make_async_copy(x_ref, rows(*me), local_sem)
        mine.start()
        first = [copy(0, me, sibling, src=x_ref)]
        first += [copy(1 + j, me, (chip[0], chip[1], c), src=x_ref) for j, chip in enumerate(chips)]
        for cp in first:
            cp.start()
        passed = [copy(4 + j, (chip[0], chip[1], c), sibling) for j, chip in enumerate(chips)]
        for j, chip in enumerate(chips):
            copy(1 + j, (chip[0], chip[1], c), me).wait_recv()
            passed[j].start()
        copy(0, sibling, me).wait_recv()
        for j, chip in enumerate(chips):
            copy(4 + j, (chip[0], chip[1], 1 - c), me).wait_recv()
        for cp in first + passed:
            cp.wait_send()
        mine.wait()
        if reduce:
            acc = all_ref[pl.ds(0, m), :]
            for d in range(1, 8):
                acc = acc + all_ref[pl.ds(d * m, m), :]
            out_ref[...] = acc

    sems = [pltpu.SemaphoreType.DMA((7,)), pltpu.SemaphoreType.DMA((7,)), pltpu.SemaphoreType.DMA]
    scratch = ([pltpu.VMEM((8 * m, n), F32)] if reduce else []) + sems
    return pl.pallas_call(
        body, name=name,
        out_shape=jax.ShapeDtypeStruct((m, n) if reduce else (8 * m, n), F32),
        in_specs=[pl.BlockSpec(memory_space=pltpu.VMEM)], out_specs=pl.BlockSpec(memory_space=pltpu.VMEM),
        scratch_shapes=scratch)(block)


def _pad_rows(a, rows):
    return jnp.concatenate([a, jnp.zeros((rows - a.shape[0], a.shape[1]), a.dtype)], axis=0)


def kernel(x, p, norm_g, w_in_a, conv_w, conv_b, ln_g, ln_b, w_out_a, kv_norm_g, w_kv, k_norm_g, w_in_b, q_norm_g, w_out_b, ple_norm_g, w_ple_gate, w_ple_proj, loss_target, m_norm_g, m_w_in_a, m_conv_w, m_conv_b, m_ln_g, m_ln_b, m_w_out_a, m_kv_norm_g, m_w_kv, m_k_norm_g, m_w_in_b, m_q_norm_g, m_w_out_b, m_ple_norm_g, m_w_ple_gate, m_w_ple_proj, v_norm_g, v_w_in_a, v_conv_w, v_conv_b, v_ln_g, v_ln_b, v_w_out_a, v_kv_norm_g, v_w_kv, v_k_norm_g, v_w_in_b, v_q_norm_g, v_w_out_b, v_ple_norm_g, v_w_ple_gate, v_w_ple_proj):
    nb, seq, dm = x.shape
    t = nb * seq
    ple = p.shape[-1]
    ccs = conv_w.shape[-1]
    cc = N_CHIPS * ccs
    da = dm
    nheads = da // HEAD_DIM
    assert seq == DILATIONS[-1] * SPAN and da % 128 == 0 and ccs % 128 == 0

    core = lax.axis_index("c").astype(jnp.int32).reshape(1)
    chip = (2 * lax.axis_index("x") + lax.axis_index("y")).astype(jnp.int32)
    chip1 = chip.reshape(1)
    sum_order = jnp.concatenate([(chip1 + k) % N_CHIPS for k in range(N_CHIPS)] + [core])

    x2 = x.reshape(t, dm)
    tgt2 = loss_target.reshape(t, dm)
    p0 = p[0].reshape(t, ple)
    p1 = p[1].reshape(t, ple)

    big = [
        ("w_in_a", w_in_a[0], "col"), ("w_out_a", w_out_a[0], "row"), ("w_kv", w_kv, "col"),
        ("w_in_b", w_in_b[0], "col"), ("w_out_b", w_out_b[0], "row"),
        ("w_ple_gate0", w_ple_gate[0], "row"), ("w_ple_gate1", w_ple_gate[1], "row"),
        ("w_ple_proj0", w_ple_proj[0], "col"), ("w_ple_proj1", w_ple_proj[1], "col"),
    ]
    shard_shape = {nm: w.shape for nm, w, _ in big}
    names = [nm for nm, _, _ in big]
    own = [_cast_bf16("cast_" + nm, w, chip1) for nm, w, _ in big]
    W = {}

    vec_rows = 40
    small = _pad_rows(jnp.concatenate([conv_w[0], conv_b, ln_g, ln_b], axis=0), vec_rows)
    allv = _gather8("gather_conv_vectors", small, reduce=False).reshape(N_CHIPS, 2, vec_rows, ccs)[:, 0]
    allv = allv.transpose(1, 0, 2).reshape(vec_rows, cc)
    cw_full, cb_full, lg_full, lb_full = allv[:HALO], allv[31:32], allv[32:33], allv[33:34]
    cw_full = cw_full * (lax.broadcasted_iota(jnp.int32, (HALO, 1), 0) < CONV_WIDTH).astype(F32)
    tables = _rope_tables(seq)
    gain_q = jnp.tile(q_norm_g[0][:, None, :], (1, nheads, 1)).reshape(1, 3 * da)
    gain_k = jnp.tile(k_norm_g[None, :], (1, nheads))
    g0, g1 = norm_g[0:1], norm_g[1:2]
    pg0, pg1 = ple_norm_g[0:1], ple_norm_g[1:2]
    kvg = kv_norm_g[None, :]

    (u0,), (W[names[0]],) = _rms_fwd("rms_u0", x2, [g0], plan=_gather_plan(own[:1]))
    pa, gathered = _mm_nn("mm_in_a", u0, W["w_in_a"], "col", out_dtype=BF16, plan=_gather_plan(own[1:]))
    W.update(zip(names[1:], gathered))
    for nm in ("w_ple_proj0", "w_ple_proj1"):
        W[nm] = W[nm].transpose(1, 0, 2).reshape(1, W[nm].shape[1], -1)
    conv_out, m_a = _mixa_fwd("mixa_fwd", pa, cw_full, cb_full, lg_full, lb_full, seq)
    h0, r0 = _mm_nn("mm_out_a", m_a, W["w_out_a"], "row", resid=x2, norm_gain=pg0)
    gpre0 = _mm_nn("mm_gate0", r0, W["w_ple_gate0"], "row", out_dtype=BF16)
    pp0 = _mm_nn("mm_proj0", p0, W["w_ple_proj0"], "col", out_dtype=BF16)
    x1, kvn, u1 = _ple_fwd("ple_fwd0", h0, gpre0, pp0, [kvg, g1])
    kv, kn = _mm_nn("mm_kv", kvn, W["w_kv"], "col", heads=(gain_k, tables, da, seq))
    pb, qn = _mm_nn("mm_in_b", u1, W["w_in_b"], "col", out_dtype=BF16, heads=(gain_q, tables, 3 * da, seq))
    o, lse, m_b = _attn_fwd("attn_fwd", qn, kn, kv, pb, seq)
    h1, r1 = _mm_nn("mm_out_b", m_b, W["w_out_b"], "row", resid=x1, norm_gain=pg1)
    gpre1 = _mm_nn("mm_gate1", r1, W["w_ple_gate1"], "row", out_dtype=BF16)
    pp1 = _mm_nn("mm_proj1", p1, W["w_ple_proj1"], "col", out_dtype=BF16)
    dy, dgp1, dpp1, sq = _ple_loss("ple_loss", h1, gpre1, pp1, tgt2)
    loss = lax.psum(0.5 * sq[0, 0] / dm, ("x", "y", "c"))

    G = {}
    G["w_ple_gate1"] = _mm_tn("tn_gate1", r1, dgp1, "row", shard_shape["w_ple_gate1"])
    G["w_ple_proj1"] = _mm_tn("tn_proj1", p1, dpp1, "col", shard_shape["w_ple_proj1"], whole=True)
    dr1 = _mm_nt("nt_gate1", dgp1, W["w_ple_gate1"], "row", out_dtype=BF16)
    dh1, (dpg1,) = _rms_bwd("rms_bwd_r1", h1, dy, [(pg1, dr1)])
    G["w_out_b"] = _mm_tn("tn_out_b", m_b, dh1, "row", shard_shape["w_out_b"])
    dm_b = _mm_nt("nt_out_b", dh1, W["w_out_b"], "row", out_dtype=BF16)
    d_o, dgt, dsum = _gate_bwd("gate_bwd", dm_b, o, pb)
    dq0, dq1, dq2, dk, dv = _attn_bwd("attn_bwd", qn, kn, kv, d_o, lse, dsum, seq)
    dpb, dgq = _q_bwd("q_bwd", pb, gain_q, tables, [dq0, dq1, dq2], dgt, seq)
    dkv, dgk = _k_bwd("k_bwd", kv, gain_k, tables, dk, dv, seq)
    G["w_in_b"] = _mm_tn("tn_in_b", u1, dpb, "col", shard_shape["w_in_b"])
    du1 = _mm_nt("nt_in_b", dpb, W["w_in_b"], "col", out_dtype=BF16)
    G["w_kv"] = _mm_tn("tn_kv", kvn, dkv, "col", shard_shape["w_kv"])
    dkvn = _mm_nt("nt_kv", dkv, W["w_kv"], "col", out_dtype=BF16)
    dx1, (dg1, dkvg), dgp0, dpp0 = _rms_bwd("rms_bwd_x1", x1, dh1, [(g1, du1), (kvg, dkvn)], ple=(gpre0, pp0))
    G["w_ple_gate0"] = _mm_tn("tn_gate0", r0, dgp0, "row", shard_shape["w_ple_gate0"])
    G["w_ple_proj0"] = _mm_tn("tn_proj0", p0, dpp0, "col", shard_shape["w_ple_proj0"], whole=True)
    dr0 = _mm_nt("nt_gate0", dgp0, W["w_ple_gate0"], "row", out_dtype=BF16)
    dh0, (dpg0,) = _rms_bwd("rms_bwd_r0", h0, dx1, [(pg0, dr0)])
    G["w_out_a"] = _mm_tn("tn_out_a", m_a, dh0, "row", shard_shape["w_out_a"])
    dm_a = _mm_nt("nt_out_a", dh0, W["w_out_a"], "row", out_dtype=BF16)

    def pair_sums(tag, batch):
        recv = _pair_exchange("pair_exchange_" + tag, [G[nm] for nm in batch])
        return [_pair_sum("pair_sum_" + nm, G[nm], rc, core) for nm, rc in zip(batch, recv)]

    late = ["w_kv", "w_in_b", "w_out_b", "w_ple_gate1", "w_ple_proj1"]
    early = ["w_in_a", "w_out_a", "w_ple_gate0", "w_ple_proj0"]
    sums_late = pair_sums("late", late)
    (dc, dz, dlg, dlb, dcb), parts_late = _mixa_bwd1("mixa_bwd1", conv_out, pa, dm_a, lg_full, lb_full,
                                                     plan=_scatter_plan(sums_late))
    dpa, dcw = _mixa_bwd2("mixa_bwd2", dc, pa, dz, cw_full, seq)
    G["w_in_a"] = _mm_tn("tn_in_a", u0, dpa, "col", shard_shape["w_in_a"])
    sums_early = pair_sums("early", early)
    du0, parts_early = _mm_nt("nt_in_a", dpa, W["w_in_a"], "col", out_dtype=BF16, plan=_scatter_plan(sums_early))
    dx, (dg0,) = _rms_bwd("rms_bwd_x", x2, dh0, [(g0, du0)])
    grad_x = dx.reshape(nb, seq, dm)

    sums = dict(zip(late + early, sums_late + sums_early))
    parts = dict(zip(late + early, parts_late + parts_early))
    halves = [_chip_sum("chip_sum_" + nm, sums[nm], parts[nm], sum_order) for nm in names]
    gfull = dict(zip(names, _sibling_join(halves)))

    def as_rows(a):
        return a.reshape(-1, dm)

    small_parts = [as_rows(dcw), as_rows(dcb), as_rows(dlg), as_rows(dlb), dg0, dg1, dkvg, dpg0, dpg1, as_rows(dgk), as_rows(dgq)]
    counts = [a.shape[0] for a in small_parts]
    total = sum(counts)
    packed = _pad_rows(jnp.concatenate(small_parts, axis=0), -(-total // 8) * 8)
    red = _gather8("reduce_small", packed, reduce=True)
    pieces, off = [], 0
    for n_ in counts:
        pieces.append(red[off:off + n_])
        off += n_
    r_dcw, r_dcb, r_dlg, r_dlb, r_g0, r_g1, r_kvg, r_pg0, r_pg1, r_gk, r_gq = pieces
    my_cols = lambda a: lax.dynamic_slice_in_dim(a.reshape(-1, cc), chip * ccs, ccs, axis=1)
    small_grads = {
        "norm_g": jnp.concatenate([r_g0, r_g1], axis=0),
        "conv_w": my_cols(r_dcw)[:CONV_WIDTH],
        "conv_b": my_cols(r_dcb), "ln_g": my_cols(r_dlg), "ln_b": my_cols(r_dlb),
        "kv_norm_g": r_kvg,
        "k_norm_g": r_gk.reshape(nheads, HEAD_DIM).sum(axis=0, keepdims=True),
        "q_norm_g": r_gq.reshape(3, nheads, HEAD_DIM).sum(axis=1),
        "ple_norm_g": jnp.concatenate([r_pg0, r_pg1], axis=0),
    }

    given = dict(norm_g=norm_g, w_in_a=w_in_a, conv_w=conv_w, conv_b=conv_b, ln_g=ln_g, ln_b=ln_b, w_out_a=w_out_a,
                 kv_norm_g=kv_norm_g, w_kv=w_kv, k_norm_g=k_norm_g, w_in_b=w_in_b, q_norm_g=q_norm_g, w_out_b=w_out_b,
                 ple_norm_g=ple_norm_g, w_ple_gate=w_ple_gate, w_ple_proj=w_ple_proj)
    mom1 = dict(norm_g=m_norm_g, w_in_a=m_w_in_a, conv_w=m_conv_w, conv_b=m_conv_b, ln_g=m_ln_g, ln_b=m_ln_b,
                w_out_a=m_w_out_a, kv_norm_g=m_kv_norm_g, w_kv=m_w_kv, k_norm_g=m_k_norm_g, w_in_b=m_w_in_b,
                q_norm_g=m_q_norm_g, w_out_b=m_w_out_b, ple_norm_g=m_ple_norm_g, w_ple_gate=m_w_ple_gate,
                w_ple_proj=m_w_ple_proj)
    mom2 = dict(norm_g=v_norm_g, w_in_a=v_w_in_a, conv_w=v_conv_w, conv_b=v_conv_b, ln_g=v_ln_g, ln_b=v_ln_b,
                w_out_a=v_w_out_a, kv_norm_g=v_kv_norm_g, w_kv=v_w_kv, k_norm_g=v_k_norm_g, w_in_b=v_w_in_b,
                q_norm_g=v_q_norm_g, w_out_b=v_w_out_b, ple_norm_g=v_ple_norm_g, w_ple_gate=v_w_ple_gate,
                w_ple_proj=v_w_ple_proj)
    order = ["norm_g", "w_in_a", "conv_w", "conv_b", "ln_g", "ln_b", "w_out_a", "kv_norm_g", "w_kv", "k_norm_g", "w_in_b",
             "q_norm_g", "w_out_b", "ple_norm_g", "w_ple_gate", "w_ple_proj"]
    grads, deltas, new_m, new_v = {}, {}, {}, {}
    for nm in order:
        shape = given[nm].shape
        if nm in ("w_ple_gate", "w_ple_proj"):
            g2 = jnp.concatenate([gfull[nm + "0"], gfull[nm + "1"]], axis=0)
        elif nm in gfull:
            g2 = gfull[nm]
        else:
            g2 = small_grads[nm]
        two_d = g2.shape
        d2, m2, v2 = _adamw("adamw_" + nm, given[nm].reshape(two_d), g2, mom1[nm].reshape(two_d), mom2[nm].reshape(two_d))
        grads[nm], deltas[nm], new_m[nm], new_v[nm] = (a.reshape(shape) for a in (g2, d2, m2, v2))

    return (loss, grad_x, *[grads[n_] for n_ in order], *[deltas[n_] for n_ in order],
            *[new_m[n_] for n_ in order], *[new_v[n_] for n_ in order])
```

```python
import functools

import jax
import jax.numpy as jnp
from jax import lax
from jax.experimental import pallas as pl
from jax.experimental.pallas import tpu as pltpu

F32 = jnp.float32
BF16 = jnp.bfloat16
MESH = pl.DeviceIdType.MESH

EPS = 1e-6
NEG_INF = -1e30
HEAD_DIM = 64
ROPE_DIM = 16
ROPE_THETA = 500000.0
CONV_WIDTH = 31
SUBLANES = 8
CONV_ROWS = 64
HALO = 32
SPAN = 128
DILATIONS = (1, 4, 16)
ADAM_LR, ADAM_B1, ADAM_B2, ADAM_EPS, ADAM_WD, ADAM_STEP = 0.001, 0.9, 0.999, 1e-08, 0.01, 10
N_CHIPS = 4
VMEM_LIMIT = 56 * 1024 * 1024


def _tile(n, target, mult=128):
    best = None
    t = mult
    while t <= min(n, target):
        if n % t == 0:
            best = t
        t += mult
    return best if best is not None else n


def _params(*sem):
    return pltpu.CompilerParams(dimension_semantics=tuple(sem) if sem else None, vmem_limit_bytes=VMEM_LIMIT)


def _sigmoid(x):
    return 0.5 * jnp.tanh(0.5 * x) + 0.5


def _seg_allsum64(x, terms=1):
    tr, w = x.shape
    cw = 256 if w % 256 == 0 else 128
    n = w // cw
    shift = HEAD_DIM.bit_length() - 1
    ri = lax.shift_right_logical(lax.broadcasted_iota(jnp.int32, (cw, cw), 0), shift)
    ci = lax.shift_right_logical(lax.broadcasted_iota(jnp.int32, (cw, cw), 1), shift)
    ones = (ri == ci).astype(BF16)

    def stack(v):
        return jnp.concatenate([v[:, j * cw:(j + 1) * cw] for j in range(n)], axis=0)

    hi = x.astype(BF16)
    s = jnp.dot(stack(hi), ones, preferred_element_type=F32)
    if terms == 2:
        lo = (x - hi.astype(F32)).astype(BF16)
        s = s + jnp.dot(stack(lo), ones, preferred_element_type=F32)
    return jnp.concatenate([s[j * tr:(j + 1) * tr] for j in range(n)], axis=1)


def _colsum(x):
    return jnp.sum(x, axis=0, keepdims=True)


def _shards_view(w, kind):
    return w if kind == "col" else w.reshape(1, -1, w.shape[2])


def _mm_nn(name, a, w, kind, *, out_dtype=F32, resid=None, norm_gain=None, heads=None, plan=None):
    t = a.shape[0]
    w = _shards_view(w, kind)
    ns, k, c = w.shape
    n = ns * c
    tm = _tile(t, 1024 if norm_gain is None else 512, 8)
    tk = _tile(k, 2048)
    tn = _tile(c, 1024)
    nk = k // tk
    per = c // tn
    assert norm_gain is None or tn == n, "the fused RMSNorm needs whole rows in one tile"
    assert norm_gain is None or heads is None
    n_in = 2 + (resid is not None) + (norm_gain is not None) + (4 if heads is not None else 0)
    if heads is not None:
        h_gain, h_tables, h_width, h_seq = heads
        assert h_width % tn == 0 and h_seq % tm == 0
        h_blocks = h_width // tn

    def body(*refs):
        a_ref, w_ref = refs[:2]
        r_ref = refs[2] if resid is not None else None
        g_ref = refs[n_in - 1] if norm_gain is not None else None
        o_ref = refs[n_in]
        part = jnp.dot(a_ref[...].astype(BF16), w_ref[...], preferred_element_type=F32)

        def finish(out):
            if resid is not None:
                out = out + r_ref[...]
            stored = out.astype(out_dtype)
            o_ref[...] = stored
            if norm_gain is not None:
                normed = out * lax.rsqrt(jnp.mean(out * out, axis=-1, keepdims=True) + EPS) * g_ref[...]
                refs[n_in + 1][...] = normed.astype(BF16)
            if heads is not None:
                hg_ref, tc_ref, ta_ref, tb_ref = refs[n_in - 4:n_in]

                @pl.when(pl.program_id(1) < h_blocks)
                def _():
                    refs[n_in + 1][...] = _hnr_fwd_math(stored.astype(F32), hg_ref[...], tc_ref, ta_ref, tb_ref)

        if nk == 1:
            finish(part)
            return
        acc = refs[-1]
        kk = pl.program_id(2)

        @pl.when(kk == 0)
        def _():
            acc[...] = part

        @pl.when(kk > 0)
        def _():
            acc[...] += part

        @pl.when(kk == nk - 1)
        def _():
            finish(acc[...])

    in_specs = [pl.BlockSpec((tm, tk), lambda i, j, kk: (i, kk)),
                pl.BlockSpec((None, tk, tn), lambda i, j, kk: (j // per, kk, j % per))]
    args = [a, w]
    if resid is not None:
        in_specs.append(pl.BlockSpec((tm, tn), lambda i, j, kk: (i, j)))
        args.append(resid)
    out_specs = [pl.BlockSpec((tm, tn), lambda i, j, kk: (i, j))]
    out_shape = [jax.ShapeDtypeStruct((t, n), out_dtype)]
    if norm_gain is not None:
        in_specs.append(pl.BlockSpec((1, n), lambda i, j, kk: (0, 0)))
        args.append(norm_gain)
        out_specs.append(pl.BlockSpec((tm, tn), lambda i, j, kk: (i, j)))
        out_shape.append(jax.ShapeDtypeStruct((t, n), BF16))
    if heads is not None:
        last = h_blocks - 1
        per_seq = h_seq // tm
        tab = pl.BlockSpec((tm, 128), lambda i, j, kk: (i % per_seq, 0))
        in_specs += [pl.BlockSpec((1, tn), lambda i, j, kk: (0, jnp.minimum(j, last))), tab, tab, tab]
        args += [h_gain, *h_tables]
        out_specs.append(pl.BlockSpec((tm, tn), lambda i, j, kk: (i, jnp.minimum(j, last))))
        out_shape.append(jax.ShapeDtypeStruct((t, h_width), F32))
    outs, carried = _hosted_call(
        body, plan, name=name, grid=(t // tm, n // tn, nk), in_specs=in_specs, out_specs=out_specs, out_shape=out_shape,
        scratch_shapes=[pltpu.VMEM((tm, tn), F32)] if nk > 1 else [],
        args=args, sem=("parallel", "arbitrary", "arbitrary"))
    out = outs[0] if len(outs) == 1 else tuple(outs)
    return out if plan is None else (out, carried)


def _mm_nt(name, d, w, kind, *, out_dtype=F32, plan=None):
    t = d.shape[0]
    w = _shards_view(w, kind)
    ns, k, c = w.shape
    n = ns * c
    tm = _tile(t, 1024, 8)
    to = _tile(k, 1024)
    tc = _tile(c, 1536)
    nc = n // tc
    per = c // tc

    def body(d_ref, w_ref, o_ref, *scratch):
        part = lax.dot_general(d_ref[...].astype(BF16), w_ref[...], (((1,), (1,)), ((), ())),
                               preferred_element_type=F32)
        if nc == 1:
            o_ref[...] = part.astype(out_dtype)
            return
        acc = scratch[0]
        kk = pl.program_id(2)

        @pl.when(kk == 0)
        def _():
            acc[...] = part

        @pl.when(kk > 0)
        def _():
            acc[...] += part

        @pl.when(kk == nc - 1)
        def _():
            o_ref[...] = acc[...].astype(out_dtype)

    (out,), carried = _hosted_call(
        body, plan, name=name, grid=(t // tm, k // to, nc),
        in_specs=[pl.BlockSpec((tm, tc), lambda i, j, kk: (i, kk)),
                  pl.BlockSpec((None, to, tc), lambda i, j, kk: (kk // per, j, kk % per))],
        out_specs=[pl.BlockSpec((tm, to), lambda i, j, kk: (i, j))],
        out_shape=[jax.ShapeDtypeStruct((t, k), out_dtype)],
        scratch_shapes=[pltpu.VMEM((tm, to), F32)] if nc > 1 else [],
        args=[d, w], sem=("parallel", "parallel", "arbitrary"))
    return out if plan is None else (out, carried)


def _mm_tn(name, a, d, kind, shard_shape, whole=False):
    t, k = a.shape
    n = d.shape[1]
    ns = N_CHIPS if kind == "col" and not whole else 1
    c = n // ns
    tkm = _tile(k, 1024)
    tn = _tile(c, 1536)
    tt = _tile(t, 1024, 8)
    nt = t // tt
    per = c // tn

    def body(a_ref, d_ref, o_ref, acc):
        kk = pl.program_id(2)
        part = lax.dot_general(a_ref[...].astype(BF16), d_ref[...].astype(BF16), (((0,), (0,)), ((), ())),
                               preferred_element_type=F32)

        @pl.when(kk == 0)
        def _():
            acc[...] = part

        @pl.when(kk > 0)
        def _():
            acc[...] += part

        @pl.when(kk == nt - 1)
        def _():
            o_ref[...] = acc[...].astype(BF16)

    out = pl.pallas_call(
        body, name=name, grid=(k // tkm, n // tn, nt),
        in_specs=[pl.BlockSpec((tt, tkm), lambda i, j, kk: (kk, i)),
                  pl.BlockSpec((tt, tn), lambda i, j, kk: (kk, j))],
        out_specs=pl.BlockSpec((None, tkm, tn), lambda i, j, kk: (j // per, i, j % per)),
        out_shape=jax.ShapeDtypeStruct((ns, k, c), BF16),
        scratch_shapes=[pltpu.VMEM((tkm, tn), F32)],
        compiler_params=_params("parallel", "parallel", "arbitrary"))(a, d)
    if kind == "col" and whole:
        return out.reshape(k, N_CHIPS, n // N_CHIPS).transpose(1, 0, 2)
    return out.reshape((N_CHIPS,) + tuple(shard_shape))


def _row_spec(tr, w, col=0):
    return pl.BlockSpec((tr, w), lambda i: (i, col))


def _full_spec(shape):
    return pl.BlockSpec(shape, lambda i: tuple(0 for _ in shape))


def _rms_fwd(name, x, gains, plan=None):
    t, dm = x.shape
    tr = _tile(t, 256, 8)
    n = len(gains)

    def body(x_ref, *refs):
        xv = x_ref[...]
        xn = xv * lax.rsqrt(jnp.mean(xv * xv, axis=-1, keepdims=True) + EPS)
        for g_ref, o_ref in zip(refs[:n], refs[n:]):
            o_ref[...] = (xn * g_ref[...]).astype(BF16)

    outs, carried = _hosted_call(
        body, plan, name=name, grid=(t // tr,),
        in_specs=[_row_spec(tr, dm)] + [_full_spec((1, dm))] * n,
        out_specs=[_row_spec(tr, dm)] * n,
        out_shape=[jax.ShapeDtypeStruct((t, dm), BF16)] * n,
        scratch_shapes=[], args=[x, *gains], sem=("parallel",))
    return outs if plan is None else (outs, carried)


def _ple_bwd_math(dy, gpre, pp):
    sg = _sigmoid(gpre)
    return (dy * pp * sg * (1.0 - sg)).astype(BF16), (dy * sg).astype(BF16)


def _rms_bwd(name, x, resid, pairs, ple=None):
    t, dm = x.shape
    tr = _tile(t, 256, 8)
    n = len(pairs)
    n_in = 2 * n + (2 if ple is not None else 0)

    def body(x_ref, r_ref, *refs):
        ins, outs = refs[:n_in], refs[n_in:]
        i = pl.program_id(0)
        xv = x_ref[...]
        rs = lax.rsqrt(jnp.mean(xv * xv, axis=-1, keepdims=True) + EPS)
        xn = xv * rs
        total = r_ref[...]
        for kx in range(n):
            g_ref, du_ref = ins[2 * kx], ins[2 * kx + 1]
            dg_ref = outs[1 + kx]
            du = du_ref[...].astype(F32)

            @pl.when(i == 0)
            def _():
                dg_ref[...] = jnp.zeros_like(dg_ref)

            dg_ref[...] += _colsum(du * xn)
            dxh = du * g_ref[...]
            total = total + rs * (dxh - xn * jnp.mean(dxh * xn, axis=-1, keepdims=True))
        outs[0][...] = total
        if ple is not None:
            outs[1 + n][...], outs[2 + n][...] = _ple_bwd_math(total, ins[2 * n][...].astype(F32), ins[2 * n + 1][...].astype(F32))

    in_specs = [_row_spec(tr, dm), _row_spec(tr, dm)]
    args = [x, resid]
    for g, du in pairs:
        in_specs += [_full_spec((1, dm)), _row_spec(tr, dm)]
        args += [g, du]
    out_specs = [_row_spec(tr, dm)] + [_full_spec((1, dm))] * n
    out_shape = [jax.ShapeDtypeStruct((t, dm), F32)] + [jax.ShapeDtypeStruct((1, dm), F32)] * n
    if ple is not None:
        in_specs += [_row_spec(tr, dm)] * 2
        args += list(ple)
        out_specs += [_row_spec(tr, dm)] * 2
        out_shape += [jax.ShapeDtypeStruct((t, dm), BF16)] * 2
    outs = pl.pallas_call(
        body, name=name, grid=(t // tr,), in_specs=in_specs, out_specs=out_specs, out_shape=out_shape,
        compiler_params=_params("arbitrary"))(*args)
    if ple is not None:
        return outs[0], list(outs[1:1 + n]), outs[1 + n], outs[2 + n]
    return outs[0], list(outs[1:])


def _shifted_copies(ext, sh, rows):
    for s in range(1, SUBLANES):
        sh[s - 1] = ext[pl.ds(s, rows), :]


def _window(ext, sh, off, row0, rows, lanes):
    s = off % SUBLANES
    src = ext if s == 0 else sh.at[s - 1]
    return src[pl.ds(off - s + row0, rows), lanes]


def _mixa_fwd(name, pa, cw, cb, lg, lb, seq):
    t, w3 = pa.shape
    cc = w3 // 3
    tr = _tile(seq, 128, HALO)
    per_seq = seq // tr
    hb = tr // HALO
    lead = HALO - (CONV_WIDTH - 1)

    def body(a_ref, b_ref, z_ref, ah_ref, bh_ref, cw_ref, cb_ref, lg_ref, lb_ref, c_ref, m_ref, ext, sh, conv):
        i = pl.program_id(0)
        gh = ah_ref[...].astype(F32) * _sigmoid(bh_ref[...].astype(F32))
        ext[pl.ds(0, HALO), :] = jnp.where((i % per_seq) == 0, 0.0, gh)
        ext[pl.ds(HALO, tr), :] = a_ref[...].astype(F32) * _sigmoid(b_ref[...].astype(F32))
        _shifted_copies(ext, sh, tr + HALO - SUBLANES)
        for lc in range(cc // 128):
            lanes = pl.ds(lc * 128, 128)
            taps = [cw_ref[pl.ds(k, 1), lanes] for k in range(CONV_WIDTH)]
            for row0 in range(0, tr, CONV_ROWS):
                acc = jnp.broadcast_to(cb_ref[:, lanes], (CONV_ROWS, 128))
                for k in range(CONV_WIDTH):
                    acc = acc + _window(ext, sh, lead + k, row0, CONV_ROWS, lanes) * taps[k]
                conv[pl.ds(row0, CONV_ROWS), lanes] = acc
        acc = conv[...]
        c_ref[...] = acc.astype(BF16)
        xc = acc - jnp.mean(acc, axis=-1, keepdims=True)
        nrm = xc * lax.rsqrt(jnp.mean(xc * xc, axis=-1, keepdims=True) + EPS)
        l = nrm * lg_ref[...] + lb_ref[...]
        z = z_ref[...].astype(F32)
        m_ref[...] = (l * _sigmoid(l) * z * _sigmoid(z)).astype(BF16)

    halo = lambda col: pl.BlockSpec((HALO, cc), lambda i: (jnp.maximum(i * hb - 1, 0), col))
    return pl.pallas_call(
        body, name=name, grid=(t // tr,),
        in_specs=[_row_spec(tr, cc, 0), _row_spec(tr, cc, 1), _row_spec(tr, cc, 2), halo(0), halo(1),
                  _full_spec((HALO, cc)), _full_spec((1, cc)), _full_spec((1, cc)), _full_spec((1, cc))],
        out_specs=[_row_spec(tr, cc), _row_spec(tr, cc)],
        out_shape=[jax.ShapeDtypeStruct((t, cc), BF16), jax.ShapeDtypeStruct((t, cc), BF16)],
        scratch_shapes=[pltpu.VMEM((tr + HALO, cc), F32), pltpu.VMEM((SUBLANES - 1, tr + HALO - SUBLANES, cc), F32),
                        pltpu.VMEM((tr, cc), F32)],
        compiler_params=_params("parallel"))(pa, pa, pa, pa, pa, cw, cb, lg, lb)


def _mixa_bwd1(name, c, pa, dm, lg, lb, plan=None):
    t, cc = c.shape
    tr = _tile(t, 128, 8)

    def body(c_ref, z_ref, dm_ref, lg_ref, lb_ref, dc_ref, dz_ref, dlg_ref, dlb_ref, dcb_ref):
        i = pl.program_id(0)
        cv = c_ref[...].astype(F32)
        xc = cv - jnp.mean(cv, axis=-1, keepdims=True)
        rs = lax.rsqrt(jnp.mean(xc * xc, axis=-1, keepdims=True) + EPS)
        nrm = xc * rs
        l = nrm * lg_ref[...] + lb_ref[...]
        z = z_ref[...].astype(F32)
        sl, sz = _sigmoid(l), _sigmoid(z)
        dmv = dm_ref[...].astype(F32)
        ds = dmv * (z * sz)
        dzz = dmv * (l * sl)
        dz_ref[...] = (dzz * (sz * (1.0 + z * (1.0 - sz)))).astype(BF16)
        dl = ds * (sl * (1.0 + l * (1.0 - sl)))
        dn = dl * lg_ref[...]
        dc = rs * (dn - jnp.mean(dn, axis=-1, keepdims=True) - nrm * jnp.mean(dn * nrm, axis=-1, keepdims=True))
        dc_ref[...] = dc.astype(BF16)

        @pl.when(i == 0)
        def _():
            dlg_ref[...] = jnp.zeros_like(dlg_ref)
            dlb_ref[...] = jnp.zeros_like(dlb_ref)
            dcb_ref[...] = jnp.zeros_like(dcb_ref)

        dlg_ref[...] += _colsum(dl * nrm)
        dlb_ref[...] += _colsum(dl)
        dcb_ref[...] += _colsum(dc)

    vec = jax.ShapeDtypeStruct((1, cc), F32)
    outs, carried = _hosted_call(
        body, plan, name=name, grid=(t // tr,),
        in_specs=[_row_spec(tr, cc), _row_spec(tr, cc, 2), _row_spec(tr, cc), _full_spec((1, cc)), _full_spec((1, cc))],
        out_specs=[_row_spec(tr, cc), _row_spec(tr, cc)] + [_full_spec((1, cc))] * 3,
        out_shape=[jax.ShapeDtypeStruct((t, cc), BF16), jax.ShapeDtypeStruct((t, cc), BF16), vec, vec, vec],
        scratch_shapes=[], args=[c, pa, dm, lg, lb], sem=("arbitrary",))
    return outs if plan is None else (outs, carried)


def _mixa_bwd2(name, dc, pa, dz, cw, seq):
    t, cc = dc.shape
    tr = _tile(seq, 128, HALO)
    per_seq = seq // tr
    hb = tr // HALO
    steps = t // tr
    last_halo = t // HALO - 1

    def body(dc_ref, dcn_ref, a_ref, b_ref, dz_ref, cw_ref, dp_ref, dcw_ref, ext, sh, sums):
        i = pl.program_id(0)
        ext[pl.ds(0, tr), :] = dc_ref[...].astype(F32)
        ext[pl.ds(tr, HALO), :] = jnp.where((i % per_seq) == per_seq - 1, 0.0, dcn_ref[...].astype(F32))
        _shifted_copies(ext, sh, tr + HALO - SUBLANES)

        @pl.when(i == 0)
        def _():
            sums[...] = jnp.zeros_like(sums)
            dcw_ref[...] = jnp.zeros_like(dcw_ref)

        av = a_ref[...].astype(F32)
        sb = _sigmoid(b_ref[...].astype(F32))
        glu = av * sb
        dglu = jnp.zeros((tr, cc), F32)
        for k in range(CONV_WIDTH):
            wd = _window(ext, sh, CONV_WIDTH - 1 - k, 0, tr, slice(None))
            dglu = dglu + wd * cw_ref[pl.ds(k, 1), :]
            sums[pl.ds(k * SUBLANES, SUBLANES), :] += (wd * glu).reshape(tr // SUBLANES, SUBLANES, cc).sum(axis=0)
        dp_ref[:, pl.ds(0, cc)] = (dglu * sb).astype(BF16)
        dp_ref[:, pl.ds(cc, cc)] = (dglu * av * sb * (1.0 - sb)).astype(BF16)
        dp_ref[:, pl.ds(2 * cc, cc)] = dz_ref[...]

        @pl.when(i == steps - 1)
        def _():
            for k in range(CONV_WIDTH):
                dcw_ref[pl.ds(k, 1), :] = _colsum(sums[pl.ds(k * SUBLANES, SUBLANES), :])

    nxt = pl.BlockSpec((HALO, cc), lambda i: (jnp.minimum((i + 1) * hb, last_halo), 0))
    return pl.pallas_call(
        body, name=name, grid=(steps,),
        in_specs=[_row_spec(tr, cc), nxt, _row_spec(tr, cc, 0), _row_spec(tr, cc, 1), _row_spec(tr, cc),
                  _full_spec((HALO, cc))],
        out_specs=[_row_spec(tr, 3 * cc), _full_spec((HALO, cc))],
        out_shape=[jax.ShapeDtypeStruct((t, 3 * cc), BF16), jax.ShapeDtypeStruct((HALO, cc), F32)],
        scratch_shapes=[pltpu.VMEM((tr + HALO, cc), F32), pltpu.VMEM((SUBLANES - 1, tr + HALO - SUBLANES, cc), F32),
                        pltpu.VMEM((HALO * SUBLANES, cc), F32)],
        compiler_params=_params("arbitrary"))(dc, dc, pa, pa, dz, cw)


def _ple_fwd(name, h, gpre, pp, gains):
    t, dm = h.shape
    tr = _tile(t, 256, 8)

    n = len(gains)

    def body(h_ref, g_ref, p_ref, *refs):
        o_ref = refs[n]
        xv = h_ref[...] + _sigmoid(g_ref[...].astype(F32)) * p_ref[...].astype(F32)
        o_ref[...] = xv
        xn = xv * lax.rsqrt(jnp.mean(xv * xv, axis=-1, keepdims=True) + EPS)
        for gain_ref, n_ref in zip(refs[:n], refs[n + 1:]):
            n_ref[...] = (xn * gain_ref[...]).astype(BF16)

    return pl.pallas_call(
        body, name=name, grid=(t // tr,), in_specs=[_row_spec(tr, dm)] * 3 + [_full_spec((1, dm))] * n,
        out_specs=[_row_spec(tr, dm)] * (1 + n),
        out_shape=[jax.ShapeDtypeStruct((t, dm), F32)] + [jax.ShapeDtypeStruct((t, dm), BF16)] * n,
        compiler_params=_params("parallel"))(h, gpre, pp, *gains)


def _ple_loss(name, h, gpre, pp, target):
    t, dm = h.shape
    tr = _tile(t, 256, 8)

    def body(h_ref, g_ref, p_ref, t_ref, dy_ref, dg_ref, dp_ref, sq_ref):
        i = pl.program_id(0)
        gpre_v, pp_v = g_ref[...].astype(F32), p_ref[...].astype(F32)
        err = h_ref[...] + _sigmoid(gpre_v) * pp_v - t_ref[...]
        dy = err * (1.0 / dm)
        dy_ref[...] = dy
        dg_ref[...], dp_ref[...] = _ple_bwd_math(dy, gpre_v, pp_v)

        @pl.when(i == 0)
        def _():
            sq_ref[...] = jnp.zeros_like(sq_ref)

        sq_ref[...] += jnp.sum(jnp.sum(err * err, axis=1, keepdims=True), axis=0, keepdims=True)

    return pl.pallas_call(
        body, name=name, grid=(t // tr,), in_specs=[_row_spec(tr, dm)] * 4,
        out_specs=[_row_spec(tr, dm)] * 3 + [_full_spec((1, 1))],
        out_shape=[jax.ShapeDtypeStruct((t, dm), F32)] + [jax.ShapeDtypeStruct((t, dm), BF16)] * 2
        + [jax.ShapeDtypeStruct((1, 1), F32)],
        compiler_params=_params("arbitrary"))(h, gpre, pp, target)


def _rope_tables(seq):
    half = ROPE_DIM // 2
    inv = ROPE_THETA ** (-jnp.arange(half, dtype=F32) * (2.0 / ROPE_DIM))
    ang = jnp.arange(seq).astype(F32)[:, None] * inv[None, :]
    cos, sin = jnp.cos(ang), jnp.sin(ang)
    rest = HEAD_DIM - ROPE_DIM
    one = jnp.ones((seq, rest), F32)
    zero = jnp.zeros((seq, rest), F32)
    zh = jnp.zeros((seq, half), F32)
    tc = jnp.concatenate([cos, cos, one], axis=1)
    ta = jnp.concatenate([-sin, zh, zero], axis=1)
    tb = jnp.concatenate([zh, sin, zero], axis=1)
    return [jnp.tile(tb_, (1, 128 // HEAD_DIM)) for tb_ in (tc, ta, tb)]


def _wide(tab_ref, w):
    return jnp.tile(tab_ref[...], (1, w // 128))


def _hnr_fwd_math(xv, gain, tc_ref, ta_ref, tb_ref):
    width = xv.shape[1]
    rs = lax.rsqrt(_seg_allsum64(xv * xv) * (1.0 / HEAD_DIM) + EPS)
    y = xv * rs * gain
    return (y * _wide(tc_ref, width) + pltpu.roll(y, width - ROPE_DIM // 2, 1) * _wide(ta_ref, width)
            + pltpu.roll(y, ROPE_DIM // 2, 1) * _wide(tb_ref, width))


def _hnr_bwd_math(xv, gain, dout, tc, ta, tb, width):
    dy = dout * tc + pltpu.roll(dout * ta, ROPE_DIM // 2, 1) + pltpu.roll(dout * tb, width - ROPE_DIM // 2, 1)
    rs = lax.rsqrt(_seg_allsum64(xv * xv) * (1.0 / HEAD_DIM) + EPS)
    xn = xv * rs
    dyh = dy * gain
    dx = rs * (dyh - xn * (_seg_allsum64(dyh * xn, terms=2) * (1.0 / HEAD_DIM)))
    return dx, _colsum(dy * xn)


def _q_bwd(name, p1, gain, tables, dqs, dgt, seq):
    t, w4 = p1.shape
    da = w4 // 4
    width = 3 * da
    tr = _tile(seq, 128, 8)
    per_seq = seq // tr

    def body(x_ref, g_ref, tc_ref, ta_ref, tb_ref, d0_ref, d1_ref, d2_ref, dgt_ref, o_ref, dg_ref):
        i = pl.program_id(0)
        dout = jnp.concatenate([d0_ref[...], d1_ref[...], d2_ref[...]], axis=1)
        dx, dg = _hnr_bwd_math(x_ref[...].astype(F32), g_ref[...], dout, _wide(tc_ref, width), _wide(ta_ref, width),
                               _wide(tb_ref, width), width)

        @pl.when(i == 0)
        def _():
            dg_ref[...] = jnp.zeros_like(dg_ref)

        dg_ref[...] += dg
        o_ref[:, pl.ds(0, width)] = dx.astype(BF16)
        o_ref[:, pl.ds(width, da)] = dgt_ref[...]

    tab = pl.BlockSpec((tr, 128), lambda i: (i % per_seq, 0))
    return pl.pallas_call(
        body, name=name, grid=(t // tr,),
        in_specs=[_row_spec(tr, width), _full_spec((1, width)), tab, tab, tab] + [_row_spec(tr, da)] * 4,
        out_specs=[_row_spec(tr, w4), _full_spec((1, width))],
        out_shape=[jax.ShapeDtypeStruct((t, w4), BF16), jax.ShapeDtypeStruct((1, width), F32)],
        compiler_params=_params("arbitrary"))(p1, gain, *tables, *dqs, dgt)


def _k_bwd(name, kv, gain, tables, dk, dv, seq):
    t, w2 = kv.shape
    da = w2 // 2
    tr = _tile(seq, 256, 8)
    per_seq = seq // tr

    def body(x_ref, g_ref, tc_ref, ta_ref, tb_ref, dk_ref, dv_ref, o_ref, dg_ref):
        i = pl.program_id(0)
        dx, dg = _hnr_bwd_math(x_ref[...], g_ref[...], dk_ref[...], _wide(tc_ref, da), _wide(ta_ref, da),
                               _wide(tb_ref, da), da)

        @pl.when(i == 0)
        def _():
            dg_ref[...] = jnp.zeros_like(dg_ref)

        dg_ref[...] += dg
        o_ref[:, pl.ds(0, da)] = dx.astype(BF16)
        o_ref[:, pl.ds(da, da)] = dv_ref[...].astype(BF16)

    tab = pl.BlockSpec((tr, 128), lambda i: (i % per_seq, 0))
    return pl.pallas_call(
        body, name=name, grid=(t // tr,),
        in_specs=[_row_spec(tr, da), _full_spec((1, da)), tab, tab, tab] + [_row_spec(tr, da)] * 2,
        out_specs=[_row_spec(tr, w2), _full_spec((1, da))],
        out_shape=[jax.ShapeDtypeStruct((t, w2), BF16), jax.ShapeDtypeStruct((1, da), F32)],
        compiler_params=_params("arbitrary"))(kv, gain, *tables, dk, dv)


def _unit_index(dil, r, blk):
    if dil == 1:
        start = blk * SPAN
        return pl.ds(start if isinstance(start, int) else pl.multiple_of(start, SPAN), SPAN)
    return pl.ds(r + dil * SPAN * blk, SPAN, stride=dil)


def _unit_rows(ref, dil, r, blk):
    return ref[_unit_index(dil, r, blk), :]


def _store_rows(ref, dil, r, blk, val):
    ref[_unit_index(dil, r, blk), :] = val


def _over_units(dil, nblk, unit, carry0, after=None):
    for r in range(dil):
        carry = carry0
        for blk in range(nblk):
            carry = unit(r, blk, blk > 0, carry)
        if after is not None:
            after(r, carry)


def _band_mask(with_prev):
    nk = 2 * SPAN if with_prev else SPAN
    qi = lax.broadcasted_iota(jnp.int32, (SPAN, nk), 0)
    kj = lax.broadcasted_iota(jnp.int32, (SPAN, nk), 1)
    if with_prev:
        return (kj >= qi) & (kj <= qi + SPAN)
    return kj <= qi


_NT = (((1,), (1,)), ((), ()))
_TN = (((0,), (0,)), ((), ()))
N_PAIR = 128 // HEAD_DIM


def _per_head(x):
    head = lax.shift_right_logical(lax.broadcasted_iota(jnp.int32, x.shape, 1), HEAD_DIM.bit_length() - 1)
    return [jnp.where(head == hh, x, 0.0).astype(BF16) for hh in range(N_PAIR)]


def _head_columns(stat):
    return [stat[:, hh * HEAD_DIM:hh * HEAD_DIM + 1] for hh in range(N_PAIR)]


def _by_head(cols):
    head = lax.shift_right_logical(lax.broadcasted_iota(jnp.int32, (cols[0].shape[0], 128), 1), HEAD_DIM.bit_length() - 1)
    out = cols[-1]
    for hh in range(N_PAIR - 2, -1, -1):
        out = jnp.where(head == hh, cols[hh], out)
    return jnp.broadcast_to(out, (cols[0].shape[0], 128))


def _group_fwd(q_ref, k_ref, v_ref, o_ref, l_ref, dil, seq):
    nblk = seq // (dil * SPAN)
    scale = HEAD_DIM ** -0.5

    def unit(r, blk, with_prev, carry):
        q = (_unit_rows(q_ref, dil, r, blk) * scale).astype(BF16)
        kc = _per_head(_unit_rows(k_ref, dil, r, blk))
        vc = _per_head(_unit_rows(v_ref, dil, r, blk))
        mask = _band_mask(with_prev)
        out = None
        lses, rdens = [], []
        for hh in range(N_PAIR):
            kh = jnp.concatenate([carry[0][hh], kc[hh]], axis=0) if with_prev else kc[hh]
            vh = jnp.concatenate([carry[1][hh], vc[hh]], axis=0) if with_prev else vc[hh]
            s = lax.dot_general(q, kh, _NT, preferred_element_type=F32)
            s = jnp.where(mask, s, NEG_INF)
            mx = jnp.max(s, axis=-1, keepdims=True)
            p = jnp.exp(s - mx)
            den = jnp.sum(p, axis=-1, keepdims=True)
            part = jnp.dot(p.astype(BF16), vh, preferred_element_type=F32)
            out = part if out is None else out + part
            lses.append(mx + jnp.log(den))
            rdens.append(1.0 / den)
        _store_rows(o_ref, dil, r, blk, out * _by_head(rdens))
        _store_rows(l_ref, dil, r, blk, _by_head(lses))
        return kc, vc

    _over_units(dil, nblk, unit, None)


def _attn_fwd(name, qn, kn, kv, p1, seq):
    t, da = kn.shape
    hp = da // 128
    ng = len(DILATIONS)

    def body(q0_ref, q1_ref, q2_ref, k_ref, v_ref, g_ref, o_ref, l_ref, m_ref, og, lg):
        for g, q_ref in enumerate((q0_ref, q1_ref, q2_ref)):
            _group_fwd(q_ref, k_ref, v_ref, og.at[g], lg.at[g], DILATIONS[g], seq)
        a0, a1, a2 = lg[0], lg[1], lg[2]
        mx = jnp.maximum(jnp.maximum(a0, a1), a2)
        e0, e1, e2 = jnp.exp(a0 - mx), jnp.exp(a1 - mx), jnp.exp(a2 - mx)
        den = e0 + e1 + e2
        o = (e0 * og[0] + e1 * og[1] + e2 * og[2]) / den
        o_ref[...] = o
        l_ref[...] = mx + jnp.log(den)
        gt = g_ref[...].astype(F32)
        m_ref[...] = (o * gt * _sigmoid(gt)).astype(BF16)

    blk_spec = lambda off: pl.BlockSpec((seq, 128), lambda b, h: (b, off + h))
    return pl.pallas_call(
        body, name=name, grid=(t // seq, hp),
        in_specs=[blk_spec(0), blk_spec(hp), blk_spec(2 * hp), blk_spec(0), blk_spec(hp), blk_spec(3 * hp)],
        out_specs=[blk_spec(0)] * 3,
        out_shape=[jax.ShapeDtypeStruct((t, da), F32)] * 2 + [jax.ShapeDtypeStruct((t, da), BF16)],
        scratch_shapes=[pltpu.VMEM((ng, seq, 128), F32), pltpu.VMEM((ng, seq, 128), F32)],
        compiler_params=_params("parallel", "parallel"))(qn, qn, qn, kn, kv, p1)


def _group_bwd(q_ref, k_ref, v_ref, do_ref, l_ref, d_ref, dq_ref, dk_ref, dv_ref, dil, seq, first):
    nblk = seq // (dil * SPAN)
    scale = HEAD_DIM ** -0.5

    def put(ref, r, blk, val):
        if not first:
            val = val + _unit_rows(ref, dil, r, blk)
        _store_rows(ref, dil, r, blk, val)

    def unit(r, blk, with_prev, carry):
        kp, vp, pend_k, pend_v = carry if with_prev else (None,) * 4
        q = _unit_rows(q_ref, dil, r, blk) * scale
        kc = _per_head(_unit_rows(k_ref, dil, r, blk))
        vc = _per_head(_unit_rows(v_ref, dil, r, blk))
        dov = _unit_rows(do_ref, dil, r, blk)
        lcols = _head_columns(_unit_rows(l_ref, dil, r, blk))
        dcols = _head_columns(_unit_rows(d_ref, dil, r, blk))
        mask = _band_mask(with_prev)
        qb, dob = q.astype(BF16), dov.astype(BF16)
        dq = dkcat = dvcat = None
        for hh, (qh, doh) in enumerate(zip(_per_head(q), _per_head(dov))):
            kh = jnp.concatenate([kp[hh], kc[hh]], axis=0) if with_prev else kc[hh]
            vh = jnp.concatenate([vp[hh], vc[hh]], axis=0) if with_prev else vc[hh]
            s = lax.dot_general(qb, kh, _NT, preferred_element_type=F32)
            p = jnp.where(mask, jnp.exp(s - lcols[hh]), 0.0)
            dp = lax.dot_general(dob, vh, _NT, preferred_element_type=F32)
            ds = (p * (dp - dcols[hh])).astype(BF16)
            parts = (jnp.dot(ds, kh, preferred_element_type=F32),
                     lax.dot_general(ds, qh, _TN, preferred_element_type=F32),
                     lax.dot_general(p.astype(BF16), doh, _TN, preferred_element_type=F32))
            dq, dkcat, dvcat = parts if dq is None else (dq + parts[0], dkcat + parts[1], dvcat + parts[2])
        _store_rows(dq_ref, dil, r, blk, dq * scale)
        if with_prev:
            put(dk_ref, r, blk - 1, pend_k + dkcat[:SPAN])
            put(dv_ref, r, blk - 1, pend_v + dvcat[:SPAN])
            return kc, vc, dkcat[SPAN:], dvcat[SPAN:]
        return kc, vc, dkcat, dvcat

    def after(r, carry):
        put(dk_ref, r, nblk - 1, carry[2])
        put(dv_ref, r, nblk - 1, carry[3])

    _over_units(dil, nblk, unit, None, after)


def _attn_bwd(name, qn, kn, kv, do, lse, dsum, seq):
    t, da = kn.shape
    hp = da // 128

    def body(q0_ref, q1_ref, q2_ref, k_ref, v_ref, do_ref, l_ref, d_ref, dq0_ref, dq1_ref, dq2_ref, dk_ref, dv_ref):
        groups = ((q0_ref, dq0_ref), (q1_ref, dq1_ref), (q2_ref, dq2_ref))
        for g, (q_ref, dq_ref) in enumerate(groups):
            _group_bwd(q_ref, k_ref, v_ref, do_ref, l_ref, d_ref, dq_ref, dk_ref, dv_ref, DILATIONS[g], seq, g == 0)

    blk_spec = lambda off: pl.BlockSpec((seq, 128), lambda b, h: (b, off + h))
    return pl.pallas_call(
        body, name=name, grid=(t // seq, hp),
        in_specs=[blk_spec(0), blk_spec(hp), blk_spec(2 * hp), blk_spec(0), blk_spec(hp), blk_spec(0), blk_spec(0), blk_spec(0)],
        out_specs=[blk_spec(0)] * 5,
        out_shape=[jax.ShapeDtypeStruct((t, da), F32)] * 5,
        compiler_params=_params("parallel", "parallel"))(qn, qn, qn, kn, kv, do, lse, dsum)


def _gate_bwd(name, dm, o, p1):
    t, da = o.shape
    tr = _tile(t, 256, 8)

    def body(dm_ref, o_ref, g_ref, do_ref, dg_ref, ds_ref):
        g = g_ref[...].astype(F32)
        sg = _sigmoid(g)
        dmv, ov = dm_ref[...].astype(F32), o_ref[...]
        do = dmv * (g * sg)
        do_ref[...] = do
        dg_ref[...] = (dmv * ov * (sg * (1.0 + g * (1.0 - sg)))).astype(BF16)
        ds_ref[...] = _seg_allsum64(do * ov, terms=2)

    return pl.pallas_call(
        body, name=name, grid=(t // tr,),
        in_specs=[_row_spec(tr, da), _row_spec(tr, da), _row_spec(tr, da, 3)],
        out_specs=[_row_spec(tr, da)] * 3,
        out_shape=[jax.ShapeDtypeStruct((t, da), F32), jax.ShapeDtypeStruct((t, da), BF16), jax.ShapeDtypeStruct((t, da), F32)],
        compiler_params=_params("parallel"))(dm, o, p1)


def _cast_bf16(name, w2d, chip):
    r, c = w2d.shape
    tr = _tile(r, 256, 16)

    def body(chip_ref, x_ref, o_ref):
        o_ref[...] = x_ref[...].astype(BF16)

    grid_spec = pltpu.PrefetchScalarGridSpec(
        num_scalar_prefetch=1, grid=(r // tr,),
        in_specs=[pl.BlockSpec((tr, c), lambda i, m: (i, 0))],
        out_specs=pl.BlockSpec((None, tr, c), lambda i, m: (m[0], i, 0)))
    return pl.pallas_call(
        body, name=name, grid_spec=grid_spec, out_shape=jax.ShapeDtypeStruct((N_CHIPS, r, c), BF16),
        compiler_params=_params("parallel"))(chip, w2d)


def _adamw(name, w, g, m, v):
    r, c = w.shape
    tr = _tile(r, 256, 8)
    c1 = 1.0 - ADAM_B1 ** ADAM_STEP
    c2 = 1.0 - ADAM_B2 ** ADAM_STEP

    def body(w_ref, g_ref, m_ref, v_ref, d_ref, nm_ref, nv_ref):
        gv = g_ref[...]
        nm = ADAM_B1 * m_ref[...] + (1.0 - ADAM_B1) * gv
        nv = ADAM_B2 * v_ref[...] + (1.0 - ADAM_B2) * (gv * gv)
        nm_ref[...] = nm
        nv_ref[...] = nv
        d_ref[...] = -ADAM_LR * ((nm / c1) / (jnp.sqrt(nv / c2) + ADAM_EPS) + ADAM_WD * w_ref[...])

    sds = jax.ShapeDtypeStruct((r, c), F32)
    return pl.pallas_call(
        body, name=name, grid=(r // tr,), in_specs=[_row_spec(tr, c)] * 4, out_specs=[_row_spec(tr, c)] * 3,
        out_shape=[sds] * 3, compiler_params=_params("parallel"))(w, g, m, v)


def _pair_sum(name, gd, recv, core):
    _, r, c = gd.shape
    rh = r // 2
    tr = _tile(rh, 256, 8)
    nrt = rh // tr

    def body(c_ref, a_ref, b_ref, o_ref):
        o_ref[...] = (a_ref[...].astype(F32) + b_ref[...].astype(F32)).astype(BF16)

    grid_spec = pltpu.PrefetchScalarGridSpec(
        num_scalar_prefetch=1, grid=(N_CHIPS, nrt),
        in_specs=[pl.BlockSpec((None, tr, c), lambda j, i, cr: (j, cr[0] * nrt + i, 0)),
                  pl.BlockSpec((None, tr, c), lambda j, i, cr: (j, i, 0))],
        out_specs=pl.BlockSpec((None, tr, c), lambda j, i, cr: (j, i, 0)))
    return pl.pallas_call(
        body, name=name, grid_spec=grid_spec, out_shape=jax.ShapeDtypeStruct((N_CHIPS, rh, c), BF16),
        compiler_params=_params("parallel", "parallel"))(core, gd, recv)


def _chip_sum(name, sums, parts, order):
    _, rh, c = parts.shape
    tr = _tile(rh, 256, 16)
    nrt = rh // tr

    def body(o_ref_, s_ref, p1_ref, p2_ref, p3_ref, o_ref):
        acc = s_ref[...].astype(F32)
        for p_ref in (p1_ref, p2_ref, p3_ref):
            acc = acc + p_ref[...].astype(F32)
        o_ref[...] = acc

    slot = lambda k: pl.BlockSpec((None, tr, c), lambda i, o: (o[k], i, 0))
    grid_spec = pltpu.PrefetchScalarGridSpec(
        num_scalar_prefetch=1, grid=(nrt,),
        in_specs=[slot(0), slot(1), slot(2), slot(3)],
        out_specs=pl.BlockSpec((tr, c), lambda i, o: (o[N_CHIPS] * nrt + i, 0)))
    return pl.pallas_call(
        body, name=name, grid_spec=grid_spec, out_shape=jax.ShapeDtypeStruct((2 * rh, c), F32),
        compiler_params=_params("parallel"))(order, sums, parts, parts, parts)


HBM = pl.BlockSpec(memory_space=pl.ANY)


def _place():
    x, y, c = lax.axis_index("x"), lax.axis_index("y"), lax.axis_index("c")
    chips = [(1 - x, y), (x, 1 - y), (1 - x, 1 - y)]
    return x, y, c, chips


def _half(ref, hc):
    rows = ref.shape[0] // 2
    return ref.at[pl.ds(hc * rows, rows)]


class _Plan:
    def __init__(self, ins, out_shapes, aliases, n_sems, start, finish):
        self.ins, self.out_shapes, self.aliases, self.n_sems = list(ins), list(out_shapes), dict(aliases), n_sems
        self.start, self.finish = start, finish


def _gather_plan(shards):
    n = len(shards)

    def ici(place, outs, send_sems, recv_sems, i, k, slot):
        x, y, c, chips = place
        half = _half(outs[i].at[slot], c)
        return pltpu.make_async_remote_copy(
            src_ref=half, dst_ref=half, send_sem=send_sems.at[6 * i + k], recv_sem=recv_sems.at[6 * i + k],
            device_id=(chips[k][0], chips[k][1], c), device_id_type=MESH)

    def d2d(place, outs, send_sems, recv_sems, i, k, slot, hc):
        x, y, c, chips = place
        half = _half(outs[i].at[slot], hc)
        return pltpu.make_async_remote_copy(
            src_ref=half, dst_ref=half, send_sem=send_sems.at[6 * i + 3 + k], recv_sem=recv_sems.at[6 * i + 3 + k],
            device_id=(x, y, 1 - c), device_id_type=MESH)

    def start(place, ins, outs, send_sems, recv_sems):
        x, y, c, chips = place
        for i in range(n):
            for k in range(3):
                ici(place, outs, send_sems, recv_sems, i, k, 2 * x + y).start()

    def finish(place, ins, outs, send_sems, recv_sems):
        x, y, c, chips = place
        theirs = [2 * chip[0] + chip[1] for chip in chips]
        for i in range(n):
            for k in range(3):
                ici(place, outs, send_sems, recv_sems, i, k, theirs[k]).wait_recv()
                d2d(place, outs, send_sems, recv_sems, i, k, theirs[k], c).start()
        for i in range(n):
            for k in range(3):
                d2d(place, outs, send_sems, recv_sems, i, k, theirs[k], 1 - c).wait_recv()
        for i in range(n):
            for k in range(3):
                ici(place, outs, send_sems, recv_sems, i, k, 2 * x + y).wait_send()
                d2d(place, outs, send_sems, recv_sems, i, k, theirs[k], c).wait_send()

    return _Plan(shards, [jax.ShapeDtypeStruct(s.shape, s.dtype) for s in shards], {i: i for i in range(n)}, 6 * n,
                 start, finish)


def _scatter_plan(sums):
    n = len(sums)

    def copy(place, ins, outs, send_sems, recv_sems, i, k, src_slot, dst_slot):
        x, y, c, chips = place
        return pltpu.make_async_remote_copy(
            src_ref=ins[i].at[src_slot], dst_ref=outs[i].at[dst_slot],
            send_sem=send_sems.at[3 * i + k], recv_sem=recv_sems.at[3 * i + k],
            device_id=(chips[k][0], chips[k][1], c), device_id_type=MESH)

    def start(place, ins, outs, send_sems, recv_sems):
        x, y, c, chips = place
        for i in range(n):
            for k, chip in enumerate(chips):
                copy(place, ins, outs, send_sems, recv_sems, i, k, 2 * chip[0] + chip[1], 2 * x + y).start()

    def finish(place, ins, outs, send_sems, recv_sems):
        x, y, c, chips = place
        for i in range(n):
            for k, chip in enumerate(chips):
                theirs = 2 * chip[0] + chip[1]
                copy(place, ins, outs, send_sems, recv_sems, i, k, theirs, 2 * x + y).wait_send()
                copy(place, ins, outs, send_sems, recv_sems, i, k, theirs, theirs).wait_recv()

    return _Plan(sums, [jax.ShapeDtypeStruct(s.shape, s.dtype) for s in sums], {}, 3 * n, start, finish)


def _hosted_call(body, plan, *, name, grid, in_specs, out_specs, out_shape, scratch_shapes, args, sem):
    in_specs, out_specs, out_shape, scratch_shapes = list(in_specs), list(out_specs), list(out_shape), list(scratch_shapes)
    if plan is None:
        res = pl.pallas_call(body, name=name, grid=grid, in_specs=in_specs, out_specs=out_specs, out_shape=out_shape,
                             scratch_shapes=scratch_shapes, compiler_params=_params(*sem))(*args)
        return list(res), []
    n_in, n_out, n_scr = len(in_specs), len(out_specs), len(scratch_shapes)
    p_in, p_out = len(plan.ins), len(plan.out_shapes)

    def hosted(*refs):
        refs = list(refs)
        ins, pins = refs[:n_in], refs[n_in:n_in + p_in]
        outs = refs[n_in + p_in:n_in + p_in + n_out]
        pouts = refs[n_in + p_in + n_out:n_in + p_in + n_out + p_out]
        scr = refs[n_in + p_in + n_out + p_out:n_in + p_in + n_out + p_out + n_scr]
        send_sems, recv_sems = refs[-2:]
        place = _place()
        ids = [pl.program_id(d) for d in range(len(grid))]
        first = functools.reduce(jnp.logical_and, [i == 0 for i in ids])
        last = functools.reduce(jnp.logical_and, [i == g - 1 for i, g in zip(ids, grid)])

        @pl.when(first)
        def _():
            plan.start(place, pins, pouts, send_sems, recv_sems)

        body(*ins, *outs, *scr)

        @pl.when(last)
        def _():
            plan.finish(place, pins, pouts, send_sems, recv_sems)

    res = pl.pallas_call(
        hosted, name=name, grid=grid, in_specs=in_specs + [HBM] * p_in, out_specs=out_specs + [HBM] * p_out,
        out_shape=out_shape + plan.out_shapes,
        input_output_aliases={n_in + i: n_out + o for i, o in plan.aliases.items()},
        scratch_shapes=scratch_shapes + [pltpu.SemaphoreType.DMA((plan.n_sems,)), pltpu.SemaphoreType.DMA((plan.n_sems,))],
        compiler_params=_params(*(("arbitrary",) * len(grid))))(*args, *plan.ins)
    return list(res[:n_out]), list(res[n_out:])


def _pair_exchange(name, grads):
    n = len(grads)

    def body(*refs):
        ins, outs = refs[:n], refs[n:2 * n]
        send_sems, recv_sems = refs[2 * n:]
        x, y, c, _ = _place()
        cps = []
        for i in range(n):
            rows = ins[i].shape[1] // 2
            cp = pltpu.make_async_remote_copy(
                src_ref=ins[i].at[:, pl.ds((1 - c) * rows, rows), :], dst_ref=outs[i],
                send_sem=send_sems.at[i], recv_sem=recv_sems.at[i], device_id=(x, y, 1 - c), device_id_type=MESH)
            cp.start()
            cps.append(cp)
        for cp in cps:
            cp.wait()

    return pl.pallas_call(
        body, name=name, in_specs=[HBM] * n, out_specs=[HBM] * n,
        out_shape=[jax.ShapeDtypeStruct((N_CHIPS, g.shape[1] // 2, g.shape[2]), g.dtype) for g in grads],
        scratch_shapes=[pltpu.SemaphoreType.DMA((n,)), pltpu.SemaphoreType.DMA((n,))],
        )(*grads)


def _sibling_join(grads):
    n = len(grads)

    def body(*refs):
        outs = refs[n:2 * n]
        send_sems, recv_sems = refs[2 * n:]
        x, y, c, _ = _place()
        cps = []
        for i in range(n):
            cp = pltpu.make_async_remote_copy(
                src_ref=_half(outs[i], c), dst_ref=_half(outs[i], c), send_sem=send_sems.at[i], recv_sem=recv_sems.at[i],
                device_id=(x, y, 1 - c), device_id_type=MESH)
            cp.start()
            cps.append(cp)
        for i, cp in enumerate(cps):
            cp.wait_send()
            pltpu.make_async_remote_copy(
                src_ref=_half(outs[i], 1 - c), dst_ref=_half(outs[i], 1 - c), send_sem=send_sems.at[i],
                recv_sem=recv_sems.at[i], device_id=(x, y, 1 - c), device_id_type=MESH).wait_recv()

    return pl.pallas_call(
        body, name="sibling_join", in_specs=[HBM] * n, out_specs=[HBM] * n,
        out_shape=[jax.ShapeDtypeStruct(g.shape, g.dtype) for g in grads],
        input_output_aliases={i: i for i in range(n)},
        scratch_shapes=[pltpu.SemaphoreType.DMA((n,)), pltpu.SemaphoreType.DMA((n,))],
        )(*grads)


def _gather8(name, block, reduce):
    m, n = block.shape

    def body(x_ref, out_ref, *scratch):
        if reduce:
            all_ref, send_sems, recv_sems, local_sem = scratch
        else:
            all_ref = out_ref
            send_sems, recv_sems, local_sem = scratch
        x, y, c, chips = _place()
        me, sibling = (x, y, c), (x, y, 1 - c)

        def rows(px, py, pc):
            return all_ref.at[pl.ds((4 * px + 2 * py + pc) * m, m), :]

        def copy(k, blk, to, src=None):
            return pltpu.make_async_remote_copy(
                src_ref=rows(*blk) if src is None else src, dst_ref=rows(*blk),
                send_sem=send_sems.at[k], recv_sem=recv_sems.at[k], device_id=to, device_id_type=MESH)

        mine = pltpu.make_async_copy(x_ref, rows(*me), local_sem)
        mine.start()
        first = [copy(0, me, sibling, src=x_ref)]
        first += [copy(1 + j, me, (chip[0], chip[1], c), src=x_ref) for j, chip in enumerate(chips)]
        for cp in first:
            cp.start()
        passed = [copy(4 + j, (chip[0], chip[1], c), sibling) for j, chip in enumerate(chips)]
        for j, chip in enumerate(chips):
            copy(1 + j, (chip[0], chip[1], c), me).wait_recv()
            passed[j].start()
        copy(0, sibling, me).wait_recv()
        for j, chip in enumerate(chips):
            copy(4 + j, (chip[0], chip[1], 1 - c), me).wait_recv()
        for cp in first + passed:
            cp.wait_send()
        mine.wait()
        if reduce:
            acc = all_ref[pl.ds(0, m), :]
            for d in range(1, 8):
                acc = acc + all_ref[pl.ds(d * m, m), :]
            out_ref[...] = acc

    sems = [pltpu.SemaphoreType.DMA((7,)), pltpu.SemaphoreType.DMA((7,)), pltpu.SemaphoreType.DMA]
    scratch = ([pltpu.VMEM((8 * m, n), F32)] if reduce else []) + sems
    return pl.pallas_call(
        body, name=name,
        out_shape=jax.ShapeDtypeStruct((m, n) if reduce else (8 * m, n), F32),
        in_specs=[pl.BlockSpec(memory_space=pltpu.VMEM)], out_specs=pl.BlockSpec(memory_space=pltpu.VMEM),
        scratch_shapes=scratch)(block)


def _pad_rows(a, rows):
    return jnp.concatenate([a, jnp.zeros((rows - a.shape[0], a.shape[1]), a.dtype)], axis=0)


def kernel(x, p, norm_g, w_in_a, conv_w, conv_b, ln_g, ln_b, w_out_a, kv_norm_g, w_kv, k_norm_g, w_in_b, q_norm_g, w_out_b, ple_norm_g, w_ple_gate, w_ple_proj, loss_target, m_norm_g, m_w_in_a, m_conv_w, m_conv_b, m_ln_g, m_ln_b, m_w_out_a, m_kv_norm_g, m_w_kv, m_k_norm_g, m_w_in_b, m_q_norm_g, m_w_out_b, m_ple_norm_g, m_w_ple_gate, m_w_ple_proj, v_norm_g, v_w_in_a, v_conv_w, v_conv_b, v_ln_g, v_ln_b, v_w_out_a, v_kv_norm_g, v_w_kv, v_k_norm_g, v_w_in_b, v_q_norm_g, v_w_out_b, v_ple_norm_g, v_w_ple_gate, v_w_ple_proj):
    nb, seq, dm = x.shape
    t = nb * seq
    ple = p.shape[-1]
    ccs = conv_w.shape[-1]
    cc = N_CHIPS * ccs
    da = dm
    nheads = da // HEAD_DIM
    assert seq == DILATIONS[-1] * SPAN and da % 128 == 0 and ccs % 128 == 0

    core = lax.axis_index("c").astype(jnp.int32).reshape(1)
    chip = (2 * lax.axis_index("x") + lax.axis_index("y")).astype(jnp.int32)
    chip1 = chip.reshape(1)
    sum_order = jnp.concatenate([(chip1 + k) % N_CHIPS for k in range(N_CHIPS)] + [core])

    x2 = x.reshape(t, dm)
    tgt2 = loss_target.reshape(t, dm)
    p0 = p[0].reshape(t, ple)
    p1 = p[1].reshape(t, ple)

    big = [
        ("w_in_a", w_in_a[0], "col"), ("w_out_a", w_out_a[0], "row"), ("w_kv", w_kv, "col"),
        ("w_in_b", w_in_b[0], "col"), ("w_out_b", w_out_b[0], "row"),
        ("w_ple_gate0", w_ple_gate[0], "row"), ("w_ple_gate1", w_ple_gate[1], "row"),
        ("w_ple_proj0", w_ple_proj[0], "col"), ("w_ple_proj1", w_ple_proj[1], "col"),
    ]
    shard_shape = {nm: w.shape for nm, w, _ in big}
    names = [nm for nm, _, _ in big]
    own = [_cast_bf16("cast_" + nm, w, chip1) for nm, w, _ in big]
    W = {}

    vec_rows = 40
    small = _pad_rows(jnp.concatenate([conv_w[0], conv_b, ln_g, ln_b], axis=0), vec_rows)
    allv = _gather8("gather_conv_vectors", small, reduce=False).reshape(N_CHIPS, 2, vec_rows, ccs)[:, 0]
    allv = allv.transpose(1, 0, 2).reshape(vec_rows, cc)
    cw_full, cb_full, lg_full, lb_full = allv[:HALO], allv[31:32], allv[32:33], allv[33:34]
    cw_full = cw_full * (lax.broadcasted_iota(jnp.int32, (HALO, 1), 0) < CONV_WIDTH).astype(F32)
    tables = _rope_tables(seq)
    gain_q = jnp.tile(q_norm_g[0][:, None, :], (1, nheads, 1)).reshape(1, 3 * da)
    gain_k = jnp.tile(k_norm_g[None, :], (1, nheads))
    g0, g1 = norm_g[0:1], norm_g[1:2]
    pg0, pg1 = ple_norm_g[0:1], ple_norm_g[1:2]
    kvg = kv_norm_g[None, :]

    (u0,), (W[names[0]],) = _rms_fwd("rms_u0", x2, [g0], plan=_gather_plan(own[:1]))
    pa, gathered = _mm_nn("mm_in_a", u0, W["w_in_a"], "col", out_dtype=BF16, plan=_gather_plan(own[1:]))
    W.update(zip(names[1:], gathered))
    for nm in ("w_ple_proj0", "w_ple_proj1"):
        W[nm] = W[nm].transpose(1, 0, 2).reshape(1, W[nm].shape[1], -1)
    conv_out, m_a = _mixa_fwd("mixa_fwd", pa, cw_full, cb_full, lg_full, lb_full, seq)
    h0, r0 = _mm_nn("mm_out_a", m_a, W["w_out_a"], "row", resid=x2, norm_gain=pg0)
    gpre0 = _mm_nn("mm_gate0", r0, W["w_ple_gate0"], "row", out_dtype=BF16)
    pp0 = _mm_nn("mm_proj0", p0, W["w_ple_proj0"], "col", out_dtype=BF16)
    x1, kvn, u1 = _ple_fwd("ple_fwd0", h0, gpre0, pp0, [kvg, g1])
    kv, kn = _mm_nn("mm_kv", kvn, W["w_kv"], "col", heads=(gain_k, tables, da, seq))
    pb, qn = _mm_nn("mm_in_b", u1, W["w_in_b"], "col", out_dtype=BF16, heads=(gain_q, tables, 3 * da, seq))
    o, lse, m_b = _attn_fwd("attn_fwd", qn, kn, kv, pb, seq)
    h1, r1 = _mm_nn("mm_out_b", m_b, W["w_out_b"], "row", resid=x1, norm_gain=pg1)
    gpre1 = _mm_nn("mm_gate1", r1, W["w_ple_gate1"], "row", out_dtype=BF16)
    pp1 = _mm_nn("mm_proj1", p1, W["w_ple_proj1"], "col", out_dtype=BF16)
    dy, dgp1, dpp1, sq = _ple_loss("ple_loss", h1, gpre1, pp1, tgt2)
    loss = lax.psum(0.5 * sq[0, 0] / dm, ("x", "y", "c"))

    G = {}
    G["w_ple_gate1"] = _mm_tn("tn_gate1", r1, dgp1, "row", shard_shape["w_ple_gate1"])
    G["w_ple_proj1"] = _mm_tn("tn_proj1", p1, dpp1, "col", shard_shape["w_ple_proj1"], whole=True)
    dr1 = _mm_nt("nt_gate1", dgp1, W["w_ple_gate1"], "row", out_dtype=BF16)
    dh1, (dpg1,) = _rms_bwd("rms_bwd_r1", h1, dy, [(pg1, dr1)])
    G["w_out_b"] = _mm_tn("tn_out_b", m_b, dh1, "row", shard_shape["w_out_b"])
    dm_b = _mm_nt("nt_out_b", dh1, W["w_out_b"], "row", out_dtype=BF16)
    d_o, dgt, dsum = _gate_bwd("gate_bwd", dm_b, o, pb)
    dq0, dq1, dq2, dk, dv = _attn_bwd("attn_bwd", qn, kn, kv, d_o, lse, dsum, seq)
    dpb, dgq = _q_bwd("q_bwd", pb, gain_q, tables, [dq0, dq1, dq2], dgt, seq)
    dkv, dgk = _k_bwd("k_bwd", kv, gain_k, tables, dk, dv, seq)
    G["w_in_b"] = _mm_tn("tn_in_b", u1, dpb, "col", shard_shape["w_in_b"])
    du1 = _mm_nt("nt_in_b", dpb, W["w_in_b"], "col", out_dtype=BF16)
    G["w_kv"] = _mm_tn("tn_kv", kvn, dkv, "col", shard_shape["w_kv"])
    dkvn = _mm_nt("nt_kv", dkv, W["w_kv"], "col", out_dtype=BF16)
    dx1, (dg1, dkvg), dgp0, dpp0 = _rms_bwd("rms_bwd_x1", x1, dh1, [(g1, du1), (kvg, dkvn)], ple=(gpre0, pp0))
    G["w_ple_gate0"] = _mm_tn("tn_gate0", r0, dgp0, "row", shard_shape["w_ple_gate0"])
    G["w_ple_proj0"] = _mm_tn("tn_proj0", p0, dpp0, "col", shard_shape["w_ple_proj0"], whole=True)
    dr0 = _mm_nt("nt_gate0", dgp0, W["w_ple_gate0"], "row", out_dtype=BF16)
    dh0, (dpg0,) = _rms_bwd("rms_bwd_r0", h0, dx1, [(pg0, dr0)])
    G["w_out_a"] = _mm_tn("tn_out_a", m_a, dh0, "row", shard_shape["w_out_a"])
    dm_a = _mm_nt("nt_out_a", dh0, W["w_out_a"], "row", out_dtype=BF16)

    def pair_sums(tag, batch):
        recv = _pair_exchange("pair_exchange_" + tag, [G[nm] for nm in batch])
        return [_pair_sum("pair_sum_" + nm, G[nm], rc, core) for nm, rc in zip(batch, recv)]

    layer1 = ["w_kv", "w_in_b", "w_out_b", "w_ple_gate1", "w_ple_proj1"]
    layer0 = ["w_in_a", "w_out_a", "w_ple_gate0", "w_ple_proj0"]
    sums1 = pair_sums("layer1", layer1)
    (dc, dz, dlg, dlb, dcb), parts1 = _mixa_bwd1("mixa_bwd1", conv_out, pa, dm_a, lg_full, lb_full,
                                                 plan=_scatter_plan(sums1))
    dpa, dcw = _mixa_bwd2("mixa_bwd2", dc, pa, dz, cw_full, seq)
    G["w_in_a"] = _mm_tn("tn_in_a", u0, dpa, "col", shard_shape["w_in_a"])
    sums0 = pair_sums("layer0", layer0)
    du0, parts0 = _mm_nt("nt_in_a", dpa, W["w_in_a"], "col", out_dtype=BF16, plan=_scatter_plan(sums0))
    dx, (dg0,) = _rms_bwd("rms_bwd_x", x2, dh0, [(g0, du0)])
    grad_x = dx.reshape(nb, seq, dm)

    sums = dict(zip(layer1 + layer0, sums1 + sums0))
    parts = dict(zip(layer1 + layer0, parts1 + parts0))
    halves = [_chip_sum("chip_sum_" + nm, sums[nm], parts[nm], sum_order) for nm in names]
    gfull = dict(zip(names, _sibling_join(halves)))

    def as_rows(a):
        return a.reshape(-1, dm)

    small_parts = [as_rows(dcw), as_rows(dcb), as_rows(dlg), as_rows(dlb), dg0, dg1, dkvg, dpg0, dpg1, as_rows(dgk), as_rows(dgq)]
    counts = [a.shape[0] for a in small_parts]
    total = sum(counts)
    packed = _pad_rows(jnp.concatenate(small_parts, axis=0), -(-total // 8) * 8)
    red = _gather8("reduce_small", packed, reduce=True)
    pieces, off = [], 0
    for n_ in counts:
        pieces.append(red[off:off + n_])
        off += n_
    r_dcw, r_dcb, r_dlg, r_dlb, r_g0, r_g1, r_kvg, r_pg0, r_pg1, r_gk, r_gq = pieces
    my_cols = lambda a: lax.dynamic_slice_in_dim(a.reshape(-1, cc), chip * ccs, ccs, axis=1)
    small_grads = {
        "norm_g": jnp.concatenate([r_g0, r_g1], axis=0),
        "conv_w": my_cols(r_dcw)[:CONV_WIDTH],
        "conv_b": my_cols(r_dcb), "ln_g": my_cols(r_dlg), "ln_b": my_cols(r_dlb),
        "kv_norm_g": r_kvg,
        "k_norm_g": r_gk.reshape(nheads, HEAD_DIM).sum(axis=0, keepdims=True),
        "q_norm_g": r_gq.reshape(3, nheads, HEAD_DIM).sum(axis=1),
        "ple_norm_g": jnp.concatenate([r_pg0, r_pg1], axis=0),
    }

    given = dict(norm_g=norm_g, w_in_a=w_in_a, conv_w=conv_w, conv_b=conv_b, ln_g=ln_g, ln_b=ln_b, w_out_a=w_out_a,
                 kv_norm_g=kv_norm_g, w_kv=w_kv, k_norm_g=k_norm_g, w_in_b=w_in_b, q_norm_g=q_norm_g, w_out_b=w_out_b,
                 ple_norm_g=ple_norm_g, w_ple_gate=w_ple_gate, w_ple_proj=w_ple_proj)
    mom1 = dict(norm_g=m_norm_g, w_in_a=m_w_in_a, conv_w=m_conv_w, conv_b=m_conv_b, ln_g=m_ln_g, ln_b=m_ln_b,
                w_out_a=m_w_out_a, kv_norm_g=m_kv_norm_g, w_kv=m_w_kv, k_norm_g=m_k_norm_g, w_in_b=m_w_in_b,
                q_norm_g=m_q_norm_g, w_out_b=m_w_out_b, ple_norm_g=m_ple_norm_g, w_ple_gate=m_w_ple_gate,
                w_ple_proj=m_w_ple_proj)
    mom2 = dict(norm_g=v_norm_g, w_in_a=v_w_in_a, conv_w=v_conv_w, conv_b=v_conv_b, ln_g=v_ln_g, ln_b=v_ln_b,
                w_out_a=v_w_out_a, kv_norm_g=v_kv_norm_g, w_kv=v_w_kv, k_norm_g=v_k_norm_g, w_in_b=v_w_in_b,
                q_norm_g=v_q_norm_g, w_out_b=v_w_out_b, ple_norm_g=v_ple_norm_g, w_ple_gate=v_w_ple_gate,
                w_ple_proj=v_w_ple_proj)
    order = ["norm_g", "w_in_a", "conv_w", "conv_b", "ln_g", "ln_b", "w_out_a", "kv_norm_g", "w_kv", "k_norm_g", "w_in_b",
             "q_norm_g", "w_out_b", "ple_norm_g", "w_ple_gate", "w_ple_proj"]
    grads, deltas, new_m, new_v = {}, {}, {}, {}
    for nm in order:
        shape = given[nm].shape
        if nm in ("w_ple_gate", "w_ple_proj"):
            g2 = jnp.concatenate([gfull[nm + "0"], gfull[nm + "1"]], axis=0)
        elif nm in gfull:
            g2 = gfull[nm]
        else:
            g2 = small_grads[nm]
        two_d = g2.shape
        d2, m2, v2 = _adamw("adamw_" + nm, given[nm].reshape(two_d), g2, mom1[nm].reshape(two_d), mom2[nm].reshape(two_d))
        grads[nm], deltas[nm], new_m[nm], new_v[nm] = (a.reshape(shape) for a in (g2, d2, m2, v2))

    return (loss, grad_x, *[grads[n_] for n_ in order], *[deltas[n_] for n_ in order],
            *[new_m[n_] for n_ in order], *[new_v[n_] for n_ in order])
```

```python
import functools

import jax
import jax.numpy as jnp
from jax import lax
from jax.experimental import pallas as pl
from jax.experimental.pallas import tpu as pltpu

F32 = jnp.float32
BF16 = jnp.bfloat16
MESH = pl.DeviceIdType.MESH

EPS = 1e-6
NEG_INF = -1e30
HEAD_DIM = 64
ROPE_DIM = 16
ROPE_THETA = 500000.0
CONV_WIDTH = 31
SUBLANES = 8
CONV_ROWS = 64
HALO = 32
SPAN = 128
DILATIONS = (1, 4, 16)
ADAM_LR, ADAM_B1, ADAM_B2, ADAM_EPS, ADAM_WD, ADAM_STEP = 0.001, 0.9, 0.999, 1e-08, 0.01, 10
N_CHIPS = 4
VMEM_LIMIT = 56 * 1024 * 1024


def _tile(n, target, mult=128):
    best = None
    t = mult
    while t <= min(n, target):
        if n % t == 0:
            best = t
        t += mult
    return best if best is not None else n


def _params(*sem):
    return pltpu.CompilerParams(dimension_semantics=tuple(sem) if sem else None, vmem_limit_bytes=VMEM_LIMIT)


def _sigmoid(x):
    return 0.5 * jnp.tanh(0.5 * x) + 0.5


def _seg_allsum64(x, terms=1):
    tr, w = x.shape
    cw = 256 if w % 256 == 0 else 128
    n = w // cw
    shift = HEAD_DIM.bit_length() - 1
    ri = lax.shift_right_logical(lax.broadcasted_iota(jnp.int32, (cw, cw), 0), shift)
    ci = lax.shift_right_logical(lax.broadcasted_iota(jnp.int32, (cw, cw), 1), shift)
    ones = (ri == ci).astype(BF16)

    def stack(v):
        return jnp.concatenate([v[:, j * cw:(j + 1) * cw] for j in range(n)], axis=0)

    hi = x.astype(BF16)
    s = jnp.dot(stack(hi), ones, preferred_element_type=F32)
    if terms == 2:
        lo = (x - hi.astype(F32)).astype(BF16)
        s = s + jnp.dot(stack(lo), ones, preferred_element_type=F32)
    return jnp.concatenate([s[j * tr:(j + 1) * tr] for j in range(n)], axis=1)


def _colsum(x):
    return jnp.sum(x, axis=0, keepdims=True)


def _shards_view(w, kind):
    return w if kind == "col" else w.reshape(1, -1, w.shape[2])


def _mm_nn(name, a, w, kind, *, out_dtype=F32, resid=None, norm_gain=None, heads=None, plan=None):
    t = a.shape[0]
    w = _shards_view(w, kind)
    ns, k, c = w.shape
    n = ns * c
    tm = _tile(t, 1024 if norm_gain is None else 512, 8)
    tk = _tile(k, 2048)
    tn = _tile(c, 1024)
    nk = k // tk
    per = c // tn
    assert norm_gain is None or tn == n, "the fused RMSNorm needs whole rows in one tile"
    assert norm_gain is None or heads is None
    n_in = 2 + (resid is not None) + (norm_gain is not None) + (4 if heads is not None else 0)
    if heads is not None:
        h_gain, h_tables, h_width, h_seq = heads
        assert h_width % tn == 0 and h_seq % tm == 0
        h_blocks = h_width // tn

    def body(*refs):
        a_ref, w_ref = refs[:2]
        r_ref = refs[2] if resid is not None else None
        g_ref = refs[n_in - 1] if norm_gain is not None else None
        o_ref = refs[n_in]
        part = jnp.dot(a_ref[...].astype(BF16), w_ref[...], preferred_element_type=F32)

        def finish(out):
            if resid is not None:
                out = out + r_ref[...]
            stored = out.astype(out_dtype)
            o_ref[...] = stored
            if norm_gain is not None:
                normed = out * lax.rsqrt(jnp.mean(out * out, axis=-1, keepdims=True) + EPS) * g_ref[...]
                refs[n_in + 1][...] = normed.astype(BF16)
            if heads is not None:
                hg_ref, tc_ref, ta_ref, tb_ref = refs[n_in - 4:n_in]

                @pl.when(pl.program_id(1) < h_blocks)
                def _():
                    refs[n_in + 1][...] = _hnr_fwd_math(stored.astype(F32), hg_ref[...], tc_ref, ta_ref, tb_ref)

        if nk == 1:
            finish(part)
            return
        acc = refs[-1]
        kk = pl.program_id(2)

        @pl.when(kk == 0)
        def _():
            acc[...] = part

        @pl.when(kk > 0)
        def _():
            acc[...] += part

        @pl.when(kk == nk - 1)
        def _():
            finish(acc[...])

    in_specs = [pl.BlockSpec((tm, tk), lambda i, j, kk: (i, kk)),
                pl.BlockSpec((None, tk, tn), lambda i, j, kk: (j // per, kk, j % per))]
    args = [a, w]
    if resid is not None:
        in_specs.append(pl.BlockSpec((tm, tn), lambda i, j, kk: (i, j)))
        args.append(resid)
    out_specs = [pl.BlockSpec((tm, tn), lambda i, j, kk: (i, j))]
    out_shape = [jax.ShapeDtypeStruct((t, n), out_dtype)]
    if norm_gain is not None:
        in_specs.append(pl.BlockSpec((1, n), lambda i, j, kk: (0, 0)))
        args.append(norm_gain)
        out_specs.append(pl.BlockSpec((tm, tn), lambda i, j, kk: (i, j)))
        out_shape.append(jax.ShapeDtypeStruct((t, n), BF16))
    if heads is not None:
        last = h_blocks - 1
        per_seq = h_seq // tm
        tab = pl.BlockSpec((tm, 128), lambda i, j, kk: (i % per_seq, 0))
        in_specs += [pl.BlockSpec((1, tn), lambda i, j, kk: (0, jnp.minimum(j, last))), tab, tab, tab]
        args += [h_gain, *h_tables]
        out_specs.append(pl.BlockSpec((tm, tn), lambda i, j, kk: (i, jnp.minimum(j, last))))
        out_shape.append(jax.ShapeDtypeStruct((t, h_width), F32))
    outs, carried = _hosted_call(
        body, plan, name=name, grid=(t // tm, n // tn, nk), in_specs=in_specs, out_specs=out_specs, out_shape=out_shape,
        scratch_shapes=[pltpu.VMEM((tm, tn), F32)] if nk > 1 else [],
        args=args, sem=("parallel", "arbitrary", "arbitrary"))
    out = outs[0] if len(outs) == 1 else tuple(outs)
    return out if plan is None else (out, carried)


def _mm_nt(name, d, w, kind, *, out_dtype=F32, plan=None):
    t = d.shape[0]
    w = _shards_view(w, kind)
    ns, k, c = w.shape
    n = ns * c
    tm = _tile(t, 1024, 8)
    to = _tile(k, 1024)
    tc = _tile(c, 1536)
    nc = n // tc
    per = c // tc

    def body(d_ref, w_ref, o_ref, *scratch):
        part = lax.dot_general(d_ref[...].astype(BF16), w_ref[...], (((1,), (1,)), ((), ())),
                               preferred_element_type=F32)
        if nc == 1:
            o_ref[...] = part.astype(out_dtype)
            return
        acc = scratch[0]
        kk = pl.program_id(2)

        @pl.when(kk == 0)
        def _():
            acc[...] = part

        @pl.when(kk > 0)
        def _():
            acc[...] += part

        @pl.when(kk == nc - 1)
        def _():
            o_ref[...] = acc[...].astype(out_dtype)

    (out,), carried = _hosted_call(
        body, plan, name=name, grid=(t // tm, k // to, nc),
        in_specs=[pl.BlockSpec((tm, tc), lambda i, j, kk: (i, kk)),
                  pl.BlockSpec((None, to, tc), lambda i, j, kk: (kk // per, j, kk % per))],
        out_specs=[pl.BlockSpec((tm, to), lambda i, j, kk: (i, j))],
        out_shape=[jax.ShapeDtypeStruct((t, k), out_dtype)],
        scratch_shapes=[pltpu.VMEM((tm, to), F32)] if nc > 1 else [],
        args=[d, w], sem=("parallel", "parallel", "arbitrary"))
    return out if plan is None else (out, carried)


def _mm_tn(name, a, d, kind, shard_shape, whole=False):
    t, k = a.shape
    n = d.shape[1]
    ns = N_CHIPS if kind == "col" and not whole else 1
    c = n // ns
    tkm = _tile(k, 1024)
    tn = _tile(c, 1536)
    tt = _tile(t, 1024, 8)
    nt = t // tt
    per = c // tn

    def body(a_ref, d_ref, o_ref, acc):
        kk = pl.program_id(2)
        part = lax.dot_general(a_ref[...].astype(BF16), d_ref[...].astype(BF16), (((0,), (0,)), ((), ())),
                               preferred_element_type=F32)

        @pl.when(kk == 0)
        def _():
            acc[...] = part

        @pl.when(kk > 0)
        def _():
            acc[...] += part

        @pl.when(kk == nt - 1)
        def _():
            o_ref[...] = acc[...].astype(BF16)

    out = pl.pallas_call(
        body, name=name, grid=(k // tkm, n // tn, nt),
        in_specs=[pl.BlockSpec((tt, tkm), lambda i, j, kk: (kk, i)),
                  pl.BlockSpec((tt, tn), lambda i, j, kk: (kk, j))],
        out_specs=pl.BlockSpec((None, tkm, tn), lambda i, j, kk: (j // per, i, j % per)),
        out_shape=jax.ShapeDtypeStruct((ns, k, c), BF16),
        scratch_shapes=[pltpu.VMEM((tkm, tn), F32)],
        compiler_params=_params("parallel", "parallel", "arbitrary"))(a, d)
    if kind == "col" and whole:
        return out.reshape(k, N_CHIPS, n // N_CHIPS).transpose(1, 0, 2)
    return out.reshape((N_CHIPS,) + tuple(shard_shape))


def _row_spec(tr, w, col=0):
    return pl.BlockSpec((tr, w), lambda i: (i, col))


def _full_spec(shape):
    return pl.BlockSpec(shape, lambda i: tuple(0 for _ in shape))


def _rms_fwd(name, x, gains, plan=None):
    t, dm = x.shape
    tr = _tile(t, 256, 8)
    n = len(gains)

    def body(x_ref, *refs):
        xv = x_ref[...]
        xn = xv * lax.rsqrt(jnp.mean(xv * xv, axis=-1, keepdims=True) + EPS)
        for g_ref, o_ref in zip(refs[:n], refs[n:]):
            o_ref[...] = (xn * g_ref[...]).astype(BF16)

    outs, carried = _hosted_call(
        body, plan, name=name, grid=(t // tr,),
        in_specs=[_row_spec(tr, dm)] + [_full_spec((1, dm))] * n,
        out_specs=[_row_spec(tr, dm)] * n,
        out_shape=[jax.ShapeDtypeStruct((t, dm), BF16)] * n,
        scratch_shapes=[], args=[x, *gains], sem=("parallel",))
    return outs if plan is None else (outs, carried)


def _ple_bwd_math(dy, gpre, pp):
    sg = _sigmoid(gpre)
    return (dy * pp * sg * (1.0 - sg)).astype(BF16), (dy * sg).astype(BF16)


def _rms_bwd(name, x, resid, pairs, ple=None):
    t, dm = x.shape
    tr = _tile(t, 256, 8)
    n = len(pairs)
    n_in = 2 * n + (2 if ple is not None else 0)

    def body(x_ref, r_ref, *refs):
        ins, outs = refs[:n_in], refs[n_in:]
        i = pl.program_id(0)
        xv = x_ref[...]
        rs = lax.rsqrt(jnp.mean(xv * xv, axis=-1, keepdims=True) + EPS)
        xn = xv * rs
        total = r_ref[...]
        for kx in range(n):
            g_ref, du_ref = ins[2 * kx], ins[2 * kx + 1]
            dg_ref = outs[1 + kx]
            du = du_ref[...].astype(F32)

            @pl.when(i == 0)
            def _():
                dg_ref[...] = jnp.zeros_like(dg_ref)

            dg_ref[...] += _colsum(du * xn)
            dxh = du * g_ref[...]
            total = total + rs * (dxh - xn * jnp.mean(dxh * xn, axis=-1, keepdims=True))
        outs[0][...] = total
        if ple is not None:
            outs[1 + n][...], outs[2 + n][...] = _ple_bwd_math(total, ins[2 * n][...].astype(F32), ins[2 * n + 1][...].astype(F32))

    in_specs = [_row_spec(tr, dm), _row_spec(tr, dm)]
    args = [x, resid]
    for g, du in pairs:
        in_specs += [_full_spec((1, dm)), _row_spec(tr, dm)]
        args += [g, du]
    out_specs = [_row_spec(tr, dm)] + [_full_spec((1, dm))] * n
    out_shape = [jax.ShapeDtypeStruct((t, dm), F32)] + [jax.ShapeDtypeStruct((1, dm), F32)] * n
    if ple is not None:
        in_specs += [_row_spec(tr, dm)] * 2
        args += list(ple)
        out_specs += [_row_spec(tr, dm)] * 2
        out_shape += [jax.ShapeDtypeStruct((t, dm), BF16)] * 2
    outs = pl.pallas_call(
        body, name=name, grid=(t // tr,), in_specs=in_specs, out_specs=out_specs, out_shape=out_shape,
        compiler_params=_params("arbitrary"))(*args)
    if ple is not None:
        return outs[0], list(outs[1:1 + n]), outs[1 + n], outs[2 + n]
    return outs[0], list(outs[1:])


def _shifted_copies(ext, sh, rows):
    for s in range(1, SUBLANES):
        sh[s - 1] = ext[pl.ds(s, rows), :]


def _window(ext, sh, off, row0, rows, lanes):
    s = off % SUBLANES
    src = ext if s == 0 else sh.at[s - 1]
    return src[pl.ds(off - s + row0, rows), lanes]


def _mixa_fwd(name, pa, cw, cb, lg, lb, seq):
    t, w3 = pa.shape
    cc = w3 // 3
    tr = _tile(seq, 128, HALO)
    per_seq = seq // tr
    hb = tr // HALO
    lead = HALO - (CONV_WIDTH - 1)

    def body(a_ref, b_ref, z_ref, ah_ref, bh_ref, cw_ref, cb_ref, lg_ref, lb_ref, c_ref, m_ref, ext, sh, conv):
        i = pl.program_id(0)
        gh = ah_ref[...].astype(F32) * _sigmoid(bh_ref[...].astype(F32))
        ext[pl.ds(0, HALO), :] = jnp.where((i % per_seq) == 0, 0.0, gh)
        ext[pl.ds(HALO, tr), :] = a_ref[...].astype(F32) * _sigmoid(b_ref[...].astype(F32))
        _shifted_copies(ext, sh, tr + HALO - SUBLANES)
        for lc in range(cc // 128):
            lanes = pl.ds(lc * 128, 128)
            taps = [cw_ref[pl.ds(k, 1), lanes] for k in range(CONV_WIDTH)]
            for row0 in range(0, tr, CONV_ROWS):
                acc = jnp.broadcast_to(cb_ref[:, lanes], (CONV_ROWS, 128))
                for k in range(CONV_WIDTH):
                    acc = acc + _window(ext, sh, lead + k, row0, CONV_ROWS, lanes) * taps[k]
                conv[pl.ds(row0, CONV_ROWS), lanes] = acc
        acc = conv[...]
        c_ref[...] = acc.astype(BF16)
        xc = acc - jnp.mean(acc, axis=-1, keepdims=True)
        nrm = xc * lax.rsqrt(jnp.mean(xc * xc, axis=-1, keepdims=True) + EPS)
        l = nrm * lg_ref[...] + lb_ref[...]
        z = z_ref[...].astype(F32)
        m_ref[...] = (l * _sigmoid(l) * z * _sigmoid(z)).astype(BF16)

    halo = lambda col: pl.BlockSpec((HALO, cc), lambda i: (jnp.maximum(i * hb - 1, 0), col))
    return pl.pallas_call(
        body, name=name, grid=(t // tr,),
        in_specs=[_row_spec(tr, cc, 0), _row_spec(tr, cc, 1), _row_spec(tr, cc, 2), halo(0), halo(1),
                  _full_spec((HALO, cc)), _full_spec((1, cc)), _full_spec((1, cc)), _full_spec((1, cc))],
        out_specs=[_row_spec(tr, cc), _row_spec(tr, cc)],
        out_shape=[jax.ShapeDtypeStruct((t, cc), BF16), jax.ShapeDtypeStruct((t, cc), BF16)],
        scratch_shapes=[pltpu.VMEM((tr + HALO, cc), F32), pltpu.VMEM((SUBLANES - 1, tr + HALO - SUBLANES, cc), F32),
                        pltpu.VMEM((tr, cc), F32)],
        compiler_params=_params("parallel"))(pa, pa, pa, pa, pa, cw, cb, lg, lb)


def _mixa_bwd1(name, c, pa, dm, lg, lb, plan=None):
    t, cc = c.shape
    tr = _tile(t, 128, 8)

    def body(c_ref, z_ref, dm_ref, lg_ref, lb_ref, dc_ref, dz_ref, dlg_ref, dlb_ref, dcb_ref):
        i = pl.program_id(0)
        cv = c_ref[...].astype(F32)
        xc = cv - jnp.mean(cv, axis=-1, keepdims=True)
        rs = lax.rsqrt(jnp.mean(xc * xc, axis=-1, keepdims=True) + EPS)
        nrm = xc * rs
        l = nrm * lg_ref[...] + lb_ref[...]
        z = z_ref[...].astype(F32)
        sl, sz = _sigmoid(l), _sigmoid(z)
        dmv = dm_ref[...].astype(F32)
        ds = dmv * (z * sz)
        dzz = dmv * (l * sl)
        dz_ref[...] = (dzz * (sz * (1.0 + z * (1.0 - sz)))).astype(BF16)
        dl = ds * (sl * (1.0 + l * (1.0 - sl)))
        dn = dl * lg_ref[...]
        dc = rs * (dn - jnp.mean(dn, axis=-1, keepdims=True) - nrm * jnp.mean(dn * nrm, axis=-1, keepdims=True))
        dc_ref[...] = dc.astype(BF16)

        @pl.when(i == 0)
        def _():
            dlg_ref[...] = jnp.zeros_like(dlg_ref)
            dlb_ref[...] = jnp.zeros_like(dlb_ref)
            dcb_ref[...] = jnp.zeros_like(dcb_ref)

        dlg_ref[...] += _colsum(dl * nrm)
        dlb_ref[...] += _colsum(dl)
        dcb_ref[...] += _colsum(dc)

    vec = jax.ShapeDtypeStruct((1, cc), F32)
    outs, carried = _hosted_call(
        body, plan, name=name, grid=(t // tr,),
        in_specs=[_row_spec(tr, cc), _row_spec(tr, cc, 2), _row_spec(tr, cc), _full_spec((1, cc)), _full_spec((1, cc))],
        out_specs=[_row_spec(tr, cc), _row_spec(tr, cc)] + [_full_spec((1, cc))] * 3,
        out_shape=[jax.ShapeDtypeStruct((t, cc), BF16), jax.ShapeDtypeStruct((t, cc), BF16), vec, vec, vec],
        scratch_shapes=[], args=[c, pa, dm, lg, lb], sem=("arbitrary",))
    return outs if plan is None else (outs, carried)


def _mixa_bwd2(name, dc, pa, dz, cw, seq):
    t, cc = dc.shape
    tr = _tile(seq, 128, HALO)
    per_seq = seq // tr
    hb = tr // HALO
    steps = t // tr
    last_halo = t // HALO - 1

    def body(dc_ref, dcn_ref, a_ref, b_ref, dz_ref, cw_ref, dp_ref, dcw_ref, ext, sh, sums):
        i = pl.program_id(0)
        ext[pl.ds(0, tr), :] = dc_ref[...].astype(F32)
        ext[pl.ds(tr, HALO), :] = jnp.where((i % per_seq) == per_seq - 1, 0.0, dcn_ref[...].astype(F32))
        _shifted_copies(ext, sh, tr + HALO - SUBLANES)

        @pl.when(i == 0)
        def _():
            sums[...] = jnp.zeros_like(sums)
            dcw_ref[...] = jnp.zeros_like(dcw_ref)

        av = a_ref[...].astype(F32)
        sb = _sigmoid(b_ref[...].astype(F32))
        glu = av * sb
        dglu = jnp.zeros((tr, cc), F32)
        for k in range(CONV_WIDTH):
            wd = _window(ext, sh, CONV_WIDTH - 1 - k, 0, tr, slice(None))
            dglu = dglu + wd * cw_ref[pl.ds(k, 1), :]
            sums[pl.ds(k * SUBLANES, SUBLANES), :] += (wd * glu).reshape(tr // SUBLANES, SUBLANES, cc).sum(axis=0)
        dp_ref[:, pl.ds(0, cc)] = (dglu * sb).astype(BF16)
        dp_ref[:, pl.ds(cc, cc)] = (dglu * av * sb * (1.0 - sb)).astype(BF16)
        dp_ref[:, pl.ds(2 * cc, cc)] = dz_ref[...]

        @pl.when(i == steps - 1)
        def _():
            for k in range(CONV_WIDTH):
                dcw_ref[pl.ds(k, 1), :] = _colsum(sums[pl.ds(k * SUBLANES, SUBLANES), :])

    nxt = pl.BlockSpec((HALO, cc), lambda i: (jnp.minimum((i + 1) * hb, last_halo), 0))
    return pl.pallas_call(
        body, name=name, grid=(steps,),
        in_specs=[_row_spec(tr, cc), nxt, _row_spec(tr, cc, 0), _row_spec(tr, cc, 1), _row_spec(tr, cc),
                  _full_spec((HALO, cc))],
        out_specs=[_row_spec(tr, 3 * cc), _full_spec((HALO, cc))],
        out_shape=[jax.ShapeDtypeStruct((t, 3 * cc), BF16), jax.ShapeDtypeStruct((HALO, cc), F32)],
        scratch_shapes=[pltpu.VMEM((tr + HALO, cc), F32), pltpu.VMEM((SUBLANES - 1, tr + HALO - SUBLANES, cc), F32),
                        pltpu.VMEM((HALO * SUBLANES, cc), F32)],
        compiler_params=_params("arbitrary"))(dc, dc, pa, pa, dz, cw)


def _ple_fwd(name, h, gpre, pp, gains):
    t, dm = h.shape
    tr = _tile(t, 256, 8)

    n = len(gains)

    def body(h_ref, g_ref, p_ref, *refs):
        o_ref = refs[n]
        xv = h_ref[...] + _sigmoid(g_ref[...].astype(F32)) * p_ref[...].astype(F32)
        o_ref[...] = xv
        xn = xv * lax.rsqrt(jnp.mean(xv * xv, axis=-1, keepdims=True) + EPS)
        for gain_ref, n_ref in zip(refs[:n], refs[n + 1:]):
            n_ref[...] = (xn * gain_ref[...]).astype(BF16)

    return pl.pallas_call(
        body, name=name, grid=(t // tr,), in_specs=[_row_spec(tr, dm)] * 3 + [_full_spec((1, dm))] * n,
        out_specs=[_row_spec(tr, dm)] * (1 + n),
        out_shape=[jax.ShapeDtypeStruct((t, dm), F32)] + [jax.ShapeDtypeStruct((t, dm), BF16)] * n,
        compiler_params=_params("parallel"))(h, gpre, pp, *gains)


def _ple_loss(name, h, gpre, pp, target):
    t, dm = h.shape
    tr = _tile(t, 256, 8)

    def body(h_ref, g_ref, p_ref, t_ref, dy_ref, dg_ref, dp_ref, sq_ref):
        i = pl.program_id(0)
        gpre_v, pp_v = g_ref[...].astype(F32), p_ref[...].astype(F32)
        err = h_ref[...] + _sigmoid(gpre_v) * pp_v - t_ref[...]
        dy = err * (1.0 / dm)
        dy_ref[...] = dy
        dg_ref[...], dp_ref[...] = _ple_bwd_math(dy, gpre_v, pp_v)

        @pl.when(i == 0)
        def _():
            sq_ref[...] = jnp.zeros_like(sq_ref)

        sq_ref[...] += jnp.sum(jnp.sum(err * err, axis=1, keepdims=True), axis=0, keepdims=True)

    return pl.pallas_call(
        body, name=name, grid=(t // tr,), in_specs=[_row_spec(tr, dm)] * 4,
        out_specs=[_row_spec(tr, dm)] * 3 + [_full_spec((1, 1))],
        out_shape=[jax.ShapeDtypeStruct((t, dm), F32)] + [jax.ShapeDtypeStruct((t, dm), BF16)] * 2
        + [jax.ShapeDtypeStruct((1, 1), F32)],
        compiler_params=_params("arbitrary"))(h, gpre, pp, target)


def _rope_tables(seq):
    half = ROPE_DIM // 2
    inv = ROPE_THETA ** (-jnp.arange(half, dtype=F32) * (2.0 / ROPE_DIM))
    ang = jnp.arange(seq).astype(F32)[:, None] * inv[None, :]
    cos, sin = jnp.cos(ang), jnp.sin(ang)
    rest = HEAD_DIM - ROPE_DIM
    one = jnp.ones((seq, rest), F32)
    zero = jnp.zeros((seq, rest), F32)
    zh = jnp.zeros((seq, half), F32)
    tc = jnp.concatenate([cos, cos, one], axis=1)
    ta = jnp.concatenate([-sin, zh, zero], axis=1)
    tb = jnp.concatenate([zh, sin, zero], axis=1)
    return [jnp.tile(tb_, (1, 128 // HEAD_DIM)) for tb_ in (tc, ta, tb)]


def _wide(tab_ref, w):
    return jnp.tile(tab_ref[...], (1, w // 128))


def _hnr_fwd_math(xv, gain, tc_ref, ta_ref, tb_ref):
    width = xv.shape[1]
    rs = lax.rsqrt(_seg_allsum64(xv * xv) * (1.0 / HEAD_DIM) + EPS)
    y = xv * rs * gain
    return (y * _wide(tc_ref, width) + pltpu.roll(y, width - ROPE_DIM // 2, 1) * _wide(ta_ref, width)
            + pltpu.roll(y, ROPE_DIM // 2, 1) * _wide(tb_ref, width))


def _hnr_bwd_math(xv, gain, dout, tc, ta, tb, width):
    dy = dout * tc + pltpu.roll(dout * ta, ROPE_DIM // 2, 1) + pltpu.roll(dout * tb, width - ROPE_DIM // 2, 1)
    rs = lax.rsqrt(_seg_allsum64(xv * xv) * (1.0 / HEAD_DIM) + EPS)
    xn = xv * rs
    dyh = dy * gain
    dx = rs * (dyh - xn * (_seg_allsum64(dyh * xn, terms=2) * (1.0 / HEAD_DIM)))
    return dx, _colsum(dy * xn)


def _q_bwd(name, p1, gain, tables, dqs, dgt, seq):
    t, w4 = p1.shape
    da = w4 // 4
    width = 3 * da
    tr = _tile(seq, 128, 8)
    per_seq = seq // tr

    def body(x_ref, g_ref, tc_ref, ta_ref, tb_ref, d0_ref, d1_ref, d2_ref, dgt_ref, o_ref, dg_ref):
        i = pl.program_id(0)
        dout = jnp.concatenate([d0_ref[...], d1_ref[...], d2_ref[...]], axis=1)
        dx, dg = _hnr_bwd_math(x_ref[...].astype(F32), g_ref[...], dout, _wide(tc_ref, width), _wide(ta_ref, width),
                               _wide(tb_ref, width), width)

        @pl.when(i == 0)
        def _():
            dg_ref[...] = jnp.zeros_like(dg_ref)

        dg_ref[...] += dg
        o_ref[:, pl.ds(0, width)] = dx.astype(BF16)
        o_ref[:, pl.ds(width, da)] = dgt_ref[...]

    tab = pl.BlockSpec((tr, 128), lambda i: (i % per_seq, 0))
    return pl.pallas_call(
        body, name=name, grid=(t // tr,),
        in_specs=[_row_spec(tr, width), _full_spec((1, width)), tab, tab, tab] + [_row_spec(tr, da)] * 4,
        out_specs=[_row_spec(tr, w4), _full_spec((1, width))],
        out_shape=[jax.ShapeDtypeStruct((t, w4), BF16), jax.ShapeDtypeStruct((1, width), F32)],
        compiler_params=_params("arbitrary"))(p1, gain, *tables, *dqs, dgt)


def _k_bwd(name, kv, gain, tables, dk, dv, seq):
    t, w2 = kv.shape
    da = w2 // 2
    tr = _tile(seq, 256, 8)
    per_seq = seq // tr

    def body(x_ref, g_ref, tc_ref, ta_ref, tb_ref, dk_ref, dv_ref, o_ref, dg_ref):
        i = pl.program_id(0)
        dx, dg = _hnr_bwd_math(x_ref[...], g_ref[...], dk_ref[...], _wide(tc_ref, da), _wide(ta_ref, da),
                               _wide(tb_ref, da), da)

        @pl.when(i == 0)
        def _():
            dg_ref[...] = jnp.zeros_like(dg_ref)

        dg_ref[...] += dg
        o_ref[:, pl.ds(0, da)] = dx.astype(BF16)
        o_ref[:, pl.ds(da, da)] = dv_ref[...].astype(BF16)

    tab = pl.BlockSpec((tr, 128), lambda i: (i % per_seq, 0))
    return pl.pallas_call(
        body, name=name, grid=(t // tr,),
        in_specs=[_row_spec(tr, da), _full_spec((1, da)), tab, tab, tab] + [_row_spec(tr, da)] * 2,
        out_specs=[_row_spec(tr, w2), _full_spec((1, da))],
        out_shape=[jax.ShapeDtypeStruct((t, w2), BF16), jax.ShapeDtypeStruct((1, da), F32)],
        compiler_params=_params("arbitrary"))(kv, gain, *tables, dk, dv)


def _unit_index(dil, r, blk):
    if dil == 1:
        start = blk * SPAN
        return pl.ds(start if isinstance(start, int) else pl.multiple_of(start, SPAN), SPAN)
    return pl.ds(r + dil * SPAN * blk, SPAN, stride=dil)


def _unit_rows(ref, dil, r, blk):
    return ref[_unit_index(dil, r, blk), :]


def _store_rows(ref, dil, r, blk, val):
    ref[_unit_index(dil, r, blk), :] = val


def _over_units(dil, nblk, unit, carry0, after=None):
    for r in range(dil):
        carry = carry0
        for blk in range(nblk):
            carry = unit(r, blk, blk > 0, carry)
        if after is not None:
            after(r, carry)


def _band_mask(with_prev):
    nk = 2 * SPAN if with_prev else SPAN
    qi = lax.broadcasted_iota(jnp.int32, (SPAN, nk), 0)
    kj = lax.broadcasted_iota(jnp.int32, (SPAN, nk), 1)
    if with_prev:
        return (kj >= qi) & (kj <= qi + SPAN)
    return kj <= qi


_NT = (((1,), (1,)), ((), ()))
_TN = (((0,), (0,)), ((), ()))
N_PAIR = 128 // HEAD_DIM


def _per_head(x):
    head = lax.shift_right_logical(lax.broadcasted_iota(jnp.int32, x.shape, 1), HEAD_DIM.bit_length() - 1)
    return [jnp.where(head == hh, x, 0.0).astype(BF16) for hh in range(N_PAIR)]


def _head_columns(stat):
    return [stat[:, hh * HEAD_DIM:hh * HEAD_DIM + 1] for hh in range(N_PAIR)]


def _by_head(cols):
    head = lax.shift_right_logical(lax.broadcasted_iota(jnp.int32, (cols[0].shape[0], 128), 1), HEAD_DIM.bit_length() - 1)
    out = cols[-1]
    for hh in range(N_PAIR - 2, -1, -1):
        out = jnp.where(head == hh, cols[hh], out)
    return jnp.broadcast_to(out, (cols[0].shape[0], 128))


def _group_fwd(q_ref, k_ref, v_ref, o_ref, l_ref, dil, seq):
    nblk = seq // (dil * SPAN)
    scale = HEAD_DIM ** -0.5

    def unit(r, blk, with_prev, carry):
        q = (_unit_rows(q_ref, dil, r, blk) * scale).astype(BF16)
        kc = _per_head(_unit_rows(k_ref, dil, r, blk))
        vc = _per_head(_unit_rows(v_ref, dil, r, blk))
        mask = _band_mask(with_prev)
        nk = 2 * SPAN if with_prev else SPAN
        khs = [jnp.concatenate([carry[0][hh], kc[hh]], axis=0) if with_prev else kc[hh] for hh in range(N_PAIR)]
        vhs = [jnp.concatenate([carry[1][hh], vc[hh]], axis=0) if with_prev else vc[hh] for hh in range(N_PAIR)]
        s_all = lax.dot_general(q, jnp.concatenate(khs, axis=0), _NT, preferred_element_type=F32)
        ps, lses, rdens = [], [], []
        for hh in range(N_PAIR):
            s = jnp.where(mask, s_all[:, hh * nk:(hh + 1) * nk], NEG_INF)
            mx = jnp.max(s, axis=-1, keepdims=True)
            p = jnp.exp(s - mx)
            den = jnp.sum(p, axis=-1, keepdims=True)
            ps.append(p.astype(BF16))
            lses.append(mx + jnp.log(den))
            rdens.append(1.0 / den)
        out = jnp.dot(jnp.concatenate(ps, axis=1), jnp.concatenate(vhs, axis=0), preferred_element_type=F32)
        _store_rows(o_ref, dil, r, blk, out * _by_head(rdens))
        _store_rows(l_ref, dil, r, blk, _by_head(lses))
        return kc, vc

    _over_units(dil, nblk, unit, None)


def _attn_fwd(name, qn, kn, kv, p1, seq):
    t, da = kn.shape
    hp = da // 128
    ng = len(DILATIONS)

    def body(q0_ref, q1_ref, q2_ref, k_ref, v_ref, g_ref, o_ref, l_ref, m_ref, og, lg):
        for g, q_ref in enumerate((q0_ref, q1_ref, q2_ref)):
            _group_fwd(q_ref, k_ref, v_ref, og.at[g], lg.at[g], DILATIONS[g], seq)
        a0, a1, a2 = lg[0], lg[1], lg[2]
        mx = jnp.maximum(jnp.maximum(a0, a1), a2)
        e0, e1, e2 = jnp.exp(a0 - mx), jnp.exp(a1 - mx), jnp.exp(a2 - mx)
        den = e0 + e1 + e2
        o = (e0 * og[0] + e1 * og[1] + e2 * og[2]) / den
        o_ref[...] = o
        l_ref[...] = mx + jnp.log(den)
        gt = g_ref[...].astype(F32)
        m_ref[...] = (o * gt * _sigmoid(gt)).astype(BF16)

    blk_spec = lambda off: pl.BlockSpec((seq, 128), lambda b, h: (b, off + h))
    return pl.pallas_call(
        body, name=name, grid=(t // seq, hp),
        in_specs=[blk_spec(0), blk_spec(hp), blk_spec(2 * hp), blk_spec(0), blk_spec(hp), blk_spec(3 * hp)],
        out_specs=[blk_spec(0)] * 3,
        out_shape=[jax.ShapeDtypeStruct((t, da), F32)] * 2 + [jax.ShapeDtypeStruct((t, da), BF16)],
        scratch_shapes=[pltpu.VMEM((ng, seq, 128), F32), pltpu.VMEM((ng, seq, 128), F32)],
        compiler_params=_params("parallel", "parallel"))(qn, qn, qn, kn, kv, p1)


def _group_bwd(q_ref, k_ref, v_ref, do_ref, l_ref, d_ref, dq_ref, dk_ref, dv_ref, dil, seq, first):
    nblk = seq // (dil * SPAN)
    scale = HEAD_DIM ** -0.5

    def put(ref, r, blk, val):
        if not first:
            val = val + _unit_rows(ref, dil, r, blk)
        _store_rows(ref, dil, r, blk, val)

    def unit(r, blk, with_prev, carry):
        kp, vp, pend_k, pend_v = carry if with_prev else (None,) * 4
        q = _unit_rows(q_ref, dil, r, blk) * scale
        kc = _per_head(_unit_rows(k_ref, dil, r, blk))
        vc = _per_head(_unit_rows(v_ref, dil, r, blk))
        dov = _unit_rows(do_ref, dil, r, blk)
        lcols = _head_columns(_unit_rows(l_ref, dil, r, blk))
        dcols = _head_columns(_unit_rows(d_ref, dil, r, blk))
        mask = _band_mask(with_prev)
        qb, dob = q.astype(BF16), dov.astype(BF16)
        nk = 2 * SPAN if with_prev else SPAN
        khs = [jnp.concatenate([kp[hh], kc[hh]], axis=0) if with_prev else kc[hh] for hh in range(N_PAIR)]
        vhs = [jnp.concatenate([vp[hh], vc[hh]], axis=0) if with_prev else vc[hh] for hh in range(N_PAIR)]
        kall, vall = jnp.concatenate(khs, axis=0), jnp.concatenate(vhs, axis=0)
        s_all = lax.dot_general(qb, kall, _NT, preferred_element_type=F32)
        dp_all = lax.dot_general(dob, vall, _NT, preferred_element_type=F32)
        ps, dss = [], []
        for hh in range(N_PAIR):
            cols = slice(hh * nk, (hh + 1) * nk)
            p = jnp.where(mask, jnp.exp(s_all[:, cols] - lcols[hh]), 0.0)
            ps.append(p.astype(BF16))
            dss.append((p * (dp_all[:, cols] - dcols[hh])).astype(BF16))
        dq = jnp.dot(jnp.concatenate(dss, axis=1), kall, preferred_element_type=F32)
        qall = jnp.concatenate(_per_head(q), axis=0)
        doall = jnp.concatenate(_per_head(dov), axis=0)
        dkcat = lax.dot_general(jnp.concatenate(dss, axis=0), qall, _TN, preferred_element_type=F32)
        dvcat = lax.dot_general(jnp.concatenate(ps, axis=0), doall, _TN, preferred_element_type=F32)
        _store_rows(dq_ref, dil, r, blk, dq * scale)
        if with_prev:
            put(dk_ref, r, blk - 1, pend_k + dkcat[:SPAN])
            put(dv_ref, r, blk - 1, pend_v + dvcat[:SPAN])
            return kc, vc, dkcat[SPAN:], dvcat[SPAN:]
        return kc, vc, dkcat, dvcat

    def after(r, carry):
        put(dk_ref, r, nblk - 1, carry[2])
        put(dv_ref, r, nblk - 1, carry[3])

    _over_units(dil, nblk, unit, None, after)


def _attn_bwd(name, qn, kn, kv, do, lse, dsum, seq):
    t, da = kn.shape
    hp = da // 128

    def body(q0_ref, q1_ref, q2_ref, k_ref, v_ref, do_ref, l_ref, d_ref, dq0_ref, dq1_ref, dq2_ref, dk_ref, dv_ref):
        groups = ((q0_ref, dq0_ref), (q1_ref, dq1_ref), (q2_ref, dq2_ref))
        for g, (q_ref, dq_ref) in enumerate(groups):
            _group_bwd(q_ref, k_ref, v_ref, do_ref, l_ref, d_ref, dq_ref, dk_ref, dv_ref, DILATIONS[g], seq, g == 0)

    blk_spec = lambda off: pl.BlockSpec((seq, 128), lambda b, h: (b, off + h))
    return pl.pallas_call(
        body, name=name, grid=(t // seq, hp),
        in_specs=[blk_spec(0), blk_spec(hp), blk_spec(2 * hp), blk_spec(0), blk_spec(hp), blk_spec(0), blk_spec(0), blk_spec(0)],
        out_specs=[blk_spec(0)] * 5,
        out_shape=[jax.ShapeDtypeStruct((t, da), F32)] * 5,
        compiler_params=_params("parallel", "parallel"))(qn, qn, qn, kn, kv, do, lse, dsum)


def _gate_bwd(name, dm, o, p1):
    t, da = o.shape
    tr = _tile(t, 256, 8)

    def body(dm_ref, o_ref, g_ref, do_ref, dg_ref, ds_ref):
        g = g_ref[...].astype(F32)
        sg = _sigmoid(g)
        dmv, ov = dm_ref[...].astype(F32), o_ref[...]
        do = dmv * (g * sg)
        do_ref[...] = do
        dg_ref[...] = (dmv * ov * (sg * (1.0 + g * (1.0 - sg)))).astype(BF16)
        ds_ref[...] = _seg_allsum64(do * ov, terms=2)

    return pl.pallas_call(
        body, name=name, grid=(t // tr,),
        in_specs=[_row_spec(tr, da), _row_spec(tr, da), _row_spec(tr, da, 3)],
        out_specs=[_row_spec(tr, da)] * 3,
        out_shape=[jax.ShapeDtypeStruct((t, da), F32), jax.ShapeDtypeStruct((t, da), BF16), jax.ShapeDtypeStruct((t, da), F32)],
        compiler_params=_params("parallel"))(dm, o, p1)


def _cast_bf16(name, w2d, chip):
    r, c = w2d.shape
    tr = _tile(r, 256, 16)

    def body(chip_ref, x_ref, o_ref):
        o_ref[...] = x_ref[...].astype(BF16)

    grid_spec = pltpu.PrefetchScalarGridSpec(
        num_scalar_prefetch=1, grid=(r // tr,),
        in_specs=[pl.BlockSpec((tr, c), lambda i, m: (i, 0))],
        out_specs=pl.BlockSpec((None, tr, c), lambda i, m: (m[0], i, 0)))
    return pl.pallas_call(
        body, name=name, grid_spec=grid_spec, out_shape=jax.ShapeDtypeStruct((N_CHIPS, r, c), BF16),
        compiler_params=_params("parallel"))(chip, w2d)


def _adamw(name, w, g, m, v):
    r, c = w.shape
    tr = _tile(r, 256, 8)
    c1 = 1.0 - ADAM_B1 ** ADAM_STEP
    c2 = 1.0 - ADAM_B2 ** ADAM_STEP

    def body(w_ref, g_ref, m_ref, v_ref, d_ref, nm_ref, nv_ref):
        gv = g_ref[...]
        nm = ADAM_B1 * m_ref[...] + (1.0 - ADAM_B1) * gv
        nv = ADAM_B2 * v_ref[...] + (1.0 - ADAM_B2) * (gv * gv)
        nm_ref[...] = nm
        nv_ref[...] = nv
        d_ref[...] = -ADAM_LR * ((nm / c1) / (jnp.sqrt(nv / c2) + ADAM_EPS) + ADAM_WD * w_ref[...])

    sds = jax.ShapeDtypeStruct((r, c), F32)
    return pl.pallas_call(
        body, name=name, grid=(r // tr,), in_specs=[_row_spec(tr, c)] * 4, out_specs=[_row_spec(tr, c)] * 3,
        out_shape=[sds] * 3, compiler_params=_params("parallel"))(w, g, m, v)


def _pair_sum(name, gd, recv, core):
    _, r, c = gd.shape
    rh = r // 2
    tr = _tile(rh, 256, 8)
    nrt = rh // tr

    def body(c_ref, a_ref, b_ref, o_ref):
        o_ref[...] = (a_ref[...].astype(F32) + b_ref[...].astype(F32)).astype(BF16)

    grid_spec = pltpu.PrefetchScalarGridSpec(
        num_scalar_prefetch=1, grid=(N_CHIPS, nrt),
        in_specs=[pl.BlockSpec((None, tr, c), lambda j, i, cr: (j, cr[0] * nrt + i, 0)),
                  pl.BlockSpec((None, tr, c), lambda j, i, cr: (j, i, 0))],
        out_specs=pl.BlockSpec((None, tr, c), lambda j, i, cr: (j, i, 0)))
    return pl.pallas_call(
        body, name=name, grid_spec=grid_spec, out_shape=jax.ShapeDtypeStruct((N_CHIPS, rh, c), BF16),
        compiler_params=_params("parallel", "parallel"))(core, gd, recv)


def _chip_sum(name, sums, parts, order):
    _, rh, c = parts.shape
    tr = _tile(rh, 256, 16)
    nrt = rh // tr

    def body(o_ref_, s_ref, p1_ref, p2_ref, p3_ref, o_ref):
        acc = s_ref[...].astype(F32)
        for p_ref in (p1_ref, p2_ref, p3_ref):
            acc = acc + p_ref[...].astype(F32)
        o_ref[...] = acc

    slot = lambda k: pl.BlockSpec((None, tr, c), lambda i, o: (o[k], i, 0))
    grid_spec = pltpu.PrefetchScalarGridSpec(
        num_scalar_prefetch=1, grid=(nrt,),
        in_specs=[slot(0), slot(1), slot(2), slot(3)],
        out_specs=pl.BlockSpec((tr, c), lambda i, o: (o[N_CHIPS] * nrt + i, 0)))
    return pl.pallas_call(
        body, name=name, grid_spec=grid_spec, out_shape=jax.ShapeDtypeStruct((2 * rh, c), F32),
        compiler_params=_params("parallel"))(order, sums, parts, parts, parts)


HBM = pl.BlockSpec(memory_space=pl.ANY)


def _place():
    x, y, c = lax.axis_index("x"), lax.axis_index("y"), lax.axis_index("c")
    chips = [(1 - x, y), (x, 1 - y), (1 - x, 1 - y)]
    return x, y, c, chips


def _half(ref, hc):
    rows = ref.shape[0] // 2
    return ref.at[pl.ds(hc * rows, rows)]


class _Plan:
    def __init__(self, ins, out_shapes, aliases, n_sems, start, finish):
        self.ins, self.out_shapes, self.aliases, self.n_sems = list(ins), list(out_shapes), dict(aliases), n_sems
        self.start, self.finish = start, finish


def _gather_plan(shards):
    n = len(shards)

    def ici(place, outs, send_sems, recv_sems, i, k, slot):
        x, y, c, chips = place
        half = _half(outs[i].at[slot], c)
        return pltpu.make_async_remote_copy(
            src_ref=half, dst_ref=half, send_sem=send_sems.at[6 * i + k], recv_sem=recv_sems.at[6 * i + k],
            device_id=(chips[k][0], chips[k][1], c), device_id_type=MESH)

    def d2d(place, outs, send_sems, recv_sems, i, k, slot, hc):
        x, y, c, chips = place
        half = _half(outs[i].at[slot], hc)
        return pltpu.make_async_remote_copy(
            src_ref=half, dst_ref=half, send_sem=send_sems.at[6 * i + 3 + k], recv_sem=recv_sems.at[6 * i + 3 + k],
            device_id=(x, y, 1 - c), device_id_type=MESH)

    def start(place, ins, outs, send_sems, recv_sems):
        x, y, c, chips = place
        for i in range(n):
            for k in range(3):
                ici(place, outs, send_sems, recv_sems, i, k, 2 * x + y).start()

    def finish(place, ins, outs, send_sems, recv_sems):
        x, y, c, chips = place
        theirs = [2 * chip[0] + chip[1] for chip in chips]
        for i in range(n):
            for k in range(3):
                ici(place, outs, send_sems, recv_sems, i, k, theirs[k]).wait_recv()
                d2d(place, outs, send_sems, recv_sems, i, k, theirs[k], c).start()
        for i in range(n):
            for k in range(3):
                d2d(place, outs, send_sems, recv_sems, i, k, theirs[k], 1 - c).wait_recv()
        for i in range(n):
            for k in range(3):
                ici(place, outs, send_sems, recv_sems, i, k, 2 * x + y).wait_send()
                d2d(place, outs, send_sems, recv_sems, i, k, theirs[k], c).wait_send()

    return _Plan(shards, [jax.ShapeDtypeStruct(s.shape, s.dtype) for s in shards], {i: i for i in range(n)}, 6 * n,
                 start, finish)


def _scatter_plan(sums):
    n = len(sums)

    def copy(place, ins, outs, send_sems, recv_sems, i, k, src_slot, dst_slot):
        x, y, c, chips = place
        return pltpu.make_async_remote_copy(
            src_ref=ins[i].at[src_slot], dst_ref=outs[i].at[dst_slot],
            send_sem=send_sems.at[3 * i + k], recv_sem=recv_sems.at[3 * i + k],
            device_id=(chips[k][0], chips[k][1], c), device_id_type=MESH)

    def start(place, ins, outs, send_sems, recv_sems):
        x, y, c, chips = place
        for i in range(n):
            for k, chip in enumerate(chips):
                copy(place, ins, outs, send_sems, recv_sems, i, k, 2 * chip[0] + chip[1], 2 * x + y).start()

    def finish(place, ins, outs, send_sems, recv_sems):
        x, y, c, chips = place
        for i in range(n):
            for k, chip in enumerate(chips):
                theirs = 2 * chip[0] + chip[1]
                copy(place, ins, outs, send_sems, recv_sems, i, k, theirs, 2 * x + y).wait_send()
                copy(place, ins, outs, send_sems, recv_sems, i, k, theirs, theirs).wait_recv()

    return _Plan(sums, [jax.ShapeDtypeStruct(s.shape, s.dtype) for s in sums], {}, 3 * n, start, finish)


def _hosted_call(body, plan, *, name, grid, in_specs, out_specs, out_shape, scratch_shapes, args, sem):
    in_specs, out_specs, out_shape, scratch_shapes = list(in_specs), list(out_specs), list(out_shape), list(scratch_shapes)
    if plan is None:
        res = pl.pallas_call(body, name=name, grid=grid, in_specs=in_specs, out_specs=out_specs, out_shape=out_shape,
                             scratch_shapes=scratch_shapes, compiler_params=_params(*sem))(*args)
        return list(res), []
    n_in, n_out, n_scr = len(in_specs), len(out_specs), len(scratch_shapes)
    p_in, p_out = len(plan.ins), len(plan.out_shapes)

    def hosted(*refs):
        refs = list(refs)
        ins, pins = refs[:n_in], refs[n_in:n_in + p_in]
        outs = refs[n_in + p_in:n_in + p_in + n_out]
        pouts = refs[n_in + p_in + n_out:n_in + p_in + n_out + p_out]
        scr = refs[n_in + p_in + n_out + p_out:n_in + p_in + n_out + p_out + n_scr]
        send_sems, recv_sems = refs[-2:]
        place = _place()
        ids = [pl.program_id(d) for d in range(len(grid))]
        first = functools.reduce(jnp.logical_and, [i == 0 for i in ids])
        last = functools.reduce(jnp.logical_and, [i == g - 1 for i, g in zip(ids, grid)])

        @pl.when(first)
        def _():
            plan.start(place, pins, pouts, send_sems, recv_sems)

        body(*ins, *outs, *scr)

        @pl.when(last)
        def _():
            plan.finish(place, pins, pouts, send_sems, recv_sems)

    res = pl.pallas_call(
        hosted, name=name, grid=grid, in_specs=in_specs + [HBM] * p_in, out_specs=out_specs + [HBM] * p_out,
        out_shape=out_shape + plan.out_shapes,
        input_output_aliases={n_in + i: n_out + o for i, o in plan.aliases.items()},
        scratch_shapes=scratch_shapes + [pltpu.SemaphoreType.DMA((plan.n_sems,)), pltpu.SemaphoreType.DMA((plan.n_sems,))],
        compiler_params=_params(*(("arbitrary",) * len(grid))))(*args, *plan.ins)
    return list(res[:n_out]), list(res[n_out:])


def _pair_exchange(name, grads):
    n = len(grads)

    def body(*refs):
        ins, outs = refs[:n], refs[n:2 * n]
        send_sems, recv_sems = refs[2 * n:]
        x, y, c, _ = _place()
        cps = []
        for i in range(n):
            rows = ins[i].shape[1] // 2
            cp = pltpu.make_async_remote_copy(
                src_ref=ins[i].at[:, pl.ds((1 - c) * rows, rows), :], dst_ref=outs[i],
                send_sem=send_sems.at[i], recv_sem=recv_sems.at[i], device_id=(x, y, 1 - c), device_id_type=MESH)
            cp.start()
            cps.append(cp)
        for cp in cps:
            cp.wait()

    return pl.pallas_call(
        body, name=name, in_specs=[HBM] * n, out_specs=[HBM] * n,
        out_shape=[jax.ShapeDtypeStruct((N_CHIPS, g.shape[1] // 2, g.shape[2]), g.dtype) for g in grads],
        scratch_shapes=[pltpu.SemaphoreType.DMA((n,)), pltpu.SemaphoreType.DMA((n,))],
        )(*grads)


def _sibling_join(grads):
    n = len(grads)

    def body(*refs):
        outs = refs[n:2 * n]
        send_sems, recv_sems = refs[2 * n:]
        x, y, c, _ = _place()
        cps = []
        for i in range(n):
            cp = pltpu.make_async_remote_copy(
                src_ref=_half(outs[i], c), dst_ref=_half(outs[i], c), send_sem=send_sems.at[i], recv_sem=recv_sems.at[i],
                device_id=(x, y, 1 - c), device_id_type=MESH)
            cp.start()
            cps.append(cp)
        for i, cp in enumerate(cps):
            cp.wait_send()
            pltpu.make_async_remote_copy(
                src_ref=_half(outs[i], 1 - c), dst_ref=_half(outs[i], 1 - c), send_sem=send_sems.at[i],
                recv_sem=recv_sems.at[i], device_id=(x, y, 1 - c), device_id_type=MESH).wait_recv()

    return pl.pallas_call(
        body, name="sibling_join", in_specs=[HBM] * n, out_specs=[HBM] * n,
        out_shape=[jax.ShapeDtypeStruct(g.shape, g.dtype) for g in grads],
        input_output_aliases={i: i for i in range(n)},
        scratch_shapes=[pltpu.SemaphoreType.DMA((n,)), pltpu.SemaphoreType.DMA((n,))],
        )(*grads)


def _gather8(name, block, reduce):
    m, n = block.shape

    def body(x_ref, out_ref, *scratch):
        if reduce:
            all_ref, send_sems, recv_sems, local_sem = scratch
        else:
            all_ref = out_ref
            send_sems, recv_sems, local_sem = scratch
        x, y, c, chips = _place()
        me, sibling = (x, y, c), (x, y, 1 - c)

        def rows(px, py, pc):
            return all_ref.at[pl.ds((4 * px + 2 * py + pc) * m, m), :]

        def copy(k, blk, to, src=None):
            return pltpu.make_async_remote_copy(
                src_ref=rows(*blk) if src is None else src, dst_ref=rows(*blk),
                send_sem=send_sems.at[k], recv_sem=recv_sems.at[k], device_id=to, device_id_type=MESH)

        mine = pltpu.make_async_copy(x_ref, rows(*me), local_sem)
        mine.start()
        first = [copy(0, me, sibling, src=x_ref)]
        first += [copy(1 + j, me, (chip[0], chip[1], c), src=x_ref) for j, chip in enumerate(chips)]
        for cp in first:
            cp.start()
        passed = [copy(4 + j, (chip[0], chip[1], c), sibling) for j, chip in enumerate(chips)]
        for j, chip in enumerate(chips):
            copy(1 + j, (chip[0], chip[1], c), me).wait_recv()
            passed[j].start()
        copy(0, sibling, me).wait_recv()
        for j, chip in enumerate(chips):
            copy(4 + j, (chip[0], chip[1], 1 - c), me).wait_recv()
        for cp in first + passed:
            cp.wait_send()
        mine.wait()
        if reduce:
            acc = all_ref[pl.ds(0, m), :]
            for d in range(1, 8):
                acc = acc + all_ref[pl.ds(d * m, m), :]
            out_ref[...] = acc

    sems = [pltpu.SemaphoreType.DMA((7,)), pltpu.SemaphoreType.DMA((7,)), pltpu.SemaphoreType.DMA]
    scratch = ([pltpu.VMEM((8 * m, n), F32)] if reduce else []) + sems
    return pl.pallas_call(
        body, name=name,
        out_shape=jax.ShapeDtypeStruct((m, n) if reduce else (8 * m, n), F32),
        in_specs=[pl.BlockSpec(memory_space=pltpu.VMEM)], out_specs=pl.BlockSpec(memory_space=pltpu.VMEM),
        scratch_shapes=scratch)(block)


def _pad_rows(a, rows):
    return jnp.concatenate([a, jnp.zeros((rows - a.shape[0], a.shape[1]), a.dtype)], axis=0)


def kernel(x, p, norm_g, w_in_a, conv_w, conv_b, ln_g, ln_b, w_out_a, kv_norm_g, w_kv, k_norm_g, w_in_b, q_norm_g, w_out_b, ple_norm_g, w_ple_gate, w_ple_proj, loss_target, m_norm_g, m_w_in_a, m_conv_w, m_conv_b, m_ln_g, m_ln_b, m_w_out_a, m_kv_norm_g, m_w_kv, m_k_norm_g, m_w_in_b, m_q_norm_g, m_w_out_b, m_ple_norm_g, m_w_ple_gate, m_w_ple_proj, v_norm_g, v_w_in_a, v_conv_w, v_conv_b, v_ln_g, v_ln_b, v_w_out_a, v_kv_norm_g, v_w_kv, v_k_norm_g, v_w_in_b, v_q_norm_g, v_w_out_b, v_ple_norm_g, v_w_ple_gate, v_w_ple_proj):
    nb, seq, dm = x.shape
    t = nb * seq
    ple = p.shape[-1]
    ccs = conv_w.shape[-1]
    cc = N_CHIPS * ccs
    da = dm
    nheads = da // HEAD_DIM
    assert seq == DILATIONS[-1] * SPAN and da % 128 == 0 and ccs % 128 == 0

    core = lax.axis_index("c").astype(jnp.int32).reshape(1)
    chip = (2 * lax.axis_index("x") + lax.axis_index("y")).astype(jnp.int32)
    chip1 = chip.reshape(1)
    sum_order = jnp.concatenate([(chip1 + k) % N_CHIPS for k in range(N_CHIPS)] + [core])

    x2 = x.reshape(t, dm)
    tgt2 = loss_target.reshape(t, dm)
    p0 = p[0].reshape(t, ple)
    p1 = p[1].reshape(t, ple)

    big = [
        ("w_in_a", w_in_a[0], "col"), ("w_out_a", w_out_a[0], "row"), ("w_kv", w_kv, "col"),
        ("w_in_b", w_in_b[0], "col"), ("w_out_b", w_out_b[0], "row"),
        ("w_ple_gate0", w_ple_gate[0], "row"), ("w_ple_gate1", w_ple_gate[1], "row"),
        ("w_ple_proj0", w_ple_proj[0], "col"), ("w_ple_proj1", w_ple_proj[1], "col"),
    ]
    shard_shape = {nm: w.shape for nm, w, _ in big}
    names = [nm for nm, _, _ in big]
    own = [_cast_bf16("cast_" + nm, w, chip1) for nm, w, _ in big]
    W = {}

    vec_rows = 40
    small = _pad_rows(jnp.concatenate([conv_w[0], conv_b, ln_g, ln_b], axis=0), vec_rows)
    allv = _gather8("gather_conv_vectors", small, reduce=False).reshape(N_CHIPS, 2, vec_rows, ccs)[:, 0]
    allv = allv.transpose(1, 0, 2).reshape(vec_rows, cc)
    cw_full, cb_full, lg_full, lb_full = allv[:HALO], allv[31:32], allv[32:33], allv[33:34]
    cw_full = cw_full * (lax.broadcasted_iota(jnp.int32, (HALO, 1), 0) < CONV_WIDTH).astype(F32)
    tables = _rope_tables(seq)
    gain_q = jnp.tile(q_norm_g[0][:, None, :], (1, nheads, 1)).reshape(1, 3 * da)
    gain_k = jnp.tile(k_norm_g[None, :], (1, nheads))
    g0, g1 = norm_g[0:1], norm_g[1:2]
    pg0, pg1 = ple_norm_g[0:1], ple_norm_g[1:2]
    kvg = kv_norm_g[None, :]

    (u0,), (W[names[0]],) = _rms_fwd("rms_u0", x2, [g0], plan=_gather_plan(own[:1]))
    pa, gathered = _mm_nn("mm_in_a", u0, W["w_in_a"], "col", out_dtype=BF16, plan=_gather_plan(own[1:]))
    W.update(zip(names[1:], gathered))
    for nm in ("w_ple_proj0", "w_ple_proj1"):
        W[nm] = W[nm].transpose(1, 0, 2).reshape(1, W[nm].shape[1], -1)
    conv_out, m_a = _mixa_fwd("mixa_fwd", pa, cw_full, cb_full, lg_full, lb_full, seq)
    h0, r0 = _mm_nn("mm_out_a", m_a, W["w_out_a"], "row", resid=x2, norm_gain=pg0)
    gpre0 = _mm_nn("mm_gate0", r0, W["w_ple_gate0"], "row", out_dtype=BF16)
    pp0 = _mm_nn("mm_proj0", p0, W["w_ple_proj0"], "col", out_dtype=BF16)
    x1, kvn, u1 = _ple_fwd("ple_fwd0", h0, gpre0, pp0, [kvg, g1])
    kv, kn = _mm_nn("mm_kv", kvn, W["w_kv"], "col", heads=(gain_k, tables, da, seq))
    pb, qn = _mm_nn("mm_in_b", u1, W["w_in_b"], "col", out_dtype=BF16, heads=(gain_q, tables, 3 * da, seq))
    o, lse, m_b = _attn_fwd("attn_fwd", qn, kn, kv, pb, seq)
    h1, r1 = _mm_nn("mm_out_b", m_b, W["w_out_b"], "row", resid=x1, norm_gain=pg1)
    gpre1 = _mm_nn("mm_gate1", r1, W["w_ple_gate1"], "row", out_dtype=BF16)
    pp1 = _mm_nn("mm_proj1", p1, W["w_ple_proj1"], "col", out_dtype=BF16)
    dy, dgp1, dpp1, sq = _ple_loss("ple_loss", h1, gpre1, pp1, tgt2)
    loss = lax.psum(0.5 * sq[0, 0] / dm, ("x", "y", "c"))

    G = {}
    G["w_ple_gate1"] = _mm_tn("tn_gate1", r1, dgp1, "row", shard_shape["w_ple_gate1"])
    G["w_ple_proj1"] = _mm_tn("tn_proj1", p1, dpp1, "col", shard_shape["w_ple_proj1"], whole=True)
    dr1 = _mm_nt("nt_gate1", dgp1, W["w_ple_gate1"], "row", out_dtype=BF16)
    dh1, (dpg1,) = _rms_bwd("rms_bwd_r1", h1, dy, [(pg1, dr1)])
    G["w_out_b"] = _mm_tn("tn_out_b", m_b, dh1, "row", shard_shape["w_out_b"])
    dm_b = _mm_nt("nt_out_b", dh1, W["w_out_b"], "row", out_dtype=BF16)
    d_o, dgt, dsum = _gate_bwd("gate_bwd", dm_b, o, pb)
    dq0, dq1, dq2, dk, dv = _attn_bwd("attn_bwd", qn, kn, kv, d_o, lse, dsum, seq)
    dpb, dgq = _q_bwd("q_bwd", pb, gain_q, tables, [dq0, dq1, dq2], dgt, seq)
    dkv, dgk = _k_bwd("k_bwd", kv, gain_k, tables, dk, dv, seq)
    G["w_in_b"] = _mm_tn("tn_in_b", u1, dpb, "col", shard_shape["w_in_b"])
    du1 = _mm_nt("nt_in_b", dpb, W["w_in_b"], "col", out_dtype=BF16)
    G["w_kv"] = _mm_tn("tn_kv", kvn, dkv, "col", shard_shape["w_kv"])
    dkvn = _mm_nt("nt_kv", dkv, W["w_kv"], "col", out_dtype=BF16)
    dx1, (dg1, dkvg), dgp0, dpp0 = _rms_bwd("rms_bwd_x1", x1, dh1, [(g1, du1), (kvg, dkvn)], ple=(gpre0, pp0))
    G["w_ple_gate0"] = _mm_tn("tn_gate0", r0, dgp0, "row", shard_shape["w_ple_gate0"])
    G["w_ple_proj0"] = _mm_tn("tn_proj0", p0, dpp0, "col", shard_shape["w_ple_proj0"], whole=True)
    dr0 = _mm_nt("nt_gate0", dgp0, W["w_ple_gate0"], "row", out_dtype=BF16)
    dh0, (dpg0,) = _rms_bwd("rms_bwd_r0", h0, dx1, [(pg0, dr0)])
    G["w_out_a"] = _mm_tn("tn_out_a", m_a, dh0, "row", shard_shape["w_out_a"])
    dm_a = _mm_nt("nt_out_a", dh0, W["w_out_a"], "row", out_dtype=BF16)

    def pair_sums(tag, batch):
        recv = _pair_exchange("pair_exchange_" + tag, [G[nm] for nm in batch])
        return [_pair_sum("pair_sum_" + nm, G[nm], rc, core) for nm, rc in zip(batch, recv)]

    layer1 = ["w_kv", "w_in_b", "w_out_b", "w_ple_gate1", "w_ple_proj1"]
    layer0 = ["w_in_a", "w_out_a", "w_ple_gate0", "w_ple_proj0"]
    sums1 = pair_sums("layer1", layer1)
    (dc, dz, dlg, dlb, dcb), parts1 = _mixa_bwd1("mixa_bwd1", conv_out, pa, dm_a, lg_full, lb_full,
                                                 plan=_scatter_plan(sums1))
    dpa, dcw = _mixa_bwd2("mixa_bwd2", dc, pa, dz, cw_full, seq)
    G["w_in_a"] = _mm_tn("tn_in_a", u0, dpa, "col", shard_shape["w_in_a"])
    sums0 = pair_sums("layer0", layer0)
    du0, parts0 = _mm_nt("nt_in_a", dpa, W["w_in_a"], "col", out_dtype=BF16, plan=_scatter_plan(sums0))
    dx, (dg0,) = _rms_bwd("rms_bwd_x", x2, dh0, [(g0, du0)])
    grad_x = dx.reshape(nb, seq, dm)

    sums = dict(zip(layer1 + layer0, sums1 + sums0))
    parts = dict(zip(layer1 + layer0, parts1 + parts0))
    halves = [_chip_sum("chip_sum_" + nm, sums[nm], parts[nm], sum_order) for nm in names]
    gfull = dict(zip(names, _sibling_join(halves)))

    def as_rows(a):
        return a.reshape(-1, dm)

    small_parts = [as_rows(dcw), as_rows(dcb), as_rows(dlg), as_rows(dlb), dg0, dg1, dkvg, dpg0, dpg1, as_rows(dgk), as_rows(dgq)]
    counts = [a.shape[0] for a in small_parts]
    total = sum(counts)
    packed = _pad_rows(jnp.concatenate(small_parts, axis=0), -(-total // 8) * 8)
    red = _gather8("reduce_small", packed, reduce=True)
    pieces, off = [], 0
    for n_ in counts:
        pieces.append(red[off:off + n_])
        off += n_
    r_dcw, r_dcb, r_dlg, r_dlb, r_g0, r_g1, r_kvg, r_pg0, r_pg1, r_gk, r_gq = pieces
    my_cols = lambda a: lax.dynamic_slice_in_dim(a.reshape(-1, cc), chip * ccs, ccs, axis=1)
    small_grads = {
        "norm_g": jnp.concatenate([r_g0, r_g1], axis=0),
        "conv_w": my_cols(r_dcw)[:CONV_WIDTH],
        "conv_b": my_cols(r_dcb), "ln_g": my_cols(r_dlg), "ln_b": my_cols(r_dlb),
        "kv_norm_g": r_kvg,
        "k_norm_g": r_gk.reshape(nheads, HEAD_DIM).sum(axis=0, keepdims=True),
        "q_norm_g": r_gq.reshape(3, nheads, HEAD_DIM).sum(axis=1),
        "ple_norm_g": jnp.concatenate([r_pg0, r_pg1], axis=0),
    }

    given = dict(norm_g=norm_g, w_in_a=w_in_a, conv_w=conv_w, conv_b=conv_b, ln_g=ln_g, ln_b=ln_b, w_out_a=w_out_a,
                 kv_norm_g=kv_norm_g, w_kv=w_kv, k_norm_g=k_norm_g, w_in_b=w_in_b, q_norm_g=q_norm_g, w_out_b=w_out_b,
                 ple_norm_g=ple_norm_g, w_ple_gate=w_ple_gate, w_ple_proj=w_ple_proj)
    mom1 = dict(norm_g=m_norm_g, w_in_a=m_w_in_a, conv_w=m_conv_w, conv_b=m_conv_b, ln_g=m_ln_g, ln_b=m_ln_b,
                w_out_a=m_w_out_a, kv_norm_g=m_kv_norm_g, w_kv=m_w_kv, k_norm_g=m_k_norm_g, w_in_b=m_w_in_b,
                q_norm_g=m_q_norm_g, w_out_b=m_w_out_b, ple_norm_g=m_ple_norm_g, w_ple_gate=m_w_ple_gate,
                w_ple_proj=m_w_ple_proj)
    mom2 = dict(norm_g=v_norm_g, w_in_a=v_w_in_a, conv_w=v_conv_w, conv_b=v_conv_b, ln_g=v_ln_g, ln_b=v_ln_b,
                w_out_a=v_w_out_a, kv_norm_g=v_kv_norm_g, w_kv=v_w_kv, k_norm_g=v_k_norm_g, w_in_b=v_w_in_b,
                q_norm_g=v_q_norm_g, w_out_b=v_w_out_b, ple_norm_g=v_ple_norm_g, w_ple_gate=v_w_ple_gate,
                w_ple_proj=v_w_ple_proj)
    order = ["norm_g", "w_in_a", "conv_w", "conv_b", "ln_g", "ln_b", "w_out_a", "kv_norm_g", "w_kv", "k_norm_g", "w_in_b",
             "q_norm_g", "w_out_b", "ple_norm_g", "w_ple_gate", "w_ple_proj"]
    grads, deltas, new_m, new_v = {}, {}, {}, {}
    for nm in order:
        shape = given[nm].shape
        if nm in ("w_ple_gate", "w_ple_proj"):
            g2 = jnp.concatenate([gfull[nm + "0"], gfull[nm + "1"]], axis=0)
        elif nm in gfull:
            g2 = gfull[nm]
        else:
            g2 = small_grads[nm]
        two_d = g2.shape
        d2, m2, v2 = _adamw("adamw_" + nm, given[nm].reshape(two_d), g2, mom1[nm].reshape(two_d), mom2[nm].reshape(two_d))
        grads[nm], deltas[nm], new_m[nm], new_v[nm] = (a.reshape(shape) for a in (g2, d2, m2, v2))

    return (loss, grad_x, *[grads[n_] for n_ in order], *[deltas[n_] for n_ in order],
            *[new_m[n_] for n_ in order], *[new_v[n_] for n_ in order])
```

```python
import functools

import jax
import jax.numpy as jnp
from jax import lax
from jax.experimental import pallas as pl
from jax.experimental.pallas import tpu as pltpu

F32 = jnp.float32
BF16 = jnp.bfloat16
MESH = pl.DeviceIdType.MESH

EPS = 1e-6
NEG_INF = -1e30
HEAD_DIM = 64
ROPE_DIM = 16
ROPE_THETA = 500000.0
CONV_WIDTH = 31
SUBLANES = 8
CONV_ROWS = 64
HALO = 32
SPAN = 128
DILATIONS = (1, 4, 16)
ADAM_LR, ADAM_B1, ADAM_B2, ADAM_EPS, ADAM_WD, ADAM_STEP = 0.001, 0.9, 0.999, 1e-08, 0.01, 10
N_CHIPS = 4
VMEM_LIMIT = 56 * 1024 * 1024


def _tile(n, target, mult=128):
    best = None
    t = mult
    while t <= min(n, target):
        if n % t == 0:
            best = t
        t += mult
    return best if best is not None else n


def _params(*sem):
    return pltpu.CompilerParams(dimension_semantics=tuple(sem) if sem else None, vmem_limit_bytes=VMEM_LIMIT)


def _sigmoid(x):
    return 0.5 * jnp.tanh(0.5 * x) + 0.5


def _seg_allsum64(x, terms=1, mean=False):
    tr, w = x.shape
    cw = 256 if w % 256 == 0 else 128
    n = w // cw
    shift = HEAD_DIM.bit_length() - 1
    ri = lax.shift_right_logical(lax.broadcasted_iota(jnp.int32, (cw, cw), 0), shift)
    ci = lax.shift_right_logical(lax.broadcasted_iota(jnp.int32, (cw, cw), 1), shift)
    ones = jnp.where(ri == ci, 1.0 / HEAD_DIM if mean else 1.0, 0.0).astype(BF16)

    def stack(v):
        return jnp.concatenate([v[:, j * cw:(j + 1) * cw] for j in range(n)], axis=0)

    hi = x.astype(BF16)
    s = jnp.dot(stack(hi), ones, preferred_element_type=F32)
    if terms == 2:
        lo = (x - hi.astype(F32)).astype(BF16)
        s = s + jnp.dot(stack(lo), ones, preferred_element_type=F32)
    return jnp.concatenate([s[j * tr:(j + 1) * tr] for j in range(n)], axis=1)


def _colsum(x):
    return jnp.sum(x, axis=0, keepdims=True)


def _shards_view(w, kind):
    return w if kind == "col" else w.reshape(1, -1, w.shape[2])


def _mm_nn(name, a, w, kind, *, out_dtype=F32, resid=None, norm_gain=None, heads=None, plan=None):
    t = a.shape[0]
    w = _shards_view(w, kind)
    ns, k, c = w.shape
    n = ns * c
    tm = _tile(t, 1024 if norm_gain is None else 512, 8)
    tk = _tile(k, 2048)
    tn = _tile(c, 1024)
    nk = k // tk
    per = c // tn
    assert norm_gain is None or tn == n, "the fused RMSNorm needs whole rows in one tile"
    assert norm_gain is None or heads is None
    n_in = 2 + (resid is not None) + (norm_gain is not None) + (4 if heads is not None else 0)
    if heads is not None:
        h_gain, h_tables, h_width, h_seq = heads
        assert h_width % tn == 0 and h_seq % tm == 0
        h_blocks = h_width // tn

    def body(*refs):
        a_ref, w_ref = refs[:2]
        r_ref = refs[2] if resid is not None else None
        g_ref = refs[n_in - 1] if norm_gain is not None else None
        o_ref = refs[n_in]
        part = jnp.dot(a_ref[...].astype(BF16), w_ref[...], preferred_element_type=F32)

        def finish(out):
            if resid is not None:
                out = out + r_ref[...]
            stored = out.astype(out_dtype)
            o_ref[...] = stored
            if norm_gain is not None:
                normed = out * lax.rsqrt(jnp.mean(out * out, axis=-1, keepdims=True) + EPS) * g_ref[...]
                refs[n_in + 1][...] = normed.astype(BF16)
            if heads is not None:
                hg_ref, tc_ref, ta_ref, tb_ref = refs[n_in - 4:n_in]

                @pl.when(pl.program_id(1) < h_blocks)
                def _():
                    refs[n_in + 1][...] = _hnr_fwd_math(stored.astype(F32), hg_ref[...], tc_ref, ta_ref, tb_ref)

        if nk == 1:
            finish(part)
            return
        acc = refs[-1]
        kk = pl.program_id(2)

        @pl.when(kk == 0)
        def _():
            acc[...] = part

        @pl.when(kk > 0)
        def _():
            acc[...] += part

        @pl.when(kk == nk - 1)
        def _():
            finish(acc[...])

    in_specs = [pl.BlockSpec((tm, tk), lambda i, j, kk: (i, kk)),
                pl.BlockSpec((None, tk, tn), lambda i, j, kk: (j // per, kk, j % per))]
    args = [a, w]
    if resid is not None:
        in_specs.append(pl.BlockSpec((tm, tn), lambda i, j, kk: (i, j)))
        args.append(resid)
    out_specs = [pl.BlockSpec((tm, tn), lambda i, j, kk: (i, j))]
    out_shape = [jax.ShapeDtypeStruct((t, n), out_dtype)]
    if norm_gain is not None:
        in_specs.append(pl.BlockSpec((1, n), lambda i, j, kk: (0, 0)))
        args.append(norm_gain)
        out_specs.append(pl.BlockSpec((tm, tn), lambda i, j, kk: (i, j)))
        out_shape.append(jax.ShapeDtypeStruct((t, n), BF16))
    if heads is not None:
        last = h_blocks - 1
        per_seq = h_seq // tm
        tab = pl.BlockSpec((tm, 128), lambda i, j, kk: (i % per_seq, 0))
        in_specs += [pl.BlockSpec((1, tn), lambda i, j, kk: (0, jnp.minimum(j, last))), tab, tab, tab]
        args += [h_gain, *h_tables]
        out_specs.append(pl.BlockSpec((tm, tn), lambda i, j, kk: (i, jnp.minimum(j, last))))
        out_shape.append(jax.ShapeDtypeStruct((t, h_width), F32))
    outs, carried = _hosted_call(
        body, plan, name=name, grid=(t // tm, n // tn, nk), in_specs=in_specs, out_specs=out_specs, out_shape=out_shape,
        scratch_shapes=[pltpu.VMEM((tm, tn), F32)] if nk > 1 else [],
        args=args, sem=("parallel", "arbitrary", "arbitrary"))
    out = outs[0] if len(outs) == 1 else tuple(outs)
    return out if plan is None else (out, carried)


def _mm_nt(name, d, w, kind, *, out_dtype=F32, plan=None):
    t = d.shape[0]
    w = _shards_view(w, kind)
    ns, k, c = w.shape
    n = ns * c
    tm = _tile(t, 1024, 8)
    to = _tile(k, 1024)
    tc = _tile(c, 1536)
    nc = n // tc
    per = c // tc

    def body(d_ref, w_ref, o_ref, *scratch):
        part = lax.dot_general(d_ref[...].astype(BF16), w_ref[...], (((1,), (1,)), ((), ())),
                               preferred_element_type=F32)
        if nc == 1:
            o_ref[...] = part.astype(out_dtype)
            return
        acc = scratch[0]
        kk = pl.program_id(2)

        @pl.when(kk == 0)
        def _():
            acc[...] = part

        @pl.when(kk > 0)
        def _():
            acc[...] += part

        @pl.when(kk == nc - 1)
        def _():
            o_ref[...] = acc[...].astype(out_dtype)

    (out,), carried = _hosted_call(
        body, plan, name=name, grid=(t // tm, k // to, nc),
        in_specs=[pl.BlockSpec((tm, tc), lambda i, j, kk: (i, kk)),
                  pl.BlockSpec((None, to, tc), lambda i, j, kk: (kk // per, j, kk % per))],
        out_specs=[pl.BlockSpec((tm, to), lambda i, j, kk: (i, j))],
        out_shape=[jax.ShapeDtypeStruct((t, k), out_dtype)],
        scratch_shapes=[pltpu.VMEM((tm, to), F32)] if nc > 1 else [],
        args=[d, w], sem=("parallel", "parallel", "arbitrary"))
    return out if plan is None else (out, carried)


def _mm_tn(name, a, d, kind, shard_shape, whole=False):
    t, k = a.shape
    n = d.shape[1]
    ns = N_CHIPS if kind == "col" and not whole else 1
    c = n // ns
    tkm = _tile(k, 1024)
    tn = _tile(c, 1536)
    tt = _tile(t, 1024, 8)
    nt = t // tt
    per = c // tn

    def body(a_ref, d_ref, o_ref, acc):
        kk = pl.program_id(2)
        part = lax.dot_general(a_ref[...].astype(BF16), d_ref[...].astype(BF16), (((0,), (0,)), ((), ())),
                               preferred_element_type=F32)

        @pl.when(kk == 0)
        def _():
            acc[...] = part

        @pl.when(kk > 0)
        def _():
            acc[...] += part

        @pl.when(kk == nt - 1)
        def _():
            o_ref[...] = acc[...].astype(BF16)

    out = pl.pallas_call(
        body, name=name, grid=(k // tkm, n // tn, nt),
        in_specs=[pl.BlockSpec((tt, tkm), lambda i, j, kk: (kk, i)),
                  pl.BlockSpec((tt, tn), lambda i, j, kk: (kk, j))],
        out_specs=pl.BlockSpec((None, tkm, tn), lambda i, j, kk: (j // per, i, j % per)),
        out_shape=jax.ShapeDtypeStruct((ns, k, c), BF16),
        scratch_shapes=[pltpu.VMEM((tkm, tn), F32)],
        compiler_params=_params("parallel", "parallel", "arbitrary"))(a, d)
    if kind == "col" and whole:
        return out.reshape(k, N_CHIPS, n // N_CHIPS).transpose(1, 0, 2)
    return out.reshape((N_CHIPS,) + tuple(shard_shape))


def _row_spec(tr, w, col=0):
    return pl.BlockSpec((tr, w), lambda i: (i, col))


def _full_spec(shape):
    return pl.BlockSpec(shape, lambda i: tuple(0 for _ in shape))


def _rms_fwd(name, x, gains, plan=None):
    t, dm = x.shape
    tr = _tile(t, 256, 8)
    n = len(gains)

    def body(x_ref, *refs):
        xv = x_ref[...]
        xn = xv * lax.rsqrt(jnp.mean(xv * xv, axis=-1, keepdims=True) + EPS)
        for g_ref, o_ref in zip(refs[:n], refs[n:]):
            o_ref[...] = (xn * g_ref[...]).astype(BF16)

    outs, carried = _hosted_call(
        body, plan, name=name, grid=(t // tr,),
        in_specs=[_row_spec(tr, dm)] + [_full_spec((1, dm))] * n,
        out_specs=[_row_spec(tr, dm)] * n,
        out_shape=[jax.ShapeDtypeStruct((t, dm), BF16)] * n,
        scratch_shapes=[], args=[x, *gains], sem=("parallel",))
    return outs if plan is None else (outs, carried)


def _ple_bwd_math(dy, gpre, pp):
    sg = _sigmoid(gpre)
    return (dy * pp * sg * (1.0 - sg)).astype(BF16), (dy * sg).astype(BF16)


def _rms_bwd(name, x, resid, pairs, ple=None):
    t, dm = x.shape
    tr = _tile(t, 256, 8)
    n = len(pairs)
    n_in = 2 * n + (2 if ple is not None else 0)

    def body(x_ref, r_ref, *refs):
        ins, outs = refs[:n_in], refs[n_in:]
        i = pl.program_id(0)
        xv = x_ref[...]
        rs = lax.rsqrt(jnp.mean(xv * xv, axis=-1, keepdims=True) + EPS)
        xn = xv * rs
        total = r_ref[...]
        for kx in range(n):
            g_ref, du_ref = ins[2 * kx], ins[2 * kx + 1]
            dg_ref = outs[1 + kx]
            du = du_ref[...].astype(F32)

            @pl.when(i == 0)
            def _():
                dg_ref[...] = jnp.zeros_like(dg_ref)

            dg_ref[...] += _colsum(du * xn)
            dxh = du * g_ref[...]
            total = total + rs * (dxh - xn * jnp.mean(dxh * xn, axis=-1, keepdims=True))
        outs[0][...] = total
        if ple is not None:
            outs[1 + n][...], outs[2 + n][...] = _ple_bwd_math(total, ins[2 * n][...].astype(F32), ins[2 * n + 1][...].astype(F32))

    in_specs = [_row_spec(tr, dm), _row_spec(tr, dm)]
    args = [x, resid]
    for g, du in pairs:
        in_specs += [_full_spec((1, dm)), _row_spec(tr, dm)]
        args += [g, du]
    out_specs = [_row_spec(tr, dm)] + [_full_spec((1, dm))] * n
    out_shape = [jax.ShapeDtypeStruct((t, dm), F32)] + [jax.ShapeDtypeStruct((1, dm), F32)] * n
    if ple is not None:
        in_specs += [_row_spec(tr, dm)] * 2
        args += list(ple)
        out_specs += [_row_spec(tr, dm)] * 2
        out_shape += [jax.ShapeDtypeStruct((t, dm), BF16)] * 2
    outs = pl.pallas_call(
        body, name=name, grid=(t // tr,), in_specs=in_specs, out_specs=out_specs, out_shape=out_shape,
        compiler_params=_params("arbitrary"))(*args)
    if ple is not None:
        return outs[0], list(outs[1:1 + n]), outs[1 + n], outs[2 + n]
    return outs[0], list(outs[1:])


def _shifted_copies(ext, sh, rows):
    for s in range(1, SUBLANES):
        sh[s - 1] = ext[pl.ds(s, rows), :]


def _window(ext, sh, off, row0, rows, lanes):
    s = off % SUBLANES
    src = ext if s == 0 else sh.at[s - 1]
    return src[pl.ds(off - s + row0, rows), lanes]


def _mixa_fwd(name, pa, cw, cb, lg, lb, seq):
    t, w3 = pa.shape
    cc = w3 // 3
    tr = _tile(seq, 128, HALO)
    per_seq = seq // tr
    hb = tr // HALO
    lead = HALO - (CONV_WIDTH - 1)

    def body(a_ref, b_ref, z_ref, ah_ref, bh_ref, cw_ref, cb_ref, lg_ref, lb_ref, c_ref, m_ref, ext, sh, conv):
        i = pl.program_id(0)
        gh = ah_ref[...].astype(F32) * _sigmoid(bh_ref[...].astype(F32))
        ext[pl.ds(0, HALO), :] = jnp.where((i % per_seq) == 0, 0.0, gh)
        ext[pl.ds(HALO, tr), :] = a_ref[...].astype(F32) * _sigmoid(b_ref[...].astype(F32))
        _shifted_copies(ext, sh, tr + HALO - SUBLANES)
        for lc in range(cc // 128):
            lanes = pl.ds(lc * 128, 128)
            taps = [cw_ref[pl.ds(k, 1), lanes] for k in range(CONV_WIDTH)]
            for row0 in range(0, tr, CONV_ROWS):
                acc = jnp.broadcast_to(cb_ref[:, lanes], (CONV_ROWS, 128))
                for k in range(CONV_WIDTH):
                    acc = acc + _window(ext, sh, lead + k, row0, CONV_ROWS, lanes) * taps[k]
                conv[pl.ds(row0, CONV_ROWS), lanes] = acc
        acc = conv[...]
        c_ref[...] = acc.astype(BF16)
        xc = acc - jnp.mean(acc, axis=-1, keepdims=True)
        nrm = xc * lax.rsqrt(jnp.mean(xc * xc, axis=-1, keepdims=True) + EPS)
        l = nrm * lg_ref[...] + lb_ref[...]
        z = z_ref[...].astype(F32)
        m_ref[...] = (l * _sigmoid(l) * z * _sigmoid(z)).astype(BF16)

    halo = lambda col: pl.BlockSpec((HALO, cc), lambda i: (jnp.maximum(i * hb - 1, 0), col))
    return pl.pallas_call(
        body, name=name, grid=(t // tr,),
        in_specs=[_row_spec(tr, cc, 0), _row_spec(tr, cc, 1), _row_spec(tr, cc, 2), halo(0), halo(1),
                  _full_spec((HALO, cc)), _full_spec((1, cc)), _full_spec((1, cc)), _full_spec((1, cc))],
        out_specs=[_row_spec(tr, cc), _row_spec(tr, cc)],
        out_shape=[jax.ShapeDtypeStruct((t, cc), BF16), jax.ShapeDtypeStruct((t, cc), BF16)],
        scratch_shapes=[pltpu.VMEM((tr + HALO, cc), F32), pltpu.VMEM((SUBLANES - 1, tr + HALO - SUBLANES, cc), F32),
                        pltpu.VMEM((tr, cc), F32)],
        compiler_params=_params("parallel"))(pa, pa, pa, pa, pa, cw, cb, lg, lb)


def _mixa_bwd1(name, c, pa, dm, lg, lb, plan=None):
    t, cc = c.shape
    tr = _tile(t, 128, 8)

    def body(c_ref, z_ref, dm_ref, lg_ref, lb_ref, dc_ref, dz_ref, dlg_ref, dlb_ref, dcb_ref):
        i = pl.program_id(0)
        cv = c_ref[...].astype(F32)
        xc = cv - jnp.mean(cv, axis=-1, keepdims=True)
        rs = lax.rsqrt(jnp.mean(xc * xc, axis=-1, keepdims=True) + EPS)
        nrm = xc * rs
        l = nrm * lg_ref[...] + lb_ref[...]
        z = z_ref[...].astype(F32)
        sl, sz = _sigmoid(l), _sigmoid(z)
        dmv = dm_ref[...].astype(F32)
        ds = dmv * (z * sz)
        dzz = dmv * (l * sl)
        dz_ref[...] = (dzz * (sz * (1.0 + z * (1.0 - sz)))).astype(BF16)
        dl = ds * (sl * (1.0 + l * (1.0 - sl)))
        dn = dl * lg_ref[...]
        dc = rs * (dn - jnp.mean(dn, axis=-1, keepdims=True) - nrm * jnp.mean(dn * nrm, axis=-1, keepdims=True))
        dc_ref[...] = dc.astype(BF16)

        @pl.when(i == 0)
        def _():
            dlg_ref[...] = jnp.zeros_like(dlg_ref)
            dlb_ref[...] = jnp.zeros_like(dlb_ref)
            dcb_ref[...] = jnp.zeros_like(dcb_ref)

        dlg_ref[...] += _colsum(dl * nrm)
        dlb_ref[...] += _colsum(dl)
        dcb_ref[...] += _colsum(dc)

    vec = jax.ShapeDtypeStruct((1, cc), F32)
    outs, carried = _hosted_call(
        body, plan, name=name, grid=(t // tr,),
        in_specs=[_row_spec(tr, cc), _row_spec(tr, cc, 2), _row_spec(tr, cc), _full_spec((1, cc)), _full_spec((1, cc))],
        out_specs=[_row_spec(tr, cc), _row_spec(tr, cc)] + [_full_spec((1, cc))] * 3,
        out_shape=[jax.ShapeDtypeStruct((t, cc), BF16), jax.ShapeDtypeStruct((t, cc), BF16), vec, vec, vec],
        scratch_shapes=[], args=[c, pa, dm, lg, lb], sem=("arbitrary",))
    return outs if plan is None else (outs, carried)


def _mixa_bwd2(name, dc, pa, dz, cw, seq):
    t, cc = dc.shape
    tr = _tile(seq, 128, HALO)
    per_seq = seq // tr
    hb = tr // HALO
    steps = t // tr
    last_halo = t // HALO - 1

    def body(dc_ref, dcn_ref, a_ref, b_ref, dz_ref, cw_ref, dp_ref, dcw_ref, ext, sh, sums):
        i = pl.program_id(0)
        ext[pl.ds(0, tr), :] = dc_ref[...].astype(F32)
        ext[pl.ds(tr, HALO), :] = jnp.where((i % per_seq) == per_seq - 1, 0.0, dcn_ref[...].astype(F32))
        _shifted_copies(ext, sh, tr + HALO - SUBLANES)

        @pl.when(i == 0)
        def _():
            sums[...] = jnp.zeros_like(sums)
            dcw_ref[...] = jnp.zeros_like(dcw_ref)

        av = a_ref[...].astype(F32)
        sb = _sigmoid(b_ref[...].astype(F32))
        glu = av * sb
        dglu = jnp.zeros((tr, cc), F32)
        for k in range(CONV_WIDTH):
            wd = _window(ext, sh, CONV_WIDTH - 1 - k, 0, tr, slice(None))
            dglu = dglu + wd * cw_ref[pl.ds(k, 1), :]
            sums[pl.ds(k * SUBLANES, SUBLANES), :] += (wd * glu).reshape(tr // SUBLANES, SUBLANES, cc).sum(axis=0)
        dp_ref[:, pl.ds(0, cc)] = (dglu * sb).astype(BF16)
        dp_ref[:, pl.ds(cc, cc)] = (dglu * av * sb * (1.0 - sb)).astype(BF16)
        dp_ref[:, pl.ds(2 * cc, cc)] = dz_ref[...]

        @pl.when(i == steps - 1)
        def _():
            for k in range(CONV_WIDTH):
                dcw_ref[pl.ds(k, 1), :] = _colsum(sums[pl.ds(k * SUBLANES, SUBLANES), :])

    nxt = pl.BlockSpec((HALO, cc), lambda i: (jnp.minimum((i + 1) * hb, last_halo), 0))
    return pl.pallas_call(
        body, name=name, grid=(steps,),
        in_specs=[_row_spec(tr, cc), nxt, _row_spec(tr, cc, 0), _row_spec(tr, cc, 1), _row_spec(tr, cc),
                  _full_spec((HALO, cc))],
        out_specs=[_row_spec(tr, 3 * cc), _full_spec((HALO, cc))],
        out_shape=[jax.ShapeDtypeStruct((t, 3 * cc), BF16), jax.ShapeDtypeStruct((HALO, cc), F32)],
        scratch_shapes=[pltpu.VMEM((tr + HALO, cc), F32), pltpu.VMEM((SUBLANES - 1, tr + HALO - SUBLANES, cc), F32),
                        pltpu.VMEM((HALO * SUBLANES, cc), F32)],
        compiler_params=_params("arbitrary"))(dc, dc, pa, pa, dz, cw)


def _ple_fwd(name, h, gpre, pp, gains):
    t, dm = h.shape
    tr = _tile(t, 256, 8)

    n = len(gains)

    def body(h_ref, g_ref, p_ref, *refs):
        o_ref = refs[n]
        xv = h_ref[...] + _sigmoid(g_ref[...].astype(F32)) * p_ref[...].astype(F32)
        o_ref[...] = xv
        xn = xv * lax.rsqrt(jnp.mean(xv * xv, axis=-1, keepdims=True) + EPS)
        for gain_ref, n_ref in zip(refs[:n], refs[n + 1:]):
            n_ref[...] = (xn * gain_ref[...]).astype(BF16)

    return pl.pallas_call(
        body, name=name, grid=(t // tr,), in_specs=[_row_spec(tr, dm)] * 3 + [_full_spec((1, dm))] * n,
        out_specs=[_row_spec(tr, dm)] * (1 + n),
        out_shape=[jax.ShapeDtypeStruct((t, dm), F32)] + [jax.ShapeDtypeStruct((t, dm), BF16)] * n,
        compiler_params=_params("parallel"))(h, gpre, pp, *gains)


def _ple_loss(name, h, gpre, pp, target):
    t, dm = h.shape
    tr = _tile(t, 256, 8)

    def body(h_ref, g_ref, p_ref, t_ref, dy_ref, dg_ref, dp_ref, sq_ref):
        i = pl.program_id(0)
        gpre_v, pp_v = g_ref[...].astype(F32), p_ref[...].astype(F32)
        err = h_ref[...] + _sigmoid(gpre_v) * pp_v - t_ref[...]
        dy = err * (1.0 / dm)
        dy_ref[...] = dy
        dg_ref[...], dp_ref[...] = _ple_bwd_math(dy, gpre_v, pp_v)

        @pl.when(i == 0)
        def _():
            sq_ref[...] = jnp.zeros_like(sq_ref)

        sq_ref[...] += jnp.sum(jnp.sum(err * err, axis=1, keepdims=True), axis=0, keepdims=True)

    return pl.pallas_call(
        body, name=name, grid=(t // tr,), in_specs=[_row_spec(tr, dm)] * 4,
        out_specs=[_row_spec(tr, dm)] * 3 + [_full_spec((1, 1))],
        out_shape=[jax.ShapeDtypeStruct((t, dm), F32)] + [jax.ShapeDtypeStruct((t, dm), BF16)] * 2
        + [jax.ShapeDtypeStruct((1, 1), F32)],
        compiler_params=_params("arbitrary"))(h, gpre, pp, target)


def _rope_tables(seq):
    half = ROPE_DIM // 2
    inv = ROPE_THETA ** (-jnp.arange(half, dtype=F32) * (2.0 / ROPE_DIM))
    ang = jnp.arange(seq).astype(F32)[:, None] * inv[None, :]
    cos, sin = jnp.cos(ang), jnp.sin(ang)
    rest = HEAD_DIM - ROPE_DIM
    one = jnp.ones((seq, rest), F32)
    zero = jnp.zeros((seq, rest), F32)
    zh = jnp.zeros((seq, half), F32)
    tc = jnp.concatenate([cos, cos, one], axis=1)
    ta = jnp.concatenate([-sin, zh, zero], axis=1)
    tb = jnp.concatenate([zh, sin, zero], axis=1)
    return [jnp.tile(tb_, (1, 128 // HEAD_DIM)) for tb_ in (tc, ta, tb)]


def _wide(tab_ref, w):
    return jnp.tile(tab_ref[...], (1, w // 128))


def _hnr_fwd_math(xv, gain, tc_ref, ta_ref, tb_ref):
    width = xv.shape[1]
    rs = lax.rsqrt(_seg_allsum64(xv * xv, mean=True) + EPS)
    y = xv * rs * gain
    return (y * _wide(tc_ref, width) + pltpu.roll(y, width - ROPE_DIM // 2, 1) * _wide(ta_ref, width)
            + pltpu.roll(y, ROPE_DIM // 2, 1) * _wide(tb_ref, width))


def _hnr_bwd_math(xv, gain, dout, tc, ta, tb, width):
    dy = dout * tc + pltpu.roll(dout * ta, ROPE_DIM // 2, 1) + pltpu.roll(dout * tb, width - ROPE_DIM // 2, 1)
    rs = lax.rsqrt(_seg_allsum64(xv * xv, mean=True) + EPS)
    xn = xv * rs
    dyh = dy * gain
    dx = rs * (dyh - xn * _seg_allsum64(dyh * xn, terms=2, mean=True))
    return dx, _colsum(dy * xn)


def _q_bwd(name, p1, gain, tables, dqs, dgt, seq):
    t, w4 = p1.shape
    da = w4 // 4
    width = 3 * da
    tr = _tile(seq, 128, 8)
    per_seq = seq // tr

    def body(x_ref, g_ref, tc_ref, ta_ref, tb_ref, d0_ref, d1_ref, d2_ref, dgt_ref, o_ref, dg_ref):
        i = pl.program_id(0)
        dout = jnp.concatenate([d0_ref[...], d1_ref[...], d2_ref[...]], axis=1)
        dx, dg = _hnr_bwd_math(x_ref[...].astype(F32), g_ref[...], dout, _wide(tc_ref, width), _wide(ta_ref, width),
                               _wide(tb_ref, width), width)

        @pl.when(i == 0)
        def _():
            dg_ref[...] = jnp.zeros_like(dg_ref)

        dg_ref[...] += dg
        o_ref[:, pl.ds(0, width)] = dx.astype(BF16)
        o_ref[:, pl.ds(width, da)] = dgt_ref[...]

    tab = pl.BlockSpec((tr, 128), lambda i: (i % per_seq, 0))
    return pl.pallas_call(
        body, name=name, grid=(t // tr,),
        in_specs=[_row_spec(tr, width), _full_spec((1, width)), tab, tab, tab] + [_row_spec(tr, da)] * 4,
        out_specs=[_row_spec(tr, w4), _full_spec((1, width))],
        out_shape=[jax.ShapeDtypeStruct((t, w4), BF16), jax.ShapeDtypeStruct((1, width), F32)],
        compiler_params=_params("arbitrary"))(p1, gain, *tables, *dqs, dgt)


def _k_bwd(name, kv, gain, tables, dk, dv, seq):
    t, w2 = kv.shape
    da = w2 // 2
    tr = _tile(seq, 256, 8)
    per_seq = seq // tr

    def body(x_ref, g_ref, tc_ref, ta_ref, tb_ref, dk_ref, dv_ref, o_ref, dg_ref):
        i = pl.program_id(0)
        dx, dg = _hnr_bwd_math(x_ref[...], g_ref[...], dk_ref[...], _wide(tc_ref, da), _wide(ta_ref, da),
                               _wide(tb_ref, da), da)

        @pl.when(i == 0)
        def _():
            dg_ref[...] = jnp.zeros_like(dg_ref)

        dg_ref[...] += dg
        o_ref[:, pl.ds(0, da)] = dx.astype(BF16)
        o_ref[:, pl.ds(da, da)] = dv_ref[...].astype(BF16)

    tab = pl.BlockSpec((tr, 128), lambda i: (i % per_seq, 0))
    return pl.pallas_call(
        body, name=name, grid=(t // tr,),
        in_specs=[_row_spec(tr, da), _full_spec((1, da)), tab, tab, tab] + [_row_spec(tr, da)] * 2,
        out_specs=[_row_spec(tr, w2), _full_spec((1, da))],
        out_shape=[jax.ShapeDtypeStruct((t, w2), BF16), jax.ShapeDtypeStruct((1, da), F32)],
        compiler_params=_params("arbitrary"))(kv, gain, *tables, dk, dv)


def _unit_index(dil, r, blk):
    if dil == 1:
        start = blk * SPAN
        return pl.ds(start if isinstance(start, int) else pl.multiple_of(start, SPAN), SPAN)
    return pl.ds(r + dil * SPAN * blk, SPAN, stride=dil)


def _unit_rows(ref, dil, r, blk):
    return ref[_unit_index(dil, r, blk), :]


def _store_rows(ref, dil, r, blk, val):
    ref[_unit_index(dil, r, blk), :] = val


def _over_units(dil, nblk, unit, carry0, after=None):
    for r in range(dil):
        carry = carry0
        for blk in range(nblk):
            carry = unit(r, blk, blk > 0, carry)
        if after is not None:
            after(r, carry)


def _band_mask(with_prev):
    nk = 2 * SPAN if with_prev else SPAN
    qi = lax.broadcasted_iota(jnp.int32, (SPAN, nk), 0)
    kj = lax.broadcasted_iota(jnp.int32, (SPAN, nk), 1)
    if with_prev:
        return (kj >= qi) & (kj <= qi + SPAN)
    return kj <= qi


_NT = (((1,), (1,)), ((), ()))
_TN = (((0,), (0,)), ((), ()))
N_PAIR = 128 // HEAD_DIM


def _per_head(x):
    head = lax.shift_right_logical(lax.broadcasted_iota(jnp.int32, x.shape, 1), HEAD_DIM.bit_length() - 1)
    return [jnp.where(head == hh, x, 0.0).astype(BF16) for hh in range(N_PAIR)]


def _head_columns(stat):
    return [stat[:, hh * HEAD_DIM:hh * HEAD_DIM + 1] for hh in range(N_PAIR)]


def _by_head(cols):
    head = lax.shift_right_logical(lax.broadcasted_iota(jnp.int32, (cols[0].shape[0], 128), 1), HEAD_DIM.bit_length() - 1)
    out = cols[-1]
    for hh in range(N_PAIR - 2, -1, -1):
        out = jnp.where(head == hh, cols[hh], out)
    return jnp.broadcast_to(out, (cols[0].shape[0], 128))


def _group_fwd(q_ref, k_ref, v_ref, o_ref, l_ref, dil, seq):
    nblk = seq // (dil * SPAN)
    scale = HEAD_DIM ** -0.5

    def unit(r, blk, with_prev, carry):
        q = (_unit_rows(q_ref, dil, r, blk) * scale).astype(BF16)
        kc = _per_head(_unit_rows(k_ref, dil, r, blk))
        vc = _per_head(_unit_rows(v_ref, dil, r, blk))
        mask = _band_mask(with_prev)
        nk = 2 * SPAN if with_prev else SPAN
        khs = [jnp.concatenate([carry[0][hh], kc[hh]], axis=0) if with_prev else kc[hh] for hh in range(N_PAIR)]
        vhs = [jnp.concatenate([carry[1][hh], vc[hh]], axis=0) if with_prev else vc[hh] for hh in range(N_PAIR)]
        s_all = lax.dot_general(q, jnp.concatenate(khs, axis=0), _NT, preferred_element_type=F32)
        ps, lses, rdens = [], [], []
        for hh in range(N_PAIR):
            s = jnp.where(mask, s_all[:, hh * nk:(hh + 1) * nk], NEG_INF)
            mx = jnp.max(s, axis=-1, keepdims=True)
            p = jnp.exp(s - mx)
            den = jnp.sum(p, axis=-1, keepdims=True)
            ps.append(p.astype(BF16))
            lses.append(mx + jnp.log(den))
            rdens.append(1.0 / den)
        out = jnp.dot(jnp.concatenate(ps, axis=1), jnp.concatenate(vhs, axis=0), preferred_element_type=F32)
        _store_rows(o_ref, dil, r, blk, out * _by_head(rdens))
        _store_rows(l_ref, dil, r, blk, _by_head(lses))
        return kc, vc

    _over_units(dil, nblk, unit, None)


def _attn_fwd(name, qn, kn, kv, p1, seq):
    t, da = kn.shape
    hp = da // 128
    ng = len(DILATIONS)

    def body(q0_ref, q1_ref, q2_ref, k_ref, v_ref, g_ref, o_ref, l_ref, m_ref, og, lg):
        for g, q_ref in enumerate((q0_ref, q1_ref, q2_ref)):
            _group_fwd(q_ref, k_ref, v_ref, og.at[g], lg.at[g], DILATIONS[g], seq)
        a0, a1, a2 = lg[0], lg[1], lg[2]
        mx = jnp.maximum(jnp.maximum(a0, a1), a2)
        e0, e1, e2 = jnp.exp(a0 - mx), jnp.exp(a1 - mx), jnp.exp(a2 - mx)
        den = e0 + e1 + e2
        o = (e0 * og[0] + e1 * og[1] + e2 * og[2]) / den
        o_ref[...] = o
        l_ref[...] = mx + jnp.log(den)
        gt = g_ref[...].astype(F32)
        m_ref[...] = (o * gt * _sigmoid(gt)).astype(BF16)

    blk_spec = lambda off: pl.BlockSpec((seq, 128), lambda b, h: (b, off + h))
    return pl.pallas_call(
        body, name=name, grid=(t // seq, hp),
        in_specs=[blk_spec(0), blk_spec(hp), blk_spec(2 * hp), blk_spec(0), blk_spec(hp), blk_spec(3 * hp)],
        out_specs=[blk_spec(0)] * 3,
        out_shape=[jax.ShapeDtypeStruct((t, da), F32)] * 2 + [jax.ShapeDtypeStruct((t, da), BF16)],
        scratch_shapes=[pltpu.VMEM((ng, seq, 128), F32), pltpu.VMEM((ng, seq, 128), F32)],
        compiler_params=_params("parallel", "parallel"))(qn, qn, qn, kn, kv, p1)


def _group_bwd(q_ref, k_ref, v_ref, do_ref, l_ref, d_ref, dq_ref, dk_ref, dv_ref, dil, seq, first):
    nblk = seq // (dil * SPAN)
    scale = HEAD_DIM ** -0.5

    def put(ref, r, blk, val):
        if not first:
            val = val + _unit_rows(ref, dil, r, blk)
        _store_rows(ref, dil, r, blk, val)

    def unit(r, blk, with_prev, carry):
        kp, vp, pend_k, pend_v = carry if with_prev else (None,) * 4
        q = _unit_rows(q_ref, dil, r, blk) * scale
        kc = _per_head(_unit_rows(k_ref, dil, r, blk))
        vc = _per_head(_unit_rows(v_ref, dil, r, blk))
        dov = _unit_rows(do_ref, dil, r, blk)
        lcols = _head_columns(_unit_rows(l_ref, dil, r, blk))
        dcols = _head_columns(_unit_rows(d_ref, dil, r, blk))
        mask = _band_mask(with_prev)
        qb, dob = q.astype(BF16), dov.astype(BF16)
        nk = 2 * SPAN if with_prev else SPAN
        khs = [jnp.concatenate([kp[hh], kc[hh]], axis=0) if with_prev else kc[hh] for hh in range(N_PAIR)]
        vhs = [jnp.concatenate([vp[hh], vc[hh]], axis=0) if with_prev else vc[hh] for hh in range(N_PAIR)]
        kall, vall = jnp.concatenate(khs, axis=0), jnp.concatenate(vhs, axis=0)
        s_all = lax.dot_general(qb, kall, _NT, preferred_element_type=F32)
        dp_all = lax.dot_general(dob, vall, _NT, preferred_element_type=F32)
        ps, dss = [], []
        for hh in range(N_PAIR):
            cols = slice(hh * nk, (hh + 1) * nk)
            p = jnp.where(mask, jnp.exp(s_all[:, cols] - lcols[hh]), 0.0)
            ps.append(p.astype(BF16))
            dss.append((p * (dp_all[:, cols] - dcols[hh])).astype(BF16))
        dq = jnp.dot(jnp.concatenate(dss, axis=1), kall, preferred_element_type=F32)
        qall = jnp.concatenate(_per_head(q), axis=0)
        doall = jnp.concatenate(_per_head(dov), axis=0)
        dkcat = lax.dot_general(jnp.concatenate(dss, axis=0), qall, _TN, preferred_element_type=F32)
        dvcat = lax.dot_general(jnp.concatenate(ps, axis=0), doall, _TN, preferred_element_type=F32)
        _store_rows(dq_ref, dil, r, blk, dq * scale)
        if with_prev:
            put(dk_ref, r, blk - 1, pend_k + dkcat[:SPAN])
            put(dv_ref, r, blk - 1, pend_v + dvcat[:SPAN])
            return kc, vc, dkcat[SPAN:], dvcat[SPAN:]
        return kc, vc, dkcat, dvcat

    def after(r, carry):
        put(dk_ref, r, nblk - 1, carry[2])
        put(dv_ref, r, nblk - 1, carry[3])

    _over_units(dil, nblk, unit, None, after)


def _attn_bwd(name, qn, kn, kv, do, lse, dsum, seq):
    t, da = kn.shape
    hp = da // 128

    def body(q0_ref, q1_ref, q2_ref, k_ref, v_ref, do_ref, l_ref, d_ref, dq0_ref, dq1_ref, dq2_ref, dk_ref, dv_ref):
        groups = ((q0_ref, dq0_ref), (q1_ref, dq1_ref), (q2_ref, dq2_ref))
        for g, (q_ref, dq_ref) in enumerate(groups):
            _group_bwd(q_ref, k_ref, v_ref, do_ref, l_ref, d_ref, dq_ref, dk_ref, dv_ref, DILATIONS[g], seq, g == 0)

    blk_spec = lambda off: pl.BlockSpec((seq, 128), lambda b, h: (b, off + h))
    return pl.pallas_call(
        body, name=name, grid=(t // seq, hp),
        in_specs=[blk_spec(0), blk_spec(hp), blk_spec(2 * hp), blk_spec(0), blk_spec(hp), blk_spec(0), blk_spec(0), blk_spec(0)],
        out_specs=[blk_spec(0)] * 5,
        out_shape=[jax.ShapeDtypeStruct((t, da), F32)] * 5,
        compiler_params=_params("parallel", "parallel"))(qn, qn, qn, kn, kv, do, lse, dsum)


def _gate_bwd(name, dm, o, p1):
    t, da = o.shape
    tr = _tile(t, 256, 8)

    def body(dm_ref, o_ref, g_ref, do_ref, dg_ref, ds_ref):
        g = g_ref[...].astype(F32)
        sg = _sigmoid(g)
        dmv, ov = dm_ref[...].astype(F32), o_ref[...]
        do = dmv * (g * sg)
        do_ref[...] = do
        dg_ref[...] = (dmv * ov * (sg * (1.0 + g * (1.0 - sg)))).astype(BF16)
        ds_ref[...] = _seg_allsum64(do * ov, terms=2)

    return pl.pallas_call(
        body, name=name, grid=(t // tr,),
        in_specs=[_row_spec(tr, da), _row_spec(tr, da), _row_spec(tr, da, 3)],
        out_specs=[_row_spec(tr, da)] * 3,
        out_shape=[jax.ShapeDtypeStruct((t, da), F32), jax.ShapeDtypeStruct((t, da), BF16), jax.ShapeDtypeStruct((t, da), F32)],
        compiler_params=_params("parallel"))(dm, o, p1)


def _cast_bf16(name, w2d, chip):
    r, c = w2d.shape
    tr = _tile(r, 256, 16)

    def body(chip_ref, x_ref, o_ref):
        o_ref[...] = x_ref[...].astype(BF16)

    grid_spec = pltpu.PrefetchScalarGridSpec(
        num_scalar_prefetch=1, grid=(r // tr,),
        in_specs=[pl.BlockSpec((tr, c), lambda i, m: (i, 0))],
        out_specs=pl.BlockSpec((None, tr, c), lambda i, m: (m[0], i, 0)))
    return pl.pallas_call(
        body, name=name, grid_spec=grid_spec, out_shape=jax.ShapeDtypeStruct((N_CHIPS, r, c), BF16),
        compiler_params=_params("parallel"))(chip, w2d)


def _adamw(name, w, g, m, v):
    r, c = w.shape
    tr = _tile(r, 256, 8)
    c1 = 1.0 - ADAM_B1 ** ADAM_STEP
    c2 = 1.0 - ADAM_B2 ** ADAM_STEP

    def body(w_ref, g_ref, m_ref, v_ref, d_ref, nm_ref, nv_ref):
        gv = g_ref[...]
        nm = ADAM_B1 * m_ref[...] + (1.0 - ADAM_B1) * gv
        nv = ADAM_B2 * v_ref[...] + (1.0 - ADAM_B2) * (gv * gv)
        nm_ref[...] = nm
        nv_ref[...] = nv
        d_ref[...] = -ADAM_LR * ((nm / c1) / (jnp.sqrt(nv / c2) + ADAM_EPS) + ADAM_WD * w_ref[...])

    sds = jax.ShapeDtypeStruct((r, c), F32)
    return pl.pallas_call(
        body, name=name, grid=(r // tr,), in_specs=[_row_spec(tr, c)] * 4, out_specs=[_row_spec(tr, c)] * 3,
        out_shape=[sds] * 3, compiler_params=_params("parallel"))(w, g, m, v)


def _pair_sum(name, gd, recv, core):
    _, r, c = gd.shape
    rh = r // 2
    tr = _tile(rh, 256, 8)
    nrt = rh // tr

    def body(c_ref, a_ref, b_ref, o_ref):
        o_ref[...] = (a_ref[...].astype(F32) + b_ref[...].astype(F32)).astype(BF16)

    grid_spec = pltpu.PrefetchScalarGridSpec(
        num_scalar_prefetch=1, grid=(N_CHIPS, nrt),
        in_specs=[pl.BlockSpec((None, tr, c), lambda j, i, cr: (j, cr[0] * nrt + i, 0)),
                  pl.BlockSpec((None, tr, c), lambda j, i, cr: (j, i, 0))],
        out_specs=pl.BlockSpec((None, tr, c), lambda j, i, cr: (j, i, 0)))
    return pl.pallas_call(
        body, name=name, grid_spec=grid_spec, out_shape=jax.ShapeDtypeStruct((N_CHIPS, rh, c), BF16),
        compiler_params=_params("parallel", "parallel"))(core, gd, recv)


def _chip_sum(name, sums, parts, order):
    _, rh, c = parts.shape
    tr = _tile(rh, 256, 16)
    nrt = rh // tr

    def body(o_ref_, s_ref, p1_ref, p2_ref, p3_ref, o_ref):
        acc = s_ref[...].astype(F32)
        for p_ref in (p1_ref, p2_ref, p3_ref):
            acc = acc + p_ref[...].astype(F32)
        o_ref[...] = acc

    slot = lambda k: pl.BlockSpec((None, tr, c), lambda i, o: (o[k], i, 0))
    grid_spec = pltpu.PrefetchScalarGridSpec(
        num_scalar_prefetch=1, grid=(nrt,),
        in_specs=[slot(0), slot(1), slot(2), slot(3)],
        out_specs=pl.BlockSpec((tr, c), lambda i, o: (o[N_CHIPS] * nrt + i, 0)))
    return pl.pallas_call(
        body, name=name, grid_spec=grid_spec, out_shape=jax.ShapeDtypeStruct((2 * rh, c), F32),
        compiler_params=_params("parallel"))(order, sums, parts, parts, parts)


HBM = pl.BlockSpec(memory_space=pl.ANY)


def _place():
    x, y, c = lax.axis_index("x"), lax.axis_index("y"), lax.axis_index("c")
    chips = [(1 - x, y), (x, 1 - y), (1 - x, 1 - y)]
    return x, y, c, chips


def _half(ref, hc):
    rows = ref.shape[0] // 2
    return ref.at[pl.ds(hc * rows, rows)]


class _Plan:
    def __init__(self, ins, out_shapes, aliases, n_sems, start, finish):
        self.ins, self.out_shapes, self.aliases, self.n_sems = list(ins), list(out_shapes), dict(aliases), n_sems
        self.start, self.finish = start, finish


def _gather_plan(shards):
    n = len(shards)

    def ici(place, outs, send_sems, recv_sems, i, k, slot):
        x, y, c, chips = place
        half = _half(outs[i].at[slot], c)
        return pltpu.make_async_remote_copy(
            src_ref=half, dst_ref=half, send_sem=send_sems.at[6 * i + k], recv_sem=recv_sems.at[6 * i + k],
            device_id=(chips[k][0], chips[k][1], c), device_id_type=MESH)

    def d2d(place, outs, send_sems, recv_sems, i, k, slot, hc):
        x, y, c, chips = place
        half = _half(outs[i].at[slot], hc)
        return pltpu.make_async_remote_copy(
            src_ref=half, dst_ref=half, send_sem=send_sems.at[6 * i + 3 + k], recv_sem=recv_sems.at[6 * i + 3 + k],
            device_id=(x, y, 1 - c), device_id_type=MESH)

    def start(place, ins, outs, send_sems, recv_sems):
        x, y, c, chips = place
        for i in range(n):
            for k in range(3):
                ici(place, outs, send_sems, recv_sems, i, k, 2 * x + y).start()

    def finish(place, ins, outs, send_sems, recv_sems):
        x, y, c, chips = place
        theirs = [2 * chip[0] + chip[1] for chip in chips]
        for i in range(n):
            for k in range(3):
                ici(place, outs, send_sems, recv_sems, i, k, theirs[k]).wait_recv()
                d2d(place, outs, send_sems, recv_sems, i, k, theirs[k], c).start()
        for i in range(n):
            for k in range(3):
                d2d(place, outs, send_sems, recv_sems, i, k, theirs[k], 1 - c).wait_recv()
        for i in range(n):
            for k in range(3):
                ici(place, outs, send_sems, recv_sems, i, k, 2 * x + y).wait_send()
                d2d(place, outs, send_sems, recv_sems, i, k, theirs[k], c).wait_send()

    return _Plan(shards, [jax.ShapeDtypeStruct(s.shape, s.dtype) for s in shards], {i: i for i in range(n)}, 6 * n,
                 start, finish)


def _scatter_plan(sums):
    n = len(sums)

    def copy(place, ins, outs, send_sems, recv_sems, i, k, src_slot, dst_slot):
        x, y, c, chips = place
        return pltpu.make_async_remote_copy(
            src_ref=ins[i].at[src_slot], dst_ref=outs[i].at[dst_slot],
            send_sem=send_sems.at[3 * i + k], recv_sem=recv_sems.at[3 * i + k],
            device_id=(chips[k][0], chips[k][1], c), device_id_type=MESH)

    def start(place, ins, outs, send_sems, recv_sems):
        x, y, c, chips = place
        for i in range(n):
            for k, chip in enumerate(chips):
                copy(place, ins, outs, send_sems, recv_sems, i, k, 2 * chip[0] + chip[1], 2 * x + y).start()

    def finish(place, ins, outs, send_sems, recv_sems):
        x, y, c, chips = place
        for i in range(n):
            for k, chip in enumerate(chips):
                theirs = 2 * chip[0] + chip[1]
                copy(place, ins, outs, send_sems, recv_sems, i, k, theirs, 2 * x + y).wait_send()
                copy(place, ins, outs, send_sems, recv_sems, i, k, theirs, theirs).wait_recv()

    return _Plan(sums, [jax.ShapeDtypeStruct(s.shape, s.dtype) for s in sums], {}, 3 * n, start, finish)


def _exchange_plan(grads):
    n = len(grads)

    def copy(place, ins, outs, send_sems, recv_sems, i):
        x, y, c, _ = place
        rows = ins[i].shape[1] // 2
        return pltpu.make_async_remote_copy(
            src_ref=ins[i].at[:, pl.ds((1 - c) * rows, rows), :], dst_ref=outs[i],
            send_sem=send_sems.at[i], recv_sem=recv_sems.at[i], device_id=(x, y, 1 - c), device_id_type=MESH)

    def start(place, ins, outs, send_sems, recv_sems):
        for i in range(n):
            copy(place, ins, outs, send_sems, recv_sems, i).start()

    def finish(place, ins, outs, send_sems, recv_sems):
        for i in range(n):
            copy(place, ins, outs, send_sems, recv_sems, i).wait()

    shapes = [jax.ShapeDtypeStruct((N_CHIPS, g.shape[1] // 2, g.shape[2]), g.dtype) for g in grads]
    return _Plan(grads, shapes, {}, n, start, finish)


def _hosted_call(body, plan, *, name, grid, in_specs, out_specs, out_shape, scratch_shapes, args, sem):
    in_specs, out_specs, out_shape, scratch_shapes = list(in_specs), list(out_specs), list(out_shape), list(scratch_shapes)
    if plan is None:
        res = pl.pallas_call(body, name=name, grid=grid, in_specs=in_specs, out_specs=out_specs, out_shape=out_shape,
                             scratch_shapes=scratch_shapes, compiler_params=_params(*sem))(*args)
        return list(res), []
    n_in, n_out, n_scr = len(in_specs), len(out_specs), len(scratch_shapes)
    p_in, p_out = len(plan.ins), len(plan.out_shapes)

    def hosted(*refs):
        refs = list(refs)
        ins, pins = refs[:n_in], refs[n_in:n_in + p_in]
        outs = refs[n_in + p_in:n_in + p_in + n_out]
        pouts = refs[n_in + p_in + n_out:n_in + p_in + n_out + p_out]
        scr = refs[n_in + p_in + n_out + p_out:n_in + p_in + n_out + p_out + n_scr]
        send_sems, recv_sems = refs[-2:]
        place = _place()
        ids = [pl.program_id(d) for d in range(len(grid))]
        first = functools.reduce(jnp.logical_and, [i == 0 for i in ids])
        last = functools.reduce(jnp.logical_and, [i == g - 1 for i, g in zip(ids, grid)])

        @pl.when(first)
        def _():
            plan.start(place, pins, pouts, send_sems, recv_sems)

        body(*ins, *outs, *scr)

        @pl.when(last)
        def _():
            plan.finish(place, pins, pouts, send_sems, recv_sems)

    res = pl.pallas_call(
        hosted, name=name, grid=grid, in_specs=in_specs + [HBM] * p_in, out_specs=out_specs + [HBM] * p_out,
        out_shape=out_shape + plan.out_shapes,
        input_output_aliases={n_in + i: n_out + o for i, o in plan.aliases.items()},
        scratch_shapes=scratch_shapes + [pltpu.SemaphoreType.DMA((plan.n_sems,)), pltpu.SemaphoreType.DMA((plan.n_sems,))],
        compiler_params=_params(*(("arbitrary",) * len(grid))))(*args, *plan.ins)
    return list(res[:n_out]), list(res[n_out:])


def _pair_exchange(name, grads):
    n = len(grads)

    def body(*refs):
        ins, outs = refs[:n], refs[n:2 * n]
        send_sems, recv_sems = refs[2 * n:]
        x, y, c, _ = _place()
        cps = []
        for i in range(n):
            rows = ins[i].shape[1] // 2
            cp = pltpu.make_async_remote_copy(
                src_ref=ins[i].at[:, pl.ds((1 - c) * rows, rows), :], dst_ref=outs[i],
                send_sem=send_sems.at[i], recv_sem=recv_sems.at[i], device_id=(x, y, 1 - c), device_id_type=MESH)
            cp.start()
            cps.append(cp)
        for cp in cps:
            cp.wait()

    return pl.pallas_call(
        body, name=name, in_specs=[HBM] * n, out_specs=[HBM] * n,
        out_shape=[jax.ShapeDtypeStruct((N_CHIPS, g.shape[1] // 2, g.shape[2]), g.dtype) for g in grads],
        scratch_shapes=[pltpu.SemaphoreType.DMA((n,)), pltpu.SemaphoreType.DMA((n,))],
        )(*grads)


def _sibling_join(grads):
    n = len(grads)

    def body(*refs):
        outs = refs[n:2 * n]
        send_sems, recv_sems = refs[2 * n:]
        x, y, c, _ = _place()
        cps = []
        for i in range(n):
            cp = pltpu.make_async_remote_copy(
                src_ref=_half(outs[i], c), dst_ref=_half(outs[i], c), send_sem=send_sems.at[i], recv_sem=recv_sems.at[i],
                device_id=(x, y, 1 - c), device_id_type=MESH)
            cp.start()
            cps.append(cp)
        for i, cp in enumerate(cps):
            cp.wait_send()
            pltpu.make_async_remote_copy(
                src_ref=_half(outs[i], 1 - c), dst_ref=_half(outs[i], 1 - c), send_sem=send_sems.at[i],
                recv_sem=recv_sems.at[i], device_id=(x, y, 1 - c), device_id_type=MESH).wait_recv()

    return pl.pallas_call(
        body, name="sibling_join", in_specs=[HBM] * n, out_specs=[HBM] * n,
        out_shape=[jax.ShapeDtypeStruct(g.shape, g.dtype) for g in grads],
        input_output_aliases={i: i for i in range(n)},
        scratch_shapes=[pltpu.SemaphoreType.DMA((n,)), pltpu.SemaphoreType.DMA((n,))],
        )(*grads)


def _gather8(name, block, reduce):
    m, n = block.shape

    def body(x_ref, out_ref, *scratch):
        if reduce:
            all_ref, send_sems, recv_sems, local_sem = scratch
        else:
            all_ref = out_ref
            send_sems, recv_sems, local_sem = scratch
        x, y, c, chips = _place()
        me, sibling = (x, y, c), (x, y, 1 - c)

        def rows(px, py, pc):
            return all_ref.at[pl.ds((4 * px + 2 * py + pc) * m, m), :]

        def copy(k, blk, to, src=None):
            return pltpu.make_async_remote_copy(
                src_ref=rows(*blk) if src is None else src, dst_ref=rows(*blk),
                send_sem=send_sems.at[k], recv_sem=recv_sems.at[k], device_id=to, device_id_type=MESH)

        mine = pltpu.make_async_copy(x_ref, rows(*me), local_sem)
        mine.start()
        first = [copy(0, me, sibling, src=x_ref)]
        first += [copy(1 + j, me, (chip[0], chip[1], c), src=x_ref) for j, chip in enumerate(chips)]
        for cp in first:
            cp.start()
        passed = [copy(4 + j, (chip[0], chip[1], c), sibling) for j, chip in enumerate(chips)]
        for j, chip in enumerate(chips):
            copy(1 + j, (chip[0], chip[1], c), me).wait_recv()
            passed[j].start()
        copy(0, sibling, me).wait_recv()
        for j, chip in enumerate(chips):
            copy(4 + j, (chip[0], chip[1], 1 - c), me).wait_recv()
        for cp in first + passed:
            cp.wait_send()
        mine.wait()
        if reduce:
            acc = all_ref[pl.ds(0, m), :]
            for d in range(1, 8):
                acc = acc + all_ref[pl.ds(d * m, m), :]
            out_ref[...] = acc

    sems = [pltpu.SemaphoreType.DMA((7,)), pltpu.SemaphoreType.DMA((7,)), pltpu.SemaphoreType.DMA]
    scratch = ([pltpu.VMEM((8 * m, n), F32)] if reduce else []) + sems
    return pl.pallas_call(
        body, name=name,
        out_shape=jax.ShapeDtypeStruct((m, n) if reduce else (8 * m, n), F32),
        in_specs=[pl.BlockSpec(memory_space=pltpu.VMEM)], out_specs=pl.BlockSpec(memory_space=pltpu.VMEM),
        scratch_shapes=scratch)(block)


def _pad_rows(a, rows):
    return jnp.concatenate([a, jnp.zeros((rows - a.shape[0], a.shape[1]), a.dtype)], axis=0)


def kernel(x, p, norm_g, w_in_a, conv_w, conv_b, ln_g, ln_b, w_out_a, kv_norm_g, w_kv, k_norm_g, w_in_b, q_norm_g, w_out_b, ple_norm_g, w_ple_gate, w_ple_proj, loss_target, m_norm_g, m_w_in_a, m_conv_w, m_conv_b, m_ln_g, m_ln_b, m_w_out_a, m_kv_norm_g, m_w_kv, m_k_norm_g, m_w_in_b, m_q_norm_g, m_w_out_b, m_ple_norm_g, m_w_ple_gate, m_w_ple_proj, v_norm_g, v_w_in_a, v_conv_w, v_conv_b, v_ln_g, v_ln_b, v_w_out_a, v_kv_norm_g, v_w_kv, v_k_norm_g, v_w_in_b, v_q_norm_g, v_w_out_b, v_ple_norm_g, v_w_ple_gate, v_w_ple_proj):
    nb, seq, dm = x.shape
    t = nb * seq
    ple = p.shape[-1]
    ccs = conv_w.shape[-1]
    cc = N_CHIPS * ccs
    da = dm
    nheads = da // HEAD_DIM
    assert seq == DILATIONS[-1] * SPAN and da % 128 == 0 and ccs % 128 == 0

    core = lax.axis_index("c").astype(jnp.int32).reshape(1)
    chip = (2 * lax.axis_index("x") + lax.axis_index("y")).astype(jnp.int32)
    chip1 = chip.reshape(1)
    sum_order = jnp.concatenate([(chip1 + k) % N_CHIPS for k in range(N_CHIPS)] + [core])

    x2 = x.reshape(t, dm)
    tgt2 = loss_target.reshape(t, dm)
    p0 = p[0].reshape(t, ple)
    p1 = p[1].reshape(t, ple)

    big = [
        ("w_in_a", w_in_a[0], "col"), ("w_out_a", w_out_a[0], "row"), ("w_kv", w_kv, "col"),
        ("w_in_b", w_in_b[0], "col"), ("w_out_b", w_out_b[0], "row"),
        ("w_ple_gate0", w_ple_gate[0], "row"), ("w_ple_gate1", w_ple_gate[1], "row"),
        ("w_ple_proj0", w_ple_proj[0], "col"), ("w_ple_proj1", w_ple_proj[1], "col"),
    ]
    shard_shape = {nm: w.shape for nm, w, _ in big}
    names = [nm for nm, _, _ in big]
    own = [_cast_bf16("cast_" + nm, w, chip1) for nm, w, _ in big]
    W = {}

    vec_rows = 40
    small = _pad_rows(jnp.concatenate([conv_w[0], conv_b, ln_g, ln_b], axis=0), vec_rows)
    allv = _gather8("gather_conv_vectors", small, reduce=False).reshape(N_CHIPS, 2, vec_rows, ccs)[:, 0]
    allv = allv.transpose(1, 0, 2).reshape(vec_rows, cc)
    cw_full, cb_full, lg_full, lb_full = allv[:HALO], allv[31:32], allv[32:33], allv[33:34]
    cw_full = cw_full * (lax.broadcasted_iota(jnp.int32, (HALO, 1), 0) < CONV_WIDTH).astype(F32)
    tables = _rope_tables(seq)
    gain_q = jnp.tile(q_norm_g[0][:, None, :], (1, nheads, 1)).reshape(1, 3 * da)
    gain_k = jnp.tile(k_norm_g[None, :], (1, nheads))
    g0, g1 = norm_g[0:1], norm_g[1:2]
    pg0, pg1 = ple_norm_g[0:1], ple_norm_g[1:2]
    kvg = kv_norm_g[None, :]

    (u0,), (W[names[0]],) = _rms_fwd("rms_u0", x2, [g0], plan=_gather_plan(own[:1]))
    pa, gathered = _mm_nn("mm_in_a", u0, W["w_in_a"], "col", out_dtype=BF16, plan=_gather_plan(own[1:]))
    W.update(zip(names[1:], gathered))
    for nm in ("w_ple_proj0", "w_ple_proj1"):
        W[nm] = W[nm].transpose(1, 0, 2).reshape(1, W[nm].shape[1], -1)
    conv_out, m_a = _mixa_fwd("mixa_fwd", pa, cw_full, cb_full, lg_full, lb_full, seq)
    h0, r0 = _mm_nn("mm_out_a", m_a, W["w_out_a"], "row", resid=x2, norm_gain=pg0)
    gpre0 = _mm_nn("mm_gate0", r0, W["w_ple_gate0"], "row", out_dtype=BF16)
    pp0 = _mm_nn("mm_proj0", p0, W["w_ple_proj0"], "col", out_dtype=BF16)
    x1, kvn, u1 = _ple_fwd("ple_fwd0", h0, gpre0, pp0, [kvg, g1])
    kv, kn = _mm_nn("mm_kv", kvn, W["w_kv"], "col", heads=(gain_k, tables, da, seq))
    pb, qn = _mm_nn("mm_in_b", u1, W["w_in_b"], "col", out_dtype=BF16, heads=(gain_q, tables, 3 * da, seq))
    o, lse, m_b = _attn_fwd("attn_fwd", qn, kn, kv, pb, seq)
    h1, r1 = _mm_nn("mm_out_b", m_b, W["w_out_b"], "row", resid=x1, norm_gain=pg1)
    gpre1 = _mm_nn("mm_gate1", r1, W["w_ple_gate1"], "row", out_dtype=BF16)
    pp1 = _mm_nn("mm_proj1", p1, W["w_ple_proj1"], "col", out_dtype=BF16)
    dy, dgp1, dpp1, sq = _ple_loss("ple_loss", h1, gpre1, pp1, tgt2)
    loss = lax.psum(0.5 * sq[0, 0] / dm, ("x", "y", "c"))

    G = {}
    G["w_ple_gate1"] = _mm_tn("tn_gate1", r1, dgp1, "row", shard_shape["w_ple_gate1"])
    G["w_ple_proj1"] = _mm_tn("tn_proj1", p1, dpp1, "col", shard_shape["w_ple_proj1"], whole=True)
    dr1 = _mm_nt("nt_gate1", dgp1, W["w_ple_gate1"], "row", out_dtype=BF16)
    dh1, (dpg1,) = _rms_bwd("rms_bwd_r1", h1, dy, [(pg1, dr1)])
    G["w_out_b"] = _mm_tn("tn_out_b", m_b, dh1, "row", shard_shape["w_out_b"])
    dm_b = _mm_nt("nt_out_b", dh1, W["w_out_b"], "row", out_dtype=BF16)
    d_o, dgt, dsum = _gate_bwd("gate_bwd", dm_b, o, pb)
    dq0, dq1, dq2, dk, dv = _attn_bwd("attn_bwd", qn, kn, kv, d_o, lse, dsum, seq)
    dpb, dgq = _q_bwd("q_bwd", pb, gain_q, tables, [dq0, dq1, dq2], dgt, seq)
    dkv, dgk = _k_bwd("k_bwd", kv, gain_k, tables, dk, dv, seq)
    G["w_in_b"] = _mm_tn("tn_in_b", u1, dpb, "col", shard_shape["w_in_b"])
    du1 = _mm_nt("nt_in_b", dpb, W["w_in_b"], "col", out_dtype=BF16)
    G["w_kv"] = _mm_tn("tn_kv", kvn, dkv, "col", shard_shape["w_kv"])
    dkvn = _mm_nt("nt_kv", dkv, W["w_kv"], "col", out_dtype=BF16)
    dx1, (dg1, dkvg), dgp0, dpp0 = _rms_bwd("rms_bwd_x1", x1, dh1, [(g1, du1), (kvg, dkvn)], ple=(gpre0, pp0))
    G["w_ple_gate0"] = _mm_tn("tn_gate0", r0, dgp0, "row", shard_shape["w_ple_gate0"])
    G["w_ple_proj0"] = _mm_tn("tn_proj0", p0, dpp0, "col", shard_shape["w_ple_proj0"], whole=True)
    dr0 = _mm_nt("nt_gate0", dgp0, W["w_ple_gate0"], "row", out_dtype=BF16)
    dh0, (dpg0,) = _rms_bwd("rms_bwd_r0", h0, dx1, [(pg0, dr0)])
    G["w_out_a"] = _mm_tn("tn_out_a", m_a, dh0, "row", shard_shape["w_out_a"])
    layer1 = ["w_kv", "w_in_b", "w_out_b", "w_ple_gate1", "w_ple_proj1"]
    layer0 = ["w_in_a", "w_out_a", "w_ple_gate0", "w_ple_proj0"]

    def pair_sums(batch, recv):
        return [_pair_sum("pair_sum_" + nm, G[nm], rc, core) for nm, rc in zip(batch, recv)]

    dm_a, recv1 = _mm_nt("nt_out_a", dh0, W["w_out_a"], "row", out_dtype=BF16,
                         plan=_exchange_plan([G[nm] for nm in layer1]))
    sums1 = pair_sums(layer1, recv1)
    (dc, dz, dlg, dlb, dcb), parts1 = _mixa_bwd1("mixa_bwd1", conv_out, pa, dm_a, lg_full, lb_full,
                                                 plan=_scatter_plan(sums1))
    dpa, dcw = _mixa_bwd2("mixa_bwd2", dc, pa, dz, cw_full, seq)
    G["w_in_a"] = _mm_tn("tn_in_a", u0, dpa, "col", shard_shape["w_in_a"])
    sums0 = pair_sums(layer0, _pair_exchange("pair_exchange_layer0", [G[nm] for nm in layer0]))
    du0, parts0 = _mm_nt("nt_in_a", dpa, W["w_in_a"], "col", out_dtype=BF16, plan=_scatter_plan(sums0))
    dx, (dg0,) = _rms_bwd("rms_bwd_x", x2, dh0, [(g0, du0)])
    grad_x = dx.reshape(nb, seq, dm)

    sums = dict(zip(layer1 + layer0, sums1 + sums0))
    parts = dict(zip(layer1 + layer0, parts1 + parts0))
    halves = [_chip_sum("chip_sum_" + nm, sums[nm], parts[nm], sum_order) for nm in names]
    gfull = dict(zip(names, _sibling_join(halves)))

    def as_rows(a):
        return a.reshape(-1, dm)

    small_parts = [as_rows(dcw), as_rows(dcb), as_rows(dlg), as_rows(dlb), dg0, dg1, dkvg, dpg0, dpg1, as_rows(dgk), as_rows(dgq)]
    counts = [a.shape[0] for a in small_parts]
    total = sum(counts)
    packed = _pad_rows(jnp.concatenate(small_parts, axis=0), -(-total // 8) * 8)
    red = _gather8("reduce_small", packed, reduce=True)
    pieces, off = [], 0
    for n_ in counts:
        pieces.append(red[off:off + n_])
        off += n_
    r_dcw, r_dcb, r_dlg, r_dlb, r_g0, r_g1, r_kvg, r_pg0, r_pg1, r_gk, r_gq = pieces
    my_cols = lambda a: lax.dynamic_slice_in_dim(a.reshape(-1, cc), chip * ccs, ccs, axis=1)
    small_grads = {
        "norm_g": jnp.concatenate([r_g0, r_g1], axis=0),
        "conv_w": my_cols(r_dcw)[:CONV_WIDTH],
        "conv_b": my_cols(r_dcb), "ln_g": my_cols(r_dlg), "ln_b": my_cols(r_dlb),
        "kv_norm_g": r_kvg,
        "k_norm_g": r_gk.reshape(nheads, HEAD_DIM).sum(axis=0, keepdims=True),
        "q_norm_g": r_gq.reshape(3, nheads, HEAD_DIM).sum(axis=1),
        "ple_norm_g": jnp.concatenate([r_pg0, r_pg1], axis=0),
    }

    given = dict(norm_g=norm_g, w_in_a=w_in_a, conv_w=conv_w, conv_b=conv_b, ln_g=ln_g, ln_b=ln_b, w_out_a=w_out_a,
                 kv_norm_g=kv_norm_g, w_kv=w_kv, k_norm_g=k_norm_g, w_in_b=w_in_b, q_norm_g=q_norm_g, w_out_b=w_out_b,
                 ple_norm_g=ple_norm_g, w_ple_gate=w_ple_gate, w_ple_proj=w_ple_proj)
    mom1 = dict(norm_g=m_norm_g, w_in_a=m_w_in_a, conv_w=m_conv_w, conv_b=m_conv_b, ln_g=m_ln_g, ln_b=m_ln_b,
                w_out_a=m_w_out_a, kv_norm_g=m_kv_norm_g, w_kv=m_w_kv, k_norm_g=m_k_norm_g, w_in_b=m_w_in_b,
                q_norm_g=m_q_norm_g, w_out_b=m_w_out_b, ple_norm_g=m_ple_norm_g, w_ple_gate=m_w_ple_gate,
                w_ple_proj=m_w_ple_proj)
    mom2 = dict(norm_g=v_norm_g, w_in_a=v_w_in_a, conv_w=v_conv_w, conv_b=v_conv_b, ln_g=v_ln_g, ln_b=v_ln_b,
                w_out_a=v_w_out_a, kv_norm_g=v_kv_norm_g, w_kv=v_w_kv, k_norm_g=v_k_norm_g, w_in_b=v_w_in_b,
                q_norm_g=v_q_norm_g, w_out_b=v_w_out_b, ple_norm_g=v_ple_norm_g, w_ple_gate=v_w_ple_gate,
                w_ple_proj=v_w_ple_proj)
    order = ["norm_g", "w_in_a", "conv_w", "conv_b", "ln_g", "ln_b", "w_out_a", "kv_norm_g", "w_kv", "k_norm_g", "w_in_b",
             "q_norm_g", "w_out_b", "ple_norm_g", "w_ple_gate", "w_ple_proj"]
    grads, deltas, new_m, new_v = {}, {}, {}, {}
    for nm in order:
        shape = given[nm].shape
        if nm in ("w_ple_gate", "w_ple_proj"):
            g2 = jnp.concatenate([gfull[nm + "0"], gfull[nm + "1"]], axis=0)
        elif nm in gfull:
            g2 = gfull[nm]
        else:
            g2 = small_grads[nm]
        two_d = g2.shape
        d2, m2, v2 = _adamw("adamw_" + nm, given[nm].reshape(two_d), g2, mom1[nm].reshape(two_d), mom2[nm].reshape(two_d))
        grads[nm], deltas[nm], new_m[nm], new_v[nm] = (a.reshape(shape) for a in (g2, d2, m2, v2))

    return (loss, grad_x, *[grads[n_] for n_ in order], *[deltas[n_] for n_ in order],
            *[new_m[n_] for n_ in order], *[new_v[n_] for n_ in order])
```

```python
import functools

import jax
import jax.numpy as jnp
from jax import lax
from jax.experimental import pallas as pl
from jax.experimental.pallas import tpu as pltpu

F32 = jnp.float32
BF16 = jnp.bfloat16
MESH = pl.DeviceIdType.MESH

EPS = 1e-6
NEG_INF = -1e30
HEAD_DIM = 64
ROPE_DIM = 16
ROPE_THETA = 500000.0
CONV_WIDTH = 31
SUBLANES = 8
CONV_ROWS = 64
HALO = 32
SPAN = 128
DILATIONS = (1, 4, 16)
ADAM_LR, ADAM_B1, ADAM_B2, ADAM_EPS, ADAM_WD, ADAM_STEP = 0.001, 0.9, 0.999, 1e-08, 0.01, 10
N_CHIPS = 4
VMEM_LIMIT = 56 * 1024 * 1024


def _tile(n, target, mult=128):
    best = None
    t = mult
    while t <= min(n, target):
        if n % t == 0:
            best = t
        t += mult
    return best if best is not None else n


def _params(*sem):
    return pltpu.CompilerParams(dimension_semantics=tuple(sem) if sem else None, vmem_limit_bytes=VMEM_LIMIT)


def _sigmoid(x):
    return 0.5 * jnp.tanh(0.5 * x) + 0.5


def _seg_allsum64(x, terms=1, mean=False):
    tr, w = x.shape
    cw = 256 if w % 256 == 0 else 128
    n = w // cw
    shift = HEAD_DIM.bit_length() - 1
    ri = lax.shift_right_logical(lax.broadcasted_iota(jnp.int32, (cw, cw), 0), shift)
    ci = lax.shift_right_logical(lax.broadcasted_iota(jnp.int32, (cw, cw), 1), shift)
    ones = jnp.where(ri == ci, 1.0 / HEAD_DIM if mean else 1.0, 0.0).astype(BF16)

    def stack(v):
        return jnp.concatenate([v[:, j * cw:(j + 1) * cw] for j in range(n)], axis=0)

    hi = x.astype(BF16)
    s = jnp.dot(stack(hi), ones, preferred_element_type=F32)
    if terms == 2:
        lo = (x - hi.astype(F32)).astype(BF16)
        s = s + jnp.dot(stack(lo), ones, preferred_element_type=F32)
    return jnp.concatenate([s[j * tr:(j + 1) * tr] for j in range(n)], axis=1)


def _colsum(x):
    return jnp.sum(x, axis=0, keepdims=True)


def _shards_view(w, kind):
    return w if kind == "col" else w.reshape(1, -1, w.shape[2])


def _mm_nn(name, a, w, kind, *, out_dtype=F32, resid=None, norm_gain=None, heads=None, plan=None):
    t = a.shape[0]
    w = _shards_view(w, kind)
    ns, k, c = w.shape
    n = ns * c
    plain = norm_gain is None and heads is None and resid is None
    tm = _tile(t, 2048 if plain and out_dtype == BF16 else (1024 if norm_gain is None else 512), 8)
    tk = _tile(k, 2048)
    tn = _tile(c, 1024)
    nk = k // tk
    per = c // tn
    assert norm_gain is None or tn == n, "the fused RMSNorm needs whole rows in one tile"
    assert norm_gain is None or heads is None
    n_in = 2 + (resid is not None) + (norm_gain is not None) + (4 if heads is not None else 0)
    if heads is not None:
        h_gain, h_tables, h_width, h_seq = heads
        assert h_width % tn == 0 and h_seq % tm == 0
        h_blocks = h_width // tn

    def body(*refs):
        a_ref, w_ref = refs[:2]
        r_ref = refs[2] if resid is not None else None
        g_ref = refs[n_in - 1] if norm_gain is not None else None
        o_ref = refs[n_in]
        part = jnp.dot(a_ref[...].astype(BF16), w_ref[...], preferred_element_type=F32)

        def finish(out):
            if resid is not None:
                out = out + r_ref[...]
            stored = out.astype(out_dtype)
            o_ref[...] = stored
            if norm_gain is not None:
                normed = out * lax.rsqrt(jnp.mean(out * out, axis=-1, keepdims=True) + EPS) * g_ref[...]
                refs[n_in + 1][...] = normed.astype(BF16)
            if heads is not None:
                hg_ref, tc_ref, ta_ref, tb_ref = refs[n_in - 4:n_in]

                @pl.when(pl.program_id(1) < h_blocks)
                def _():
                    refs[n_in + 1][...] = _hnr_fwd_math(stored.astype(F32), hg_ref[...], tc_ref, ta_ref, tb_ref)

        if nk == 1:
            finish(part)
            return
        acc = refs[-1]
        kk = pl.program_id(2)

        @pl.when(kk == 0)
        def _():
            acc[...] = part

        @pl.when(kk > 0)
        def _():
            acc[...] += part

        @pl.when(kk == nk - 1)
        def _():
            finish(acc[...])

    in_specs = [pl.BlockSpec((tm, tk), lambda i, j, kk: (i, kk)),
                pl.BlockSpec((None, tk, tn), lambda i, j, kk: (j // per, kk, j % per))]
    args = [a, w]
    if resid is not None:
        in_specs.append(pl.BlockSpec((tm, tn), lambda i, j, kk: (i, j)))
        args.append(resid)
    out_specs = [pl.BlockSpec((tm, tn), lambda i, j, kk: (i, j))]
    out_shape = [jax.ShapeDtypeStruct((t, n), out_dtype)]
    if norm_gain is not None:
        in_specs.append(pl.BlockSpec((1, n), lambda i, j, kk: (0, 0)))
        args.append(norm_gain)
        out_specs.append(pl.BlockSpec((tm, tn), lambda i, j, kk: (i, j)))
        out_shape.append(jax.ShapeDtypeStruct((t, n), BF16))
    if heads is not None:
        last = h_blocks - 1
        per_seq = h_seq // tm
        tab = pl.BlockSpec((tm, 128), lambda i, j, kk: (i % per_seq, 0))
        in_specs += [pl.BlockSpec((1, tn), lambda i, j, kk: (0, jnp.minimum(j, last))), tab, tab, tab]
        args += [h_gain, *h_tables]
        out_specs.append(pl.BlockSpec((tm, tn), lambda i, j, kk: (i, jnp.minimum(j, last))))
        out_shape.append(jax.ShapeDtypeStruct((t, h_width), F32))
    outs, carried = _hosted_call(
        body, plan, name=name, grid=(t // tm, n // tn, nk), in_specs=in_specs, out_specs=out_specs, out_shape=out_shape,
        scratch_shapes=[pltpu.VMEM((tm, tn), F32)] if nk > 1 else [],
        args=args, sem=("parallel", "arbitrary", "arbitrary"))
    out = outs[0] if len(outs) == 1 else tuple(outs)
    return out if plan is None else (out, carried)


def _mm_nt(name, d, w, kind, *, out_dtype=F32, plan=None):
    t = d.shape[0]
    w = _shards_view(w, kind)
    ns, k, c = w.shape
    n = ns * c
    tm = _tile(t, 1024, 8)
    to = _tile(k, 1024)
    tc = _tile(c, 1536)
    nc = n // tc
    per = c // tc

    def body(d_ref, w_ref, o_ref, *scratch):
        part = lax.dot_general(d_ref[...].astype(BF16), w_ref[...], (((1,), (1,)), ((), ())),
                               preferred_element_type=F32)
        if nc == 1:
            o_ref[...] = part.astype(out_dtype)
            return
        acc = scratch[0]
        kk = pl.program_id(2)

        @pl.when(kk == 0)
        def _():
            acc[...] = part

        @pl.when(kk > 0)
        def _():
            acc[...] += part

        @pl.when(kk == nc - 1)
        def _():
            o_ref[...] = acc[...].astype(out_dtype)

    (out,), carried = _hosted_call(
        body, plan, name=name, grid=(t // tm, k // to, nc),
        in_specs=[pl.BlockSpec((tm, tc), lambda i, j, kk: (i, kk)),
                  pl.BlockSpec((None, to, tc), lambda i, j, kk: (kk // per, j, kk % per))],
        out_specs=[pl.BlockSpec((tm, to), lambda i, j, kk: (i, j))],
        out_shape=[jax.ShapeDtypeStruct((t, k), out_dtype)],
        scratch_shapes=[pltpu.VMEM((tm, to), F32)] if nc > 1 else [],
        args=[d, w], sem=("parallel", "parallel", "arbitrary"))
    return out if plan is None else (out, carried)


def _mm_tn(name, a, d, kind, shard_shape, whole=False):
    t, k = a.shape
    n = d.shape[1]
    ns = N_CHIPS if kind == "col" and not whole else 1
    c = n // ns
    tkm = _tile(k, 1024)
    tn = _tile(c, 1536)
    tt = _tile(t, 1024, 8)
    nt = t // tt
    per = c // tn

    def body(a_ref, d_ref, o_ref, acc):
        kk = pl.program_id(2)
        part = lax.dot_general(a_ref[...].astype(BF16), d_ref[...].astype(BF16), (((0,), (0,)), ((), ())),
                               preferred_element_type=F32)

        @pl.when(kk == 0)
        def _():
            acc[...] = part

        @pl.when(kk > 0)
        def _():
            acc[...] += part

        @pl.when(kk == nt - 1)
        def _():
            o_ref[...] = acc[...].astype(BF16)

    out = pl.pallas_call(
        body, name=name, grid=(k // tkm, n // tn, nt),
        in_specs=[pl.BlockSpec((tt, tkm), lambda i, j, kk: (kk, i)),
                  pl.BlockSpec((tt, tn), lambda i, j, kk: (kk, j))],
        out_specs=pl.BlockSpec((None, tkm, tn), lambda i, j, kk: (j // per, i, j % per)),
        out_shape=jax.ShapeDtypeStruct((ns, k, c), BF16),
        scratch_shapes=[pltpu.VMEM((tkm, tn), F32)],
        compiler_params=_params("parallel", "parallel", "arbitrary"))(a, d)
    if kind == "col" and whole:
        return out.reshape(k, N_CHIPS, n // N_CHIPS).transpose(1, 0, 2)
    return out.reshape((N_CHIPS,) + tuple(shard_shape))


def _row_spec(tr, w, col=0):
    return pl.BlockSpec((tr, w), lambda i: (i, col))


def _full_spec(shape):
    return pl.BlockSpec(shape, lambda i: tuple(0 for _ in shape))


def _rms_fwd(name, x, gains, plan=None):
    t, dm = x.shape
    tr = _tile(t, 256, 8)
    n = len(gains)

    def body(x_ref, *refs):
        xv = x_ref[...]
        xn = xv * lax.rsqrt(jnp.mean(xv * xv, axis=-1, keepdims=True) + EPS)
        for g_ref, o_ref in zip(refs[:n], refs[n:]):
            o_ref[...] = (xn * g_ref[...]).astype(BF16)

    outs, carried = _hosted_call(
        body, plan, name=name, grid=(t // tr,),
        in_specs=[_row_spec(tr, dm)] + [_full_spec((1, dm))] * n,
        out_specs=[_row_spec(tr, dm)] * n,
        out_shape=[jax.ShapeDtypeStruct((t, dm), BF16)] * n,
        scratch_shapes=[], args=[x, *gains], sem=("parallel",))
    return outs if plan is None else (outs, carried)


def _ple_bwd_math(dy, gpre, pp):
    sg = _sigmoid(gpre)
    return (dy * pp * sg * (1.0 - sg)).astype(BF16), (dy * sg).astype(BF16)


def _rms_bwd(name, x, resid, pairs, ple=None):
    t, dm = x.shape
    tr = _tile(t, 256, 8)
    n = len(pairs)
    n_in = 2 * n + (2 if ple is not None else 0)

    def body(x_ref, r_ref, *refs):
        ins, outs = refs[:n_in], refs[n_in:]
        i = pl.program_id(0)
        xv = x_ref[...]
        rs = lax.rsqrt(jnp.mean(xv * xv, axis=-1, keepdims=True) + EPS)
        xn = xv * rs
        total = r_ref[...]
        for kx in range(n):
            g_ref, du_ref = ins[2 * kx], ins[2 * kx + 1]
            dg_ref = outs[1 + kx]
            du = du_ref[...].astype(F32)

            @pl.when(i == 0)
            def _():
                dg_ref[...] = jnp.zeros_like(dg_ref)

            dg_ref[...] += _colsum(du * xn)
            dxh = du * g_ref[...]
            total = total + rs * (dxh - xn * jnp.mean(dxh * xn, axis=-1, keepdims=True))
        outs[0][...] = total
        if ple is not None:
            outs[1 + n][...], outs[2 + n][...] = _ple_bwd_math(total, ins[2 * n][...].astype(F32), ins[2 * n + 1][...].astype(F32))

    in_specs = [_row_spec(tr, dm), _row_spec(tr, dm)]
    args = [x, resid]
    for g, du in pairs:
        in_specs += [_full_spec((1, dm)), _row_spec(tr, dm)]
        args += [g, du]
    out_specs = [_row_spec(tr, dm)] + [_full_spec((1, dm))] * n
    out_shape = [jax.ShapeDtypeStruct((t, dm), F32)] + [jax.ShapeDtypeStruct((1, dm), F32)] * n
    if ple is not None:
        in_specs += [_row_spec(tr, dm)] * 2
        args += list(ple)
        out_specs += [_row_spec(tr, dm)] * 2
        out_shape += [jax.ShapeDtypeStruct((t, dm), BF16)] * 2
    outs = pl.pallas_call(
        body, name=name, grid=(t // tr,), in_specs=in_specs, out_specs=out_specs, out_shape=out_shape,
        compiler_params=_params("arbitrary"))(*args)
    if ple is not None:
        return outs[0], list(outs[1:1 + n]), outs[1 + n], outs[2 + n]
    return outs[0], list(outs[1:])


def _shifted_copies(ext, sh, rows):
    for s in range(1, SUBLANES):
        sh[s - 1] = ext[pl.ds(s, rows), :]


def _window(ext, sh, off, row0, rows, lanes):
    s = off % SUBLANES
    src = ext if s == 0 else sh.at[s - 1]
    return src[pl.ds(off - s + row0, rows), lanes]


def _mixa_fwd(name, pa, cw, cb, lg, lb, seq):
    t, w3 = pa.shape
    cc = w3 // 3
    tr = _tile(seq, 128, HALO)
    per_seq = seq // tr
    hb = tr // HALO
    lead = HALO - (CONV_WIDTH - 1)

    def body(a_ref, b_ref, z_ref, ah_ref, bh_ref, cw_ref, cb_ref, lg_ref, lb_ref, c_ref, m_ref, ext, sh, conv):
        i = pl.program_id(0)
        gh = ah_ref[...].astype(F32) * _sigmoid(bh_ref[...].astype(F32))
        ext[pl.ds(0, HALO), :] = jnp.where((i % per_seq) == 0, 0.0, gh)
        ext[pl.ds(HALO, tr), :] = a_ref[...].astype(F32) * _sigmoid(b_ref[...].astype(F32))
        _shifted_copies(ext, sh, tr + HALO - SUBLANES)
        for lc in range(cc // 128):
            lanes = pl.ds(lc * 128, 128)
            taps = [cw_ref[pl.ds(k, 1), lanes] for k in range(CONV_WIDTH)]
            for row0 in range(0, tr, CONV_ROWS):
                acc = jnp.broadcast_to(cb_ref[:, lanes], (CONV_ROWS, 128))
                for k in range(CONV_WIDTH):
                    acc = acc + _window(ext, sh, lead + k, row0, CONV_ROWS, lanes) * taps[k]
                conv[pl.ds(row0, CONV_ROWS), lanes] = acc
        acc = conv[...]
        c_ref[...] = acc.astype(BF16)
        xc = acc - jnp.mean(acc, axis=-1, keepdims=True)
        nrm = xc * lax.rsqrt(jnp.mean(xc * xc, axis=-1, keepdims=True) + EPS)
        l = nrm * lg_ref[...] + lb_ref[...]
        z = z_ref[...].astype(F32)
        m_ref[...] = (l * _sigmoid(l) * z * _sigmoid(z)).astype(BF16)

    halo = lambda col: pl.BlockSpec((HALO, cc), lambda i: (jnp.maximum(i * hb - 1, 0), col))
    return pl.pallas_call(
        body, name=name, grid=(t // tr,),
        in_specs=[_row_spec(tr, cc, 0), _row_spec(tr, cc, 1), _row_spec(tr, cc, 2), halo(0), halo(1),
                  _full_spec((HALO, cc)), _full_spec((1, cc)), _full_spec((1, cc)), _full_spec((1, cc))],
        out_specs=[_row_spec(tr, cc), _row_spec(tr, cc)],
        out_shape=[jax.ShapeDtypeStruct((t, cc), BF16), jax.ShapeDtypeStruct((t, cc), BF16)],
        scratch_shapes=[pltpu.VMEM((tr + HALO, cc), F32), pltpu.VMEM((SUBLANES - 1, tr + HALO - SUBLANES, cc), F32),
                        pltpu.VMEM((tr, cc), F32)],
        compiler_params=_params("parallel"))(pa, pa, pa, pa, pa, cw, cb, lg, lb)


def _mixa_bwd1(name, c, pa, dm, lg, lb, plan=None):
    t, cc = c.shape
    tr = _tile(t, 128, 8)

    def body(c_ref, z_ref, dm_ref, lg_ref, lb_ref, dc_ref, dz_ref, dlg_ref, dlb_ref, dcb_ref):
        i = pl.program_id(0)
        cv = c_ref[...].astype(F32)
        xc = cv - jnp.mean(cv, axis=-1, keepdims=True)
        rs = lax.rsqrt(jnp.mean(xc * xc, axis=-1, keepdims=True) + EPS)
        nrm = xc * rs
        l = nrm * lg_ref[...] + lb_ref[...]
        z = z_ref[...].astype(F32)
        sl, sz = _sigmoid(l), _sigmoid(z)
        dmv = dm_ref[...].astype(F32)
        ds = dmv * (z * sz)
        dzz = dmv * (l * sl)
        dz_ref[...] = (dzz * (sz * (1.0 + z * (1.0 - sz)))).astype(BF16)
        dl = ds * (sl * (1.0 + l * (1.0 - sl)))
        dn = dl * lg_ref[...]
        dc = rs * (dn - jnp.mean(dn, axis=-1, keepdims=True) - nrm * jnp.mean(dn * nrm, axis=-1, keepdims=True))
        dc_ref[...] = dc.astype(BF16)

        @pl.when(i == 0)
        def _():
            dlg_ref[...] = jnp.zeros_like(dlg_ref)
            dlb_ref[...] = jnp.zeros_like(dlb_ref)
            dcb_ref[...] = jnp.zeros_like(dcb_ref)

        dlg_ref[...] += _colsum(dl * nrm)
        dlb_ref[...] += _colsum(dl)
        dcb_ref[...] += _colsum(dc)

    vec = jax.ShapeDtypeStruct((1, cc), F32)
    outs, carried = _hosted_call(
        body, plan, name=name, grid=(t // tr,),
        in_specs=[_row_spec(tr, cc), _row_spec(tr, cc, 2), _row_spec(tr, cc), _full_spec((1, cc)), _full_spec((1, cc))],
        out_specs=[_row_spec(tr, cc), _row_spec(tr, cc)] + [_full_spec((1, cc))] * 3,
        out_shape=[jax.ShapeDtypeStruct((t, cc), BF16), jax.ShapeDtypeStruct((t, cc), BF16), vec, vec, vec],
        scratch_shapes=[], args=[c, pa, dm, lg, lb], sem=("arbitrary",))
    return outs if plan is None else (outs, carried)


def _mixa_bwd2(name, dc, pa, dz, cw, seq):
    t, cc = dc.shape
    tr = _tile(seq, 128, HALO)
    per_seq = seq // tr
    hb = tr // HALO
    steps = t // tr
    last_halo = t // HALO - 1

    def body(dc_ref, dcn_ref, a_ref, b_ref, dz_ref, cw_ref, dp_ref, dcw_ref, ext, sh, sums):
        i = pl.program_id(0)
        ext[pl.ds(0, tr), :] = dc_ref[...].astype(F32)
        ext[pl.ds(tr, HALO), :] = jnp.where((i % per_seq) == per_seq - 1, 0.0, dcn_ref[...].astype(F32))
        _shifted_copies(ext, sh, tr + HALO - SUBLANES)

        @pl.when(i == 0)
        def _():
            sums[...] = jnp.zeros_like(sums)
            dcw_ref[...] = jnp.zeros_like(dcw_ref)

        av = a_ref[...].astype(F32)
        sb = _sigmoid(b_ref[...].astype(F32))
        glu = av * sb
        dglu = jnp.zeros((tr, cc), F32)
        for k in range(CONV_WIDTH):
            wd = _window(ext, sh, CONV_WIDTH - 1 - k, 0, tr, slice(None))
            dglu = dglu + wd * cw_ref[pl.ds(k, 1), :]
            sums[pl.ds(k * SUBLANES, SUBLANES), :] += (wd * glu).reshape(tr // SUBLANES, SUBLANES, cc).sum(axis=0)
        dp_ref[:, pl.ds(0, cc)] = (dglu * sb).astype(BF16)
        dp_ref[:, pl.ds(cc, cc)] = (dglu * av * sb * (1.0 - sb)).astype(BF16)
        dp_ref[:, pl.ds(2 * cc, cc)] = dz_ref[...]

        @pl.when(i == steps - 1)
        def _():
            for k in range(CONV_WIDTH):
                dcw_ref[pl.ds(k, 1), :] = _colsum(sums[pl.ds(k * SUBLANES, SUBLANES), :])

    nxt = pl.BlockSpec((HALO, cc), lambda i: (jnp.minimum((i + 1) * hb, last_halo), 0))
    return pl.pallas_call(
        body, name=name, grid=(steps,),
        in_specs=[_row_spec(tr, cc), nxt, _row_spec(tr, cc, 0), _row_spec(tr, cc, 1), _row_spec(tr, cc),
                  _full_spec((HALO, cc))],
        out_specs=[_row_spec(tr, 3 * cc), _full_spec((HALO, cc))],
        out_shape=[jax.ShapeDtypeStruct((t, 3 * cc), BF16), jax.ShapeDtypeStruct((HALO, cc), F32)],
        scratch_shapes=[pltpu.VMEM((tr + HALO, cc), F32), pltpu.VMEM((SUBLANES - 1, tr + HALO - SUBLANES, cc), F32),
                        pltpu.VMEM((HALO * SUBLANES, cc), F32)],
        compiler_params=_params("arbitrary"))(dc, dc, pa, pa, dz, cw)


def _ple_fwd(name, h, gpre, pp, gains):
    t, dm = h.shape
    tr = _tile(t, 256, 8)

    n = len(gains)

    def body(h_ref, g_ref, p_ref, *refs):
        o_ref = refs[n]
        xv = h_ref[...] + _sigmoid(g_ref[...].astype(F32)) * p_ref[...].astype(F32)
        o_ref[...] = xv
        xn = xv * lax.rsqrt(jnp.mean(xv * xv, axis=-1, keepdims=True) + EPS)
        for gain_ref, n_ref in zip(refs[:n], refs[n + 1:]):
            n_ref[...] = (xn * gain_ref[...]).astype(BF16)

    return pl.pallas_call(
        body, name=name, grid=(t // tr,), in_specs=[_row_spec(tr, dm)] * 3 + [_full_spec((1, dm))] * n,
        out_specs=[_row_spec(tr, dm)] * (1 + n),
        out_shape=[jax.ShapeDtypeStruct((t, dm), F32)] + [jax.ShapeDtypeStruct((t, dm), BF16)] * n,
        compiler_params=_params("parallel"))(h, gpre, pp, *gains)


def _ple_loss(name, h, gpre, pp, target):
    t, dm = h.shape
    tr = _tile(t, 256, 8)

    def body(h_ref, g_ref, p_ref, t_ref, dy_ref, dg_ref, dp_ref, sq_ref):
        i = pl.program_id(0)
        gpre_v, pp_v = g_ref[...].astype(F32), p_ref[...].astype(F32)
        err = h_ref[...] + _sigmoid(gpre_v) * pp_v - t_ref[...]
        dy = err * (1.0 / dm)
        dy_ref[...] = dy
        dg_ref[...], dp_ref[...] = _ple_bwd_math(dy, gpre_v, pp_v)

        @pl.when(i == 0)
        def _():
            sq_ref[...] = jnp.zeros_like(sq_ref)

        sq_ref[...] += jnp.sum(jnp.sum(err * err, axis=1, keepdims=True), axis=0, keepdims=True)

    return pl.pallas_call(
        body, name=name, grid=(t // tr,), in_specs=[_row_spec(tr, dm)] * 4,
        out_specs=[_row_spec(tr, dm)] * 3 + [_full_spec((1, 1))],
        out_shape=[jax.ShapeDtypeStruct((t, dm), F32)] + [jax.ShapeDtypeStruct((t, dm), BF16)] * 2
        + [jax.ShapeDtypeStruct((1, 1), F32)],
        compiler_params=_params("arbitrary"))(h, gpre, pp, target)


def _rope_tables(seq):
    half = ROPE_DIM // 2
    inv = ROPE_THETA ** (-jnp.arange(half, dtype=F32) * (2.0 / ROPE_DIM))
    ang = jnp.arange(seq).astype(F32)[:, None] * inv[None, :]
    cos, sin = jnp.cos(ang), jnp.sin(ang)
    rest = HEAD_DIM - ROPE_DIM
    one = jnp.ones((seq, rest), F32)
    zero = jnp.zeros((seq, rest), F32)
    zh = jnp.zeros((seq, half), F32)
    tc = jnp.concatenate([cos, cos, one], axis=1)
    ta = jnp.concatenate([-sin, zh, zero], axis=1)
    tb = jnp.concatenate([zh, sin, zero], axis=1)
    return [jnp.tile(tb_, (1, 128 // HEAD_DIM)) for tb_ in (tc, ta, tb)]


def _wide(tab_ref, w):
    return jnp.tile(tab_ref[...], (1, w // 128))


def _hnr_fwd_math(xv, gain, tc_ref, ta_ref, tb_ref):
    width = xv.shape[1]
    rs = lax.rsqrt(_seg_allsum64(xv * xv, mean=True) + EPS)
    y = xv * rs * gain
    return (y * _wide(tc_ref, width) + pltpu.roll(y, width - ROPE_DIM // 2, 1) * _wide(ta_ref, width)
            + pltpu.roll(y, ROPE_DIM // 2, 1) * _wide(tb_ref, width))


def _hnr_bwd_math(xv, gain, dout, tc, ta, tb, width):
    dy = dout * tc + pltpu.roll(dout * ta, ROPE_DIM // 2, 1) + pltpu.roll(dout * tb, width - ROPE_DIM // 2, 1)
    rs = lax.rsqrt(_seg_allsum64(xv * xv, mean=True) + EPS)
    xn = xv * rs
    dyh = dy * gain
    dx = rs * (dyh - xn * _seg_allsum64(dyh * xn, terms=2, mean=True))
    return dx, _colsum(dy * xn)


def _q_bwd(name, p1, gain, tables, dqs, dgt, seq):
    t, w4 = p1.shape
    da = w4 // 4
    width = 3 * da
    tr = _tile(seq, 128, 8)
    per_seq = seq // tr

    def body(x_ref, g_ref, tc_ref, ta_ref, tb_ref, d0_ref, d1_ref, d2_ref, dgt_ref, o_ref, dg_ref):
        i = pl.program_id(0)
        dout = jnp.concatenate([d0_ref[...], d1_ref[...], d2_ref[...]], axis=1)
        dx, dg = _hnr_bwd_math(x_ref[...].astype(F32), g_ref[...], dout, _wide(tc_ref, width), _wide(ta_ref, width),
                               _wide(tb_ref, width), width)

        @pl.when(i == 0)
        def _():
            dg_ref[...] = jnp.zeros_like(dg_ref)

        dg_ref[...] += dg
        o_ref[:, pl.ds(0, width)] = dx.astype(BF16)
        o_ref[:, pl.ds(width, da)] = dgt_ref[...]

    tab = pl.BlockSpec((tr, 128), lambda i: (i % per_seq, 0))
    return pl.pallas_call(
        body, name=name, grid=(t // tr,),
        in_specs=[_row_spec(tr, width), _full_spec((1, width)), tab, tab, tab] + [_row_spec(tr, da)] * 4,
        out_specs=[_row_spec(tr, w4), _full_spec((1, width))],
        out_shape=[jax.ShapeDtypeStruct((t, w4), BF16), jax.ShapeDtypeStruct((1, width), F32)],
        compiler_params=_params("arbitrary"))(p1, gain, *tables, *dqs, dgt)


def _k_bwd(name, kv, gain, tables, dk, dv, seq):
    t, w2 = kv.shape
    da = w2 // 2
    tr = _tile(seq, 256, 8)
    per_seq = seq // tr

    def body(x_ref, g_ref, tc_ref, ta_ref, tb_ref, dk_ref, dv_ref, o_ref, dg_ref):
        i = pl.program_id(0)
        dx, dg = _hnr_bwd_math(x_ref[...], g_ref[...], dk_ref[...], _wide(tc_ref, da), _wide(ta_ref, da),
                               _wide(tb_ref, da), da)

        @pl.when(i == 0)
        def _():
            dg_ref[...] = jnp.zeros_like(dg_ref)

        dg_ref[...] += dg
        o_ref[:, pl.ds(0, da)] = dx.astype(BF16)
        o_ref[:, pl.ds(da, da)] = dv_ref[...].astype(BF16)

    tab = pl.BlockSpec((tr, 128), lambda i: (i % per_seq, 0))
    return pl.pallas_call(
        body, name=name, grid=(t // tr,),
        in_specs=[_row_spec(tr, da), _full_spec((1, da)), tab, tab, tab] + [_row_spec(tr, da)] * 2,
        out_specs=[_row_spec(tr, w2), _full_spec((1, da))],
        out_shape=[jax.ShapeDtypeStruct((t, w2), BF16), jax.ShapeDtypeStruct((1, da), F32)],
        compiler_params=_params("arbitrary"))(kv, gain, *tables, dk, dv)


def _unit_index(dil, r, blk):
    if dil == 1:
        start = blk * SPAN
        return pl.ds(start if isinstance(start, int) else pl.multiple_of(start, SPAN), SPAN)
    return pl.ds(r + dil * SPAN * blk, SPAN, stride=dil)


def _unit_rows(ref, dil, r, blk):
    return ref[_unit_index(dil, r, blk), :]


def _store_rows(ref, dil, r, blk, val):
    ref[_unit_index(dil, r, blk), :] = val


def _over_units(dil, nblk, unit, carry0, after=None):
    for r in range(dil):
        carry = carry0
        for blk in range(nblk):
            carry = unit(r, blk, blk > 0, carry)
        if after is not None:
            after(r, carry)


def _band_mask(with_prev):
    nk = 2 * SPAN if with_prev else SPAN
    qi = lax.broadcasted_iota(jnp.int32, (SPAN, nk), 0)
    kj = lax.broadcasted_iota(jnp.int32, (SPAN, nk), 1)
    if with_prev:
        return (kj >= qi) & (kj <= qi + SPAN)
    return kj <= qi


_NT = (((1,), (1,)), ((), ()))
_TN = (((0,), (0,)), ((), ()))
N_PAIR = 128 // HEAD_DIM


def _per_head(x):
    head = lax.shift_right_logical(lax.broadcasted_iota(jnp.int32, x.shape, 1), HEAD_DIM.bit_length() - 1)
    return [jnp.where(head == hh, x, 0.0).astype(BF16) for hh in range(N_PAIR)]


def _head_columns(stat):
    return [stat[:, hh * HEAD_DIM:hh * HEAD_DIM + 1] for hh in range(N_PAIR)]


def _by_head(cols):
    head = lax.shift_right_logical(lax.broadcasted_iota(jnp.int32, (cols[0].shape[0], 128), 1), HEAD_DIM.bit_length() - 1)
    out = cols[-1]
    for hh in range(N_PAIR - 2, -1, -1):
        out = jnp.where(head == hh, cols[hh], out)
    return jnp.broadcast_to(out, (cols[0].shape[0], 128))


def _group_fwd(q_ref, k_ref, v_ref, o_ref, l_ref, dil, seq):
    nblk = seq // (dil * SPAN)
    scale = HEAD_DIM ** -0.5

    def unit(r, blk, with_prev, carry):
        q = (_unit_rows(q_ref, dil, r, blk) * scale).astype(BF16)
        kc = _per_head(_unit_rows(k_ref, dil, r, blk))
        vc = _per_head(_unit_rows(v_ref, dil, r, blk))
        mask = _band_mask(with_prev)
        nk = 2 * SPAN if with_prev else SPAN
        khs = [jnp.concatenate([carry[0][hh], kc[hh]], axis=0) if with_prev else kc[hh] for hh in range(N_PAIR)]
        vhs = [jnp.concatenate([carry[1][hh], vc[hh]], axis=0) if with_prev else vc[hh] for hh in range(N_PAIR)]
        s_all = lax.dot_general(q, jnp.concatenate(khs, axis=0), _NT, preferred_element_type=F32)
        ps, lses, rdens = [], [], []
        for hh in range(N_PAIR):
            s = jnp.where(mask, s_all[:, hh * nk:(hh + 1) * nk], NEG_INF)
            mx = jnp.max(s, axis=-1, keepdims=True)
            p = jnp.exp(s - mx)
            den = jnp.sum(p, axis=-1, keepdims=True)
            ps.append(p.astype(BF16))
            lses.append(mx + jnp.log(den))
            rdens.append(1.0 / den)
        out = jnp.dot(jnp.concatenate(ps, axis=1), jnp.concatenate(vhs, axis=0), preferred_element_type=F32)
        _store_rows(o_ref, dil, r, blk, out * _by_head(rdens))
        _store_rows(l_ref, dil, r, blk, _by_head(lses))
        return kc, vc

    _over_units(dil, nblk, unit, None)


def _attn_fwd(name, qn, kn, kv, p1, seq):
    t, da = kn.shape
    hp = da // 128
    ng = len(DILATIONS)

    def body(q0_ref, q1_ref, q2_ref, k_ref, v_ref, g_ref, o_ref, l_ref, m_ref, og, lg):
        for g, q_ref in enumerate((q0_ref, q1_ref, q2_ref)):
            _group_fwd(q_ref, k_ref, v_ref, og.at[g], lg.at[g], DILATIONS[g], seq)
        a0, a1, a2 = lg[0], lg[1], lg[2]
        mx = jnp.maximum(jnp.maximum(a0, a1), a2)
        e0, e1, e2 = jnp.exp(a0 - mx), jnp.exp(a1 - mx), jnp.exp(a2 - mx)
        den = e0 + e1 + e2
        o = (e0 * og[0] + e1 * og[1] + e2 * og[2]) / den
        o_ref[...] = o
        l_ref[...] = mx + jnp.log(den)
        gt = g_ref[...].astype(F32)
        m_ref[...] = (o * gt * _sigmoid(gt)).astype(BF16)

    blk_spec = lambda off: pl.BlockSpec((seq, 128), lambda b, h: (b, off + h))
    return pl.pallas_call(
        body, name=name, grid=(t // seq, hp),
        in_specs=[blk_spec(0), blk_spec(hp), blk_spec(2 * hp), blk_spec(0), blk_spec(hp), blk_spec(3 * hp)],
        out_specs=[blk_spec(0)] * 3,
        out_shape=[jax.ShapeDtypeStruct((t, da), F32)] * 2 + [jax.ShapeDtypeStruct((t, da), BF16)],
        scratch_shapes=[pltpu.VMEM((ng, seq, 128), F32), pltpu.VMEM((ng, seq, 128), F32)],
        compiler_params=_params("parallel", "parallel"))(qn, qn, qn, kn, kv, p1)


def _group_bwd(q_ref, k_ref, v_ref, do_ref, l_ref, d_ref, dq_ref, dk_ref, dv_ref, dil, seq, first):
    nblk = seq // (dil * SPAN)
    scale = HEAD_DIM ** -0.5

    def put(ref, r, blk, val):
        if not first:
            val = val + _unit_rows(ref, dil, r, blk)
        _store_rows(ref, dil, r, blk, val)

    def unit(r, blk, with_prev, carry):
        kp, vp, pend_k, pend_v = carry if with_prev else (None,) * 4
        q = _unit_rows(q_ref, dil, r, blk) * scale
        kc = _per_head(_unit_rows(k_ref, dil, r, blk))
        vc = _per_head(_unit_rows(v_ref, dil, r, blk))
        dov = _unit_rows(do_ref, dil, r, blk)
        lcols = _head_columns(_unit_rows(l_ref, dil, r, blk))
        dcols = _head_columns(_unit_rows(d_ref, dil, r, blk))
        mask = _band_mask(with_prev)
        qb, dob = q.astype(BF16), dov.astype(BF16)
        nk = 2 * SPAN if with_prev else SPAN
        khs = [jnp.concatenate([kp[hh], kc[hh]], axis=0) if with_prev else kc[hh] for hh in range(N_PAIR)]
        vhs = [jnp.concatenate([vp[hh], vc[hh]], axis=0) if with_prev else vc[hh] for hh in range(N_PAIR)]
        kall, vall = jnp.concatenate(khs, axis=0), jnp.concatenate(vhs, axis=0)
        s_all = lax.dot_general(qb, kall, _NT, preferred_element_type=F32)
        dp_all = lax.dot_general(dob, vall, _NT, preferred_element_type=F32)
        ps, dss = [], []
        for hh in range(N_PAIR):
            cols = slice(hh * nk, (hh + 1) * nk)
            p = jnp.where(mask, jnp.exp(s_all[:, cols] - lcols[hh]), 0.0)
            ps.append(p.astype(BF16))
            dss.append((p * (dp_all[:, cols] - dcols[hh])).astype(BF16))
        dq = jnp.dot(jnp.concatenate(dss, axis=1), kall, preferred_element_type=F32)
        qall = jnp.concatenate(_per_head(q), axis=0)
        doall = jnp.concatenate(_per_head(dov), axis=0)
        dkcat = lax.dot_general(jnp.concatenate(dss, axis=0), qall, _TN, preferred_element_type=F32)
        dvcat = lax.dot_general(jnp.concatenate(ps, axis=0), doall, _TN, preferred_element_type=F32)
        _store_rows(dq_ref, dil, r, blk, dq * scale)
        if with_prev:
            put(dk_ref, r, blk - 1, pend_k + dkcat[:SPAN])
            put(dv_ref, r, blk - 1, pend_v + dvcat[:SPAN])
            return kc, vc, dkcat[SPAN:], dvcat[SPAN:]
        return kc, vc, dkcat, dvcat

    def after(r, carry):
        put(dk_ref, r, nblk - 1, carry[2])
        put(dv_ref, r, nblk - 1, carry[3])

    _over_units(dil, nblk, unit, None, after)


def _attn_bwd(name, qn, kn, kv, do, lse, dsum, seq):
    t, da = kn.shape
    hp = da // 128

    def body(q0_ref, q1_ref, q2_ref, k_ref, v_ref, do_ref, l_ref, d_ref, dq0_ref, dq1_ref, dq2_ref, dk_ref, dv_ref):
        groups = ((q0_ref, dq0_ref), (q1_ref, dq1_ref), (q2_ref, dq2_ref))
        for g, (q_ref, dq_ref) in enumerate(groups):
            _group_bwd(q_ref, k_ref, v_ref, do_ref, l_ref, d_ref, dq_ref, dk_ref, dv_ref, DILATIONS[g], seq, g == 0)

    blk_spec = lambda off: pl.BlockSpec((seq, 128), lambda b, h: (b, off + h))
    return pl.pallas_call(
        body, name=name, grid=(t // seq, hp),
        in_specs=[blk_spec(0), blk_spec(hp), blk_spec(2 * hp), blk_spec(0), blk_spec(hp), blk_spec(0), blk_spec(0), blk_spec(0)],
        out_specs=[blk_spec(0)] * 5,
        out_shape=[jax.ShapeDtypeStruct((t, da), F32)] * 5,
        compiler_params=_params("parallel", "parallel"))(qn, qn, qn, kn, kv, do, lse, dsum)


def _gate_bwd(name, dm, o, p1):
    t, da = o.shape
    tr = _tile(t, 256, 8)

    def body(dm_ref, o_ref, g_ref, do_ref, dg_ref, ds_ref):
        g = g_ref[...].astype(F32)
        sg = _sigmoid(g)
        dmv, ov = dm_ref[...].astype(F32), o_ref[...]
        do = dmv * (g * sg)
        do_ref[...] = do
        dg_ref[...] = (dmv * ov * (sg * (1.0 + g * (1.0 - sg)))).astype(BF16)
        ds_ref[...] = _seg_allsum64(do * ov, terms=2)

    return pl.pallas_call(
        body, name=name, grid=(t // tr,),
        in_specs=[_row_spec(tr, da), _row_spec(tr, da), _row_spec(tr, da, 3)],
        out_specs=[_row_spec(tr, da)] * 3,
        out_shape=[jax.ShapeDtypeStruct((t, da), F32), jax.ShapeDtypeStruct((t, da), BF16), jax.ShapeDtypeStruct((t, da), F32)],
        compiler_params=_params("parallel"))(dm, o, p1)


def _cast_bf16(name, w2d, chip):
    r, c = w2d.shape
    tr = _tile(r, 256, 16)

    def body(chip_ref, x_ref, o_ref):
        o_ref[...] = x_ref[...].astype(BF16)

    grid_spec = pltpu.PrefetchScalarGridSpec(
        num_scalar_prefetch=1, grid=(r // tr,),
        in_specs=[pl.BlockSpec((tr, c), lambda i, m: (i, 0))],
        out_specs=pl.BlockSpec((None, tr, c), lambda i, m: (m[0], i, 0)))
    return pl.pallas_call(
        body, name=name, grid_spec=grid_spec, out_shape=jax.ShapeDtypeStruct((N_CHIPS, r, c), BF16),
        compiler_params=_params("parallel"))(chip, w2d)


def _adamw(name, w, g, m, v):
    r, c = w.shape
    tr = _tile(r, 256, 8)
    c1 = 1.0 - ADAM_B1 ** ADAM_STEP
    c2 = 1.0 - ADAM_B2 ** ADAM_STEP

    def body(w_ref, g_ref, m_ref, v_ref, d_ref, nm_ref, nv_ref):
        gv = g_ref[...]
        nm = ADAM_B1 * m_ref[...] + (1.0 - ADAM_B1) * gv
        nv = ADAM_B2 * v_ref[...] + (1.0 - ADAM_B2) * (gv * gv)
        nm_ref[...] = nm
        nv_ref[...] = nv
        d_ref[...] = -ADAM_LR * ((nm / c1) / (jnp.sqrt(nv / c2) + ADAM_EPS) + ADAM_WD * w_ref[...])

    sds = jax.ShapeDtypeStruct((r, c), F32)
    return pl.pallas_call(
        body, name=name, grid=(r // tr,), in_specs=[_row_spec(tr, c)] * 4, out_specs=[_row_spec(tr, c)] * 3,
        out_shape=[sds] * 3, compiler_params=_params("parallel"))(w, g, m, v)


def _pair_sum(name, gd, recv, core):
    _, r, c = gd.shape
    rh = r // 2
    tr = _tile(rh, 256, 8)
    nrt = rh // tr

    def body(c_ref, a_ref, b_ref, o_ref):
        o_ref[...] = (a_ref[...].astype(F32) + b_ref[...].astype(F32)).astype(BF16)

    grid_spec = pltpu.PrefetchScalarGridSpec(
        num_scalar_prefetch=1, grid=(N_CHIPS, nrt),
        in_specs=[pl.BlockSpec((None, tr, c), lambda j, i, cr: (j, cr[0] * nrt + i, 0)),
                  pl.BlockSpec((None, tr, c), lambda j, i, cr: (j, i, 0))],
        out_specs=pl.BlockSpec((None, tr, c), lambda j, i, cr: (j, i, 0)))
    return pl.pallas_call(
        body, name=name, grid_spec=grid_spec, out_shape=jax.ShapeDtypeStruct((N_CHIPS, rh, c), BF16),
        compiler_params=_params("parallel", "parallel"))(core, gd, recv)


def _chip_sum(name, sums, parts, order):
    _, rh, c = parts.shape
    tr = _tile(rh, 256, 16)
    nrt = rh // tr

    def body(o_ref_, s_ref, p1_ref, p2_ref, p3_ref, o_ref):
        acc = s_ref[...].astype(F32)
        for p_ref in (p1_ref, p2_ref, p3_ref):
            acc = acc + p_ref[...].astype(F32)
        o_ref[...] = acc

    slot = lambda k: pl.BlockSpec((None, tr, c), lambda i, o: (o[k], i, 0))
    grid_spec = pltpu.PrefetchScalarGridSpec(
        num_scalar_prefetch=1, grid=(nrt,),
        in_specs=[slot(0), slot(1), slot(2), slot(3)],
        out_specs=pl.BlockSpec((tr, c), lambda i, o: (o[N_CHIPS] * nrt + i, 0)))
    return pl.pallas_call(
        body, name=name, grid_spec=grid_spec, out_shape=jax.ShapeDtypeStruct((2 * rh, c), F32),
        compiler_params=_params("parallel"))(order, sums, parts, parts, parts)


HBM = pl.BlockSpec(memory_space=pl.ANY)


def _place():
    x, y, c = lax.axis_index("x"), lax.axis_index("y"), lax.axis_index("c")
    chips = [(1 - x, y), (x, 1 - y), (1 - x, 1 - y)]
    return x, y, c, chips


def _half(ref, hc):
    rows = ref.shape[0] // 2
    return ref.at[pl.ds(hc * rows, rows)]


class _Plan:
    def __init__(self, ins, out_shapes, aliases, n_sems, start, finish):
        self.ins, self.out_shapes, self.aliases, self.n_sems = list(ins), list(out_shapes), dict(aliases), n_sems
        self.start, self.finish = start, finish


def _gather_plan(shards):
    n = len(shards)

    def ici(place, outs, send_sems, recv_sems, i, k, slot):
        x, y, c, chips = place
        half = _half(outs[i].at[slot], c)
        return pltpu.make_async_remote_copy(
            src_ref=half, dst_ref=half, send_sem=send_sems.at[6 * i + k], recv_sem=recv_sems.at[6 * i + k],
            device_id=(chips[k][0], chips[k][1], c), device_id_type=MESH)

    def d2d(place, outs, send_sems, recv_sems, i, k, slot, hc):
        x, y, c, chips = place
        half = _half(outs[i].at[slot], hc)
        return pltpu.make_async_remote_copy(
            src_ref=half, dst_ref=half, send_sem=send_sems.at[6 * i + 3 + k], recv_sem=recv_sems.at[6 * i + 3 + k],
            device_id=(x, y, 1 - c), device_id_type=MESH)

    def start(place, ins, outs, send_sems, recv_sems):
        x, y, c, chips = place
        for i in range(n):
            for k in range(3):
                ici(place, outs, send_sems, recv_sems, i, k, 2 * x + y).start()

    def finish(place, ins, outs, send_sems, recv_sems):
        x, y, c, chips = place
        theirs = [2 * chip[0] + chip[1] for chip in chips]
        for i in range(n):
            for k in range(3):
                ici(place, outs, send_sems, recv_sems, i, k, theirs[k]).wait_recv()
                d2d(place, outs, send_sems, recv_sems, i, k, theirs[k], c).start()
        for i in range(n):
            for k in range(3):
                d2d(place, outs, send_sems, recv_sems, i, k, theirs[k], 1 - c).wait_recv()
        for i in range(n):
            for k in range(3):
                ici(place, outs, send_sems, recv_sems, i, k, 2 * x + y).wait_send()
                d2d(place, outs, send_sems, recv_sems, i, k, theirs[k], c).wait_send()

    return _Plan(shards, [jax.ShapeDtypeStruct(s.shape, s.dtype) for s in shards], {i: i for i in range(n)}, 6 * n,
                 start, finish)


def _scatter_plan(sums):
    n = len(sums)

    def copy(place, ins, outs, send_sems, recv_sems, i, k, src_slot, dst_slot):
        x, y, c, chips = place
        return pltpu.make_async_remote_copy(
            src_ref=ins[i].at[src_slot], dst_ref=outs[i].at[dst_slot],
            send_sem=send_sems.at[3 * i + k], recv_sem=recv_sems.at[3 * i + k],
            device_id=(chips[k][0], chips[k][1], c), device_id_type=MESH)

    def start(place, ins, outs, send_sems, recv_sems):
        x, y, c, chips = place
        for i in range(n):
            for k, chip in enumerate(chips):
                copy(place, ins, outs, send_sems, recv_sems, i, k, 2 * chip[0] + chip[1], 2 * x + y).start()

    def finish(place, ins, outs, send_sems, recv_sems):
        x, y, c, chips = place
        for i in range(n):
            for k, chip in enumerate(chips):
                theirs = 2 * chip[0] + chip[1]
                copy(place, ins, outs, send_sems, recv_sems, i, k, theirs, 2 * x + y).wait_send()
                copy(place, ins, outs, send_sems, recv_sems, i, k, theirs, theirs).wait_recv()

    return _Plan(sums, [jax.ShapeDtypeStruct(s.shape, s.dtype) for s in sums], {}, 3 * n, start, finish)


def _exchange_plan(grads):
    n = len(grads)

    def copy(place, ins, outs, send_sems, recv_sems, i):
        x, y, c, _ = place
        rows = ins[i].shape[1] // 2
        return pltpu.make_async_remote_copy(
            src_ref=ins[i].at[:, pl.ds((1 - c) * rows, rows), :], dst_ref=outs[i],
            send_sem=send_sems.at[i], recv_sem=recv_sems.at[i], device_id=(x, y, 1 - c), device_id_type=MESH)

    def start(place, ins, outs, send_sems, recv_sems):
        for i in range(n):
            copy(place, ins, outs, send_sems, recv_sems, i).start()

    def finish(place, ins, outs, send_sems, recv_sems):
        for i in range(n):
            copy(place, ins, outs, send_sems, recv_sems, i).wait()

    shapes = [jax.ShapeDtypeStruct((N_CHIPS, g.shape[1] // 2, g.shape[2]), g.dtype) for g in grads]
    return _Plan(grads, shapes, {}, n, start, finish)


def _hosted_call(body, plan, *, name, grid, in_specs, out_specs, out_shape, scratch_shapes, args, sem):
    in_specs, out_specs, out_shape, scratch_shapes = list(in_specs), list(out_specs), list(out_shape), list(scratch_shapes)
    if plan is None:
        res = pl.pallas_call(body, name=name, grid=grid, in_specs=in_specs, out_specs=out_specs, out_shape=out_shape,
                             scratch_shapes=scratch_shapes, compiler_params=_params(*sem))(*args)
        return list(res), []
    n_in, n_out, n_scr = len(in_specs), len(out_specs), len(scratch_shapes)
    p_in, p_out = len(plan.ins), len(plan.out_shapes)

    def hosted(*refs):
        refs = list(refs)
        ins, pins = refs[:n_in], refs[n_in:n_in + p_in]
        outs = refs[n_in + p_in:n_in + p_in + n_out]
        pouts = refs[n_in + p_in + n_out:n_in + p_in + n_out + p_out]
        scr = refs[n_in + p_in + n_out + p_out:n_in + p_in + n_out + p_out + n_scr]
        send_sems, recv_sems = refs[-2:]
        place = _place()
        ids = [pl.program_id(d) for d in range(len(grid))]
        first = functools.reduce(jnp.logical_and, [i == 0 for i in ids])
        last = functools.reduce(jnp.logical_and, [i == g - 1 for i, g in zip(ids, grid)])

        @pl.when(first)
        def _():
            plan.start(place, pins, pouts, send_sems, recv_sems)

        body(*ins, *outs, *scr)

        @pl.when(last)
        def _():
            plan.finish(place, pins, pouts, send_sems, recv_sems)

    res = pl.pallas_call(
        hosted, name=name, grid=grid, in_specs=in_specs + [HBM] * p_in, out_specs=out_specs + [HBM] * p_out,
        out_shape=out_shape + plan.out_shapes,
        input_output_aliases={n_in + i: n_out + o for i, o in plan.aliases.items()},
        scratch_shapes=scratch_shapes + [pltpu.SemaphoreType.DMA((plan.n_sems,)), pltpu.SemaphoreType.DMA((plan.n_sems,))],
        compiler_params=_params(*(("arbitrary",) * len(grid))))(*args, *plan.ins)
    return list(res[:n_out]), list(res[n_out:])


def _pair_exchange(name, grads):
    n = len(grads)

    def body(*refs):
        ins, outs = refs[:n], refs[n:2 * n]
        send_sems, recv_sems = refs[2 * n:]
        x, y, c, _ = _place()
        cps = []
        for i in range(n):
            rows = ins[i].shape[1] // 2
            cp = pltpu.make_async_remote_copy(
                src_ref=ins[i].at[:, pl.ds((1 - c) * rows, rows), :], dst_ref=outs[i],
                send_sem=send_sems.at[i], recv_sem=recv_sems.at[i], device_id=(x, y, 1 - c), device_id_type=MESH)
            cp.start()
            cps.append(cp)
        for cp in cps:
            cp.wait()

    return pl.pallas_call(
        body, name=name, in_specs=[HBM] * n, out_specs=[HBM] * n,
        out_shape=[jax.ShapeDtypeStruct((N_CHIPS, g.shape[1] // 2, g.shape[2]), g.dtype) for g in grads],
        scratch_shapes=[pltpu.SemaphoreType.DMA((n,)), pltpu.SemaphoreType.DMA((n,))],
        )(*grads)


def _sibling_join(grads):
    n = len(grads)

    def body(*refs):
        outs = refs[n:2 * n]
        send_sems, recv_sems = refs[2 * n:]
        x, y, c, _ = _place()
        cps = []
        for i in range(n):
            cp = pltpu.make_async_remote_copy(
                src_ref=_half(outs[i], c), dst_ref=_half(outs[i], c), send_sem=send_sems.at[i], recv_sem=recv_sems.at[i],
                device_id=(x, y, 1 - c), device_id_type=MESH)
            cp.start()
            cps.append(cp)
        for i, cp in enumerate(cps):
            cp.wait_send()
            pltpu.make_async_remote_copy(
                src_ref=_half(outs[i], 1 - c), dst_ref=_half(outs[i], 1 - c), send_sem=send_sems.at[i],
                recv_sem=recv_sems.at[i], device_id=(x, y, 1 - c), device_id_type=MESH).wait_recv()

    return pl.pallas_call(
        body, name="sibling_join", in_specs=[HBM] * n, out_specs=[HBM] * n,
        out_shape=[jax.ShapeDtypeStruct(g.shape, g.dtype) for g in grads],
        input_output_aliases={i: i for i in range(n)},
        scratch_shapes=[pltpu.SemaphoreType.DMA((n,)), pltpu.SemaphoreType.DMA((n,))],
        )(*grads)


def _gather8(name, block, reduce):
    m, n = block.shape

    def body(x_ref, out_ref, *scratch):
        if reduce:
            all_ref, send_sems, recv_sems, local_sem = scratch
        else:
            all_ref = out_ref
            send_sems, recv_sems, local_sem = scratch
        x, y, c, chips = _place()
        me, sibling = (x, y, c), (x, y, 1 - c)

        def rows(px, py, pc):
            return all_ref.at[pl.ds((4 * px + 2 * py + pc) * m, m), :]

        def copy(k, blk, to, src=None):
            return pltpu.make_async_remote_copy(
                src_ref=rows(*blk) if src is None else src, dst_ref=rows(*blk),
                send_sem=send_sems.at[k], recv_sem=recv_sems.at[k], device_id=to, device_id_type=MESH)

        mine = pltpu.make_async_copy(x_ref, rows(*me), local_sem)
        mine.start()
        first = [copy(0, me, sibling, src=x_ref)]
        first += [copy(1 + j, me, (chip[0], chip[1], c), src=x_ref) for j, chip in enumerate(chips)]
        for cp in first:
            cp.start()
        passed = [copy(4 + j, (chip[0], chip[1], c), sibling) for j, chip in enumerate(chips)]
        for j, chip in enumerate(chips):
            copy(1 + j, (chip[0], chip[1], c), me).wait_recv()
            passed[j].start()
        copy(0, sibling, me).wait_recv()
        for j, chip in enumerate(chips):
            copy(4 + j, (chip[0], chip[1], 1 - c), me).wait_recv()
        for cp in first + passed:
            cp.wait_send()
        mine.wait()
        if reduce:
            acc = all_ref[pl.ds(0, m), :]
            for d in range(1, 8):
                acc = acc + all_ref[pl.ds(d * m, m), :]
            out_ref[...] = acc

    sems = [pltpu.SemaphoreType.DMA((7,)), pltpu.SemaphoreType.DMA((7,)), pltpu.SemaphoreType.DMA]
    scratch = ([pltpu.VMEM((8 * m, n), F32)] if reduce else []) + sems
    return pl.pallas_call(
        body, name=name,
        out_shape=jax.ShapeDtypeStruct((m, n) if reduce else (8 * m, n), F32),
        in_specs=[pl.BlockSpec(memory_space=pltpu.VMEM)], out_specs=pl.BlockSpec(memory_space=pltpu.VMEM),
        scratch_shapes=scratch)(block)


def _pad_rows(a, rows):
    return jnp.concatenate([a, jnp.zeros((rows - a.shape[0], a.shape[1]), a.dtype)], axis=0)


def kernel(x, p, norm_g, w_in_a, conv_w, conv_b, ln_g, ln_b, w_out_a, kv_norm_g, w_kv, k_norm_g, w_in_b, q_norm_g, w_out_b, ple_norm_g, w_ple_gate, w_ple_proj, loss_target, m_norm_g, m_w_in_a, m_conv_w, m_conv_b, m_ln_g, m_ln_b, m_w_out_a, m_kv_norm_g, m_w_kv, m_k_norm_g, m_w_in_b, m_q_norm_g, m_w_out_b, m_ple_norm_g, m_w_ple_gate, m_w_ple_proj, v_norm_g, v_w_in_a, v_conv_w, v_conv_b, v_ln_g, v_ln_b, v_w_out_a, v_kv_norm_g, v_w_kv, v_k_norm_g, v_w_in_b, v_q_norm_g, v_w_out_b, v_ple_norm_g, v_w_ple_gate, v_w_ple_proj):
    nb, seq, dm = x.shape
    t = nb * seq
    ple = p.shape[-1]
    ccs = conv_w.shape[-1]
    cc = N_CHIPS * ccs
    da = dm
    nheads = da // HEAD_DIM
    assert seq == DILATIONS[-1] * SPAN and da % 128 == 0 and ccs % 128 == 0

    core = lax.axis_index("c").astype(jnp.int32).reshape(1)
    chip = (2 * lax.axis_index("x") + lax.axis_index("y")).astype(jnp.int32)
    chip1 = chip.reshape(1)
    sum_order = jnp.concatenate([(chip1 + k) % N_CHIPS for k in range(N_CHIPS)] + [core])

    x2 = x.reshape(t, dm)
    tgt2 = loss_target.reshape(t, dm)
    p0 = p[0].reshape(t, ple)
    p1 = p[1].reshape(t, ple)

    big = [
        ("w_in_a", w_in_a[0], "col"), ("w_out_a", w_out_a[0], "row"), ("w_kv", w_kv, "col"),
        ("w_in_b", w_in_b[0], "col"), ("w_out_b", w_out_b[0], "row"),
        ("w_ple_gate0", w_ple_gate[0], "row"), ("w_ple_gate1", w_ple_gate[1], "row"),
        ("w_ple_proj0", w_ple_proj[0], "col"), ("w_ple_proj1", w_ple_proj[1], "col"),
    ]
    shard_shape = {nm: w.shape for nm, w, _ in big}
    names = [nm for nm, _, _ in big]
    own = [_cast_bf16("cast_" + nm, w, chip1) for nm, w, _ in big]
    W = {}

    vec_rows = 40
    small = _pad_rows(jnp.concatenate([conv_w[0], conv_b, ln_g, ln_b], axis=0), vec_rows)
    allv = _gather8("gather_conv_vectors", small, reduce=False).reshape(N_CHIPS, 2, vec_rows, ccs)[:, 0]
    allv = allv.transpose(1, 0, 2).reshape(vec_rows, cc)
    cw_full, cb_full, lg_full, lb_full = allv[:HALO], allv[31:32], allv[32:33], allv[33:34]
    cw_full = cw_full * (lax.broadcasted_iota(jnp.int32, (HALO, 1), 0) < CONV_WIDTH).astype(F32)
    tables = _rope_tables(seq)
    gain_q = jnp.tile(q_norm_g[0][:, None, :], (1, nheads, 1)).reshape(1, 3 * da)
    gain_k = jnp.tile(k_norm_g[None, :], (1, nheads))
    g0, g1 = norm_g[0:1], norm_g[1:2]
    pg0, pg1 = ple_norm_g[0:1], ple_norm_g[1:2]
    kvg = kv_norm_g[None, :]

    (u0,), (W[names[0]],) = _rms_fwd("rms_u0", x2, [g0], plan=_gather_plan(own[:1]))
    pa, gathered = _mm_nn("mm_in_a", u0, W["w_in_a"], "col", out_dtype=BF16, plan=_gather_plan(own[1:]))
    W.update(zip(names[1:], gathered))
    for nm in ("w_ple_proj0", "w_ple_proj1"):
        W[nm] = W[nm].transpose(1, 0, 2).reshape(1, W[nm].shape[1], -1)
    conv_out, m_a = _mixa_fwd("mixa_fwd", pa, cw_full, cb_full, lg_full, lb_full, seq)
    h0, r0 = _mm_nn("mm_out_a", m_a, W["w_out_a"], "row", resid=x2, norm_gain=pg0)
    gpre0 = _mm_nn("mm_gate0", r0, W["w_ple_gate0"], "row", out_dtype=BF16)
    pp0 = _mm_nn("mm_proj0", p0, W["w_ple_proj0"], "col", out_dtype=BF16)
    x1, kvn, u1 = _ple_fwd("ple_fwd0", h0, gpre0, pp0, [kvg, g1])
    kv, kn = _mm_nn("mm_kv", kvn, W["w_kv"], "col", heads=(gain_k, tables, da, seq))
    pb, qn = _mm_nn("mm_in_b", u1, W["w_in_b"], "col", out_dtype=BF16, heads=(gain_q, tables, 3 * da, seq))
    o, lse, m_b = _attn_fwd("attn_fwd", qn, kn, kv, pb, seq)
    h1, r1 = _mm_nn("mm_out_b", m_b, W["w_out_b"], "row", resid=x1, norm_gain=pg1)
    gpre1 = _mm_nn("mm_gate1", r1, W["w_ple_gate1"], "row", out_dtype=BF16)
    pp1 = _mm_nn("mm_proj1", p1, W["w_ple_proj1"], "col", out_dtype=BF16)
    dy, dgp1, dpp1, sq = _ple_loss("ple_loss", h1, gpre1, pp1, tgt2)
    loss = lax.psum(0.5 * sq[0, 0] / dm, ("x", "y", "c"))

    G = {}
    G["w_ple_gate1"] = _mm_tn("tn_gate1", r1, dgp1, "row", shard_shape["w_ple_gate1"])
    G["w_ple_proj1"] = _mm_tn("tn_proj1", p1, dpp1, "col", shard_shape["w_ple_proj1"], whole=True)
    dr1 = _mm_nt("nt_gate1", dgp1, W["w_ple_gate1"], "row", out_dtype=BF16)
    dh1, (dpg1,) = _rms_bwd("rms_bwd_r1", h1, dy, [(pg1, dr1)])
    G["w_out_b"] = _mm_tn("tn_out_b", m_b, dh1, "row", shard_shape["w_out_b"])
    dm_b = _mm_nt("nt_out_b", dh1, W["w_out_b"], "row", out_dtype=BF16)
    d_o, dgt, dsum = _gate_bwd("gate_bwd", dm_b, o, pb)
    dq0, dq1, dq2, dk, dv = _attn_bwd("attn_bwd", qn, kn, kv, d_o, lse, dsum, seq)
    dpb, dgq = _q_bwd("q_bwd", pb, gain_q, tables, [dq0, dq1, dq2], dgt, seq)
    dkv, dgk = _k_bwd("k_bwd", kv, gain_k, tables, dk, dv, seq)
    G["w_in_b"] = _mm_tn("tn_in_b", u1, dpb, "col", shard_shape["w_in_b"])
    du1 = _mm_nt("nt_in_b", dpb, W["w_in_b"], "col", out_dtype=BF16)
    G["w_kv"] = _mm_tn("tn_kv", kvn, dkv, "col", shard_shape["w_kv"])
    dkvn = _mm_nt("nt_kv", dkv, W["w_kv"], "col", out_dtype=BF16)
    dx1, (dg1, dkvg), dgp0, dpp0 = _rms_bwd("rms_bwd_x1", x1, dh1, [(g1, du1), (kvg, dkvn)], ple=(gpre0, pp0))
    G["w_ple_gate0"] = _mm_tn("tn_gate0", r0, dgp0, "row", shard_shape["w_ple_gate0"])
    G["w_ple_proj0"] = _mm_tn("tn_proj0", p0, dpp0, "col", shard_shape["w_ple_proj0"], whole=True)
    dr0 = _mm_nt("nt_gate0", dgp0, W["w_ple_gate0"], "row", out_dtype=BF16)
    dh0, (dpg0,) = _rms_bwd("rms_bwd_r0", h0, dx1, [(pg0, dr0)])
    G["w_out_a"] = _mm_tn("tn_out_a", m_a, dh0, "row", shard_shape["w_out_a"])
    layer1 = ["w_kv", "w_in_b", "w_out_b", "w_ple_gate1", "w_ple_proj1"]
    layer0 = ["w_in_a", "w_out_a", "w_ple_gate0", "w_ple_proj0"]

    def pair_sums(batch, recv):
        return [_pair_sum("pair_sum_" + nm, G[nm], rc, core) for nm, rc in zip(batch, recv)]

    dm_a, recv1 = _mm_nt("nt_out_a", dh0, W["w_out_a"], "row", out_dtype=BF16,
                         plan=_exchange_plan([G[nm] for nm in layer1]))
    sums1 = pair_sums(layer1, recv1)
    (dc, dz, dlg, dlb, dcb), parts1 = _mixa_bwd1("mixa_bwd1", conv_out, pa, dm_a, lg_full, lb_full,
                                                 plan=_scatter_plan(sums1))
    dpa, dcw = _mixa_bwd2("mixa_bwd2", dc, pa, dz, cw_full, seq)
    G["w_in_a"] = _mm_tn("tn_in_a", u0, dpa, "col", shard_shape["w_in_a"])
    sums0 = pair_sums(layer0, _pair_exchange("pair_exchange_layer0", [G[nm] for nm in layer0]))
    du0, parts0 = _mm_nt("nt_in_a", dpa, W["w_in_a"], "col", out_dtype=BF16, plan=_scatter_plan(sums0))
    dx, (dg0,) = _rms_bwd("rms_bwd_x", x2, dh0, [(g0, du0)])
    grad_x = dx.reshape(nb, seq, dm)

    sums = dict(zip(layer1 + layer0, sums1 + sums0))
    parts = dict(zip(layer1 + layer0, parts1 + parts0))
    halves = [_chip_sum("chip_sum_" + nm, sums[nm], parts[nm], sum_order) for nm in names]
    gfull = dict(zip(names, _sibling_join(halves)))

    def as_rows(a):
        return a.reshape(-1, dm)

    small_parts = [as_rows(dcw), as_rows(dcb), as_rows(dlg), as_rows(dlb), dg0, dg1, dkvg, dpg0, dpg1, as_rows(dgk), as_rows(dgq)]
    counts = [a.shape[0] for a in small_parts]
    total = sum(counts)
    packed = _pad_rows(jnp.concatenate(small_parts, axis=0), -(-total // 8) * 8)
    red = _gather8("reduce_small", packed, reduce=True)
    pieces, off = [], 0
    for n_ in counts:
        pieces.append(red[off:off + n_])
        off += n_
    r_dcw, r_dcb, r_dlg, r_dlb, r_g0, r_g1, r_kvg, r_pg0, r_pg1, r_gk, r_gq = pieces
    my_cols = lambda a: lax.dynamic_slice_in_dim(a.reshape(-1, cc), chip * ccs, ccs, axis=1)
    small_grads = {
        "norm_g": jnp.concatenate([r_g0, r_g1], axis=0),
        "conv_w": my_cols(r_dcw)[:CONV_WIDTH],
        "conv_b": my_cols(r_dcb), "ln_g": my_cols(r_dlg), "ln_b": my_cols(r_dlb),
        "kv_norm_g": r_kvg,
        "k_norm_g": r_gk.reshape(nheads, HEAD_DIM).sum(axis=0, keepdims=True),
        "q_norm_g": r_gq.reshape(3, nheads, HEAD_DIM).sum(axis=1),
        "ple_norm_g": jnp.concatenate([r_pg0, r_pg1], axis=0),
    }

    given = dict(norm_g=norm_g, w_in_a=w_in_a, conv_w=conv_w, conv_b=conv_b, ln_g=ln_g, ln_b=ln_b, w_out_a=w_out_a,
                 kv_norm_g=kv_norm_g, w_kv=w_kv, k_norm_g=k_norm_g, w_in_b=w_in_b, q_norm_g=q_norm_g, w_out_b=w_out_b,
                 ple_norm_g=ple_norm_g, w_ple_gate=w_ple_gate, w_ple_proj=w_ple_proj)
    mom1 = dict(norm_g=m_norm_g, w_in_a=m_w_in_a, conv_w=m_conv_w, conv_b=m_conv_b, ln_g=m_ln_g, ln_b=m_ln_b,
                w_out_a=m_w_out_a, kv_norm_g=m_kv_norm_g, w_kv=m_w_kv, k_norm_g=m_k_norm_g, w_in_b=m_w_in_b,
                q_norm_g=m_q_norm_g, w_out_b=m_w_out_b, ple_norm_g=m_ple_norm_g, w_ple_gate=m_w_ple_gate,
                w_ple_proj=m_w_ple_proj)
    mom2 = dict(norm_g=v_norm_g, w_in_a=v_w_in_a, conv_w=v_conv_w, conv_b=v_conv_b, ln_g=v_ln_g, ln_b=v_ln_b,
                w_out_a=v_w_out_a, kv_norm_g=v_kv_norm_g, w_kv=v_w_kv, k_norm_g=v_k_norm_g, w_in_b=v_w_in_b,
                q_norm_g=v_q_norm_g, w_out_b=v_w_out_b, ple_norm_g=v_ple_norm_g, w_ple_gate=v_w_ple_gate,
                w_ple_proj=v_w_ple_proj)
    order = ["norm_g", "w_in_a", "conv_w", "conv_b", "ln_g", "ln_b", "w_out_a", "kv_norm_g", "w_kv", "k_norm_g", "w_in_b",
             "q_norm_g", "w_out_b", "ple_norm_g", "w_ple_gate", "w_ple_proj"]
    grads, deltas, new_m, new_v = {}, {}, {}, {}
    for nm in order:
        shape = given[nm].shape
        if nm in ("w_ple_gate", "w_ple_proj"):
            g2 = jnp.concatenate([gfull[nm + "0"], gfull[nm + "1"]], axis=0)
        elif nm in gfull:
            g2 = gfull[nm]
        else:
            g2 = small_grads[nm]
        two_d = g2.shape
        d2, m2, v2 = _adamw("adamw_" + nm, given[nm].reshape(two_d), g2, mom1[nm].reshape(two_d), mom2[nm].reshape(two_d))
        grads[nm], deltas[nm], new_m[nm], new_v[nm] = (a.reshape(shape) for a in (g2, d2, m2, v2))

    return (loss, grad_x, *[grads[n_] for n_ in order], *[deltas[n_] for n_ in order],
            *[new_m[n_] for n_ in order], *[new_v[n_] for n_ in order])
```
